```python
import math
import jax, jax.numpy as jnp
from jax import lax
import numpy as np

D_MODEL = 1024
BATCH = 2
SEQ = 8192
DEPTH = 2

CHUNK = 64
WIDTH_A = D_MODEL // 2
HEAD_DIM_A = 128
N_HEADS_A = WIDTH_A // HEAD_DIM_A
CONV_K = 4
WIDTH_B = D_MODEL - WIDTH_A
HEAD_DIM_B = 64
N_HEADS_B = WIDTH_B // HEAD_DIM_B
IDX_HEADS = 8
IDX_DIM = 64
TOPK_KEYS_MAX = 256
Q_BLOCK = 128
REL_BUCKETS = 32
REL_MAX_DIST = 1024
N_EXPERTS = 32
TOP_K = 4
D_FF_EXPERT = 1024
SWIGLU_ALPHA = 1.702
SWIGLU_LIMIT = 7.0
MOE_BLOCK = 128
EPS = 1e-6
IN_SPLITS = (3 * WIDTH_A, WIDTH_A, N_HEADS_A, N_HEADS_A, 3 * WIDTH_B, IDX_HEADS * IDX_DIM, IDX_DIM, IDX_HEADS)
D_IN = 3 * WIDTH_A + WIDTH_A + 2 * N_HEADS_A + 3 * WIDTH_B + IDX_HEADS * IDX_DIM + IDX_DIM + IDX_HEADS

kernel_name = "hybrid_gdn_dsa_moe_block"


def rms_norm(x, w):
    xf = x.astype(jnp.float32)
    y = xf * lax.rsqrt(jnp.mean(xf * xf, axis=-1, keepdims=True) + EPS)
    return (y * w.astype(jnp.float32)).astype(x.dtype)


def l2_norm(x):
    return x * lax.rsqrt(jnp.sum(x * x, axis=-1, keepdims=True) + EPS)


def t5_bucket(rel):
    nb = REL_BUCKETS // 2
    max_exact = nb // 2
    side = jnp.where(rel > 0, nb, 0)
    n = jnp.abs(rel)
    nf = jnp.maximum(n, 1).astype(jnp.float32)
    large = max_exact + (jnp.log(nf / max_exact) / math.log(REL_MAX_DIST / max_exact) * (nb - max_exact)).astype(jnp.int32)
    large = jnp.minimum(large, nb - 1)
    return side + jnp.where(n < max_exact, n, large)


def causal_short_conv(u, w):
    S = u.shape[1]
    up = jnp.pad(u, ((0, 0), (CONV_K - 1, 0), (0, 0)))
    y = up[:, 0:S] * w[0]
    for j in range(1, CONV_K):
        y = y + up[:, j:j + S] * w[j]
    return jax.nn.silu(y)


def gated_deltanet(qkv, z, b_lin, a_lin, conv_w, a_log, dt_bias, norm_w):
    f32 = jnp.float32
    Bsz, S, _ = qkv.shape
    N = S // CHUNK
    u = causal_short_conv(qkv, conv_w).astype(f32)
    q, k, v = jnp.split(u, 3, axis=-1)

    def heads_chunks(t):
        return t.reshape(Bsz, N, CHUNK, N_HEADS_A, HEAD_DIM_A).transpose(0, 3, 1, 2, 4)

    def to_bhnc(t):
        return t.reshape(Bsz, N, CHUNK, N_HEADS_A).transpose(0, 3, 1, 2)

    q = l2_norm(heads_chunks(q)) * HEAD_DIM_A ** -0.5
    k = l2_norm(heads_chunks(k))
    v = heads_chunks(v)
    beta = to_bhnc(jax.nn.sigmoid(b_lin.astype(f32)))
    g = -jnp.exp(a_log.astype(f32)) * jax.nn.softplus(a_lin.astype(f32) + dt_bias.astype(f32))
    gc = jnp.cumsum(to_bhnc(g), axis=-1)
    causal = jnp.tril(jnp.ones((CHUNK, CHUNK), bool))
    strict = jnp.tril(jnp.ones((CHUNK, CHUNK), bool), -1)
    decay = jnp.exp(jnp.where(causal, gc[..., :, None] - gc[..., None, :], -jnp.inf))
    k_beta = k * beta[..., None]
    lmat = jnp.where(strict, jnp.einsum('bhncd,bhnsd->bhncs', k_beta, k) * decay, 0.0)
    rhs = jnp.concatenate([v * beta[..., None], k_beta * jnp.exp(gc)[..., None]], axis=-1)
    sol = lax.linalg.triangular_solve(lmat + jnp.eye(CHUNK, dtype=f32), rhs,
                                      left_side=True, lower=True, unit_diagonal=True)
    u_val, w_dec = sol[..., :HEAD_DIM_A], sol[..., HEAD_DIM_A:]
    attn = jnp.einsum('bhncd,bhnsd->bhncs', q, k) * decay

    def step(state, inp):
        q_n, k_n, u_n, w_n, gc_n, attn_n = inp
        v_new = u_n - jnp.einsum('bhck,bhkv->bhcv', w_n, state)
        o = (jnp.einsum('bhck,bhkv->bhcv', q_n * jnp.exp(gc_n)[..., None], state)
             + jnp.einsum('bhcs,bhsv->bhcv', attn_n, v_new))
        g_last = gc_n[..., -1]
        k_dec = k_n * jnp.exp(g_last[..., None] - gc_n)[..., None]
        state = state * jnp.exp(g_last)[..., None, None] + jnp.einsum('bhck,bhcv->bhkv', k_dec, v_new)
        return state, o

    xs = (jnp.moveaxis(q, 2, 0), jnp.moveaxis(k, 2, 0), jnp.moveaxis(u_val, 2, 0),
          jnp.moveaxis(w_dec, 2, 0), jnp.moveaxis(gc, 2, 0), jnp.moveaxis(attn, 2, 0))
    state0 = jnp.zeros((Bsz, N_HEADS_A, HEAD_DIM_A, HEAD_DIM_A), f32)
    _, o = lax.scan(step, state0, xs)
    o = o.transpose(1, 0, 3, 2, 4).reshape(Bsz, S, N_HEADS_A, HEAD_DIM_A)
    zh = z.reshape(Bsz, S, N_HEADS_A, HEAD_DIM_A).astype(f32)
    o = rms_norm(o, norm_w) * jax.nn.silu(zh)
    return o.reshape(Bsz, S, WIDTH_A).astype(qkv.dtype)


def dsa_attention(qkv_b, q_idx, k_idx, w_idx, q_norm_w, k_norm_w, rel_bias):
    f32 = jnp.float32
    Bsz, S, _ = qkv_b.shape
    k_sel = min(TOPK_KEYS_MAX, S // 4)
    nblk = S // Q_BLOCK
    q, k, v = jnp.split(qkv_b, 3, axis=-1)
    q = rms_norm(q.reshape(Bsz, S, N_HEADS_B, HEAD_DIM_B), q_norm_w)
    k = rms_norm(k.reshape(Bsz, S, N_HEADS_B, HEAD_DIM_B), k_norm_w)
    v = v.reshape(Bsz, S, N_HEADS_B, HEAD_DIM_B)
    q_idx = q_idx.reshape(Bsz, S, IDX_HEADS, IDX_DIM)
    w_idx = w_idx * (IDX_HEADS ** -0.5 * IDX_DIM ** -0.5)
    key_chunk = jnp.arange(S) // CHUNK
    bidx = jnp.arange(Bsz)[:, None, None]

    def blocks(t):
        return jnp.moveaxis(t.reshape((Bsz, nblk, Q_BLOCK) + t.shape[2:]), 1, 0)

    def one_block(args):
        blk, q_b, qi_b, w_b = args
        t_pos = blk * Q_BLOCK + jnp.arange(Q_BLOCK)
        q_chunk = t_pos // CHUNK
        dots = jnp.einsum('bqhd,bsd->bqhs', qi_b, k_idx)
        score = jnp.einsum('bqhs,bqh->bqs', jax.nn.relu(dots), w_b).astype(f32)
        admissible = key_chunk[None, :] <= q_chunk[:, None]
        score = jnp.where(admissible[None], score, -jnp.inf)
        _, sel = lax.top_k(score, k_sel)
        valid = (sel // CHUNK) <= q_chunk[None, :, None]
        k_g = k[bidx, sel]
        v_g = v[bidx, sel]
        bias = rel_bias[t5_bucket(sel - t_pos[None, :, None])]
        logits = (jnp.einsum('bqhd,bqkhd->bqhk', q_b, k_g).astype(f32) * HEAD_DIM_B ** -0.5
                  + jnp.moveaxis(bias, -1, 2).astype(f32))
        logits = jnp.where(valid[:, :, None, :], logits, -jnp.inf)
        p = jax.nn.softmax(logits, axis=-1).astype(v.dtype)
        return jnp.einsum('bqhk,bqkhd->bqhd', p, v_g)

    out = lax.map(one_block, (jnp.arange(nblk), blocks(q), blocks(q_idx), blocks(w_idx)))
    return jnp.moveaxis(out, 0, 1).reshape(Bsz, S, WIDTH_B)


def clamped_swiglu(u):
    u_glu, u_lin = u[..., ::2], u[..., 1::2]
    u_glu = jnp.minimum(u_glu, SWIGLU_LIMIT)
    u_lin = jnp.clip(u_lin, -SWIGLU_LIMIT, SWIGLU_LIMIT)
    return u_glu * jax.nn.sigmoid(SWIGLU_ALPHA * u_glu) * (u_lin + 1.0)


def moe_ffn(h, router_w, router_b, w1, b1, w2, b2):
    T, D = h.shape
    TK = T * TOP_K
    logits = (h @ router_w + router_b).astype(jnp.float32)
    top_val, top_idx = lax.top_k(logits, TOP_K)
    gate = jax.nn.softmax(top_val, axis=-1)
    flat_e = top_idx.reshape(-1)
    flat_tok = jnp.arange(TK) // TOP_K
    order = jnp.argsort(flat_e)
    sorted_e = flat_e[order]
    sorted_tok = flat_tok[order]
    gate_sorted = gate.reshape(-1)[order]
    counts = jnp.bincount(flat_e, length=N_EXPERTS)
    padded_counts = (counts + MOE_BLOCK - 1) // MOE_BLOCK * MOE_BLOCK
    group_start = jnp.cumsum(counts) - counts
    padded_end = jnp.cumsum(padded_counts)
    padded_start = padded_end - padded_counts
    dest = padded_start[sorted_e] + (jnp.arange(TK) - group_start[sorted_e])
    n_blocks = -(-TK // MOE_BLOCK) + N_EXPERTS
    P = n_blocks * MOE_BLOCK
    buf_tok = jnp.zeros((P,), jnp.int32).at[dest].set(sorted_tok)
    block_e = jnp.minimum(jnp.searchsorted(padded_end, jnp.arange(n_blocks) * MOE_BLOCK, side='right'), N_EXPERTS - 1)
    x_buf = h[buf_tok].reshape(n_blocks, MOE_BLOCK, D)

    def block_ffn(args):
        xb, e = args
        act = clamped_swiglu(xb @ w1[e] + b1[e])
        return act @ w2[e] + b2[e]

    y_buf = lax.map(block_ffn, (x_buf, block_e)).reshape(P, D)
    contrib = gate_sorted[:, None].astype(h.dtype) * y_buf[dest]
    return jnp.zeros((T, D), h.dtype).at[sorted_tok].add(contrib)


def setup_inputs(seed: int = 0) -> dict:
    key = jax.random.key(seed)
    ks = jax.random.split(key, 24)
    f32 = jnp.float32
    nrm = lambda k, shape, s: jax.random.normal(k, shape, f32) * s
    D = D_MODEL
    return {
        'x': nrm(ks[0], (BATCH, SEQ, D), 1.0),
        'c': nrm(ks[1], (BATCH, D), 1.0),
        'rel_bias': nrm(ks[2], (REL_BUCKETS, N_HEADS_B), 0.5),
        'mod_w': nrm(ks[3], (DEPTH, D, 6 * D), 0.5 * D ** -0.5),
        'mod_b': nrm(ks[4], (DEPTH, 6 * D), 0.02),
        'norm_mix_w': 1.0 + nrm(ks[5], (DEPTH, D), 0.02),
        'norm_ffn_w': 1.0 + nrm(ks[6], (DEPTH, D), 0.02),
        'w_in': nrm(ks[7], (DEPTH, D, D_IN), D ** -0.5),
        'conv_w': nrm(ks[8], (DEPTH, CONV_K, 3 * WIDTH_A), 0.5),
        'a_log': jnp.log(jax.random.uniform(ks[9], (DEPTH, N_HEADS_A), f32, 1.0, 16.0)),
        'dt_bias': nrm(ks[10], (DEPTH, N_HEADS_A), 0.1),
        'gdn_norm_w': 1.0 + nrm(ks[11], (DEPTH, HEAD_DIM_A), 0.02),
        'q_norm_w': 1.0 + nrm(ks[12], (DEPTH, HEAD_DIM_B), 0.02),
        'k_norm_w': 1.0 + nrm(ks[13], (DEPTH, HEAD_DIM_B), 0.02),
        'w_out': nrm(ks[14], (DEPTH, D, D), D ** -0.5),
        'router_w': nrm(ks[15], (DEPTH, D, N_EXPERTS), D ** -0.5),
        'router_b': nrm(ks[16], (DEPTH, N_EXPERTS), 0.01),
        'w1': nrm(ks[17], (DEPTH, N_EXPERTS, D, 2 * D_FF_EXPERT), D ** -0.5),
        'b1': nrm(ks[18], (DEPTH, N_EXPERTS, 2 * D_FF_EXPERT), 0.02),
        'w2': nrm(ks[19], (DEPTH, N_EXPERTS, D_FF_EXPERT, D), D_FF_EXPERT ** -0.5),
        'b2': nrm(ks[20], (DEPTH, N_EXPERTS, D), 0.02),
    }


def reference(x, c, rel_bias, mod_w, mod_b, norm_mix_w, norm_ffn_w, w_in, conv_w, a_log, dt_bias,
              gdn_norm_w, q_norm_w, k_norm_w, w_out, router_w, router_b, w1, b1, w2, b2):
    Bsz, S, D = x.shape
    split_pts = np.cumsum(IN_SPLITS)[:-1].tolist()
    for l in range(DEPTH):
        mod = (jax.nn.silu(c) @ mod_w[l] + mod_b[l])[:, None, :]
        sh1, sc1, g1, sh2, sc2, g2 = jnp.split(mod, 6, axis=-1)
        h = rms_norm(x, norm_mix_w[l]) * (1.0 + sc1) + sh1
        proj = h @ w_in[l]
        qkv_a, z_a, b_a, a_a, qkv_b, qi_b, ki_b, wi_b = jnp.split(proj, split_pts, axis=-1)
        y_a = gated_deltanet(qkv_a, z_a, b_a, a_a, conv_w[l], a_log[l], dt_bias[l], gdn_norm_w[l])
        y_b = dsa_attention(qkv_b, qi_b, ki_b, wi_b, q_norm_w[l], k_norm_w[l], rel_bias)
        x = x + g1 * (jnp.concatenate([y_a, y_b], axis=-1) @ w_out[l])
        h = rms_norm(x, norm_ffn_w[l]) * (1.0 + sc2) + sh2
        y = moe_ffn(h.reshape(Bsz * S, D), router_w[l], router_b[l], w1[l], b1[l], w2[l], b2[l])
        x = x + g2 * y.reshape(Bsz, S, D)
    return x
```

```python
import functools
import math

import jax
import jax.numpy as jnp
import numpy as np
from jax import lax
from jax.experimental import pallas as pl
from jax.experimental.pallas import tpu as pltpu

F32 = jnp.float32
BF16 = jnp.bfloat16
I32 = jnp.int32
U32 = jnp.uint32
HIGHEST = lax.Precision.HIGHEST

LANES = 128
SUBLANES = 8
VMEM_LIMIT_BYTES = 56 * 1024 * 1024

CHUNK = 64
HEAD_DIM_A = 128
N_HEADS_A = 4
WIDTH_A = N_HEADS_A * HEAD_DIM_A
CONV_K = 4
HEAD_DIM_B = 64
N_HEADS_B = 8
WIDTH_B = N_HEADS_B * HEAD_DIM_B
IDX_HEADS = 8
IDX_DIM = 64
TOPK_KEYS_MAX = 256
REL_BUCKETS = 32
REL_MAX_DIST = 1024
N_EXPERTS = 32
TOP_K = 4
SWIGLU_ALPHA = 1.702
SWIGLU_LIMIT = 7.0
EPS = 1e-6
NEG_BIG = -1e30

C_QKVA = 0
C_Z = C_QKVA + 3 * WIDTH_A
C_QB = C_Z + WIDTH_A
C_KB = C_QB + WIDTH_B
C_VB = C_KB + WIDTH_B
C_QI = C_VB + WIDTH_B
C_SMALL = C_QI + IDX_HEADS * IDX_DIM
D_IN_PAD = C_SMALL + LANES
S_KIDX = 0
S_B = IDX_DIM
S_A = S_B + N_HEADS_A
S_WIDX = S_A + N_HEADS_A


def _cparams(sem):
    return pltpu.CompilerParams(dimension_semantics=sem, vmem_limit_bytes=VMEM_LIMIT_BYTES)


def _silu(x):
    return x * (1.0 / (1.0 + jnp.exp(-x)))


def _sigmoid(x):
    return 1.0 / (1.0 + jnp.exp(-x))


def _softplus(x):
    return jnp.maximum(x, 0.0) + jnp.log(1.0 + jnp.exp(-jnp.abs(x)))


def _mod_kernel(c_ref, w_ref, b_ref, o_ref):
    a = _silu(c_ref[...])
    o_ref[0] = jnp.dot(a, w_ref[0], precision=HIGHEST, preferred_element_type=F32) + b_ref[0]


def _modulation(c, mod_w, mod_b):
    depth, d, n = mod_w.shape
    bsz = c.shape[0]
    rows = -(-bsz // SUBLANES) * SUBLANES
    c_pad = jnp.zeros((rows, d), F32).at[:bsz].set(c)
    tn = 1536
    out = pl.pallas_call(
        _mod_kernel,
        grid=(depth, n // tn),
        in_specs=[
            pl.BlockSpec((rows, d), lambda l, j: (0, 0)),
            pl.BlockSpec((1, d, tn), lambda l, j: (l, 0, j)),
            pl.BlockSpec((1, 1, tn), lambda l, j: (l, 0, j)),
        ],
        out_specs=pl.BlockSpec((1, rows, tn), lambda l, j: (l, 0, j)),
        out_shape=jax.ShapeDtypeStruct((depth, rows, n), F32),
        compiler_params=_cparams(("arbitrary", "arbitrary")),
        name="adaln_mod",
    )(c_pad, mod_w, mod_b.reshape(depth, 1, n))
    return out[:, :bsz].reshape(depth, bsz, 6, d)


def _head_rms(t, group_ref, wn, inv_dim):
    t2 = t * t
    hi = t2.astype(BF16)
    lo = (t2 - hi.astype(F32)).astype(BF16)
    ss = (jnp.dot(hi, group_ref[...], preferred_element_type=F32)
          + jnp.dot(lo, group_ref[...], preferred_element_type=F32))
    return t * lax.rsqrt(ss * inv_dim + EPS) * wn


def _inproj_kernel(x_ref, mod_ref, nw_ref, w_ref, group_ref, qn_ref, kn_ref,
                   qkva_ref, z_ref, qb_ref, kb_ref, vb_ref, qi_ref, small_ref):
    x = x_ref[0]
    ms = jnp.mean(x * x, axis=-1, keepdims=True)
    y = x * lax.rsqrt(ms + EPS) * nw_ref[...]
    h = y * (1.0 + mod_ref[0, 1:2, :]) + mod_ref[0, 0:1, :]
    hb = h.astype(BF16)

    def mm(lo, width):
        return jnp.dot(hb, w_ref[:, lo:lo + width], preferred_element_type=F32)

    qkva_ref[0] = mm(C_QKVA, 3 * WIDTH_A)
    z_ref[0] = mm(C_Z, WIDTH_A)
    q = _head_rms(mm(C_QB, WIDTH_B), group_ref, qn_ref[...], 1.0 / HEAD_DIM_B)
    qb_ref[0] = (q * (HEAD_DIM_B ** -0.5)).astype(BF16)
    k = _head_rms(mm(C_KB, WIDTH_B), group_ref, kn_ref[...], 1.0 / HEAD_DIM_B)
    kb_ref[0] = k.astype(BF16)
    vb_ref[0] = mm(C_VB, WIDTH_B).astype(BF16)
    qi_ref[0] = mm(C_QI, IDX_HEADS * IDX_DIM).astype(BF16)
    small_ref[0] = mm(C_SMALL, LANES)


def _permute_w_in(w_in_l):
    d = w_in_l.shape[0]
    o = 0
    qkva = w_in_l[:, o:o + 3 * WIDTH_A]; o += 3 * WIDTH_A
    z = w_in_l[:, o:o + WIDTH_A]; o += WIDTH_A
    b = w_in_l[:, o:o + N_HEADS_A]; o += N_HEADS_A
    a = w_in_l[:, o:o + N_HEADS_A]; o += N_HEADS_A
    qkvb = w_in_l[:, o:o + 3 * WIDTH_B]; o += 3 * WIDTH_B
    qi = w_in_l[:, o:o + IDX_HEADS * IDX_DIM]; o += IDX_HEADS * IDX_DIM
    ki = w_in_l[:, o:o + IDX_DIM]; o += IDX_DIM
    wi = w_in_l[:, o:o + IDX_HEADS]; o += IDX_HEADS
    pad = jnp.zeros((d, LANES - IDX_DIM - 2 * N_HEADS_A - IDX_HEADS), w_in_l.dtype)
    return jnp.concatenate([qkva, z, qkvb, qi, ki, b, a, wi, pad], axis=1).astype(BF16)


def _group_ones(width, group):
    g = np.arange(width) // group
    return jnp.asarray((g[:, None] == g[None, :]).astype(np.float32), dtype=BF16)


def _inproj(x, mod_l, norm_w, w_perm, q_norm_w, k_norm_w, tm):
    bsz, s, d = x.shape
    f = lambda b, i: (b, i, 0)
    const2 = lambda b, i: (0, 0)
    outs = [
        (3 * WIDTH_A, F32), (WIDTH_A, F32), (WIDTH_B, BF16), (WIDTH_B, BF16), (WIDTH_B, BF16),
        (IDX_HEADS * IDX_DIM, BF16), (LANES, F32),
    ]
    return pl.pallas_call(
        _inproj_kernel,
        grid=(bsz, s // tm),
        in_specs=[
            pl.BlockSpec((1, tm, d), f),
            pl.BlockSpec((1, 6, d), lambda b, i: (b, 0, 0)),
            pl.BlockSpec((1, d), const2),
            pl.BlockSpec((d, D_IN_PAD), const2),
            pl.BlockSpec((WIDTH_B, WIDTH_B), const2),
            pl.BlockSpec((1, WIDTH_B), const2),
            pl.BlockSpec((1, WIDTH_B), const2),
        ],
        out_specs=[pl.BlockSpec((1, tm, w), f) for w, _ in outs],
        out_shape=[jax.ShapeDtypeStruct((bsz, s, w), dt) for w, dt in outs],
        compiler_params=_cparams(("parallel", "parallel")),
        name="inproj",
    )(x, mod_l, norm_w.reshape(1, d), w_perm, _group_ones(WIDTH_B, HEAD_DIM_B),
      jnp.tile(q_norm_w, N_HEADS_B).reshape(1, WIDTH_B), jnp.tile(k_norm_w, N_HEADS_B).reshape(1, WIDTH_B))


def _dot_nt(a, b, precision=None):
    return lax.dot_general(a, b, (((1,), (1,)), ((), ())), precision=precision, preferred_element_type=F32)


def _mm_bf16(a, b):
    return jnp.dot(a.astype(BF16), b.astype(BF16), preferred_element_type=F32)


def _mm_f32(a, b):
    return jnp.dot(a, b, precision=HIGHEST, preferred_element_type=F32)


def _gdn_kernel(qkv_ref, z_ref, small_ref, barow_ref, convw_ref, alog_ref, dtb_ref, nw_ref,
                y_ref, xe_ref, u_ref, state_ref, *, sb):
    n_chunks = sb // CHUNK
    halo = SUBLANES

    @pl.when(pl.program_id(1) == 0)
    def _():
        xe_ref[0:halo, :] = jnp.zeros((halo, 3 * WIDTH_A), F32)
        state_ref[...] = jnp.zeros_like(state_ref)

    xe_ref[halo:halo + sb, :] = qkv_ref[0]

    rows = 128
    for g in range(3 * WIDTH_A // LANES):
        cs = slice(g * LANES, (g + 1) * LANES)
        for r in range(sb // rows):
            base = halo - (CONV_K - 1) + r * rows
            acc = xe_ref[base:base + rows, cs] * convw_ref[0:1, cs]
            for j in range(1, CONV_K):
                acc = acc + xe_ref[base + j:base + j + rows, cs] * convw_ref[j:j + 1, cs]
            u_ref[r * rows:(r + 1) * rows, cs] = _silu(acc)

    xe_ref[0:halo, :] = xe_ref[sb:sb + halo, :]

    ii = lax.broadcasted_iota(I32, (CHUNK, CHUNK), 0)
    jj = lax.broadcasted_iota(I32, (CHUNK, CHUNK), 1)
    eye = (ii == jj).astype(F32)

    def chunk_body(c, carry):
        r0 = pl.multiple_of(c * CHUNK, CHUNK)
        rs = pl.ds(r0, CHUNK)
        for h in range(N_HEADS_A):
            hs = slice(h * HEAD_DIM_A, (h + 1) * HEAD_DIM_A)
            q = u_ref[rs, h * HEAD_DIM_A:(h + 1) * HEAD_DIM_A]
            k = u_ref[rs, WIDTH_A + h * HEAD_DIM_A:WIDTH_A + (h + 1) * HEAD_DIM_A]
            v = u_ref[rs, 2 * WIDTH_A + h * HEAD_DIM_A:2 * WIDTH_A + (h + 1) * HEAD_DIM_A]
            qn = q * (lax.rsqrt(jnp.sum(q * q, axis=-1, keepdims=True) + EPS) * (HEAD_DIM_A ** -0.5))
            kn = k * lax.rsqrt(jnp.sum(k * k, axis=-1, keepdims=True) + EPS)

            neg_ea = -jnp.exp(alog_ref[0:1, h:h + 1])
            dtb = dtb_ref[0:1, h:h + 1]
            b_col = small_ref[0, rs, S_B + h:S_B + h + 1]
            a_col = small_ref[0, rs, S_A + h:S_A + h + 1]
            a_row = barow_ref[0, c, N_HEADS_A + h:N_HEADS_A + h + 1, :]
            beta = _sigmoid(b_col)
            g_col = neg_ea * _softplus(a_col + dtb)
            g_row = neg_ea * _softplus(a_row + dtb)
            gc_col = jnp.sum(jnp.where(jj <= ii, g_row, 0.0), axis=1, keepdims=True)
            gc_row = jnp.sum(jnp.where(ii <= jj, g_col, 0.0), axis=0, keepdims=True)
            decay = jnp.exp(jnp.where(ii >= jj, gc_col - gc_row, NEG_BIG))

            k_beta = kn * beta
            kk = _dot_nt(k_beta.astype(BF16), kn.astype(BF16))
            a_mat = -jnp.where(ii > jj, kk * decay, 0.0)
            t_mat = eye + a_mat
            p_mat = a_mat
            for _ in range(int(math.log2(CHUNK)) - 1):
                p_mat = _mm_f32(p_mat, p_mat)
                t_mat = t_mat + _mm_f32(t_mat, p_mat)
            egc = jnp.exp(gc_col)
            u_val = _mm_f32(t_mat, v * beta)
            w_dec = _mm_f32(t_mat, k_beta * egc)
            attn = _dot_nt(qn.astype(BF16), kn.astype(BF16)) * decay

            state = state_ref[h]
            v_new = u_val - _mm_bf16(w_dec, state)
            o = _mm_bf16(qn * egc, state) + _mm_bf16(attn, v_new)
            g_last = gc_col[CHUNK - 1:CHUNK, :]
            k_dec = kn * jnp.exp(g_last - gc_col)
            state_ref[h] = state * jnp.exp(g_last) + _mm_bf16(k_dec.T, v_new)

            on = o * lax.rsqrt(jnp.mean(o * o, axis=-1, keepdims=True) + EPS) * nw_ref[...]
            y_ref[0, rs, hs] = (on * _silu(z_ref[0, rs, hs])).astype(BF16)
        return carry

    lax.fori_loop(0, n_chunks, chunk_body, 0)


def _gdn(qkv_a, z_a, small, conv_w, a_log, dt_bias, norm_w, sb):
    bsz, s, _ = qkv_a.shape
    n_c = s // CHUNK
    ba = small[:, :, S_B:S_B + 2 * N_HEADS_A].reshape(bsz, n_c, CHUNK, 2 * N_HEADS_A)
    ba_row = jnp.swapaxes(ba, 2, 3)
    f = lambda b, i: (b, i, 0)
    const2 = lambda b, i: (0, 0)
    return pl.pallas_call(
        functools.partial(_gdn_kernel, sb=sb),
        grid=(bsz, s // sb),
        in_specs=[
            pl.BlockSpec((1, sb, 3 * WIDTH_A), f),
            pl.BlockSpec((1, sb, WIDTH_A), f),
            pl.BlockSpec((1, sb, LANES), f),
            pl.BlockSpec((1, sb // CHUNK, 2 * N_HEADS_A, CHUNK), lambda b, i: (b, i, 0, 0)),
            pl.BlockSpec((CONV_K, 3 * WIDTH_A), const2),
            pl.BlockSpec((1, N_HEADS_A), const2),
            pl.BlockSpec((1, N_HEADS_A), const2),
            pl.BlockSpec((1, HEAD_DIM_A), const2),
        ],
        out_specs=pl.BlockSpec((1, sb, WIDTH_A), f),
        out_shape=jax.ShapeDtypeStruct((bsz, s, WIDTH_A), BF16),
        scratch_shapes=[
            pltpu.VMEM((sb + SUBLANES, 3 * WIDTH_A), F32),
            pltpu.VMEM((sb, 3 * WIDTH_A), F32),
            pltpu.VMEM((N_HEADS_A, HEAD_DIM_A, HEAD_DIM_A), F32),
        ],
        compiler_params=_cparams(("parallel", "arbitrary")),
        name="gdn",
    )(qkv_a, z_a, small, ba_row, conv_w, a_log.reshape(1, -1), dt_bias.reshape(1, -1), norm_w.reshape(1, -1))


QB = 128
FAR_T = 512
FAR_G = FAR_T // LANES
NEAR_D = 9
NEAR_MIN = 5
INT_MIN = -2 ** 31


def _t5_bucket_np(rel):
    nb = REL_BUCKETS // 2
    max_exact = nb // 2
    side = np.where(rel > 0, nb, 0)
    n = np.abs(rel)
    nf = np.maximum(n, 1).astype(np.float32)
    large = max_exact + (np.log(nf / np.float32(max_exact)) / np.float32(math.log(REL_MAX_DIST / max_exact))
                         * np.float32(nb - max_exact)).astype(np.int32)
    large = np.minimum(large, nb - 1)
    return (side + np.where(n < max_exact, n, large)).astype(np.int32)


def _near_bucket_table():
    r = np.arange(QB)[:, None]
    c = np.arange(LANES)[None, :]
    return np.stack([_t5_bucket_np(c - r - LANES * d) for d in range(NEAR_D)])


FAR_BUCKET = int(_t5_bucket_np(np.array([-(NEAR_MIN * LANES + 1)]))[0])
assert all(int(b) == FAR_BUCKET for b in _t5_bucket_np(-np.arange((NEAR_MIN + 1) * LANES - (QB - 1), 1 << 20, 997)))


def _sortable_key(score):
    bits = pltpu.bitcast(score + 0.0, I32)
    return bits ^ ((bits >> 31) & 0x7FFFFFFF)


def _dsa_kernel(rb_ref, qb_ref, qi_ref, small_ref, kb_ref, vb_ref, kidx2_ref, tab_ref,
                y_ref, qis_ref, qs_ref, wb_ref, sdot_ref, keys_ref, nbias_ref, thr_ref, jlim_ref,
                m_ref, l_ref, acc_ref, *, seq, k_sel):
    i = pl.program_id(1)
    lane = lax.broadcasted_iota(I32, (QB, LANES), 1)
    row = lax.broadcasted_iota(I32, (QB, LANES), 0)
    even_f = (lane < HEAD_DIM_B).astype(F32)
    even_b = even_f.astype(BF16)
    odd_b = (1.0 - even_f).astype(BF16)

    @pl.when(i == 0)
    def _():
        nbias_ref[...] = jnp.zeros_like(nbias_ref)

        def d_body(d, c0):
            tab = tab_ref[d]

            def b_body(bk, c1):
                hit = tab == bk
                for h in range(N_HEADS_B):
                    nbias_ref[d * N_HEADS_B + h] = jnp.where(hit, rb_ref[bk, h], nbias_ref[d * N_HEADS_B + h])
                return c1

            return lax.fori_loop(0, REL_BUCKETS, b_body, c0)

        lax.fori_loop(0, NEAR_D, d_body, 0)

    for p in range(N_HEADS_B // 2):
        ps = slice(p * LANES, (p + 1) * LANES)
        qi_pair = qi_ref[0, :, ps]
        qis_ref[(2 * p) * QB:(2 * p + 1) * QB, :] = qi_pair * even_b
        qis_ref[(2 * p + 1) * QB:(2 * p + 2) * QB, :] = qi_pair * odd_b
        q_pair = qb_ref[0, :, ps]
        qs_ref[2 * p] = q_pair * even_b
        qs_ref[2 * p + 1] = q_pair * odd_b
    w_scale = IDX_HEADS ** -0.5 * IDX_DIM ** -0.5
    for h in range(IDX_HEADS):
        wb_ref[h] = jnp.broadcast_to(small_ref[0, :, S_WIDX + h:S_WIDX + h + 1] * w_scale, (QB, LANES))

    limit = i * QB + CHUNK + jnp.where(row >= CHUNK, CHUNK, 0)

    def score_body(t, c0):
        k0 = pl.multiple_of(t * FAR_T, FAR_T)
        sdot_ref[...] = _dot_nt(qis_ref[...], kidx2_ref[0, pl.ds(k0, FAR_T), :])
        for g in range(FAR_G):
            gs = slice(g * LANES, (g + 1) * LANES)
            acc = jnp.maximum(sdot_ref[0:QB, gs], 0.0) * wb_ref[0]
            for h in range(1, IDX_HEADS):
                acc = acc + jnp.maximum(sdot_ref[h * QB:(h + 1) * QB, gs], 0.0) * wb_ref[h]
            col = k0 + g * LANES + lane
            keys_ref[:, pl.ds(pl.multiple_of(k0 + g * LANES, LANES), LANES)] = jnp.where(
                col < limit, _sortable_key(acc), INT_MIN)
        return c0

    lax.fori_loop(0, i // FAR_G + 1, score_body, 0)

    n_groups = i + 1

    def count(pred):
        def g_body(g, acc):
            kt = keys_ref[:, pl.ds(pl.multiple_of(g * LANES, LANES), LANES)]
            col = g * LANES + lane
            return acc + jnp.where(pred(kt, col), 1, 0)

        acc = lax.fori_loop(0, n_groups, g_body, jnp.zeros((QB, LANES), I32))
        return jnp.broadcast_to(jnp.sum(acc, axis=1, keepdims=True), (QB, LANES))

    thr_ref[...] = jnp.full((QB, LANES), INT_MIN, I32)
    jlim_ref[...] = jnp.full((QB, LANES), -1, I32)

    @pl.when(n_groups * QB > k_sel)
    def _():
        def bit_body(step, r):
            cand = r + lax.shift_left(jnp.int32(1), 31 - step)
            cnt = count(lambda kt, col: kt >= cand)
            return jnp.where(cnt >= k_sel, cand, r)

        r = lax.fori_loop(0, 32, bit_body, jnp.full((QB, LANES), INT_MIN, I32))
        thr_ref[...] = r
        need = k_sel - count(lambda kt, col: kt > r)
        excess = count(lambda kt, col: kt == r) - need
        jlim_ref[...] = jnp.where(r == INT_MIN, -1, seq)

        @pl.when(jnp.max(excess) > 0)
        def _():
            def j_body(step, jl):
                cand = jl + lax.shift_left(jnp.int32(1), (seq.bit_length() - 1) - step)
                cnt = count(lambda kt, col: (kt == r) & (col < cand))
                return jnp.where(cnt < need, cand, jl)

            jl = lax.fori_loop(0, seq.bit_length(), j_body, jnp.zeros((QB, LANES), I32))
            jlim_ref[...] = jnp.where(r == INT_MIN, -1, jl)

    m_ref[...] = jnp.full(m_ref.shape, NEG_BIG, F32)
    l_ref[...] = jnp.zeros_like(l_ref)
    acc_ref[...] = jnp.zeros_like(acc_ref)

    def sel_mask(k0, width):
        parts = []
        for g in range(width // LANES):
            kt = keys_ref[:, pl.ds(pl.multiple_of(k0 + g * LANES, LANES), LANES)]
            col = k0 + g * LANES + lane
            sel = (kt > thr_ref[...]) | ((kt == thr_ref[...]) & (col <= jlim_ref[...]))
            parts.append(jnp.where(sel, 0.0, -jnp.inf))
        return parts[0] if len(parts) == 1 else jnp.concatenate(parts, axis=1)

    def attend(h, k0, width, negm, bias):
        ps = slice((h // 2) * LANES, (h // 2 + 1) * LANES)
        ks = pl.ds(pl.multiple_of(k0, LANES), width)
        s = _dot_nt(qs_ref[h], kb_ref[0, ks, ps]) + bias + negm
        m_old = m_ref[h]
        m_new = jnp.maximum(m_old, jnp.max(s, axis=1, keepdims=True))
        alpha = jnp.exp(m_old - m_new)
        reps = width // LANES
        p = jnp.exp(s - (m_new if reps == 1 else jnp.concatenate([m_new] * reps, axis=1)))
        l_ref[h] = alpha * l_ref[h] + jnp.sum(p, axis=1, keepdims=True)
        acc_ref[h] = alpha * acc_ref[h] + jnp.dot(p.astype(BF16), vb_ref[0, ks, ps], preferred_element_type=F32)
        m_ref[h] = m_new

    near0 = (jnp.maximum(i - NEAR_MIN, 0) // FAR_G) * FAR_G

    def far_body(t, c0):
        k0 = pl.multiple_of(t * FAR_T, FAR_T)
        negm = sel_mask(k0, FAR_T)
        for h in range(N_HEADS_B):
            attend(h, k0, FAR_T, negm, rb_ref[FAR_BUCKET, h])
        return c0

    lax.fori_loop(0, near0 // FAR_G, far_body, 0)

    def near_body(j, c0):
        k0 = pl.multiple_of(j * LANES, LANES)
        negm = sel_mask(k0, LANES)
        for h in range(N_HEADS_B):
            attend(h, k0, LANES, negm, nbias_ref[(i - j) * N_HEADS_B + h])
        return c0

    lax.fori_loop(near0, i + 1, near_body, 0)

    for p in range(N_HEADS_B // 2):
        o_even = acc_ref[2 * p] / l_ref[2 * p]
        o_odd = acc_ref[2 * p + 1] / l_ref[2 * p + 1]
        y_ref[0, :, p * LANES:(p + 1) * LANES] = jnp.where(lane < HEAD_DIM_B, o_even, o_odd).astype(BF16)


def _dsa(qb, kb, vb, qi, small, rel_bias):
    bsz, s, _ = qb.shape
    assert s % FAR_T == 0
    k_sel = min(TOPK_KEYS_MAX, s // 4)
    kidx = small[:, :, S_KIDX:S_KIDX + IDX_DIM].astype(BF16)
    kidx2 = jnp.concatenate([kidx, kidx], axis=-1)
    tab = jnp.asarray(_near_bucket_table())
    blk = lambda b, i: (b, i, 0)
    full = lambda b, i: (b, 0, 0)
    one = pl.Buffered(1)
    return pl.pallas_call(
        functools.partial(_dsa_kernel, seq=s, k_sel=k_sel),
        grid=(bsz, s // QB),
        in_specs=[
            pl.BlockSpec(memory_space=pltpu.SMEM),
            pl.BlockSpec((1, QB, WIDTH_B), blk),
            pl.BlockSpec((1, QB, IDX_HEADS * IDX_DIM), blk),
            pl.BlockSpec((1, QB, LANES), blk),
            pl.BlockSpec((1, s, WIDTH_B), full, pipeline_mode=one),
            pl.BlockSpec((1, s, WIDTH_B), full, pipeline_mode=one),
            pl.BlockSpec((1, s, LANES), full, pipeline_mode=one),
            pl.BlockSpec((NEAR_D, QB, LANES), lambda b, i: (0, 0, 0), pipeline_mode=one),
        ],
        out_specs=pl.BlockSpec((1, QB, WIDTH_B), blk),
        out_shape=jax.ShapeDtypeStruct((bsz, s, WIDTH_B), BF16),
        scratch_shapes=[
            pltpu.VMEM((IDX_HEADS * QB, LANES), BF16),
            pltpu.VMEM((N_HEADS_B, QB, LANES), BF16),
            pltpu.VMEM((IDX_HEADS, QB, LANES), F32),
            pltpu.VMEM((IDX_HEADS * QB, FAR_T), F32),
            pltpu.VMEM((QB, s), I32),
            pltpu.VMEM((NEAR_D * N_HEADS_B, QB, LANES), F32),
            pltpu.VMEM((QB, LANES), I32),
            pltpu.VMEM((QB, LANES), I32),
            pltpu.VMEM((N_HEADS_B, QB, LANES), F32),
            pltpu.VMEM((N_HEADS_B, QB, LANES), F32),
            pltpu.VMEM((N_HEADS_B, QB, LANES), F32),
        ],
        compiler_params=_cparams(("parallel", "arbitrary")),
        name="dsa",
    )(rel_bias, qb, qi, small, kb, vb, kidx2, tab)


HALF_MASK = 0xFFFF0000


def _pack_halves(t):
    w = t.shape[1] // 2
    bits = pltpu.bitcast(t.astype(BF16).astype(F32), U32)
    return (bits[:, :w] >> 16) | (bits[:, w:] & jnp.uint32(HALF_MASK))


def _unpack_halves(p):
    lo = pltpu.bitcast(p << 16, F32)
    hi = pltpu.bitcast(p & jnp.uint32(HALF_MASK), F32)
    return jnp.concatenate([lo, hi], axis=1)


def _outproj_kernel(ya_ref, yb_ref, x_ref, mod_ref, wo_ref, nw_ref, rw_ref, rbias_ref,
                    xn_ref, hp_ref, ridx_ref, gate_ref):
    wa = ya_ref.shape[2]
    y = (jnp.dot(ya_ref[0], wo_ref[0:wa, :], preferred_element_type=F32)
         + jnp.dot(yb_ref[0], wo_ref[wa:, :], preferred_element_type=F32))
    xn = x_ref[0] + mod_ref[0, 2:3, :] * y
    xn_ref[0] = xn
    ms = jnp.mean(xn * xn, axis=-1, keepdims=True)
    h = xn * lax.rsqrt(ms + EPS) * nw_ref[...] * (1.0 + mod_ref[0, 4:5, :]) + mod_ref[0, 3:4, :]
    hp_ref[0] = _pack_halves(h)

    logits = jnp.dot(h, rw_ref[...], precision=HIGHEST, preferred_element_type=F32) + rbias_ref[...]
    lane = lax.broadcasted_iota(I32, logits.shape, 1)
    cur = logits
    vals, ridx = [], jnp.zeros(logits.shape, I32)
    for k in range(TOP_K):
        mx = jnp.max(cur, axis=1, keepdims=True)
        am = jnp.min(jnp.where(cur == mx, lane, LANES), axis=1, keepdims=True)
        cur = jnp.where(lane == am, -jnp.inf, cur)
        vals.append(mx)
        ridx = jnp.where(lane == k, am, ridx)
    ex = [jnp.exp(v - vals[0]) for v in vals]
    inv = 1.0 / (ex[0] + ex[1] + ex[2] + ex[3])
    gate = jnp.zeros(logits.shape, F32)
    for k in range(TOP_K):
        gate = jnp.where(lane == k, ex[k] * inv, gate)
    ridx_ref[0] = ridx
    gate_ref[0] = gate


def _outproj(y_a, y_b, x, mod_l, w_out_bf, norm_w, router_w, router_b, tm):
    bsz, s, d = x.shape
    n_e = router_w.shape[1]
    rw = jnp.zeros((d, LANES), F32).at[:, :n_e].set(router_w)
    rbias = jnp.full((1, LANES), NEG_BIG, F32).at[0, :n_e].set(router_b)
    blk = lambda b, i: (b, i, 0)
    const2 = lambda b, i: (0, 0)
    return pl.pallas_call(
        _outproj_kernel,
        grid=(bsz, s // tm),
        in_specs=[
            pl.BlockSpec((1, tm, y_a.shape[2]), blk),
            pl.BlockSpec((1, tm, y_b.shape[2]), blk),
            pl.BlockSpec((1, tm, d), blk),
            pl.BlockSpec((1, 6, d), lambda b, i: (b, 0, 0)),
            pl.BlockSpec((d, d), const2),
            pl.BlockSpec((1, d), const2),
            pl.BlockSpec((d, LANES), const2),
            pl.BlockSpec((1, LANES), const2),
        ],
        out_specs=[pl.BlockSpec((1, tm, d), blk), pl.BlockSpec((1, tm, d // 2), blk),
                   pl.BlockSpec((1, tm, LANES), blk), pl.BlockSpec((1, tm, LANES), blk)],
        out_shape=[jax.ShapeDtypeStruct((bsz, s, d), F32), jax.ShapeDtypeStruct((bsz, s, d // 2), U32),
                   jax.ShapeDtypeStruct((bsz, s, LANES), I32), jax.ShapeDtypeStruct((bsz, s, LANES), F32)],
        compiler_params=_cparams(("parallel", "parallel")),
        name="outproj_router",
    )(y_a, y_b, x, mod_l, w_out_bf, norm_w.reshape(1, d), rw, rbias)


MOE_TB = 2048
MOE_RB = 512
MOE_M = 128


def _moe_kernel(cnt_ref, off_ref, list_ref, hp_ref, w1_ref, b1_ref, w2_ref, b2_ref, gate_ref, x_ref, g2_ref,
                o_ref, slots_ref, xg_ref, yb_ref, *, tb, rb):
    sb = pl.program_id(0)
    e = pl.program_id(1)
    dff = w2_ref.shape[1]
    last = TOP_K * tb - 1

    @pl.when(e < N_EXPERTS)
    def _():
        cnt = cnt_ref[sb * N_EXPERTS + e]
        off = off_ref[sb * N_EXPERTS + e]

        def chunk(c, carry):
            base = off + c * MOE_M

            def gather(r, c1):
                code = list_ref[0, 0, jnp.minimum(base + r, last)]
                xg_ref[pl.ds(r, 1), :] = hp_ref[pl.ds(code >> 2, 1), :]
                return c1

            lax.fori_loop(0, MOE_M, gather, 0, unroll=8)
            xb = _unpack_halves(xg_ref[...]).astype(BF16)
            u = jnp.dot(xb, w1_ref[0], preferred_element_type=F32) + b1_ref[0]
            glu = jnp.minimum(u[:, :dff], SWIGLU_LIMIT)
            lin = jnp.clip(u[:, dff:], -SWIGLU_LIMIT, SWIGLU_LIMIT)
            act = glu * _sigmoid(SWIGLU_ALPHA * glu) * (lin + 1.0)
            y = jnp.dot(act.astype(BF16), w2_ref[0], preferred_element_type=F32) + b2_ref[0]
            yb_ref[...] = _pack_halves(y)

            def scatter(r, c1):
                code = list_ref[0, 0, base + r]
                slots_ref[code & (TOP_K - 1), pl.ds(code >> 2, 1), :] = yb_ref[pl.ds(r, 1), :]
                return c1

            lax.fori_loop(0, jnp.minimum(cnt - c * MOE_M, MOE_M), scatter, 0)
            return carry

        lax.fori_loop(0, (cnt + MOE_M - 1) // MOE_M, chunk, 0)

    @pl.when(e >= N_EXPERTS)
    def _():
        r0 = pl.multiple_of((e - N_EXPERTS) * rb, rb)
        acc = gate_ref[:, 0:1] * _unpack_halves(slots_ref[0, pl.ds(r0, rb), :])
        for k in range(1, TOP_K):
            acc = acc + gate_ref[:, k:k + 1] * _unpack_halves(slots_ref[k, pl.ds(r0, rb), :])
        o_ref[...] = x_ref[...] + g2_ref[0] * acc


def _moe(xn, hp, ridx, gate, g2, w1p, b1p, w2b, b2, tb, rb):
    bsz, s, d = xn.shape
    t = bsz * s
    n_super = t // tb
    n_piece = tb // rb
    dff = w2b.shape[1]
    flat_e = ridx[:, :, :TOP_K].reshape(n_super, tb * TOP_K)
    order = jnp.argsort(flat_e, axis=1, stable=True).astype(I32)
    counts = jnp.sum(flat_e[:, :, None] == jnp.arange(N_EXPERTS, dtype=I32)[None, None, :], axis=1).astype(I32)
    offs = (jnp.cumsum(counts, axis=1) - counts).astype(I32)

    piece = lambda sb, e, *_: (sb * n_piece + jnp.maximum(e - N_EXPERTS, 0), 0)
    wmap = lambda sb, e, *_: (jnp.minimum(e, N_EXPERTS - 1), 0, 0)
    grid_spec = pltpu.PrefetchScalarGridSpec(
        num_scalar_prefetch=2,
        grid=(n_super, N_EXPERTS + n_piece),
        in_specs=[
            pl.BlockSpec((1, 1, tb * TOP_K), lambda sb, e, *_: (sb, 0, 0), memory_space=pltpu.SMEM),
            pl.BlockSpec((tb, d // 2), lambda sb, e, *_: (sb, 0), pipeline_mode=pl.Buffered(1)),
            pl.BlockSpec((1, d, 2 * dff), wmap),
            pl.BlockSpec((1, 1, 2 * dff), wmap),
            pl.BlockSpec((1, dff, d), wmap),
            pl.BlockSpec((1, 1, d), wmap),
            pl.BlockSpec((rb, LANES), piece),
            pl.BlockSpec((rb, d), piece),
            pl.BlockSpec((1, 1, d), lambda sb, e, *_: ((sb * tb) // s, 0, 0)),
        ],
        out_specs=pl.BlockSpec((rb, d), piece),
        scratch_shapes=[
            pltpu.VMEM((TOP_K, tb, d // 2), U32),
            pltpu.VMEM((MOE_M, d // 2), U32),
            pltpu.VMEM((MOE_M, d // 2), U32),
        ],
    )
    out = pl.pallas_call(
        functools.partial(_moe_kernel, tb=tb, rb=rb),
        grid_spec=grid_spec,
        out_shape=jax.ShapeDtypeStruct((t, d), F32),
        compiler_params=_cparams(("arbitrary", "arbitrary")),
        name="moe",
    )(counts.reshape(-1), offs.reshape(-1), order.reshape(n_super, 1, tb * TOP_K), hp.reshape(t, d // 2),
      w1p, b1p, w2b, b2, gate.reshape(t, LANES), xn.reshape(t, d), g2.reshape(bsz, 1, d))
    return out.reshape(bsz, s, d)


def _prep_expert_weights(w1_l, b1_l, w2_l, b2_l):
    n_e, d, two_f = w1_l.shape
    w1p = jnp.concatenate([w1_l[:, :, 0::2], w1_l[:, :, 1::2]], axis=2).astype(BF16)
    b1p = jnp.concatenate([b1_l[:, 0::2], b1_l[:, 1::2]], axis=1).reshape(n_e, 1, two_f)
    return w1p, b1p, w2_l.astype(BF16), b2_l.reshape(n_e, 1, -1)


def kernel(x, c, rel_bias, mod_w, mod_b, norm_mix_w, norm_ffn_w, w_in, conv_w, a_log, dt_bias, gdn_norm_w,
           q_norm_w, k_norm_w, w_out, router_w, router_b, w1, b1, w2, b2):
    depth = mod_w.shape[0]
    bsz, s, d = x.shape
    mod = _modulation(c, mod_w, mod_b)
    tm = min(512, s)
    tb = min(MOE_TB, bsz * s)
    rb = min(MOE_RB, tb)
    for l in range(depth):
        qkv_a, z_a, qb, kb, vb, qi, small = _inproj(
            x, mod[l], norm_mix_w[l], _permute_w_in(w_in[l]), q_norm_w[l], k_norm_w[l], tm)
        y_a = _gdn(qkv_a, z_a, small, conv_w[l], a_log[l], dt_bias[l], gdn_norm_w[l], sb=tm)
        y_b = _dsa(qb, kb, vb, qi, small, rel_bias)
        xn, hp, ridx, gate = _outproj(y_a, y_b, x, mod[l], w_out[l].astype(BF16), norm_ffn_w[l],
                                      router_w[l], router_b[l], tm)
        x = _moe(xn, hp, ridx, gate, mod[l][:, 5], *_prep_expert_weights(w1[l], b1[l], w2[l], b2[l]), tb, rb)
    return x
```

```python
import functools
import math

import jax
import jax.numpy as jnp
import numpy as np
from jax import lax
from jax.experimental import pallas as pl
from jax.experimental.pallas import tpu as pltpu

F32 = jnp.float32
BF16 = jnp.bfloat16
I32 = jnp.int32
U32 = jnp.uint32
HIGHEST = lax.Precision.HIGHEST

LANES = 128
SUBLANES = 8
VMEM_LIMIT_BYTES = 56 * 1024 * 1024

CHUNK = 64
HEAD_DIM_A = 128
N_HEADS_A = 4
WIDTH_A = N_HEADS_A * HEAD_DIM_A
CONV_K = 4
HEAD_DIM_B = 64
N_HEADS_B = 8
WIDTH_B = N_HEADS_B * HEAD_DIM_B
IDX_HEADS = 8
IDX_DIM = 64
TOPK_KEYS_MAX = 256
REL_BUCKETS = 32
REL_MAX_DIST = 1024
N_EXPERTS = 32
TOP_K = 4
SWIGLU_ALPHA = 1.702
SWIGLU_LIMIT = 7.0
EPS = 1e-6
NEG_BIG = -1e30

C_QKVA = 0
C_Z = C_QKVA + 3 * WIDTH_A
C_QB = C_Z + WIDTH_A
C_KB = C_QB + WIDTH_B
C_VB = C_KB + WIDTH_B
C_QI = C_VB + WIDTH_B
C_SMALL = C_QI + IDX_HEADS * IDX_DIM
D_IN_PAD = C_SMALL + LANES
S_KIDX = 0
S_B = IDX_DIM
S_A = S_B + N_HEADS_A
S_WIDX = S_A + N_HEADS_A


def _cparams(sem):
    return pltpu.CompilerParams(dimension_semantics=sem, vmem_limit_bytes=VMEM_LIMIT_BYTES)


def _silu(x):
    return x * (1.0 / (1.0 + jnp.exp(-x)))


def _sigmoid(x):
    return 1.0 / (1.0 + jnp.exp(-x))


def _softplus(x):
    return jnp.maximum(x, 0.0) + jnp.log(1.0 + jnp.exp(-jnp.abs(x)))


def _mod_kernel(c_ref, w_ref, b_ref, o_ref):
    a = _silu(c_ref[...])
    o_ref[0] = jnp.dot(a, w_ref[0], precision=HIGHEST, preferred_element_type=F32) + b_ref[0]


def _modulation(c, mod_w, mod_b):
    depth, d, n = mod_w.shape
    bsz = c.shape[0]
    rows = -(-bsz // SUBLANES) * SUBLANES
    c_pad = jnp.zeros((rows, d), F32).at[:bsz].set(c)
    tn = 1536
    out = pl.pallas_call(
        _mod_kernel,
        grid=(depth, n // tn),
        in_specs=[
            pl.BlockSpec((rows, d), lambda l, j: (0, 0)),
            pl.BlockSpec((1, d, tn), lambda l, j: (l, 0, j)),
            pl.BlockSpec((1, 1, tn), lambda l, j: (l, 0, j)),
        ],
        out_specs=pl.BlockSpec((1, rows, tn), lambda l, j: (l, 0, j)),
        out_shape=jax.ShapeDtypeStruct((depth, rows, n), F32),
        compiler_params=_cparams(("arbitrary", "arbitrary")),
        name="adaln_mod",
    )(c_pad, mod_w, mod_b.reshape(depth, 1, n))
    return out[:, :bsz].reshape(depth, bsz, 6, d)


def _head_rms(t, group_ref, wn, inv_dim):
    t2 = t * t
    hi = t2.astype(BF16)
    lo = (t2 - hi.astype(F32)).astype(BF16)
    ss = (jnp.dot(hi, group_ref[...], preferred_element_type=F32)
          + jnp.dot(lo, group_ref[...], preferred_element_type=F32))
    return t * lax.rsqrt(ss * inv_dim + EPS) * wn


def _inproj_kernel(x_ref, mod_ref, nw_ref, w_ref, group_ref, qn_ref, kn_ref,
                   qkva_ref, z_ref, qb_ref, kb_ref, vb_ref, qi_ref, small_ref):
    x = x_ref[0]
    ms = jnp.mean(x * x, axis=-1, keepdims=True)
    y = x * lax.rsqrt(ms + EPS) * nw_ref[...]
    h = y * (1.0 + mod_ref[0, 1:2, :]) + mod_ref[0, 0:1, :]
    hb = h.astype(BF16)

    def mm(lo, width):
        return jnp.dot(hb, w_ref[:, lo:lo + width], preferred_element_type=F32)

    qkva_ref[0] = mm(C_QKVA, 3 * WIDTH_A)
    z_ref[0] = mm(C_Z, WIDTH_A)
    q = _head_rms(mm(C_QB, WIDTH_B), group_ref, qn_ref[...], 1.0 / HEAD_DIM_B)
    qb_ref[0] = (q * (HEAD_DIM_B ** -0.5)).astype(BF16)
    k = _head_rms(mm(C_KB, WIDTH_B), group_ref, kn_ref[...], 1.0 / HEAD_DIM_B)
    kb_ref[0] = k.astype(BF16)
    vb_ref[0] = mm(C_VB, WIDTH_B).astype(BF16)
    qi_ref[0] = mm(C_QI, IDX_HEADS * IDX_DIM).astype(BF16)
    small_ref[0] = mm(C_SMALL, LANES)


def _permute_w_in(w_in_l):
    d = w_in_l.shape[0]
    o = 0
    qkva = w_in_l[:, o:o + 3 * WIDTH_A]; o += 3 * WIDTH_A
    z = w_in_l[:, o:o + WIDTH_A]; o += WIDTH_A
    b = w_in_l[:, o:o + N_HEADS_A]; o += N_HEADS_A
    a = w_in_l[:, o:o + N_HEADS_A]; o += N_HEADS_A
    qkvb = w_in_l[:, o:o + 3 * WIDTH_B]; o += 3 * WIDTH_B
    qi = w_in_l[:, o:o + IDX_HEADS * IDX_DIM]; o += IDX_HEADS * IDX_DIM
    ki = w_in_l[:, o:o + IDX_DIM]; o += IDX_DIM
    wi = w_in_l[:, o:o + IDX_HEADS]; o += IDX_HEADS
    pad = jnp.zeros((d, LANES - IDX_DIM - 2 * N_HEADS_A - IDX_HEADS), w_in_l.dtype)
    return jnp.concatenate([qkva, z, qkvb, qi, ki, b, a, wi, pad], axis=1).astype(BF16)


def _group_ones(width, group):
    g = np.arange(width) // group
    return jnp.asarray((g[:, None] == g[None, :]).astype(np.float32), dtype=BF16)


def _inproj(x, mod_l, norm_w, w_perm, q_norm_w, k_norm_w, tm):
    bsz, s, d = x.shape
    f = lambda b, i: (b, i, 0)
    const2 = lambda b, i: (0, 0)
    outs = [
        (3 * WIDTH_A, F32), (WIDTH_A, F32), (WIDTH_B, BF16), (WIDTH_B, BF16), (WIDTH_B, BF16),
        (IDX_HEADS * IDX_DIM, BF16), (LANES, F32),
    ]
    return pl.pallas_call(
        _inproj_kernel,
        grid=(bsz, s // tm),
        in_specs=[
            pl.BlockSpec((1, tm, d), f),
            pl.BlockSpec((1, 6, d), lambda b, i: (b, 0, 0)),
            pl.BlockSpec((1, d), const2),
            pl.BlockSpec((d, D_IN_PAD), const2),
            pl.BlockSpec((WIDTH_B, WIDTH_B), const2),
            pl.BlockSpec((1, WIDTH_B), const2),
            pl.BlockSpec((1, WIDTH_B), const2),
        ],
        out_specs=[pl.BlockSpec((1, tm, w), f) for w, _ in outs],
        out_shape=[jax.ShapeDtypeStruct((bsz, s, w), dt) for w, dt in outs],
        compiler_params=_cparams(("parallel", "parallel")),
        name="inproj",
    )(x, mod_l, norm_w.reshape(1, d), w_perm, _group_ones(WIDTH_B, HEAD_DIM_B),
      jnp.tile(q_norm_w, N_HEADS_B).reshape(1, WIDTH_B), jnp.tile(k_norm_w, N_HEADS_B).reshape(1, WIDTH_B))


def _dot_nt(a, b, precision=None):
    return lax.dot_general(a, b, (((1,), (1,)), ((), ())), precision=precision, preferred_element_type=F32)


def _mm_bf16(a, b):
    return jnp.dot(a.astype(BF16), b.astype(BF16), preferred_element_type=F32)


def _mm_f32(a, b):
    return jnp.dot(a, b, precision=HIGHEST, preferred_element_type=F32)


def _gdn_kernel(qkv_ref, z_ref, small_ref, barow_ref, convw_ref, alog_ref, dtb_ref, nw_ref,
                y_ref, xe_ref, u_ref, state_ref, *, sb):
    n_chunks = sb // CHUNK
    halo = SUBLANES

    @pl.when(pl.program_id(1) == 0)
    def _():
        xe_ref[0:halo, :] = jnp.zeros((halo, 3 * WIDTH_A), F32)
        state_ref[...] = jnp.zeros_like(state_ref)

    xe_ref[halo:halo + sb, :] = qkv_ref[0]

    rows = 128
    for g in range(3 * WIDTH_A // LANES):
        cs = slice(g * LANES, (g + 1) * LANES)
        for r in range(sb // rows):
            base = halo - (CONV_K - 1) + r * rows
            acc = xe_ref[base:base + rows, cs] * convw_ref[0:1, cs]
            for j in range(1, CONV_K):
                acc = acc + xe_ref[base + j:base + j + rows, cs] * convw_ref[j:j + 1, cs]
            u_ref[r * rows:(r + 1) * rows, cs] = _silu(acc)

    xe_ref[0:halo, :] = xe_ref[sb:sb + halo, :]

    ii = lax.broadcasted_iota(I32, (CHUNK, CHUNK), 0)
    jj = lax.broadcasted_iota(I32, (CHUNK, CHUNK), 1)
    eye = (ii == jj).astype(F32)

    def chunk_body(c, carry):
        r0 = pl.multiple_of(c * CHUNK, CHUNK)
        rs = pl.ds(r0, CHUNK)
        for h in range(N_HEADS_A):
            hs = slice(h * HEAD_DIM_A, (h + 1) * HEAD_DIM_A)
            q = u_ref[rs, h * HEAD_DIM_A:(h + 1) * HEAD_DIM_A]
            k = u_ref[rs, WIDTH_A + h * HEAD_DIM_A:WIDTH_A + (h + 1) * HEAD_DIM_A]
            v = u_ref[rs, 2 * WIDTH_A + h * HEAD_DIM_A:2 * WIDTH_A + (h + 1) * HEAD_DIM_A]
            qn = q * (lax.rsqrt(jnp.sum(q * q, axis=-1, keepdims=True) + EPS) * (HEAD_DIM_A ** -0.5))
            kn = k * lax.rsqrt(jnp.sum(k * k, axis=-1, keepdims=True) + EPS)

            neg_ea = -jnp.exp(alog_ref[0:1, h:h + 1])
            dtb = dtb_ref[0:1, h:h + 1]
            b_col = small_ref[0, rs, S_B + h:S_B + h + 1]
            a_col = small_ref[0, rs, S_A + h:S_A + h + 1]
            a_row = barow_ref[0, c, N_HEADS_A + h:N_HEADS_A + h + 1, :]
            beta = _sigmoid(b_col)
            g_col = neg_ea * _softplus(a_col + dtb)
            g_row = neg_ea * _softplus(a_row + dtb)
            gc_col = jnp.sum(jnp.where(jj <= ii, g_row, 0.0), axis=1, keepdims=True)
            gc_row = jnp.sum(jnp.where(ii <= jj, g_col, 0.0), axis=0, keepdims=True)
            decay = jnp.exp(jnp.where(ii >= jj, gc_col - gc_row, NEG_BIG))

            k_beta = kn * beta
            kk = _dot_nt(k_beta.astype(BF16), kn.astype(BF16))
            a_mat = -jnp.where(ii > jj, kk * decay, 0.0)
            t_mat = eye + a_mat
            p_mat = a_mat
            for _ in range(int(math.log2(CHUNK)) - 1):
                p_mat = _mm_f32(p_mat, p_mat)
                t_mat = t_mat + _mm_f32(t_mat, p_mat)
            egc = jnp.exp(gc_col)
            u_val = _mm_f32(t_mat, v * beta)
            w_dec = _mm_f32(t_mat, k_beta * egc)
            attn = _dot_nt(qn.astype(BF16), kn.astype(BF16)) * decay

            state = state_ref[h]
            v_new = u_val - _mm_bf16(w_dec, state)
            o = _mm_bf16(qn * egc, state) + _mm_bf16(attn, v_new)
            g_last = gc_col[CHUNK - 1:CHUNK, :]
            k_dec = kn * jnp.exp(g_last - gc_col)
            state_ref[h] = state * jnp.exp(g_last) + _mm_bf16(k_dec.T, v_new)

            on = o * lax.rsqrt(jnp.mean(o * o, axis=-1, keepdims=True) + EPS) * nw_ref[...]
            y_ref[0, rs, hs] = (on * _silu(z_ref[0, rs, hs])).astype(BF16)
        return carry

    lax.fori_loop(0, n_chunks, chunk_body, 0)


def _gdn(qkv_a, z_a, small, conv_w, a_log, dt_bias, norm_w, sb):
    bsz, s, _ = qkv_a.shape
    n_c = s // CHUNK
    ba = small[:, :, S_B:S_B + 2 * N_HEADS_A].reshape(bsz, n_c, CHUNK, 2 * N_HEADS_A)
    ba_row = jnp.swapaxes(ba, 2, 3)
    f = lambda b, i: (b, i, 0)
    const2 = lambda b, i: (0, 0)
    return pl.pallas_call(
        functools.partial(_gdn_kernel, sb=sb),
        grid=(bsz, s // sb),
        in_specs=[
            pl.BlockSpec((1, sb, 3 * WIDTH_A), f),
            pl.BlockSpec((1, sb, WIDTH_A), f),
            pl.BlockSpec((1, sb, LANES), f),
            pl.BlockSpec((1, sb // CHUNK, 2 * N_HEADS_A, CHUNK), lambda b, i: (b, i, 0, 0)),
            pl.BlockSpec((CONV_K, 3 * WIDTH_A), const2),
            pl.BlockSpec((1, N_HEADS_A), const2),
            pl.BlockSpec((1, N_HEADS_A), const2),
            pl.BlockSpec((1, HEAD_DIM_A), const2),
        ],
        out_specs=pl.BlockSpec((1, sb, WIDTH_A), f),
        out_shape=jax.ShapeDtypeStruct((bsz, s, WIDTH_A), BF16),
        scratch_shapes=[
            pltpu.VMEM((sb + SUBLANES, 3 * WIDTH_A), F32),
            pltpu.VMEM((sb, 3 * WIDTH_A), F32),
            pltpu.VMEM((N_HEADS_A, HEAD_DIM_A, HEAD_DIM_A), F32),
        ],
        compiler_params=_cparams(("parallel", "arbitrary")),
        name="gdn",
    )(qkv_a, z_a, small, ba_row, conv_w, a_log.reshape(1, -1), dt_bias.reshape(1, -1), norm_w.reshape(1, -1))


QB = 128
FAR_T = 512
FAR_G = FAR_T // LANES
NEAR_D = 9
NEAR_MIN = 5
INT_MIN = -2 ** 31


def _t5_bucket_np(rel):
    nb = REL_BUCKETS // 2
    max_exact = nb // 2
    side = np.where(rel > 0, nb, 0)
    n = np.abs(rel)
    nf = np.maximum(n, 1).astype(np.float32)
    large = max_exact + (np.log(nf / np.float32(max_exact)) / np.float32(math.log(REL_MAX_DIST / max_exact))
                         * np.float32(nb - max_exact)).astype(np.int32)
    large = np.minimum(large, nb - 1)
    return (side + np.where(n < max_exact, n, large)).astype(np.int32)


def _near_bucket_table():
    r = np.arange(QB)[:, None]
    c = np.arange(LANES)[None, :]
    return np.stack([_t5_bucket_np(c - r - LANES * d) for d in range(NEAR_D)])


FAR_BUCKET = int(_t5_bucket_np(np.array([-(NEAR_MIN * LANES + 1)]))[0])
assert all(int(b) == FAR_BUCKET for b in _t5_bucket_np(-np.arange((NEAR_MIN + 1) * LANES - (QB - 1), 1 << 20, 997)))


def _sortable_key(score):
    bits = pltpu.bitcast(score + 0.0, I32)
    return bits ^ ((bits >> 31) & 0x7FFFFFFF)


def _dsa_kernel(rb_ref, qb_ref, qi_ref, small_ref, kb_ref, vb_ref, kidx2_ref, tab_ref,
                y_ref, qis_ref, qs_ref, wb_ref, sdot_ref, keys_ref, nbias_ref, thr_ref, jlim_ref,
                m_ref, l_ref, acc_ref, *, seq, k_sel):
    i = pl.program_id(1)
    lane = lax.broadcasted_iota(I32, (QB, LANES), 1)
    row = lax.broadcasted_iota(I32, (QB, LANES), 0)
    even_f = (lane < HEAD_DIM_B).astype(F32)
    even_b = even_f.astype(BF16)
    odd_b = (1.0 - even_f).astype(BF16)

    @pl.when(i == 0)
    def _():
        nbias_ref[...] = jnp.zeros_like(nbias_ref)

        def d_body(d, c0):
            tab = tab_ref[d]

            def b_body(bk, c1):
                hit = tab == bk
                for h in range(N_HEADS_B):
                    nbias_ref[d * N_HEADS_B + h] = jnp.where(hit, rb_ref[bk, h], nbias_ref[d * N_HEADS_B + h])
                return c1

            return lax.fori_loop(0, REL_BUCKETS, b_body, c0)

        lax.fori_loop(0, NEAR_D, d_body, 0)

    for p in range(N_HEADS_B // 2):
        ps = slice(p * LANES, (p + 1) * LANES)
        qi_pair = qi_ref[0, :, ps]
        qis_ref[(2 * p) * QB:(2 * p + 1) * QB, :] = qi_pair * even_b
        qis_ref[(2 * p + 1) * QB:(2 * p + 2) * QB, :] = qi_pair * odd_b
        q_pair = qb_ref[0, :, ps]
        qs_ref[2 * p] = q_pair * even_b
        qs_ref[2 * p + 1] = q_pair * odd_b
    w_scale = IDX_HEADS ** -0.5 * IDX_DIM ** -0.5
    for h in range(IDX_HEADS):
        wb_ref[h] = jnp.broadcast_to(small_ref[0, :, S_WIDX + h:S_WIDX + h + 1] * w_scale, (QB, LANES))

    limit = i * QB + CHUNK + jnp.where(row >= CHUNK, CHUNK, 0)

    def score_body(t, c0):
        k0 = pl.multiple_of(t * FAR_T, FAR_T)
        sdot_ref[...] = _dot_nt(qis_ref[...], kidx2_ref[0, pl.ds(k0, FAR_T), :])
        for g in range(FAR_G):
            gs = slice(g * LANES, (g + 1) * LANES)
            acc = jnp.maximum(sdot_ref[0:QB, gs], 0.0) * wb_ref[0]
            for h in range(1, IDX_HEADS):
                acc = acc + jnp.maximum(sdot_ref[h * QB:(h + 1) * QB, gs], 0.0) * wb_ref[h]
            col = k0 + g * LANES + lane
            keys_ref[:, pl.ds(pl.multiple_of(k0 + g * LANES, LANES), LANES)] = jnp.where(
                col < limit, _sortable_key(acc), INT_MIN)
        return c0

    lax.fori_loop(0, i // FAR_G + 1, score_body, 0)

    n_groups = i + 1
    n_tiles = i // FAR_G + 1

    def count(pred):
        def t_body(t, acc):
            for g in range(FAR_G):
                c0 = pl.multiple_of(t * FAR_T + g * LANES, LANES)
                kt = keys_ref[:, pl.ds(c0, LANES)]
                acc = acc + jnp.where(pred(kt, c0 + lane), 1, 0)
            return acc

        acc = lax.fori_loop(0, n_tiles, t_body, jnp.zeros((QB, LANES), I32))
        return jnp.broadcast_to(jnp.sum(acc, axis=1, keepdims=True), (QB, LANES))

    thr_ref[...] = jnp.full((QB, LANES), INT_MIN, I32)
    jlim_ref[...] = jnp.full((QB, LANES), -1, I32)

    @pl.when(n_groups * QB > k_sel)
    def _():
        def bit_body(step, r):
            cand = r + lax.shift_left(jnp.int32(1), 31 - step)
            cnt = count(lambda kt, col: kt >= cand)
            return jnp.where(cnt >= k_sel, cand, r)

        r = lax.fori_loop(0, 32, bit_body, jnp.full((QB, LANES), INT_MIN, I32))
        thr_ref[...] = r
        need = k_sel - count(lambda kt, col: kt > r)
        excess = count(lambda kt, col: kt == r) - need
        jlim_ref[...] = jnp.where(r == INT_MIN, -1, seq)

        @pl.when(jnp.max(excess) > 0)
        def _():
            def j_body(step, jl):
                cand = jl + lax.shift_left(jnp.int32(1), (seq.bit_length() - 1) - step)
                cnt = count(lambda kt, col: (kt == r) & (col < cand))
                return jnp.where(cnt < need, cand, jl)

            jl = lax.fori_loop(0, seq.bit_length(), j_body, jnp.zeros((QB, LANES), I32))
            jlim_ref[...] = jnp.where(r == INT_MIN, -1, jl)

    m_ref[...] = jnp.full(m_ref.shape, NEG_BIG, F32)
    l_ref[...] = jnp.zeros_like(l_ref)
    acc_ref[...] = jnp.zeros_like(acc_ref)

    def sel_mask(k0, width):
        parts = []
        for g in range(width // LANES):
            kt = keys_ref[:, pl.ds(pl.multiple_of(k0 + g * LANES, LANES), LANES)]
            col = k0 + g * LANES + lane
            sel = (kt > thr_ref[...]) | ((kt == thr_ref[...]) & (col <= jlim_ref[...]))
            parts.append(jnp.where(sel, 0.0, -jnp.inf))
        return parts[0] if len(parts) == 1 else jnp.concatenate(parts, axis=1)

    def attend(h, k0, width, negm, bias):
        ps = slice((h // 2) * LANES, (h // 2 + 1) * LANES)
        ks = pl.ds(pl.multiple_of(k0, LANES), width)
        s = _dot_nt(qs_ref[h], kb_ref[0, ks, ps]) + bias + negm
        m_old = m_ref[h]
        m_new = jnp.maximum(m_old, jnp.max(s, axis=1, keepdims=True))
        alpha = jnp.exp(m_old - m_new)
        reps = width // LANES
        p = jnp.exp(s - (m_new if reps == 1 else jnp.concatenate([m_new] * reps, axis=1)))
        l_ref[h] = alpha * l_ref[h] + jnp.sum(p, axis=1, keepdims=True)
        acc_ref[h] = alpha * acc_ref[h] + jnp.dot(p.astype(BF16), vb_ref[0, ks, ps], preferred_element_type=F32)
        m_ref[h] = m_new

    near0 = (jnp.maximum(i - NEAR_MIN, 0) // FAR_G) * FAR_G

    def far_body(t, c0):
        k0 = pl.multiple_of(t * FAR_T, FAR_T)
        negm = sel_mask(k0, FAR_T)
        for h in range(N_HEADS_B):
            attend(h, k0, FAR_T, negm, rb_ref[FAR_BUCKET, h])
        return c0

    lax.fori_loop(0, near0 // FAR_G, far_body, 0)

    def near_body(t, c0):
        k0 = pl.multiple_of(t * FAR_T, FAR_T)
        negm = sel_mask(k0, FAR_T)
        dist = [jnp.maximum(i - (t * FAR_G + g), 0) for g in range(FAR_G)]
        for h in range(N_HEADS_B):
            bias = jnp.concatenate([nbias_ref[dist[g] * N_HEADS_B + h] for g in range(FAR_G)], axis=1)
            attend(h, k0, FAR_T, negm, bias)
        return c0

    lax.fori_loop(near0 // FAR_G, n_tiles, near_body, 0)

    for p in range(N_HEADS_B // 2):
        o_even = acc_ref[2 * p] / l_ref[2 * p]
        o_odd = acc_ref[2 * p + 1] / l_ref[2 * p + 1]
        y_ref[0, :, p * LANES:(p + 1) * LANES] = jnp.where(lane < HEAD_DIM_B, o_even, o_odd).astype(BF16)


def _dsa(qb, kb, vb, qi, small, rel_bias):
    bsz, s, _ = qb.shape
    assert s % FAR_T == 0
    k_sel = min(TOPK_KEYS_MAX, s // 4)
    kidx = small[:, :, S_KIDX:S_KIDX + IDX_DIM].astype(BF16)
    kidx2 = jnp.concatenate([kidx, kidx], axis=-1)
    tab = jnp.asarray(_near_bucket_table())
    blk = lambda b, i: (b, i, 0)
    full = lambda b, i: (b, 0, 0)
    one = pl.Buffered(1)
    return pl.pallas_call(
        functools.partial(_dsa_kernel, seq=s, k_sel=k_sel),
        grid=(bsz, s // QB),
        in_specs=[
            pl.BlockSpec(memory_space=pltpu.SMEM),
            pl.BlockSpec((1, QB, WIDTH_B), blk),
            pl.BlockSpec((1, QB, IDX_HEADS * IDX_DIM), blk),
            pl.BlockSpec((1, QB, LANES), blk),
            pl.BlockSpec((1, s, WIDTH_B), full, pipeline_mode=one),
            pl.BlockSpec((1, s, WIDTH_B), full, pipeline_mode=one),
            pl.BlockSpec((1, s, LANES), full, pipeline_mode=one),
            pl.BlockSpec((NEAR_D, QB, LANES), lambda b, i: (0, 0, 0), pipeline_mode=one),
        ],
        out_specs=pl.BlockSpec((1, QB, WIDTH_B), blk),
        out_shape=jax.ShapeDtypeStruct((bsz, s, WIDTH_B), BF16),
        scratch_shapes=[
            pltpu.VMEM((IDX_HEADS * QB, LANES), BF16),
            pltpu.VMEM((N_HEADS_B, QB, LANES), BF16),
            pltpu.VMEM((IDX_HEADS, QB, LANES), F32),
            pltpu.VMEM((IDX_HEADS * QB, FAR_T), F32),
            pltpu.VMEM((QB, s), I32),
            pltpu.VMEM((NEAR_D * N_HEADS_B, QB, LANES), F32),
            pltpu.VMEM((QB, LANES), I32),
            pltpu.VMEM((QB, LANES), I32),
            pltpu.VMEM((N_HEADS_B, QB, LANES), F32),
            pltpu.VMEM((N_HEADS_B, QB, LANES), F32),
            pltpu.VMEM((N_HEADS_B, QB, LANES), F32),
        ],
        compiler_params=_cparams(("parallel", "arbitrary")),
        name="dsa",
    )(rel_bias, qb, qi, small, kb, vb, kidx2, tab)


HALF_MASK = 0xFFFF0000


def _pack_halves(t):
    w = t.shape[1] // 2
    bits = pltpu.bitcast(t.astype(BF16).astype(F32), U32)
    return (bits[:, :w] >> 16) | (bits[:, w:] & jnp.uint32(HALF_MASK))


def _unpack_halves(p):
    lo = pltpu.bitcast(p << 16, F32)
    hi = pltpu.bitcast(p & jnp.uint32(HALF_MASK), F32)
    return jnp.concatenate([lo, hi], axis=1)


def _outproj_kernel(ya_ref, yb_ref, x_ref, mod_ref, wo_ref, nw_ref, rw_ref, rbias_ref,
                    xn_ref, hp_ref, ridx_ref, gate_ref):
    wa = ya_ref.shape[2]
    y = (jnp.dot(ya_ref[0], wo_ref[0:wa, :], preferred_element_type=F32)
         + jnp.dot(yb_ref[0], wo_ref[wa:, :], preferred_element_type=F32))
    xn = x_ref[0] + mod_ref[0, 2:3, :] * y
    xn_ref[0] = xn
    ms = jnp.mean(xn * xn, axis=-1, keepdims=True)
    h = xn * lax.rsqrt(ms + EPS) * nw_ref[...] * (1.0 + mod_ref[0, 4:5, :]) + mod_ref[0, 3:4, :]
    hp_ref[0] = _pack_halves(h)

    logits = jnp.dot(h, rw_ref[...], precision=HIGHEST, preferred_element_type=F32) + rbias_ref[...]
    lane = lax.broadcasted_iota(I32, logits.shape, 1)
    cur = logits
    vals, ridx = [], jnp.zeros(logits.shape, I32)
    for k in range(TOP_K):
        mx = jnp.max(cur, axis=1, keepdims=True)
        am = jnp.min(jnp.where(cur == mx, lane, LANES), axis=1, keepdims=True)
        cur = jnp.where(lane == am, -jnp.inf, cur)
        vals.append(mx)
        ridx = jnp.where(lane == k, am, ridx)
    ex = [jnp.exp(v - vals[0]) for v in vals]
    inv = 1.0 / (ex[0] + ex[1] + ex[2] + ex[3])
    gate = jnp.zeros(logits.shape, F32)
    for k in range(TOP_K):
        gate = jnp.where(lane == k, ex[k] * inv, gate)
    ridx_ref[0] = ridx
    gate_ref[0] = gate


def _outproj(y_a, y_b, x, mod_l, w_out_bf, norm_w, router_w, router_b, tm):
    bsz, s, d = x.shape
    n_e = router_w.shape[1]
    rw = jnp.zeros((d, LANES), F32).at[:, :n_e].set(router_w)
    rbias = jnp.full((1, LANES), NEG_BIG, F32).at[0, :n_e].set(router_b)
    blk = lambda b, i: (b, i, 0)
    const2 = lambda b, i: (0, 0)
    return pl.pallas_call(
        _outproj_kernel,
        grid=(bsz, s // tm),
        in_specs=[
            pl.BlockSpec((1, tm, y_a.shape[2]), blk),
            pl.BlockSpec((1, tm, y_b.shape[2]), blk),
            pl.BlockSpec((1, tm, d), blk),
            pl.BlockSpec((1, 6, d), lambda b, i: (b, 0, 0)),
            pl.BlockSpec((d, d), const2),
            pl.BlockSpec((1, d), const2),
            pl.BlockSpec((d, LANES), const2),
            pl.BlockSpec((1, LANES), const2),
        ],
        out_specs=[pl.BlockSpec((1, tm, d), blk), pl.BlockSpec((1, tm, d // 2), blk),
                   pl.BlockSpec((1, tm, LANES), blk), pl.BlockSpec((1, tm, LANES), blk)],
        out_shape=[jax.ShapeDtypeStruct((bsz, s, d), F32), jax.ShapeDtypeStruct((bsz, s, d // 2), U32),
                   jax.ShapeDtypeStruct((bsz, s, LANES), I32), jax.ShapeDtypeStruct((bsz, s, LANES), F32)],
        compiler_params=_cparams(("parallel", "parallel")),
        name="outproj_router",
    )(y_a, y_b, x, mod_l, w_out_bf, norm_w.reshape(1, d), rw, rbias)


MOE_TB = 2048
MOE_RB = 512
MOE_M = 128


def _moe_kernel(cnt_ref, off_ref, list_ref, hp_ref, w1_ref, b1_ref, w2_ref, b2_ref, gate_ref, x_ref, g2_ref,
                o_ref, slots_ref, xg_ref, yb_ref, *, tb, rb):
    sb = pl.program_id(0)
    e = pl.program_id(1)
    dff = w2_ref.shape[1]
    last = TOP_K * tb - 1

    @pl.when(e < N_EXPERTS)
    def _():
        cnt = cnt_ref[sb * N_EXPERTS + e]
        off = off_ref[sb * N_EXPERTS + e]

        def chunk(c, carry):
            base = off + c * MOE_M

            def gather(r, c1):
                code = list_ref[0, 0, jnp.minimum(base + r, last)]
                xg_ref[pl.ds(r, 1), :] = hp_ref[pl.ds(code >> 2, 1), :]
                return c1

            lax.fori_loop(0, MOE_M, gather, 0, unroll=8)
            xb = _unpack_halves(xg_ref[...]).astype(BF16)
            u = jnp.dot(xb, w1_ref[0], preferred_element_type=F32) + b1_ref[0]
            glu = jnp.minimum(u[:, :dff], SWIGLU_LIMIT)
            lin = jnp.clip(u[:, dff:], -SWIGLU_LIMIT, SWIGLU_LIMIT)
            act = glu * _sigmoid(SWIGLU_ALPHA * glu) * (lin + 1.0)
            y = jnp.dot(act.astype(BF16), w2_ref[0], preferred_element_type=F32) + b2_ref[0]
            yb_ref[...] = _pack_halves(y)

            def scatter(r, c1):
                code = list_ref[0, 0, base + r]
                slots_ref[code & (TOP_K - 1), pl.ds(code >> 2, 1), :] = yb_ref[pl.ds(r, 1), :]
                return c1

            lax.fori_loop(0, jnp.minimum(cnt - c * MOE_M, MOE_M), scatter, 0)
            return carry

        lax.fori_loop(0, (cnt + MOE_M - 1) // MOE_M, chunk, 0)

    @pl.when(e >= N_EXPERTS)
    def _():
        r0 = pl.multiple_of((e - N_EXPERTS) * rb, rb)
        acc = gate_ref[:, 0:1] * _unpack_halves(slots_ref[0, pl.ds(r0, rb), :])
        for k in range(1, TOP_K):
            acc = acc + gate_ref[:, k:k + 1] * _unpack_halves(slots_ref[k, pl.ds(r0, rb), :])
        o_ref[...] = x_ref[...] + g2_ref[0] * acc


def _moe(xn, hp, ridx, gate, g2, w1p, b1p, w2b, b2, tb, rb):
    bsz, s, d = xn.shape
    t = bsz * s
    n_super = t // tb
    n_piece = tb // rb
    dff = w2b.shape[1]
    flat_e = ridx[:, :, :TOP_K].reshape(n_super, tb * TOP_K)
    order = jnp.argsort(flat_e, axis=1, stable=True).astype(I32)
    counts = jnp.sum(flat_e[:, :, None] == jnp.arange(N_EXPERTS, dtype=I32)[None, None, :], axis=1).astype(I32)
    offs = (jnp.cumsum(counts, axis=1) - counts).astype(I32)

    piece = lambda sb, e, *_: (sb * n_piece + jnp.maximum(e - N_EXPERTS, 0), 0)
    wmap = lambda sb, e, *_: (jnp.minimum(e, N_EXPERTS - 1), 0, 0)
    grid_spec = pltpu.PrefetchScalarGridSpec(
        num_scalar_prefetch=2,
        grid=(n_super, N_EXPERTS + n_piece),
        in_specs=[
            pl.BlockSpec((1, 1, tb * TOP_K), lambda sb, e, *_: (sb, 0, 0), memory_space=pltpu.SMEM),
            pl.BlockSpec((tb, d // 2), lambda sb, e, *_: (sb, 0), pipeline_mode=pl.Buffered(1)),
            pl.BlockSpec((1, d, 2 * dff), wmap),
            pl.BlockSpec((1, 1, 2 * dff), wmap),
            pl.BlockSpec((1, dff, d), wmap),
            pl.BlockSpec((1, 1, d), wmap),
            pl.BlockSpec((rb, LANES), piece),
            pl.BlockSpec((rb, d), piece),
            pl.BlockSpec((1, 1, d), lambda sb, e, *_: ((sb * tb) // s, 0, 0)),
        ],
        out_specs=pl.BlockSpec((rb, d), piece),
        scratch_shapes=[
            pltpu.VMEM((TOP_K, tb, d // 2), U32),
            pltpu.VMEM((MOE_M, d // 2), U32),
            pltpu.VMEM((MOE_M, d // 2), U32),
        ],
    )
    out = pl.pallas_call(
        functools.partial(_moe_kernel, tb=tb, rb=rb),
        grid_spec=grid_spec,
        out_shape=jax.ShapeDtypeStruct((t, d), F32),
        compiler_params=_cparams(("arbitrary", "arbitrary")),
        name="moe",
    )(counts.reshape(-1), offs.reshape(-1), order.reshape(n_super, 1, tb * TOP_K), hp.reshape(t, d // 2),
      w1p, b1p, w2b, b2, gate.reshape(t, LANES), xn.reshape(t, d), g2.reshape(bsz, 1, d))
    return out.reshape(bsz, s, d)


MXU_COLS = 256


def _deinterleave_kernel(w_ref, perm_ref, o_ref):
    half = w_ref.shape[2] // 2
    hw = MXU_COLS // 2
    for b in range(w_ref.shape[2] // MXU_COLS):
        blk = w_ref[0, :, b * MXU_COLS:(b + 1) * MXU_COLS].astype(BF16)
        y = jnp.dot(blk, perm_ref[...], preferred_element_type=F32).astype(BF16)
        o_ref[0, :, b * hw:(b + 1) * hw] = y[:, :hw]
        o_ref[0, :, half + b * hw:half + (b + 1) * hw] = y[:, hw:]


def _deinterleave_cast(w1_l, rows=512):
    n_e, d, two_f = w1_l.shape
    src = np.concatenate([np.arange(0, MXU_COLS, 2), np.arange(1, MXU_COLS, 2)])
    perm = np.zeros((MXU_COLS, MXU_COLS), np.float32)
    perm[src, np.arange(MXU_COLS)] = 1.0
    return pl.pallas_call(
        _deinterleave_kernel,
        grid=(n_e, d // rows),
        in_specs=[pl.BlockSpec((1, rows, two_f), lambda e, r: (e, r, 0)),
                  pl.BlockSpec((MXU_COLS, MXU_COLS), lambda e, r: (0, 0))],
        out_specs=pl.BlockSpec((1, rows, two_f), lambda e, r: (e, r, 0)),
        out_shape=jax.ShapeDtypeStruct((n_e, d, two_f), BF16),
        compiler_params=_cparams(("parallel", "parallel")),
        name="w1_deinterleave",
    )(w1_l, jnp.asarray(perm, BF16))


def _prep_expert_weights(w1_l, b1_l, w2_l, b2_l):
    n_e, d, two_f = w1_l.shape
    b1p = jnp.concatenate([b1_l[:, 0::2], b1_l[:, 1::2]], axis=1).reshape(n_e, 1, two_f)
    return _deinterleave_cast(w1_l), b1p, w2_l.astype(BF16), b2_l.reshape(n_e, 1, -1)


def kernel(x, c, rel_bias, mod_w, mod_b, norm_mix_w, norm_ffn_w, w_in, conv_w, a_log, dt_bias, gdn_norm_w,
           q_norm_w, k_norm_w, w_out, router_w, router_b, w1, b1, w2, b2):
    depth = mod_w.shape[0]
    bsz, s, d = x.shape
    mod = _modulation(c, mod_w, mod_b)
    tm = min(512, s)
    tb = min(MOE_TB, bsz * s)
    rb = min(MOE_RB, tb)
    for l in range(depth):
        qkv_a, z_a, qb, kb, vb, qi, small = _inproj(
            x, mod[l], norm_mix_w[l], _permute_w_in(w_in[l]), q_norm_w[l], k_norm_w[l], tm)
        y_a = _gdn(qkv_a, z_a, small, conv_w[l], a_log[l], dt_bias[l], gdn_norm_w[l], sb=tm)
        y_b = _dsa(qb, kb, vb, qi, small, rel_bias)
        xn, hp, ridx, gate = _outproj(y_a, y_b, x, mod[l], w_out[l].astype(BF16), norm_ffn_w[l],
                                      router_w[l], router_b[l], tm)
        x = _moe(xn, hp, ridx, gate, mod[l][:, 5], *_prep_expert_weights(w1[l], b1[l], w2[l], b2[l]), tb, rb)
    return x
```

```python
import functools
import math

import jax
import jax.numpy as jnp
import numpy as np
from jax import lax
from jax.experimental import pallas as pl
from jax.experimental.pallas import tpu as pltpu

F32 = jnp.float32
BF16 = jnp.bfloat16
I32 = jnp.int32
U32 = jnp.uint32
HIGHEST = lax.Precision.HIGHEST

LANES = 128
SUBLANES = 8
VMEM_LIMIT_BYTES = 56 * 1024 * 1024

CHUNK = 64
HEAD_DIM_A = 128
N_HEADS_A = 4
WIDTH_A = N_HEADS_A * HEAD_DIM_A
CONV_K = 4
HEAD_DIM_B = 64
N_HEADS_B = 8
WIDTH_B = N_HEADS_B * HEAD_DIM_B
IDX_HEADS = 8
IDX_DIM = 64
TOPK_KEYS_MAX = 256
REL_BUCKETS = 32
REL_MAX_DIST = 1024
N_EXPERTS = 32
TOP_K = 4
SWIGLU_ALPHA = 1.702
SWIGLU_LIMIT = 7.0
EPS = 1e-6
NEG_BIG = -1e30

C_QKVA = 0
C_Z = C_QKVA + 3 * WIDTH_A
C_QB = C_Z + WIDTH_A
C_KB = C_QB + WIDTH_B
C_VB = C_KB + WIDTH_B
C_QI = C_VB + WIDTH_B
C_SMALL = C_QI + IDX_HEADS * IDX_DIM
D_IN_PAD = C_SMALL + LANES
S_KIDX = 0
S_B = IDX_DIM
S_A = S_B + N_HEADS_A
S_WIDX = S_A + N_HEADS_A


def _cparams(sem):
    return pltpu.CompilerParams(dimension_semantics=sem, vmem_limit_bytes=VMEM_LIMIT_BYTES)


def _silu(x):
    return x * (1.0 / (1.0 + jnp.exp(-x)))


def _sigmoid(x):
    return 1.0 / (1.0 + jnp.exp(-x))


def _softplus(x):
    return jnp.maximum(x, 0.0) + jnp.log(1.0 + jnp.exp(-jnp.abs(x)))


def _mod_kernel(c_ref, w_ref, b_ref, o_ref):
    a = _silu(c_ref[...])
    o_ref[0] = jnp.dot(a, w_ref[0], precision=HIGHEST, preferred_element_type=F32) + b_ref[0]


def _modulation(c, mod_w, mod_b):
    depth, d, n = mod_w.shape
    bsz = c.shape[0]
    rows = -(-bsz // SUBLANES) * SUBLANES
    c_pad = jnp.zeros((rows, d), F32).at[:bsz].set(c)
    tn = 1536
    out = pl.pallas_call(
        _mod_kernel,
        grid=(depth, n // tn),
        in_specs=[
            pl.BlockSpec((rows, d), lambda l, j: (0, 0)),
            pl.BlockSpec((1, d, tn), lambda l, j: (l, 0, j)),
            pl.BlockSpec((1, 1, tn), lambda l, j: (l, 0, j)),
        ],
        out_specs=pl.BlockSpec((1, rows, tn), lambda l, j: (l, 0, j)),
        out_shape=jax.ShapeDtypeStruct((depth, rows, n), F32),
        compiler_params=_cparams(("arbitrary", "arbitrary")),
        name="adaln_mod",
    )(c_pad, mod_w, mod_b.reshape(depth, 1, n))
    return out[:, :bsz].reshape(depth, bsz, 6, d)


def _head_rms(t, group_ref, wn, inv_dim):
    t2 = t * t
    hi = t2.astype(BF16)
    lo = (t2 - hi.astype(F32)).astype(BF16)
    ss = (jnp.dot(hi, group_ref[...], preferred_element_type=F32)
          + jnp.dot(lo, group_ref[...], preferred_element_type=F32))
    return t * lax.rsqrt(ss * inv_dim + EPS) * wn


def _inproj_kernel(x_ref, mod_ref, nw_ref, w_ref, group_ref, qn_ref, kn_ref,
                   qkva_ref, z_ref, qb_ref, kb_ref, vb_ref, qi_ref, small_ref):
    x = x_ref[0]
    ms = jnp.mean(x * x, axis=-1, keepdims=True)
    y = x * lax.rsqrt(ms + EPS) * nw_ref[...]
    h = y * (1.0 + mod_ref[0, 1:2, :]) + mod_ref[0, 0:1, :]
    hb = h.astype(BF16)

    def mm(lo, width):
        return jnp.dot(hb, w_ref[:, lo:lo + width], preferred_element_type=F32)

    qkva_ref[0] = mm(C_QKVA, 3 * WIDTH_A)
    z_ref[0] = mm(C_Z, WIDTH_A)
    q = _head_rms(mm(C_QB, WIDTH_B), group_ref, qn_ref[...], 1.0 / HEAD_DIM_B)
    qb_ref[0] = (q * (HEAD_DIM_B ** -0.5 * LOG2E)).astype(BF16)
    k = _head_rms(mm(C_KB, WIDTH_B), group_ref, kn_ref[...], 1.0 / HEAD_DIM_B)
    kb_ref[0] = k.astype(BF16)
    vb_ref[0] = mm(C_VB, WIDTH_B).astype(BF16)
    qi_ref[0] = mm(C_QI, IDX_HEADS * IDX_DIM).astype(BF16)
    small_ref[0] = mm(C_SMALL, LANES)


def _permute_w_in(w_in_l):
    d = w_in_l.shape[0]
    o = 0
    qkva = w_in_l[:, o:o + 3 * WIDTH_A]; o += 3 * WIDTH_A
    z = w_in_l[:, o:o + WIDTH_A]; o += WIDTH_A
    b = w_in_l[:, o:o + N_HEADS_A]; o += N_HEADS_A
    a = w_in_l[:, o:o + N_HEADS_A]; o += N_HEADS_A
    qkvb = w_in_l[:, o:o + 3 * WIDTH_B]; o += 3 * WIDTH_B
    qi = w_in_l[:, o:o + IDX_HEADS * IDX_DIM]; o += IDX_HEADS * IDX_DIM
    ki = w_in_l[:, o:o + IDX_DIM]; o += IDX_DIM
    wi = w_in_l[:, o:o + IDX_HEADS]; o += IDX_HEADS
    pad = jnp.zeros((d, LANES - IDX_DIM - 2 * N_HEADS_A - IDX_HEADS), w_in_l.dtype)
    return jnp.concatenate([qkva, z, qkvb, qi, ki, b, a, wi, pad], axis=1).astype(BF16)


def _group_ones(width, group):
    g = np.arange(width) // group
    return jnp.asarray((g[:, None] == g[None, :]).astype(np.float32), dtype=BF16)


def _inproj(x, mod_l, norm_w, w_perm, q_norm_w, k_norm_w, tm):
    bsz, s, d = x.shape
    f = lambda b, i: (b, i, 0)
    const2 = lambda b, i: (0, 0)
    outs = [
        (3 * WIDTH_A, F32), (WIDTH_A, F32), (WIDTH_B, BF16), (WIDTH_B, BF16), (WIDTH_B, BF16),
        (IDX_HEADS * IDX_DIM, BF16), (LANES, F32),
    ]
    return pl.pallas_call(
        _inproj_kernel,
        grid=(bsz, s // tm),
        in_specs=[
            pl.BlockSpec((1, tm, d), f),
            pl.BlockSpec((1, 6, d), lambda b, i: (b, 0, 0)),
            pl.BlockSpec((1, d), const2),
            pl.BlockSpec((d, D_IN_PAD), const2),
            pl.BlockSpec((WIDTH_B, WIDTH_B), const2),
            pl.BlockSpec((1, WIDTH_B), const2),
            pl.BlockSpec((1, WIDTH_B), const2),
        ],
        out_specs=[pl.BlockSpec((1, tm, w), f) for w, _ in outs],
        out_shape=[jax.ShapeDtypeStruct((bsz, s, w), dt) for w, dt in outs],
        compiler_params=_cparams(("parallel", "parallel")),
        name="inproj",
    )(x, mod_l, norm_w.reshape(1, d), w_perm, _group_ones(WIDTH_B, HEAD_DIM_B),
      jnp.tile(q_norm_w, N_HEADS_B).reshape(1, WIDTH_B), jnp.tile(k_norm_w, N_HEADS_B).reshape(1, WIDTH_B))


def _dot_nt(a, b, precision=None):
    return lax.dot_general(a, b, (((1,), (1,)), ((), ())), precision=precision, preferred_element_type=F32)


def _mm_bf16(a, b):
    return jnp.dot(a.astype(BF16), b.astype(BF16), preferred_element_type=F32)


def _mm_f32(a, b):
    return jnp.dot(a, b, precision=HIGHEST, preferred_element_type=F32)


def _gdn_kernel(qkv_ref, z_ref, small_ref, barow_ref, convw_ref, alog_ref, dtb_ref, nw_ref,
                y_ref, xe_ref, u_ref, state_ref, *, sb):
    n_chunks = sb // CHUNK
    halo = SUBLANES

    @pl.when(pl.program_id(1) == 0)
    def _():
        xe_ref[0:halo, :] = jnp.zeros((halo, 3 * WIDTH_A), F32)
        state_ref[...] = jnp.zeros_like(state_ref)

    xe_ref[halo:halo + sb, :] = qkv_ref[0]

    rows = 128
    for g in range(3 * WIDTH_A // LANES):
        cs = slice(g * LANES, (g + 1) * LANES)
        for r in range(sb // rows):
            base = halo - (CONV_K - 1) + r * rows
            acc = xe_ref[base:base + rows, cs] * convw_ref[0:1, cs]
            for j in range(1, CONV_K):
                acc = acc + xe_ref[base + j:base + j + rows, cs] * convw_ref[j:j + 1, cs]
            u_ref[r * rows:(r + 1) * rows, cs] = _silu(acc)

    xe_ref[0:halo, :] = xe_ref[sb:sb + halo, :]

    ii = lax.broadcasted_iota(I32, (CHUNK, CHUNK), 0)
    jj = lax.broadcasted_iota(I32, (CHUNK, CHUNK), 1)
    eye = (ii == jj).astype(F32)

    def chunk_body(c, carry):
        r0 = pl.multiple_of(c * CHUNK, CHUNK)
        rs = pl.ds(r0, CHUNK)
        for h in range(N_HEADS_A):
            hs = slice(h * HEAD_DIM_A, (h + 1) * HEAD_DIM_A)
            q = u_ref[rs, h * HEAD_DIM_A:(h + 1) * HEAD_DIM_A]
            k = u_ref[rs, WIDTH_A + h * HEAD_DIM_A:WIDTH_A + (h + 1) * HEAD_DIM_A]
            v = u_ref[rs, 2 * WIDTH_A + h * HEAD_DIM_A:2 * WIDTH_A + (h + 1) * HEAD_DIM_A]
            qn = q * (lax.rsqrt(jnp.sum(q * q, axis=-1, keepdims=True) + EPS) * (HEAD_DIM_A ** -0.5))
            kn = k * lax.rsqrt(jnp.sum(k * k, axis=-1, keepdims=True) + EPS)

            neg_ea = -jnp.exp(alog_ref[0:1, h:h + 1])
            dtb = dtb_ref[0:1, h:h + 1]
            b_col = small_ref[0, rs, S_B + h:S_B + h + 1]
            a_col = small_ref[0, rs, S_A + h:S_A + h + 1]
            a_row = barow_ref[0, c, N_HEADS_A + h:N_HEADS_A + h + 1, :]
            beta = _sigmoid(b_col)
            g_col = neg_ea * _softplus(a_col + dtb)
            g_row = neg_ea * _softplus(a_row + dtb)
            gc_col = jnp.sum(jnp.where(jj <= ii, g_row, 0.0), axis=1, keepdims=True)
            gc_row = jnp.sum(jnp.where(ii <= jj, g_col, 0.0), axis=0, keepdims=True)
            decay = jnp.exp(jnp.where(ii >= jj, gc_col - gc_row, NEG_BIG))

            k_beta = kn * beta
            kk = _dot_nt(k_beta.astype(BF16), kn.astype(BF16))
            a_mat = -jnp.where(ii > jj, kk * decay, 0.0)
            t_mat = eye + a_mat
            p_mat = a_mat
            for _ in range(int(math.log2(CHUNK)) - 1):
                p_mat = _mm_f32(p_mat, p_mat)
                t_mat = t_mat + _mm_f32(t_mat, p_mat)
            egc = jnp.exp(gc_col)
            u_val = _mm_f32(t_mat, v * beta)
            w_dec = _mm_f32(t_mat, k_beta * egc)
            attn = _dot_nt(qn.astype(BF16), kn.astype(BF16)) * decay

            state = state_ref[h]
            v_new = u_val - _mm_bf16(w_dec, state)
            o = _mm_bf16(qn * egc, state) + _mm_bf16(attn, v_new)
            g_last = gc_col[CHUNK - 1:CHUNK, :]
            k_dec = kn * jnp.exp(g_last - gc_col)
            state_ref[h] = state * jnp.exp(g_last) + _mm_bf16(k_dec.T, v_new)

            on = o * lax.rsqrt(jnp.mean(o * o, axis=-1, keepdims=True) + EPS) * nw_ref[...]
            y_ref[0, rs, hs] = (on * _silu(z_ref[0, rs, hs])).astype(BF16)
        return carry

    lax.fori_loop(0, n_chunks, chunk_body, 0)


def _gdn(qkv_a, z_a, small, conv_w, a_log, dt_bias, norm_w, sb):
    bsz, s, _ = qkv_a.shape
    n_c = s // CHUNK
    ba = small[:, :, S_B:S_B + 2 * N_HEADS_A].reshape(bsz, n_c, CHUNK, 2 * N_HEADS_A)
    ba_row = jnp.swapaxes(ba, 2, 3)
    f = lambda b, i: (b, i, 0)
    const2 = lambda b, i: (0, 0)
    return pl.pallas_call(
        functools.partial(_gdn_kernel, sb=sb),
        grid=(bsz, s // sb),
        in_specs=[
            pl.BlockSpec((1, sb, 3 * WIDTH_A), f),
            pl.BlockSpec((1, sb, WIDTH_A), f),
            pl.BlockSpec((1, sb, LANES), f),
            pl.BlockSpec((1, sb // CHUNK, 2 * N_HEADS_A, CHUNK), lambda b, i: (b, i, 0, 0)),
            pl.BlockSpec((CONV_K, 3 * WIDTH_A), const2),
            pl.BlockSpec((1, N_HEADS_A), const2),
            pl.BlockSpec((1, N_HEADS_A), const2),
            pl.BlockSpec((1, HEAD_DIM_A), const2),
        ],
        out_specs=pl.BlockSpec((1, sb, WIDTH_A), f),
        out_shape=jax.ShapeDtypeStruct((bsz, s, WIDTH_A), BF16),
        scratch_shapes=[
            pltpu.VMEM((sb + SUBLANES, 3 * WIDTH_A), F32),
            pltpu.VMEM((sb, 3 * WIDTH_A), F32),
            pltpu.VMEM((N_HEADS_A, HEAD_DIM_A, HEAD_DIM_A), F32),
        ],
        compiler_params=_cparams(("parallel", "arbitrary")),
        name="gdn",
    )(qkv_a, z_a, small, ba_row, conv_w, a_log.reshape(1, -1), dt_bias.reshape(1, -1), norm_w.reshape(1, -1))


QB = 128
FAR_T = 512
FAR_G = FAR_T // LANES
ATT_W = 256
LOG2E = 1.4426950408889634
NEAR_D = 9
NEAR_MIN = 5
INT_MIN = -2 ** 31


def _t5_bucket_np(rel):
    nb = REL_BUCKETS // 2
    max_exact = nb // 2
    side = np.where(rel > 0, nb, 0)
    n = np.abs(rel)
    nf = np.maximum(n, 1).astype(np.float32)
    large = max_exact + (np.log(nf / np.float32(max_exact)) / np.float32(math.log(REL_MAX_DIST / max_exact))
                         * np.float32(nb - max_exact)).astype(np.int32)
    large = np.minimum(large, nb - 1)
    return (side + np.where(n < max_exact, n, large)).astype(np.int32)


def _near_bucket_table():
    r = np.arange(QB)[:, None]
    c = np.arange(LANES)[None, :]
    return np.stack([_t5_bucket_np(c - r - LANES * d) for d in range(NEAR_D)])


FAR_BUCKET = int(_t5_bucket_np(np.array([-(NEAR_MIN * LANES + 1)]))[0])
assert all(int(b) == FAR_BUCKET for b in _t5_bucket_np(-np.arange((NEAR_MIN + 1) * LANES - (QB - 1), 1 << 20, 997)))


def _sortable_key(score):
    bits = pltpu.bitcast(score + 0.0, I32)
    return bits ^ ((bits >> 31) & 0x7FFFFFFF)


def _dsa_kernel(rb_ref, qb_ref, qi_ref, small_ref, kb_ref, va_ref, kidx2_ref, tab_ref,
                y_ref, qis_ref, qs_ref, wb_ref, sdot_ref, keys_ref, nbias_ref, thr_ref, jlim_ref,
                m_ref, acc_ref, s_ref, p_ref, *, seq, k_sel):
    i = pl.program_id(1)
    lane = lax.broadcasted_iota(I32, (QB, LANES), 1)
    row = lax.broadcasted_iota(I32, (QB, LANES), 0)
    even_f = (lane < HEAD_DIM_B).astype(F32)
    even_b = even_f.astype(BF16)
    odd_b = (1.0 - even_f).astype(BF16)

    @pl.when(i == 0)
    def _():
        nbias_ref[...] = jnp.zeros_like(nbias_ref)

        def d_body(d, c0):
            tab = tab_ref[d]

            def b_body(bk, c1):
                hit = tab == bk
                for h in range(N_HEADS_B):
                    nbias_ref[d * N_HEADS_B + h] = jnp.where(hit, rb_ref[bk, h] * LOG2E,
                                                             nbias_ref[d * N_HEADS_B + h])
                return c1

            return lax.fori_loop(0, REL_BUCKETS, b_body, c0)

        lax.fori_loop(0, NEAR_D, d_body, 0)

    for p in range(N_HEADS_B // 2):
        ps = slice(p * LANES, (p + 1) * LANES)
        qi_pair = qi_ref[0, :, ps]
        qis_ref[(2 * p) * QB:(2 * p + 1) * QB, :] = qi_pair * even_b
        qis_ref[(2 * p + 1) * QB:(2 * p + 2) * QB, :] = qi_pair * odd_b
        q_pair = qb_ref[0, :, ps]
        qs_ref[(2 * p) * QB:(2 * p + 1) * QB, :] = q_pair * even_b
        qs_ref[(2 * p + 1) * QB:(2 * p + 2) * QB, :] = q_pair * odd_b
    w_scale = IDX_HEADS ** -0.5 * IDX_DIM ** -0.5
    for h in range(IDX_HEADS):
        wb_ref[h] = jnp.broadcast_to(small_ref[0, :, S_WIDX + h:S_WIDX + h + 1] * w_scale, (QB, LANES))

    limit = i * QB + CHUNK + jnp.where(row >= CHUNK, CHUNK, 0)

    def score_body(t, c0):
        k0 = pl.multiple_of(t * FAR_T, FAR_T)
        sdot_ref[...] = _dot_nt(qis_ref[...], kidx2_ref[0, pl.ds(k0, FAR_T), :])
        for g in range(FAR_G):
            gs = slice(g * LANES, (g + 1) * LANES)
            acc = jnp.maximum(sdot_ref[0:QB, gs], 0.0) * wb_ref[0]
            for h in range(1, IDX_HEADS):
                acc = acc + jnp.maximum(sdot_ref[h * QB:(h + 1) * QB, gs], 0.0) * wb_ref[h]
            col = k0 + g * LANES + lane
            keys_ref[:, pl.ds(pl.multiple_of(k0 + g * LANES, LANES), LANES)] = jnp.where(
                col < limit, _sortable_key(acc), INT_MIN)
        return c0

    lax.fori_loop(0, i // FAR_G + 1, score_body, 0)

    n_groups = i + 1
    n_tiles = i // FAR_G + 1

    def count(pred):
        def t_body(t, acc):
            for g in range(FAR_G):
                c0 = pl.multiple_of(t * FAR_T + g * LANES, LANES)
                kt = keys_ref[:, pl.ds(c0, LANES)]
                acc = acc + jnp.where(pred(kt, c0 + lane), 1, 0)
            return acc

        acc = lax.fori_loop(0, n_tiles, t_body, jnp.zeros((QB, LANES), I32))
        return jnp.broadcast_to(jnp.sum(acc, axis=1, keepdims=True), (QB, LANES))

    thr_ref[...] = jnp.full((QB, LANES), INT_MIN, I32)
    jlim_ref[...] = jnp.full((QB, LANES), -1, I32)

    @pl.when(n_groups * QB > k_sel)
    def _():
        def bit_body(step, r):
            cand = r + lax.shift_left(jnp.int32(1), 31 - step)
            cnt = count(lambda kt, col: kt >= cand)
            return jnp.where(cnt >= k_sel, cand, r)

        r = lax.fori_loop(0, 32, bit_body, jnp.full((QB, LANES), INT_MIN, I32))
        thr_ref[...] = r
        need = k_sel - count(lambda kt, col: kt > r)
        excess = count(lambda kt, col: kt == r) - need
        jlim_ref[...] = jnp.where(r == INT_MIN, -1, seq)

        @pl.when(jnp.max(excess) > 0)
        def _():
            def j_body(step, jl):
                cand = jl + lax.shift_left(jnp.int32(1), (seq.bit_length() - 1) - step)
                cnt = count(lambda kt, col: (kt == r) & (col < cand))
                return jnp.where(cnt < need, cand, jl)

            jl = lax.fori_loop(0, seq.bit_length(), j_body, jnp.zeros((QB, LANES), I32))
            jlim_ref[...] = jnp.where(r == INT_MIN, -1, jl)

    m_ref[...] = jnp.full(m_ref.shape, NEG_BIG, F32)
    acc_ref[...] = jnp.zeros_like(acc_ref)

    def sel_mask(k0, width):
        parts = []
        for g in range(width // LANES):
            kt = keys_ref[:, pl.ds(pl.multiple_of(k0 + g * LANES, LANES), LANES)]
            col = k0 + g * LANES + lane
            sel = (kt > thr_ref[...]) | ((kt == thr_ref[...]) & (col <= jlim_ref[...]))
            parts.append(jnp.where(sel, 0.0, -jnp.inf))
        return parts[0] if len(parts) == 1 else jnp.concatenate(parts, axis=1)

    n_part = FAR_T // ATT_W
    part_g = ATT_W // LANES

    def logits_phase(k0):
        negm = sel_mask(k0, FAR_T)
        ks = pl.ds(k0, FAR_T)
        for p in range(N_HEADS_B // 2):
            s = _dot_nt(qs_ref[2 * p * QB:(2 * p + 2) * QB, :], kb_ref[0, ks, p * LANES:(p + 1) * LANES])
            s_ref[2 * p] = s[:QB] + negm
            s_ref[2 * p + 1] = s[QB:] + negm

    def softmax_phase(h, bias, bias_scalar):
        def part(c):
            sc = s_ref[h, :, c * ATT_W:(c + 1) * ATT_W]
            return sc if bias is None else sc + bias(c)

        rmax = None
        for c in range(n_part):
            r = jnp.max(part(c), axis=1, keepdims=True)
            rmax = r if rmax is None else jnp.maximum(rmax, r)
        m_old = m_ref[h]
        m_new = jnp.maximum(m_old, rmax + bias_scalar)
        shift = jnp.concatenate([m_new - bias_scalar] * part_g, axis=1)
        for c in range(n_part):
            p_ref[h, :, c * ATT_W:(c + 1) * ATT_W] = jnp.exp2(part(c) - shift).astype(BF16)
        acc_ref[h] = jnp.exp2(m_old - m_new) * acc_ref[h]
        m_ref[h] = m_new

    def pv_phase(k0):
        ks = pl.ds(k0, FAR_T)
        for h in range(N_HEADS_B):
            acc_ref[h] += jnp.dot(p_ref[h], va_ref[0, ks, h * LANES:(h + 1) * LANES], preferred_element_type=F32)

    near0 = (jnp.maximum(i - NEAR_MIN, 0) // FAR_G) * FAR_G

    def far_body(t, c0):
        k0 = pl.multiple_of(t * FAR_T, FAR_T)
        logits_phase(k0)
        for h in range(N_HEADS_B):
            softmax_phase(h, None, rb_ref[FAR_BUCKET, h] * LOG2E)
        pv_phase(k0)
        return c0

    lax.fori_loop(0, near0 // FAR_G, far_body, 0)

    def near_body(t, c0):
        k0 = pl.multiple_of(t * FAR_T, FAR_T)
        logits_phase(k0)
        dist = [jnp.maximum(i - (t * FAR_G + g), 0) for g in range(FAR_G)]
        for h in range(N_HEADS_B):
            def bias(c, h=h):
                return jnp.concatenate([nbias_ref[dist[c * part_g + g] * N_HEADS_B + h] for g in range(part_g)],
                                       axis=1)
            softmax_phase(h, bias, 0.0)
        pv_phase(k0)
        return c0

    lax.fori_loop(near0 // FAR_G, n_tiles, near_body, 0)

    def head_out(h):
        a = acc_ref[h]
        return a * (1.0 / a[:, HEAD_DIM_B:HEAD_DIM_B + 1])

    for p in range(N_HEADS_B // 2):
        o_odd = pltpu.roll(head_out(2 * p + 1), HEAD_DIM_B, axis=1)
        y_ref[0, :, p * LANES:(p + 1) * LANES] = jnp.where(lane < HEAD_DIM_B, head_out(2 * p), o_odd).astype(BF16)


def _dsa(qb, kb, vb, qi, small, rel_bias):
    bsz, s, _ = qb.shape
    assert s % FAR_T == 0
    k_sel = min(TOPK_KEYS_MAX, s // 4)
    kidx = small[:, :, S_KIDX:S_KIDX + IDX_DIM].astype(BF16)
    kidx2 = jnp.concatenate([kidx, kidx], axis=-1)
    v4 = vb.reshape(bsz, s, N_HEADS_B, HEAD_DIM_B)
    va = jnp.concatenate([v4, jnp.ones((bsz, s, N_HEADS_B, 1), BF16),
                          jnp.zeros((bsz, s, N_HEADS_B, LANES - HEAD_DIM_B - 1), BF16)], axis=-1)
    va = va.reshape(bsz, s, N_HEADS_B * LANES)
    tab = jnp.asarray(_near_bucket_table())
    blk = lambda b, i: (b, i, 0)
    full = lambda b, i: (b, 0, 0)
    one = pl.Buffered(1)
    return pl.pallas_call(
        functools.partial(_dsa_kernel, seq=s, k_sel=k_sel),
        grid=(bsz, s // QB),
        in_specs=[
            pl.BlockSpec(memory_space=pltpu.SMEM),
            pl.BlockSpec((1, QB, WIDTH_B), blk),
            pl.BlockSpec((1, QB, IDX_HEADS * IDX_DIM), blk),
            pl.BlockSpec((1, QB, LANES), blk),
            pl.BlockSpec((1, s, WIDTH_B), full, pipeline_mode=one),
            pl.BlockSpec((1, s, N_HEADS_B * LANES), full, pipeline_mode=one),
            pl.BlockSpec((1, s, LANES), full, pipeline_mode=one),
            pl.BlockSpec((NEAR_D, QB, LANES), lambda b, i: (0, 0, 0), pipeline_mode=one),
        ],
        out_specs=pl.BlockSpec((1, QB, WIDTH_B), blk),
        out_shape=jax.ShapeDtypeStruct((bsz, s, WIDTH_B), BF16),
        scratch_shapes=[
            pltpu.VMEM((IDX_HEADS * QB, LANES), BF16),
            pltpu.VMEM((N_HEADS_B * QB, LANES), BF16),
            pltpu.VMEM((IDX_HEADS, QB, LANES), F32),
            pltpu.VMEM((IDX_HEADS * QB, FAR_T), F32),
            pltpu.VMEM((QB, s), I32),
            pltpu.VMEM((NEAR_D * N_HEADS_B, QB, LANES), F32),
            pltpu.VMEM((QB, LANES), I32),
            pltpu.VMEM((QB, LANES), I32),
            pltpu.VMEM((N_HEADS_B, QB, LANES), F32),
            pltpu.VMEM((N_HEADS_B, QB, LANES), F32),
            pltpu.VMEM((N_HEADS_B, QB, FAR_T), F32),
            pltpu.VMEM((N_HEADS_B, QB, FAR_T), BF16),
        ],
        compiler_params=_cparams(("parallel", "arbitrary")),
        name="dsa",
    )(rel_bias, qb, qi, small, kb, va, kidx2, tab)


HALF_MASK = 0xFFFF0000


def _pack_halves(t):
    w = t.shape[1] // 2
    bits = pltpu.bitcast(t.astype(BF16).astype(F32), U32)
    return (bits[:, :w] >> 16) | (bits[:, w:] & jnp.uint32(HALF_MASK))


def _unpack_halves(p):
    lo = pltpu.bitcast(p << 16, F32)
    hi = pltpu.bitcast(p & jnp.uint32(HALF_MASK), F32)
    return jnp.concatenate([lo, hi], axis=1)


def _outproj_kernel(ya_ref, yb_ref, x_ref, mod_ref, wo_ref, nw_ref, rw_ref, rbias_ref,
                    xn_ref, hp_ref, ridx_ref, gate_ref):
    wa = ya_ref.shape[2]
    y = (jnp.dot(ya_ref[0], wo_ref[0:wa, :], preferred_element_type=F32)
         + jnp.dot(yb_ref[0], wo_ref[wa:, :], preferred_element_type=F32))
    xn = x_ref[0] + mod_ref[0, 2:3, :] * y
    xn_ref[0] = xn
    ms = jnp.mean(xn * xn, axis=-1, keepdims=True)
    h = xn * lax.rsqrt(ms + EPS) * nw_ref[...] * (1.0 + mod_ref[0, 4:5, :]) + mod_ref[0, 3:4, :]
    hp_ref[0] = _pack_halves(h)

    logits = jnp.dot(h, rw_ref[...], precision=HIGHEST, preferred_element_type=F32) + rbias_ref[...]
    lane = lax.broadcasted_iota(I32, logits.shape, 1)
    cur = logits
    vals, ridx = [], jnp.zeros(logits.shape, I32)
    for k in range(TOP_K):
        mx = jnp.max(cur, axis=1, keepdims=True)
        am = jnp.min(jnp.where(cur == mx, lane, LANES), axis=1, keepdims=True)
        cur = jnp.where(lane == am, -jnp.inf, cur)
        vals.append(mx)
        ridx = jnp.where(lane == k, am, ridx)
    ex = [jnp.exp(v - vals[0]) for v in vals]
    inv = 1.0 / (ex[0] + ex[1] + ex[2] + ex[3])
    gate = jnp.zeros(logits.shape, F32)
    for k in range(TOP_K):
        gate = jnp.where(lane == k, ex[k] * inv, gate)
    ridx_ref[0] = ridx
    gate_ref[0] = gate


def _outproj(y_a, y_b, x, mod_l, w_out_bf, norm_w, router_w, router_b, tm):
    bsz, s, d = x.shape
    n_e = router_w.shape[1]
    rw = jnp.zeros((d, LANES), F32).at[:, :n_e].set(router_w)
    rbias = jnp.full((1, LANES), NEG_BIG, F32).at[0, :n_e].set(router_b)
    blk = lambda b, i: (b, i, 0)
    const2 = lambda b, i: (0, 0)
    return pl.pallas_call(
        _outproj_kernel,
        grid=(bsz, s // tm),
        in_specs=[
            pl.BlockSpec((1, tm, y_a.shape[2]), blk),
            pl.BlockSpec((1, tm, y_b.shape[2]), blk),
            pl.BlockSpec((1, tm, d), blk),
            pl.BlockSpec((1, 6, d), lambda b, i: (b, 0, 0)),
            pl.BlockSpec((d, d), const2),
            pl.BlockSpec((1, d), const2),
            pl.BlockSpec((d, LANES), const2),
            pl.BlockSpec((1, LANES), const2),
        ],
        out_specs=[pl.BlockSpec((1, tm, d), blk), pl.BlockSpec((1, tm, d // 2), blk),
                   pl.BlockSpec((1, tm, LANES), blk), pl.BlockSpec((1, tm, LANES), blk)],
        out_shape=[jax.ShapeDtypeStruct((bsz, s, d), F32), jax.ShapeDtypeStruct((bsz, s, d // 2), U32),
                   jax.ShapeDtypeStruct((bsz, s, LANES), I32), jax.ShapeDtypeStruct((bsz, s, LANES), F32)],
        compiler_params=_cparams(("parallel", "parallel")),
        name="outproj_router",
    )(y_a, y_b, x, mod_l, w_out_bf, norm_w.reshape(1, d), rw, rbias)


MOE_TB = 2048
MOE_RB = 512
MOE_M = 128


def _moe_kernel(cnt_ref, off_ref, list_ref, hp_ref, w1_ref, b1_ref, w2_ref, b2_ref, gate_ref, x_ref, g2_ref,
                o_ref, slots_ref, xg_ref, yb_ref, *, tb, rb):
    sb = pl.program_id(0)
    e = pl.program_id(1)
    dff = w2_ref.shape[1]
    last = TOP_K * tb - 1

    @pl.when(e < N_EXPERTS)
    def _():
        cnt = cnt_ref[sb * N_EXPERTS + e]
        off = off_ref[sb * N_EXPERTS + e]

        def chunk(c, carry):
            base = off + c * MOE_M

            def gather(r, c1):
                code = list_ref[0, 0, jnp.minimum(base + r, last)]
                xg_ref[pl.ds(r, 1), :] = hp_ref[pl.ds(code >> 2, 1), :]
                return c1

            lax.fori_loop(0, MOE_M, gather, 0, unroll=8)
            xb = _unpack_halves(xg_ref[...]).astype(BF16)
            u = jnp.dot(xb, w1_ref[0], preferred_element_type=F32) + b1_ref[0]
            glu = jnp.minimum(u[:, :dff], SWIGLU_LIMIT)
            lin = jnp.clip(u[:, dff:], -SWIGLU_LIMIT, SWIGLU_LIMIT)
            act = glu * _sigmoid(SWIGLU_ALPHA * glu) * (lin + 1.0)
            y = jnp.dot(act.astype(BF16), w2_ref[0], preferred_element_type=F32) + b2_ref[0]
            yb_ref[...] = _pack_halves(y)

            n_valid = cnt - c * MOE_M

            def scatter(r, c1):
                code = jnp.where(r < n_valid, list_ref[0, 0, jnp.minimum(base + r, last)], TOP_K * tb)
                slots_ref[code & (TOP_K - 1), pl.ds(code >> 2, 1), :] = yb_ref[pl.ds(r, 1), :]
                return c1

            lax.fori_loop(0, MOE_M, scatter, 0, unroll=8)
            return carry

        lax.fori_loop(0, (cnt + MOE_M - 1) // MOE_M, chunk, 0)

    @pl.when(e >= N_EXPERTS)
    def _():
        r0 = pl.multiple_of((e - N_EXPERTS) * rb, rb)
        acc = gate_ref[:, 0:1] * _unpack_halves(slots_ref[0, pl.ds(r0, rb), :])
        for k in range(1, TOP_K):
            acc = acc + gate_ref[:, k:k + 1] * _unpack_halves(slots_ref[k, pl.ds(r0, rb), :])
        o_ref[...] = x_ref[...] + g2_ref[0] * acc


def _moe(xn, hp, ridx, gate, g2, w1p, b1p, w2b, b2, tb, rb):
    bsz, s, d = xn.shape
    t = bsz * s
    n_super = t // tb
    n_piece = tb // rb
    dff = w2b.shape[1]
    flat_e = ridx[:, :, :TOP_K].reshape(n_super, tb * TOP_K)
    order = jnp.argsort(flat_e, axis=1, stable=True).astype(I32)
    counts = jnp.sum(flat_e[:, :, None] == jnp.arange(N_EXPERTS, dtype=I32)[None, None, :], axis=1).astype(I32)
    offs = (jnp.cumsum(counts, axis=1) - counts).astype(I32)

    piece = lambda sb, e, *_: (sb * n_piece + jnp.maximum(e - N_EXPERTS, 0), 0)
    wmap = lambda sb, e, *_: (jnp.minimum(e, N_EXPERTS - 1), 0, 0)
    grid_spec = pltpu.PrefetchScalarGridSpec(
        num_scalar_prefetch=2,
        grid=(n_super, N_EXPERTS + n_piece),
        in_specs=[
            pl.BlockSpec((1, 1, tb * TOP_K), lambda sb, e, *_: (sb, 0, 0), memory_space=pltpu.SMEM),
            pl.BlockSpec((tb, d // 2), lambda sb, e, *_: (sb, 0), pipeline_mode=pl.Buffered(1)),
            pl.BlockSpec((1, d, 2 * dff), wmap),
            pl.BlockSpec((1, 1, 2 * dff), wmap),
            pl.BlockSpec((1, dff, d), wmap),
            pl.BlockSpec((1, 1, d), wmap),
            pl.BlockSpec((rb, LANES), piece),
            pl.BlockSpec((rb, d), piece),
            pl.BlockSpec((1, 1, d), lambda sb, e, *_: ((sb * tb) // s, 0, 0)),
        ],
        out_specs=pl.BlockSpec((rb, d), piece),
        scratch_shapes=[
            pltpu.VMEM((TOP_K, tb + SUBLANES, d // 2), U32),
            pltpu.VMEM((MOE_M, d // 2), U32),
            pltpu.VMEM((MOE_M, d // 2), U32),
        ],
    )
    out = pl.pallas_call(
        functools.partial(_moe_kernel, tb=tb, rb=rb),
        grid_spec=grid_spec,
        out_shape=jax.ShapeDtypeStruct((t, d), F32),
        compiler_params=_cparams(("arbitrary", "arbitrary")),
        name="moe",
    )(counts.reshape(-1), offs.reshape(-1), order.reshape(n_super, 1, tb * TOP_K), hp.reshape(t, d // 2),
      w1p, b1p, w2b, b2, gate.reshape(t, LANES), xn.reshape(t, d), g2.reshape(bsz, 1, d))
    return out.reshape(bsz, s, d)


MXU_COLS = 256


def _deinterleave_kernel(w_ref, perm_ref, o_ref):
    half = w_ref.shape[2] // 2
    hw = MXU_COLS // 2
    for b in range(w_ref.shape[2] // MXU_COLS):
        blk = w_ref[0, :, b * MXU_COLS:(b + 1) * MXU_COLS].astype(BF16)
        y = jnp.dot(blk, perm_ref[...], preferred_element_type=F32).astype(BF16)
        o_ref[0, :, b * hw:(b + 1) * hw] = y[:, :hw]
        o_ref[0, :, half + b * hw:half + (b + 1) * hw] = y[:, hw:]


def _deinterleave_cast(w1_l, rows=512):
    n_e, d, two_f = w1_l.shape
    src = np.concatenate([np.arange(0, MXU_COLS, 2), np.arange(1, MXU_COLS, 2)])
    perm = np.zeros((MXU_COLS, MXU_COLS), np.float32)
    perm[src, np.arange(MXU_COLS)] = 1.0
    return pl.pallas_call(
        _deinterleave_kernel,
        grid=(n_e, d // rows),
        in_specs=[pl.BlockSpec((1, rows, two_f), lambda e, r: (e, r, 0)),
                  pl.BlockSpec((MXU_COLS, MXU_COLS), lambda e, r: (0, 0))],
        out_specs=pl.BlockSpec((1, rows, two_f), lambda e, r: (e, r, 0)),
        out_shape=jax.ShapeDtypeStruct((n_e, d, two_f), BF16),
        compiler_params=_cparams(("parallel", "parallel")),
        name="w1_deinterleave",
    )(w1_l, jnp.asarray(perm, BF16))


def _prep_expert_weights(w1_l, b1_l, w2_l, b2_l):
    n_e, d, two_f = w1_l.shape
    b1p = jnp.concatenate([b1_l[:, 0::2], b1_l[:, 1::2]], axis=1).reshape(n_e, 1, two_f)
    return _deinterleave_cast(w1_l), b1p, w2_l.astype(BF16), b2_l.reshape(n_e, 1, -1)


def kernel(x, c, rel_bias, mod_w, mod_b, norm_mix_w, norm_ffn_w, w_in, conv_w, a_log, dt_bias, gdn_norm_w,
           q_norm_w, k_norm_w, w_out, router_w, router_b, w1, b1, w2, b2):
    depth = mod_w.shape[0]
    bsz, s, d = x.shape
    mod = _modulation(c, mod_w, mod_b)
    tm = min(512, s)
    tb = min(MOE_TB, bsz * s)
    rb = min(MOE_RB, tb)
    for l in range(depth):
        qkv_a, z_a, qb, kb, vb, qi, small = _inproj(
            x, mod[l], norm_mix_w[l], _permute_w_in(w_in[l]), q_norm_w[l], k_norm_w[l], tm)
        y_a = _gdn(qkv_a, z_a, small, conv_w[l], a_log[l], dt_bias[l], gdn_norm_w[l], sb=tm)
        y_b = _dsa(qb, kb, vb, qi, small, rel_bias)
        xn, hp, ridx, gate = _outproj(y_a, y_b, x, mod[l], w_out[l].astype(BF16), norm_ffn_w[l],
                                      router_w[l], router_b[l], tm)
        x = _moe(xn, hp, ridx, gate, mod[l][:, 5], *_prep_expert_weights(w1[l], b1[l], w2[l], b2[l]), tb, rb)
    return x
```

```python
import functools
import math

import jax
import jax.numpy as jnp
import numpy as np
from jax import lax
from jax.experimental import pallas as pl
from jax.experimental.pallas import tpu as pltpu

F32 = jnp.float32
BF16 = jnp.bfloat16
I32 = jnp.int32
I16 = jnp.int16
U32 = jnp.uint32
HIGHEST = lax.Precision.HIGHEST

LANES = 128
SUBLANES = 8
VMEM_LIMIT_BYTES = 56 * 1024 * 1024

CHUNK = 64
HEAD_DIM_A = 128
N_HEADS_A = 4
WIDTH_A = N_HEADS_A * HEAD_DIM_A
CONV_K = 4
HEAD_DIM_B = 64
N_HEADS_B = 8
WIDTH_B = N_HEADS_B * HEAD_DIM_B
IDX_HEADS = 8
IDX_DIM = 64
TOPK_KEYS_MAX = 256
REL_BUCKETS = 32
REL_MAX_DIST = 1024
N_EXPERTS = 32
TOP_K = 4
SWIGLU_ALPHA = 1.702
SWIGLU_LIMIT = 7.0
EPS = 1e-6
NEG_BIG = -1e30

C_QKVA = 0
C_Z = C_QKVA + 3 * WIDTH_A
C_QB = C_Z + WIDTH_A
C_KB = C_QB + WIDTH_B
C_VB = C_KB + WIDTH_B
C_QI = C_VB + WIDTH_B
C_SMALL = C_QI + IDX_HEADS * IDX_DIM
D_IN_PAD = C_SMALL + LANES
S_KIDX = 0
S_B = IDX_DIM
S_A = S_B + N_HEADS_A
S_WIDX = S_A + N_HEADS_A


def _cparams(sem):
    return pltpu.CompilerParams(dimension_semantics=sem, vmem_limit_bytes=VMEM_LIMIT_BYTES)


def _silu(x):
    return x * (1.0 / (1.0 + jnp.exp(-x)))


def _sigmoid(x):
    return 1.0 / (1.0 + jnp.exp(-x))


def _softplus(x):
    return jnp.maximum(x, 0.0) + jnp.log(1.0 + jnp.exp(-jnp.abs(x)))


def _mod_kernel(c_ref, w_ref, b_ref, o_ref):
    a = _silu(c_ref[...])
    o_ref[0] = jnp.dot(a, w_ref[0], precision=HIGHEST, preferred_element_type=F32) + b_ref[0]


def _modulation(c, mod_w, mod_b):
    depth, d, n = mod_w.shape
    bsz = c.shape[0]
    rows = -(-bsz // SUBLANES) * SUBLANES
    c_pad = jnp.zeros((rows, d), F32).at[:bsz].set(c)
    tn = 1536
    out = pl.pallas_call(
        _mod_kernel,
        grid=(depth, n // tn),
        in_specs=[
            pl.BlockSpec((rows, d), lambda l, j: (0, 0)),
            pl.BlockSpec((1, d, tn), lambda l, j: (l, 0, j)),
            pl.BlockSpec((1, 1, tn), lambda l, j: (l, 0, j)),
        ],
        out_specs=pl.BlockSpec((1, rows, tn), lambda l, j: (l, 0, j)),
        out_shape=jax.ShapeDtypeStruct((depth, rows, n), F32),
        compiler_params=_cparams(("arbitrary", "arbitrary")),
        name="adaln_mod",
    )(c_pad, mod_w, mod_b.reshape(depth, 1, n))
    return out[:, :bsz].reshape(depth, bsz, 6, d)


def _head_rms(t, group_ref, wn, inv_dim):
    t2 = t * t
    hi = t2.astype(BF16)
    lo = (t2 - hi.astype(F32)).astype(BF16)
    ss = (jnp.dot(hi, group_ref[...], preferred_element_type=F32)
          + jnp.dot(lo, group_ref[...], preferred_element_type=F32))
    return t * lax.rsqrt(ss * inv_dim + EPS) * wn


def _inproj_kernel(x_ref, mod_ref, nw_ref, w_ref, group_ref, qn_ref, kn_ref,
                   qkva_ref, z_ref, qb_ref, kb_ref, vb_ref, qi_ref, small_ref):
    x = x_ref[0]
    ms = jnp.mean(x * x, axis=-1, keepdims=True)
    y = x * lax.rsqrt(ms + EPS) * nw_ref[...]
    h = y * (1.0 + mod_ref[0, 1:2, :]) + mod_ref[0, 0:1, :]
    hb = h.astype(BF16)

    def mm(lo, width):
        return jnp.dot(hb, w_ref[:, lo:lo + width], preferred_element_type=F32)

    qkva_ref[0] = mm(C_QKVA, 3 * WIDTH_A)
    z_ref[0] = mm(C_Z, WIDTH_A)
    q = _head_rms(mm(C_QB, WIDTH_B), group_ref, qn_ref[...], 1.0 / HEAD_DIM_B)
    qb_ref[0] = (q * (HEAD_DIM_B ** -0.5 * LOG2E)).astype(BF16)
    k = _head_rms(mm(C_KB, WIDTH_B), group_ref, kn_ref[...], 1.0 / HEAD_DIM_B)
    kb_ref[0] = k.astype(BF16)
    vb_ref[0] = mm(C_VB, WIDTH_B).astype(BF16)
    qi_ref[0] = mm(C_QI, IDX_HEADS * IDX_DIM).astype(BF16)
    small_ref[0] = mm(C_SMALL, LANES)


def _permute_w_in(w_in_l):
    d = w_in_l.shape[0]
    o = 0
    qkva = w_in_l[:, o:o + 3 * WIDTH_A]; o += 3 * WIDTH_A
    z = w_in_l[:, o:o + WIDTH_A]; o += WIDTH_A
    b = w_in_l[:, o:o + N_HEADS_A]; o += N_HEADS_A
    a = w_in_l[:, o:o + N_HEADS_A]; o += N_HEADS_A
    qkvb = w_in_l[:, o:o + 3 * WIDTH_B]; o += 3 * WIDTH_B
    qi = w_in_l[:, o:o + IDX_HEADS * IDX_DIM]; o += IDX_HEADS * IDX_DIM
    ki = w_in_l[:, o:o + IDX_DIM]; o += IDX_DIM
    wi = w_in_l[:, o:o + IDX_HEADS]; o += IDX_HEADS
    pad = jnp.zeros((d, LANES - IDX_DIM - 2 * N_HEADS_A - IDX_HEADS), w_in_l.dtype)
    return jnp.concatenate([qkva, z, qkvb, qi, ki, b, a, wi, pad], axis=1).astype(BF16)


def _group_ones(width, group):
    g = np.arange(width) // group
    return jnp.asarray((g[:, None] == g[None, :]).astype(np.float32), dtype=BF16)


def _inproj(x, mod_l, norm_w, w_perm, q_norm_w, k_norm_w, tm):
    bsz, s, d = x.shape
    f = lambda b, i: (b, i, 0)
    const2 = lambda b, i: (0, 0)
    outs = [
        (3 * WIDTH_A, F32), (WIDTH_A, F32), (WIDTH_B, BF16), (WIDTH_B, BF16), (WIDTH_B, BF16),
        (IDX_HEADS * IDX_DIM, BF16), (LANES, F32),
    ]
    return pl.pallas_call(
        _inproj_kernel,
        grid=(bsz, s // tm),
        in_specs=[
            pl.BlockSpec((1, tm, d), f),
            pl.BlockSpec((1, 6, d), lambda b, i: (b, 0, 0)),
            pl.BlockSpec((1, d), const2),
            pl.BlockSpec((d, D_IN_PAD), const2),
            pl.BlockSpec((WIDTH_B, WIDTH_B), const2),
            pl.BlockSpec((1, WIDTH_B), const2),
            pl.BlockSpec((1, WIDTH_B), const2),
        ],
        out_specs=[pl.BlockSpec((1, tm, w), f) for w, _ in outs],
        out_shape=[jax.ShapeDtypeStruct((bsz, s, w), dt) for w, dt in outs],
        compiler_params=_cparams(("parallel", "parallel")),
        name="inproj",
    )(x, mod_l, norm_w.reshape(1, d), w_perm, _group_ones(WIDTH_B, HEAD_DIM_B),
      jnp.tile(q_norm_w, N_HEADS_B).reshape(1, WIDTH_B), jnp.tile(k_norm_w, N_HEADS_B).reshape(1, WIDTH_B))


def _dot_nt(a, b, precision=None):
    return lax.dot_general(a, b, (((1,), (1,)), ((), ())), precision=precision, preferred_element_type=F32)


def _mm_bf16(a, b):
    return jnp.dot(a.astype(BF16), b.astype(BF16), preferred_element_type=F32)


def _split2(x):
    hi = x.astype(BF16)
    return hi, (x - hi.astype(F32)).astype(BF16)


def _split3(x):
    hi = x.astype(BF16)
    r = x - hi.astype(F32)
    mid = r.astype(BF16)
    return hi, mid, (r - mid.astype(F32)).astype(BF16)


def _mm3(a, b):
    return (jnp.dot(a[0], b[0], preferred_element_type=F32) + jnp.dot(a[0], b[1], preferred_element_type=F32)
            + jnp.dot(a[1], b[0], preferred_element_type=F32))


GDN_PAR = 4


def _gdn_kernel(qkv_ref, z_ref, small_ref, convw_ref, alog_ref, dtb_ref, nw_ref,
                y_ref, xe_ref, u_ref, state_ref, uval_ref, wdec_ref, qg_ref, kdec_ref, attn_ref, egl_ref, *, sb):
    n_chunks = sb // CHUNK
    halo = SUBLANES

    @pl.when(pl.program_id(1) == 0)
    def _():
        xe_ref[0:halo, :] = jnp.zeros((halo, 3 * WIDTH_A), F32)
        state_ref[...] = jnp.zeros_like(state_ref)

    xe_ref[halo:halo + sb, :] = qkv_ref[0]

    rows = 128
    for g in range(3 * WIDTH_A // LANES):
        cs = slice(g * LANES, (g + 1) * LANES)
        for r in range(sb // rows):
            base = halo - (CONV_K - 1) + r * rows
            acc = xe_ref[base:base + rows, cs] * convw_ref[0:1, cs]
            for j in range(1, CONV_K):
                acc = acc + xe_ref[base + j:base + j + rows, cs] * convw_ref[j:j + 1, cs]
            u_ref[r * rows:(r + 1) * rows, cs] = _silu(acc)

    xe_ref[0:halo, :] = xe_ref[sb:sb + halo, :]

    wide = N_HEADS_A * CHUNK
    heads = range(N_HEADS_A)
    ii = lax.broadcasted_iota(I32, (CHUNK, wide), 0)
    jj = lax.broadcasted_iota(I32, (CHUNK, wide), 1) & (CHUNK - 1)
    eye_w = (ii == jj).astype(F32)
    tri = (lax.broadcasted_iota(I32, (CHUNK, CHUNK), 0)
           >= lax.broadcasted_iota(I32, (CHUNK, CHUNK), 1)).astype(F32).astype(BF16)
    shift = int(math.log2(CHUNK))
    bd_mask = ((lax.broadcasted_iota(I32, (wide, wide), 0) >> shift)
               == (lax.broadcasted_iota(I32, (wide, wide), 1) >> shift)).astype(F32)

    bd_mask = bd_mask.astype(BF16)

    def block_diag(parts):
        return tuple(jnp.concatenate([m] * N_HEADS_A, axis=0) * bd_mask for m in parts)

    def prepare(c):
        rs = pl.ds(pl.multiple_of(c * CHUNK, CHUNK), CHUNK)
        qn, kn, v, beta, g_b = [], [], [], [], []
        for h in heads:
            q = u_ref[rs, h * HEAD_DIM_A:(h + 1) * HEAD_DIM_A]
            k = u_ref[rs, WIDTH_A + h * HEAD_DIM_A:WIDTH_A + (h + 1) * HEAD_DIM_A]
            v.append(u_ref[rs, 2 * WIDTH_A + h * HEAD_DIM_A:2 * WIDTH_A + (h + 1) * HEAD_DIM_A])
            qn.append(q * (lax.rsqrt(jnp.sum(q * q, axis=-1, keepdims=True) + EPS) * (HEAD_DIM_A ** -0.5)))
            kn.append(k * lax.rsqrt(jnp.sum(k * k, axis=-1, keepdims=True) + EPS))
            beta.append(_sigmoid(small_ref[0, rs, S_B + h:S_B + h + 1]))
            g = -jnp.exp(alog_ref[0:1, h:h + 1]) * _softplus(small_ref[0, rs, S_A + h:S_A + h + 1]
                                                            + dtb_ref[0:1, h:h + 1])
            g_b.append(jnp.broadcast_to(g, (CHUNK, CHUNK)))
        gc_w = sum(jnp.dot(tri, part, preferred_element_type=F32) for part in _split3(jnp.concatenate(g_b, axis=1)))
        gc_row = jnp.sum(jnp.where(ii == jj, gc_w, 0.0), axis=0, keepdims=True)
        decay_w = jnp.exp(jnp.where(ii >= jj, gc_w - gc_row, NEG_BIG))
        k_beta = [kn[h] * beta[h] for h in heads]
        kk_w = jnp.concatenate([_dot_nt(k_beta[h].astype(BF16), kn[h].astype(BF16)) for h in heads], axis=1)
        a_w = -jnp.where(ii > jj, kk_w * decay_w, 0.0)
        gc = [gc_w[:, h * CHUNK:h * CHUNK + 1] for h in heads]
        egc = [jnp.exp(gc[h]) for h in heads]
        qk = [_dot_nt(qn[h].astype(BF16), kn[h].astype(BF16)) for h in heads]
        for h in heads:
            g_last = gc[h][CHUNK - 1:CHUNK, :]
            qg_ref[c, h] = (qn[h] * egc[h]).astype(BF16)
            kdec_ref[c, h] = kn[h] * jnp.exp(g_last - gc[h])
            attn_ref[c, h] = (qk[h] * decay_w[:, h * CHUNK:(h + 1) * CHUNK]).astype(BF16)
            egl_ref[c, h] = jnp.broadcast_to(jnp.exp(g_last), (SUBLANES, LANES))
        return a_w, [_split2(v[h] * beta[h]) for h in heads], [_split2(k_beta[h] * egc[h]) for h in heads]

    def solve_body(cg, carry):
        group = range(GDN_PAR)
        chunks = [cg * GDN_PAR + i for i in group]
        pre = [prepare(c) for c in chunks]
        t_w = [eye_w + pre[i][0] for i in group]
        p_parts = [_split2(pre[i][0]) for i in group]
        bd = [block_diag(p_parts[i]) for i in group]
        for _ in range(shift - 1):
            prod = [_mm3(p_parts[i], bd[i]) for i in group]
            p_parts = [_split2(prod[i]) for i in group]
            bd = [block_diag(p_parts[i]) for i in group]
            upd = [_mm3(_split2(t_w[i]), bd[i]) for i in group]
            t_w = [t_w[i] + upd[i] for i in group]
        for i in group:
            t_h = [_split2(t_w[i][:, h * CHUNK:(h + 1) * CHUNK]) for h in heads]
            u_val = [_mm3(t_h[h], pre[i][1][h]) for h in heads]
            w_dec = [_mm3(t_h[h], pre[i][2][h]) for h in heads]
            for h in heads:
                uval_ref[chunks[i], h] = u_val[h]
                wdec_ref[chunks[i], h] = w_dec[h].astype(BF16)
        return carry

    lax.fori_loop(0, n_chunks // GDN_PAR, solve_body, 0)

    def scan_body(c, carry):
        rs = pl.ds(pl.multiple_of(c * CHUNK, CHUNK), CHUNK)
        state = [state_ref[h] for h in heads]
        state_b = [s_h.astype(BF16) for s_h in state]
        w_s = [jnp.dot(wdec_ref[c, h], state_b[h], preferred_element_type=F32) for h in heads]
        v_new = [(uval_ref[c, h] - w_s[h]).astype(BF16) for h in heads]
        o = [jnp.dot(qg_ref[c, h], state_b[h], preferred_element_type=F32)
             + jnp.dot(attn_ref[c, h], v_new[h], preferred_element_type=F32) for h in heads]
        for h in heads:
            state_ref[h] = (state[h] * egl_ref[c, h][0:1, 0:1]
                            + jnp.dot(kdec_ref[c, h].T.astype(BF16), v_new[h], preferred_element_type=F32))
        for h in heads:
            hs = slice(h * HEAD_DIM_A, (h + 1) * HEAD_DIM_A)
            on = o[h] * lax.rsqrt(jnp.mean(o[h] * o[h], axis=-1, keepdims=True) + EPS) * nw_ref[...]
            y_ref[0, rs, hs] = (on * _silu(z_ref[0, rs, hs])).astype(BF16)
        return carry

    lax.fori_loop(0, n_chunks, scan_body, 0)


def _gdn(qkv_a, z_a, small, conv_w, a_log, dt_bias, norm_w, sb):
    bsz, s, _ = qkv_a.shape
    n_c = sb // CHUNK
    per_head = (n_c, N_HEADS_A, CHUNK, HEAD_DIM_A)
    f = lambda b, i: (b, i, 0)
    const2 = lambda b, i: (0, 0)
    return pl.pallas_call(
        functools.partial(_gdn_kernel, sb=sb),
        grid=(bsz, s // sb),
        in_specs=[
            pl.BlockSpec((1, sb, 3 * WIDTH_A), f),
            pl.BlockSpec((1, sb, WIDTH_A), f),
            pl.BlockSpec((1, sb, LANES), f),
            pl.BlockSpec((CONV_K, 3 * WIDTH_A), const2),
            pl.BlockSpec((1, N_HEADS_A), const2),
            pl.BlockSpec((1, N_HEADS_A), const2),
            pl.BlockSpec((1, HEAD_DIM_A), const2),
        ],
        out_specs=pl.BlockSpec((1, sb, WIDTH_A), f),
        out_shape=jax.ShapeDtypeStruct((bsz, s, WIDTH_A), BF16),
        scratch_shapes=[
            pltpu.VMEM((sb + SUBLANES, 3 * WIDTH_A), F32),
            pltpu.VMEM((sb, 3 * WIDTH_A), F32),
            pltpu.VMEM((N_HEADS_A, HEAD_DIM_A, HEAD_DIM_A), F32),
            pltpu.VMEM(per_head, F32),
            pltpu.VMEM(per_head, BF16),
            pltpu.VMEM(per_head, BF16),
            pltpu.VMEM(per_head, F32),
            pltpu.VMEM((n_c, N_HEADS_A, CHUNK, CHUNK), BF16),
            pltpu.VMEM((n_c, N_HEADS_A, SUBLANES, LANES), F32),
        ],
        compiler_params=_cparams(("parallel", "arbitrary")),
        name="gdn",
    )(qkv_a, z_a, small, conv_w, a_log.reshape(1, -1), dt_bias.reshape(1, -1), norm_w.reshape(1, -1))


QB = 128
FAR_T = 512
FAR_G = FAR_T // LANES
ATT_W = 256
LOG2E = 1.4426950408889634
NEAR_D = 9
NEAR_MIN = 5
INT_MIN = -2 ** 31
HALF_MIN = -2 ** 15


def _t5_bucket_np(rel):
    nb = REL_BUCKETS // 2
    max_exact = nb // 2
    side = np.where(rel > 0, nb, 0)
    n = np.abs(rel)
    nf = np.maximum(n, 1).astype(np.float32)
    large = max_exact + (np.log(nf / np.float32(max_exact)) / np.float32(math.log(REL_MAX_DIST / max_exact))
                         * np.float32(nb - max_exact)).astype(np.int32)
    large = np.minimum(large, nb - 1)
    return (side + np.where(n < max_exact, n, large)).astype(np.int32)


def _near_bucket_table():
    r = np.arange(QB)[:, None]
    c = np.arange(LANES)[None, :]
    return np.stack([_t5_bucket_np(c - r - LANES * d) for d in range(NEAR_D)])


FAR_BUCKET = int(_t5_bucket_np(np.array([-(NEAR_MIN * LANES + 1)]))[0])
assert all(int(b) == FAR_BUCKET for b in _t5_bucket_np(-np.arange((NEAR_MIN + 1) * LANES - (QB - 1), 1 << 20, 997)))


def _sortable_key(score):
    bits = pltpu.bitcast(score + 0.0, I32)
    return bits ^ ((bits >> 31) & 0x7FFFFFFF)


def _dsa_kernel(rb_ref, qb_ref, qi_ref, small_ref, kb_ref, va_ref, kidx2_ref, tab_ref,
                y_ref, qis_ref, qs_ref, wb_ref, sdot_ref, keys_ref, nbias_ref, thr_ref, jlim_ref,
                m_ref, acc_ref, s_ref, p_ref, half_ref, *, seq, k_sel):
    i = pl.program_id(1)
    lane = lax.broadcasted_iota(I32, (QB, LANES), 1)
    row = lax.broadcasted_iota(I32, (QB, LANES), 0)
    even_f = (lane < HEAD_DIM_B).astype(F32)
    even_b = even_f.astype(BF16)
    odd_b = (1.0 - even_f).astype(BF16)

    @pl.when(i == 0)
    def _():
        nbias_ref[...] = jnp.zeros_like(nbias_ref)

        def d_body(d, c0):
            tab = tab_ref[d]

            def b_body(bk, c1):
                hit = tab == bk
                for h in range(N_HEADS_B):
                    nbias_ref[d * N_HEADS_B + h] = jnp.where(hit, rb_ref[bk, h] * LOG2E,
                                                             nbias_ref[d * N_HEADS_B + h])
                return c1

            return lax.fori_loop(0, REL_BUCKETS, b_body, c0)

        lax.fori_loop(0, NEAR_D, d_body, 0)

    for p in range(N_HEADS_B // 2):
        ps = slice(p * LANES, (p + 1) * LANES)
        qi_pair = qi_ref[0, :, ps]
        qis_ref[(2 * p) * QB:(2 * p + 1) * QB, :] = qi_pair * even_b
        qis_ref[(2 * p + 1) * QB:(2 * p + 2) * QB, :] = qi_pair * odd_b
        q_pair = qb_ref[0, :, ps]
        qs_ref[(2 * p) * QB:(2 * p + 1) * QB, :] = q_pair * even_b
        qs_ref[(2 * p + 1) * QB:(2 * p + 2) * QB, :] = q_pair * odd_b
    w_scale = IDX_HEADS ** -0.5 * IDX_DIM ** -0.5
    for h in range(IDX_HEADS):
        wb_ref[h] = jnp.broadcast_to(small_ref[0, :, S_WIDX + h:S_WIDX + h + 1] * w_scale, (QB, LANES))

    limit = i * QB + CHUNK + jnp.where(row >= CHUNK, CHUNK, 0)

    def score_body(t, c0):
        k0 = pl.multiple_of(t * FAR_T, FAR_T)
        sdot_ref[...] = _dot_nt(qis_ref[...], kidx2_ref[0, pl.ds(k0, FAR_T), :])
        for g in range(FAR_G):
            gs = slice(g * LANES, (g + 1) * LANES)
            acc = jnp.maximum(sdot_ref[0:QB, gs], 0.0) * wb_ref[0]
            for h in range(1, IDX_HEADS):
                acc = acc + jnp.maximum(sdot_ref[h * QB:(h + 1) * QB, gs], 0.0) * wb_ref[h]
            col = k0 + g * LANES + lane
            key = jnp.where(col < limit, _sortable_key(acc), INT_MIN)
            cs = pl.ds(pl.multiple_of(k0 + g * LANES, LANES), LANES)
            keys_ref[:, cs] = key
            half_ref[:, cs] = (key >> 16).astype(I16)
        return c0

    lax.fori_loop(0, i // FAR_G + 1, score_body, 0)

    n_groups = i + 1
    n_tiles = i // FAR_G + 1

    def count(pred):
        def t_body(t, acc):
            for g in range(FAR_G):
                c0 = pl.multiple_of(t * FAR_T + g * LANES, LANES)
                kt = keys_ref[:, pl.ds(c0, LANES)]
                acc = acc + jnp.where(pred(kt, c0 + lane), 1, 0)
            return acc

        acc = lax.fori_loop(0, n_tiles, t_body, jnp.zeros((QB, LANES), I32))
        return jnp.broadcast_to(jnp.sum(acc, axis=1, keepdims=True), (QB, LANES))

    thr_ref[...] = jnp.full((QB, LANES), INT_MIN, I32)
    jlim_ref[...] = jnp.full((QB, LANES), -1, I32)

    def count_half(pred):
        def t_body(t, acc):
            for g in range(FAR_G):
                ht = half_ref[:, pl.ds(pl.multiple_of(t * FAR_T + g * LANES, LANES), LANES)]
                acc = acc + jnp.where(pred(ht), jnp.int16(1), jnp.int16(0))
            return acc

        acc = lax.fori_loop(0, n_tiles, t_body, jnp.zeros((QB, LANES), I16))
        return jnp.broadcast_to(jnp.sum(acc.astype(I32), axis=1, keepdims=True), (QB, LANES))

    def search_half(base):
        def bit_body(step, r):
            cand = r + lax.shift_left(jnp.int32(1), 15 - step)
            cand16 = cand.astype(I16)
            cnt = base + count_half(lambda ht: ht >= cand16)
            return jnp.where(cnt >= k_sel, cand, r)

        return lax.fori_loop(0, 16, bit_body, jnp.full((QB, LANES), HALF_MIN, I32))

    @pl.when(n_groups * QB > k_sel)
    def _():
        top = search_half(0)
        top16 = top.astype(I16)
        above = count_half(lambda ht: ht > top16)

        def low_body(t, c0):
            for g in range(FAR_G):
                cs = pl.ds(pl.multiple_of(t * FAR_T + g * LANES, LANES), LANES)
                kt = keys_ref[:, cs]
                low = (kt & 0xFFFF) + HALF_MIN
                half_ref[:, cs] = jnp.where((kt >> 16) == top, low, HALF_MIN).astype(I16)
            return c0

        lax.fori_loop(0, n_tiles, low_body, 0)
        low = search_half(above)
        r = lax.shift_left(top, 16) + (low - HALF_MIN)
        thr_ref[...] = r
        need = k_sel - count(lambda kt, col: kt > r)
        excess = count(lambda kt, col: kt == r) - need
        jlim_ref[...] = jnp.where(r == INT_MIN, -1, seq)

        @pl.when(jnp.max(excess) > 0)
        def _():
            def j_body(step, jl):
                cand = jl + lax.shift_left(jnp.int32(1), (seq.bit_length() - 1) - step)
                cnt = count(lambda kt, col: (kt == r) & (col < cand))
                return jnp.where(cnt < need, cand, jl)

            jl = lax.fori_loop(0, seq.bit_length(), j_body, jnp.zeros((QB, LANES), I32))
            jlim_ref[...] = jnp.where(r == INT_MIN, -1, jl)

    m_ref[...] = jnp.full(m_ref.shape, NEG_BIG, F32)
    acc_ref[...] = jnp.zeros_like(acc_ref)

    def sel_mask(k0, width):
        parts = []
        for g in range(width // LANES):
            kt = keys_ref[:, pl.ds(pl.multiple_of(k0 + g * LANES, LANES), LANES)]
            col = k0 + g * LANES + lane
            sel = (kt > thr_ref[...]) | ((kt == thr_ref[...]) & (col <= jlim_ref[...]))
            parts.append(jnp.where(sel, 0.0, -jnp.inf))
        return parts[0] if len(parts) == 1 else jnp.concatenate(parts, axis=1)

    n_part = FAR_T // ATT_W
    part_g = ATT_W // LANES

    def logits_phase(k0):
        negm = sel_mask(k0, FAR_T)
        ks = pl.ds(k0, FAR_T)
        for p in range(N_HEADS_B // 2):
            s = _dot_nt(qs_ref[2 * p * QB:(2 * p + 2) * QB, :], kb_ref[0, ks, p * LANES:(p + 1) * LANES])
            s_ref[2 * p] = s[:QB] + negm
            s_ref[2 * p + 1] = s[QB:] + negm

    def softmax_phase(h, bias, bias_scalar):
        def part(c):
            sc = s_ref[h, :, c * ATT_W:(c + 1) * ATT_W]
            return sc if bias is None else sc + bias(c)

        rmax = None
        for c in range(n_part):
            r = jnp.max(part(c), axis=1, keepdims=True)
            rmax = r if rmax is None else jnp.maximum(rmax, r)
        m_old = m_ref[h]
        m_new = jnp.maximum(m_old, rmax + bias_scalar)
        shift = jnp.concatenate([m_new - bias_scalar] * part_g, axis=1)
        for c in range(n_part):
            p_ref[h, :, c * ATT_W:(c + 1) * ATT_W] = jnp.exp2(part(c) - shift).astype(BF16)
        acc_ref[h] = jnp.exp2(m_old - m_new) * acc_ref[h]
        m_ref[h] = m_new

    def pv_phase(k0):
        ks = pl.ds(k0, FAR_T)
        for h in range(N_HEADS_B):
            acc_ref[h] += jnp.dot(p_ref[h], va_ref[0, ks, h * LANES:(h + 1) * LANES], preferred_element_type=F32)

    near0 = (jnp.maximum(i - NEAR_MIN, 0) // FAR_G) * FAR_G

    def far_body(t, c0):
        k0 = pl.multiple_of(t * FAR_T, FAR_T)
        logits_phase(k0)
        for h in range(N_HEADS_B):
            softmax_phase(h, None, rb_ref[FAR_BUCKET, h] * LOG2E)
        pv_phase(k0)
        return c0

    lax.fori_loop(0, near0 // FAR_G, far_body, 0)

    def near_body(t, c0):
        k0 = pl.multiple_of(t * FAR_T, FAR_T)
        logits_phase(k0)
        dist = [jnp.maximum(i - (t * FAR_G + g), 0) for g in range(FAR_G)]
        for h in range(N_HEADS_B):
            def bias(c, h=h):
                return jnp.concatenate([nbias_ref[dist[c * part_g + g] * N_HEADS_B + h] for g in range(part_g)],
                                       axis=1)
            softmax_phase(h, bias, 0.0)
        pv_phase(k0)
        return c0

    lax.fori_loop(near0 // FAR_G, n_tiles, near_body, 0)

    def head_out(h):
        a = acc_ref[h]
        return a * (1.0 / a[:, HEAD_DIM_B:HEAD_DIM_B + 1])

    for p in range(N_HEADS_B // 2):
        o_odd = pltpu.roll(head_out(2 * p + 1), HEAD_DIM_B, axis=1)
        y_ref[0, :, p * LANES:(p + 1) * LANES] = jnp.where(lane < HEAD_DIM_B, head_out(2 * p), o_odd).astype(BF16)


def _dsa(qb, kb, vb, qi, small, rel_bias):
    bsz, s, _ = qb.shape
    assert s % FAR_T == 0
    k_sel = min(TOPK_KEYS_MAX, s // 4)
    kidx = small[:, :, S_KIDX:S_KIDX + IDX_DIM].astype(BF16)
    kidx2 = jnp.concatenate([kidx, kidx], axis=-1)
    v4 = vb.reshape(bsz, s, N_HEADS_B, HEAD_DIM_B)
    va = jnp.concatenate([v4, jnp.ones((bsz, s, N_HEADS_B, 1), BF16),
                          jnp.zeros((bsz, s, N_HEADS_B, LANES - HEAD_DIM_B - 1), BF16)], axis=-1)
    va = va.reshape(bsz, s, N_HEADS_B * LANES)
    tab = jnp.asarray(_near_bucket_table())
    blk = lambda b, i: (b, i, 0)
    full = lambda b, i: (b, 0, 0)
    one = pl.Buffered(1)
    return pl.pallas_call(
        functools.partial(_dsa_kernel, seq=s, k_sel=k_sel),
        grid=(bsz, s // QB),
        in_specs=[
            pl.BlockSpec(memory_space=pltpu.SMEM),
            pl.BlockSpec((1, QB, WIDTH_B), blk),
            pl.BlockSpec((1, QB, IDX_HEADS * IDX_DIM), blk),
            pl.BlockSpec((1, QB, LANES), blk),
            pl.BlockSpec((1, s, WIDTH_B), full, pipeline_mode=one),
            pl.BlockSpec((1, s, N_HEADS_B * LANES), full, pipeline_mode=one),
            pl.BlockSpec((1, s, LANES), full, pipeline_mode=one),
            pl.BlockSpec((NEAR_D, QB, LANES), lambda b, i: (0, 0, 0), pipeline_mode=one),
        ],
        out_specs=pl.BlockSpec((1, QB, WIDTH_B), blk),
        out_shape=jax.ShapeDtypeStruct((bsz, s, WIDTH_B), BF16),
        scratch_shapes=[
            pltpu.VMEM((IDX_HEADS * QB, LANES), BF16),
            pltpu.VMEM((N_HEADS_B * QB, LANES), BF16),
            pltpu.VMEM((IDX_HEADS, QB, LANES), F32),
            pltpu.VMEM((IDX_HEADS * QB, FAR_T), F32),
            pltpu.VMEM((QB, s), I32),
            pltpu.VMEM((NEAR_D * N_HEADS_B, QB, LANES), F32),
            pltpu.VMEM((QB, LANES), I32),
            pltpu.VMEM((QB, LANES), I32),
            pltpu.VMEM((N_HEADS_B, QB, LANES), F32),
            pltpu.VMEM((N_HEADS_B, QB, LANES), F32),
            pltpu.VMEM((N_HEADS_B, QB, FAR_T), F32),
            pltpu.VMEM((N_HEADS_B, QB, FAR_T), BF16),
            pltpu.VMEM((QB, s), I16),
        ],
        compiler_params=_cparams(("parallel", "arbitrary")),
        name="dsa",
    )(rel_bias, qb, qi, small, kb, va, kidx2, tab)


HALF_MASK = 0xFFFF0000


def _pack_halves(t):
    w = t.shape[1] // 2
    bits = pltpu.bitcast(t.astype(BF16).astype(F32), U32)
    return (bits[:, :w] >> 16) | (bits[:, w:] & jnp.uint32(HALF_MASK))


def _unpack_halves(p):
    lo = pltpu.bitcast(p << 16, F32)
    hi = pltpu.bitcast(p & jnp.uint32(HALF_MASK), F32)
    return jnp.concatenate([lo, hi], axis=1)


def _outproj_kernel(ya_ref, yb_ref, x_ref, mod_ref, wo_ref, nw_ref, rw_ref, rbias_ref,
                    xn_ref, hp_ref, ridx_ref, gate_ref):
    wa = ya_ref.shape[2]
    y = (jnp.dot(ya_ref[0], wo_ref[0:wa, :], preferred_element_type=F32)
         + jnp.dot(yb_ref[0], wo_ref[wa:, :], preferred_element_type=F32))
    xn = x_ref[0] + mod_ref[0, 2:3, :] * y
    xn_ref[0] = xn
    ms = jnp.mean(xn * xn, axis=-1, keepdims=True)
    h = xn * lax.rsqrt(ms + EPS) * nw_ref[...] * (1.0 + mod_ref[0, 4:5, :]) + mod_ref[0, 3:4, :]
    hp_ref[0] = _pack_halves(h)

    logits = jnp.dot(h, rw_ref[...], precision=HIGHEST, preferred_element_type=F32) + rbias_ref[...]
    lane = lax.broadcasted_iota(I32, logits.shape, 1)
    cur = logits
    vals, ridx = [], jnp.zeros(logits.shape, I32)
    for k in range(TOP_K):
        mx = jnp.max(cur, axis=1, keepdims=True)
        am = jnp.min(jnp.where(cur == mx, lane, LANES), axis=1, keepdims=True)
        cur = jnp.where(lane == am, -jnp.inf, cur)
        vals.append(mx)
        ridx = jnp.where(lane == k, am, ridx)
    ex = [jnp.exp(v - vals[0]) for v in vals]
    inv = 1.0 / (ex[0] + ex[1] + ex[2] + ex[3])
    gate = jnp.zeros(logits.shape, F32)
    for k in range(TOP_K):
        gate = jnp.where(lane == k, ex[k] * inv, gate)
    ridx_ref[0] = ridx
    gate_ref[0] = gate


def _outproj(y_a, y_b, x, mod_l, w_out_bf, norm_w, router_w, router_b, tm):
    bsz, s, d = x.shape
    n_e = router_w.shape[1]
    rw = jnp.zeros((d, LANES), F32).at[:, :n_e].set(router_w)
    rbias = jnp.full((1, LANES), NEG_BIG, F32).at[0, :n_e].set(router_b)
    blk = lambda b, i: (b, i, 0)
    const2 = lambda b, i: (0, 0)
    return pl.pallas_call(
        _outproj_kernel,
        grid=(bsz, s // tm),
        in_specs=[
            pl.BlockSpec((1, tm, y_a.shape[2]), blk),
            pl.BlockSpec((1, tm, y_b.shape[2]), blk),
            pl.BlockSpec((1, tm, d), blk),
            pl.BlockSpec((1, 6, d), lambda b, i: (b, 0, 0)),
            pl.BlockSpec((d, d), const2),
            pl.BlockSpec((1, d), const2),
            pl.BlockSpec((d, LANES), const2),
            pl.BlockSpec((1, LANES), const2),
        ],
        out_specs=[pl.BlockSpec((1, tm, d), blk), pl.BlockSpec((1, tm, d // 2), blk),
                   pl.BlockSpec((1, tm, LANES), blk), pl.BlockSpec((1, tm, LANES), blk)],
        out_shape=[jax.ShapeDtypeStruct((bsz, s, d), F32), jax.ShapeDtypeStruct((bsz, s, d // 2), U32),
                   jax.ShapeDtypeStruct((bsz, s, LANES), I32), jax.ShapeDtypeStruct((bsz, s, LANES), F32)],
        compiler_params=_cparams(("parallel", "parallel")),
        name="outproj_router",
    )(y_a, y_b, x, mod_l, w_out_bf, norm_w.reshape(1, d), rw, rbias)


MOE_TB = 2048
MOE_RB = 512
MOE_M = 128


def _moe_kernel(cnt_ref, off_ref, list_ref, hp_ref, w1_ref, b1_ref, w2_ref, b2_ref, gate_ref, x_ref, g2_ref,
                o_ref, slots_ref, xg_ref, yb_ref, *, tb, rb):
    sb = pl.program_id(0)
    e = pl.program_id(1)
    dff = w2_ref.shape[1]
    last = TOP_K * tb - 1

    @pl.when(e < N_EXPERTS)
    def _():
        cnt = cnt_ref[sb * N_EXPERTS + e]
        off = off_ref[sb * N_EXPERTS + e]

        def chunk(c, carry):
            base = off + c * MOE_M

            def gather(r, c1):
                code = list_ref[0, 0, jnp.minimum(base + r, last)]
                xg_ref[pl.ds(r, 1), :] = hp_ref[pl.ds(code >> 2, 1), :]
                return c1

            lax.fori_loop(0, MOE_M, gather, 0, unroll=8)
            xb = _unpack_halves(xg_ref[...]).astype(BF16)
            u = jnp.dot(xb, w1_ref[0], preferred_element_type=F32) + b1_ref[0]
            glu = jnp.minimum(u[:, :dff], SWIGLU_LIMIT)
            lin = jnp.clip(u[:, dff:], -SWIGLU_LIMIT, SWIGLU_LIMIT)
            act = glu * _sigmoid(SWIGLU_ALPHA * glu) * (lin + 1.0)
            y = jnp.dot(act.astype(BF16), w2_ref[0], preferred_element_type=F32) + b2_ref[0]
            yb_ref[...] = _pack_halves(y)

            n_valid = cnt - c * MOE_M

            def scatter(r, c1):
                code = jnp.where(r < n_valid, list_ref[0, 0, jnp.minimum(base + r, last)], TOP_K * tb)
                slots_ref[code & (TOP_K - 1), pl.ds(code >> 2, 1), :] = yb_ref[pl.ds(r, 1), :]
                return c1

            lax.fori_loop(0, MOE_M, scatter, 0, unroll=8)
            return carry

        lax.fori_loop(0, (cnt + MOE_M - 1) // MOE_M, chunk, 0)

    @pl.when(e >= N_EXPERTS)
    def _():
        r0 = pl.multiple_of((e - N_EXPERTS) * rb, rb)
        acc = gate_ref[:, 0:1] * _unpack_halves(slots_ref[0, pl.ds(r0, rb), :])
        for k in range(1, TOP_K):
            acc = acc + gate_ref[:, k:k + 1] * _unpack_halves(slots_ref[k, pl.ds(r0, rb), :])
        o_ref[...] = x_ref[...] + g2_ref[0] * acc


def _moe(xn, hp, ridx, gate, g2, w1p, b1p, w2b, b2, tb, rb):
    bsz, s, d = xn.shape
    t = bsz * s
    n_super = t // tb
    n_piece = tb // rb
    dff = w2b.shape[1]
    flat_e = ridx[:, :, :TOP_K].reshape(n_super, tb * TOP_K)
    order = jnp.argsort(flat_e, axis=1, stable=True).astype(I32)
    counts = jnp.sum(flat_e[:, :, None] == jnp.arange(N_EXPERTS, dtype=I32)[None, None, :], axis=1).astype(I32)
    offs = (jnp.cumsum(counts, axis=1) - counts).astype(I32)

    piece = lambda sb, e, *_: (sb * n_piece + jnp.maximum(e - N_EXPERTS, 0), 0)
    wmap = lambda sb, e, *_: (jnp.minimum(e, N_EXPERTS - 1), 0, 0)
    grid_spec = pltpu.PrefetchScalarGridSpec(
        num_scalar_prefetch=2,
        grid=(n_super, N_EXPERTS + n_piece),
        in_specs=[
            pl.BlockSpec((1, 1, tb * TOP_K), lambda sb, e, *_: (sb, 0, 0), memory_space=pltpu.SMEM),
            pl.BlockSpec((tb, d // 2), lambda sb, e, *_: (sb, 0), pipeline_mode=pl.Buffered(1)),
            pl.BlockSpec((1, d, 2 * dff), wmap),
            pl.BlockSpec((1, 1, 2 * dff), wmap),
            pl.BlockSpec((1, dff, d), wmap),
            pl.BlockSpec((1, 1, d), wmap),
            pl.BlockSpec((rb, LANES), piece),
            pl.BlockSpec((rb, d), piece),
            pl.BlockSpec((1, 1, d), lambda sb, e, *_: ((sb * tb) // s, 0, 0)),
        ],
        out_specs=pl.BlockSpec((rb, d), piece),
        scratch_shapes=[
            pltpu.VMEM((TOP_K, tb + SUBLANES, d // 2), U32),
            pltpu.VMEM((MOE_M, d // 2), U32),
            pltpu.VMEM((MOE_M, d // 2), U32),
        ],
    )
    out = pl.pallas_call(
        functools.partial(_moe_kernel, tb=tb, rb=rb),
        grid_spec=grid_spec,
        out_shape=jax.ShapeDtypeStruct((t, d), F32),
        compiler_params=_cparams(("arbitrary", "arbitrary")),
        name="moe",
    )(counts.reshape(-1), offs.reshape(-1), order.reshape(n_super, 1, tb * TOP_K), hp.reshape(t, d // 2),
      w1p, b1p, w2b, b2, gate.reshape(t, LANES), xn.reshape(t, d), g2.reshape(bsz, 1, d))
    return out.reshape(bsz, s, d)


MXU_COLS = 256


def _deinterleave_kernel(w_ref, perm_ref, o_ref):
    half = w_ref.shape[2] // 2
    hw = MXU_COLS // 2
    for b in range(w_ref.shape[2] // MXU_COLS):
        blk = w_ref[0, :, b * MXU_COLS:(b + 1) * MXU_COLS].astype(BF16)
        y = jnp.dot(blk, perm_ref[...], preferred_element_type=F32).astype(BF16)
        o_ref[0, :, b * hw:(b + 1) * hw] = y[:, :hw]
        o_ref[0, :, half + b * hw:half + (b + 1) * hw] = y[:, hw:]


def _deinterleave_cast(w1_l, rows=512):
    n_e, d, two_f = w1_l.shape
    src = np.concatenate([np.arange(0, MXU_COLS, 2), np.arange(1, MXU_COLS, 2)])
    perm = np.zeros((MXU_COLS, MXU_COLS), np.float32)
    perm[src, np.arange(MXU_COLS)] = 1.0
    return pl.pallas_call(
        _deinterleave_kernel,
        grid=(n_e, d // rows),
        in_specs=[pl.BlockSpec((1, rows, two_f), lambda e, r: (e, r, 0)),
                  pl.BlockSpec((MXU_COLS, MXU_COLS), lambda e, r: (0, 0))],
        out_specs=pl.BlockSpec((1, rows, two_f), lambda e, r: (e, r, 0)),
        out_shape=jax.ShapeDtypeStruct((n_e, d, two_f), BF16),
        compiler_params=_cparams(("parallel", "parallel")),
        name="w1_deinterleave",
    )(w1_l, jnp.asarray(perm, BF16))


def _prep_expert_weights(w1_l, b1_l, w2_l, b2_l):
    n_e, d, two_f = w1_l.shape
    b1p = jnp.concatenate([b1_l[:, 0::2], b1_l[:, 1::2]], axis=1).reshape(n_e, 1, two_f)
    return _deinterleave_cast(w1_l), b1p, w2_l.astype(BF16), b2_l.reshape(n_e, 1, -1)


def kernel(x, c, rel_bias, mod_w, mod_b, norm_mix_w, norm_ffn_w, w_in, conv_w, a_log, dt_bias, gdn_norm_w,
           q_norm_w, k_norm_w, w_out, router_w, router_b, w1, b1, w2, b2):
    depth = mod_w.shape[0]
    bsz, s, d = x.shape
    mod = _modulation(c, mod_w, mod_b)
    tm = min(512, s)
    tb = min(MOE_TB, bsz * s)
    rb = min(MOE_RB, tb)
    for l in range(depth):
        qkv_a, z_a, qb, kb, vb, qi, small = _inproj(
            x, mod[l], norm_mix_w[l], _permute_w_in(w_in[l]), q_norm_w[l], k_norm_w[l], tm)
        y_a = _gdn(qkv_a, z_a, small, conv_w[l], a_log[l], dt_bias[l], gdn_norm_w[l], sb=tm)
        y_b = _dsa(qb, kb, vb, qi, small, rel_bias)
        xn, hp, ridx, gate = _outproj(y_a, y_b, x, mod[l], w_out[l].astype(BF16), norm_ffn_w[l],
                                      router_w[l], router_b[l], tm)
        x = _moe(xn, hp, ridx, gate, mod[l][:, 5], *_prep_expert_weights(w1[l], b1[l], w2[l], b2[l]), tb, rb)
    return x
```

```python
import functools
import math

import jax
import jax.numpy as jnp
import numpy as np
from jax import lax
from jax.experimental import pallas as pl
from jax.experimental.pallas import tpu as pltpu

F32 = jnp.float32
BF16 = jnp.bfloat16
I32 = jnp.int32
I16 = jnp.int16
U32 = jnp.uint32
HIGHEST = lax.Precision.HIGHEST

LANES = 128
SUBLANES = 8
VMEM_LIMIT_BYTES = 56 * 1024 * 1024

CHUNK = 64
HEAD_DIM_A = 128
N_HEADS_A = 4
WIDTH_A = N_HEADS_A * HEAD_DIM_A
CONV_K = 4
HEAD_DIM_B = 64
N_HEADS_B = 8
WIDTH_B = N_HEADS_B * HEAD_DIM_B
IDX_HEADS = 8
IDX_DIM = 64
TOPK_KEYS_MAX = 256
REL_BUCKETS = 32
REL_MAX_DIST = 1024
N_EXPERTS = 32
TOP_K = 4
SWIGLU_ALPHA = 1.702
SWIGLU_LIMIT = 7.0
EPS = 1e-6
NEG_BIG = -1e30

C_QKVA = 0
C_Z = C_QKVA + 3 * WIDTH_A
C_QB = C_Z + WIDTH_A
C_KB = C_QB + WIDTH_B
C_VB = C_KB + WIDTH_B
C_QI = C_VB + WIDTH_B
C_SMALL = C_QI + IDX_HEADS * IDX_DIM
D_IN_PAD = C_SMALL + LANES
S_KIDX = 0
S_B = IDX_DIM
S_A = S_B + N_HEADS_A
S_WIDX = S_A + N_HEADS_A


def _cparams(sem):
    return pltpu.CompilerParams(dimension_semantics=sem, vmem_limit_bytes=VMEM_LIMIT_BYTES)


def _silu(x):
    return x * (1.0 / (1.0 + jnp.exp(-x)))


def _sigmoid(x):
    return 1.0 / (1.0 + jnp.exp(-x))


def _softplus(x):
    return jnp.maximum(x, 0.0) + jnp.log(1.0 + jnp.exp(-jnp.abs(x)))


def _mod_kernel(c_ref, w_ref, b_ref, o_ref):
    a = _silu(c_ref[...])
    o_ref[0] = jnp.dot(a, w_ref[0], precision=HIGHEST, preferred_element_type=F32) + b_ref[0]


def _modulation(c, mod_w, mod_b):
    depth, d, n = mod_w.shape
    bsz = c.shape[0]
    rows = -(-bsz // SUBLANES) * SUBLANES
    c_pad = jnp.zeros((rows, d), F32).at[:bsz].set(c)
    tn = 1536
    out = pl.pallas_call(
        _mod_kernel,
        grid=(depth, n // tn),
        in_specs=[
            pl.BlockSpec((rows, d), lambda l, j: (0, 0)),
            pl.BlockSpec((1, d, tn), lambda l, j: (l, 0, j)),
            pl.BlockSpec((1, 1, tn), lambda l, j: (l, 0, j)),
        ],
        out_specs=pl.BlockSpec((1, rows, tn), lambda l, j: (l, 0, j)),
        out_shape=jax.ShapeDtypeStruct((depth, rows, n), F32),
        compiler_params=_cparams(("arbitrary", "arbitrary")),
        name="adaln_mod",
    )(c_pad, mod_w, mod_b.reshape(depth, 1, n))
    return out[:, :bsz].reshape(depth, bsz, 6, d)


def _head_rms(t, group_ref, wn, inv_dim):
    t2 = t * t
    hi = t2.astype(BF16)
    lo = (t2 - hi.astype(F32)).astype(BF16)
    ss = (jnp.dot(hi, group_ref[...], preferred_element_type=F32)
          + jnp.dot(lo, group_ref[...], preferred_element_type=F32))
    return t * lax.rsqrt(ss * inv_dim + EPS) * wn


def _inproj_kernel(x_ref, mod_ref, nw_ref, w_ref, group_ref, qn_ref, kn_ref,
                   qkva_ref, z_ref, qb_ref, kb_ref, vb_ref, qi_ref, small_ref):
    x = x_ref[0]
    ms = jnp.mean(x * x, axis=-1, keepdims=True)
    y = x * lax.rsqrt(ms + EPS) * nw_ref[...]
    h = y * (1.0 + mod_ref[0, 1:2, :]) + mod_ref[0, 0:1, :]
    hb = h.astype(BF16)

    def mm(lo, width):
        return jnp.dot(hb, w_ref[:, lo:lo + width], preferred_element_type=F32)

    qkva_ref[0] = mm(C_QKVA, 3 * WIDTH_A)
    z_ref[0] = mm(C_Z, WIDTH_A)
    q = _head_rms(mm(C_QB, WIDTH_B), group_ref, qn_ref[...], 1.0 / HEAD_DIM_B)
    qb_ref[0] = (q * (HEAD_DIM_B ** -0.5 * LOG2E)).astype(BF16)
    k = _head_rms(mm(C_KB, WIDTH_B), group_ref, kn_ref[...], 1.0 / HEAD_DIM_B)
    kb_ref[0] = k.astype(BF16)
    vb_ref[0] = mm(C_VB, WIDTH_B).astype(BF16)
    qi_ref[0] = mm(C_QI, IDX_HEADS * IDX_DIM).astype(BF16)
    small_ref[0] = mm(C_SMALL, LANES)


def _permute_w_in(w_in_l):
    d = w_in_l.shape[0]
    o = 0
    qkva = w_in_l[:, o:o + 3 * WIDTH_A]; o += 3 * WIDTH_A
    z = w_in_l[:, o:o + WIDTH_A]; o += WIDTH_A
    b = w_in_l[:, o:o + N_HEADS_A]; o += N_HEADS_A
    a = w_in_l[:, o:o + N_HEADS_A]; o += N_HEADS_A
    qkvb = w_in_l[:, o:o + 3 * WIDTH_B]; o += 3 * WIDTH_B
    qi = w_in_l[:, o:o + IDX_HEADS * IDX_DIM]; o += IDX_HEADS * IDX_DIM
    ki = w_in_l[:, o:o + IDX_DIM]; o += IDX_DIM
    wi = w_in_l[:, o:o + IDX_HEADS]; o += IDX_HEADS
    pad = jnp.zeros((d, LANES - IDX_DIM - 2 * N_HEADS_A - IDX_HEADS), w_in_l.dtype)
    return jnp.concatenate([qkva, z, qkvb, qi, ki, b, a, wi, pad], axis=1).astype(BF16)


def _group_ones(width, group):
    g = np.arange(width) // group
    return jnp.asarray((g[:, None] == g[None, :]).astype(np.float32), dtype=BF16)


def _inproj(x, mod_l, norm_w, w_perm, q_norm_w, k_norm_w, tm):
    bsz, s, d = x.shape
    f = lambda b, i: (b, i, 0)
    const2 = lambda b, i: (0, 0)
    outs = [
        (3 * WIDTH_A, F32), (WIDTH_A, F32), (WIDTH_B, BF16), (WIDTH_B, BF16), (WIDTH_B, BF16),
        (IDX_HEADS * IDX_DIM, BF16), (LANES, F32),
    ]
    return pl.pallas_call(
        _inproj_kernel,
        grid=(bsz, s // tm),
        in_specs=[
            pl.BlockSpec((1, tm, d), f),
            pl.BlockSpec((1, 6, d), lambda b, i: (b, 0, 0)),
            pl.BlockSpec((1, d), const2),
            pl.BlockSpec((d, D_IN_PAD), const2),
            pl.BlockSpec((WIDTH_B, WIDTH_B), const2),
            pl.BlockSpec((1, WIDTH_B), const2),
            pl.BlockSpec((1, WIDTH_B), const2),
        ],
        out_specs=[pl.BlockSpec((1, tm, w), f) for w, _ in outs],
        out_shape=[jax.ShapeDtypeStruct((bsz, s, w), dt) for w, dt in outs],
        compiler_params=_cparams(("parallel", "parallel")),
        name="inproj",
    )(x, mod_l, norm_w.reshape(1, d), w_perm, _group_ones(WIDTH_B, HEAD_DIM_B),
      jnp.tile(q_norm_w, N_HEADS_B).reshape(1, WIDTH_B), jnp.tile(k_norm_w, N_HEADS_B).reshape(1, WIDTH_B))


def _dot_nt(a, b, precision=None):
    return lax.dot_general(a, b, (((1,), (1,)), ((), ())), precision=precision, preferred_element_type=F32)


def _mm_bf16(a, b):
    return jnp.dot(a.astype(BF16), b.astype(BF16), preferred_element_type=F32)


def _split2(x):
    hi = x.astype(BF16)
    return hi, (x - hi.astype(F32)).astype(BF16)


def _split3(x):
    hi = x.astype(BF16)
    r = x - hi.astype(F32)
    mid = r.astype(BF16)
    return hi, mid, (r - mid.astype(F32)).astype(BF16)


def _mm3(a, b):
    return (jnp.dot(a[0], b[0], preferred_element_type=F32) + jnp.dot(a[0], b[1], preferred_element_type=F32)
            + jnp.dot(a[1], b[0], preferred_element_type=F32))


GDN_PAR = 4


def _gdn_kernel(qkv_ref, z_ref, small_ref, convw_ref, alog_ref, dtb_ref, nw_ref,
                y_ref, xe_ref, u_ref, state_ref, uval_ref, wdec_ref, qg_ref, kdec_ref, attn_ref, egl_ref, *, sb):
    n_chunks = sb // CHUNK
    halo = SUBLANES

    @pl.when(pl.program_id(1) == 0)
    def _():
        xe_ref[0:halo, :] = jnp.zeros((halo, 3 * WIDTH_A), F32)
        state_ref[...] = jnp.zeros_like(state_ref)

    xe_ref[halo:halo + sb, :] = qkv_ref[0]

    rows = 128
    for g in range(3 * WIDTH_A // LANES):
        cs = slice(g * LANES, (g + 1) * LANES)
        for r in range(sb // rows):
            base = halo - (CONV_K - 1) + r * rows
            acc = xe_ref[base:base + rows, cs] * convw_ref[0:1, cs]
            for j in range(1, CONV_K):
                acc = acc + xe_ref[base + j:base + j + rows, cs] * convw_ref[j:j + 1, cs]
            u_ref[r * rows:(r + 1) * rows, cs] = _silu(acc)

    xe_ref[0:halo, :] = xe_ref[sb:sb + halo, :]

    wide = N_HEADS_A * CHUNK
    heads = range(N_HEADS_A)
    ii = lax.broadcasted_iota(I32, (CHUNK, wide), 0)
    jj = lax.broadcasted_iota(I32, (CHUNK, wide), 1) & (CHUNK - 1)
    eye_w = (ii == jj).astype(F32)
    tri = (lax.broadcasted_iota(I32, (CHUNK, CHUNK), 0)
           >= lax.broadcasted_iota(I32, (CHUNK, CHUNK), 1)).astype(F32).astype(BF16)
    shift = int(math.log2(CHUNK))
    bd_mask = ((lax.broadcasted_iota(I32, (wide, wide), 0) >> shift)
               == (lax.broadcasted_iota(I32, (wide, wide), 1) >> shift)).astype(F32)

    bd_mask = bd_mask.astype(BF16)

    def block_diag(parts):
        return tuple(jnp.concatenate([m] * N_HEADS_A, axis=0) * bd_mask for m in parts)

    def prepare(c):
        rs = pl.ds(pl.multiple_of(c * CHUNK, CHUNK), CHUNK)
        qn, kn, v, beta, g_b = [], [], [], [], []
        for h in heads:
            q = u_ref[rs, h * HEAD_DIM_A:(h + 1) * HEAD_DIM_A]
            k = u_ref[rs, WIDTH_A + h * HEAD_DIM_A:WIDTH_A + (h + 1) * HEAD_DIM_A]
            v.append(u_ref[rs, 2 * WIDTH_A + h * HEAD_DIM_A:2 * WIDTH_A + (h + 1) * HEAD_DIM_A])
            qn.append(q * (lax.rsqrt(jnp.sum(q * q, axis=-1, keepdims=True) + EPS) * (HEAD_DIM_A ** -0.5)))
            kn.append(k * lax.rsqrt(jnp.sum(k * k, axis=-1, keepdims=True) + EPS))
            beta.append(_sigmoid(small_ref[0, rs, S_B + h:S_B + h + 1]))
            g = -jnp.exp(alog_ref[0:1, h:h + 1]) * _softplus(small_ref[0, rs, S_A + h:S_A + h + 1]
                                                            + dtb_ref[0:1, h:h + 1])
            g_b.append(jnp.broadcast_to(g, (CHUNK, CHUNK)))
        gc_w = sum(jnp.dot(tri, part, preferred_element_type=F32) for part in _split3(jnp.concatenate(g_b, axis=1)))
        gc_row = jnp.sum(jnp.where(ii == jj, gc_w, 0.0), axis=0, keepdims=True)
        decay_w = jnp.exp(jnp.where(ii >= jj, gc_w - gc_row, NEG_BIG))
        k_beta = [kn[h] * beta[h] for h in heads]
        kk_w = jnp.concatenate([_dot_nt(k_beta[h].astype(BF16), kn[h].astype(BF16)) for h in heads], axis=1)
        a_w = -jnp.where(ii > jj, kk_w * decay_w, 0.0)
        gc = [gc_w[:, h * CHUNK:h * CHUNK + 1] for h in heads]
        egc = [jnp.exp(gc[h]) for h in heads]
        qk = [_dot_nt(qn[h].astype(BF16), kn[h].astype(BF16)) for h in heads]
        for h in heads:
            g_last = gc[h][CHUNK - 1:CHUNK, :]
            qg_ref[c, h] = (qn[h] * egc[h]).astype(BF16)
            kdec_ref[c, h] = kn[h] * jnp.exp(g_last - gc[h])
            attn_ref[c, h] = (qk[h] * decay_w[:, h * CHUNK:(h + 1) * CHUNK]).astype(BF16)
            egl_ref[c, h] = jnp.broadcast_to(jnp.exp(g_last), (SUBLANES, LANES))
        return a_w, [_split2(v[h] * beta[h]) for h in heads], [_split2(k_beta[h] * egc[h]) for h in heads]

    def solve_body(cg, carry):
        group = range(GDN_PAR)
        chunks = [cg * GDN_PAR + i for i in group]
        pre = [prepare(c) for c in chunks]
        t_w = [eye_w + pre[i][0] for i in group]
        p_parts = [_split2(pre[i][0]) for i in group]
        bd = [block_diag(p_parts[i]) for i in group]
        for _ in range(shift - 1):
            prod = [_mm3(p_parts[i], bd[i]) for i in group]
            p_parts = [_split2(prod[i]) for i in group]
            bd = [block_diag(p_parts[i]) for i in group]
            upd = [_mm3(_split2(t_w[i]), bd[i]) for i in group]
            t_w = [t_w[i] + upd[i] for i in group]
        for i in group:
            t_h = [_split2(t_w[i][:, h * CHUNK:(h + 1) * CHUNK]) for h in heads]
            u_val = [_mm3(t_h[h], pre[i][1][h]) for h in heads]
            w_dec = [_mm3(t_h[h], pre[i][2][h]) for h in heads]
            for h in heads:
                uval_ref[chunks[i], h] = u_val[h]
                wdec_ref[chunks[i], h] = w_dec[h].astype(BF16)
        return carry

    lax.fori_loop(0, n_chunks // GDN_PAR, solve_body, 0)

    def scan_body(c, carry):
        rs = pl.ds(pl.multiple_of(c * CHUNK, CHUNK), CHUNK)
        state = [state_ref[h] for h in heads]
        state_b = [s_h.astype(BF16) for s_h in state]
        w_s = [jnp.dot(wdec_ref[c, h], state_b[h], preferred_element_type=F32) for h in heads]
        v_new = [(uval_ref[c, h] - w_s[h]).astype(BF16) for h in heads]
        o = [jnp.dot(qg_ref[c, h], state_b[h], preferred_element_type=F32)
             + jnp.dot(attn_ref[c, h], v_new[h], preferred_element_type=F32) for h in heads]
        for h in heads:
            state_ref[h] = (state[h] * egl_ref[c, h][0:1, 0:1]
                            + jnp.dot(kdec_ref[c, h].T.astype(BF16), v_new[h], preferred_element_type=F32))
        for h in heads:
            hs = slice(h * HEAD_DIM_A, (h + 1) * HEAD_DIM_A)
            on = o[h] * lax.rsqrt(jnp.mean(o[h] * o[h], axis=-1, keepdims=True) + EPS) * nw_ref[...]
            y_ref[0, rs, hs] = (on * _silu(z_ref[0, rs, hs])).astype(BF16)
        return carry

    lax.fori_loop(0, n_chunks, scan_body, 0)


def _gdn(qkv_a, z_a, small, conv_w, a_log, dt_bias, norm_w, sb):
    bsz, s, _ = qkv_a.shape
    n_c = sb // CHUNK
    per_head = (n_c, N_HEADS_A, CHUNK, HEAD_DIM_A)
    f = lambda b, i: (b, i, 0)
    const2 = lambda b, i: (0, 0)
    return pl.pallas_call(
        functools.partial(_gdn_kernel, sb=sb),
        grid=(bsz, s // sb),
        in_specs=[
            pl.BlockSpec((1, sb, 3 * WIDTH_A), f),
            pl.BlockSpec((1, sb, WIDTH_A), f),
            pl.BlockSpec((1, sb, LANES), f),
            pl.BlockSpec((CONV_K, 3 * WIDTH_A), const2),
            pl.BlockSpec((1, N_HEADS_A), const2),
            pl.BlockSpec((1, N_HEADS_A), const2),
            pl.BlockSpec((1, HEAD_DIM_A), const2),
        ],
        out_specs=pl.BlockSpec((1, sb, WIDTH_A), f),
        out_shape=jax.ShapeDtypeStruct((bsz, s, WIDTH_A), BF16),
        scratch_shapes=[
            pltpu.VMEM((sb + SUBLANES, 3 * WIDTH_A), F32),
            pltpu.VMEM((sb, 3 * WIDTH_A), F32),
            pltpu.VMEM((N_HEADS_A, HEAD_DIM_A, HEAD_DIM_A), F32),
            pltpu.VMEM(per_head, F32),
            pltpu.VMEM(per_head, BF16),
            pltpu.VMEM(per_head, BF16),
            pltpu.VMEM(per_head, F32),
            pltpu.VMEM((n_c, N_HEADS_A, CHUNK, CHUNK), BF16),
            pltpu.VMEM((n_c, N_HEADS_A, SUBLANES, LANES), F32),
        ],
        compiler_params=_cparams(("parallel", "arbitrary")),
        name="gdn",
    )(qkv_a, z_a, small, conv_w, a_log.reshape(1, -1), dt_bias.reshape(1, -1), norm_w.reshape(1, -1))


QB = 128
FAR_T = 512
FAR_G = FAR_T // LANES
ATT_W = 256
LOG2E = 1.4426950408889634
NEAR_D = 9
NEAR_MIN = 5
INT_MIN = -2 ** 31
HALF_MIN = -2 ** 15


def _t5_bucket_np(rel):
    nb = REL_BUCKETS // 2
    max_exact = nb // 2
    side = np.where(rel > 0, nb, 0)
    n = np.abs(rel)
    nf = np.maximum(n, 1).astype(np.float32)
    large = max_exact + (np.log(nf / np.float32(max_exact)) / np.float32(math.log(REL_MAX_DIST / max_exact))
                         * np.float32(nb - max_exact)).astype(np.int32)
    large = np.minimum(large, nb - 1)
    return (side + np.where(n < max_exact, n, large)).astype(np.int32)


def _near_bucket_table():
    r = np.arange(QB)[:, None]
    c = np.arange(LANES)[None, :]
    return np.stack([_t5_bucket_np(c - r - LANES * d) for d in range(NEAR_D)])


FAR_BUCKET = int(_t5_bucket_np(np.array([-(NEAR_MIN * LANES + 1)]))[0])
assert all(int(b) == FAR_BUCKET for b in _t5_bucket_np(-np.arange((NEAR_MIN + 1) * LANES - (QB - 1), 1 << 20, 997)))


def _sortable_key(score):
    bits = pltpu.bitcast(score + 0.0, I32)
    return bits ^ ((bits >> 31) & 0x7FFFFFFF)


def _dsa_kernel(rb_ref, qb_ref, qi_ref, small_ref, kb_ref, va_ref, kidx2_ref, tab_ref,
                y_ref, qis_ref, qs_ref, wb_ref, sdot_ref, keys_ref, nbias_ref, thr_ref, jlim_ref,
                m_ref, acc_ref, s_ref, p_ref, *, seq, k_sel):
    i = pl.program_id(1)
    lane = lax.broadcasted_iota(I32, (QB, LANES), 1)
    row = lax.broadcasted_iota(I32, (QB, LANES), 0)
    even_f = (lane < HEAD_DIM_B).astype(F32)
    even_b = even_f.astype(BF16)
    odd_b = (1.0 - even_f).astype(BF16)

    @pl.when(i == 0)
    def _():
        nbias_ref[...] = jnp.zeros_like(nbias_ref)

        def d_body(d, c0):
            tab = tab_ref[d]

            def b_body(bk, c1):
                hit = tab == bk
                for h in range(N_HEADS_B):
                    nbias_ref[d * N_HEADS_B + h] = jnp.where(hit, rb_ref[bk, h] * LOG2E,
                                                             nbias_ref[d * N_HEADS_B + h])
                return c1

            return lax.fori_loop(0, REL_BUCKETS, b_body, c0)

        lax.fori_loop(0, NEAR_D, d_body, 0)

    for p in range(N_HEADS_B // 2):
        ps = slice(p * LANES, (p + 1) * LANES)
        qi_pair = qi_ref[0, :, ps]
        qis_ref[(2 * p) * QB:(2 * p + 1) * QB, :] = qi_pair * even_b
        qis_ref[(2 * p + 1) * QB:(2 * p + 2) * QB, :] = qi_pair * odd_b
        q_pair = qb_ref[0, :, ps]
        qs_ref[(2 * p) * QB:(2 * p + 1) * QB, :] = q_pair * even_b
        qs_ref[(2 * p + 1) * QB:(2 * p + 2) * QB, :] = q_pair * odd_b
    w_scale = IDX_HEADS ** -0.5 * IDX_DIM ** -0.5
    for h in range(IDX_HEADS):
        wb_ref[h] = jnp.broadcast_to(small_ref[0, :, S_WIDX + h:S_WIDX + h + 1] * w_scale, (QB, LANES))

    limit = i * QB + CHUNK + jnp.where(row >= CHUNK, CHUNK, 0)

    def score_body(t, c0):
        k0 = pl.multiple_of(t * FAR_T, FAR_T)
        sdot_ref[...] = _dot_nt(qis_ref[...], kidx2_ref[0, pl.ds(k0, FAR_T), :])
        for g in range(FAR_G):
            gs = slice(g * LANES, (g + 1) * LANES)
            acc = jnp.maximum(sdot_ref[0:QB, gs], 0.0) * wb_ref[0]
            for h in range(1, IDX_HEADS):
                acc = acc + jnp.maximum(sdot_ref[h * QB:(h + 1) * QB, gs], 0.0) * wb_ref[h]
            col = k0 + g * LANES + lane
            cs = pl.ds(pl.multiple_of(k0 + g * LANES, LANES), LANES)
            keys_ref[:, cs] = jnp.where(col < limit, _sortable_key(acc), INT_MIN)
        return c0

    n_groups = i + 1
    n_tiles = i // FAR_G + 1
    lax.fori_loop(0, n_tiles, score_body, 0)

    @pl.when(n_tiles % 2 == 1)
    def _():
        keys_ref[:, pl.ds(pl.multiple_of(n_tiles * FAR_T, FAR_T), FAR_T)] = jnp.full((QB, FAR_T), INT_MIN, I32)

    def count(pred):
        def t_body(t, acc):
            for g in range(2 * FAR_G):
                c0 = pl.multiple_of(t * (2 * FAR_T) + g * LANES, LANES)
                kt = keys_ref[:, pl.ds(c0, LANES)]
                acc = acc + jnp.where(pred(kt, c0 + lane), 1, 0)
            return acc

        acc = lax.fori_loop(0, (n_tiles + 1) // 2, t_body, jnp.zeros((QB, LANES), I32))
        return jnp.broadcast_to(jnp.sum(acc, axis=1, keepdims=True), (QB, LANES))

    thr_ref[...] = jnp.full((QB, LANES), INT_MIN, I32)
    jlim_ref[...] = jnp.full((QB, LANES), -1, I32)

    @pl.when(n_groups * QB > k_sel)
    def _():
        def bit_body(step, r):
            cand = r + lax.shift_left(jnp.int32(1), 31 - step)
            cnt = count(lambda kt, col: kt >= cand)
            return jnp.where(cnt >= k_sel, cand, r)

        r = lax.fori_loop(0, 32, bit_body, jnp.full((QB, LANES), INT_MIN, I32))
        thr_ref[...] = r
        need = k_sel - count(lambda kt, col: kt > r)
        excess = count(lambda kt, col: kt == r) - need
        jlim_ref[...] = jnp.where(r == INT_MIN, -1, seq)

        @pl.when(jnp.max(excess) > 0)
        def _():
            def j_body(step, jl):
                cand = jl + lax.shift_left(jnp.int32(1), (seq.bit_length() - 1) - step)
                cnt = count(lambda kt, col: (kt == r) & (col < cand))
                return jnp.where(cnt < need, cand, jl)

            jl = lax.fori_loop(0, seq.bit_length(), j_body, jnp.zeros((QB, LANES), I32))
            jlim_ref[...] = jnp.where(r == INT_MIN, -1, jl)

    m_ref[...] = jnp.full(m_ref.shape, NEG_BIG, F32)
    acc_ref[...] = jnp.zeros_like(acc_ref)

    def sel_mask(k0, width):
        parts = []
        for g in range(width // LANES):
            kt = keys_ref[:, pl.ds(pl.multiple_of(k0 + g * LANES, LANES), LANES)]
            col = k0 + g * LANES + lane
            sel = (kt > thr_ref[...]) | ((kt == thr_ref[...]) & (col <= jlim_ref[...]))
            parts.append(jnp.where(sel, 0.0, -jnp.inf))
        return parts[0] if len(parts) == 1 else jnp.concatenate(parts, axis=1)

    n_part = FAR_T // ATT_W
    part_g = ATT_W // LANES

    def logits_phase(k0):
        negm = sel_mask(k0, FAR_T)
        ks = pl.ds(k0, FAR_T)
        for p in range(N_HEADS_B // 2):
            s = _dot_nt(qs_ref[2 * p * QB:(2 * p + 2) * QB, :], kb_ref[0, ks, p * LANES:(p + 1) * LANES])
            s_ref[2 * p] = s[:QB] + negm
            s_ref[2 * p + 1] = s[QB:] + negm

    def softmax_phase(h, bias, bias_scalar):
        def part(c):
            sc = s_ref[h, :, c * ATT_W:(c + 1) * ATT_W]
            return sc if bias is None else sc + bias(c)

        rmax = None
        for c in range(n_part):
            r = jnp.max(part(c), axis=1, keepdims=True)
            rmax = r if rmax is None else jnp.maximum(rmax, r)
        m_old = m_ref[h]
        m_new = jnp.maximum(m_old, rmax + bias_scalar)
        shift = jnp.concatenate([m_new - bias_scalar] * part_g, axis=1)
        for c in range(n_part):
            p_ref[h, :, c * ATT_W:(c + 1) * ATT_W] = jnp.exp2(part(c) - shift).astype(BF16)
        acc_ref[h] = jnp.exp2(m_old - m_new) * acc_ref[h]
        m_ref[h] = m_new

    def pv_phase(k0):
        ks = pl.ds(k0, FAR_T)
        for h in range(N_HEADS_B):
            acc_ref[h] += jnp.dot(p_ref[h], va_ref[0, ks, h * LANES:(h + 1) * LANES], preferred_element_type=F32)

    near0 = (jnp.maximum(i - NEAR_MIN, 0) // FAR_G) * FAR_G

    def far_body(t, c0):
        k0 = pl.multiple_of(t * FAR_T, FAR_T)
        logits_phase(k0)
        for h in range(N_HEADS_B):
            softmax_phase(h, None, rb_ref[FAR_BUCKET, h] * LOG2E)
        pv_phase(k0)
        return c0

    lax.fori_loop(0, near0 // FAR_G, far_body, 0)

    def near_body(t, c0):
        k0 = pl.multiple_of(t * FAR_T, FAR_T)
        logits_phase(k0)
        dist = [jnp.maximum(i - (t * FAR_G + g), 0) for g in range(FAR_G)]
        for h in range(N_HEADS_B):
            def bias(c, h=h):
                return jnp.concatenate([nbias_ref[dist[c * part_g + g] * N_HEADS_B + h] for g in range(part_g)],
                                       axis=1)
            softmax_phase(h, bias, 0.0)
        pv_phase(k0)
        return c0

    lax.fori_loop(near0 // FAR_G, n_tiles, near_body, 0)

    def head_out(h):
        a = acc_ref[h]
        return a * (1.0 / a[:, HEAD_DIM_B:HEAD_DIM_B + 1])

    for p in range(N_HEADS_B // 2):
        o_odd = pltpu.roll(head_out(2 * p + 1), HEAD_DIM_B, axis=1)
        y_ref[0, :, p * LANES:(p + 1) * LANES] = jnp.where(lane < HEAD_DIM_B, head_out(2 * p), o_odd).astype(BF16)


def _dsa(qb, kb, vb, qi, small, rel_bias):
    bsz, s, _ = qb.shape
    assert s % FAR_T == 0
    k_sel = min(TOPK_KEYS_MAX, s // 4)
    kidx = small[:, :, S_KIDX:S_KIDX + IDX_DIM].astype(BF16)
    kidx2 = jnp.concatenate([kidx, kidx], axis=-1)
    v4 = vb.reshape(bsz, s, N_HEADS_B, HEAD_DIM_B)
    va = jnp.concatenate([v4, jnp.ones((bsz, s, N_HEADS_B, 1), BF16),
                          jnp.zeros((bsz, s, N_HEADS_B, LANES - HEAD_DIM_B - 1), BF16)], axis=-1)
    va = va.reshape(bsz, s, N_HEADS_B * LANES)
    tab = jnp.asarray(_near_bucket_table())
    blk = lambda b, i: (b, i, 0)
    full = lambda b, i: (b, 0, 0)
    one = pl.Buffered(1)
    return pl.pallas_call(
        functools.partial(_dsa_kernel, seq=s, k_sel=k_sel),
        grid=(bsz, s // QB),
        in_specs=[
            pl.BlockSpec(memory_space=pltpu.SMEM),
            pl.BlockSpec((1, QB, WIDTH_B), blk),
            pl.BlockSpec((1, QB, IDX_HEADS * IDX_DIM), blk),
            pl.BlockSpec((1, QB, LANES), blk),
            pl.BlockSpec((1, s, WIDTH_B), full, pipeline_mode=one),
            pl.BlockSpec((1, s, N_HEADS_B * LANES), full, pipeline_mode=one),
            pl.BlockSpec((1, s, LANES), full, pipeline_mode=one),
            pl.BlockSpec((NEAR_D, QB, LANES), lambda b, i: (0, 0, 0), pipeline_mode=one),
        ],
        out_specs=pl.BlockSpec((1, QB, WIDTH_B), blk),
        out_shape=jax.ShapeDtypeStruct((bsz, s, WIDTH_B), BF16),
        scratch_shapes=[
            pltpu.VMEM((IDX_HEADS * QB, LANES), BF16),
            pltpu.VMEM((N_HEADS_B * QB, LANES), BF16),
            pltpu.VMEM((IDX_HEADS, QB, LANES), F32),
            pltpu.VMEM((IDX_HEADS * QB, FAR_T), F32),
            pltpu.VMEM((QB, s), I32),
            pltpu.VMEM((NEAR_D * N_HEADS_B, QB, LANES), F32),
            pltpu.VMEM((QB, LANES), I32),
            pltpu.VMEM((QB, LANES), I32),
            pltpu.VMEM((N_HEADS_B, QB, LANES), F32),
            pltpu.VMEM((N_HEADS_B, QB, LANES), F32),
            pltpu.VMEM((N_HEADS_B, QB, FAR_T), F32),
            pltpu.VMEM((N_HEADS_B, QB, FAR_T), BF16),
        ],
        compiler_params=_cparams(("parallel", "arbitrary")),
        name="dsa",
    )(rel_bias, qb, qi, small, kb, va, kidx2, tab)


HALF_MASK = 0xFFFF0000


def _pack_halves(t):
    w = t.shape[1] // 2
    bits = pltpu.bitcast(t.astype(BF16).astype(F32), U32)
    return (bits[:, :w] >> 16) | (bits[:, w:] & jnp.uint32(HALF_MASK))


def _unpack_halves(p):
    lo = pltpu.bitcast(p << 16, F32)
    hi = pltpu.bitcast(p & jnp.uint32(HALF_MASK), F32)
    return jnp.concatenate([lo, hi], axis=1)


def _outproj_kernel(ya_ref, yb_ref, x_ref, mod_ref, wo_ref, nw_ref, rw_ref, rbias_ref,
                    xn_ref, hp_ref, ridx_ref, gate_ref):
    wa = ya_ref.shape[2]
    y = (jnp.dot(ya_ref[0], wo_ref[0:wa, :], preferred_element_type=F32)
         + jnp.dot(yb_ref[0], wo_ref[wa:, :], preferred_element_type=F32))
    xn = x_ref[0] + mod_ref[0, 2:3, :] * y
    xn_ref[0] = xn
    ms = jnp.mean(xn * xn, axis=-1, keepdims=True)
    h = xn * lax.rsqrt(ms + EPS) * nw_ref[...] * (1.0 + mod_ref[0, 4:5, :]) + mod_ref[0, 3:4, :]
    hp_ref[0] = _pack_halves(h)

    logits = jnp.dot(h, rw_ref[...], precision=HIGHEST, preferred_element_type=F32) + rbias_ref[...]
    lane = lax.broadcasted_iota(I32, logits.shape, 1)
    cur = logits
    vals, ridx = [], jnp.zeros(logits.shape, I32)
    for k in range(TOP_K):
        mx = jnp.max(cur, axis=1, keepdims=True)
        am = jnp.min(jnp.where(cur == mx, lane, LANES), axis=1, keepdims=True)
        cur = jnp.where(lane == am, -jnp.inf, cur)
        vals.append(mx)
        ridx = jnp.where(lane == k, am, ridx)
    ex = [jnp.exp(v - vals[0]) for v in vals]
    inv = 1.0 / (ex[0] + ex[1] + ex[2] + ex[3])
    gate = jnp.zeros(logits.shape, F32)
    for k in range(TOP_K):
        gate = jnp.where(lane == k, ex[k] * inv, gate)
    ridx_ref[0] = ridx
    gate_ref[0] = gate


def _outproj(y_a, y_b, x, mod_l, w_out_bf, norm_w, router_w, router_b, tm):
    bsz, s, d = x.shape
    n_e = router_w.shape[1]
    rw = jnp.zeros((d, LANES), F32).at[:, :n_e].set(router_w)
    rbias = jnp.full((1, LANES), NEG_BIG, F32).at[0, :n_e].set(router_b)
    blk = lambda b, i: (b, i, 0)
    const2 = lambda b, i: (0, 0)
    return pl.pallas_call(
        _outproj_kernel,
        grid=(bsz, s // tm),
        in_specs=[
            pl.BlockSpec((1, tm, y_a.shape[2]), blk),
            pl.BlockSpec((1, tm, y_b.shape[2]), blk),
            pl.BlockSpec((1, tm, d), blk),
            pl.BlockSpec((1, 6, d), lambda b, i: (b, 0, 0)),
            pl.BlockSpec((d, d), const2),
            pl.BlockSpec((1, d), const2),
            pl.BlockSpec((d, LANES), const2),
            pl.BlockSpec((1, LANES), const2),
        ],
        out_specs=[pl.BlockSpec((1, tm, d), blk), pl.BlockSpec((1, tm, d // 2), blk),
                   pl.BlockSpec((1, tm, LANES), blk), pl.BlockSpec((1, tm, LANES), blk)],
        out_shape=[jax.ShapeDtypeStruct((bsz, s, d), F32), jax.ShapeDtypeStruct((bsz, s, d // 2), U32),
                   jax.ShapeDtypeStruct((bsz, s, LANES), I32), jax.ShapeDtypeStruct((bsz, s, LANES), F32)],
        compiler_params=_cparams(("parallel", "parallel")),
        name="outproj_router",
    )(y_a, y_b, x, mod_l, w_out_bf, norm_w.reshape(1, d), rw, rbias)


MOE_TB = 2048
MOE_RB = 512
MOE_M = 144


def _moe_kernel(first_ref, nch_ref, cbase_ref, cvalid_ref, list_ref, hp_ref, w1_ref, b1_ref, w2_ref, b2_ref,
                gate_ref, x_ref, g2_ref, o_ref, slots_ref, xg_ref, yb_ref, *, tb, rb, table_len):
    sb = pl.program_id(0)
    e = pl.program_id(1)
    dff = w2_ref.shape[1]
    table = sb * table_len + 1

    def gather(j):
        base = cbase_ref[table + j]
        buf = (j + 2) % 2
        for r in range(MOE_M):
            code = list_ref[0, 0, base + r]
            xg_ref[buf, pl.ds(r, 1), :] = hp_ref[pl.ds(code >> 2, 1), :]

    def scatter(j):
        base = cbase_ref[table + j]
        n_valid = cvalid_ref[table + j]
        buf = (j + 2) % 2
        for r in range(MOE_M):
            code = jnp.where(r < n_valid, list_ref[0, 0, base + r], TOP_K * tb)
            slots_ref[code & (TOP_K - 1), pl.ds(code >> 2, 1), :] = yb_ref[buf, pl.ds(r, 1), :]

    @pl.when(e == 0)
    def _():
        gather(0)

    @pl.when(e < N_EXPERTS)
    def _():
        j0 = first_ref[sb * N_EXPERTS + e]

        def chunk(j, carry):
            buf = j % 2
            xb = _unpack_halves(xg_ref[buf]).astype(BF16)
            gather(j + 1)
            scatter(j - 1)
            u = jnp.dot(xb, w1_ref[0], preferred_element_type=F32) + b1_ref[0]
            glu = jnp.minimum(u[:, :dff], SWIGLU_LIMIT)
            lin = jnp.clip(u[:, dff:], -SWIGLU_LIMIT, SWIGLU_LIMIT)
            act = glu * _sigmoid(SWIGLU_ALPHA * glu) * (lin + 1.0)
            y = jnp.dot(act.astype(BF16), w2_ref[0], preferred_element_type=F32) + b2_ref[0]
            yb_ref[buf] = _pack_halves(y)
            return carry

        lax.fori_loop(j0, j0 + nch_ref[sb * N_EXPERTS + e], chunk, 0)

    @pl.when(e == N_EXPERTS)
    def _():
        n_total = first_ref[sb * N_EXPERTS + N_EXPERTS - 1] + nch_ref[sb * N_EXPERTS + N_EXPERTS - 1]
        scatter(n_total - 1)

    @pl.when(e >= N_EXPERTS)
    def _():
        r0 = pl.multiple_of((e - N_EXPERTS) * rb, rb)
        acc = gate_ref[:, 0:1] * _unpack_halves(slots_ref[0, pl.ds(r0, rb), :])
        for k in range(1, TOP_K):
            acc = acc + gate_ref[:, k:k + 1] * _unpack_halves(slots_ref[k, pl.ds(r0, rb), :])
        o_ref[...] = x_ref[...] + g2_ref[0] * acc


def _moe(xn, hp, ridx, gate, g2, w1p, b1p, w2b, b2, tb, rb, layer=0):
    bsz, s, d = xn.shape
    t = bsz * s
    n_super = t // tb
    n_piece = tb // rb
    dff = w2b.shape[1]
    w2map = lambda sb, e, *_: (layer * N_EXPERTS + jnp.minimum(e, N_EXPERTS - 1), 0, 0)
    flat_e = ridx[:, :, :TOP_K].reshape(n_super, tb * TOP_K)
    order = jnp.argsort(flat_e, axis=1, stable=True).astype(I32)
    counts = jnp.sum(flat_e[:, :, None] == jnp.arange(N_EXPERTS, dtype=I32)[None, None, :], axis=1).astype(I32)
    offs = (jnp.cumsum(counts, axis=1) - counts).astype(I32)
    nch = (counts + MOE_M - 1) // MOE_M
    first = (jnp.cumsum(nch, axis=1) - nch).astype(I32)
    table_len = -(-(tb * TOP_K // MOE_M + N_EXPERTS + 2) // SUBLANES) * SUBLANES
    j = jnp.arange(table_len - 1, dtype=I32)[None, :]
    owner = jnp.sum(j[:, :, None] >= (first + nch)[:, None, :], axis=2)
    owner_c = jnp.minimum(owner, N_EXPERTS - 1)
    within = j - jnp.take_along_axis(first, owner_c, axis=1)
    c_valid = jnp.where(owner < N_EXPERTS,
                        jnp.minimum(jnp.take_along_axis(counts, owner_c, axis=1) - within * MOE_M, MOE_M), 0)
    c_base = jnp.where(owner < N_EXPERTS, jnp.take_along_axis(offs, owner_c, axis=1) + within * MOE_M, 0)
    zero = jnp.zeros((n_super, 1), I32)
    c_valid = jnp.concatenate([zero, c_valid.astype(I32)], axis=1)
    c_base = jnp.concatenate([zero, c_base.astype(I32)], axis=1)

    piece = lambda sb, e, *_: (sb * n_piece + jnp.maximum(e - N_EXPERTS, 0), 0)
    wmap = lambda sb, e, *_: (jnp.minimum(e, N_EXPERTS - 1), 0, 0)
    grid_spec = pltpu.PrefetchScalarGridSpec(
        num_scalar_prefetch=4,
        grid=(n_super, N_EXPERTS + n_piece),
        in_specs=[
            pl.BlockSpec((1, 1, tb * TOP_K + MOE_M), lambda sb, e, *_: (sb, 0, 0), memory_space=pltpu.SMEM),
            pl.BlockSpec((tb, d // 2), lambda sb, e, *_: (sb, 0), pipeline_mode=pl.Buffered(1)),
            pl.BlockSpec((1, d, 2 * dff), wmap),
            pl.BlockSpec((1, 1, 2 * dff), wmap),
            pl.BlockSpec((1, dff, d), w2map),
            pl.BlockSpec((1, 1, d), wmap),
            pl.BlockSpec((rb, LANES), piece),
            pl.BlockSpec((rb, d), piece),
            pl.BlockSpec((1, 1, d), lambda sb, e, *_: ((sb * tb) // s, 0, 0)),
        ],
        out_specs=pl.BlockSpec((rb, d), piece),
        scratch_shapes=[
            pltpu.VMEM((TOP_K, tb + SUBLANES, d // 2), U32),
            pltpu.VMEM((2, MOE_M, d // 2), U32),
            pltpu.VMEM((2, MOE_M, d // 2), U32),
        ],
    )
    out = pl.pallas_call(
        functools.partial(_moe_kernel, tb=tb, rb=rb, table_len=table_len),
        grid_spec=grid_spec,
        out_shape=jax.ShapeDtypeStruct((t, d), F32),
        compiler_params=_cparams(("arbitrary", "arbitrary")),
        name="moe",
    )(first.reshape(-1), nch.astype(I32).reshape(-1), c_base.reshape(-1), c_valid.reshape(-1),
      jnp.pad(order, ((0, 0), (0, MOE_M))).reshape(n_super, 1, tb * TOP_K + MOE_M), hp.reshape(t, d // 2),
      w1p, b1p, w2b, b2, gate.reshape(t, LANES), xn.reshape(t, d), g2.reshape(bsz, 1, d))
    return out.reshape(bsz, s, d)


MXU_COLS = 256


def _deinterleave_kernel(w_ref, perm_ref, o_ref):
    half = w_ref.shape[2] // 2
    hw = MXU_COLS // 2
    for b in range(w_ref.shape[2] // MXU_COLS):
        blk = w_ref[0, :, b * MXU_COLS:(b + 1) * MXU_COLS].astype(BF16)
        y = jnp.dot(blk, perm_ref[...], preferred_element_type=F32).astype(BF16)
        o_ref[0, :, b * hw:(b + 1) * hw] = y[:, :hw]
        o_ref[0, :, half + b * hw:half + (b + 1) * hw] = y[:, hw:]


def _deinterleave_cast(w1, layer, rows=512):
    depth, n_e, d, two_f = w1.shape
    src = np.concatenate([np.arange(0, MXU_COLS, 2), np.arange(1, MXU_COLS, 2)])
    perm = np.zeros((MXU_COLS, MXU_COLS), np.float32)
    perm[src, np.arange(MXU_COLS)] = 1.0
    return pl.pallas_call(
        _deinterleave_kernel,
        grid=(n_e, d // rows),
        in_specs=[pl.BlockSpec((1, rows, two_f), lambda e, r: (layer * n_e + e, r, 0)),
                  pl.BlockSpec((MXU_COLS, MXU_COLS), lambda e, r: (0, 0))],
        out_specs=pl.BlockSpec((1, rows, two_f), lambda e, r: (e, r, 0)),
        out_shape=jax.ShapeDtypeStruct((n_e, d, two_f), BF16),
        compiler_params=_cparams(("parallel", "parallel")),
        name="w1_deinterleave",
    )(w1.reshape(depth * n_e, d, two_f), jnp.asarray(perm, BF16))


def _deinterleave_bias(b1_l):
    n_e, two_f = b1_l.shape
    return jnp.concatenate([b1_l[:, 0::2], b1_l[:, 1::2]], axis=1).reshape(n_e, 1, two_f)


def kernel(x, c, rel_bias, mod_w, mod_b, norm_mix_w, norm_ffn_w, w_in, conv_w, a_log, dt_bias, gdn_norm_w,
           q_norm_w, k_norm_w, w_out, router_w, router_b, w1, b1, w2, b2):
    depth = mod_w.shape[0]
    bsz, s, d = x.shape
    mod = _modulation(c, mod_w, mod_b)
    tm = min(512, s)
    tb = min(MOE_TB, bsz * s)
    rb = min(MOE_RB, tb)
    n_e, dff = w2.shape[1], w2.shape[2]
    w2b = w2.astype(BF16).reshape(depth * n_e, dff, d)
    for l in range(depth):
        qkv_a, z_a, qb, kb, vb, qi, small = _inproj(
            x, mod[l], norm_mix_w[l], _permute_w_in(w_in[l]), q_norm_w[l], k_norm_w[l], tm)
        y_a = _gdn(qkv_a, z_a, small, conv_w[l], a_log[l], dt_bias[l], gdn_norm_w[l], sb=tm)
        y_b = _dsa(qb, kb, vb, qi, small, rel_bias)
        xn, hp, ridx, gate = _outproj(y_a, y_b, x, mod[l], w_out[l].astype(BF16), norm_ffn_w[l],
                                      router_w[l], router_b[l], tm)
        x = _moe(xn, hp, ridx, gate, mod[l][:, 5], _deinterleave_cast(w1, l), _deinterleave_bias(b1[l]), w2b,
                 b2[l].reshape(n_e, 1, d), tb, rb, layer=l)
    return x
```

```python
import functools
import math

import jax
import jax.numpy as jnp
import numpy as np
from jax import lax
from jax.experimental import pallas as pl
from jax.experimental.pallas import tpu as pltpu

F32 = jnp.float32
BF16 = jnp.bfloat16
I32 = jnp.int32
I16 = jnp.int16
U32 = jnp.uint32
HIGHEST = lax.Precision.HIGHEST

LANES = 128
SUBLANES = 8
VMEM_LIMIT_BYTES = 56 * 1024 * 1024

CHUNK = 64
HEAD_DIM_A = 128
N_HEADS_A = 4
WIDTH_A = N_HEADS_A * HEAD_DIM_A
CONV_K = 4
HEAD_DIM_B = 64
N_HEADS_B = 8
WIDTH_B = N_HEADS_B * HEAD_DIM_B
IDX_HEADS = 8
IDX_DIM = 64
TOPK_KEYS_MAX = 256
REL_BUCKETS = 32
REL_MAX_DIST = 1024
N_EXPERTS = 32
TOP_K = 4
SWIGLU_ALPHA = 1.702
SWIGLU_LIMIT = 7.0
EPS = 1e-6
NEG_BIG = -1e30

C_QKVA = 0
C_Z = C_QKVA + 3 * WIDTH_A
C_QB = C_Z + WIDTH_A
C_KB = C_QB + WIDTH_B
C_VB = C_KB + WIDTH_B
C_QI = C_VB + WIDTH_B
C_SMALL = C_QI + IDX_HEADS * IDX_DIM
D_IN_PAD = C_SMALL + LANES
S_KIDX = 0
S_B = IDX_DIM
S_A = S_B + N_HEADS_A
S_WIDX = S_A + N_HEADS_A


def _cparams(sem):
    return pltpu.CompilerParams(dimension_semantics=sem, vmem_limit_bytes=VMEM_LIMIT_BYTES)


def _silu(x):
    return x * (1.0 / (1.0 + jnp.exp(-x)))


def _sigmoid(x):
    return 1.0 / (1.0 + jnp.exp(-x))


def _softplus(x):
    return jnp.maximum(x, 0.0) + jnp.log(1.0 + jnp.exp(-jnp.abs(x)))


def _mod_kernel(c_ref, w_ref, b_ref, o_ref):
    a = _silu(c_ref[...])
    o_ref[0] = jnp.dot(a, w_ref[0], precision=HIGHEST, preferred_element_type=F32) + b_ref[0]


def _modulation(c, mod_w, mod_b):
    depth, d, n = mod_w.shape
    bsz = c.shape[0]
    rows = -(-bsz // SUBLANES) * SUBLANES
    c_pad = jnp.zeros((rows, d), F32).at[:bsz].set(c)
    tn = 1536
    out = pl.pallas_call(
        _mod_kernel,
        grid=(depth, n // tn),
        in_specs=[
            pl.BlockSpec((rows, d), lambda l, j: (0, 0)),
            pl.BlockSpec((1, d, tn), lambda l, j: (l, 0, j)),
            pl.BlockSpec((1, 1, tn), lambda l, j: (l, 0, j)),
        ],
        out_specs=pl.BlockSpec((1, rows, tn), lambda l, j: (l, 0, j)),
        out_shape=jax.ShapeDtypeStruct((depth, rows, n), F32),
        compiler_params=_cparams(("arbitrary", "arbitrary")),
        name="adaln_mod",
    )(c_pad, mod_w, mod_b.reshape(depth, 1, n))
    return out[:, :bsz].reshape(depth, bsz, 6, d)


def _head_rms(t, group_ref, wn, inv_dim):
    t2 = t * t
    hi = t2.astype(BF16)
    lo = (t2 - hi.astype(F32)).astype(BF16)
    ss = (jnp.dot(hi, group_ref[...], preferred_element_type=F32)
          + jnp.dot(lo, group_ref[...], preferred_element_type=F32))
    return t * lax.rsqrt(ss * inv_dim + EPS) * wn


def _inproj_kernel(x_ref, mod_ref, nw_ref, w_ref, group_ref, qn_ref, kn_ref,
                   qkva_ref, z_ref, qb_ref, kb_ref, vb_ref, qi_ref, small_ref):
    x = x_ref[0]
    ms = jnp.mean(x * x, axis=-1, keepdims=True)
    y = x * lax.rsqrt(ms + EPS) * nw_ref[...]
    h = y * (1.0 + mod_ref[0, 1:2, :]) + mod_ref[0, 0:1, :]
    hb = h.astype(BF16)

    def mm(lo, width):
        return jnp.dot(hb, w_ref[:, lo:lo + width], preferred_element_type=F32)

    qkva_ref[0] = mm(C_QKVA, 3 * WIDTH_A)
    z_ref[0] = mm(C_Z, WIDTH_A)
    q = _head_rms(mm(C_QB, WIDTH_B), group_ref, qn_ref[...], 1.0 / HEAD_DIM_B)
    qb_ref[0] = (q * (HEAD_DIM_B ** -0.5 * LOG2E)).astype(BF16)
    k = _head_rms(mm(C_KB, WIDTH_B), group_ref, kn_ref[...], 1.0 / HEAD_DIM_B)
    kb_ref[0] = k.astype(BF16)
    vb_ref[0] = mm(C_VB, WIDTH_B).astype(BF16)
    qi_ref[0] = mm(C_QI, IDX_HEADS * IDX_DIM).astype(BF16)
    small_ref[0] = mm(C_SMALL, LANES)


def _permute_w_in(w_in_l):
    d = w_in_l.shape[0]
    o = 0
    qkva = w_in_l[:, o:o + 3 * WIDTH_A]; o += 3 * WIDTH_A
    z = w_in_l[:, o:o + WIDTH_A]; o += WIDTH_A
    b = w_in_l[:, o:o + N_HEADS_A]; o += N_HEADS_A
    a = w_in_l[:, o:o + N_HEADS_A]; o += N_HEADS_A
    qkvb = w_in_l[:, o:o + 3 * WIDTH_B]; o += 3 * WIDTH_B
    qi = w_in_l[:, o:o + IDX_HEADS * IDX_DIM]; o += IDX_HEADS * IDX_DIM
    ki = w_in_l[:, o:o + IDX_DIM]; o += IDX_DIM
    wi = w_in_l[:, o:o + IDX_HEADS]; o += IDX_HEADS
    pad = jnp.zeros((d, LANES - IDX_DIM - 2 * N_HEADS_A - IDX_HEADS), w_in_l.dtype)
    return jnp.concatenate([qkva, z, qkvb, qi, ki, b, a, wi, pad], axis=1).astype(BF16)


def _group_ones(width, group):
    g = np.arange(width) // group
    return jnp.asarray((g[:, None] == g[None, :]).astype(np.float32), dtype=BF16)


def _inproj(x, mod_l, norm_w, w_perm, q_norm_w, k_norm_w, tm):
    bsz, s, d = x.shape
    f = lambda b, i: (b, i, 0)
    const2 = lambda b, i: (0, 0)
    outs = [
        (3 * WIDTH_A, F32), (WIDTH_A, F32), (WIDTH_B, BF16), (WIDTH_B, BF16), (WIDTH_B, BF16),
        (IDX_HEADS * IDX_DIM, BF16), (LANES, F32),
    ]
    return pl.pallas_call(
        _inproj_kernel,
        grid=(bsz, s // tm),
        in_specs=[
            pl.BlockSpec((1, tm, d), f),
            pl.BlockSpec((1, 6, d), lambda b, i: (b, 0, 0)),
            pl.BlockSpec((1, d), const2),
            pl.BlockSpec((d, D_IN_PAD), const2),
            pl.BlockSpec((WIDTH_B, WIDTH_B), const2),
            pl.BlockSpec((1, WIDTH_B), const2),
            pl.BlockSpec((1, WIDTH_B), const2),
        ],
        out_specs=[pl.BlockSpec((1, tm, w), f) for w, _ in outs],
        out_shape=[jax.ShapeDtypeStruct((bsz, s, w), dt) for w, dt in outs],
        compiler_params=_cparams(("parallel", "parallel")),
        name="inproj",
    )(x, mod_l, norm_w.reshape(1, d), w_perm, _group_ones(WIDTH_B, HEAD_DIM_B),
      jnp.tile(q_norm_w, N_HEADS_B).reshape(1, WIDTH_B), jnp.tile(k_norm_w, N_HEADS_B).reshape(1, WIDTH_B))


def _dot_nt(a, b, precision=None):
    return lax.dot_general(a, b, (((1,), (1,)), ((), ())), precision=precision, preferred_element_type=F32)


def _mm_bf16(a, b):
    return jnp.dot(a.astype(BF16), b.astype(BF16), preferred_element_type=F32)


def _split2(x):
    hi = x.astype(BF16)
    return hi, (x - hi.astype(F32)).astype(BF16)


def _split3(x):
    hi = x.astype(BF16)
    r = x - hi.astype(F32)
    mid = r.astype(BF16)
    return hi, mid, (r - mid.astype(F32)).astype(BF16)


def _mm3(a, b):
    return (jnp.dot(a[0], b[0], preferred_element_type=F32) + jnp.dot(a[0], b[1], preferred_element_type=F32)
            + jnp.dot(a[1], b[0], preferred_element_type=F32))


GDN_PAR = 4


def _gdn_kernel(qkv_ref, z_ref, small_ref, convw_ref, alog_ref, dtb_ref, nw_ref,
                y_ref, xe_ref, u_ref, state_ref, uval_ref, wdec_ref, qg_ref, kdec_ref, attn_ref, egl_ref, *, sb):
    n_chunks = sb // CHUNK
    halo = SUBLANES

    @pl.when(pl.program_id(1) == 0)
    def _():
        xe_ref[0:halo, :] = jnp.zeros((halo, 3 * WIDTH_A), F32)
        state_ref[...] = jnp.zeros_like(state_ref)

    xe_ref[halo:halo + sb, :] = qkv_ref[0]

    rows = 128
    for g in range(3 * WIDTH_A // LANES):
        cs = slice(g * LANES, (g + 1) * LANES)
        for r in range(sb // rows):
            base = halo - (CONV_K - 1) + r * rows
            acc = xe_ref[base:base + rows, cs] * convw_ref[0:1, cs]
            for j in range(1, CONV_K):
                acc = acc + xe_ref[base + j:base + j + rows, cs] * convw_ref[j:j + 1, cs]
            u_ref[r * rows:(r + 1) * rows, cs] = _silu(acc)

    xe_ref[0:halo, :] = xe_ref[sb:sb + halo, :]

    wide = N_HEADS_A * CHUNK
    heads = range(N_HEADS_A)
    ii = lax.broadcasted_iota(I32, (CHUNK, wide), 0)
    jj = lax.broadcasted_iota(I32, (CHUNK, wide), 1) & (CHUNK - 1)
    eye_w = (ii == jj).astype(F32)
    tri = (lax.broadcasted_iota(I32, (CHUNK, CHUNK), 0)
           >= lax.broadcasted_iota(I32, (CHUNK, CHUNK), 1)).astype(F32).astype(BF16)
    shift = int(math.log2(CHUNK))
    bd_mask = ((lax.broadcasted_iota(I32, (wide, wide), 0) >> shift)
               == (lax.broadcasted_iota(I32, (wide, wide), 1) >> shift)).astype(F32)

    bd_mask = bd_mask.astype(BF16)

    def block_diag(parts):
        return tuple(jnp.concatenate([m] * N_HEADS_A, axis=0) * bd_mask for m in parts)

    def prepare(c):
        rs = pl.ds(pl.multiple_of(c * CHUNK, CHUNK), CHUNK)
        qn, kn, v, beta, g_b = [], [], [], [], []
        for h in heads:
            q = u_ref[rs, h * HEAD_DIM_A:(h + 1) * HEAD_DIM_A]
            k = u_ref[rs, WIDTH_A + h * HEAD_DIM_A:WIDTH_A + (h + 1) * HEAD_DIM_A]
            v.append(u_ref[rs, 2 * WIDTH_A + h * HEAD_DIM_A:2 * WIDTH_A + (h + 1) * HEAD_DIM_A])
            qn.append(q * (lax.rsqrt(jnp.sum(q * q, axis=-1, keepdims=True) + EPS) * (HEAD_DIM_A ** -0.5)))
            kn.append(k * lax.rsqrt(jnp.sum(k * k, axis=-1, keepdims=True) + EPS))
            beta.append(_sigmoid(small_ref[0, rs, S_B + h:S_B + h + 1]))
            g = -jnp.exp(alog_ref[0:1, h:h + 1]) * _softplus(small_ref[0, rs, S_A + h:S_A + h + 1]
                                                            + dtb_ref[0:1, h:h + 1])
            g_b.append(jnp.broadcast_to(g, (CHUNK, CHUNK)))
        gc_w = sum(jnp.dot(tri, part, preferred_element_type=F32) for part in _split3(jnp.concatenate(g_b, axis=1)))
        gc_row = jnp.sum(jnp.where(ii == jj, gc_w, 0.0), axis=0, keepdims=True)
        decay_w = jnp.exp(jnp.where(ii >= jj, gc_w - gc_row, NEG_BIG))
        k_beta = [kn[h] * beta[h] for h in heads]
        kk_w = jnp.concatenate([_dot_nt(k_beta[h].astype(BF16), kn[h].astype(BF16)) for h in heads], axis=1)
        a_w = -jnp.where(ii > jj, kk_w * decay_w, 0.0)
        gc = [gc_w[:, h * CHUNK:h * CHUNK + 1] for h in heads]
        egc = [jnp.exp(gc[h]) for h in heads]
        qk = [_dot_nt(qn[h].astype(BF16), kn[h].astype(BF16)) for h in heads]
        for h in heads:
            g_last = gc[h][CHUNK - 1:CHUNK, :]
            qg_ref[c, h] = (qn[h] * egc[h]).astype(BF16)
            kdec_ref[c, h] = kn[h] * jnp.exp(g_last - gc[h])
            attn_ref[c, h] = (qk[h] * decay_w[:, h * CHUNK:(h + 1) * CHUNK]).astype(BF16)
            egl_ref[c, h] = jnp.broadcast_to(jnp.exp(g_last), (SUBLANES, LANES))
        return a_w, [_split2(v[h] * beta[h]) for h in heads], [_split2(k_beta[h] * egc[h]) for h in heads]

    def solve_body(cg, carry):
        group = range(GDN_PAR)
        chunks = [cg * GDN_PAR + i for i in group]
        pre = [prepare(c) for c in chunks]
        t_w = [eye_w + pre[i][0] for i in group]
        p_parts = [_split2(pre[i][0]) for i in group]
        bd = [block_diag(p_parts[i]) for i in group]
        for _ in range(shift - 1):
            prod = [_mm3(p_parts[i], bd[i]) for i in group]
            p_parts = [_split2(prod[i]) for i in group]
            bd = [block_diag(p_parts[i]) for i in group]
            upd = [_mm3(_split2(t_w[i]), bd[i]) for i in group]
            t_w = [t_w[i] + upd[i] for i in group]
        for i in group:
            t_h = [_split2(t_w[i][:, h * CHUNK:(h + 1) * CHUNK]) for h in heads]
            u_val = [_mm3(t_h[h], pre[i][1][h]) for h in heads]
            w_dec = [_mm3(t_h[h], pre[i][2][h]) for h in heads]
            for h in heads:
                uval_ref[chunks[i], h] = u_val[h]
                wdec_ref[chunks[i], h] = w_dec[h].astype(BF16)
        return carry

    lax.fori_loop(0, n_chunks // GDN_PAR, solve_body, 0)

    def scan_body(c, carry):
        rs = pl.ds(pl.multiple_of(c * CHUNK, CHUNK), CHUNK)
        state = [state_ref[h] for h in heads]
        state_b = [s_h.astype(BF16) for s_h in state]
        w_s = [jnp.dot(wdec_ref[c, h], state_b[h], preferred_element_type=F32) for h in heads]
        v_new = [(uval_ref[c, h] - w_s[h]).astype(BF16) for h in heads]
        o = [jnp.dot(qg_ref[c, h], state_b[h], preferred_element_type=F32)
             + jnp.dot(attn_ref[c, h], v_new[h], preferred_element_type=F32) for h in heads]
        for h in heads:
            state_ref[h] = (state[h] * egl_ref[c, h][0:1, 0:1]
                            + jnp.dot(kdec_ref[c, h].T.astype(BF16), v_new[h], preferred_element_type=F32))
        for h in heads:
            hs = slice(h * HEAD_DIM_A, (h + 1) * HEAD_DIM_A)
            on = o[h] * lax.rsqrt(jnp.mean(o[h] * o[h], axis=-1, keepdims=True) + EPS) * nw_ref[...]
            y_ref[0, rs, hs] = (on * _silu(z_ref[0, rs, hs])).astype(BF16)
        return carry

    lax.fori_loop(0, n_chunks, scan_body, 0)


def _gdn(qkv_a, z_a, small, conv_w, a_log, dt_bias, norm_w, sb):
    bsz, s, _ = qkv_a.shape
    n_c = sb // CHUNK
    per_head = (n_c, N_HEADS_A, CHUNK, HEAD_DIM_A)
    f = lambda b, i: (b, i, 0)
    const2 = lambda b, i: (0, 0)
    return pl.pallas_call(
        functools.partial(_gdn_kernel, sb=sb),
        grid=(bsz, s // sb),
        in_specs=[
            pl.BlockSpec((1, sb, 3 * WIDTH_A), f),
            pl.BlockSpec((1, sb, WIDTH_A), f),
            pl.BlockSpec((1, sb, LANES), f),
            pl.BlockSpec((CONV_K, 3 * WIDTH_A), const2),
            pl.BlockSpec((1, N_HEADS_A), const2),
            pl.BlockSpec((1, N_HEADS_A), const2),
            pl.BlockSpec((1, HEAD_DIM_A), const2),
        ],
        out_specs=pl.BlockSpec((1, sb, WIDTH_A), f),
        out_shape=jax.ShapeDtypeStruct((bsz, s, WIDTH_A), BF16),
        scratch_shapes=[
            pltpu.VMEM((sb + SUBLANES, 3 * WIDTH_A), F32),
            pltpu.VMEM((sb, 3 * WIDTH_A), F32),
            pltpu.VMEM((N_HEADS_A, HEAD_DIM_A, HEAD_DIM_A), F32),
            pltpu.VMEM(per_head, F32),
            pltpu.VMEM(per_head, BF16),
            pltpu.VMEM(per_head, BF16),
            pltpu.VMEM(per_head, F32),
            pltpu.VMEM((n_c, N_HEADS_A, CHUNK, CHUNK), BF16),
            pltpu.VMEM((n_c, N_HEADS_A, SUBLANES, LANES), F32),
        ],
        compiler_params=_cparams(("parallel", "arbitrary")),
        name="gdn",
    )(qkv_a, z_a, small, conv_w, a_log.reshape(1, -1), dt_bias.reshape(1, -1), norm_w.reshape(1, -1))


QB = 128
FAR_T = 512
FAR_G = FAR_T // LANES
ATT_W = 256
LOG2E = 1.4426950408889634
NEAR_D = 9
NEAR_MIN = 5
INT_MIN = -2 ** 31
HALF_MIN = -2 ** 15


def _t5_bucket_np(rel):
    nb = REL_BUCKETS // 2
    max_exact = nb // 2
    side = np.where(rel > 0, nb, 0)
    n = np.abs(rel)
    nf = np.maximum(n, 1).astype(np.float32)
    large = max_exact + (np.log(nf / np.float32(max_exact)) / np.float32(math.log(REL_MAX_DIST / max_exact))
                         * np.float32(nb - max_exact)).astype(np.int32)
    large = np.minimum(large, nb - 1)
    return (side + np.where(n < max_exact, n, large)).astype(np.int32)


def _near_bucket_table():
    r = np.arange(QB)[:, None]
    c = np.arange(LANES)[None, :]
    return np.stack([_t5_bucket_np(c - r - LANES * d) for d in range(NEAR_D)])


FAR_BUCKET = int(_t5_bucket_np(np.array([-(NEAR_MIN * LANES + 1)]))[0])
assert all(int(b) == FAR_BUCKET for b in _t5_bucket_np(-np.arange((NEAR_MIN + 1) * LANES - (QB - 1), 1 << 20, 997)))


def _sortable_key(score):
    bits = pltpu.bitcast(score + 0.0, I32)
    return bits ^ ((bits >> 31) & 0x7FFFFFFF)


def _dsa_kernel(rb_ref, qb_ref, qi_ref, small_ref, kb_ref, va_ref, kidx2_ref, tab_ref,
                y_ref, qis_ref, qs_ref, wb_ref, sdot_ref, keys_ref, nbias_ref, thr_ref, jlim_ref,
                m_ref, acc_ref, s_ref, p_ref, *, seq, k_sel):
    i = pl.program_id(1)
    lane = lax.broadcasted_iota(I32, (QB, LANES), 1)
    row = lax.broadcasted_iota(I32, (QB, LANES), 0)
    even_f = (lane < HEAD_DIM_B).astype(F32)
    even_b = even_f.astype(BF16)
    odd_b = (1.0 - even_f).astype(BF16)

    @pl.when(i == 0)
    def _():
        nbias_ref[...] = jnp.zeros_like(nbias_ref)

        def d_body(d, c0):
            tab = tab_ref[d]

            def b_body(bk, c1):
                hit = tab == bk
                for h in range(N_HEADS_B):
                    nbias_ref[d * N_HEADS_B + h] = jnp.where(hit, rb_ref[bk, h] * LOG2E,
                                                             nbias_ref[d * N_HEADS_B + h])
                return c1

            return lax.fori_loop(0, REL_BUCKETS, b_body, c0)

        lax.fori_loop(0, NEAR_D, d_body, 0)

    for p in range(N_HEADS_B // 2):
        ps = slice(p * LANES, (p + 1) * LANES)
        qi_pair = qi_ref[0, :, ps]
        qis_ref[(2 * p) * QB:(2 * p + 1) * QB, :] = qi_pair * even_b
        qis_ref[(2 * p + 1) * QB:(2 * p + 2) * QB, :] = qi_pair * odd_b
        q_pair = qb_ref[0, :, ps]
        qs_ref[(2 * p) * QB:(2 * p + 1) * QB, :] = q_pair * even_b
        qs_ref[(2 * p + 1) * QB:(2 * p + 2) * QB, :] = q_pair * odd_b
    w_scale = IDX_HEADS ** -0.5 * IDX_DIM ** -0.5
    for h in range(IDX_HEADS):
        wb_ref[h] = jnp.broadcast_to(small_ref[0, :, S_WIDX + h:S_WIDX + h + 1] * w_scale, (QB, LANES))

    limit = i * QB + CHUNK + jnp.where(row >= CHUNK, CHUNK, 0)

    def score_body(t, c0):
        k0 = pl.multiple_of(t * FAR_T, FAR_T)
        sdot = _dot_nt(qis_ref[...], kidx2_ref[0, pl.ds(k0, FAR_T), :])
        for g in range(FAR_G):
            sdot_ref[g] = sdot[:, g * LANES:(g + 1) * LANES]
        for g in range(FAR_G):
            acc = jnp.maximum(sdot_ref[g, 0:QB, :], 0.0) * wb_ref[0]
            for h in range(1, IDX_HEADS):
                acc = acc + jnp.maximum(sdot_ref[g, h * QB:(h + 1) * QB, :], 0.0) * wb_ref[h]
            col = k0 + g * LANES + lane
            keys_ref[t * FAR_G + g] = jnp.where(col < limit, _sortable_key(acc), INT_MIN)
        return c0

    n_groups = i + 1
    n_tiles = i // FAR_G + 1
    lax.fori_loop(0, n_tiles, score_body, 0)

    @pl.when(n_tiles % 2 == 1)
    def _():
        for g in range(FAR_G):
            keys_ref[n_tiles * FAR_G + g] = jnp.full((QB, LANES), INT_MIN, I32)

    def count(pred):
        def t_body(t, acc):
            for g in range(2 * FAR_G):
                grp = t * (2 * FAR_G) + g
                acc = acc + jnp.where(pred(keys_ref[grp], grp * LANES + lane), 1, 0)
            return acc

        acc = lax.fori_loop(0, (n_tiles + 1) // 2, t_body, jnp.zeros((QB, LANES), I32))
        return jnp.broadcast_to(jnp.sum(acc, axis=1, keepdims=True), (QB, LANES))

    thr_ref[...] = jnp.full((QB, LANES), INT_MIN, I32)
    jlim_ref[...] = jnp.full((QB, LANES), -1, I32)

    @pl.when(n_groups * QB > k_sel)
    def _():
        def bit_body(step, r):
            cand = r + lax.shift_left(jnp.int32(1), 31 - step)
            cnt = count(lambda kt, col: kt >= cand)
            return jnp.where(cnt >= k_sel, cand, r)

        r = lax.fori_loop(0, 32, bit_body, jnp.full((QB, LANES), INT_MIN, I32))
        thr_ref[...] = r
        need = k_sel - count(lambda kt, col: kt > r)
        excess = count(lambda kt, col: kt == r) - need
        jlim_ref[...] = jnp.where(r == INT_MIN, -1, seq)

        @pl.when(jnp.max(excess) > 0)
        def _():
            def j_body(step, jl):
                cand = jl + lax.shift_left(jnp.int32(1), (seq.bit_length() - 1) - step)
                cnt = count(lambda kt, col: (kt == r) & (col < cand))
                return jnp.where(cnt < need, cand, jl)

            jl = lax.fori_loop(0, seq.bit_length(), j_body, jnp.zeros((QB, LANES), I32))
            jlim_ref[...] = jnp.where(r == INT_MIN, -1, jl)

    m_ref[...] = jnp.full(m_ref.shape, NEG_BIG, F32)
    acc_ref[...] = jnp.zeros_like(acc_ref)

    def sel_mask(k0, width):
        parts = []
        for g in range(width // LANES):
            kt = keys_ref[k0 // LANES + g]
            col = k0 + g * LANES + lane
            sel = (kt > thr_ref[...]) | ((kt == thr_ref[...]) & (col <= jlim_ref[...]))
            parts.append(jnp.where(sel, 0.0, -jnp.inf))
        return parts

    n_part = FAR_T // ATT_W
    part_g = ATT_W // LANES

    def logits_phase(k0):
        negm = sel_mask(k0, FAR_T)
        ks = pl.ds(k0, FAR_T)
        for p in range(N_HEADS_B // 2):
            s = _dot_nt(qs_ref[2 * p * QB:(2 * p + 2) * QB, :], kb_ref[0, ks, p * LANES:(p + 1) * LANES])
            for g in range(FAR_G):
                gs = slice(g * LANES, (g + 1) * LANES)
                s_ref[2 * p, g] = s[:QB, gs] + negm[g]
                s_ref[2 * p + 1, g] = s[QB:, gs] + negm[g]

    def softmax_phase(h, bias, bias_scalar):
        def part(c):
            sc = jnp.concatenate([s_ref[h, c * part_g + g] for g in range(part_g)], axis=1)
            return sc if bias is None else sc + bias(c)

        rmax = None
        for c in range(n_part):
            r = jnp.max(part(c), axis=1, keepdims=True)
            rmax = r if rmax is None else jnp.maximum(rmax, r)
        m_old = m_ref[h]
        m_new = jnp.maximum(m_old, rmax + bias_scalar)
        shift = jnp.concatenate([m_new - bias_scalar] * part_g, axis=1)
        for c in range(n_part):
            p_ref[h, :, c * ATT_W:(c + 1) * ATT_W] = jnp.exp2(part(c) - shift).astype(BF16)
        acc_ref[h] = jnp.exp2(m_old - m_new) * acc_ref[h]
        m_ref[h] = m_new

    def pv_phase(k0):
        ks = pl.ds(k0, FAR_T)
        for h in range(N_HEADS_B):
            acc_ref[h] += jnp.dot(p_ref[h], va_ref[0, ks, h * LANES:(h + 1) * LANES], preferred_element_type=F32)

    near0 = (jnp.maximum(i - NEAR_MIN, 0) // FAR_G) * FAR_G

    def far_body(t, c0):
        k0 = pl.multiple_of(t * FAR_T, FAR_T)
        logits_phase(k0)
        for h in range(N_HEADS_B):
            softmax_phase(h, None, rb_ref[FAR_BUCKET, h] * LOG2E)
        pv_phase(k0)
        return c0

    lax.fori_loop(0, near0 // FAR_G, far_body, 0)

    def near_body(t, c0):
        k0 = pl.multiple_of(t * FAR_T, FAR_T)
        logits_phase(k0)
        dist = [jnp.maximum(i - (t * FAR_G + g), 0) for g in range(FAR_G)]
        for h in range(N_HEADS_B):
            def bias(c, h=h):
                return jnp.concatenate([nbias_ref[dist[c * part_g + g] * N_HEADS_B + h] for g in range(part_g)],
                                       axis=1)
            softmax_phase(h, bias, 0.0)
        pv_phase(k0)
        return c0

    lax.fori_loop(near0 // FAR_G, n_tiles, near_body, 0)

    def head_out(h):
        a = acc_ref[h]
        return a * (1.0 / a[:, HEAD_DIM_B:HEAD_DIM_B + 1])

    for p in range(N_HEADS_B // 2):
        o_odd = pltpu.roll(head_out(2 * p + 1), HEAD_DIM_B, axis=1)
        y_ref[0, :, p * LANES:(p + 1) * LANES] = jnp.where(lane < HEAD_DIM_B, head_out(2 * p), o_odd).astype(BF16)


def _dsa(qb, kb, vb, qi, small, rel_bias):
    bsz, s, _ = qb.shape
    assert s % FAR_T == 0
    k_sel = min(TOPK_KEYS_MAX, s // 4)
    kidx = small[:, :, S_KIDX:S_KIDX + IDX_DIM].astype(BF16)
    kidx2 = jnp.concatenate([kidx, kidx], axis=-1)
    v4 = vb.reshape(bsz, s, N_HEADS_B, HEAD_DIM_B)
    va = jnp.concatenate([v4, jnp.ones((bsz, s, N_HEADS_B, 1), BF16),
                          jnp.zeros((bsz, s, N_HEADS_B, LANES - HEAD_DIM_B - 1), BF16)], axis=-1)
    va = va.reshape(bsz, s, N_HEADS_B * LANES)
    tab = jnp.asarray(_near_bucket_table())
    blk = lambda b, i: (b, i, 0)
    full = lambda b, i: (b, 0, 0)
    one = pl.Buffered(1)
    return pl.pallas_call(
        functools.partial(_dsa_kernel, seq=s, k_sel=k_sel),
        grid=(bsz, s // QB),
        in_specs=[
            pl.BlockSpec(memory_space=pltpu.SMEM),
            pl.BlockSpec((1, QB, WIDTH_B), blk),
            pl.BlockSpec((1, QB, IDX_HEADS * IDX_DIM), blk),
            pl.BlockSpec((1, QB, LANES), blk),
            pl.BlockSpec((1, s, WIDTH_B), full, pipeline_mode=one),
            pl.BlockSpec((1, s, N_HEADS_B * LANES), full, pipeline_mode=one),
            pl.BlockSpec((1, s, LANES), full, pipeline_mode=one),
            pl.BlockSpec((NEAR_D, QB, LANES), lambda b, i: (0, 0, 0), pipeline_mode=one),
        ],
        out_specs=pl.BlockSpec((1, QB, WIDTH_B), blk),
        out_shape=jax.ShapeDtypeStruct((bsz, s, WIDTH_B), BF16),
        scratch_shapes=[
            pltpu.VMEM((IDX_HEADS * QB, LANES), BF16),
            pltpu.VMEM((N_HEADS_B * QB, LANES), BF16),
            pltpu.VMEM((IDX_HEADS, QB, LANES), F32),
            pltpu.VMEM((FAR_G, IDX_HEADS * QB, LANES), F32),
            pltpu.VMEM((s // LANES, QB, LANES), I32),
            pltpu.VMEM((NEAR_D * N_HEADS_B, QB, LANES), F32),
            pltpu.VMEM((QB, LANES), I32),
            pltpu.VMEM((QB, LANES), I32),
            pltpu.VMEM((N_HEADS_B, QB, LANES), F32),
            pltpu.VMEM((N_HEADS_B, QB, LANES), F32),
            pltpu.VMEM((N_HEADS_B, FAR_G, QB, LANES), F32),
            pltpu.VMEM((N_HEADS_B, QB, FAR_T), BF16),
        ],
        compiler_params=_cparams(("parallel", "arbitrary")),
        name="dsa",
    )(rel_bias, qb, qi, small, kb, va, kidx2, tab)


HALF_MASK = 0xFFFF0000


def _pack_halves(t):
    w = t.shape[1] // 2
    bits = pltpu.bitcast(t.astype(BF16).astype(F32), U32)
    return (bits[:, :w] >> 16) | (bits[:, w:] & jnp.uint32(HALF_MASK))


def _unpack_halves(p):
    lo = pltpu.bitcast(p << 16, F32)
    hi = pltpu.bitcast(p & jnp.uint32(HALF_MASK), F32)
    return jnp.concatenate([lo, hi], axis=1)


def _outproj_kernel(ya_ref, yb_ref, x_ref, mod_ref, wo_ref, nw_ref, rw_ref, rbias_ref,
                    xn_ref, hp_ref, ridx_ref, gate_ref):
    wa = ya_ref.shape[2]
    y = (jnp.dot(ya_ref[0], wo_ref[0:wa, :], preferred_element_type=F32)
         + jnp.dot(yb_ref[0], wo_ref[wa:, :], preferred_element_type=F32))
    xn = x_ref[0] + mod_ref[0, 2:3, :] * y
    xn_ref[0] = xn
    ms = jnp.mean(xn * xn, axis=-1, keepdims=True)
    h = xn * lax.rsqrt(ms + EPS) * nw_ref[...] * (1.0 + mod_ref[0, 4:5, :]) + mod_ref[0, 3:4, :]
    hp_ref[0] = _pack_halves(h)

    logits = jnp.dot(h, rw_ref[...], precision=HIGHEST, preferred_element_type=F32) + rbias_ref[...]
    lane = lax.broadcasted_iota(I32, logits.shape, 1)
    cur = logits
    vals, ridx = [], jnp.zeros(logits.shape, I32)
    for k in range(TOP_K):
        mx = jnp.max(cur, axis=1, keepdims=True)
        am = jnp.min(jnp.where(cur == mx, lane, LANES), axis=1, keepdims=True)
        cur = jnp.where(lane == am, -jnp.inf, cur)
        vals.append(mx)
        ridx = jnp.where(lane == k, am, ridx)
    ex = [jnp.exp(v - vals[0]) for v in vals]
    inv = 1.0 / (ex[0] + ex[1] + ex[2] + ex[3])
    gate = jnp.zeros(logits.shape, F32)
    for k in range(TOP_K):
        gate = jnp.where(lane == k, ex[k] * inv, gate)
    ridx_ref[0] = ridx
    gate_ref[0] = gate


def _outproj(y_a, y_b, x, mod_l, w_out_bf, norm_w, router_w, router_b, tm):
    bsz, s, d = x.shape
    n_e = router_w.shape[1]
    rw = jnp.zeros((d, LANES), F32).at[:, :n_e].set(router_w)
    rbias = jnp.full((1, LANES), NEG_BIG, F32).at[0, :n_e].set(router_b)
    blk = lambda b, i: (b, i, 0)
    const2 = lambda b, i: (0, 0)
    return pl.pallas_call(
        _outproj_kernel,
        grid=(bsz, s // tm),
        in_specs=[
            pl.BlockSpec((1, tm, y_a.shape[2]), blk),
            pl.BlockSpec((1, tm, y_b.shape[2]), blk),
            pl.BlockSpec((1, tm, d), blk),
            pl.BlockSpec((1, 6, d), lambda b, i: (b, 0, 0)),
            pl.BlockSpec((d, d), const2),
            pl.BlockSpec((1, d), const2),
            pl.BlockSpec((d, LANES), const2),
            pl.BlockSpec((1, LANES), const2),
        ],
        out_specs=[pl.BlockSpec((1, tm, d), blk), pl.BlockSpec((1, tm, d // 2), blk),
                   pl.BlockSpec((1, tm, LANES), blk), pl.BlockSpec((1, tm, LANES), blk)],
        out_shape=[jax.ShapeDtypeStruct((bsz, s, d), F32), jax.ShapeDtypeStruct((bsz, s, d // 2), U32),
                   jax.ShapeDtypeStruct((bsz, s, LANES), I32), jax.ShapeDtypeStruct((bsz, s, LANES), F32)],
        compiler_params=_cparams(("parallel", "parallel")),
        name="outproj_router",
    )(y_a, y_b, x, mod_l, w_out_bf, norm_w.reshape(1, d), rw, rbias)


MOE_TB = 2048
MOE_RB = 512
MOE_M = 144


def _moe_kernel(first_ref, nch_ref, cbase_ref, cvalid_ref, list_ref, hp_ref, w1_ref, b1_ref, w2_ref, b2_ref,
                gate_ref, x_ref, g2_ref, o_ref, slots_ref, xg_ref, yb_ref, *, tb, rb, table_len):
    sb = pl.program_id(0)
    e = pl.program_id(1)
    dff = w2_ref.shape[1]
    table = sb * table_len + 1

    def gather(j):
        base = cbase_ref[table + j]
        buf = (j + 2) % 2
        for r in range(MOE_M):
            code = list_ref[0, 0, base + r]
            xg_ref[buf, pl.ds(r, 1), :] = hp_ref[pl.ds(code >> 2, 1), :]

    def scatter(j):
        base = cbase_ref[table + j]
        n_valid = cvalid_ref[table + j]
        buf = (j + 2) % 2
        for r in range(MOE_M):
            code = jnp.where(r < n_valid, list_ref[0, 0, base + r], TOP_K * tb)
            slots_ref[code & (TOP_K - 1), pl.ds(code >> 2, 1), :] = yb_ref[buf, pl.ds(r, 1), :]

    @pl.when(e == 0)
    def _():
        gather(0)

    @pl.when(e < N_EXPERTS)
    def _():
        j0 = first_ref[sb * N_EXPERTS + e]

        def chunk(j, carry):
            buf = j % 2
            xb = _unpack_halves(xg_ref[buf]).astype(BF16)
            gather(j + 1)
            scatter(j - 1)
            u = jnp.dot(xb, w1_ref[0], preferred_element_type=F32) + b1_ref[0]
            glu = jnp.minimum(u[:, :dff], SWIGLU_LIMIT)
            lin = jnp.clip(u[:, dff:], -SWIGLU_LIMIT, SWIGLU_LIMIT)
            act = glu * _sigmoid(SWIGLU_ALPHA * glu) * (lin + 1.0)
            y = jnp.dot(act.astype(BF16), w2_ref[0], preferred_element_type=F32) + b2_ref[0]
            yb_ref[buf] = _pack_halves(y)
            return carry

        lax.fori_loop(j0, j0 + nch_ref[sb * N_EXPERTS + e], chunk, 0)

    @pl.when(e == N_EXPERTS)
    def _():
        n_total = first_ref[sb * N_EXPERTS + N_EXPERTS - 1] + nch_ref[sb * N_EXPERTS + N_EXPERTS - 1]
        scatter(n_total - 1)

    @pl.when(e >= N_EXPERTS)
    def _():
        r0 = pl.multiple_of((e - N_EXPERTS) * rb, rb)
        acc = gate_ref[:, 0:1] * _unpack_halves(slots_ref[0, pl.ds(r0, rb), :])
        for k in range(1, TOP_K):
            acc = acc + gate_ref[:, k:k + 1] * _unpack_halves(slots_ref[k, pl.ds(r0, rb), :])
        o_ref[...] = x_ref[...] + g2_ref[0] * acc


def _moe(xn, hp, ridx, gate, g2, w1p, b1p, w2b, b2, tb, rb, layer=0):
    bsz, s, d = xn.shape
    t = bsz * s
    n_super = t // tb
    n_piece = tb // rb
    dff = w2b.shape[1]
    w2map = lambda sb, e, *_: (layer * N_EXPERTS + jnp.minimum(e, N_EXPERTS - 1), 0, 0)
    flat_e = ridx[:, :, :TOP_K].reshape(n_super, tb * TOP_K)
    order = jnp.argsort(flat_e, axis=1, stable=True).astype(I32)
    counts = jnp.sum(flat_e[:, :, None] == jnp.arange(N_EXPERTS, dtype=I32)[None, None, :], axis=1).astype(I32)
    offs = (jnp.cumsum(counts, axis=1) - counts).astype(I32)
    nch = (counts + MOE_M - 1) // MOE_M
    first = (jnp.cumsum(nch, axis=1) - nch).astype(I32)
    table_len = -(-(tb * TOP_K // MOE_M + N_EXPERTS + 2) // SUBLANES) * SUBLANES
    j = jnp.arange(table_len - 1, dtype=I32)[None, :]
    owner = jnp.sum(j[:, :, None] >= (first + nch)[:, None, :], axis=2)
    owner_c = jnp.minimum(owner, N_EXPERTS - 1)
    within = j - jnp.take_along_axis(first, owner_c, axis=1)
    c_valid = jnp.where(owner < N_EXPERTS,
                        jnp.minimum(jnp.take_along_axis(counts, owner_c, axis=1) - within * MOE_M, MOE_M), 0)
    c_base = jnp.where(owner < N_EXPERTS, jnp.take_along_axis(offs, owner_c, axis=1) + within * MOE_M, 0)
    zero = jnp.zeros((n_super, 1), I32)
    c_valid = jnp.concatenate([zero, c_valid.astype(I32)], axis=1)
    c_base = jnp.concatenate([zero, c_base.astype(I32)], axis=1)

    piece = lambda sb, e, *_: (sb * n_piece + jnp.maximum(e - N_EXPERTS, 0), 0)
    wmap = lambda sb, e, *_: (jnp.minimum(e, N_EXPERTS - 1), 0, 0)
    grid_spec = pltpu.PrefetchScalarGridSpec(
        num_scalar_prefetch=4,
        grid=(n_super, N_EXPERTS + n_piece),
        in_specs=[
            pl.BlockSpec((1, 1, tb * TOP_K + MOE_M), lambda sb, e, *_: (sb, 0, 0), memory_space=pltpu.SMEM),
            pl.BlockSpec((tb, d // 2), lambda sb, e, *_: (sb, 0), pipeline_mode=pl.Buffered(1)),
            pl.BlockSpec((1, d, 2 * dff), wmap),
            pl.BlockSpec((1, 1, 2 * dff), wmap),
            pl.BlockSpec((1, dff, d), w2map),
            pl.BlockSpec((1, 1, d), wmap),
            pl.BlockSpec((rb, LANES), piece),
            pl.BlockSpec((rb, d), piece),
            pl.BlockSpec((1, 1, d), lambda sb, e, *_: ((sb * tb) // s, 0, 0)),
        ],
        out_specs=pl.BlockSpec((rb, d), piece),
        scratch_shapes=[
            pltpu.VMEM((TOP_K, tb + SUBLANES, d // 2), U32),
            pltpu.VMEM((2, MOE_M, d // 2), U32),
            pltpu.VMEM((2, MOE_M, d // 2), U32),
        ],
    )
    out = pl.pallas_call(
        functools.partial(_moe_kernel, tb=tb, rb=rb, table_len=table_len),
        grid_spec=grid_spec,
        out_shape=jax.ShapeDtypeStruct((t, d), F32),
        compiler_params=_cparams(("arbitrary", "arbitrary")),
        name="moe",
    )(first.reshape(-1), nch.astype(I32).reshape(-1), c_base.reshape(-1), c_valid.reshape(-1),
      jnp.pad(order, ((0, 0), (0, MOE_M))).reshape(n_super, 1, tb * TOP_K + MOE_M), hp.reshape(t, d // 2),
      w1p, b1p, w2b, b2, gate.reshape(t, LANES), xn.reshape(t, d), g2.reshape(bsz, 1, d))
    return out.reshape(bsz, s, d)


MXU_COLS = 256


def _deinterleave_kernel(w_ref, perm_ref, o_ref):
    half = w_ref.shape[2] // 2
    hw = MXU_COLS // 2
    for b in range(w_ref.shape[2] // MXU_COLS):
        blk = w_ref[0, :, b * MXU_COLS:(b + 1) * MXU_COLS].astype(BF16)
        y = jnp.dot(blk, perm_ref[...], preferred_element_type=F32).astype(BF16)
        o_ref[0, :, b * hw:(b + 1) * hw] = y[:, :hw]
        o_ref[0, :, half + b * hw:half + (b + 1) * hw] = y[:, hw:]


def _deinterleave_cast(w1, layer, rows=512):
    depth, n_e, d, two_f = w1.shape
    src = np.concatenate([np.arange(0, MXU_COLS, 2), np.arange(1, MXU_COLS, 2)])
    perm = np.zeros((MXU_COLS, MXU_COLS), np.float32)
    perm[src, np.arange(MXU_COLS)] = 1.0
    return pl.pallas_call(
        _deinterleave_kernel,
        grid=(n_e, d // rows),
        in_specs=[pl.BlockSpec((1, rows, two_f), lambda e, r: (layer * n_e + e, r, 0)),
                  pl.BlockSpec((MXU_COLS, MXU_COLS), lambda e, r: (0, 0))],
        out_specs=pl.BlockSpec((1, rows, two_f), lambda e, r: (e, r, 0)),
        out_shape=jax.ShapeDtypeStruct((n_e, d, two_f), BF16),
        compiler_params=_cparams(("parallel", "parallel")),
        name="w1_deinterleave",
    )(w1.reshape(depth * n_e, d, two_f), jnp.asarray(perm, BF16))


def _deinterleave_bias(b1_l):
    n_e, two_f = b1_l.shape
    return jnp.concatenate([b1_l[:, 0::2], b1_l[:, 1::2]], axis=1).reshape(n_e, 1, two_f)


def kernel(x, c, rel_bias, mod_w, mod_b, norm_mix_w, norm_ffn_w, w_in, conv_w, a_log, dt_bias, gdn_norm_w,
           q_norm_w, k_norm_w, w_out, router_w, router_b, w1, b1, w2, b2):
    depth = mod_w.shape[0]
    bsz, s, d = x.shape
    mod = _modulation(c, mod_w, mod_b)
    tm = min(512, s)
    tb = min(MOE_TB, bsz * s)
    rb = min(MOE_RB, tb)
    n_e, dff = w2.shape[1], w2.shape[2]
    w2b = w2.astype(BF16).reshape(depth * n_e, dff, d)
    for l in range(depth):
        qkv_a, z_a, qb, kb, vb, qi, small = _inproj(
            x, mod[l], norm_mix_w[l], _permute_w_in(w_in[l]), q_norm_w[l], k_norm_w[l], tm)
        y_a = _gdn(qkv_a, z_a, small, conv_w[l], a_log[l], dt_bias[l], gdn_norm_w[l], sb=tm)
        y_b = _dsa(qb, kb, vb, qi, small, rel_bias)
        xn, hp, ridx, gate = _outproj(y_a, y_b, x, mod[l], w_out[l].astype(BF16), norm_ffn_w[l],
                                      router_w[l], router_b[l], tm)
        x = _moe(xn, hp, ridx, gate, mod[l][:, 5], _deinterleave_cast(w1, l), _deinterleave_bias(b1[l]), w2b,
                 b2[l].reshape(n_e, 1, d), tb, rb, layer=l)
    return x
```

```python
import functools
import math

import jax
import jax.numpy as jnp
import numpy as np
from jax import lax
from jax.experimental import pallas as pl
from jax.experimental.pallas import tpu as pltpu

F32 = jnp.float32
BF16 = jnp.bfloat16
I32 = jnp.int32
I16 = jnp.int16
U32 = jnp.uint32
HIGHEST = lax.Precision.HIGHEST

LANES = 128
SUBLANES = 8
VMEM_LIMIT_BYTES = 56 * 1024 * 1024

CHUNK = 64
HEAD_DIM_A = 128
N_HEADS_A = 4
WIDTH_A = N_HEADS_A * HEAD_DIM_A
CONV_K = 4
HEAD_DIM_B = 64
N_HEADS_B = 8
WIDTH_B = N_HEADS_B * HEAD_DIM_B
IDX_HEADS = 8
IDX_DIM = 64
TOPK_KEYS_MAX = 256
REL_BUCKETS = 32
REL_MAX_DIST = 1024
N_EXPERTS = 32
TOP_K = 4
SWIGLU_ALPHA = 1.702
SWIGLU_LIMIT = 7.0
EPS = 1e-6
NEG_BIG = -1e30

C_QKVA = 0
C_Z = C_QKVA + 3 * WIDTH_A
C_QB = C_Z + WIDTH_A
C_KB = C_QB + WIDTH_B
C_VB = C_KB + WIDTH_B
C_QI = C_VB + WIDTH_B
C_SMALL = C_QI + IDX_HEADS * IDX_DIM
D_IN_PAD = C_SMALL + LANES
S_KIDX = 0
S_B = IDX_DIM
S_A = S_B + N_HEADS_A
S_WIDX = S_A + N_HEADS_A


def _cparams(sem):
    return pltpu.CompilerParams(dimension_semantics=sem, vmem_limit_bytes=VMEM_LIMIT_BYTES)


def _silu(x):
    return x * (1.0 / (1.0 + jnp.exp(-x)))


def _sigmoid(x):
    return 1.0 / (1.0 + jnp.exp(-x))


def _softplus(x):
    return jnp.maximum(x, 0.0) + jnp.log(1.0 + jnp.exp(-jnp.abs(x)))


def _mod_kernel(c_ref, w_ref, b_ref, o_ref):
    a = _silu(c_ref[...])
    o_ref[0] = jnp.dot(a, w_ref[0], precision=HIGHEST, preferred_element_type=F32) + b_ref[0]


def _modulation(c, mod_w, mod_b):
    depth, d, n = mod_w.shape
    bsz = c.shape[0]
    rows = -(-bsz // SUBLANES) * SUBLANES
    c_pad = jnp.zeros((rows, d), F32).at[:bsz].set(c)
    tn = 1536
    out = pl.pallas_call(
        _mod_kernel,
        grid=(depth, n // tn),
        in_specs=[
            pl.BlockSpec((rows, d), lambda l, j: (0, 0)),
            pl.BlockSpec((1, d, tn), lambda l, j: (l, 0, j)),
            pl.BlockSpec((1, 1, tn), lambda l, j: (l, 0, j)),
        ],
        out_specs=pl.BlockSpec((1, rows, tn), lambda l, j: (l, 0, j)),
        out_shape=jax.ShapeDtypeStruct((depth, rows, n), F32),
        compiler_params=_cparams(("arbitrary", "arbitrary")),
        name="adaln_mod",
    )(c_pad, mod_w, mod_b.reshape(depth, 1, n))
    return out[:, :bsz].reshape(depth, bsz, 6, d)


def _head_rms(t, group_ref, wn, inv_dim):
    t2 = t * t
    hi = t2.astype(BF16)
    lo = (t2 - hi.astype(F32)).astype(BF16)
    ss = (jnp.dot(hi, group_ref[...], preferred_element_type=F32)
          + jnp.dot(lo, group_ref[...], preferred_element_type=F32))
    return t * lax.rsqrt(ss * inv_dim + EPS) * wn


def _inproj_kernel(x_ref, mod_ref, nw_ref, w_ref, group_ref, qn_ref, kn_ref,
                   qkva_ref, z_ref, qb_ref, kb_ref, vb_ref, qi_ref, small_ref):
    x = x_ref[0]
    ms = jnp.mean(x * x, axis=-1, keepdims=True)
    y = x * lax.rsqrt(ms + EPS) * nw_ref[...]
    h = y * (1.0 + mod_ref[0, 1:2, :]) + mod_ref[0, 0:1, :]
    hb = h.astype(BF16)

    def mm(lo, width):
        return jnp.dot(hb, w_ref[:, lo:lo + width], preferred_element_type=F32)

    qkva_ref[0] = mm(C_QKVA, 3 * WIDTH_A)
    z_ref[0] = mm(C_Z, WIDTH_A)
    q = _head_rms(mm(C_QB, WIDTH_B), group_ref, qn_ref[...], 1.0 / HEAD_DIM_B)
    qb_ref[0] = (q * (HEAD_DIM_B ** -0.5 * LOG2E)).astype(BF16)
    k = _head_rms(mm(C_KB, WIDTH_B), group_ref, kn_ref[...], 1.0 / HEAD_DIM_B)
    kb_ref[0] = k.astype(BF16)
    v = mm(C_VB, WIDTH_B)
    lane = lax.broadcasted_iota(I32, (v.shape[0], LANES), 1)
    tail = (lane == HEAD_DIM_B).astype(F32)
    for p in range(N_HEADS_B // 2):
        pair = v[:, p * LANES:(p + 1) * LANES]
        vb_ref[0, :, (2 * p) * LANES:(2 * p + 1) * LANES] = jnp.where(lane < HEAD_DIM_B, pair, tail).astype(BF16)
        vb_ref[0, :, (2 * p + 1) * LANES:(2 * p + 2) * LANES] = jnp.where(
            lane < HEAD_DIM_B, pltpu.roll(pair, HEAD_DIM_B, axis=1), tail).astype(BF16)
    qi_ref[0] = mm(C_QI, IDX_HEADS * IDX_DIM).astype(BF16)
    small_ref[0] = mm(C_SMALL, LANES)


def _permute_w_in(w_in_l):
    d = w_in_l.shape[0]
    o = 0
    qkva = w_in_l[:, o:o + 3 * WIDTH_A]; o += 3 * WIDTH_A
    z = w_in_l[:, o:o + WIDTH_A]; o += WIDTH_A
    b = w_in_l[:, o:o + N_HEADS_A]; o += N_HEADS_A
    a = w_in_l[:, o:o + N_HEADS_A]; o += N_HEADS_A
    qkvb = w_in_l[:, o:o + 3 * WIDTH_B]; o += 3 * WIDTH_B
    qi = w_in_l[:, o:o + IDX_HEADS * IDX_DIM]; o += IDX_HEADS * IDX_DIM
    ki = w_in_l[:, o:o + IDX_DIM]; o += IDX_DIM
    wi = w_in_l[:, o:o + IDX_HEADS]; o += IDX_HEADS
    pad = jnp.zeros((d, LANES - IDX_DIM - 2 * N_HEADS_A - IDX_HEADS), w_in_l.dtype)
    return jnp.concatenate([qkva, z, qkvb, qi, ki, b, a, wi, pad], axis=1).astype(BF16)


def _group_ones(width, group):
    g = np.arange(width) // group
    return jnp.asarray((g[:, None] == g[None, :]).astype(np.float32), dtype=BF16)


def _inproj(x, mod_l, norm_w, w_perm, q_norm_w, k_norm_w, tm):
    bsz, s, d = x.shape
    f = lambda b, i: (b, i, 0)
    const2 = lambda b, i: (0, 0)
    outs = [
        (3 * WIDTH_A, F32), (WIDTH_A, F32), (WIDTH_B, BF16), (WIDTH_B, BF16), (N_HEADS_B * LANES, BF16),
        (IDX_HEADS * IDX_DIM, BF16), (LANES, F32),
    ]
    return pl.pallas_call(
        _inproj_kernel,
        grid=(bsz, s // tm),
        in_specs=[
            pl.BlockSpec((1, tm, d), f),
            pl.BlockSpec((1, 6, d), lambda b, i: (b, 0, 0)),
            pl.BlockSpec((1, d), const2),
            pl.BlockSpec((d, D_IN_PAD), const2),
            pl.BlockSpec((WIDTH_B, WIDTH_B), const2),
            pl.BlockSpec((1, WIDTH_B), const2),
            pl.BlockSpec((1, WIDTH_B), const2),
        ],
        out_specs=[pl.BlockSpec((1, tm, w), f) for w, _ in outs],
        out_shape=[jax.ShapeDtypeStruct((bsz, s, w), dt) for w, dt in outs],
        compiler_params=_cparams(("parallel", "parallel")),
        name="inproj",
    )(x, mod_l, norm_w.reshape(1, d), w_perm, _group_ones(WIDTH_B, HEAD_DIM_B),
      jnp.tile(q_norm_w, N_HEADS_B).reshape(1, WIDTH_B), jnp.tile(k_norm_w, N_HEADS_B).reshape(1, WIDTH_B))


def _dot_nt(a, b, precision=None):
    return lax.dot_general(a, b, (((1,), (1,)), ((), ())), precision=precision, preferred_element_type=F32)


def _mm_bf16(a, b):
    return jnp.dot(a.astype(BF16), b.astype(BF16), preferred_element_type=F32)


def _split2(x):
    hi = x.astype(BF16)
    return hi, (x - hi.astype(F32)).astype(BF16)


def _split3(x):
    hi = x.astype(BF16)
    r = x - hi.astype(F32)
    mid = r.astype(BF16)
    return hi, mid, (r - mid.astype(F32)).astype(BF16)


def _mm3(a, b):
    return (jnp.dot(a[0], b[0], preferred_element_type=F32) + jnp.dot(a[0], b[1], preferred_element_type=F32)
            + jnp.dot(a[1], b[0], preferred_element_type=F32))


GDN_PAR = 4


def _gdn_kernel(qkv_ref, z_ref, small_ref, convw_ref, alog_ref, dtb_ref, nw_ref,
                y_ref, xe_ref, u_ref, state_ref, uval_ref, wdec_ref, qg_ref, kdec_ref, attn_ref, egl_ref, *, sb):
    n_chunks = sb // CHUNK
    halo = SUBLANES

    @pl.when(pl.program_id(1) == 0)
    def _():
        xe_ref[0:halo, :] = jnp.zeros((halo, 3 * WIDTH_A), F32)
        state_ref[...] = jnp.zeros_like(state_ref)

    xe_ref[halo:halo + sb, :] = qkv_ref[0]

    rows = 128
    for g in range(3 * WIDTH_A // LANES):
        cs = slice(g * LANES, (g + 1) * LANES)
        for r in range(sb // rows):
            base = halo - (CONV_K - 1) + r * rows
            acc = xe_ref[base:base + rows, cs] * convw_ref[0:1, cs]
            for j in range(1, CONV_K):
                acc = acc + xe_ref[base + j:base + j + rows, cs] * convw_ref[j:j + 1, cs]
            u_ref[r * rows:(r + 1) * rows, cs] = _silu(acc)

    xe_ref[0:halo, :] = xe_ref[sb:sb + halo, :]

    wide = N_HEADS_A * CHUNK
    heads = range(N_HEADS_A)
    ii = lax.broadcasted_iota(I32, (CHUNK, wide), 0)
    jj = lax.broadcasted_iota(I32, (CHUNK, wide), 1) & (CHUNK - 1)
    eye_w = (ii == jj).astype(F32)
    tri = (lax.broadcasted_iota(I32, (CHUNK, CHUNK), 0)
           >= lax.broadcasted_iota(I32, (CHUNK, CHUNK), 1)).astype(F32).astype(BF16)
    shift = int(math.log2(CHUNK))
    bd_mask = ((lax.broadcasted_iota(I32, (wide, wide), 0) >> shift)
               == (lax.broadcasted_iota(I32, (wide, wide), 1) >> shift)).astype(F32)

    bd_mask = bd_mask.astype(BF16)

    def block_diag(parts):
        return tuple(jnp.concatenate([m] * N_HEADS_A, axis=0) * bd_mask for m in parts)

    def prepare(c):
        rs = pl.ds(pl.multiple_of(c * CHUNK, CHUNK), CHUNK)
        qn, kn, v, beta, g_b = [], [], [], [], []
        for h in heads:
            q = u_ref[rs, h * HEAD_DIM_A:(h + 1) * HEAD_DIM_A]
            k = u_ref[rs, WIDTH_A + h * HEAD_DIM_A:WIDTH_A + (h + 1) * HEAD_DIM_A]
            v.append(u_ref[rs, 2 * WIDTH_A + h * HEAD_DIM_A:2 * WIDTH_A + (h + 1) * HEAD_DIM_A])
            qn.append(q * (lax.rsqrt(jnp.sum(q * q, axis=-1, keepdims=True) + EPS) * (HEAD_DIM_A ** -0.5)))
            kn.append(k * lax.rsqrt(jnp.sum(k * k, axis=-1, keepdims=True) + EPS))
            beta.append(_sigmoid(small_ref[0, rs, S_B + h:S_B + h + 1]))
            g = -jnp.exp(alog_ref[0:1, h:h + 1]) * _softplus(small_ref[0, rs, S_A + h:S_A + h + 1]
                                                            + dtb_ref[0:1, h:h + 1])
            g_b.append(jnp.broadcast_to(g, (CHUNK, CHUNK)))
        gc_w = sum(jnp.dot(tri, part, preferred_element_type=F32) for part in _split3(jnp.concatenate(g_b, axis=1)))
        gc_row = jnp.sum(jnp.where(ii == jj, gc_w, 0.0), axis=0, keepdims=True)
        decay_w = jnp.exp(jnp.where(ii >= jj, gc_w - gc_row, NEG_BIG))
        k_beta = [kn[h] * beta[h] for h in heads]
        kk_w = jnp.concatenate([_dot_nt(k_beta[h].astype(BF16), kn[h].astype(BF16)) for h in heads], axis=1)
        a_w = -jnp.where(ii > jj, kk_w * decay_w, 0.0)
        gc = [gc_w[:, h * CHUNK:h * CHUNK + 1] for h in heads]
        egc = [jnp.exp(gc[h]) for h in heads]
        qk = [_dot_nt(qn[h].astype(BF16), kn[h].astype(BF16)) for h in heads]
        for h in heads:
            g_last = gc[h][CHUNK - 1:CHUNK, :]
            qg_ref[c, h] = (qn[h] * egc[h]).astype(BF16)
            kdec_ref[c, h] = kn[h] * jnp.exp(g_last - gc[h])
            attn_ref[c, h] = (qk[h] * decay_w[:, h * CHUNK:(h + 1) * CHUNK]).astype(BF16)
            egl_ref[c, h] = jnp.broadcast_to(jnp.exp(g_last), (SUBLANES, LANES))
        return a_w, [_split2(v[h] * beta[h]) for h in heads], [_split2(k_beta[h] * egc[h]) for h in heads]

    def solve_body(cg, carry):
        group = range(GDN_PAR)
        chunks = [cg * GDN_PAR + i for i in group]
        pre = [prepare(c) for c in chunks]
        t_w = [eye_w + pre[i][0] for i in group]
        p_parts = [_split2(pre[i][0]) for i in group]
        bd = [block_diag(p_parts[i]) for i in group]
        for _ in range(shift - 1):
            prod = [_mm3(p_parts[i], bd[i]) for i in group]
            p_parts = [_split2(prod[i]) for i in group]
            bd = [block_diag(p_parts[i]) for i in group]
            upd = [_mm3(_split2(t_w[i]), bd[i]) for i in group]
            t_w = [t_w[i] + upd[i] for i in group]
        for i in group:
            t_h = [_split2(t_w[i][:, h * CHUNK:(h + 1) * CHUNK]) for h in heads]
            u_val = [_mm3(t_h[h], pre[i][1][h]) for h in heads]
            w_dec = [_mm3(t_h[h], pre[i][2][h]) for h in heads]
            for h in heads:
                uval_ref[chunks[i], h] = u_val[h]
                wdec_ref[chunks[i], h] = w_dec[h].astype(BF16)
        return carry

    lax.fori_loop(0, n_chunks // GDN_PAR, solve_body, 0)

    def scan_body(c, carry):
        rs = pl.ds(pl.multiple_of(c * CHUNK, CHUNK), CHUNK)
        state = [state_ref[h] for h in heads]
        state_b = [s_h.astype(BF16) for s_h in state]
        w_s = [jnp.dot(wdec_ref[c, h], state_b[h], preferred_element_type=F32) for h in heads]
        v_new = [(uval_ref[c, h] - w_s[h]).astype(BF16) for h in heads]
        o = [jnp.dot(qg_ref[c, h], state_b[h], preferred_element_type=F32)
             + jnp.dot(attn_ref[c, h], v_new[h], preferred_element_type=F32) for h in heads]
        for h in heads:
            state_ref[h] = (state[h] * egl_ref[c, h][0:1, 0:1]
                            + jnp.dot(kdec_ref[c, h].T.astype(BF16), v_new[h], preferred_element_type=F32))
        for h in heads:
            hs = slice(h * HEAD_DIM_A, (h + 1) * HEAD_DIM_A)
            on = o[h] * lax.rsqrt(jnp.mean(o[h] * o[h], axis=-1, keepdims=True) + EPS) * nw_ref[...]
            y_ref[0, rs, hs] = (on * _silu(z_ref[0, rs, hs])).astype(BF16)
        return carry

    lax.fori_loop(0, n_chunks, scan_body, 0)


def _gdn(qkv_a, z_a, small, conv_w, a_log, dt_bias, norm_w, sb):
    bsz, s, _ = qkv_a.shape
    n_c = sb // CHUNK
    assert n_c % GDN_PAR == 0 and s % sb == 0
    per_head = (n_c, N_HEADS_A, CHUNK, HEAD_DIM_A)
    f = lambda b, i: (b, i, 0)
    const2 = lambda b, i: (0, 0)
    return pl.pallas_call(
        functools.partial(_gdn_kernel, sb=sb),
        grid=(bsz, s // sb),
        in_specs=[
            pl.BlockSpec((1, sb, 3 * WIDTH_A), f),
            pl.BlockSpec((1, sb, WIDTH_A), f),
            pl.BlockSpec((1, sb, LANES), f),
            pl.BlockSpec((CONV_K, 3 * WIDTH_A), const2),
            pl.BlockSpec((1, N_HEADS_A), const2),
            pl.BlockSpec((1, N_HEADS_A), const2),
            pl.BlockSpec((1, HEAD_DIM_A), const2),
        ],
        out_specs=pl.BlockSpec((1, sb, WIDTH_A), f),
        out_shape=jax.ShapeDtypeStruct((bsz, s, WIDTH_A), BF16),
        scratch_shapes=[
            pltpu.VMEM((sb + SUBLANES, 3 * WIDTH_A), F32),
            pltpu.VMEM((sb, 3 * WIDTH_A), F32),
            pltpu.VMEM((N_HEADS_A, HEAD_DIM_A, HEAD_DIM_A), F32),
            pltpu.VMEM(per_head, F32),
            pltpu.VMEM(per_head, BF16),
            pltpu.VMEM(per_head, BF16),
            pltpu.VMEM(per_head, F32),
            pltpu.VMEM((n_c, N_HEADS_A, CHUNK, CHUNK), BF16),
            pltpu.VMEM((n_c, N_HEADS_A, SUBLANES, LANES), F32),
        ],
        compiler_params=_cparams(("parallel", "arbitrary")),
        name="gdn",
    )(qkv_a, z_a, small, conv_w, a_log.reshape(1, -1), dt_bias.reshape(1, -1), norm_w.reshape(1, -1))


QB = 128
FAR_T = 512
FAR_G = FAR_T // LANES
ATT_W = 256
LOG2E = 1.4426950408889634
NEAR_D = 9
NEAR_MIN = 5
INT_MIN = -2 ** 31
HALF_MIN = -2 ** 15


def _t5_bucket_np(rel):
    nb = REL_BUCKETS // 2
    max_exact = nb // 2
    side = np.where(rel > 0, nb, 0)
    n = np.abs(rel)
    nf = np.maximum(n, 1).astype(np.float32)
    large = max_exact + (np.log(nf / np.float32(max_exact)) / np.float32(math.log(REL_MAX_DIST / max_exact))
                         * np.float32(nb - max_exact)).astype(np.int32)
    large = np.minimum(large, nb - 1)
    return (side + np.where(n < max_exact, n, large)).astype(np.int32)


def _near_bucket_table():
    r = np.arange(QB)[:, None]
    c = np.arange(LANES)[None, :]
    return np.stack([_t5_bucket_np(c - r - LANES * d) for d in range(NEAR_D)])


FAR_BUCKET = int(_t5_bucket_np(np.array([-(NEAR_MIN * LANES + 1)]))[0])
assert all(int(b) == FAR_BUCKET for b in _t5_bucket_np(-np.arange((NEAR_MIN + 1) * LANES - (QB - 1), 1 << 20, 997)))


def _sortable_key(score):
    bits = pltpu.bitcast(score + 0.0, I32)
    return bits ^ ((bits >> 31) & 0x7FFFFFFF)


def _key_to_float(key):
    return pltpu.bitcast(key ^ ((key >> 31) & 0x7FFFFFFF), F32)


KEY_LOWEST = -0x7F800000


def _dsa_kernel(rb_ref, qb_ref, qi_ref, small_ref, kb_ref, va_ref, kidx2_ref, tab_ref,
                y_ref, qis_ref, qs_ref, wb_ref, sdot_ref, keys_ref, nbias_ref, thr_ref, jlim_ref,
                m_ref, acc_ref, s_ref, p_ref, peak_ref, *, seq, k_sel):
    i = pl.program_id(1)
    lane = lax.broadcasted_iota(I32, (QB, LANES), 1)
    row = lax.broadcasted_iota(I32, (QB, LANES), 0)
    even_f = (lane < HEAD_DIM_B).astype(F32)
    even_b = even_f.astype(BF16)
    odd_b = (1.0 - even_f).astype(BF16)

    @pl.when(i == 0)
    def _():
        nbias_ref[...] = jnp.zeros_like(nbias_ref)

        def d_body(d, c0):
            tab = tab_ref[d]

            def b_body(bk, c1):
                hit = tab == bk
                for h in range(N_HEADS_B):
                    nbias_ref[d * N_HEADS_B + h] = jnp.where(hit, rb_ref[bk, h] * LOG2E,
                                                             nbias_ref[d * N_HEADS_B + h])
                return c1

            return lax.fori_loop(0, REL_BUCKETS, b_body, c0)

        lax.fori_loop(0, NEAR_D, d_body, 0)

    for p in range(N_HEADS_B // 2):
        ps = slice(p * LANES, (p + 1) * LANES)
        qi_pair = qi_ref[0, :, ps]
        qis_ref[(2 * p) * QB:(2 * p + 1) * QB, :] = qi_pair * even_b
        qis_ref[(2 * p + 1) * QB:(2 * p + 2) * QB, :] = qi_pair * odd_b
        q_pair = qb_ref[0, :, ps]
        qs_ref[(2 * p) * QB:(2 * p + 1) * QB, :] = q_pair * even_b
        qs_ref[(2 * p + 1) * QB:(2 * p + 2) * QB, :] = q_pair * odd_b
    w_scale = IDX_HEADS ** -0.5 * IDX_DIM ** -0.5
    for h in range(IDX_HEADS):
        wb_ref[h] = jnp.broadcast_to(small_ref[0, :, S_WIDX + h:S_WIDX + h + 1] * w_scale, (QB, LANES))

    limit = i * QB + CHUNK + jnp.where(row >= CHUNK, CHUNK, 0)

    def score_body(t, c0):
        k0 = pl.multiple_of(t * FAR_T, FAR_T)
        sdot = _dot_nt(qis_ref[...], kidx2_ref[0, pl.ds(k0, FAR_T), :])
        for g in range(FAR_G):
            sdot_ref[g] = sdot[:, g * LANES:(g + 1) * LANES]
        for g in range(FAR_G):
            acc = jnp.maximum(sdot_ref[g, 0:QB, :], 0.0) * wb_ref[0]
            for h in range(1, IDX_HEADS):
                acc = acc + jnp.maximum(sdot_ref[g, h * QB:(h + 1) * QB, :], 0.0) * wb_ref[h]
            col = k0 + g * LANES + lane
            keys_ref[t * FAR_G + g] = jnp.where(col < limit, _sortable_key(acc), INT_MIN)
        return c0

    n_groups = i + 1
    n_tiles = i // FAR_G + 1
    lax.fori_loop(0, n_tiles, score_body, 0)

    @pl.when(n_tiles % 2 == 1)
    def _():
        for g in range(FAR_G):
            keys_ref[n_tiles * FAR_G + g] = jnp.full((QB, LANES), INT_MIN, I32)

    def count(pred):
        def t_body(t, acc):
            for g in range(2 * FAR_G):
                grp = t * (2 * FAR_G) + g
                acc = acc + jnp.where(pred(keys_ref[grp], grp * LANES + lane), 1, 0)
            return acc

        acc = lax.fori_loop(0, (n_tiles + 1) // 2, t_body, jnp.zeros((QB, LANES), I32))
        return jnp.broadcast_to(jnp.sum(acc, axis=1, keepdims=True), (QB, LANES))

    thr_ref[...] = jnp.full((QB, LANES), INT_MIN, I32)
    jlim_ref[...] = jnp.full((QB, LANES), -1, I32)

    @pl.when(n_groups * QB > k_sel)
    def _():
        def top2_body(t, carry):
            first, second = carry
            for g in range(2 * FAR_G):
                kt = keys_ref[t * (2 * FAR_G) + g]
                second = jnp.maximum(second, jnp.minimum(first, kt))
                first = jnp.maximum(first, kt)
            return first, second

        floor_key = jnp.full((QB, LANES), KEY_LOWEST, I32)
        first, second = lax.fori_loop(0, (n_tiles + 1) // 2, top2_body, (floor_key, floor_key))
        hi = _sortable_key(jnp.max(_key_to_float(first), axis=1, keepdims=True))
        lo = _sortable_key(jnp.min(_key_to_float(second), axis=1, keepdims=True))
        shared = jnp.min(lax.clz(lo ^ hi))
        keep = jnp.where(shared == 0, 0, lax.shift_left(jnp.int32(-1), jnp.minimum(32 - shared, 31)))
        start = jnp.broadcast_to(jnp.where(shared == 0, INT_MIN, lo & keep), (QB, LANES))

        def bit_body(step, r):
            cand = r + lax.shift_left(jnp.int32(1), 31 - step)
            cnt = count(lambda kt, col: kt >= cand)
            return jnp.where(cnt >= k_sel, cand, r)

        r = lax.fori_loop(shared, 32, bit_body, start)
        thr_ref[...] = r
        need = k_sel - count(lambda kt, col: kt > r)
        excess = count(lambda kt, col: kt == r) - need
        jlim_ref[...] = jnp.where(r == INT_MIN, -1, seq)

        @pl.when(jnp.max(excess) > 0)
        def _():
            def j_body(step, jl):
                cand = jl + lax.shift_left(jnp.int32(1), (seq.bit_length() - 1) - step)
                cnt = count(lambda kt, col: (kt == r) & (col < cand))
                return jnp.where(cnt < need, cand, jl)

            jl = lax.fori_loop(0, seq.bit_length(), j_body, jnp.zeros((QB, LANES), I32))
            jlim_ref[...] = jnp.where(r == INT_MIN, -1, jl)

    m_ref[...] = jnp.full(m_ref.shape, NEG_BIG, F32)
    acc_ref[...] = jnp.zeros_like(acc_ref)

    n_part = FAR_T // ATT_W
    part_g = ATT_W // LANES
    last_tile = seq // FAR_T - 1

    def tile_mask(t):
        negm = []
        for g in range(FAR_G):
            grp = t * FAR_G + g
            kt = keys_ref[grp]
            sel = (kt > thr_ref[...]) | ((kt == thr_ref[...]) & (grp * LANES + lane <= jlim_ref[...]))
            negm.append(jnp.where(sel, 0.0, -jnp.inf))
        return negm

    def logits_pair(t, buf, p, negm):
        heads = ((2 * p, slice(0, QB)), (2 * p + 1, slice(QB, 2 * QB)))
        peak = [None, None]
        for c in range(n_part):
            ks = pl.ds(pl.multiple_of(t * FAR_T + c * ATT_W, ATT_W), ATT_W)
            s = _dot_nt(qs_ref[2 * p * QB:(2 * p + 2) * QB, :], kb_ref[0, ks, p * LANES:(p + 1) * LANES])
            for gg in range(part_g):
                g = c * part_g + gg
                for n, (h, rows) in enumerate(heads):
                    v = s[rows, gg * LANES:(gg + 1) * LANES] + negm[g]
                    s_ref[buf, h, g] = v
                    peak[n] = v if peak[n] is None else jnp.maximum(peak[n], v)
        for n, (h, _) in enumerate(heads):
            peak_ref[buf, h] = peak[n]

    def finish(buf, h, part, rmax, bias_scalar):
        m_old = m_ref[h]
        m_new = jnp.maximum(m_old, rmax + bias_scalar)
        shift = jnp.concatenate([m_new - bias_scalar] * part_g, axis=1)
        for c in range(n_part):
            p_ref[buf, h, :, c * ATT_W:(c + 1) * ATT_W] = jnp.exp2(part(c) - shift).astype(BF16)
        acc_ref[h] = jnp.exp2(m_old - m_new) * acc_ref[h]
        m_ref[h] = m_new

    def softmax_far(t, buf, h):
        def part(c):
            return jnp.concatenate([s_ref[buf, h, c * part_g + g] for g in range(part_g)], axis=1)

        finish(buf, h, part, jnp.max(peak_ref[buf, h], axis=1, keepdims=True), rb_ref[FAR_BUCKET, h] * LOG2E)

    def softmax_near(t, buf, h):
        def part(c):
            return jnp.concatenate(
                [s_ref[buf, h, c * part_g + g]
                 + nbias_ref[jnp.clip(i - (t * FAR_G + c * part_g + g), 0, NEAR_D - 1) * N_HEADS_B + h]
                 for g in range(part_g)], axis=1)

        rmax = None
        for c in range(n_part):
            r = jnp.max(part(c), axis=1, keepdims=True)
            rmax = r if rmax is None else jnp.maximum(rmax, r)
        finish(buf, h, part, rmax, 0.0)

    def pv_head(t, buf, h):
        ks = pl.ds(pl.multiple_of(t * FAR_T, FAR_T), FAR_T)
        acc_ref[h] += jnp.dot(p_ref[buf, h], va_ref[0, ks, h * LANES:(h + 1) * LANES], preferred_element_type=F32)

    def tile_body(softmax):
        def body(t, c0):
            negm = tile_mask(t)
            n_pair = N_HEADS_B // 2
            for step in range(n_pair + 2):
                if step < n_pair:
                    logits_pair(t, 0, step, negm)
                if 1 <= step <= n_pair:
                    for h in (2 * step - 2, 2 * step - 1):
                        softmax(t, 0, h)
                if step >= 2:
                    for h in (2 * step - 4, 2 * step - 3):
                        pv_head(t, 0, h)
            return c0

        return body

    far_tiles = jnp.maximum(i - NEAR_MIN, 0) // FAR_G
    lax.fori_loop(0, far_tiles, tile_body(softmax_far), 0)
    lax.fori_loop(far_tiles, n_tiles, tile_body(softmax_near), 0)

    def head_out(h):
        a = acc_ref[h]
        return a * (1.0 / a[:, HEAD_DIM_B:HEAD_DIM_B + 1])

    for p in range(N_HEADS_B // 2):
        o_odd = pltpu.roll(head_out(2 * p + 1), HEAD_DIM_B, axis=1)
        y_ref[0, :, p * LANES:(p + 1) * LANES] = jnp.where(lane < HEAD_DIM_B, head_out(2 * p), o_odd).astype(BF16)


def _dsa(qb, kb, va, qi, small, rel_bias):
    bsz, s, _ = qb.shape
    assert s % FAR_T == 0
    k_sel = min(TOPK_KEYS_MAX, s // 4)
    kidx = small[:, :, S_KIDX:S_KIDX + IDX_DIM].astype(BF16)
    kidx2 = jnp.concatenate([kidx, kidx], axis=-1)
    tab = jnp.asarray(_near_bucket_table())
    blk = lambda b, i: (b, i, 0)
    full = lambda b, i: (b, 0, 0)
    one = pl.Buffered(1)
    return pl.pallas_call(
        functools.partial(_dsa_kernel, seq=s, k_sel=k_sel),
        grid=(bsz, s // QB),
        in_specs=[
            pl.BlockSpec(memory_space=pltpu.SMEM),
            pl.BlockSpec((1, QB, WIDTH_B), blk),
            pl.BlockSpec((1, QB, IDX_HEADS * IDX_DIM), blk),
            pl.BlockSpec((1, QB, LANES), blk),
            pl.BlockSpec((1, s, WIDTH_B), full, pipeline_mode=one),
            pl.BlockSpec((1, s, N_HEADS_B * LANES), full, pipeline_mode=one),
            pl.BlockSpec((1, s, LANES), full, pipeline_mode=one),
            pl.BlockSpec((NEAR_D, QB, LANES), lambda b, i: (0, 0, 0), pipeline_mode=one),
        ],
        out_specs=pl.BlockSpec((1, QB, WIDTH_B), blk),
        out_shape=jax.ShapeDtypeStruct((bsz, s, WIDTH_B), BF16),
        scratch_shapes=[
            pltpu.VMEM((IDX_HEADS * QB, LANES), BF16),
            pltpu.VMEM((N_HEADS_B * QB, LANES), BF16),
            pltpu.VMEM((IDX_HEADS, QB, LANES), F32),
            pltpu.VMEM((FAR_G, IDX_HEADS * QB, LANES), F32),
            pltpu.VMEM((s // LANES, QB, LANES), I32),
            pltpu.VMEM((NEAR_D * N_HEADS_B, QB, LANES), F32),
            pltpu.VMEM((QB, LANES), I32),
            pltpu.VMEM((QB, LANES), I32),
            pltpu.VMEM((N_HEADS_B, QB, LANES), F32),
            pltpu.VMEM((N_HEADS_B, QB, LANES), F32),
            pltpu.VMEM((2, N_HEADS_B, FAR_G, QB, LANES), F32),
            pltpu.VMEM((2, N_HEADS_B, QB, FAR_T), BF16),
            pltpu.VMEM((2, N_HEADS_B, QB, LANES), F32),
        ],
        compiler_params=_cparams(("parallel", "arbitrary")),
        name="dsa",
    )(rel_bias, qb, qi, small, kb, va, kidx2, tab)


HALF_MASK = 0xFFFF0000


def _pack_halves(t):
    w = t.shape[1] // 2
    bits = pltpu.bitcast(t.astype(BF16).astype(F32), U32)
    return (bits[:, :w] >> 16) | (bits[:, w:] & jnp.uint32(HALF_MASK))


def _unpack_halves(p):
    lo = pltpu.bitcast(p << 16, F32)
    hi = pltpu.bitcast(p & jnp.uint32(HALF_MASK), F32)
    return jnp.concatenate([lo, hi], axis=1)


def _outproj_kernel(ya_ref, yb_ref, x_ref, mod_ref, wo_ref, nw_ref, rw_ref, rbias_ref,
                    xn_ref, hp_ref, ridx_ref, gate_ref):
    wa = ya_ref.shape[2]
    y = (jnp.dot(ya_ref[0], wo_ref[0:wa, :], preferred_element_type=F32)
         + jnp.dot(yb_ref[0], wo_ref[wa:, :], preferred_element_type=F32))
    xn = x_ref[0] + mod_ref[0, 2:3, :] * y
    xn_ref[0] = xn
    ms = jnp.mean(xn * xn, axis=-1, keepdims=True)
    h = xn * lax.rsqrt(ms + EPS) * nw_ref[...] * (1.0 + mod_ref[0, 4:5, :]) + mod_ref[0, 3:4, :]
    hp_ref[0] = _pack_halves(h)

    logits = jnp.dot(h, rw_ref[...], precision=HIGHEST, preferred_element_type=F32) + rbias_ref[...]
    lane = lax.broadcasted_iota(I32, logits.shape, 1)
    cur = logits
    vals, ridx = [], jnp.zeros(logits.shape, I32)
    for k in range(TOP_K):
        mx = jnp.max(cur, axis=1, keepdims=True)
        am = jnp.min(jnp.where(cur == mx, lane, LANES), axis=1, keepdims=True)
        cur = jnp.where(lane == am, -jnp.inf, cur)
        vals.append(mx)
        ridx = jnp.where(lane == k, am, ridx)
    ex = [jnp.exp(v - vals[0]) for v in vals]
    inv = 1.0 / (ex[0] + ex[1] + ex[2] + ex[3])
    gate = jnp.zeros(logits.shape, F32)
    for k in range(TOP_K):
        gate = jnp.where(lane == k, ex[k] * inv, gate)
    ridx_ref[0] = ridx
    gate_ref[0] = gate


def _outproj(y_a, y_b, x, mod_l, w_out_bf, norm_w, router_w, router_b, tm):
    bsz, s, d = x.shape
    n_e = router_w.shape[1]
    rw = jnp.zeros((d, LANES), F32).at[:, :n_e].set(router_w)
    rbias = jnp.full((1, LANES), NEG_BIG, F32).at[0, :n_e].set(router_b)
    blk = lambda b, i: (b, i, 0)
    const2 = lambda b, i: (0, 0)
    return pl.pallas_call(
        _outproj_kernel,
        grid=(bsz, s // tm),
        in_specs=[
            pl.BlockSpec((1, tm, y_a.shape[2]), blk),
            pl.BlockSpec((1, tm, y_b.shape[2]), blk),
            pl.BlockSpec((1, tm, d), blk),
            pl.BlockSpec((1, 6, d), lambda b, i: (b, 0, 0)),
            pl.BlockSpec((d, d), const2),
            pl.BlockSpec((1, d), const2),
            pl.BlockSpec((d, LANES), const2),
            pl.BlockSpec((1, LANES), const2),
        ],
        out_specs=[pl.BlockSpec((1, tm, d), blk), pl.BlockSpec((1, tm, d // 2), blk),
                   pl.BlockSpec((1, tm, LANES), blk), pl.BlockSpec((1, tm, LANES), blk)],
        out_shape=[jax.ShapeDtypeStruct((bsz, s, d), F32), jax.ShapeDtypeStruct((bsz, s, d // 2), U32),
                   jax.ShapeDtypeStruct((bsz, s, LANES), I32), jax.ShapeDtypeStruct((bsz, s, LANES), F32)],
        compiler_params=_cparams(("parallel", "parallel")),
        name="outproj_router",
    )(y_a, y_b, x, mod_l, w_out_bf, norm_w.reshape(1, d), rw, rbias)


MOE_TB = 2048
MOE_RB = 512
MOE_M = 144


def _moe_kernel(first_ref, nch_ref, cbase_ref, cvalid_ref, list_ref, hp_ref, w1_ref, b1_ref, w2_ref, b2_ref,
                gate_ref, x_ref, g2_ref, o_ref, slots_ref, xg_ref, yb_ref, *, tb, rb, table_len):
    sb = pl.program_id(0)
    e = pl.program_id(1)
    dff = w2_ref.shape[1]
    table = sb * table_len + 1

    def gather(j):
        base = cbase_ref[table + j]
        buf = (j + 2) % 2
        for r in range(MOE_M):
            code = list_ref[0, 0, base + r]
            xg_ref[buf, pl.ds(r, 1), :] = hp_ref[pl.ds(code >> 2, 1), :]

    def scatter(j):
        base = cbase_ref[table + j]
        n_valid = cvalid_ref[table + j]
        buf = (j + 2) % 2
        for r in range(MOE_M):
            code = jnp.where(r < n_valid, list_ref[0, 0, base + r], TOP_K * tb)
            slots_ref[code & (TOP_K - 1), pl.ds(code >> 2, 1), :] = yb_ref[buf, pl.ds(r, 1), :]

    @pl.when(e == 0)
    def _():
        gather(0)

    @pl.when(e < N_EXPERTS)
    def _():
        j0 = first_ref[sb * N_EXPERTS + e]

        def chunk(j, carry):
            buf = j % 2
            xb = _unpack_halves(xg_ref[buf]).astype(BF16)
            gather(j + 1)
            scatter(j - 1)
            u = jnp.dot(xb, w1_ref[0], preferred_element_type=F32) + b1_ref[0]
            glu = jnp.minimum(u[:, :dff], SWIGLU_LIMIT)
            lin = jnp.clip(u[:, dff:], -SWIGLU_LIMIT, SWIGLU_LIMIT)
            act = glu * _sigmoid(SWIGLU_ALPHA * glu) * (lin + 1.0)
            y = jnp.dot(act.astype(BF16), w2_ref[0], preferred_element_type=F32) + b2_ref[0]
            yb_ref[buf] = _pack_halves(y)
            return carry

        lax.fori_loop(j0, j0 + nch_ref[sb * N_EXPERTS + e], chunk, 0)

    @pl.when(e == N_EXPERTS)
    def _():
        n_total = first_ref[sb * N_EXPERTS + N_EXPERTS - 1] + nch_ref[sb * N_EXPERTS + N_EXPERTS - 1]
        scatter(n_total - 1)

    @pl.when(e >= N_EXPERTS)
    def _():
        r0 = pl.multiple_of((e - N_EXPERTS) * rb, rb)
        acc = gate_ref[:, 0:1] * _unpack_halves(slots_ref[0, pl.ds(r0, rb), :])
        for k in range(1, TOP_K):
            acc = acc + gate_ref[:, k:k + 1] * _unpack_halves(slots_ref[k, pl.ds(r0, rb), :])
        o_ref[...] = x_ref[...] + g2_ref[0] * acc


def _moe(xn, hp, ridx, gate, g2, w1p, b1p, w2b, b2, tb, rb, layer=0):
    bsz, s, d = xn.shape
    t = bsz * s
    n_super = t // tb
    n_piece = tb // rb
    dff = w2b.shape[1]
    w2map = lambda sb, e, *_: (layer * N_EXPERTS + jnp.minimum(e, N_EXPERTS - 1), 0, 0)
    flat_e = ridx[:, :, :TOP_K].reshape(n_super, tb * TOP_K)
    order = jnp.argsort(flat_e, axis=1, stable=True).astype(I32)
    counts = jnp.sum(flat_e[:, :, None] == jnp.arange(N_EXPERTS, dtype=I32)[None, None, :], axis=1).astype(I32)
    offs = (jnp.cumsum(counts, axis=1) - counts).astype(I32)
    nch = (counts + MOE_M - 1) // MOE_M
    first = (jnp.cumsum(nch, axis=1) - nch).astype(I32)
    table_len = -(-(tb * TOP_K // MOE_M + N_EXPERTS + 2) // SUBLANES) * SUBLANES
    j = jnp.arange(table_len - 1, dtype=I32)[None, :]
    owner = jnp.sum(j[:, :, None] >= (first + nch)[:, None, :], axis=2)
    owner_c = jnp.minimum(owner, N_EXPERTS - 1)
    within = j - jnp.take_along_axis(first, owner_c, axis=1)
    c_valid = jnp.where(owner < N_EXPERTS,
                        jnp.minimum(jnp.take_along_axis(counts, owner_c, axis=1) - within * MOE_M, MOE_M), 0)
    c_base = jnp.where(owner < N_EXPERTS, jnp.take_along_axis(offs, owner_c, axis=1) + within * MOE_M, 0)
    zero = jnp.zeros((n_super, 1), I32)
    c_valid = jnp.concatenate([zero, c_valid.astype(I32)], axis=1)
    c_base = jnp.concatenate([zero, c_base.astype(I32)], axis=1)

    piece = lambda sb, e, *_: (sb * n_piece + jnp.maximum(e - N_EXPERTS, 0), 0)
    wmap = lambda sb, e, *_: (jnp.minimum(e, N_EXPERTS - 1), 0, 0)
    grid_spec = pltpu.PrefetchScalarGridSpec(
        num_scalar_prefetch=4,
        grid=(n_super, N_EXPERTS + n_piece),
        in_specs=[
            pl.BlockSpec((1, 1, tb * TOP_K + MOE_M), lambda sb, e, *_: (sb, 0, 0), memory_space=pltpu.SMEM),
            pl.BlockSpec((tb, d // 2), lambda sb, e, *_: (sb, 0), pipeline_mode=pl.Buffered(1)),
            pl.BlockSpec((1, d, 2 * dff), wmap),
            pl.BlockSpec((1, 1, 2 * dff), wmap),
            pl.BlockSpec((1, dff, d), w2map),
            pl.BlockSpec((1, 1, d), wmap),
            pl.BlockSpec((rb, LANES), piece),
            pl.BlockSpec((rb, d), piece),
            pl.BlockSpec((1, 1, d), lambda sb, e, *_: ((sb * tb) // s, 0, 0)),
        ],
        out_specs=pl.BlockSpec((rb, d), piece),
        scratch_shapes=[
            pltpu.VMEM((TOP_K, tb + SUBLANES, d // 2), U32),
            pltpu.VMEM((2, MOE_M, d // 2), U32),
            pltpu.VMEM((2, MOE_M, d // 2), U32),
        ],
    )
    out = pl.pallas_call(
        functools.partial(_moe_kernel, tb=tb, rb=rb, table_len=table_len),
        grid_spec=grid_spec,
        out_shape=jax.ShapeDtypeStruct((t, d), F32),
        compiler_params=_cparams(("arbitrary", "arbitrary")),
        name="moe",
    )(first.reshape(-1), nch.astype(I32).reshape(-1), c_base.reshape(-1), c_valid.reshape(-1),
      jnp.pad(order, ((0, 0), (0, MOE_M))).reshape(n_super, 1, tb * TOP_K + MOE_M), hp.reshape(t, d // 2),
      w1p, b1p, w2b, b2, gate.reshape(t, LANES), xn.reshape(t, d), g2.reshape(bsz, 1, d))
    return out.reshape(bsz, s, d)


MXU_COLS = 256


def _deinterleave_kernel(w_ref, perm_ref, o_ref):
    half = w_ref.shape[2] // 2
    hw = MXU_COLS // 2
    for b in range(w_ref.shape[2] // MXU_COLS):
        blk = w_ref[0, :, b * MXU_COLS:(b + 1) * MXU_COLS].astype(BF16)
        y = jnp.dot(blk, perm_ref[...], preferred_element_type=F32).astype(BF16)
        o_ref[0, :, b * hw:(b + 1) * hw] = y[:, :hw]
        o_ref[0, :, half + b * hw:half + (b + 1) * hw] = y[:, hw:]


def _deinterleave_cast(w1, layer, rows=512):
    depth, n_e, d, two_f = w1.shape
    src = np.concatenate([np.arange(0, MXU_COLS, 2), np.arange(1, MXU_COLS, 2)])
    perm = np.zeros((MXU_COLS, MXU_COLS), np.float32)
    perm[src, np.arange(MXU_COLS)] = 1.0
    return pl.pallas_call(
        _deinterleave_kernel,
        grid=(n_e, d // rows),
        in_specs=[pl.BlockSpec((1, rows, two_f), lambda e, r: (layer * n_e + e, r, 0)),
                  pl.BlockSpec((MXU_COLS, MXU_COLS), lambda e, r: (0, 0))],
        out_specs=pl.BlockSpec((1, rows, two_f), lambda e, r: (e, r, 0)),
        out_shape=jax.ShapeDtypeStruct((n_e, d, two_f), BF16),
        compiler_params=_cparams(("parallel", "parallel")),
        name="w1_deinterleave",
    )(w1.reshape(depth * n_e, d, two_f), jnp.asarray(perm, BF16))


def _deinterleave_bias(b1_l):
    n_e, two_f = b1_l.shape
    return jnp.concatenate([b1_l[:, 0::2], b1_l[:, 1::2]], axis=1).reshape(n_e, 1, two_f)


def kernel(x, c, rel_bias, mod_w, mod_b, norm_mix_w, norm_ffn_w, w_in, conv_w, a_log, dt_bias, gdn_norm_w,
           q_norm_w, k_norm_w, w_out, router_w, router_b, w1, b1, w2, b2):
    depth = mod_w.shape[0]
    bsz, s, d = x.shape
    mod = _modulation(c, mod_w, mod_b)
    tm = min(512, s)
    tb = min(MOE_TB, bsz * s)
    rb = min(MOE_RB, tb)
    n_e, dff = w2.shape[1], w2.shape[2]
    w2b = w2.astype(BF16).reshape(depth * n_e, dff, d)
    for l in range(depth):
        qkv_a, z_a, qb, kb, vb, qi, small = _inproj(
            x, mod[l], norm_mix_w[l], _permute_w_in(w_in[l]), q_norm_w[l], k_norm_w[l], tm)
        y_a = _gdn(qkv_a, z_a, small, conv_w[l], a_log[l], dt_bias[l], gdn_norm_w[l], sb=tm)
        y_b = _dsa(qb, kb, vb, qi, small, rel_bias)
        xn, hp, ridx, gate = _outproj(y_a, y_b, x, mod[l], w_out[l].astype(BF16), norm_ffn_w[l],
                                      router_w[l], router_b[l], tm)
        x = _moe(xn, hp, ridx, gate, mod[l][:, 5], _deinterleave_cast(w1, l), _deinterleave_bias(b1[l]), w2b,
                 b2[l].reshape(n_e, 1, d), tb, rb, layer=l)
    return x
```

```python
import functools
import math

import jax
import jax.numpy as jnp
import numpy as np
from jax import lax
from jax.experimental import pallas as pl
from jax.experimental.pallas import tpu as pltpu

F32 = jnp.float32
BF16 = jnp.bfloat16
I32 = jnp.int32
I16 = jnp.int16
U32 = jnp.uint32
HIGHEST = lax.Precision.HIGHEST

LANES = 128
SUBLANES = 8
VMEM_LIMIT_BYTES = 56 * 1024 * 1024

CHUNK = 64
HEAD_DIM_A = 128
N_HEADS_A = 4
WIDTH_A = N_HEADS_A * HEAD_DIM_A
CONV_K = 4
HEAD_DIM_B = 64
N_HEADS_B = 8
WIDTH_B = N_HEADS_B * HEAD_DIM_B
IDX_HEADS = 8
IDX_DIM = 64
TOPK_KEYS_MAX = 256
REL_BUCKETS = 32
REL_MAX_DIST = 1024
N_EXPERTS = 32
TOP_K = 4
SWIGLU_ALPHA = 1.702
SWIGLU_LIMIT = 7.0
EPS = 1e-6
NEG_BIG = -1e30

C_QKVA = 0
C_Z = C_QKVA + 3 * WIDTH_A
C_QB = C_Z + WIDTH_A
C_KB = C_QB + WIDTH_B
C_VB = C_KB + WIDTH_B
C_QI = C_VB + WIDTH_B
C_SMALL = C_QI + IDX_HEADS * IDX_DIM
D_IN_PAD = C_SMALL + LANES
S_KIDX = 0
S_B = IDX_DIM
S_A = S_B + N_HEADS_A
S_WIDX = S_A + N_HEADS_A


def _cparams(sem):
    return pltpu.CompilerParams(dimension_semantics=sem, vmem_limit_bytes=VMEM_LIMIT_BYTES)


def _silu(x):
    return x * (1.0 / (1.0 + jnp.exp(-x)))


def _sigmoid(x):
    return 1.0 / (1.0 + jnp.exp(-x))


def _softplus(x):
    return jnp.maximum(x, 0.0) + jnp.log(1.0 + jnp.exp(-jnp.abs(x)))


def _mod_kernel(c_ref, w_ref, b_ref, o_ref):
    a = _silu(c_ref[...])
    o_ref[0] = jnp.dot(a, w_ref[0], precision=HIGHEST, preferred_element_type=F32) + b_ref[0]


def _modulation(c, mod_w, mod_b):
    depth, d, n = mod_w.shape
    bsz = c.shape[0]
    rows = -(-bsz // SUBLANES) * SUBLANES
    c_pad = jnp.zeros((rows, d), F32).at[:bsz].set(c)
    tn = 1536
    out = pl.pallas_call(
        _mod_kernel,
        grid=(depth, n // tn),
        in_specs=[
            pl.BlockSpec((rows, d), lambda l, j: (0, 0)),
            pl.BlockSpec((1, d, tn), lambda l, j: (l, 0, j)),
            pl.BlockSpec((1, 1, tn), lambda l, j: (l, 0, j)),
        ],
        out_specs=pl.BlockSpec((1, rows, tn), lambda l, j: (l, 0, j)),
        out_shape=jax.ShapeDtypeStruct((depth, rows, n), F32),
        compiler_params=_cparams(("arbitrary", "arbitrary")),
        name="adaln_mod",
    )(c_pad, mod_w, mod_b.reshape(depth, 1, n))
    return out[:, :bsz].reshape(depth, bsz, 6, d)


def _head_rms(t, group_ref, wn, inv_dim):
    t2 = t * t
    hi = t2.astype(BF16)
    lo = (t2 - hi.astype(F32)).astype(BF16)
    ss = (jnp.dot(hi, group_ref[...], preferred_element_type=F32)
          + jnp.dot(lo, group_ref[...], preferred_element_type=F32))
    return t * lax.rsqrt(ss * inv_dim + EPS) * wn


def _inproj_kernel(x_ref, mod_ref, nw_ref, w_ref, group_ref, qn_ref, kn_ref,
                   qkva_ref, z_ref, qb_ref, kb_ref, vb_ref, qi_ref, small_ref):
    x = x_ref[0]
    ms = jnp.mean(x * x, axis=-1, keepdims=True)
    y = x * lax.rsqrt(ms + EPS) * nw_ref[...]
    h = y * (1.0 + mod_ref[0, 1:2, :]) + mod_ref[0, 0:1, :]
    hb = h.astype(BF16)

    def mm(lo, width):
        return jnp.dot(hb, w_ref[:, lo:lo + width], preferred_element_type=F32)

    qkva_ref[0] = mm(C_QKVA, 3 * WIDTH_A)
    z_ref[0] = mm(C_Z, WIDTH_A)
    q = _head_rms(mm(C_QB, WIDTH_B), group_ref, qn_ref[...], 1.0 / HEAD_DIM_B)
    qb_ref[0] = (q * (HEAD_DIM_B ** -0.5 * LOG2E)).astype(BF16)
    k = _head_rms(mm(C_KB, WIDTH_B), group_ref, kn_ref[...], 1.0 / HEAD_DIM_B)
    kb_ref[0] = k.astype(BF16)
    v = mm(C_VB, WIDTH_B)
    lane = lax.broadcasted_iota(I32, (v.shape[0], LANES), 1)
    tail = (lane == HEAD_DIM_B).astype(F32)
    for p in range(N_HEADS_B // 2):
        pair = v[:, p * LANES:(p + 1) * LANES]
        vb_ref[0, :, (2 * p) * LANES:(2 * p + 1) * LANES] = jnp.where(lane < HEAD_DIM_B, pair, tail).astype(BF16)
        vb_ref[0, :, (2 * p + 1) * LANES:(2 * p + 2) * LANES] = jnp.where(
            lane < HEAD_DIM_B, pltpu.roll(pair, HEAD_DIM_B, axis=1), tail).astype(BF16)
    qi_ref[0] = mm(C_QI, IDX_HEADS * IDX_DIM).astype(BF16)
    small_ref[0] = mm(C_SMALL, LANES)


def _permute_w_in(w_in_l):
    d = w_in_l.shape[0]
    o = 0
    qkva = w_in_l[:, o:o + 3 * WIDTH_A]; o += 3 * WIDTH_A
    z = w_in_l[:, o:o + WIDTH_A]; o += WIDTH_A
    b = w_in_l[:, o:o + N_HEADS_A]; o += N_HEADS_A
    a = w_in_l[:, o:o + N_HEADS_A]; o += N_HEADS_A
    qkvb = w_in_l[:, o:o + 3 * WIDTH_B]; o += 3 * WIDTH_B
    qi = w_in_l[:, o:o + IDX_HEADS * IDX_DIM]; o += IDX_HEADS * IDX_DIM
    ki = w_in_l[:, o:o + IDX_DIM]; o += IDX_DIM
    wi = w_in_l[:, o:o + IDX_HEADS]; o += IDX_HEADS
    pad = jnp.zeros((d, LANES - IDX_DIM - 2 * N_HEADS_A - IDX_HEADS), w_in_l.dtype)
    return jnp.concatenate([qkva, z, qkvb, qi, ki, b, a, wi, pad], axis=1).astype(BF16)


def _group_ones(width, group):
    g = np.arange(width) // group
    return jnp.asarray((g[:, None] == g[None, :]).astype(np.float32), dtype=BF16)


def _inproj(x, mod_l, norm_w, w_perm, q_norm_w, k_norm_w, tm):
    bsz, s, d = x.shape
    f = lambda b, i: (b, i, 0)
    const2 = lambda b, i: (0, 0)
    outs = [
        (3 * WIDTH_A, F32), (WIDTH_A, F32), (WIDTH_B, BF16), (WIDTH_B, BF16), (N_HEADS_B * LANES, BF16),
        (IDX_HEADS * IDX_DIM, BF16), (LANES, F32),
    ]
    return pl.pallas_call(
        _inproj_kernel,
        grid=(bsz, s // tm),
        in_specs=[
            pl.BlockSpec((1, tm, d), f),
            pl.BlockSpec((1, 6, d), lambda b, i: (b, 0, 0)),
            pl.BlockSpec((1, d), const2),
            pl.BlockSpec((d, D_IN_PAD), const2),
            pl.BlockSpec((WIDTH_B, WIDTH_B), const2),
            pl.BlockSpec((1, WIDTH_B), const2),
            pl.BlockSpec((1, WIDTH_B), const2),
        ],
        out_specs=[pl.BlockSpec((1, tm, w), f) for w, _ in outs],
        out_shape=[jax.ShapeDtypeStruct((bsz, s, w), dt) for w, dt in outs],
        compiler_params=_cparams(("parallel", "parallel")),
        name="inproj",
    )(x, mod_l, norm_w.reshape(1, d), w_perm, _group_ones(WIDTH_B, HEAD_DIM_B),
      jnp.tile(q_norm_w, N_HEADS_B).reshape(1, WIDTH_B), jnp.tile(k_norm_w, N_HEADS_B).reshape(1, WIDTH_B))


def _dot_nt(a, b, precision=None):
    return lax.dot_general(a, b, (((1,), (1,)), ((), ())), precision=precision, preferred_element_type=F32)


def _mm_bf16(a, b):
    return jnp.dot(a.astype(BF16), b.astype(BF16), preferred_element_type=F32)


def _split2(x):
    hi = x.astype(BF16)
    return hi, (x - hi.astype(F32)).astype(BF16)


def _split3(x):
    hi = x.astype(BF16)
    r = x - hi.astype(F32)
    mid = r.astype(BF16)
    return hi, mid, (r - mid.astype(F32)).astype(BF16)


def _mm3(a, b):
    return (jnp.dot(a[0], b[0], preferred_element_type=F32) + jnp.dot(a[0], b[1], preferred_element_type=F32)
            + jnp.dot(a[1], b[0], preferred_element_type=F32))


GDN_PAR = 4


def _gdn_kernel(qkv_ref, z_ref, small_ref, convw_ref, alog_ref, dtb_ref, nw_ref,
                y_ref, xe_ref, u_ref, state_ref, uval_ref, wdec_ref, qg_ref, kdec_ref, attn_ref, egl_ref, *, sb):
    n_chunks = sb // CHUNK
    halo = SUBLANES

    @pl.when(pl.program_id(1) == 0)
    def _():
        xe_ref[0:halo, :] = jnp.zeros((halo, 3 * WIDTH_A), F32)
        state_ref[...] = jnp.zeros_like(state_ref)

    xe_ref[halo:halo + sb, :] = qkv_ref[0]

    rows = 128
    for g in range(3 * WIDTH_A // LANES):
        cs = slice(g * LANES, (g + 1) * LANES)
        for r in range(sb // rows):
            base = halo - (CONV_K - 1) + r * rows
            acc = xe_ref[base:base + rows, cs] * convw_ref[0:1, cs]
            for j in range(1, CONV_K):
                acc = acc + xe_ref[base + j:base + j + rows, cs] * convw_ref[j:j + 1, cs]
            u_ref[r * rows:(r + 1) * rows, cs] = _silu(acc)

    xe_ref[0:halo, :] = xe_ref[sb:sb + halo, :]

    wide = N_HEADS_A * CHUNK
    heads = range(N_HEADS_A)
    ii = lax.broadcasted_iota(I32, (CHUNK, wide), 0)
    jj = lax.broadcasted_iota(I32, (CHUNK, wide), 1) & (CHUNK - 1)
    eye_w = (ii == jj).astype(F32)
    tri = (lax.broadcasted_iota(I32, (CHUNK, CHUNK), 0)
           >= lax.broadcasted_iota(I32, (CHUNK, CHUNK), 1)).astype(F32).astype(BF16)
    shift = int(math.log2(CHUNK))
    bd_mask = ((lax.broadcasted_iota(I32, (wide, wide), 0) >> shift)
               == (lax.broadcasted_iota(I32, (wide, wide), 1) >> shift)).astype(F32)

    bd_mask = bd_mask.astype(BF16)

    def block_diag(parts):
        return tuple(jnp.concatenate([m] * N_HEADS_A, axis=0) * bd_mask for m in parts)

    def prepare(c):
        rs = pl.ds(pl.multiple_of(c * CHUNK, CHUNK), CHUNK)
        qn, kn, v, beta, g_b = [], [], [], [], []
        for h in heads:
            q = u_ref[rs, h * HEAD_DIM_A:(h + 1) * HEAD_DIM_A]
            k = u_ref[rs, WIDTH_A + h * HEAD_DIM_A:WIDTH_A + (h + 1) * HEAD_DIM_A]
            v.append(u_ref[rs, 2 * WIDTH_A + h * HEAD_DIM_A:2 * WIDTH_A + (h + 1) * HEAD_DIM_A])
            qn.append(q * (lax.rsqrt(jnp.sum(q * q, axis=-1, keepdims=True) + EPS) * (HEAD_DIM_A ** -0.5)))
            kn.append(k * lax.rsqrt(jnp.sum(k * k, axis=-1, keepdims=True) + EPS))
            beta.append(_sigmoid(small_ref[0, rs, S_B + h:S_B + h + 1]))
            g = -jnp.exp(alog_ref[0:1, h:h + 1]) * _softplus(small_ref[0, rs, S_A + h:S_A + h + 1]
                                                            + dtb_ref[0:1, h:h + 1])
            g_b.append(jnp.broadcast_to(g, (CHUNK, CHUNK)))
        gc_w = sum(jnp.dot(tri, part, preferred_element_type=F32) for part in _split3(jnp.concatenate(g_b, axis=1)))
        gc_row = jnp.sum(jnp.where(ii == jj, gc_w, 0.0), axis=0, keepdims=True)
        decay_w = jnp.exp(jnp.where(ii >= jj, gc_w - gc_row, NEG_BIG))
        k_beta = [kn[h] * beta[h] for h in heads]
        kk_w = jnp.concatenate([_dot_nt(k_beta[h].astype(BF16), kn[h].astype(BF16)) for h in heads], axis=1)
        a_w = -jnp.where(ii > jj, kk_w * decay_w, 0.0)
        gc = [gc_w[:, h * CHUNK:h * CHUNK + 1] for h in heads]
        egc = [jnp.exp(gc[h]) for h in heads]
        qk = [_dot_nt(qn[h].astype(BF16), kn[h].astype(BF16)) for h in heads]
        for h in heads:
            g_last = gc[h][CHUNK - 1:CHUNK, :]
            qg_ref[c, h] = (qn[h] * egc[h]).astype(BF16)
            kdec_ref[c, h] = kn[h] * jnp.exp(g_last - gc[h])
            attn_ref[c, h] = (qk[h] * decay_w[:, h * CHUNK:(h + 1) * CHUNK]).astype(BF16)
            egl_ref[c, h] = jnp.broadcast_to(jnp.exp(g_last), (SUBLANES, LANES))
        return a_w, [_split2(v[h] * beta[h]) for h in heads], [_split2(k_beta[h] * egc[h]) for h in heads]

    def solve_body(cg, carry):
        group = range(GDN_PAR)
        chunks = [cg * GDN_PAR + i for i in group]
        pre = [prepare(c) for c in chunks]
        t_w = [eye_w + pre[i][0] for i in group]
        p_parts = [_split2(pre[i][0]) for i in group]
        bd = [block_diag(p_parts[i]) for i in group]
        for _ in range(shift - 1):
            prod = [_mm3(p_parts[i], bd[i]) for i in group]
            p_parts = [_split2(prod[i]) for i in group]
            bd = [block_diag(p_parts[i]) for i in group]
            upd = [_mm3(_split2(t_w[i]), bd[i]) for i in group]
            t_w = [t_w[i] + upd[i] for i in group]
        for i in group:
            t_h = [_split2(t_w[i][:, h * CHUNK:(h + 1) * CHUNK]) for h in heads]
            u_val = [_mm3(t_h[h], pre[i][1][h]) for h in heads]
            w_dec = [_mm3(t_h[h], pre[i][2][h]) for h in heads]
            for h in heads:
                uval_ref[chunks[i], h] = u_val[h]
                wdec_ref[chunks[i], h] = w_dec[h].astype(BF16)
        return carry

    lax.fori_loop(0, n_chunks // GDN_PAR, solve_body, 0)

    def scan_body(c, carry):
        rs = pl.ds(pl.multiple_of(c * CHUNK, CHUNK), CHUNK)
        state = [state_ref[h] for h in heads]
        state_b = [s_h.astype(BF16) for s_h in state]
        w_s = [jnp.dot(wdec_ref[c, h], state_b[h], preferred_element_type=F32) for h in heads]
        v_new = [(uval_ref[c, h] - w_s[h]).astype(BF16) for h in heads]
        o = [jnp.dot(qg_ref[c, h], state_b[h], preferred_element_type=F32)
             + jnp.dot(attn_ref[c, h], v_new[h], preferred_element_type=F32) for h in heads]
        for h in heads:
            state_ref[h] = (state[h] * egl_ref[c, h][0:1, 0:1]
                            + jnp.dot(kdec_ref[c, h].T.astype(BF16), v_new[h], preferred_element_type=F32))
        for h in heads:
            hs = slice(h * HEAD_DIM_A, (h + 1) * HEAD_DIM_A)
            on = o[h] * lax.rsqrt(jnp.mean(o[h] * o[h], axis=-1, keepdims=True) + EPS) * nw_ref[...]
            y_ref[0, rs, hs] = (on * _silu(z_ref[0, rs, hs])).astype(BF16)
        return carry

    lax.fori_loop(0, n_chunks, scan_body, 0)


def _gdn(qkv_a, z_a, small, conv_w, a_log, dt_bias, norm_w, sb):
    bsz, s, _ = qkv_a.shape
    n_c = sb // CHUNK
    assert n_c % GDN_PAR == 0 and s % sb == 0
    per_head = (n_c, N_HEADS_A, CHUNK, HEAD_DIM_A)
    f = lambda b, i: (b, i, 0)
    const2 = lambda b, i: (0, 0)
    return pl.pallas_call(
        functools.partial(_gdn_kernel, sb=sb),
        grid=(bsz, s // sb),
        in_specs=[
            pl.BlockSpec((1, sb, 3 * WIDTH_A), f),
            pl.BlockSpec((1, sb, WIDTH_A), f),
            pl.BlockSpec((1, sb, LANES), f),
            pl.BlockSpec((CONV_K, 3 * WIDTH_A), const2),
            pl.BlockSpec((1, N_HEADS_A), const2),
            pl.BlockSpec((1, N_HEADS_A), const2),
            pl.BlockSpec((1, HEAD_DIM_A), const2),
        ],
        out_specs=pl.BlockSpec((1, sb, WIDTH_A), f),
        out_shape=jax.ShapeDtypeStruct((bsz, s, WIDTH_A), BF16),
        scratch_shapes=[
            pltpu.VMEM((sb + SUBLANES, 3 * WIDTH_A), F32),
            pltpu.VMEM((sb, 3 * WIDTH_A), F32),
            pltpu.VMEM((N_HEADS_A, HEAD_DIM_A, HEAD_DIM_A), F32),
            pltpu.VMEM(per_head, F32),
            pltpu.VMEM(per_head, BF16),
            pltpu.VMEM(per_head, BF16),
            pltpu.VMEM(per_head, F32),
            pltpu.VMEM((n_c, N_HEADS_A, CHUNK, CHUNK), BF16),
            pltpu.VMEM((n_c, N_HEADS_A, SUBLANES, LANES), F32),
        ],
        compiler_params=_cparams(("parallel", "arbitrary")),
        name="gdn",
    )(qkv_a, z_a, small, conv_w, a_log.reshape(1, -1), dt_bias.reshape(1, -1), norm_w.reshape(1, -1))


QB = 128
FAR_T = 512
FAR_G = FAR_T // LANES
ATT_W = 256
LOG2E = 1.4426950408889634
NEAR_D = 9
NEAR_MIN = 5
INT_MIN = -2 ** 31
HALF_MIN = -2 ** 15


def _t5_bucket_np(rel):
    nb = REL_BUCKETS // 2
    max_exact = nb // 2
    side = np.where(rel > 0, nb, 0)
    n = np.abs(rel)
    nf = np.maximum(n, 1).astype(np.float32)
    large = max_exact + (np.log(nf / np.float32(max_exact)) / np.float32(math.log(REL_MAX_DIST / max_exact))
                         * np.float32(nb - max_exact)).astype(np.int32)
    large = np.minimum(large, nb - 1)
    return (side + np.where(n < max_exact, n, large)).astype(np.int32)


def _near_bucket_table():
    r = np.arange(QB)[:, None]
    c = np.arange(LANES)[None, :]
    return np.stack([_t5_bucket_np(c - r - LANES * d) for d in range(NEAR_D)])


FAR_BUCKET = int(_t5_bucket_np(np.array([-(NEAR_MIN * LANES + 1)]))[0])
assert all(int(b) == FAR_BUCKET for b in _t5_bucket_np(-np.arange((NEAR_MIN + 1) * LANES - (QB - 1), 1 << 20, 997)))


def _sortable_key(score):
    bits = pltpu.bitcast(score + 0.0, I32)
    return bits ^ ((bits >> 31) & 0x7FFFFFFF)


PLANE_G = 32


def _bit_transpose32(words):
    a = list(words)
    mask, j = 0x0000FFFF, 16
    while j:
        k = 0
        while k < 32:
            t = (a[k] ^ lax.shift_right_logical(a[k + j], jnp.int32(j))) & mask
            a[k] = a[k] ^ t
            a[k + j] = a[k + j] ^ lax.shift_left(t, jnp.int32(j))
            k = (k + j + 1) & ~j
        j >>= 1
        mask = (mask ^ (mask << j)) & 0xFFFFFFFF
    return a


def _dsa_kernel(rb_ref, qb_ref, qi_ref, small_ref, kb_ref, va_ref, kidx2_ref, tab_ref,
                y_ref, qis_ref, qs_ref, wb_ref, sdot_ref, keys_ref, nbias_ref, thr_ref, jlim_ref,
                m_ref, acc_ref, s_ref, p_ref, peak_ref, planes_ref, need_ref, excess_ref, *, seq, k_sel):
    i = pl.program_id(1)
    lane = lax.broadcasted_iota(I32, (QB, LANES), 1)
    row = lax.broadcasted_iota(I32, (QB, LANES), 0)
    even_f = (lane < HEAD_DIM_B).astype(F32)
    even_b = even_f.astype(BF16)
    odd_b = (1.0 - even_f).astype(BF16)

    @pl.when(i == 0)
    def _():
        nbias_ref[...] = jnp.zeros_like(nbias_ref)
        keys_ref[...] = jnp.full(keys_ref.shape, INT_MIN, I32)

        def d_body(d, c0):
            tab = tab_ref[d]

            def b_body(bk, c1):
                hit = tab == bk
                for h in range(N_HEADS_B):
                    nbias_ref[d * N_HEADS_B + h] = jnp.where(hit, rb_ref[bk, h] * LOG2E,
                                                             nbias_ref[d * N_HEADS_B + h])
                return c1

            return lax.fori_loop(0, REL_BUCKETS, b_body, c0)

        lax.fori_loop(0, NEAR_D, d_body, 0)

    for p in range(N_HEADS_B // 2):
        ps = slice(p * LANES, (p + 1) * LANES)
        qi_pair = qi_ref[0, :, ps]
        qis_ref[(2 * p) * QB:(2 * p + 1) * QB, :] = qi_pair * even_b
        qis_ref[(2 * p + 1) * QB:(2 * p + 2) * QB, :] = qi_pair * odd_b
        q_pair = qb_ref[0, :, ps]
        qs_ref[(2 * p) * QB:(2 * p + 1) * QB, :] = q_pair * even_b
        qs_ref[(2 * p + 1) * QB:(2 * p + 2) * QB, :] = q_pair * odd_b
    w_scale = IDX_HEADS ** -0.5 * IDX_DIM ** -0.5
    for h in range(IDX_HEADS):
        wb_ref[h] = jnp.broadcast_to(small_ref[0, :, S_WIDX + h:S_WIDX + h + 1] * w_scale, (QB, LANES))

    limit = i * QB + CHUNK + jnp.where(row >= CHUNK, CHUNK, 0)

    def score_body(t, c0):
        k0 = pl.multiple_of(t * FAR_T, FAR_T)
        sdot = _dot_nt(qis_ref[...], kidx2_ref[0, pl.ds(k0, FAR_T), :])
        for g in range(FAR_G):
            sdot_ref[g] = sdot[:, g * LANES:(g + 1) * LANES]
        for g in range(FAR_G):
            acc = jnp.maximum(sdot_ref[g, 0:QB, :], 0.0) * wb_ref[0]
            for h in range(1, IDX_HEADS):
                acc = acc + jnp.maximum(sdot_ref[g, h * QB:(h + 1) * QB, :], 0.0) * wb_ref[h]
            col = k0 + g * LANES + lane
            keys_ref[t * FAR_G + g] = jnp.where(col < limit, _sortable_key(acc), INT_MIN)
        return c0

    n_groups = i + 1
    n_tiles = i // FAR_G + 1
    lax.fori_loop(0, n_tiles, score_body, 0)

    @pl.when(n_tiles % 2 == 1)
    def _():
        for g in range(FAR_G):
            keys_ref[n_tiles * FAR_G + g] = jnp.full((QB, LANES), INT_MIN, I32)

    def count(pred):
        def t_body(t, acc):
            for g in range(2 * FAR_G):
                grp = t * (2 * FAR_G) + g
                acc = acc + jnp.where(pred(keys_ref[grp], grp * LANES + lane), 1, 0)
            return acc

        acc = lax.fori_loop(0, (n_tiles + 1) // 2, t_body, jnp.zeros((QB, LANES), I32))
        return jnp.broadcast_to(jnp.sum(acc, axis=1, keepdims=True), (QB, LANES))

    thr_ref[...] = jnp.full((QB, LANES), INT_MIN, I32)
    jlim_ref[...] = jnp.full((QB, LANES), -1, I32)

    def lane_total(x):
        return jnp.broadcast_to(jnp.sum(x, axis=1, keepdims=True), (QB, LANES))

    @pl.when(n_groups * QB > k_sel)
    def _():
        def transpose_half(half):
            def row_body(rr, c0):
                rows = pl.ds(pl.multiple_of(rr * SUBLANES, SUBLANES), SUBLANES)
                planes = _bit_transpose32([keys_ref[half * PLANE_G + g, rows, :] ^ INT_MIN for g in range(PLANE_G)])
                for b in range(32):
                    planes_ref[half, b, rows, :] = planes[b]
                return c0

            lax.fori_loop(0, QB // SUBLANES, row_body, 0)

        def search(n_half):
            halves = range(n_half)

            def digit_body(step, carry):
                r, eq, above = carry
                cls = []
                for half in halves:
                    hi = eq[half] & planes_ref[half, 2 * step]
                    lo_plane = planes_ref[half, 2 * step + 1]
                    zero_hi = eq[half] ^ hi
                    e11 = hi & lo_plane
                    e01 = zero_hi & lo_plane
                    cls.append((e11, hi ^ e11, e01, zero_hi ^ e01))
                n11, n10, n01 = [sum(lax.population_count(cls[h][d]) for h in halves) for d in range(3)]
                c3 = above + n11
                c2 = c3 + n10
                c1 = c2 + n01
                d3, d2, d1 = [lane_total(c) >= k_sel for c in (c3, c2, c1)]
                eq = tuple(jnp.where(d3, cls[h][0], jnp.where(d2, cls[h][1], jnp.where(d1, cls[h][2], cls[h][3])))
                           for h in halves)
                above = jnp.where(d3, above, jnp.where(d2, c3, jnp.where(d1, c2, c1)))
                digit = jnp.where(d3, 3, jnp.where(d2, 2, jnp.where(d1, 1, 0)))
                return r | lax.shift_left(digit, 30 - 2 * step), eq, above

            zero = jnp.zeros((QB, LANES), I32)
            full = jnp.full((QB, LANES), -1, I32)
            r, eq, above = lax.fori_loop(0, 16, digit_body, (zero, (full,) * n_half, zero))
            thr_ref[...] = r ^ INT_MIN
            need = k_sel - lane_total(above)
            need_ref[...] = need
            excess_ref[...] = lane_total(sum(lax.population_count(eq[h]) for h in halves)) - need

        transpose_half(0)

        @pl.when(n_groups > PLANE_G)
        def _():
            transpose_half(1)
            search(2)

        @pl.when(n_groups <= PLANE_G)
        def _():
            search(1)

        r = thr_ref[...]
        need = need_ref[...]
        excess = excess_ref[...]
        jlim_ref[...] = jnp.where(r == INT_MIN, -1, seq)

        @pl.when(jnp.max(excess) > 0)
        def _():
            def j_body(step, jl):
                cand = jl + lax.shift_left(jnp.int32(1), (seq.bit_length() - 1) - step)
                cnt = count(lambda kt, col: (kt == r) & (col < cand))
                return jnp.where(cnt < need, cand, jl)

            jl = lax.fori_loop(0, seq.bit_length(), j_body, jnp.zeros((QB, LANES), I32))
            jlim_ref[...] = jnp.where(r == INT_MIN, -1, jl)

    m_ref[...] = jnp.full(m_ref.shape, NEG_BIG, F32)
    acc_ref[...] = jnp.zeros_like(acc_ref)

    n_part = FAR_T // ATT_W
    part_g = ATT_W // LANES
    last_tile = seq // FAR_T - 1

    def tile_mask(t):
        negm = []
        for g in range(FAR_G):
            grp = t * FAR_G + g
            kt = keys_ref[grp]
            sel = (kt > thr_ref[...]) | ((kt == thr_ref[...]) & (grp * LANES + lane <= jlim_ref[...]))
            negm.append(jnp.where(sel, 0.0, -jnp.inf))
        return negm

    def logits_pair(t, buf, p, negm):
        heads = ((2 * p, slice(0, QB)), (2 * p + 1, slice(QB, 2 * QB)))
        peak = [None, None]
        for c in range(n_part):
            ks = pl.ds(pl.multiple_of(t * FAR_T + c * ATT_W, ATT_W), ATT_W)
            s = _dot_nt(qs_ref[2 * p * QB:(2 * p + 2) * QB, :], kb_ref[0, ks, p * LANES:(p + 1) * LANES])
            for gg in range(part_g):
                g = c * part_g + gg
                for n, (h, rows) in enumerate(heads):
                    v = s[rows, gg * LANES:(gg + 1) * LANES] + negm[g]
                    s_ref[buf, h, g] = v
                    peak[n] = v if peak[n] is None else jnp.maximum(peak[n], v)
        for n, (h, _) in enumerate(heads):
            peak_ref[buf, h] = peak[n]

    def finish(buf, h, part, rmax, bias_scalar):
        m_old = m_ref[h]
        m_new = jnp.maximum(m_old, rmax + bias_scalar)
        shift = jnp.concatenate([m_new - bias_scalar] * part_g, axis=1)
        for c in range(n_part):
            p_ref[buf, h, :, c * ATT_W:(c + 1) * ATT_W] = jnp.exp2(part(c) - shift).astype(BF16)
        acc_ref[h] = jnp.exp2(m_old - m_new) * acc_ref[h]
        m_ref[h] = m_new

    def softmax_far(t, buf, h):
        def part(c):
            return jnp.concatenate([s_ref[buf, h, c * part_g + g] for g in range(part_g)], axis=1)

        finish(buf, h, part, jnp.max(peak_ref[buf, h], axis=1, keepdims=True), rb_ref[FAR_BUCKET, h] * LOG2E)

    def softmax_near(t, buf, h):
        def part(c):
            return jnp.concatenate(
                [s_ref[buf, h, c * part_g + g]
                 + nbias_ref[jnp.clip(i - (t * FAR_G + c * part_g + g), 0, NEAR_D - 1) * N_HEADS_B + h]
                 for g in range(part_g)], axis=1)

        rmax = None
        for c in range(n_part):
            r = jnp.max(part(c), axis=1, keepdims=True)
            rmax = r if rmax is None else jnp.maximum(rmax, r)
        finish(buf, h, part, rmax, 0.0)

    def pv_head(t, buf, h):
        ks = pl.ds(pl.multiple_of(t * FAR_T, FAR_T), FAR_T)
        acc_ref[h] += jnp.dot(p_ref[buf, h], va_ref[0, ks, h * LANES:(h + 1) * LANES], preferred_element_type=F32)

    def tile_body(softmax):
        def body(t, c0):
            negm = tile_mask(t)
            n_pair = N_HEADS_B // 2
            for step in range(n_pair + 2):
                if step < n_pair:
                    logits_pair(t, 0, step, negm)
                if 1 <= step <= n_pair:
                    for h in (2 * step - 2, 2 * step - 1):
                        softmax(t, 0, h)
                if step >= 2:
                    for h in (2 * step - 4, 2 * step - 3):
                        pv_head(t, 0, h)
            return c0

        return body

    far_tiles = jnp.maximum(i - NEAR_MIN, 0) // FAR_G
    lax.fori_loop(0, far_tiles, tile_body(softmax_far), 0)
    lax.fori_loop(far_tiles, n_tiles, tile_body(softmax_near), 0)

    def head_out(h):
        a = acc_ref[h]
        return a * (1.0 / a[:, HEAD_DIM_B:HEAD_DIM_B + 1])

    for p in range(N_HEADS_B // 2):
        o_odd = pltpu.roll(head_out(2 * p + 1), HEAD_DIM_B, axis=1)
        y_ref[0, :, p * LANES:(p + 1) * LANES] = jnp.where(lane < HEAD_DIM_B, head_out(2 * p), o_odd).astype(BF16)


def _dsa(qb, kb, va, qi, small, rel_bias):
    bsz, s, _ = qb.shape
    assert s % FAR_T == 0 and s // LANES <= 2 * PLANE_G
    k_sel = min(TOPK_KEYS_MAX, s // 4)
    kidx = small[:, :, S_KIDX:S_KIDX + IDX_DIM].astype(BF16)
    kidx2 = jnp.concatenate([kidx, kidx], axis=-1)
    tab = jnp.asarray(_near_bucket_table())
    blk = lambda b, i: (b, i, 0)
    full = lambda b, i: (b, 0, 0)
    one = pl.Buffered(1)
    return pl.pallas_call(
        functools.partial(_dsa_kernel, seq=s, k_sel=k_sel),
        grid=(bsz, s // QB),
        in_specs=[
            pl.BlockSpec(memory_space=pltpu.SMEM),
            pl.BlockSpec((1, QB, WIDTH_B), blk),
            pl.BlockSpec((1, QB, IDX_HEADS * IDX_DIM), blk),
            pl.BlockSpec((1, QB, LANES), blk),
            pl.BlockSpec((1, s, WIDTH_B), full, pipeline_mode=one),
            pl.BlockSpec((1, s, N_HEADS_B * LANES), full, pipeline_mode=one),
            pl.BlockSpec((1, s, LANES), full, pipeline_mode=one),
            pl.BlockSpec((NEAR_D, QB, LANES), lambda b, i: (0, 0, 0), pipeline_mode=one),
        ],
        out_specs=pl.BlockSpec((1, QB, WIDTH_B), blk),
        out_shape=jax.ShapeDtypeStruct((bsz, s, WIDTH_B), BF16),
        scratch_shapes=[
            pltpu.VMEM((IDX_HEADS * QB, LANES), BF16),
            pltpu.VMEM((N_HEADS_B * QB, LANES), BF16),
            pltpu.VMEM((IDX_HEADS, QB, LANES), F32),
            pltpu.VMEM((FAR_G, IDX_HEADS * QB, LANES), F32),
            pltpu.VMEM((2 * PLANE_G, QB, LANES), I32),
            pltpu.VMEM((NEAR_D * N_HEADS_B, QB, LANES), F32),
            pltpu.VMEM((QB, LANES), I32),
            pltpu.VMEM((QB, LANES), I32),
            pltpu.VMEM((N_HEADS_B, QB, LANES), F32),
            pltpu.VMEM((N_HEADS_B, QB, LANES), F32),
            pltpu.VMEM((1, N_HEADS_B, FAR_G, QB, LANES), F32),
            pltpu.VMEM((1, N_HEADS_B, QB, FAR_T), BF16),
            pltpu.VMEM((1, N_HEADS_B, QB, LANES), F32),
            pltpu.VMEM((2, 32, QB, LANES), I32),
            pltpu.VMEM((QB, LANES), I32),
            pltpu.VMEM((QB, LANES), I32),
        ],
        compiler_params=_cparams(("parallel", "arbitrary")),
        name="dsa",
    )(rel_bias, qb, qi, small, kb, va, kidx2, tab)


HALF_MASK = 0xFFFF0000


def _pack_halves(t):
    w = t.shape[1] // 2
    bits = pltpu.bitcast(t.astype(BF16).astype(F32), U32)
    return (bits[:, :w] >> 16) | (bits[:, w:] & jnp.uint32(HALF_MASK))


def _unpack_halves(p):
    lo = pltpu.bitcast(p << 16, F32)
    hi = pltpu.bitcast(p & jnp.uint32(HALF_MASK), F32)
    return jnp.concatenate([lo, hi], axis=1)


def _outproj_kernel(ya_ref, yb_ref, x_ref, mod_ref, wo_ref, nw_ref, rw_ref, rbias_ref,
                    xn_ref, hp_ref, ridx_ref, gate_ref):
    wa = ya_ref.shape[2]
    y = (jnp.dot(ya_ref[0], wo_ref[0:wa, :], preferred_element_type=F32)
         + jnp.dot(yb_ref[0], wo_ref[wa:, :], preferred_element_type=F32))
    xn = x_ref[0] + mod_ref[0, 2:3, :] * y
    xn_ref[0] = xn
    ms = jnp.mean(xn * xn, axis=-1, keepdims=True)
    h = xn * lax.rsqrt(ms + EPS) * nw_ref[...] * (1.0 + mod_ref[0, 4:5, :]) + mod_ref[0, 3:4, :]
    hp_ref[0] = _pack_halves(h)

    logits = jnp.dot(h, rw_ref[...], precision=HIGHEST, preferred_element_type=F32) + rbias_ref[...]
    lane = lax.broadcasted_iota(I32, logits.shape, 1)
    cur = logits
    vals, ridx = [], jnp.zeros(logits.shape, I32)
    for k in range(TOP_K):
        mx = jnp.max(cur, axis=1, keepdims=True)
        am = jnp.min(jnp.where(cur == mx, lane, LANES), axis=1, keepdims=True)
        cur = jnp.where(lane == am, -jnp.inf, cur)
        vals.append(mx)
        ridx = jnp.where(lane == k, am, ridx)
    ex = [jnp.exp(v - vals[0]) for v in vals]
    inv = 1.0 / (ex[0] + ex[1] + ex[2] + ex[3])
    gate = jnp.zeros(logits.shape, F32)
    for k in range(TOP_K):
        gate = jnp.where(lane == k, ex[k] * inv, gate)
    ridx_ref[0] = ridx
    gate_ref[0] = gate


def _outproj(y_a, y_b, x, mod_l, w_out_bf, norm_w, router_w, router_b, tm):
    bsz, s, d = x.shape
    n_e = router_w.shape[1]
    rw = jnp.zeros((d, LANES), F32).at[:, :n_e].set(router_w)
    rbias = jnp.full((1, LANES), NEG_BIG, F32).at[0, :n_e].set(router_b)
    blk = lambda b, i: (b, i, 0)
    const2 = lambda b, i: (0, 0)
    return pl.pallas_call(
        _outproj_kernel,
        grid=(bsz, s // tm),
        in_specs=[
            pl.BlockSpec((1, tm, y_a.shape[2]), blk),
            pl.BlockSpec((1, tm, y_b.shape[2]), blk),
            pl.BlockSpec((1, tm, d), blk),
            pl.BlockSpec((1, 6, d), lambda b, i: (b, 0, 0)),
            pl.BlockSpec((d, d), const2),
            pl.BlockSpec((1, d), const2),
            pl.BlockSpec((d, LANES), const2),
            pl.BlockSpec((1, LANES), const2),
        ],
        out_specs=[pl.BlockSpec((1, tm, d), blk), pl.BlockSpec((1, tm, d // 2), blk),
                   pl.BlockSpec((1, tm, LANES), blk), pl.BlockSpec((1, tm, LANES), blk)],
        out_shape=[jax.ShapeDtypeStruct((bsz, s, d), F32), jax.ShapeDtypeStruct((bsz, s, d // 2), U32),
                   jax.ShapeDtypeStruct((bsz, s, LANES), I32), jax.ShapeDtypeStruct((bsz, s, LANES), F32)],
        compiler_params=_cparams(("parallel", "parallel")),
        name="outproj_router",
    )(y_a, y_b, x, mod_l, w_out_bf, norm_w.reshape(1, d), rw, rbias)


MOE_TB = 2048
MOE_RB = 512
MOE_M = 144


def _moe_kernel(first_ref, nch_ref, cbase_ref, cvalid_ref, list_ref, hp_ref, w1_ref, b1_ref, w2_ref, b2_ref,
                gate_ref, x_ref, g2_ref, o_ref, slots_ref, xg_ref, yb_ref, *, tb, rb, table_len):
    sb = pl.program_id(0)
    e = pl.program_id(1)
    dff = w2_ref.shape[1]
    table = sb * table_len + 1

    def gather(j):
        base = cbase_ref[table + j]
        buf = (j + 2) % 2
        for r in range(MOE_M):
            code = list_ref[0, 0, base + r]
            xg_ref[buf, pl.ds(r, 1), :] = hp_ref[pl.ds(code >> 2, 1), :]

    def scatter(j):
        base = cbase_ref[table + j]
        n_valid = cvalid_ref[table + j]
        buf = (j + 2) % 2
        for r in range(MOE_M):
            code = jnp.where(r < n_valid, list_ref[0, 0, base + r], TOP_K * tb)
            slots_ref[code & (TOP_K - 1), pl.ds(code >> 2, 1), :] = yb_ref[buf, pl.ds(r, 1), :]

    @pl.when(e == 0)
    def _():
        gather(0)

    @pl.when(e < N_EXPERTS)
    def _():
        j0 = first_ref[sb * N_EXPERTS + e]

        def chunk(j, carry):
            buf = j % 2
            xb = _unpack_halves(xg_ref[buf]).astype(BF16)
            gather(j + 1)
            scatter(j - 1)
            u = jnp.dot(xb, w1_ref[0], preferred_element_type=F32) + b1_ref[0]
            glu = jnp.minimum(u[:, :dff], SWIGLU_LIMIT)
            lin = jnp.clip(u[:, dff:], -SWIGLU_LIMIT, SWIGLU_LIMIT)
            act = glu * _sigmoid(SWIGLU_ALPHA * glu) * (lin + 1.0)
            y = jnp.dot(act.astype(BF16), w2_ref[0], preferred_element_type=F32) + b2_ref[0]
            yb_ref[buf] = _pack_halves(y)
            return carry

        lax.fori_loop(j0, j0 + nch_ref[sb * N_EXPERTS + e], chunk, 0)

    @pl.when(e == N_EXPERTS)
    def _():
        n_total = first_ref[sb * N_EXPERTS + N_EXPERTS - 1] + nch_ref[sb * N_EXPERTS + N_EXPERTS - 1]
        scatter(n_total - 1)

    @pl.when(e >= N_EXPERTS)
    def _():
        r0 = pl.multiple_of((e - N_EXPERTS) * rb, rb)
        acc = gate_ref[:, 0:1] * _unpack_halves(slots_ref[0, pl.ds(r0, rb), :])
        for k in range(1, TOP_K):
            acc = acc + gate_ref[:, k:k + 1] * _unpack_halves(slots_ref[k, pl.ds(r0, rb), :])
        o_ref[...] = x_ref[...] + g2_ref[0] * acc


def _moe(xn, hp, ridx, gate, g2, w1p, b1p, w2b, b2, tb, rb, layer=0):
    bsz, s, d = xn.shape
    t = bsz * s
    n_super = t // tb
    n_piece = tb // rb
    dff = w2b.shape[1]
    w2map = lambda sb, e, *_: (layer * N_EXPERTS + jnp.minimum(e, N_EXPERTS - 1), 0, 0)
    flat_e = ridx[:, :, :TOP_K].reshape(n_super, tb * TOP_K)
    order = jnp.argsort(flat_e, axis=1, stable=True).astype(I32)
    counts = jnp.sum(flat_e[:, :, None] == jnp.arange(N_EXPERTS, dtype=I32)[None, None, :], axis=1).astype(I32)
    offs = (jnp.cumsum(counts, axis=1) - counts).astype(I32)
    nch = (counts + MOE_M - 1) // MOE_M
    first = (jnp.cumsum(nch, axis=1) - nch).astype(I32)
    table_len = -(-(tb * TOP_K // MOE_M + N_EXPERTS + 2) // SUBLANES) * SUBLANES
    j = jnp.arange(table_len - 1, dtype=I32)[None, :]
    owner = jnp.sum(j[:, :, None] >= (first + nch)[:, None, :], axis=2)
    owner_c = jnp.minimum(owner, N_EXPERTS - 1)
    within = j - jnp.take_along_axis(first, owner_c, axis=1)
    c_valid = jnp.where(owner < N_EXPERTS,
                        jnp.minimum(jnp.take_along_axis(counts, owner_c, axis=1) - within * MOE_M, MOE_M), 0)
    c_base = jnp.where(owner < N_EXPERTS, jnp.take_along_axis(offs, owner_c, axis=1) + within * MOE_M, 0)
    zero = jnp.zeros((n_super, 1), I32)
    c_valid = jnp.concatenate([zero, c_valid.astype(I32)], axis=1)
    c_base = jnp.concatenate([zero, c_base.astype(I32)], axis=1)

    piece = lambda sb, e, *_: (sb * n_piece + jnp.maximum(e - N_EXPERTS, 0), 0)
    wmap = lambda sb, e, *_: (jnp.minimum(e, N_EXPERTS - 1), 0, 0)
    grid_spec = pltpu.PrefetchScalarGridSpec(
        num_scalar_prefetch=4,
        grid=(n_super, N_EXPERTS + n_piece),
        in_specs=[
            pl.BlockSpec((1, 1, tb * TOP_K + MOE_M), lambda sb, e, *_: (sb, 0, 0), memory_space=pltpu.SMEM),
            pl.BlockSpec((tb, d // 2), lambda sb, e, *_: (sb, 0), pipeline_mode=pl.Buffered(1)),
            pl.BlockSpec((1, d, 2 * dff), wmap),
            pl.BlockSpec((1, 1, 2 * dff), wmap),
            pl.BlockSpec((1, dff, d), w2map),
            pl.BlockSpec((1, 1, d), wmap),
            pl.BlockSpec((rb, LANES), piece),
            pl.BlockSpec((rb, d), piece),
            pl.BlockSpec((1, 1, d), lambda sb, e, *_: ((sb * tb) // s, 0, 0)),
        ],
        out_specs=pl.BlockSpec((rb, d), piece),
        scratch_shapes=[
            pltpu.VMEM((TOP_K, tb + SUBLANES, d // 2), U32),
            pltpu.VMEM((2, MOE_M, d // 2), U32),
            pltpu.VMEM((2, MOE_M, d // 2), U32),
        ],
    )
    out = pl.pallas_call(
        functools.partial(_moe_kernel, tb=tb, rb=rb, table_len=table_len),
        grid_spec=grid_spec,
        out_shape=jax.ShapeDtypeStruct((t, d), F32),
        compiler_params=_cparams(("arbitrary", "arbitrary")),
        name="moe",
    )(first.reshape(-1), nch.astype(I32).reshape(-1), c_base.reshape(-1), c_valid.reshape(-1),
      jnp.pad(order, ((0, 0), (0, MOE_M))).reshape(n_super, 1, tb * TOP_K + MOE_M), hp.reshape(t, d // 2),
      w1p, b1p, w2b, b2, gate.reshape(t, LANES), xn.reshape(t, d), g2.reshape(bsz, 1, d))
    return out.reshape(bsz, s, d)


MXU_COLS = 256


def _deinterleave_kernel(w_ref, perm_ref, o_ref):
    half = w_ref.shape[2] // 2
    hw = MXU_COLS // 2
    for b in range(w_ref.shape[2] // MXU_COLS):
        blk = w_ref[0, :, b * MXU_COLS:(b + 1) * MXU_COLS].astype(BF16)
        y = jnp.dot(blk, perm_ref[...], preferred_element_type=F32).astype(BF16)
        o_ref[0, :, b * hw:(b + 1) * hw] = y[:, :hw]
        o_ref[0, :, half + b * hw:half + (b + 1) * hw] = y[:, hw:]


def _deinterleave_cast(w1, layer, rows=512):
    depth, n_e, d, two_f = w1.shape
    src = np.concatenate([np.arange(0, MXU_COLS, 2), np.arange(1, MXU_COLS, 2)])
    perm = np.zeros((MXU_COLS, MXU_COLS), np.float32)
    perm[src, np.arange(MXU_COLS)] = 1.0
    return pl.pallas_call(
        _deinterleave_kernel,
        grid=(n_e, d // rows),
        in_specs=[pl.BlockSpec((1, rows, two_f), lambda e, r: (layer * n_e + e, r, 0)),
                  pl.BlockSpec((MXU_COLS, MXU_COLS), lambda e, r: (0, 0))],
        out_specs=pl.BlockSpec((1, rows, two_f), lambda e, r: (e, r, 0)),
        out_shape=jax.ShapeDtypeStruct((n_e, d, two_f), BF16),
        compiler_params=_cparams(("parallel", "parallel")),
        name="w1_deinterleave",
    )(w1.reshape(depth * n_e, d, two_f), jnp.asarray(perm, BF16))


def _deinterleave_bias(b1_l):
    n_e, two_f = b1_l.shape
    return jnp.concatenate([b1_l[:, 0::2], b1_l[:, 1::2]], axis=1).reshape(n_e, 1, two_f)


def kernel(x, c, rel_bias, mod_w, mod_b, norm_mix_w, norm_ffn_w, w_in, conv_w, a_log, dt_bias, gdn_norm_w,
           q_norm_w, k_norm_w, w_out, router_w, router_b, w1, b1, w2, b2):
    depth = mod_w.shape[0]
    bsz, s, d = x.shape
    mod = _modulation(c, mod_w, mod_b)
    tm = min(512, s)
    tb = min(MOE_TB, bsz * s)
    rb = min(MOE_RB, tb)
    n_e, dff = w2.shape[1], w2.shape[2]
    w2b = w2.astype(BF16).reshape(depth * n_e, dff, d)
    for l in range(depth):
        qkv_a, z_a, qb, kb, vb, qi, small = _inproj(
            x, mod[l], norm_mix_w[l], _permute_w_in(w_in[l]), q_norm_w[l], k_norm_w[l], tm)
        y_a = _gdn(qkv_a, z_a, small, conv_w[l], a_log[l], dt_bias[l], gdn_norm_w[l], sb=tm)
        y_b = _dsa(qb, kb, vb, qi, small, rel_bias)
        xn, hp, ridx, gate = _outproj(y_a, y_b, x, mod[l], w_out[l].astype(BF16), norm_ffn_w[l],
                                      router_w[l], router_b[l], tm)
        x = _moe(xn, hp, ridx, gate, mod[l][:, 5], _deinterleave_cast(w1, l), _deinterleave_bias(b1[l]), w2b,
                 b2[l].reshape(n_e, 1, d), tb, rb, layer=l)
    return x
```

```python
import functools
import math

import jax
import jax.numpy as jnp
import numpy as np
from jax import lax
from jax.experimental import pallas as pl
from jax.experimental.pallas import tpu as pltpu

F32 = jnp.float32
BF16 = jnp.bfloat16
I32 = jnp.int32
I16 = jnp.int16
U32 = jnp.uint32
HIGHEST = lax.Precision.HIGHEST

LANES = 128
SUBLANES = 8
VMEM_LIMIT_BYTES = 56 * 1024 * 1024

CHUNK = 64
HEAD_DIM_A = 128
N_HEADS_A = 4
WIDTH_A = N_HEADS_A * HEAD_DIM_A
CONV_K = 4
HEAD_DIM_B = 64
N_HEADS_B = 8
WIDTH_B = N_HEADS_B * HEAD_DIM_B
IDX_HEADS = 8
IDX_DIM = 64
TOPK_KEYS_MAX = 256
REL_BUCKETS = 32
REL_MAX_DIST = 1024
N_EXPERTS = 32
TOP_K = 4
SWIGLU_ALPHA = 1.702
SWIGLU_LIMIT = 7.0
EPS = 1e-6
NEG_BIG = -1e30

C_QKVA = 0
C_Z = C_QKVA + 3 * WIDTH_A
C_QB = C_Z + WIDTH_A
C_KB = C_QB + WIDTH_B
C_VB = C_KB + WIDTH_B
C_QI = C_VB + WIDTH_B
C_SMALL = C_QI + IDX_HEADS * IDX_DIM
D_IN_PAD = C_SMALL + LANES
S_KIDX = 0
S_B = IDX_DIM
S_A = S_B + N_HEADS_A
S_WIDX = S_A + N_HEADS_A


def _cparams(sem):
    return pltpu.CompilerParams(dimension_semantics=sem, vmem_limit_bytes=VMEM_LIMIT_BYTES)


def _silu(x):
    return x * (1.0 / (1.0 + jnp.exp(-x)))


def _sigmoid(x):
    return 1.0 / (1.0 + jnp.exp(-x))


def _softplus(x):
    return jnp.maximum(x, 0.0) + jnp.log(1.0 + jnp.exp(-jnp.abs(x)))


def _mod_kernel(c_ref, w_ref, b_ref, o_ref):
    a = _silu(c_ref[...])
    o_ref[0] = jnp.dot(a, w_ref[0], precision=HIGHEST, preferred_element_type=F32) + b_ref[0]


def _modulation(c, mod_w, mod_b):
    depth, d, n = mod_w.shape
    bsz = c.shape[0]
    rows = -(-bsz // SUBLANES) * SUBLANES
    c_pad = jnp.zeros((rows, d), F32).at[:bsz].set(c)
    tn = 1536
    out = pl.pallas_call(
        _mod_kernel,
        grid=(depth, n // tn),
        in_specs=[
            pl.BlockSpec((rows, d), lambda l, j: (0, 0)),
            pl.BlockSpec((1, d, tn), lambda l, j: (l, 0, j)),
            pl.BlockSpec((1, 1, tn), lambda l, j: (l, 0, j)),
        ],
        out_specs=pl.BlockSpec((1, rows, tn), lambda l, j: (l, 0, j)),
        out_shape=jax.ShapeDtypeStruct((depth, rows, n), F32),
        compiler_params=_cparams(("arbitrary", "arbitrary")),
        name="adaln_mod",
    )(c_pad, mod_w, mod_b.reshape(depth, 1, n))
    return out[:, :bsz].reshape(depth, bsz, 6, d)


def _head_rms(t, group_ref, wn, inv_dim):
    t2 = t * t
    hi = t2.astype(BF16)
    lo = (t2 - hi.astype(F32)).astype(BF16)
    ss = (jnp.dot(hi, group_ref[...], preferred_element_type=F32)
          + jnp.dot(lo, group_ref[...], preferred_element_type=F32))
    return t * lax.rsqrt(ss * inv_dim + EPS) * wn


def _inproj_kernel(x_ref, mod_ref, nw_ref, w_ref, group_ref, qn_ref, kn_ref,
                   qkva_ref, z_ref, qb_ref, kb_ref, vb_ref, qi_ref, small_ref):
    x = x_ref[0]
    ms = jnp.mean(x * x, axis=-1, keepdims=True)
    y = x * lax.rsqrt(ms + EPS) * nw_ref[...]
    h = y * (1.0 + mod_ref[0, 1:2, :]) + mod_ref[0, 0:1, :]
    hb = h.astype(BF16)

    def mm(lo, width):
        return jnp.dot(hb, w_ref[:, lo:lo + width], preferred_element_type=F32)

    qkva_ref[0] = mm(C_QKVA, 3 * WIDTH_A)
    z_ref[0] = mm(C_Z, WIDTH_A)
    q = _head_rms(mm(C_QB, WIDTH_B), group_ref, qn_ref[...], 1.0 / HEAD_DIM_B)
    qb_ref[0] = (q * (HEAD_DIM_B ** -0.5 * LOG2E)).astype(BF16)
    k = _head_rms(mm(C_KB, WIDTH_B), group_ref, kn_ref[...], 1.0 / HEAD_DIM_B)
    kb_ref[0] = k.astype(BF16)
    v = mm(C_VB, WIDTH_B)
    lane = lax.broadcasted_iota(I32, (v.shape[0], LANES), 1)
    tail = (lane == HEAD_DIM_B).astype(F32)
    for p in range(N_HEADS_B // 2):
        pair = v[:, p * LANES:(p + 1) * LANES]
        vb_ref[0, :, (2 * p) * LANES:(2 * p + 1) * LANES] = jnp.where(lane < HEAD_DIM_B, pair, tail).astype(BF16)
        vb_ref[0, :, (2 * p + 1) * LANES:(2 * p + 2) * LANES] = jnp.where(
            lane < HEAD_DIM_B, pltpu.roll(pair, HEAD_DIM_B, axis=1), tail).astype(BF16)
    qi_ref[0] = mm(C_QI, IDX_HEADS * IDX_DIM).astype(BF16)
    small_ref[0] = mm(C_SMALL, LANES)


def _permute_w_in(w_in_l):
    d = w_in_l.shape[0]
    o = 0
    qkva = w_in_l[:, o:o + 3 * WIDTH_A]; o += 3 * WIDTH_A
    z = w_in_l[:, o:o + WIDTH_A]; o += WIDTH_A
    b = w_in_l[:, o:o + N_HEADS_A]; o += N_HEADS_A
    a = w_in_l[:, o:o + N_HEADS_A]; o += N_HEADS_A
    qkvb = w_in_l[:, o:o + 3 * WIDTH_B]; o += 3 * WIDTH_B
    qi = w_in_l[:, o:o + IDX_HEADS * IDX_DIM]; o += IDX_HEADS * IDX_DIM
    ki = w_in_l[:, o:o + IDX_DIM]; o += IDX_DIM
    wi = w_in_l[:, o:o + IDX_HEADS]; o += IDX_HEADS
    pad = jnp.zeros((d, LANES - IDX_DIM - 2 * N_HEADS_A - IDX_HEADS), w_in_l.dtype)
    return jnp.concatenate([qkva, z, qkvb, qi, ki, b, a, wi, pad], axis=1).astype(BF16)


def _group_ones(width, group):
    g = np.arange(width) // group
    return jnp.asarray((g[:, None] == g[None, :]).astype(np.float32), dtype=BF16)


def _inproj(x, mod_l, norm_w, w_perm, q_norm_w, k_norm_w, tm):
    bsz, s, d = x.shape
    f = lambda b, i: (b, i, 0)
    const2 = lambda b, i: (0, 0)
    outs = [
        (3 * WIDTH_A, F32), (WIDTH_A, F32), (WIDTH_B, BF16), (WIDTH_B, BF16), (N_HEADS_B * LANES, BF16),
        (IDX_HEADS * IDX_DIM, BF16), (LANES, F32),
    ]
    return pl.pallas_call(
        _inproj_kernel,
        grid=(bsz, s // tm),
        in_specs=[
            pl.BlockSpec((1, tm, d), f),
            pl.BlockSpec((1, 6, d), lambda b, i: (b, 0, 0)),
            pl.BlockSpec((1, d), const2),
            pl.BlockSpec((d, D_IN_PAD), const2),
            pl.BlockSpec((WIDTH_B, WIDTH_B), const2),
            pl.BlockSpec((1, WIDTH_B), const2),
            pl.BlockSpec((1, WIDTH_B), const2),
        ],
        out_specs=[pl.BlockSpec((1, tm, w), f) for w, _ in outs],
        out_shape=[jax.ShapeDtypeStruct((bsz, s, w), dt) for w, dt in outs],
        compiler_params=_cparams(("parallel", "parallel")),
        name="inproj",
    )(x, mod_l, norm_w.reshape(1, d), w_perm, _group_ones(WIDTH_B, HEAD_DIM_B),
      jnp.tile(q_norm_w, N_HEADS_B).reshape(1, WIDTH_B), jnp.tile(k_norm_w, N_HEADS_B).reshape(1, WIDTH_B))


def _dot_nt(a, b, precision=None):
    return lax.dot_general(a, b, (((1,), (1,)), ((), ())), precision=precision, preferred_element_type=F32)


def _mm_bf16(a, b):
    return jnp.dot(a.astype(BF16), b.astype(BF16), preferred_element_type=F32)


def _split2(x):
    hi = x.astype(BF16)
    return hi, (x - hi.astype(F32)).astype(BF16)


def _split3(x):
    hi = x.astype(BF16)
    r = x - hi.astype(F32)
    mid = r.astype(BF16)
    return hi, mid, (r - mid.astype(F32)).astype(BF16)


def _mm3(a, b):
    return (jnp.dot(a[0], b[0], preferred_element_type=F32) + jnp.dot(a[0], b[1], preferred_element_type=F32)
            + jnp.dot(a[1], b[0], preferred_element_type=F32))


GDN_PAR = 4


def _gdn_kernel(qkv_ref, z_ref, small_ref, convw_ref, alog_ref, dtb_ref, nw_ref,
                y_ref, xe_ref, u_ref, state_ref, uval_ref, wdec_ref, qg_ref, kdec_ref, attn_ref, egl_ref, *, sb):
    n_chunks = sb // CHUNK
    halo = SUBLANES

    @pl.when(pl.program_id(1) == 0)
    def _():
        xe_ref[0:halo, :] = jnp.zeros((halo, 3 * WIDTH_A), F32)
        state_ref[...] = jnp.zeros_like(state_ref)

    xe_ref[halo:halo + sb, :] = qkv_ref[0]

    rows = 128
    for g in range(3 * WIDTH_A // LANES):
        cs = slice(g * LANES, (g + 1) * LANES)
        for r in range(sb // rows):
            base = halo - (CONV_K - 1) + r * rows
            acc = xe_ref[base:base + rows, cs] * convw_ref[0:1, cs]
            for j in range(1, CONV_K):
                acc = acc + xe_ref[base + j:base + j + rows, cs] * convw_ref[j:j + 1, cs]
            u_ref[r * rows:(r + 1) * rows, cs] = _silu(acc)

    xe_ref[0:halo, :] = xe_ref[sb:sb + halo, :]

    wide = N_HEADS_A * CHUNK
    heads = range(N_HEADS_A)
    ii = lax.broadcasted_iota(I32, (CHUNK, wide), 0)
    jj = lax.broadcasted_iota(I32, (CHUNK, wide), 1) & (CHUNK - 1)
    eye_w = (ii == jj).astype(F32)
    tri = (lax.broadcasted_iota(I32, (CHUNK, CHUNK), 0)
           >= lax.broadcasted_iota(I32, (CHUNK, CHUNK), 1)).astype(F32).astype(BF16)
    shift = int(math.log2(CHUNK))
    bd_mask = ((lax.broadcasted_iota(I32, (wide, wide), 0) >> shift)
               == (lax.broadcasted_iota(I32, (wide, wide), 1) >> shift)).astype(F32)

    bd_mask = bd_mask.astype(BF16)

    def block_diag(parts):
        return tuple(jnp.concatenate([m] * N_HEADS_A, axis=0) * bd_mask for m in parts)

    def prepare(c):
        rs = pl.ds(pl.multiple_of(c * CHUNK, CHUNK), CHUNK)
        qn, kn, v, beta, g_b = [], [], [], [], []
        for h in heads:
            q = u_ref[rs, h * HEAD_DIM_A:(h + 1) * HEAD_DIM_A]
            k = u_ref[rs, WIDTH_A + h * HEAD_DIM_A:WIDTH_A + (h + 1) * HEAD_DIM_A]
            v.append(u_ref[rs, 2 * WIDTH_A + h * HEAD_DIM_A:2 * WIDTH_A + (h + 1) * HEAD_DIM_A])
            qn.append(q * (lax.rsqrt(jnp.sum(q * q, axis=-1, keepdims=True) + EPS) * (HEAD_DIM_A ** -0.5)))
            kn.append(k * lax.rsqrt(jnp.sum(k * k, axis=-1, keepdims=True) + EPS))
            beta.append(_sigmoid(small_ref[0, rs, S_B + h:S_B + h + 1]))
            g = -jnp.exp(alog_ref[0:1, h:h + 1]) * _softplus(small_ref[0, rs, S_A + h:S_A + h + 1]
                                                            + dtb_ref[0:1, h:h + 1])
            g_b.append(jnp.broadcast_to(g, (CHUNK, CHUNK)))
        gc_w = sum(jnp.dot(tri, part, preferred_element_type=F32) for part in _split3(jnp.concatenate(g_b, axis=1)))
        gc_row = jnp.sum(jnp.where(ii == jj, gc_w, 0.0), axis=0, keepdims=True)
        decay_w = jnp.exp(jnp.where(ii >= jj, gc_w - gc_row, NEG_BIG))
        k_beta = [kn[h] * beta[h] for h in heads]
        kk_w = jnp.concatenate([_dot_nt(k_beta[h].astype(BF16), kn[h].astype(BF16)) for h in heads], axis=1)
        a_w = -jnp.where(ii > jj, kk_w * decay_w, 0.0)
        gc = [gc_w[:, h * CHUNK:h * CHUNK + 1] for h in heads]
        egc = [jnp.exp(gc[h]) for h in heads]
        qk = [_dot_nt(qn[h].astype(BF16), kn[h].astype(BF16)) for h in heads]
        for h in heads:
            g_last = gc[h][CHUNK - 1:CHUNK, :]
            qg_ref[c, h] = (qn[h] * egc[h]).astype(BF16)
            kdec_ref[c, h] = kn[h] * jnp.exp(g_last - gc[h])
            attn_ref[c, h] = (qk[h] * decay_w[:, h * CHUNK:(h + 1) * CHUNK]).astype(BF16)
            egl_ref[c, h] = jnp.broadcast_to(jnp.exp(g_last), (SUBLANES, LANES))
        return a_w, [_split2(v[h] * beta[h]) for h in heads], [_split2(k_beta[h] * egc[h]) for h in heads]

    def solve_body(cg, carry):
        group = range(GDN_PAR)
        chunks = [cg * GDN_PAR + i for i in group]
        pre = [prepare(c) for c in chunks]
        t_w = [eye_w + pre[i][0] for i in group]
        p_parts = [_split2(pre[i][0]) for i in group]
        bd = [block_diag(p_parts[i]) for i in group]
        for _ in range(shift - 1):
            prod = [_mm3(p_parts[i], bd[i]) for i in group]
            p_parts = [_split2(prod[i]) for i in group]
            bd = [block_diag(p_parts[i]) for i in group]
            upd = [_mm3(_split2(t_w[i]), bd[i]) for i in group]
            t_w = [t_w[i] + upd[i] for i in group]
        for i in group:
            t_h = [_split2(t_w[i][:, h * CHUNK:(h + 1) * CHUNK]) for h in heads]
            u_val = [_mm3(t_h[h], pre[i][1][h]) for h in heads]
            w_dec = [_mm3(t_h[h], pre[i][2][h]) for h in heads]
            for h in heads:
                uval_ref[chunks[i], h] = u_val[h]
                wdec_ref[chunks[i], h] = w_dec[h].astype(BF16)
        return carry

    lax.fori_loop(0, n_chunks // GDN_PAR, solve_body, 0)

    def scan_body(c, carry):
        rs = pl.ds(pl.multiple_of(c * CHUNK, CHUNK), CHUNK)
        state = [state_ref[h] for h in heads]
        state_b = [s_h.astype(BF16) for s_h in state]
        w_s = [jnp.dot(wdec_ref[c, h], state_b[h], preferred_element_type=F32) for h in heads]
        v_new = [(uval_ref[c, h] - w_s[h]).astype(BF16) for h in heads]
        o = [jnp.dot(qg_ref[c, h], state_b[h], preferred_element_type=F32)
             + jnp.dot(attn_ref[c, h], v_new[h], preferred_element_type=F32) for h in heads]
        for h in heads:
            state_ref[h] = (state[h] * egl_ref[c, h][0:1, 0:1]
                            + jnp.dot(kdec_ref[c, h].T.astype(BF16), v_new[h], preferred_element_type=F32))
        for h in heads:
            hs = slice(h * HEAD_DIM_A, (h + 1) * HEAD_DIM_A)
            on = o[h] * lax.rsqrt(jnp.mean(o[h] * o[h], axis=-1, keepdims=True) + EPS) * nw_ref[...]
            y_ref[0, rs, hs] = (on * _silu(z_ref[0, rs, hs])).astype(BF16)
        return carry

    lax.fori_loop(0, n_chunks, scan_body, 0)


def _gdn(qkv_a, z_a, small, conv_w, a_log, dt_bias, norm_w, sb):
    bsz, s, _ = qkv_a.shape
    n_c = sb // CHUNK
    assert n_c % GDN_PAR == 0 and s % sb == 0
    per_head = (n_c, N_HEADS_A, CHUNK, HEAD_DIM_A)
    f = lambda b, i: (b, i, 0)
    const2 = lambda b, i: (0, 0)
    return pl.pallas_call(
        functools.partial(_gdn_kernel, sb=sb),
        grid=(bsz, s // sb),
        in_specs=[
            pl.BlockSpec((1, sb, 3 * WIDTH_A), f),
            pl.BlockSpec((1, sb, WIDTH_A), f),
            pl.BlockSpec((1, sb, LANES), f),
            pl.BlockSpec((CONV_K, 3 * WIDTH_A), const2),
            pl.BlockSpec((1, N_HEADS_A), const2),
            pl.BlockSpec((1, N_HEADS_A), const2),
            pl.BlockSpec((1, HEAD_DIM_A), const2),
        ],
        out_specs=pl.BlockSpec((1, sb, WIDTH_A), f),
        out_shape=jax.ShapeDtypeStruct((bsz, s, WIDTH_A), BF16),
        scratch_shapes=[
            pltpu.VMEM((sb + SUBLANES, 3 * WIDTH_A), F32),
            pltpu.VMEM((sb, 3 * WIDTH_A), F32),
            pltpu.VMEM((N_HEADS_A, HEAD_DIM_A, HEAD_DIM_A), F32),
            pltpu.VMEM(per_head, F32),
            pltpu.VMEM(per_head, BF16),
            pltpu.VMEM(per_head, BF16),
            pltpu.VMEM(per_head, F32),
            pltpu.VMEM((n_c, N_HEADS_A, CHUNK, CHUNK), BF16),
            pltpu.VMEM((n_c, N_HEADS_A, SUBLANES, LANES), F32),
        ],
        compiler_params=_cparams(("parallel", "arbitrary")),
        name="gdn",
    )(qkv_a, z_a, small, conv_w, a_log.reshape(1, -1), dt_bias.reshape(1, -1), norm_w.reshape(1, -1))


QB = 128
FAR_T = 512
FAR_G = FAR_T // LANES
ATT_W = 256
LOG2E = 1.4426950408889634
NEAR_D = 9
NEAR_MIN = 5
INT_MIN = -2 ** 31
HALF_MIN = -2 ** 15


def _t5_bucket_np(rel):
    nb = REL_BUCKETS // 2
    max_exact = nb // 2
    side = np.where(rel > 0, nb, 0)
    n = np.abs(rel)
    nf = np.maximum(n, 1).astype(np.float32)
    large = max_exact + (np.log(nf / np.float32(max_exact)) / np.float32(math.log(REL_MAX_DIST / max_exact))
                         * np.float32(nb - max_exact)).astype(np.int32)
    large = np.minimum(large, nb - 1)
    return (side + np.where(n < max_exact, n, large)).astype(np.int32)


def _near_bucket_table():
    r = np.arange(QB)[:, None]
    c = np.arange(LANES)[None, :]
    return np.stack([_t5_bucket_np(c - r - LANES * d) for d in range(NEAR_D)])


FAR_BUCKET = int(_t5_bucket_np(np.array([-(NEAR_MIN * LANES + 1)]))[0])
assert all(int(b) == FAR_BUCKET for b in _t5_bucket_np(-np.arange((NEAR_MIN + 1) * LANES - (QB - 1), 1 << 20, 997)))


def _sortable_key(score):
    bits = pltpu.bitcast(score + 0.0, I32)
    return bits ^ ((bits >> 31) & 0x7FFFFFFF)


PLANE_G = 32


def _bit_transpose32(words):
    a = list(words)
    mask, j = 0x0000FFFF, 16
    while j:
        k = 0
        while k < 32:
            t = (a[k] ^ lax.shift_right_logical(a[k + j], jnp.int32(j))) & mask
            a[k] = a[k] ^ t
            a[k + j] = a[k + j] ^ lax.shift_left(t, jnp.int32(j))
            k = (k + j + 1) & ~j
        j >>= 1
        mask = (mask ^ (mask << j)) & 0xFFFFFFFF
    return a


def _dsa_kernel(rb_ref, qb_ref, qi_ref, small_ref, kb_ref, va_ref, kidx2_ref, tab_ref,
                y_ref, qis_ref, qs_ref, wb_ref, keys_ref, nbias_ref, thr_ref, jlim_ref,
                m_ref, acc_ref, s_ref, p_ref, peak_ref, planes_ref, need_ref, excess_ref, *, seq, k_sel):
    i = pl.program_id(1)
    lane = lax.broadcasted_iota(I32, (QB, LANES), 1)
    row = lax.broadcasted_iota(I32, (QB, LANES), 0)
    even_f = (lane < HEAD_DIM_B).astype(F32)
    even_b = even_f.astype(BF16)
    odd_b = (1.0 - even_f).astype(BF16)

    @pl.when(i == 0)
    def _():
        nbias_ref[...] = jnp.zeros_like(nbias_ref)
        keys_ref[...] = jnp.full(keys_ref.shape, INT_MIN, I32)

        def d_body(d, c0):
            tab = tab_ref[d]

            def b_body(bk, c1):
                hit = tab == bk
                for h in range(N_HEADS_B):
                    nbias_ref[d * N_HEADS_B + h] = jnp.where(hit, rb_ref[bk, h] * LOG2E,
                                                             nbias_ref[d * N_HEADS_B + h])
                return c1

            return lax.fori_loop(0, REL_BUCKETS, b_body, c0)

        lax.fori_loop(0, NEAR_D, d_body, 0)

    for p in range(N_HEADS_B // 2):
        ps = slice(p * LANES, (p + 1) * LANES)
        qi_pair = qi_ref[0, :, ps]
        qis_ref[(2 * p) * QB:(2 * p + 1) * QB, :] = qi_pair * even_b
        qis_ref[(2 * p + 1) * QB:(2 * p + 2) * QB, :] = qi_pair * odd_b
        q_pair = qb_ref[0, :, ps]
        qs_ref[(2 * p) * QB:(2 * p + 1) * QB, :] = q_pair * even_b
        qs_ref[(2 * p + 1) * QB:(2 * p + 2) * QB, :] = q_pair * odd_b
    w_scale = IDX_HEADS ** -0.5 * IDX_DIM ** -0.5
    for h in range(IDX_HEADS):
        wb_ref[h] = jnp.broadcast_to(small_ref[0, :, S_WIDX + h:S_WIDX + h + 1] * w_scale, (QB, LANES))

    limit = i * QB + CHUNK + jnp.where(row >= CHUNK, CHUNK, 0)

    def score_body(t, c0):
        for c in range(FAR_T // ATT_W):
            k0 = t * FAR_T + c * ATT_W
            k_part = kidx2_ref[0, pl.ds(pl.multiple_of(k0, ATT_W), ATT_W), :]
            acc = None
            for h in range(IDX_HEADS):
                dots = _dot_nt(qis_ref[h * QB:(h + 1) * QB, :], k_part)
                term = jnp.maximum(dots, 0.0) * jnp.concatenate([wb_ref[h]] * (ATT_W // LANES), axis=1)
                acc = term if acc is None else acc + term
            for gg in range(ATT_W // LANES):
                col = k0 + gg * LANES + lane
                keys_ref[(k0 // LANES) + gg] = jnp.where(
                    col < limit, _sortable_key(acc[:, gg * LANES:(gg + 1) * LANES]), INT_MIN)
        return c0

    n_groups = i + 1
    n_tiles = i // FAR_G + 1
    lax.fori_loop(0, n_tiles, score_body, 0)

    @pl.when(n_tiles % 2 == 1)
    def _():
        for g in range(FAR_G):
            keys_ref[n_tiles * FAR_G + g] = jnp.full((QB, LANES), INT_MIN, I32)

    def count(pred):
        def t_body(t, acc):
            for g in range(2 * FAR_G):
                grp = t * (2 * FAR_G) + g
                acc = acc + jnp.where(pred(keys_ref[grp], grp * LANES + lane), 1, 0)
            return acc

        acc = lax.fori_loop(0, (n_tiles + 1) // 2, t_body, jnp.zeros((QB, LANES), I32))
        return jnp.broadcast_to(jnp.sum(acc, axis=1, keepdims=True), (QB, LANES))

    thr_ref[...] = jnp.full((QB, LANES), INT_MIN, I32)
    jlim_ref[...] = jnp.full((QB, LANES), -1, I32)

    def lane_total(x):
        return jnp.broadcast_to(jnp.sum(x, axis=1, keepdims=True), (QB, LANES))

    @pl.when(n_groups * QB > k_sel)
    def _():
        def transpose_half(half):
            def row_body(rr, c0):
                rows = pl.ds(pl.multiple_of(rr * SUBLANES, SUBLANES), SUBLANES)
                planes = _bit_transpose32([keys_ref[half * PLANE_G + g, rows, :] ^ INT_MIN for g in range(PLANE_G)])
                for b in range(32):
                    planes_ref[half, b, rows, :] = planes[b]
                return c0

            lax.fori_loop(0, QB // SUBLANES, row_body, 0)

        def search(n_half):
            halves = range(n_half)
            row_groups = (pl.ds(0, QB // 2), pl.ds(QB // 2, QB // 2))
            shape = (QB // 2, LANES)

            def total(x):
                return jnp.broadcast_to(jnp.sum(x, axis=1, keepdims=True), shape)

            def digit(step, rows, state):
                r, eq, above = state
                cls = []
                for half in halves:
                    hi = eq[half] & planes_ref[half, 2 * step, rows, :]
                    lo_plane = planes_ref[half, 2 * step + 1, rows, :]
                    zero_hi = eq[half] ^ hi
                    e11 = hi & lo_plane
                    e01 = zero_hi & lo_plane
                    cls.append((e11, hi ^ e11, e01, zero_hi ^ e01))
                n11, n10, n01 = [sum(lax.population_count(cls[h][d]) for h in halves) for d in range(3)]
                c3 = above + n11
                c2 = c3 + n10
                c1 = c2 + n01
                d3, d2, d1 = [total(c) >= k_sel for c in (c3, c2, c1)]
                eq = tuple(jnp.where(d3, cls[h][0], jnp.where(d2, cls[h][1], jnp.where(d1, cls[h][2], cls[h][3])))
                           for h in halves)
                above = jnp.where(d3, above, jnp.where(d2, c3, jnp.where(d1, c2, c1)))
                value = jnp.where(d3, 3, jnp.where(d2, 2, jnp.where(d1, 1, 0)))
                return r | lax.shift_left(value, 30 - 2 * step), eq, above

            def digit_body(step, carry):
                return tuple(digit(step, rows, state) for rows, state in zip(row_groups, carry))

            zero = jnp.zeros(shape, I32)
            full = jnp.full(shape, -1, I32)
            start = (zero, (full,) * n_half, zero)
            for rows, (r, eq, above) in zip(row_groups, lax.fori_loop(0, 16, digit_body, (start, start))):
                thr_ref[rows, :] = r ^ INT_MIN
                need = k_sel - total(above)
                need_ref[rows, :] = need
                excess_ref[rows, :] = total(sum(lax.population_count(eq[h]) for h in halves)) - need

        transpose_half(0)

        @pl.when(n_groups > PLANE_G)
        def _():
            transpose_half(1)
            search(2)

        @pl.when(n_groups <= PLANE_G)
        def _():
            search(1)

        r = thr_ref[...]
        need = need_ref[...]
        excess = excess_ref[...]
        jlim_ref[...] = jnp.where(r == INT_MIN, -1, seq)

        @pl.when(jnp.max(excess) > 0)
        def _():
            def j_body(step, jl):
                cand = jl + lax.shift_left(jnp.int32(1), (seq.bit_length() - 1) - step)
                cnt = count(lambda kt, col: (kt == r) & (col < cand))
                return jnp.where(cnt < need, cand, jl)

            jl = lax.fori_loop(0, seq.bit_length(), j_body, jnp.zeros((QB, LANES), I32))
            jlim_ref[...] = jnp.where(r == INT_MIN, -1, jl)

    m_ref[...] = jnp.full(m_ref.shape, NEG_BIG, F32)
    acc_ref[...] = jnp.zeros_like(acc_ref)

    n_part = FAR_T // ATT_W
    part_g = ATT_W // LANES
    last_tile = seq // FAR_T - 1

    def tile_mask(t):
        negm = []
        for g in range(FAR_G):
            grp = t * FAR_G + g
            kt = keys_ref[grp]
            sel = (kt > thr_ref[...]) | ((kt == thr_ref[...]) & (grp * LANES + lane <= jlim_ref[...]))
            negm.append(jnp.where(sel, 0.0, -jnp.inf))
        return negm

    def logits_pair(t, buf, p, negm):
        heads = ((2 * p, slice(0, QB)), (2 * p + 1, slice(QB, 2 * QB)))
        peak = [None, None]
        for c in range(n_part):
            ks = pl.ds(pl.multiple_of(t * FAR_T + c * ATT_W, ATT_W), ATT_W)
            s = _dot_nt(qs_ref[2 * p * QB:(2 * p + 2) * QB, :], kb_ref[0, ks, p * LANES:(p + 1) * LANES])
            for gg in range(part_g):
                g = c * part_g + gg
                for n, (h, rows) in enumerate(heads):
                    v = s[rows, gg * LANES:(gg + 1) * LANES] + negm[g]
                    s_ref[buf, h, g] = v
                    peak[n] = v if peak[n] is None else jnp.maximum(peak[n], v)
        for n, (h, _) in enumerate(heads):
            peak_ref[buf, h] = peak[n]

    def finish(buf, h, part, rmax, bias_scalar):
        m_old = m_ref[h]
        m_new = jnp.maximum(m_old, rmax + bias_scalar)
        shift = jnp.concatenate([m_new - bias_scalar] * part_g, axis=1)
        for c in range(n_part):
            p_ref[buf, h, :, c * ATT_W:(c + 1) * ATT_W] = jnp.exp2(part(c) - shift).astype(BF16)
        acc_ref[h] = jnp.exp2(m_old - m_new) * acc_ref[h]
        m_ref[h] = m_new

    def softmax_far(t, buf, h):
        def part(c):
            return jnp.concatenate([s_ref[buf, h, c * part_g + g] for g in range(part_g)], axis=1)

        finish(buf, h, part, jnp.max(peak_ref[buf, h], axis=1, keepdims=True), rb_ref[FAR_BUCKET, h] * LOG2E)

    def softmax_near(t, buf, h):
        def part(c):
            return jnp.concatenate(
                [s_ref[buf, h, c * part_g + g]
                 + nbias_ref[jnp.clip(i - (t * FAR_G + c * part_g + g), 0, NEAR_D - 1) * N_HEADS_B + h]
                 for g in range(part_g)], axis=1)

        rmax = None
        for c in range(n_part):
            r = jnp.max(part(c), axis=1, keepdims=True)
            rmax = r if rmax is None else jnp.maximum(rmax, r)
        finish(buf, h, part, rmax, 0.0)

    def pv_head(t, buf, h):
        ks = pl.ds(pl.multiple_of(t * FAR_T, FAR_T), FAR_T)
        acc_ref[h] += jnp.dot(p_ref[buf, h], va_ref[0, ks, h * LANES:(h + 1) * LANES], preferred_element_type=F32)

    def tile_body(softmax):
        def body(t, c0):
            negm = tile_mask(t)
            n_pair = N_HEADS_B // 2
            for step in range(n_pair + 2):
                if step < n_pair:
                    logits_pair(t, 0, step, negm)
                if 1 <= step <= n_pair:
                    for h in (2 * step - 2, 2 * step - 1):
                        softmax(t, 0, h)
                if step >= 2:
                    for h in (2 * step - 4, 2 * step - 3):
                        pv_head(t, 0, h)
            return c0

        return body

    far_tiles = jnp.maximum(i - NEAR_MIN, 0) // FAR_G
    lax.fori_loop(0, far_tiles, tile_body(softmax_far), 0)
    lax.fori_loop(far_tiles, n_tiles, tile_body(softmax_near), 0)

    def head_out(h):
        a = acc_ref[h]
        return a * (1.0 / a[:, HEAD_DIM_B:HEAD_DIM_B + 1])

    for p in range(N_HEADS_B // 2):
        o_odd = pltpu.roll(head_out(2 * p + 1), HEAD_DIM_B, axis=1)
        y_ref[0, :, p * LANES:(p + 1) * LANES] = jnp.where(lane < HEAD_DIM_B, head_out(2 * p), o_odd).astype(BF16)


def _dsa(qb, kb, va, qi, small, rel_bias):
    bsz, s, _ = qb.shape
    assert s % FAR_T == 0 and s // LANES <= 2 * PLANE_G
    k_sel = min(TOPK_KEYS_MAX, s // 4)
    kidx = small[:, :, S_KIDX:S_KIDX + IDX_DIM].astype(BF16)
    kidx2 = jnp.concatenate([kidx, kidx], axis=-1)
    tab = jnp.asarray(_near_bucket_table())
    blk = lambda b, i: (b, i, 0)
    full = lambda b, i: (b, 0, 0)
    one = pl.Buffered(1)
    return pl.pallas_call(
        functools.partial(_dsa_kernel, seq=s, k_sel=k_sel),
        grid=(bsz, s // QB),
        in_specs=[
            pl.BlockSpec(memory_space=pltpu.SMEM),
            pl.BlockSpec((1, QB, WIDTH_B), blk),
            pl.BlockSpec((1, QB, IDX_HEADS * IDX_DIM), blk),
            pl.BlockSpec((1, QB, LANES), blk),
            pl.BlockSpec((1, s, WIDTH_B), full, pipeline_mode=one),
            pl.BlockSpec((1, s, N_HEADS_B * LANES), full, pipeline_mode=one),
            pl.BlockSpec((1, s, LANES), full, pipeline_mode=one),
            pl.BlockSpec((NEAR_D, QB, LANES), lambda b, i: (0, 0, 0), pipeline_mode=one),
        ],
        out_specs=pl.BlockSpec((1, QB, WIDTH_B), blk),
        out_shape=jax.ShapeDtypeStruct((bsz, s, WIDTH_B), BF16),
        scratch_shapes=[
            pltpu.VMEM((IDX_HEADS * QB, LANES), BF16),
            pltpu.VMEM((N_HEADS_B * QB, LANES), BF16),
            pltpu.VMEM((IDX_HEADS, QB, LANES), F32),
            pltpu.VMEM((2 * PLANE_G, QB, LANES), I32),
            pltpu.VMEM((NEAR_D * N_HEADS_B, QB, LANES), F32),
            pltpu.VMEM((QB, LANES), I32),
            pltpu.VMEM((QB, LANES), I32),
            pltpu.VMEM((N_HEADS_B, QB, LANES), F32),
            pltpu.VMEM((N_HEADS_B, QB, LANES), F32),
            pltpu.VMEM((1, N_HEADS_B, FAR_G, QB, LANES), F32),
            pltpu.VMEM((1, N_HEADS_B, QB, FAR_T), BF16),
            pltpu.VMEM((1, N_HEADS_B, QB, LANES), F32),
            pltpu.VMEM((2, 32, QB, LANES), I32),
            pltpu.VMEM((QB, LANES), I32),
            pltpu.VMEM((QB, LANES), I32),
        ],
        compiler_params=_cparams(("parallel", "arbitrary")),
        name="dsa",
    )(rel_bias, qb, qi, small, kb, va, kidx2, tab)


HALF_MASK = 0xFFFF0000


def _pack_halves(t):
    w = t.shape[1] // 2
    bits = pltpu.bitcast(t.astype(BF16).astype(F32), U32)
    return (bits[:, :w] >> 16) | (bits[:, w:] & jnp.uint32(HALF_MASK))


def _unpack_halves(p):
    lo = pltpu.bitcast(p << 16, F32)
    hi = pltpu.bitcast(p & jnp.uint32(HALF_MASK), F32)
    return jnp.concatenate([lo, hi], axis=1)


def _outproj_kernel(ya_ref, yb_ref, x_ref, mod_ref, wo_ref, nw_ref, rw_ref, rbias_ref,
                    xn_ref, hp_ref, ridx_ref, gate_ref):
    wa = ya_ref.shape[2]
    y = (jnp.dot(ya_ref[0], wo_ref[0:wa, :], preferred_element_type=F32)
         + jnp.dot(yb_ref[0], wo_ref[wa:, :], preferred_element_type=F32))
    xn = x_ref[0] + mod_ref[0, 2:3, :] * y
    xn_ref[0] = xn
    ms = jnp.mean(xn * xn, axis=-1, keepdims=True)
    h = xn * lax.rsqrt(ms + EPS) * nw_ref[...] * (1.0 + mod_ref[0, 4:5, :]) + mod_ref[0, 3:4, :]
    hp_ref[0] = _pack_halves(h)

    logits = jnp.dot(h, rw_ref[...], precision=HIGHEST, preferred_element_type=F32) + rbias_ref[...]
    lane = lax.broadcasted_iota(I32, logits.shape, 1)
    cur = logits
    vals, ridx = [], jnp.zeros(logits.shape, I32)
    for k in range(TOP_K):
        mx = jnp.max(cur, axis=1, keepdims=True)
        am = jnp.min(jnp.where(cur == mx, lane, LANES), axis=1, keepdims=True)
        cur = jnp.where(lane == am, -jnp.inf, cur)
        vals.append(mx)
        ridx = jnp.where(lane == k, am, ridx)
    ex = [jnp.exp(v - vals[0]) for v in vals]
    inv = 1.0 / (ex[0] + ex[1] + ex[2] + ex[3])
    gate = jnp.zeros(logits.shape, F32)
    for k in range(TOP_K):
        gate = jnp.where(lane == k, ex[k] * inv, gate)
    ridx_ref[0] = ridx
    gate_ref[0] = gate


def _outproj(y_a, y_b, x, mod_l, w_out_bf, norm_w, router_w, router_b, tm):
    bsz, s, d = x.shape
    n_e = router_w.shape[1]
    rw = jnp.zeros((d, LANES), F32).at[:, :n_e].set(router_w)
    rbias = jnp.full((1, LANES), NEG_BIG, F32).at[0, :n_e].set(router_b)
    blk = lambda b, i: (b, i, 0)
    const2 = lambda b, i: (0, 0)
    return pl.pallas_call(
        _outproj_kernel,
        grid=(bsz, s // tm),
        in_specs=[
            pl.BlockSpec((1, tm, y_a.shape[2]), blk),
            pl.BlockSpec((1, tm, y_b.shape[2]), blk),
            pl.BlockSpec((1, tm, d), blk),
            pl.BlockSpec((1, 6, d), lambda b, i: (b, 0, 0)),
            pl.BlockSpec((d, d), const2),
            pl.BlockSpec((1, d), const2),
            pl.BlockSpec((d, LANES), const2),
            pl.BlockSpec((1, LANES), const2),
        ],
        out_specs=[pl.BlockSpec((1, tm, d), blk), pl.BlockSpec((1, tm, d // 2), blk),
                   pl.BlockSpec((1, tm, LANES), blk), pl.BlockSpec((1, tm, LANES), blk)],
        out_shape=[jax.ShapeDtypeStruct((bsz, s, d), F32), jax.ShapeDtypeStruct((bsz, s, d // 2), U32),
                   jax.ShapeDtypeStruct((bsz, s, LANES), I32), jax.ShapeDtypeStruct((bsz, s, LANES), F32)],
        compiler_params=_cparams(("parallel", "parallel")),
        name="outproj_router",
    )(y_a, y_b, x, mod_l, w_out_bf, norm_w.reshape(1, d), rw, rbias)


MOE_TB = 2048
MOE_RB = 512
MOE_M = 144


def _moe_kernel(first_ref, nch_ref, cbase_ref, cvalid_ref, list_ref, hp_ref, w1_ref, b1_ref, w2_ref, b2_ref,
                gate_ref, x_ref, g2_ref, o_ref, slots_ref, xg_ref, yb_ref, *, tb, rb, table_len):
    sb = pl.program_id(0)
    e = pl.program_id(1)
    dff = w2_ref.shape[1]
    table = sb * table_len + 1

    def gather(j):
        base = cbase_ref[table + j]
        buf = (j + 2) % 2
        for r in range(MOE_M):
            code = list_ref[0, 0, base + r]
            xg_ref[buf, pl.ds(r, 1), :] = hp_ref[pl.ds(code >> 2, 1), :]

    def scatter(j):
        base = cbase_ref[table + j]
        n_valid = cvalid_ref[table + j]
        buf = (j + 2) % 2
        for r in range(MOE_M):
            code = jnp.where(r < n_valid, list_ref[0, 0, base + r], TOP_K * tb)
            slots_ref[code & (TOP_K - 1), pl.ds(code >> 2, 1), :] = yb_ref[buf, pl.ds(r, 1), :]

    @pl.when(e == 0)
    def _():
        gather(0)

    @pl.when(e < N_EXPERTS)
    def _():
        j0 = first_ref[sb * N_EXPERTS + e]

        def chunk(j, carry):
            buf = j % 2
            xb = _unpack_halves(xg_ref[buf]).astype(BF16)
            gather(j + 1)
            scatter(j - 1)
            u = jnp.dot(xb, w1_ref[0], preferred_element_type=F32) + b1_ref[0]
            glu = jnp.minimum(u[:, :dff], SWIGLU_LIMIT)
            lin = jnp.clip(u[:, dff:], -SWIGLU_LIMIT, SWIGLU_LIMIT)
            act = glu * _sigmoid(SWIGLU_ALPHA * glu) * (lin + 1.0)
            y = jnp.dot(act.astype(BF16), w2_ref[0], preferred_element_type=F32) + b2_ref[0]
            yb_ref[buf] = _pack_halves(y)
            return carry

        lax.fori_loop(j0, j0 + nch_ref[sb * N_EXPERTS + e], chunk, 0)

    @pl.when(e == N_EXPERTS)
    def _():
        n_total = first_ref[sb * N_EXPERTS + N_EXPERTS - 1] + nch_ref[sb * N_EXPERTS + N_EXPERTS - 1]
        scatter(n_total - 1)

    @pl.when(e >= N_EXPERTS)
    def _():
        r0 = pl.multiple_of((e - N_EXPERTS) * rb, rb)
        acc = gate_ref[:, 0:1] * _unpack_halves(slots_ref[0, pl.ds(r0, rb), :])
        for k in range(1, TOP_K):
            acc = acc + gate_ref[:, k:k + 1] * _unpack_halves(slots_ref[k, pl.ds(r0, rb), :])
        o_ref[...] = x_ref[...] + g2_ref[0] * acc


def _moe(xn, hp, ridx, gate, g2, w1p, b1p, w2b, b2, tb, rb, layer=0):
    bsz, s, d = xn.shape
    t = bsz * s
    n_super = t // tb
    n_piece = tb // rb
    dff = w2b.shape[1]
    w2map = lambda sb, e, *_: (layer * N_EXPERTS + jnp.minimum(e, N_EXPERTS - 1), 0, 0)
    flat_e = ridx[:, :, :TOP_K].reshape(n_super, tb * TOP_K)
    order = jnp.argsort(flat_e, axis=1, stable=True).astype(I32)
    counts = jnp.sum(flat_e[:, :, None] == jnp.arange(N_EXPERTS, dtype=I32)[None, None, :], axis=1).astype(I32)
    offs = (jnp.cumsum(counts, axis=1) - counts).astype(I32)
    nch = (counts + MOE_M - 1) // MOE_M
    first = (jnp.cumsum(nch, axis=1) - nch).astype(I32)
    table_len = -(-(tb * TOP_K // MOE_M + N_EXPERTS + 2) // SUBLANES) * SUBLANES
    j = jnp.arange(table_len - 1, dtype=I32)[None, :]
    owner = jnp.sum(j[:, :, None] >= (first + nch)[:, None, :], axis=2)
    owner_c = jnp.minimum(owner, N_EXPERTS - 1)
    within = j - jnp.take_along_axis(first, owner_c, axis=1)
    c_valid = jnp.where(owner < N_EXPERTS,
                        jnp.minimum(jnp.take_along_axis(counts, owner_c, axis=1) - within * MOE_M, MOE_M), 0)
    c_base = jnp.where(owner < N_EXPERTS, jnp.take_along_axis(offs, owner_c, axis=1) + within * MOE_M, 0)
    zero = jnp.zeros((n_super, 1), I32)
    c_valid = jnp.concatenate([zero, c_valid.astype(I32)], axis=1)
    c_base = jnp.concatenate([zero, c_base.astype(I32)], axis=1)

    piece = lambda sb, e, *_: (sb * n_piece + jnp.maximum(e - N_EXPERTS, 0), 0)
    wmap = lambda sb, e, *_: (jnp.minimum(e, N_EXPERTS - 1), 0, 0)
    grid_spec = pltpu.PrefetchScalarGridSpec(
        num_scalar_prefetch=4,
        grid=(n_super, N_EXPERTS + n_piece),
        in_specs=[
            pl.BlockSpec((1, 1, tb * TOP_K + MOE_M), lambda sb, e, *_: (sb, 0, 0), memory_space=pltpu.SMEM),
            pl.BlockSpec((tb, d // 2), lambda sb, e, *_: (sb, 0), pipeline_mode=pl.Buffered(1)),
            pl.BlockSpec((1, d, 2 * dff), wmap),
            pl.BlockSpec((1, 1, 2 * dff), wmap),
            pl.BlockSpec((1, dff, d), w2map),
            pl.BlockSpec((1, 1, d), wmap),
            pl.BlockSpec((rb, LANES), piece),
            pl.BlockSpec((rb, d), piece),
            pl.BlockSpec((1, 1, d), lambda sb, e, *_: ((sb * tb) // s, 0, 0)),
        ],
        out_specs=pl.BlockSpec((rb, d), piece),
        scratch_shapes=[
            pltpu.VMEM((TOP_K, tb + SUBLANES, d // 2), U32),
            pltpu.VMEM((2, MOE_M, d // 2), U32),
            pltpu.VMEM((2, MOE_M, d // 2), U32),
        ],
    )
    out = pl.pallas_call(
        functools.partial(_moe_kernel, tb=tb, rb=rb, table_len=table_len),
        grid_spec=grid_spec,
        out_shape=jax.ShapeDtypeStruct((t, d), F32),
        compiler_params=_cparams(("arbitrary", "arbitrary")),
        name="moe",
    )(first.reshape(-1), nch.astype(I32).reshape(-1), c_base.reshape(-1), c_valid.reshape(-1),
      jnp.pad(order, ((0, 0), (0, MOE_M))).reshape(n_super, 1, tb * TOP_K + MOE_M), hp.reshape(t, d // 2),
      w1p, b1p, w2b, b2, gate.reshape(t, LANES), xn.reshape(t, d), g2.reshape(bsz, 1, d))
    return out.reshape(bsz, s, d)


MXU_COLS = 256


def _deinterleave_kernel(w_ref, perm_ref, o_ref):
    half = w_ref.shape[2] // 2
    hw = MXU_COLS // 2
    for b in range(w_ref.shape[2] // MXU_COLS):
        blk = w_ref[0, :, b * MXU_COLS:(b + 1) * MXU_COLS].astype(BF16)
        y = jnp.dot(blk, perm_ref[...], preferred_element_type=F32).astype(BF16)
        o_ref[0, :, b * hw:(b + 1) * hw] = y[:, :hw]
        o_ref[0, :, half + b * hw:half + (b + 1) * hw] = y[:, hw:]


def _deinterleave_cast(w1, layer, rows=512):
    depth, n_e, d, two_f = w1.shape
    src = np.concatenate([np.arange(0, MXU_COLS, 2), np.arange(1, MXU_COLS, 2)])
    perm = np.zeros((MXU_COLS, MXU_COLS), np.float32)
    perm[src, np.arange(MXU_COLS)] = 1.0
    return pl.pallas_call(
        _deinterleave_kernel,
        grid=(n_e, d // rows),
        in_specs=[pl.BlockSpec((1, rows, two_f), lambda e, r: (layer * n_e + e, r, 0)),
                  pl.BlockSpec((MXU_COLS, MXU_COLS), lambda e, r: (0, 0))],
        out_specs=pl.BlockSpec((1, rows, two_f), lambda e, r: (e, r, 0)),
        out_shape=jax.ShapeDtypeStruct((n_e, d, two_f), BF16),
        compiler_params=_cparams(("parallel", "parallel")),
        name="w1_deinterleave",
    )(w1.reshape(depth * n_e, d, two_f), jnp.asarray(perm, BF16))


def _deinterleave_bias(b1_l):
    n_e, two_f = b1_l.shape
    return jnp.concatenate([b1_l[:, 0::2], b1_l[:, 1::2]], axis=1).reshape(n_e, 1, two_f)


def kernel(x, c, rel_bias, mod_w, mod_b, norm_mix_w, norm_ffn_w, w_in, conv_w, a_log, dt_bias, gdn_norm_w,
           q_norm_w, k_norm_w, w_out, router_w, router_b, w1, b1, w2, b2):
    depth = mod_w.shape[0]
    bsz, s, d = x.shape
    mod = _modulation(c, mod_w, mod_b)
    tm = min(512, s)
    tb = min(MOE_TB, bsz * s)
    rb = min(MOE_RB, tb)
    n_e, dff = w2.shape[1], w2.shape[2]
    w2b = w2.astype(BF16).reshape(depth * n_e, dff, d)
    for l in range(depth):
        qkv_a, z_a, qb, kb, vb, qi, small = _inproj(
            x, mod[l], norm_mix_w[l], _permute_w_in(w_in[l]), q_norm_w[l], k_norm_w[l], tm)
        y_a = _gdn(qkv_a, z_a, small, conv_w[l], a_log[l], dt_bias[l], gdn_norm_w[l], sb=tm)
        y_b = _dsa(qb, kb, vb, qi, small, rel_bias)
        xn, hp, ridx, gate = _outproj(y_a, y_b, x, mod[l], w_out[l].astype(BF16), norm_ffn_w[l],
                                      router_w[l], router_b[l], tm)
        x = _moe(xn, hp, ridx, gate, mod[l][:, 5], _deinterleave_cast(w1, l), _deinterleave_bias(b1[l]), w2b,
                 b2[l].reshape(n_e, 1, d), tb, rb, layer=l)
    return x
```

```python
import functools
import math

import jax
import jax.numpy as jnp
import numpy as np
from jax import lax
from jax.experimental import pallas as pl
from jax.experimental.pallas import tpu as pltpu

F32 = jnp.float32
BF16 = jnp.bfloat16
I32 = jnp.int32
I16 = jnp.int16
U32 = jnp.uint32
HIGHEST = lax.Precision.HIGHEST

LANES = 128
SUBLANES = 8
VMEM_LIMIT_BYTES = 56 * 1024 * 1024

CHUNK = 64
HEAD_DIM_A = 128
N_HEADS_A = 4
WIDTH_A = N_HEADS_A * HEAD_DIM_A
CONV_K = 4
HEAD_DIM_B = 64
N_HEADS_B = 8
WIDTH_B = N_HEADS_B * HEAD_DIM_B
IDX_HEADS = 8
IDX_DIM = 64
TOPK_KEYS_MAX = 256
REL_BUCKETS = 32
REL_MAX_DIST = 1024
N_EXPERTS = 32
TOP_K = 4
SWIGLU_ALPHA = 1.702
SWIGLU_LIMIT = 7.0
EPS = 1e-6
NEG_BIG = -1e30

C_QKVA = 0
C_Z = C_QKVA + 3 * WIDTH_A
C_QB = C_Z + WIDTH_A
C_KB = C_QB + WIDTH_B
C_VB = C_KB + WIDTH_B
C_QI = C_VB + WIDTH_B
C_SMALL = C_QI + IDX_HEADS * IDX_DIM
D_IN_PAD = C_SMALL + LANES
S_KIDX = 0
S_B = IDX_DIM
S_A = S_B + N_HEADS_A
S_WIDX = S_A + N_HEADS_A


def _cparams(sem):
    return pltpu.CompilerParams(dimension_semantics=sem, vmem_limit_bytes=VMEM_LIMIT_BYTES)


def _silu(x):
    return x * (1.0 / (1.0 + jnp.exp(-x)))


def _sigmoid(x):
    return 1.0 / (1.0 + jnp.exp(-x))


def _softplus(x):
    return jnp.maximum(x, 0.0) + jnp.log(1.0 + jnp.exp(-jnp.abs(x)))


def _mod_kernel(c_ref, w_ref, b_ref, o_ref):
    a = _silu(c_ref[...])
    o_ref[0] = jnp.dot(a, w_ref[0], precision=HIGHEST, preferred_element_type=F32) + b_ref[0]


def _modulation(c, mod_w, mod_b):
    depth, d, n = mod_w.shape
    bsz = c.shape[0]
    rows = -(-bsz // SUBLANES) * SUBLANES
    c_pad = jnp.zeros((rows, d), F32).at[:bsz].set(c)
    tn = 1536
    out = pl.pallas_call(
        _mod_kernel,
        grid=(depth, n // tn),
        in_specs=[
            pl.BlockSpec((rows, d), lambda l, j: (0, 0)),
            pl.BlockSpec((1, d, tn), lambda l, j: (l, 0, j)),
            pl.BlockSpec((1, 1, tn), lambda l, j: (l, 0, j)),
        ],
        out_specs=pl.BlockSpec((1, rows, tn), lambda l, j: (l, 0, j)),
        out_shape=jax.ShapeDtypeStruct((depth, rows, n), F32),
        compiler_params=_cparams(("arbitrary", "arbitrary")),
        name="adaln_mod",
    )(c_pad, mod_w, mod_b.reshape(depth, 1, n))
    return out[:, :bsz].reshape(depth, bsz, 6, d)


def _head_rms(t, group_ref, wn, inv_dim):
    t2 = t * t
    hi = t2.astype(BF16)
    lo = (t2 - hi.astype(F32)).astype(BF16)
    ss = (jnp.dot(hi, group_ref[...], preferred_element_type=F32)
          + jnp.dot(lo, group_ref[...], preferred_element_type=F32))
    return t * lax.rsqrt(ss * inv_dim + EPS) * wn


def _inproj_kernel(x_ref, mod_ref, nw_ref, w_ref, group_ref, qn_ref, kn_ref,
                   qkva_ref, z_ref, qb_ref, kb_ref, vb_ref, qi_ref, small_ref):
    x = x_ref[0]
    ms = jnp.mean(x * x, axis=-1, keepdims=True)
    y = x * lax.rsqrt(ms + EPS) * nw_ref[...]
    h = y * (1.0 + mod_ref[0, 1:2, :]) + mod_ref[0, 0:1, :]
    hb = h.astype(BF16)

    def mm(lo, width):
        return jnp.dot(hb, w_ref[:, lo:lo + width], preferred_element_type=F32)

    qkva_ref[0] = mm(C_QKVA, 3 * WIDTH_A)
    z_ref[0] = mm(C_Z, WIDTH_A)
    q = _head_rms(mm(C_QB, WIDTH_B), group_ref, qn_ref[...], 1.0 / HEAD_DIM_B)
    qb_ref[0] = (q * (HEAD_DIM_B ** -0.5 * LOG2E)).astype(BF16)
    k = _head_rms(mm(C_KB, WIDTH_B), group_ref, kn_ref[...], 1.0 / HEAD_DIM_B)
    kb_ref[0] = k.astype(BF16)
    v = mm(C_VB, WIDTH_B)
    lane = lax.broadcasted_iota(I32, (v.shape[0], LANES), 1)
    tail = (lane == HEAD_DIM_B).astype(F32)
    for p in range(N_HEADS_B // 2):
        pair = v[:, p * LANES:(p + 1) * LANES]
        vb_ref[0, :, (2 * p) * LANES:(2 * p + 1) * LANES] = jnp.where(lane < HEAD_DIM_B, pair, tail).astype(BF16)
        vb_ref[0, :, (2 * p + 1) * LANES:(2 * p + 2) * LANES] = jnp.where(
            lane < HEAD_DIM_B, pltpu.roll(pair, HEAD_DIM_B, axis=1), tail).astype(BF16)
    qi_ref[0] = mm(C_QI, IDX_HEADS * IDX_DIM).astype(BF16)
    small_ref[0] = mm(C_SMALL, LANES)


def _permute_w_in(w_in_l):
    d = w_in_l.shape[0]
    o = 0
    qkva = w_in_l[:, o:o + 3 * WIDTH_A]; o += 3 * WIDTH_A
    z = w_in_l[:, o:o + WIDTH_A]; o += WIDTH_A
    b = w_in_l[:, o:o + N_HEADS_A]; o += N_HEADS_A
    a = w_in_l[:, o:o + N_HEADS_A]; o += N_HEADS_A
    qkvb = w_in_l[:, o:o + 3 * WIDTH_B]; o += 3 * WIDTH_B
    qi = w_in_l[:, o:o + IDX_HEADS * IDX_DIM]; o += IDX_HEADS * IDX_DIM
    ki = w_in_l[:, o:o + IDX_DIM]; o += IDX_DIM
    wi = w_in_l[:, o:o + IDX_HEADS]; o += IDX_HEADS
    pad = jnp.zeros((d, LANES - IDX_DIM - 2 * N_HEADS_A - IDX_HEADS), w_in_l.dtype)
    return jnp.concatenate([qkva, z, qkvb, qi, ki, b, a, wi, pad], axis=1).astype(BF16)


def _group_ones(width, group):
    g = np.arange(width) // group
    return jnp.asarray((g[:, None] == g[None, :]).astype(np.float32), dtype=BF16)


def _inproj(x, mod_l, norm_w, w_perm, q_norm_w, k_norm_w, tm):
    bsz, s, d = x.shape
    f = lambda b, i: (b, i, 0)
    const2 = lambda b, i: (0, 0)
    outs = [
        (3 * WIDTH_A, F32), (WIDTH_A, F32), (WIDTH_B, BF16), (WIDTH_B, BF16), (N_HEADS_B * LANES, BF16),
        (IDX_HEADS * IDX_DIM, BF16), (LANES, F32),
    ]
    return pl.pallas_call(
        _inproj_kernel,
        grid=(bsz, s // tm),
        in_specs=[
            pl.BlockSpec((1, tm, d), f),
            pl.BlockSpec((1, 6, d), lambda b, i: (b, 0, 0)),
            pl.BlockSpec((1, d), const2),
            pl.BlockSpec((d, D_IN_PAD), const2),
            pl.BlockSpec((WIDTH_B, WIDTH_B), const2),
            pl.BlockSpec((1, WIDTH_B), const2),
            pl.BlockSpec((1, WIDTH_B), const2),
        ],
        out_specs=[pl.BlockSpec((1, tm, w), f) for w, _ in outs],
        out_shape=[jax.ShapeDtypeStruct((bsz, s, w), dt) for w, dt in outs],
        compiler_params=_cparams(("parallel", "parallel")),
        name="inproj",
    )(x, mod_l, norm_w.reshape(1, d), w_perm, _group_ones(WIDTH_B, HEAD_DIM_B),
      jnp.tile(q_norm_w, N_HEADS_B).reshape(1, WIDTH_B), jnp.tile(k_norm_w, N_HEADS_B).reshape(1, WIDTH_B))


def _dot_nt(a, b, precision=None):
    return lax.dot_general(a, b, (((1,), (1,)), ((), ())), precision=precision, preferred_element_type=F32)


def _mm_bf16(a, b):
    return jnp.dot(a.astype(BF16), b.astype(BF16), preferred_element_type=F32)


def _split2(x):
    hi = x.astype(BF16)
    return hi, (x - hi.astype(F32)).astype(BF16)


def _split3(x):
    hi = x.astype(BF16)
    r = x - hi.astype(F32)
    mid = r.astype(BF16)
    return hi, mid, (r - mid.astype(F32)).astype(BF16)


def _mm3(a, b):
    return (jnp.dot(a[0], b[0], preferred_element_type=F32) + jnp.dot(a[0], b[1], preferred_element_type=F32)
            + jnp.dot(a[1], b[0], preferred_element_type=F32))


GDN_PAR = 4


def _gdn_kernel(qkv_ref, z_ref, small_ref, convw_ref, alog_ref, dtb_ref, nw_ref,
                y_ref, xe_ref, u_ref, state_ref, uval_ref, wdec_ref, qg_ref, kdec_ref, attn_ref, egl_ref, *, sb):
    n_chunks = sb // CHUNK
    halo = SUBLANES

    @pl.when(pl.program_id(1) == 0)
    def _():
        xe_ref[0:halo, :] = jnp.zeros((halo, 3 * WIDTH_A), F32)
        state_ref[...] = jnp.zeros_like(state_ref)

    xe_ref[halo:halo + sb, :] = qkv_ref[0]

    rows = 128
    for g in range(3 * WIDTH_A // LANES):
        cs = slice(g * LANES, (g + 1) * LANES)
        for r in range(sb // rows):
            base = halo - (CONV_K - 1) + r * rows
            acc = xe_ref[base:base + rows, cs] * convw_ref[0:1, cs]
            for j in range(1, CONV_K):
                acc = acc + xe_ref[base + j:base + j + rows, cs] * convw_ref[j:j + 1, cs]
            u_ref[r * rows:(r + 1) * rows, cs] = _silu(acc)

    xe_ref[0:halo, :] = xe_ref[sb:sb + halo, :]

    wide = N_HEADS_A * CHUNK
    heads = range(N_HEADS_A)
    ii = lax.broadcasted_iota(I32, (CHUNK, wide), 0)
    jj = lax.broadcasted_iota(I32, (CHUNK, wide), 1) & (CHUNK - 1)
    eye_w = (ii == jj).astype(F32)
    tri = (lax.broadcasted_iota(I32, (CHUNK, CHUNK), 0)
           >= lax.broadcasted_iota(I32, (CHUNK, CHUNK), 1)).astype(F32).astype(BF16)
    shift = int(math.log2(CHUNK))
    bd_mask = ((lax.broadcasted_iota(I32, (wide, wide), 0) >> shift)
               == (lax.broadcasted_iota(I32, (wide, wide), 1) >> shift)).astype(F32)

    bd_mask = bd_mask.astype(BF16)

    def block_diag(parts):
        return tuple(jnp.concatenate([m] * N_HEADS_A, axis=0) * bd_mask for m in parts)

    def prepare(c):
        rs = pl.ds(pl.multiple_of(c * CHUNK, CHUNK), CHUNK)
        qn, kn, v, beta, g_b = [], [], [], [], []
        for h in heads:
            q = u_ref[rs, h * HEAD_DIM_A:(h + 1) * HEAD_DIM_A]
            k = u_ref[rs, WIDTH_A + h * HEAD_DIM_A:WIDTH_A + (h + 1) * HEAD_DIM_A]
            v.append(u_ref[rs, 2 * WIDTH_A + h * HEAD_DIM_A:2 * WIDTH_A + (h + 1) * HEAD_DIM_A])
            qn.append(q * (lax.rsqrt(jnp.sum(q * q, axis=-1, keepdims=True) + EPS) * (HEAD_DIM_A ** -0.5)))
            kn.append(k * lax.rsqrt(jnp.sum(k * k, axis=-1, keepdims=True) + EPS))
            beta.append(_sigmoid(small_ref[0, rs, S_B + h:S_B + h + 1]))
            g = -jnp.exp(alog_ref[0:1, h:h + 1]) * _softplus(small_ref[0, rs, S_A + h:S_A + h + 1]
                                                            + dtb_ref[0:1, h:h + 1])
            g_b.append(jnp.broadcast_to(g, (CHUNK, CHUNK)))
        gc_w = sum(jnp.dot(tri, part, preferred_element_type=F32) for part in _split3(jnp.concatenate(g_b, axis=1)))
        gc_row = jnp.sum(jnp.where(ii == jj, gc_w, 0.0), axis=0, keepdims=True)
        decay_w = jnp.exp(jnp.where(ii >= jj, gc_w - gc_row, NEG_BIG))
        k_beta = [kn[h] * beta[h] for h in heads]
        kk_w = jnp.concatenate([_dot_nt(k_beta[h].astype(BF16), kn[h].astype(BF16)) for h in heads], axis=1)
        a_w = -jnp.where(ii > jj, kk_w * decay_w, 0.0)
        gc = [gc_w[:, h * CHUNK:h * CHUNK + 1] for h in heads]
        egc = [jnp.exp(gc[h]) for h in heads]
        qk = [_dot_nt(qn[h].astype(BF16), kn[h].astype(BF16)) for h in heads]
        for h in heads:
            g_last = gc[h][CHUNK - 1:CHUNK, :]
            qg_ref[c, h] = (qn[h] * egc[h]).astype(BF16)
            kdec_ref[c, h] = kn[h] * jnp.exp(g_last - gc[h])
            attn_ref[c, h] = (qk[h] * decay_w[:, h * CHUNK:(h + 1) * CHUNK]).astype(BF16)
            egl_ref[c, h] = jnp.broadcast_to(jnp.exp(g_last), (SUBLANES, LANES))
        return a_w, [_split2(v[h] * beta[h]) for h in heads], [_split2(k_beta[h] * egc[h]) for h in heads]

    def solve_body(cg, carry):
        group = range(GDN_PAR)
        chunks = [cg * GDN_PAR + i for i in group]
        pre = [prepare(c) for c in chunks]
        t_w = [eye_w + pre[i][0] for i in group]
        p_parts = [_split2(pre[i][0]) for i in group]
        bd = [block_diag(p_parts[i]) for i in group]
        for _ in range(shift - 1):
            prod = [_mm3(p_parts[i], bd[i]) for i in group]
            p_parts = [_split2(prod[i]) for i in group]
            bd = [block_diag(p_parts[i]) for i in group]
            upd = [_mm3(_split2(t_w[i]), bd[i]) for i in group]
            t_w = [t_w[i] + upd[i] for i in group]
        for i in group:
            t_h = [_split2(t_w[i][:, h * CHUNK:(h + 1) * CHUNK]) for h in heads]
            u_val = [_mm3(t_h[h], pre[i][1][h]) for h in heads]
            w_dec = [_mm3(t_h[h], pre[i][2][h]) for h in heads]
            for h in heads:
                uval_ref[chunks[i], h] = u_val[h]
                wdec_ref[chunks[i], h] = w_dec[h].astype(BF16)
        return carry

    lax.fori_loop(0, n_chunks // GDN_PAR, solve_body, 0)

    def scan_body(c, carry):
        rs = pl.ds(pl.multiple_of(c * CHUNK, CHUNK), CHUNK)
        state = [state_ref[h] for h in heads]
        state_b = [s_h.astype(BF16) for s_h in state]
        w_s = [jnp.dot(wdec_ref[c, h], state_b[h], preferred_element_type=F32) for h in heads]
        v_new = [(uval_ref[c, h] - w_s[h]).astype(BF16) for h in heads]
        o = [jnp.dot(qg_ref[c, h], state_b[h], preferred_element_type=F32)
             + jnp.dot(attn_ref[c, h], v_new[h], preferred_element_type=F32) for h in heads]
        for h in heads:
            state_ref[h] = (state[h] * egl_ref[c, h][0:1, 0:1]
                            + jnp.dot(kdec_ref[c, h].T.astype(BF16), v_new[h], preferred_element_type=F32))
        for h in heads:
            hs = slice(h * HEAD_DIM_A, (h + 1) * HEAD_DIM_A)
            on = o[h] * lax.rsqrt(jnp.mean(o[h] * o[h], axis=-1, keepdims=True) + EPS) * nw_ref[...]
            y_ref[0, rs, hs] = (on * _silu(z_ref[0, rs, hs])).astype(BF16)
        return carry

    lax.fori_loop(0, n_chunks, scan_body, 0)


def _gdn(qkv_a, z_a, small, conv_w, a_log, dt_bias, norm_w, sb):
    bsz, s, _ = qkv_a.shape
    n_c = sb // CHUNK
    assert n_c % GDN_PAR == 0 and s % sb == 0
    per_head = (n_c, N_HEADS_A, CHUNK, HEAD_DIM_A)
    f = lambda b, i: (b, i, 0)
    const2 = lambda b, i: (0, 0)
    return pl.pallas_call(
        functools.partial(_gdn_kernel, sb=sb),
        grid=(bsz, s // sb),
        in_specs=[
            pl.BlockSpec((1, sb, 3 * WIDTH_A), f),
            pl.BlockSpec((1, sb, WIDTH_A), f),
            pl.BlockSpec((1, sb, LANES), f),
            pl.BlockSpec((CONV_K, 3 * WIDTH_A), const2),
            pl.BlockSpec((1, N_HEADS_A), const2),
            pl.BlockSpec((1, N_HEADS_A), const2),
            pl.BlockSpec((1, HEAD_DIM_A), const2),
        ],
        out_specs=pl.BlockSpec((1, sb, WIDTH_A), f),
        out_shape=jax.ShapeDtypeStruct((bsz, s, WIDTH_A), BF16),
        scratch_shapes=[
            pltpu.VMEM((sb + SUBLANES, 3 * WIDTH_A), F32),
            pltpu.VMEM((sb, 3 * WIDTH_A), F32),
            pltpu.VMEM((N_HEADS_A, HEAD_DIM_A, HEAD_DIM_A), F32),
            pltpu.VMEM(per_head, F32),
            pltpu.VMEM(per_head, BF16),
            pltpu.VMEM(per_head, BF16),
            pltpu.VMEM(per_head, F32),
            pltpu.VMEM((n_c, N_HEADS_A, CHUNK, CHUNK), BF16),
            pltpu.VMEM((n_c, N_HEADS_A, SUBLANES, LANES), F32),
        ],
        compiler_params=_cparams(("parallel", "arbitrary")),
        name="gdn",
    )(qkv_a, z_a, small, conv_w, a_log.reshape(1, -1), dt_bias.reshape(1, -1), norm_w.reshape(1, -1))


QB = 128
FAR_T = 512
FAR_G = FAR_T // LANES
ATT_W = 256
LOG2E = 1.4426950408889634
NEAR_D = 9
NEAR_MIN = 5
INT_MIN = -2 ** 31
HALF_MIN = -2 ** 15


def _t5_bucket_np(rel):
    nb = REL_BUCKETS // 2
    max_exact = nb // 2
    side = np.where(rel > 0, nb, 0)
    n = np.abs(rel)
    nf = np.maximum(n, 1).astype(np.float32)
    large = max_exact + (np.log(nf / np.float32(max_exact)) / np.float32(math.log(REL_MAX_DIST / max_exact))
                         * np.float32(nb - max_exact)).astype(np.int32)
    large = np.minimum(large, nb - 1)
    return (side + np.where(n < max_exact, n, large)).astype(np.int32)


def _near_bucket_table():
    r = np.arange(QB)[:, None]
    c = np.arange(LANES)[None, :]
    return np.stack([_t5_bucket_np(c - r - LANES * d) for d in range(NEAR_D)])


FAR_BUCKET = int(_t5_bucket_np(np.array([-(NEAR_MIN * LANES + 1)]))[0])
assert all(int(b) == FAR_BUCKET for b in _t5_bucket_np(-np.arange((NEAR_MIN + 1) * LANES - (QB - 1), 1 << 20, 997)))


def _sortable_key(score):
    bits = pltpu.bitcast(score + 0.0, I32)
    return bits ^ ((bits >> 31) & 0x7FFFFFFF)


PLANE_G = 32


def _bit_transpose32(words):
    a = list(words)
    mask, j = 0x0000FFFF, 16
    while j:
        k = 0
        while k < 32:
            t = (a[k] ^ lax.shift_right_logical(a[k + j], jnp.int32(j))) & mask
            a[k] = a[k] ^ t
            a[k + j] = a[k + j] ^ lax.shift_left(t, jnp.int32(j))
            k = (k + j + 1) & ~j
        j >>= 1
        mask = (mask ^ (mask << j)) & 0xFFFFFFFF
    return a


def _dsa_kernel(rb_ref, qb_ref, qi_ref, small_ref, kb_ref, va_ref, kidx2_ref, tab_ref,
                y_ref, qis_ref, qs_ref, wb_ref, keys_ref, nbias_ref, thr_ref, jlim_ref,
                m_ref, acc_ref, s_ref, p_ref, peak_ref, planes_ref, need_ref, excess_ref, *, seq, k_sel):
    i = pl.program_id(1)
    lane = lax.broadcasted_iota(I32, (QB, LANES), 1)
    row = lax.broadcasted_iota(I32, (QB, LANES), 0)
    even_f = (lane < HEAD_DIM_B).astype(F32)
    even_b = even_f.astype(BF16)
    odd_b = (1.0 - even_f).astype(BF16)

    @pl.when(i == 0)
    def _():
        nbias_ref[...] = jnp.zeros_like(nbias_ref)
        keys_ref[...] = jnp.full(keys_ref.shape, INT_MIN, I32)

        def d_body(d, c0):
            tab = tab_ref[d]

            def b_body(bk, c1):
                hit = tab == bk
                for h in range(N_HEADS_B):
                    nbias_ref[d * N_HEADS_B + h] = jnp.where(hit, rb_ref[bk, h] * LOG2E,
                                                             nbias_ref[d * N_HEADS_B + h])
                return c1

            return lax.fori_loop(0, REL_BUCKETS, b_body, c0)

        lax.fori_loop(0, NEAR_D, d_body, 0)

    for p in range(N_HEADS_B // 2):
        ps = slice(p * LANES, (p + 1) * LANES)
        qi_pair = qi_ref[0, :, ps]
        qis_ref[(2 * p) * QB:(2 * p + 1) * QB, :] = qi_pair * even_b
        qis_ref[(2 * p + 1) * QB:(2 * p + 2) * QB, :] = qi_pair * odd_b
        q_pair = qb_ref[0, :, ps]
        qs_ref[(2 * p) * QB:(2 * p + 1) * QB, :] = q_pair * even_b
        qs_ref[(2 * p + 1) * QB:(2 * p + 2) * QB, :] = q_pair * odd_b
    w_scale = IDX_HEADS ** -0.5 * IDX_DIM ** -0.5
    for h in range(IDX_HEADS):
        wb_ref[h] = jnp.broadcast_to(small_ref[0, :, S_WIDX + h:S_WIDX + h + 1] * w_scale, (QB, LANES))

    limit = i * QB + CHUNK + jnp.where(row >= CHUNK, CHUNK, 0)

    def score_body(t, c0):
        for c in range(FAR_T // ATT_W):
            k0 = t * FAR_T + c * ATT_W
            k_part = kidx2_ref[0, pl.ds(pl.multiple_of(k0, ATT_W), ATT_W), :]
            acc = None
            for h in range(IDX_HEADS):
                dots = _dot_nt(qis_ref[h * QB:(h + 1) * QB, :], k_part)
                term = jnp.maximum(dots, 0.0) * jnp.concatenate([wb_ref[h]] * (ATT_W // LANES), axis=1)
                acc = term if acc is None else acc + term
            for gg in range(ATT_W // LANES):
                col = k0 + gg * LANES + lane
                keys_ref[(k0 // LANES) + gg] = jnp.where(
                    col < limit, _sortable_key(acc[:, gg * LANES:(gg + 1) * LANES]), INT_MIN)
        return c0

    n_groups = i + 1
    n_tiles = i // FAR_G + 1
    lax.fori_loop(0, n_tiles, score_body, 0)

    @pl.when(n_tiles % 2 == 1)
    def _():
        for g in range(FAR_G):
            keys_ref[n_tiles * FAR_G + g] = jnp.full((QB, LANES), INT_MIN, I32)

    def count(pred):
        def t_body(t, acc):
            for g in range(2 * FAR_G):
                grp = t * (2 * FAR_G) + g
                acc = acc + jnp.where(pred(keys_ref[grp], grp * LANES + lane), 1, 0)
            return acc

        acc = lax.fori_loop(0, (n_tiles + 1) // 2, t_body, jnp.zeros((QB, LANES), I32))
        return jnp.broadcast_to(jnp.sum(acc, axis=1, keepdims=True), (QB, LANES))

    thr_ref[...] = jnp.full((QB, LANES), INT_MIN, I32)
    jlim_ref[...] = jnp.full((QB, LANES), -1, I32)

    def lane_total(x):
        return jnp.broadcast_to(jnp.sum(x, axis=1, keepdims=True), (QB, LANES))

    @pl.when(n_groups * QB > k_sel)
    def _():
        def transpose_half(half):
            def row_body(rr, c0):
                rows = pl.ds(pl.multiple_of(rr * SUBLANES, SUBLANES), SUBLANES)
                planes = _bit_transpose32([keys_ref[half * PLANE_G + g, rows, :] ^ INT_MIN for g in range(PLANE_G)])
                for b in range(32):
                    planes_ref[half, b, rows, :] = planes[b]
                return c0

            lax.fori_loop(0, QB // SUBLANES, row_body, 0)

        def search(n_half):
            halves = range(n_half)
            row_groups = (pl.ds(0, QB // 2), pl.ds(QB // 2, QB // 2))
            shape = (QB // 2, LANES)

            def total(x):
                return jnp.broadcast_to(jnp.sum(x, axis=1, keepdims=True), shape)

            def digit(step, rows, state):
                r, eq, above = state
                cls = []
                for half in halves:
                    hi = eq[half] & planes_ref[half, 2 * step, rows, :]
                    lo_plane = planes_ref[half, 2 * step + 1, rows, :]
                    zero_hi = eq[half] ^ hi
                    e11 = hi & lo_plane
                    e01 = zero_hi & lo_plane
                    cls.append((e11, hi ^ e11, e01, zero_hi ^ e01))
                n11, n10, n01 = [sum(lax.population_count(cls[h][d]) for h in halves) for d in range(3)]
                c3 = above + n11
                c2 = c3 + n10
                c1 = c2 + n01
                d3, d2, d1 = [total(c) >= k_sel for c in (c3, c2, c1)]
                eq = tuple(jnp.where(d3, cls[h][0], jnp.where(d2, cls[h][1], jnp.where(d1, cls[h][2], cls[h][3])))
                           for h in halves)
                above = jnp.where(d3, above, jnp.where(d2, c3, jnp.where(d1, c2, c1)))
                value = jnp.where(d3, 3, jnp.where(d2, 2, jnp.where(d1, 1, 0)))
                return r | lax.shift_left(value, 30 - 2 * step), eq, above

            def digit_body(step, carry):
                return tuple(digit(step, rows, state) for rows, state in zip(row_groups, carry))

            zero = jnp.zeros(shape, I32)
            full = jnp.full(shape, -1, I32)
            start = (zero, (full,) * n_half, zero)
            for rows, (r, eq, above) in zip(row_groups, lax.fori_loop(0, 16, digit_body, (start, start))):
                thr_ref[rows, :] = r ^ INT_MIN
                need = k_sel - total(above)
                need_ref[rows, :] = need
                excess_ref[rows, :] = total(sum(lax.population_count(eq[h]) for h in halves)) - need

        transpose_half(0)

        @pl.when(n_groups > PLANE_G)
        def _():
            transpose_half(1)
            search(2)

        @pl.when(n_groups <= PLANE_G)
        def _():
            search(1)

        r = thr_ref[...]
        need = need_ref[...]
        excess = excess_ref[...]
        jlim_ref[...] = jnp.where(r == INT_MIN, -1, seq)

        @pl.when(jnp.max(excess) > 0)
        def _():
            def j_body(step, jl):
                cand = jl + lax.shift_left(jnp.int32(1), (seq.bit_length() - 1) - step)
                cnt = count(lambda kt, col: (kt == r) & (col < cand))
                return jnp.where(cnt < need, cand, jl)

            jl = lax.fori_loop(0, seq.bit_length(), j_body, jnp.zeros((QB, LANES), I32))
            jlim_ref[...] = jnp.where(r == INT_MIN, -1, jl)

    m_ref[...] = jnp.full(m_ref.shape, NEG_BIG, F32)
    acc_ref[...] = jnp.zeros_like(acc_ref)

    n_part = FAR_T // ATT_W
    part_g = ATT_W // LANES

    def tile_mask(t):
        negm = []
        for g in range(FAR_G):
            grp = t * FAR_G + g
            kt = keys_ref[grp]
            sel = (kt > thr_ref[...]) | ((kt == thr_ref[...]) & (grp * LANES + lane <= jlim_ref[...]))
            negm.append(jnp.where(sel, 0.0, -jnp.inf))
        return negm

    def logits_pair(t, p, negm, near):
        heads = ((2 * p, slice(0, QB)), (2 * p + 1, slice(QB, 2 * QB)))
        peak = [None, None]
        for c in range(n_part):
            ks = pl.ds(pl.multiple_of(t * FAR_T + c * ATT_W, ATT_W), ATT_W)
            s = _dot_nt(qs_ref[2 * p * QB:(2 * p + 2) * QB, :], kb_ref[0, ks, p * LANES:(p + 1) * LANES])
            for gg in range(part_g):
                g = c * part_g + gg
                for n, (h, rows) in enumerate(heads):
                    extra = negm[g]
                    if near:
                        extra = extra + nbias_ref[jnp.clip(i - (t * FAR_G + g), 0, NEAR_D - 1) * N_HEADS_B + h]
                    v = s[rows, gg * LANES:(gg + 1) * LANES] + extra
                    s_ref[0, h, g] = v
                    peak[n] = v if peak[n] is None else jnp.maximum(peak[n], v)
        for n, (h, _) in enumerate(heads):
            peak_ref[0, h] = peak[n]

    def softmax_head(h, bias_scalar):
        m_old = m_ref[h]
        m_new = jnp.maximum(m_old, jnp.max(peak_ref[0, h], axis=1, keepdims=True) + bias_scalar)
        shift = jnp.concatenate([m_new - bias_scalar] * part_g, axis=1)
        for c in range(n_part):
            part = jnp.concatenate([s_ref[0, h, c * part_g + g] for g in range(part_g)], axis=1)
            p_ref[0, h, :, c * ATT_W:(c + 1) * ATT_W] = jnp.exp2(part - shift).astype(BF16)
        acc_ref[h] = jnp.exp2(m_old - m_new) * acc_ref[h]
        m_ref[h] = m_new

    def pv_head(t, h):
        ks = pl.ds(pl.multiple_of(t * FAR_T, FAR_T), FAR_T)
        acc_ref[h] += jnp.dot(p_ref[0, h], va_ref[0, ks, h * LANES:(h + 1) * LANES], preferred_element_type=F32)

    def tile_body(near):
        def body(t, c0):
            negm = tile_mask(t)
            n_pair = N_HEADS_B // 2
            for step in range(n_pair + 2):
                if step < n_pair:
                    logits_pair(t, step, negm, near)
                if 1 <= step <= n_pair:
                    for h in (2 * step - 2, 2 * step - 1):
                        softmax_head(h, 0.0 if near else rb_ref[FAR_BUCKET, h] * LOG2E)
                if step >= 2:
                    for h in (2 * step - 4, 2 * step - 3):
                        pv_head(t, h)
            return c0

        return body

    far_tiles = jnp.maximum(i - NEAR_MIN, 0) // FAR_G
    lax.fori_loop(0, far_tiles, tile_body(False), 0)
    lax.fori_loop(far_tiles, n_tiles, tile_body(True), 0)

    def head_out(h):
        a = acc_ref[h]
        return a * (1.0 / a[:, HEAD_DIM_B:HEAD_DIM_B + 1])

    for p in range(N_HEADS_B // 2):
        o_odd = pltpu.roll(head_out(2 * p + 1), HEAD_DIM_B, axis=1)
        y_ref[0, :, p * LANES:(p + 1) * LANES] = jnp.where(lane < HEAD_DIM_B, head_out(2 * p), o_odd).astype(BF16)


def _dsa(qb, kb, va, qi, small, rel_bias):
    bsz, s, _ = qb.shape
    assert s % FAR_T == 0 and s // LANES <= 2 * PLANE_G
    k_sel = min(TOPK_KEYS_MAX, s // 4)
    kidx = small[:, :, S_KIDX:S_KIDX + IDX_DIM].astype(BF16)
    kidx2 = jnp.concatenate([kidx, kidx], axis=-1)
    tab = jnp.asarray(_near_bucket_table())
    blk = lambda b, i: (b, i, 0)
    full = lambda b, i: (b, 0, 0)
    one = pl.Buffered(1)
    return pl.pallas_call(
        functools.partial(_dsa_kernel, seq=s, k_sel=k_sel),
        grid=(bsz, s // QB),
        in_specs=[
            pl.BlockSpec(memory_space=pltpu.SMEM),
            pl.BlockSpec((1, QB, WIDTH_B), blk),
            pl.BlockSpec((1, QB, IDX_HEADS * IDX_DIM), blk),
            pl.BlockSpec((1, QB, LANES), blk),
            pl.BlockSpec((1, s, WIDTH_B), full, pipeline_mode=one),
            pl.BlockSpec((1, s, N_HEADS_B * LANES), full, pipeline_mode=one),
            pl.BlockSpec((1, s, LANES), full, pipeline_mode=one),
            pl.BlockSpec((NEAR_D, QB, LANES), lambda b, i: (0, 0, 0), pipeline_mode=one),
        ],
        out_specs=pl.BlockSpec((1, QB, WIDTH_B), blk),
        out_shape=jax.ShapeDtypeStruct((bsz, s, WIDTH_B), BF16),
        scratch_shapes=[
            pltpu.VMEM((IDX_HEADS * QB, LANES), BF16),
            pltpu.VMEM((N_HEADS_B * QB, LANES), BF16),
            pltpu.VMEM((IDX_HEADS, QB, LANES), F32),
            pltpu.VMEM((2 * PLANE_G, QB, LANES), I32),
            pltpu.VMEM((NEAR_D * N_HEADS_B, QB, LANES), F32),
            pltpu.VMEM((QB, LANES), I32),
            pltpu.VMEM((QB, LANES), I32),
            pltpu.VMEM((N_HEADS_B, QB, LANES), F32),
            pltpu.VMEM((N_HEADS_B, QB, LANES), F32),
            pltpu.VMEM((1, N_HEADS_B, FAR_G, QB, LANES), F32),
            pltpu.VMEM((1, N_HEADS_B, QB, FAR_T), BF16),
            pltpu.VMEM((1, N_HEADS_B, QB, LANES), F32),
            pltpu.VMEM((2, 32, QB, LANES), I32),
            pltpu.VMEM((QB, LANES), I32),
            pltpu.VMEM((QB, LANES), I32),
        ],
        compiler_params=_cparams(("parallel", "arbitrary")),
        name="dsa",
    )(rel_bias, qb, qi, small, kb, va, kidx2, tab)


HALF_MASK = 0xFFFF0000


def _pack_halves(t):
    w = t.shape[1] // 2
    bits = pltpu.bitcast(t.astype(BF16).astype(F32), U32)
    return (bits[:, :w] >> 16) | (bits[:, w:] & jnp.uint32(HALF_MASK))


def _unpack_halves(p):
    lo = pltpu.bitcast(p << 16, F32)
    hi = pltpu.bitcast(p & jnp.uint32(HALF_MASK), F32)
    return jnp.concatenate([lo, hi], axis=1)


def _outproj_kernel(ya_ref, yb_ref, x_ref, mod_ref, wo_ref, nw_ref, rw_ref, rbias_ref,
                    xn_ref, hp_ref, ridx_ref, gate_ref):
    wa = ya_ref.shape[2]
    y = (jnp.dot(ya_ref[0], wo_ref[0:wa, :], preferred_element_type=F32)
         + jnp.dot(yb_ref[0], wo_ref[wa:, :], preferred_element_type=F32))
    xn = x_ref[0] + mod_ref[0, 2:3, :] * y
    xn_ref[0] = xn
    ms = jnp.mean(xn * xn, axis=-1, keepdims=True)
    h = xn * lax.rsqrt(ms + EPS) * nw_ref[...] * (1.0 + mod_ref[0, 4:5, :]) + mod_ref[0, 3:4, :]
    hp_ref[0] = _pack_halves(h)

    logits = jnp.dot(h, rw_ref[...], precision=HIGHEST, preferred_element_type=F32) + rbias_ref[...]
    lane = lax.broadcasted_iota(I32, logits.shape, 1)
    cur = logits
    vals, ridx = [], jnp.zeros(logits.shape, I32)
    for k in range(TOP_K):
        mx = jnp.max(cur, axis=1, keepdims=True)
        am = jnp.min(jnp.where(cur == mx, lane, LANES), axis=1, keepdims=True)
        cur = jnp.where(lane == am, -jnp.inf, cur)
        vals.append(mx)
        ridx = jnp.where(lane == k, am, ridx)
    ex = [jnp.exp(v - vals[0]) for v in vals]
    inv = 1.0 / (ex[0] + ex[1] + ex[2] + ex[3])
    gate = jnp.zeros(logits.shape, F32)
    for k in range(TOP_K):
        gate = jnp.where(lane == k, ex[k] * inv, gate)
    ridx_ref[0] = ridx
    gate_ref[0] = gate


def _outproj(y_a, y_b, x, mod_l, w_out_bf, norm_w, router_w, router_b, tm):
    bsz, s, d = x.shape
    n_e = router_w.shape[1]
    rw = jnp.zeros((d, LANES), F32).at[:, :n_e].set(router_w)
    rbias = jnp.full((1, LANES), NEG_BIG, F32).at[0, :n_e].set(router_b)
    blk = lambda b, i: (b, i, 0)
    const2 = lambda b, i: (0, 0)
    return pl.pallas_call(
        _outproj_kernel,
        grid=(bsz, s // tm),
        in_specs=[
            pl.BlockSpec((1, tm, y_a.shape[2]), blk),
            pl.BlockSpec((1, tm, y_b.shape[2]), blk),
            pl.BlockSpec((1, tm, d), blk),
            pl.BlockSpec((1, 6, d), lambda b, i: (b, 0, 0)),
            pl.BlockSpec((d, d), const2),
            pl.BlockSpec((1, d), const2),
            pl.BlockSpec((d, LANES), const2),
            pl.BlockSpec((1, LANES), const2),
        ],
        out_specs=[pl.BlockSpec((1, tm, d), blk), pl.BlockSpec((1, tm, d // 2), blk),
                   pl.BlockSpec((1, tm, LANES), blk), pl.BlockSpec((1, tm, LANES), blk)],
        out_shape=[jax.ShapeDtypeStruct((bsz, s, d), F32), jax.ShapeDtypeStruct((bsz, s, d // 2), U32),
                   jax.ShapeDtypeStruct((bsz, s, LANES), I32), jax.ShapeDtypeStruct((bsz, s, LANES), F32)],
        compiler_params=_cparams(("parallel", "parallel")),
        name="outproj_router",
    )(y_a, y_b, x, mod_l, w_out_bf, norm_w.reshape(1, d), rw, rbias)


MOE_TB = 2048
MOE_RB = 512
MOE_M = 144


def _moe_kernel(first_ref, nch_ref, cbase_ref, cvalid_ref, list_ref, hp_ref, w1_ref, b1_ref, w2_ref, b2_ref,
                gate_ref, x_ref, g2_ref, o_ref, slots_ref, xg_ref, yb_ref, *, tb, rb, table_len):
    sb = pl.program_id(0)
    e = pl.program_id(1)
    dff = w2_ref.shape[1]
    table = sb * table_len + 1

    def gather(j):
        base = cbase_ref[table + j]
        buf = (j + 2) % 2
        for r in range(MOE_M):
            code = list_ref[0, 0, base + r]
            xg_ref[buf, pl.ds(r, 1), :] = hp_ref[pl.ds(code >> 2, 1), :]

    def scatter(j):
        base = cbase_ref[table + j]
        n_valid = cvalid_ref[table + j]
        buf = (j + 2) % 2
        for r in range(MOE_M):
            code = jnp.where(r < n_valid, list_ref[0, 0, base + r], TOP_K * tb)
            slots_ref[code & (TOP_K - 1), pl.ds(code >> 2, 1), :] = yb_ref[buf, pl.ds(r, 1), :]

    @pl.when(e == 0)
    def _():
        gather(0)

    @pl.when(e < N_EXPERTS)
    def _():
        j0 = first_ref[sb * N_EXPERTS + e]

        def chunk(j, carry):
            buf = j % 2
            xb = _unpack_halves(xg_ref[buf]).astype(BF16)
            gather(j + 1)
            scatter(j - 1)
            u = jnp.dot(xb, w1_ref[0], preferred_element_type=F32) + b1_ref[0]
            glu = jnp.minimum(u[:, :dff], SWIGLU_LIMIT)
            lin = jnp.clip(u[:, dff:], -SWIGLU_LIMIT, SWIGLU_LIMIT)
            act = glu * _sigmoid(SWIGLU_ALPHA * glu) * (lin + 1.0)
            y = jnp.dot(act.astype(BF16), w2_ref[0], preferred_element_type=F32) + b2_ref[0]
            yb_ref[buf] = _pack_halves(y)
            return carry

        lax.fori_loop(j0, j0 + nch_ref[sb * N_EXPERTS + e], chunk, 0)

    @pl.when(e == N_EXPERTS)
    def _():
        n_total = first_ref[sb * N_EXPERTS + N_EXPERTS - 1] + nch_ref[sb * N_EXPERTS + N_EXPERTS - 1]
        scatter(n_total - 1)

    @pl.when(e >= N_EXPERTS)
    def _():
        r0 = pl.multiple_of((e - N_EXPERTS) * rb, rb)
        acc = gate_ref[:, 0:1] * _unpack_halves(slots_ref[0, pl.ds(r0, rb), :])
        for k in range(1, TOP_K):
            acc = acc + gate_ref[:, k:k + 1] * _unpack_halves(slots_ref[k, pl.ds(r0, rb), :])
        o_ref[...] = x_ref[...] + g2_ref[0] * acc


def _moe(xn, hp, ridx, gate, g2, w1p, b1p, w2b, b2, tb, rb, layer=0):
    bsz, s, d = xn.shape
    t = bsz * s
    n_super = t // tb
    n_piece = tb // rb
    dff = w2b.shape[1]
    w2map = lambda sb, e, *_: (layer * N_EXPERTS + jnp.minimum(e, N_EXPERTS - 1), 0, 0)
    flat_e = ridx[:, :, :TOP_K].reshape(n_super, tb * TOP_K)
    order = jnp.argsort(flat_e, axis=1, stable=True).astype(I32)
    counts = jnp.sum(flat_e[:, :, None] == jnp.arange(N_EXPERTS, dtype=I32)[None, None, :], axis=1).astype(I32)
    offs = (jnp.cumsum(counts, axis=1) - counts).astype(I32)
    nch = (counts + MOE_M - 1) // MOE_M
    first = (jnp.cumsum(nch, axis=1) - nch).astype(I32)
    table_len = -(-(tb * TOP_K // MOE_M + N_EXPERTS + 2) // SUBLANES) * SUBLANES
    j = jnp.arange(table_len - 1, dtype=I32)[None, :]
    owner = jnp.sum(j[:, :, None] >= (first + nch)[:, None, :], axis=2)
    owner_c = jnp.minimum(owner, N_EXPERTS - 1)
    within = j - jnp.take_along_axis(first, owner_c, axis=1)
    c_valid = jnp.where(owner < N_EXPERTS,
                        jnp.minimum(jnp.take_along_axis(counts, owner_c, axis=1) - within * MOE_M, MOE_M), 0)
    c_base = jnp.where(owner < N_EXPERTS, jnp.take_along_axis(offs, owner_c, axis=1) + within * MOE_M, 0)
    zero = jnp.zeros((n_super, 1), I32)
    c_valid = jnp.concatenate([zero, c_valid.astype(I32)], axis=1)
    c_base = jnp.concatenate([zero, c_base.astype(I32)], axis=1)

    piece = lambda sb, e, *_: (sb * n_piece + jnp.maximum(e - N_EXPERTS, 0), 0)
    wmap = lambda sb, e, *_: (jnp.minimum(e, N_EXPERTS - 1), 0, 0)
    grid_spec = pltpu.PrefetchScalarGridSpec(
        num_scalar_prefetch=4,
        grid=(n_super, N_EXPERTS + n_piece),
        in_specs=[
            pl.BlockSpec((1, 1, tb * TOP_K + MOE_M), lambda sb, e, *_: (sb, 0, 0), memory_space=pltpu.SMEM),
            pl.BlockSpec((tb, d // 2), lambda sb, e, *_: (sb, 0), pipeline_mode=pl.Buffered(1)),
            pl.BlockSpec((1, d, 2 * dff), wmap),
            pl.BlockSpec((1, 1, 2 * dff), wmap),
            pl.BlockSpec((1, dff, d), w2map),
            pl.BlockSpec((1, 1, d), wmap),
            pl.BlockSpec((rb, LANES), piece),
            pl.BlockSpec((rb, d), piece),
            pl.BlockSpec((1, 1, d), lambda sb, e, *_: ((sb * tb) // s, 0, 0)),
        ],
        out_specs=pl.BlockSpec((rb, d), piece),
        scratch_shapes=[
            pltpu.VMEM((TOP_K, tb + SUBLANES, d // 2), U32),
            pltpu.VMEM((2, MOE_M, d // 2), U32),
            pltpu.VMEM((2, MOE_M, d // 2), U32),
        ],
    )
    out = pl.pallas_call(
        functools.partial(_moe_kernel, tb=tb, rb=rb, table_len=table_len),
        grid_spec=grid_spec,
        out_shape=jax.ShapeDtypeStruct((t, d), F32),
        compiler_params=_cparams(("arbitrary", "arbitrary")),
        name="moe",
    )(first.reshape(-1), nch.astype(I32).reshape(-1), c_base.reshape(-1), c_valid.reshape(-1),
      jnp.pad(order, ((0, 0), (0, MOE_M))).reshape(n_super, 1, tb * TOP_K + MOE_M), hp.reshape(t, d // 2),
      w1p, b1p, w2b, b2, gate.reshape(t, LANES), xn.reshape(t, d), g2.reshape(bsz, 1, d))
    return out.reshape(bsz, s, d)


MXU_COLS = 256


def _deinterleave_kernel(w_ref, perm_ref, o_ref):
    half = w_ref.shape[2] // 2
    hw = MXU_COLS // 2
    for b in range(w_ref.shape[2] // MXU_COLS):
        blk = w_ref[0, :, b * MXU_COLS:(b + 1) * MXU_COLS].astype(BF16)
        y = jnp.dot(blk, perm_ref[...], preferred_element_type=F32).astype(BF16)
        o_ref[0, :, b * hw:(b + 1) * hw] = y[:, :hw]
        o_ref[0, :, half + b * hw:half + (b + 1) * hw] = y[:, hw:]


def _deinterleave_cast(w1, layer, rows=512):
    depth, n_e, d, two_f = w1.shape
    src = np.concatenate([np.arange(0, MXU_COLS, 2), np.arange(1, MXU_COLS, 2)])
    perm = np.zeros((MXU_COLS, MXU_COLS), np.float32)
    perm[src, np.arange(MXU_COLS)] = 1.0
    return pl.pallas_call(
        _deinterleave_kernel,
        grid=(n_e, d // rows),
        in_specs=[pl.BlockSpec((1, rows, two_f), lambda e, r: (layer * n_e + e, r, 0)),
                  pl.BlockSpec((MXU_COLS, MXU_COLS), lambda e, r: (0, 0))],
        out_specs=pl.BlockSpec((1, rows, two_f), lambda e, r: (e, r, 0)),
        out_shape=jax.ShapeDtypeStruct((n_e, d, two_f), BF16),
        compiler_params=_cparams(("parallel", "parallel")),
        name="w1_deinterleave",
    )(w1.reshape(depth * n_e, d, two_f), jnp.asarray(perm, BF16))


def _deinterleave_bias(b1_l):
    n_e, two_f = b1_l.shape
    return jnp.concatenate([b1_l[:, 0::2], b1_l[:, 1::2]], axis=1).reshape(n_e, 1, two_f)


def kernel(x, c, rel_bias, mod_w, mod_b, norm_mix_w, norm_ffn_w, w_in, conv_w, a_log, dt_bias, gdn_norm_w,
           q_norm_w, k_norm_w, w_out, router_w, router_b, w1, b1, w2, b2):
    depth = mod_w.shape[0]
    bsz, s, d = x.shape
    mod = _modulation(c, mod_w, mod_b)
    tm = min(512, s)
    tb = min(MOE_TB, bsz * s)
    rb = min(MOE_RB, tb)
    n_e, dff = w2.shape[1], w2.shape[2]
    w2b = w2.astype(BF16).reshape(depth * n_e, dff, d)
    for l in range(depth):
        qkv_a, z_a, qb, kb, vb, qi, small = _inproj(
            x, mod[l], norm_mix_w[l], _permute_w_in(w_in[l]), q_norm_w[l], k_norm_w[l], tm)
        y_a = _gdn(qkv_a, z_a, small, conv_w[l], a_log[l], dt_bias[l], gdn_norm_w[l], sb=tm)
        y_b = _dsa(qb, kb, vb, qi, small, rel_bias)
        xn, hp, ridx, gate = _outproj(y_a, y_b, x, mod[l], w_out[l].astype(BF16), norm_ffn_w[l],
                                      router_w[l], router_b[l], tm)
        x = _moe(xn, hp, ridx, gate, mod[l][:, 5], _deinterleave_cast(w1, l), _deinterleave_bias(b1[l]), w2b,
                 b2[l].reshape(n_e, 1, d), tb, rb, layer=l)
    return x
```

```python
import functools
import math

import jax
import jax.numpy as jnp
import numpy as np
from jax import lax
from jax.experimental import pallas as pl
from jax.experimental.pallas import tpu as pltpu

F32 = jnp.float32
BF16 = jnp.bfloat16
I32 = jnp.int32
U32 = jnp.uint32
HIGHEST = lax.Precision.HIGHEST

LANES = 128
SUBLANES = 8
VMEM_LIMIT_BYTES = 56 * 1024 * 1024

CHUNK = 64
HEAD_DIM_A = 128
N_HEADS_A = 4
WIDTH_A = N_HEADS_A * HEAD_DIM_A
CONV_K = 4
HEAD_DIM_B = 64
N_HEADS_B = 8
WIDTH_B = N_HEADS_B * HEAD_DIM_B
IDX_HEADS = 8
IDX_DIM = 64
TOPK_KEYS_MAX = 256
REL_BUCKETS = 32
REL_MAX_DIST = 1024
N_EXPERTS = 32
TOP_K = 4
SWIGLU_ALPHA = 1.702
SWIGLU_LIMIT = 7.0
EPS = 1e-6
NEG_BIG = -1e30

C_QKVA = 0
C_Z = C_QKVA + 3 * WIDTH_A
C_QB = C_Z + WIDTH_A
C_KB = C_QB + WIDTH_B
C_VB = C_KB + WIDTH_B
C_QI = C_VB + WIDTH_B
C_SMALL = C_QI + IDX_HEADS * IDX_DIM
D_IN_PAD = C_SMALL + LANES
S_KIDX = 0
S_B = IDX_DIM
S_A = S_B + N_HEADS_A
S_WIDX = S_A + N_HEADS_A


def _cparams(sem):
    return pltpu.CompilerParams(dimension_semantics=sem, vmem_limit_bytes=VMEM_LIMIT_BYTES)


def _silu(x):
    return x * (1.0 / (1.0 + jnp.exp(-x)))


def _sigmoid(x):
    return 1.0 / (1.0 + jnp.exp(-x))


def _softplus(x):
    return jnp.maximum(x, 0.0) + jnp.log(1.0 + jnp.exp(-jnp.abs(x)))


def _mod_kernel(c_ref, w_ref, b_ref, o_ref):
    a = _silu(c_ref[...])
    o_ref[0] = jnp.dot(a, w_ref[0], precision=HIGHEST, preferred_element_type=F32) + b_ref[0]


def _modulation(c, mod_w, mod_b):
    depth, d, n = mod_w.shape
    bsz = c.shape[0]
    rows = -(-bsz // SUBLANES) * SUBLANES
    c_pad = jnp.zeros((rows, d), F32).at[:bsz].set(c)
    tn = 1536
    out = pl.pallas_call(
        _mod_kernel,
        grid=(depth, n // tn),
        in_specs=[
            pl.BlockSpec((rows, d), lambda l, j: (0, 0)),
            pl.BlockSpec((1, d, tn), lambda l, j: (l, 0, j)),
            pl.BlockSpec((1, 1, tn), lambda l, j: (l, 0, j)),
        ],
        out_specs=pl.BlockSpec((1, rows, tn), lambda l, j: (l, 0, j)),
        out_shape=jax.ShapeDtypeStruct((depth, rows, n), F32),
        compiler_params=_cparams(("arbitrary", "arbitrary")),
        name="adaln_mod",
    )(c_pad, mod_w, mod_b.reshape(depth, 1, n))
    return out[:, :bsz].reshape(depth, bsz, 6, d)


def _head_rms(t, group_ref, wn, inv_dim):
    t2 = t * t
    hi = t2.astype(BF16)
    lo = (t2 - hi.astype(F32)).astype(BF16)
    ss = (jnp.dot(hi, group_ref[...], preferred_element_type=F32)
          + jnp.dot(lo, group_ref[...], preferred_element_type=F32))
    return t * lax.rsqrt(ss * inv_dim + EPS) * wn


def _inproj_kernel(x_ref, mod_ref, nw_ref, w_ref, group_ref, qn_ref, kn_ref,
                   qkva_ref, z_ref, qb_ref, kb_ref, vb_ref, qi_ref, small_ref):
    x = x_ref[0]
    ms = jnp.mean(x * x, axis=-1, keepdims=True)
    y = x * lax.rsqrt(ms + EPS) * nw_ref[...]
    h = y * (1.0 + mod_ref[0, 1:2, :]) + mod_ref[0, 0:1, :]
    hb = h.astype(BF16)

    def mm(lo, width):
        return jnp.dot(hb, w_ref[:, lo:lo + width], preferred_element_type=F32)

    qkva_ref[0] = mm(C_QKVA, 3 * WIDTH_A)
    z_ref[0] = mm(C_Z, WIDTH_A)
    q = _head_rms(mm(C_QB, WIDTH_B), group_ref, qn_ref[...], 1.0 / HEAD_DIM_B)
    qb_ref[0] = (q * (HEAD_DIM_B ** -0.5 * LOG2E)).astype(BF16)
    k = _head_rms(mm(C_KB, WIDTH_B), group_ref, kn_ref[...], 1.0 / HEAD_DIM_B)
    kb_ref[0] = k.astype(BF16)
    v = mm(C_VB, WIDTH_B)
    lane = lax.broadcasted_iota(I32, (v.shape[0], LANES), 1)
    tail = (lane == HEAD_DIM_B).astype(F32)
    for p in range(N_HEADS_B // 2):
        pair = v[:, p * LANES:(p + 1) * LANES]
        vb_ref[0, :, (2 * p) * LANES:(2 * p + 1) * LANES] = jnp.where(lane < HEAD_DIM_B, pair, tail).astype(BF16)
        vb_ref[0, :, (2 * p + 1) * LANES:(2 * p + 2) * LANES] = jnp.where(
            lane < HEAD_DIM_B, pltpu.roll(pair, HEAD_DIM_B, axis=1), tail).astype(BF16)
    qi_ref[0] = mm(C_QI, IDX_HEADS * IDX_DIM).astype(BF16)
    small_ref[0] = mm(C_SMALL, LANES)


def _permute_w_in(w_in_l):
    d = w_in_l.shape[0]
    o = 0
    qkva = w_in_l[:, o:o + 3 * WIDTH_A]; o += 3 * WIDTH_A
    z = w_in_l[:, o:o + WIDTH_A]; o += WIDTH_A
    b = w_in_l[:, o:o + N_HEADS_A]; o += N_HEADS_A
    a = w_in_l[:, o:o + N_HEADS_A]; o += N_HEADS_A
    qkvb = w_in_l[:, o:o + 3 * WIDTH_B]; o += 3 * WIDTH_B
    qi = w_in_l[:, o:o + IDX_HEADS * IDX_DIM]; o += IDX_HEADS * IDX_DIM
    ki = w_in_l[:, o:o + IDX_DIM]; o += IDX_DIM
    wi = w_in_l[:, o:o + IDX_HEADS]; o += IDX_HEADS
    pad = jnp.zeros((d, LANES - IDX_DIM - 2 * N_HEADS_A - IDX_HEADS), w_in_l.dtype)
    return jnp.concatenate([qkva, z, qkvb, qi, ki, b, a, wi, pad], axis=1).astype(BF16)


def _group_ones(width, group):
    g = np.arange(width) // group
    return jnp.asarray((g[:, None] == g[None, :]).astype(np.float32), dtype=BF16)


def _inproj(x, mod_l, norm_w, w_perm, q_norm_w, k_norm_w, tm):
    bsz, s, d = x.shape
    f = lambda b, i: (b, i, 0)
    const2 = lambda b, i: (0, 0)
    outs = [
        (3 * WIDTH_A, F32), (WIDTH_A, F32), (WIDTH_B, BF16), (WIDTH_B, BF16), (N_HEADS_B * LANES, BF16),
        (IDX_HEADS * IDX_DIM, BF16), (LANES, F32),
    ]
    return pl.pallas_call(
        _inproj_kernel,
        grid=(bsz, s // tm),
        in_specs=[
            pl.BlockSpec((1, tm, d), f),
            pl.BlockSpec((1, 6, d), lambda b, i: (b, 0, 0)),
            pl.BlockSpec((1, d), const2),
            pl.BlockSpec((d, D_IN_PAD), const2),
            pl.BlockSpec((WIDTH_B, WIDTH_B), const2),
            pl.BlockSpec((1, WIDTH_B), const2),
            pl.BlockSpec((1, WIDTH_B), const2),
        ],
        out_specs=[pl.BlockSpec((1, tm, w), f) for w, _ in outs],
        out_shape=[jax.ShapeDtypeStruct((bsz, s, w), dt) for w, dt in outs],
        compiler_params=_cparams(("parallel", "parallel")),
        name="inproj",
    )(x, mod_l, norm_w.reshape(1, d), w_perm, _group_ones(WIDTH_B, HEAD_DIM_B),
      jnp.tile(q_norm_w, N_HEADS_B).reshape(1, WIDTH_B), jnp.tile(k_norm_w, N_HEADS_B).reshape(1, WIDTH_B))


def _dot_nt(a, b, precision=None):
    return lax.dot_general(a, b, (((1,), (1,)), ((), ())), precision=precision, preferred_element_type=F32)


def _split2(x):
    hi = x.astype(BF16)
    return hi, (x - hi.astype(F32)).astype(BF16)


def _split3(x):
    hi = x.astype(BF16)
    r = x - hi.astype(F32)
    mid = r.astype(BF16)
    return hi, mid, (r - mid.astype(F32)).astype(BF16)


def _mm3(a, b):
    return (jnp.dot(a[0], b[0], preferred_element_type=F32) + jnp.dot(a[0], b[1], preferred_element_type=F32)
            + jnp.dot(a[1], b[0], preferred_element_type=F32))


GDN_PAR = 8


def _gdn_kernel(qkv_ref, z_ref, small_ref, convw_ref, alog_ref, dtb_ref, nw_ref,
                y_ref, xe_ref, u_ref, state_ref, uval_ref, wdec_ref, qg_ref, kdec_ref, attn_ref, egl_ref, *, sb):
    n_chunks = sb // CHUNK
    halo = SUBLANES

    @pl.when(pl.program_id(1) == 0)
    def _():
        xe_ref[0:halo, :] = jnp.zeros((halo, 3 * WIDTH_A), F32)
        state_ref[...] = jnp.zeros_like(state_ref)

    xe_ref[halo:halo + sb, :] = qkv_ref[0]

    rows = 128
    for g in range(3 * WIDTH_A // LANES):
        cs = slice(g * LANES, (g + 1) * LANES)
        for r in range(sb // rows):
            base = halo - (CONV_K - 1) + r * rows
            acc = xe_ref[base:base + rows, cs] * convw_ref[0:1, cs]
            for j in range(1, CONV_K):
                acc = acc + xe_ref[base + j:base + j + rows, cs] * convw_ref[j:j + 1, cs]
            u_ref[r * rows:(r + 1) * rows, cs] = _silu(acc)

    xe_ref[0:halo, :] = xe_ref[sb:sb + halo, :]

    wide = N_HEADS_A * CHUNK
    heads = range(N_HEADS_A)
    ii = lax.broadcasted_iota(I32, (CHUNK, wide), 0)
    jj = lax.broadcasted_iota(I32, (CHUNK, wide), 1) & (CHUNK - 1)
    eye_w = (ii == jj).astype(F32)
    tri = (lax.broadcasted_iota(I32, (CHUNK, CHUNK), 0)
           >= lax.broadcasted_iota(I32, (CHUNK, CHUNK), 1)).astype(F32).astype(BF16)
    shift = int(math.log2(CHUNK))
    bd_mask = ((lax.broadcasted_iota(I32, (wide, wide), 0) >> shift)
               == (lax.broadcasted_iota(I32, (wide, wide), 1) >> shift)).astype(F32)

    bd_mask = bd_mask.astype(BF16)

    def block_diag(parts):
        return tuple(jnp.concatenate([m] * N_HEADS_A, axis=0) * bd_mask for m in parts)

    def prepare(c):
        rs = pl.ds(pl.multiple_of(c * CHUNK, CHUNK), CHUNK)
        qn, kn, v, beta, g_b = [], [], [], [], []
        for h in heads:
            q = u_ref[rs, h * HEAD_DIM_A:(h + 1) * HEAD_DIM_A]
            k = u_ref[rs, WIDTH_A + h * HEAD_DIM_A:WIDTH_A + (h + 1) * HEAD_DIM_A]
            v.append(u_ref[rs, 2 * WIDTH_A + h * HEAD_DIM_A:2 * WIDTH_A + (h + 1) * HEAD_DIM_A])
            qn.append(q * (lax.rsqrt(jnp.sum(q * q, axis=-1, keepdims=True) + EPS) * (HEAD_DIM_A ** -0.5)))
            kn.append(k * lax.rsqrt(jnp.sum(k * k, axis=-1, keepdims=True) + EPS))
            beta.append(_sigmoid(small_ref[0, rs, S_B + h:S_B + h + 1]))
            g = -jnp.exp(alog_ref[0:1, h:h + 1]) * _softplus(small_ref[0, rs, S_A + h:S_A + h + 1]
                                                            + dtb_ref[0:1, h:h + 1])
            g_b.append(jnp.broadcast_to(g, (CHUNK, CHUNK)))
        gc_w = sum(jnp.dot(tri, part, preferred_element_type=F32) for part in _split3(jnp.concatenate(g_b, axis=1)))
        gc_row = jnp.sum(jnp.where(ii == jj, gc_w, 0.0), axis=0, keepdims=True)
        decay_w = jnp.exp(jnp.where(ii >= jj, gc_w - gc_row, NEG_BIG))
        k_beta = [kn[h] * beta[h] for h in heads]
        kk_w = jnp.concatenate([_dot_nt(k_beta[h].astype(BF16), kn[h].astype(BF16)) for h in heads], axis=1)
        a_w = -jnp.where(ii > jj, kk_w * decay_w, 0.0)
        gc = [gc_w[:, h * CHUNK:h * CHUNK + 1] for h in heads]
        egc = [jnp.exp(gc[h]) for h in heads]
        qk = [_dot_nt(qn[h].astype(BF16), kn[h].astype(BF16)) for h in heads]
        for h in heads:
            g_last = gc[h][CHUNK - 1:CHUNK, :]
            qg_ref[c, h] = (qn[h] * egc[h]).astype(BF16)
            kdec_ref[c, h] = kn[h] * jnp.exp(g_last - gc[h])
            attn_ref[c, h] = (qk[h] * decay_w[:, h * CHUNK:(h + 1) * CHUNK]).astype(BF16)
            egl_ref[c, h] = jnp.broadcast_to(jnp.exp(g_last), (SUBLANES, LANES))
        return a_w, [_split2(v[h] * beta[h]) for h in heads], [_split2(k_beta[h] * egc[h]) for h in heads]

    def solve_body(cg, carry):
        group = range(GDN_PAR)
        chunks = [cg * GDN_PAR + i for i in group]
        pre = [prepare(c) for c in chunks]
        t_w = [eye_w + pre[i][0] for i in group]
        p_parts = [_split2(pre[i][0]) for i in group]
        bd = [block_diag(p_parts[i]) for i in group]
        for _ in range(shift - 1):
            prod = [_mm3(p_parts[i], bd[i]) for i in group]
            p_parts = [_split2(prod[i]) for i in group]
            bd = [block_diag(p_parts[i]) for i in group]
            upd = [_mm3(_split2(t_w[i]), bd[i]) for i in group]
            t_w = [t_w[i] + upd[i] for i in group]
        for i in group:
            t_h = [_split2(t_w[i][:, h * CHUNK:(h + 1) * CHUNK]) for h in heads]
            u_val = [_mm3(t_h[h], pre[i][1][h]) for h in heads]
            w_dec = [_mm3(t_h[h], pre[i][2][h]) for h in heads]
            for h in heads:
                uval_ref[chunks[i], h] = u_val[h]
                wdec_ref[chunks[i], h] = w_dec[h].astype(BF16)
        return carry

    lax.fori_loop(0, n_chunks // GDN_PAR, solve_body, 0)

    def scan_body(c, carry):
        rs = pl.ds(pl.multiple_of(c * CHUNK, CHUNK), CHUNK)
        state = [state_ref[h] for h in heads]
        state_b = [s_h.astype(BF16) for s_h in state]
        w_s = [jnp.dot(wdec_ref[c, h], state_b[h], preferred_element_type=F32) for h in heads]
        v_new = [(uval_ref[c, h] - w_s[h]).astype(BF16) for h in heads]
        o = [jnp.dot(qg_ref[c, h], state_b[h], preferred_element_type=F32)
             + jnp.dot(attn_ref[c, h], v_new[h], preferred_element_type=F32) for h in heads]
        for h in heads:
            state_ref[h] = (state[h] * egl_ref[c, h][0:1, 0:1]
                            + jnp.dot(kdec_ref[c, h].T.astype(BF16), v_new[h], preferred_element_type=F32))
        for h in heads:
            hs = slice(h * HEAD_DIM_A, (h + 1) * HEAD_DIM_A)
            on = o[h] * lax.rsqrt(jnp.mean(o[h] * o[h], axis=-1, keepdims=True) + EPS) * nw_ref[...]
            y_ref[0, rs, hs] = (on * _silu(z_ref[0, rs, hs])).astype(BF16)
        return carry

    lax.fori_loop(0, n_chunks, scan_body, 0)


def _gdn(qkv_a, z_a, small, conv_w, a_log, dt_bias, norm_w, sb):
    bsz, s, _ = qkv_a.shape
    n_c = sb // CHUNK
    assert n_c % GDN_PAR == 0 and s % sb == 0
    per_head = (n_c, N_HEADS_A, CHUNK, HEAD_DIM_A)
    f = lambda b, i: (b, i, 0)
    const2 = lambda b, i: (0, 0)
    return pl.pallas_call(
        functools.partial(_gdn_kernel, sb=sb),
        grid=(bsz, s // sb),
        in_specs=[
            pl.BlockSpec((1, sb, 3 * WIDTH_A), f),
            pl.BlockSpec((1, sb, WIDTH_A), f),
            pl.BlockSpec((1, sb, LANES), f),
            pl.BlockSpec((CONV_K, 3 * WIDTH_A), const2),
            pl.BlockSpec((1, N_HEADS_A), const2),
            pl.BlockSpec((1, N_HEADS_A), const2),
            pl.BlockSpec((1, HEAD_DIM_A), const2),
        ],
        out_specs=pl.BlockSpec((1, sb, WIDTH_A), f),
        out_shape=jax.ShapeDtypeStruct((bsz, s, WIDTH_A), BF16),
        scratch_shapes=[
            pltpu.VMEM((sb + SUBLANES, 3 * WIDTH_A), F32),
            pltpu.VMEM((sb, 3 * WIDTH_A), F32),
            pltpu.VMEM((N_HEADS_A, HEAD_DIM_A, HEAD_DIM_A), F32),
            pltpu.VMEM(per_head, F32),
            pltpu.VMEM(per_head, BF16),
            pltpu.VMEM(per_head, BF16),
            pltpu.VMEM(per_head, F32),
            pltpu.VMEM((n_c, N_HEADS_A, CHUNK, CHUNK), BF16),
            pltpu.VMEM((n_c, N_HEADS_A, SUBLANES, LANES), F32),
        ],
        compiler_params=_cparams(("parallel", "arbitrary")),
        name="gdn",
    )(qkv_a, z_a, small, conv_w, a_log.reshape(1, -1), dt_bias.reshape(1, -1), norm_w.reshape(1, -1))


QB = 128
FAR_T = 512
FAR_G = FAR_T // LANES
ATT_W = 256
LOG2E = 1.4426950408889634
NEAR_D = 9
NEAR_MIN = 5
INT_MIN = -2 ** 31


def _t5_bucket_np(rel):
    nb = REL_BUCKETS // 2
    max_exact = nb // 2
    side = np.where(rel > 0, nb, 0)
    n = np.abs(rel)
    nf = np.maximum(n, 1).astype(np.float32)
    large = max_exact + (np.log(nf / np.float32(max_exact)) / np.float32(math.log(REL_MAX_DIST / max_exact))
                         * np.float32(nb - max_exact)).astype(np.int32)
    large = np.minimum(large, nb - 1)
    return (side + np.where(n < max_exact, n, large)).astype(np.int32)


def _near_bucket_table():
    r = np.arange(QB)[:, None]
    c = np.arange(LANES)[None, :]
    return np.stack([_t5_bucket_np(c - r - LANES * d) for d in range(NEAR_D)])


FAR_BUCKET = int(_t5_bucket_np(np.array([-(NEAR_MIN * LANES + 1)]))[0])
assert all(int(b) == FAR_BUCKET for b in _t5_bucket_np(-np.arange((NEAR_MIN + 1) * LANES - (QB - 1), 1 << 20, 997)))


def _sortable_key(score):
    bits = pltpu.bitcast(score + 0.0, I32)
    return bits ^ ((bits >> 31) & 0x7FFFFFFF)


PLANE_G = 32


def _bit_transpose32(words):
    a = list(words)
    mask, j = 0x0000FFFF, 16
    while j:
        k = 0
        while k < 32:
            t = (a[k] ^ lax.shift_right_logical(a[k + j], jnp.int32(j))) & mask
            a[k] = a[k] ^ t
            a[k + j] = a[k + j] ^ lax.shift_left(t, jnp.int32(j))
            k = (k + j + 1) & ~j
        j >>= 1
        mask = (mask ^ (mask << j)) & 0xFFFFFFFF
    return a


def _dsa_kernel(rb_ref, qb_ref, qi_ref, small_ref, kb_ref, va_ref, kidx2_ref, tab_ref,
                y_ref, qis_ref, qs_ref, wb_ref, keys_ref, nbias_ref, thr_ref, jlim_ref,
                m_ref, acc_ref, s_ref, p_ref, peak_ref, planes_ref, need_ref, excess_ref, *, seq, k_sel):
    i = pl.program_id(1)
    lane = lax.broadcasted_iota(I32, (QB, LANES), 1)
    row = lax.broadcasted_iota(I32, (QB, LANES), 0)
    even_f = (lane < HEAD_DIM_B).astype(F32)
    even_b = even_f.astype(BF16)
    odd_b = (1.0 - even_f).astype(BF16)

    @pl.when(i == 0)
    def _():
        nbias_ref[...] = jnp.zeros_like(nbias_ref)
        keys_ref[...] = jnp.full(keys_ref.shape, INT_MIN, I32)

        def d_body(d, c0):
            tab = tab_ref[d]

            def b_body(bk, c1):
                hit = tab == bk
                for h in range(N_HEADS_B):
                    nbias_ref[d * N_HEADS_B + h] = jnp.where(hit, rb_ref[bk, h] * LOG2E,
                                                             nbias_ref[d * N_HEADS_B + h])
                return c1

            return lax.fori_loop(0, REL_BUCKETS, b_body, c0)

        lax.fori_loop(0, NEAR_D, d_body, 0)

    for p in range(N_HEADS_B // 2):
        ps = slice(p * LANES, (p + 1) * LANES)
        qi_pair = qi_ref[0, :, ps]
        qis_ref[(2 * p) * QB:(2 * p + 1) * QB, :] = qi_pair * even_b
        qis_ref[(2 * p + 1) * QB:(2 * p + 2) * QB, :] = qi_pair * odd_b
        q_pair = qb_ref[0, :, ps]
        qs_ref[(2 * p) * QB:(2 * p + 1) * QB, :] = q_pair * even_b
        qs_ref[(2 * p + 1) * QB:(2 * p + 2) * QB, :] = q_pair * odd_b
    w_scale = IDX_HEADS ** -0.5 * IDX_DIM ** -0.5
    for h in range(IDX_HEADS):
        wb_ref[h] = jnp.broadcast_to(small_ref[0, :, S_WIDX + h:S_WIDX + h + 1] * w_scale, (QB, LANES))

    limit = i * QB + CHUNK + jnp.where(row >= CHUNK, CHUNK, 0)

    def score_body(t, c0):
        for c in range(FAR_T // ATT_W):
            k0 = t * FAR_T + c * ATT_W
            k_part = kidx2_ref[0, pl.ds(pl.multiple_of(k0, ATT_W), ATT_W), :]
            acc = None
            for h in range(IDX_HEADS):
                dots = _dot_nt(qis_ref[h * QB:(h + 1) * QB, :], k_part)
                term = jnp.maximum(dots, 0.0) * jnp.concatenate([wb_ref[h]] * (ATT_W // LANES), axis=1)
                acc = term if acc is None else acc + term
            for gg in range(ATT_W // LANES):
                col = k0 + gg * LANES + lane
                keys_ref[(k0 // LANES) + gg] = jnp.where(
                    col < limit, _sortable_key(acc[:, gg * LANES:(gg + 1) * LANES]), INT_MIN)
        return c0

    n_groups = i + 1
    n_tiles = i // FAR_G + 1
    lax.fori_loop(0, n_tiles, score_body, 0)

    @pl.when(n_tiles % 2 == 1)
    def _():
        for g in range(FAR_G):
            keys_ref[n_tiles * FAR_G + g] = jnp.full((QB, LANES), INT_MIN, I32)

    def count(pred):
        def t_body(t, acc):
            for g in range(2 * FAR_G):
                grp = t * (2 * FAR_G) + g
                acc = acc + jnp.where(pred(keys_ref[grp], grp * LANES + lane), 1, 0)
            return acc

        acc = lax.fori_loop(0, (n_tiles + 1) // 2, t_body, jnp.zeros((QB, LANES), I32))
        return jnp.broadcast_to(jnp.sum(acc, axis=1, keepdims=True), (QB, LANES))

    thr_ref[...] = jnp.full((QB, LANES), INT_MIN, I32)
    jlim_ref[...] = jnp.full((QB, LANES), -1, I32)

    def lane_total(x):
        return jnp.broadcast_to(jnp.sum(x, axis=1, keepdims=True), (QB, LANES))

    @pl.when(n_groups * QB > k_sel)
    def _():
        def transpose_half(half):
            def row_body(rr, c0):
                rows = pl.ds(pl.multiple_of(rr * SUBLANES, SUBLANES), SUBLANES)
                planes = _bit_transpose32([keys_ref[half * PLANE_G + g, rows, :] ^ INT_MIN for g in range(PLANE_G)])
                for b in range(32):
                    planes_ref[half, b, rows, :] = planes[b]
                return c0

            lax.fori_loop(0, QB // SUBLANES, row_body, 0)

        def search(n_half):
            halves = range(n_half)
            row_groups = (pl.ds(0, QB // 2), pl.ds(QB // 2, QB // 2))
            shape = (QB // 2, LANES)

            def total(x):
                return jnp.broadcast_to(jnp.sum(x, axis=1, keepdims=True), shape)

            def digit(step, rows, state):
                r, eq, above = state
                cls = []
                for half in halves:
                    hi = eq[half] & planes_ref[half, 2 * step, rows, :]
                    lo_plane = planes_ref[half, 2 * step + 1, rows, :]
                    zero_hi = eq[half] ^ hi
                    e11 = hi & lo_plane
                    e01 = zero_hi & lo_plane
                    cls.append((e11, hi ^ e11, e01, zero_hi ^ e01))
                n11, n10, n01 = [sum(lax.population_count(cls[h][d]) for h in halves) for d in range(3)]
                c3 = above + n11
                c2 = c3 + n10
                c1 = c2 + n01
                d3, d2, d1 = [total(c) >= k_sel for c in (c3, c2, c1)]
                eq = tuple(jnp.where(d3, cls[h][0], jnp.where(d2, cls[h][1], jnp.where(d1, cls[h][2], cls[h][3])))
                           for h in halves)
                above = jnp.where(d3, above, jnp.where(d2, c3, jnp.where(d1, c2, c1)))
                value = jnp.where(d3, 3, jnp.where(d2, 2, jnp.where(d1, 1, 0)))
                return r | lax.shift_left(value, 30 - 2 * step), eq, above

            def digit_body(step, carry):
                return tuple(digit(step, rows, state) for rows, state in zip(row_groups, carry))

            zero = jnp.zeros(shape, I32)
            full = jnp.full(shape, -1, I32)
            start = (zero, (full,) * n_half, zero)
            for rows, (r, eq, above) in zip(row_groups, lax.fori_loop(0, 16, digit_body, (start, start))):
                thr_ref[rows, :] = r ^ INT_MIN
                need = k_sel - total(above)
                need_ref[rows, :] = need
                excess_ref[rows, :] = total(sum(lax.population_count(eq[h]) for h in halves)) - need

        transpose_half(0)

        @pl.when(n_groups > PLANE_G)
        def _():
            transpose_half(1)
            search(2)

        @pl.when(n_groups <= PLANE_G)
        def _():
            search(1)

        r = thr_ref[...]
        need = need_ref[...]
        excess = excess_ref[...]
        jlim_ref[...] = jnp.where(r == INT_MIN, -1, seq)

        @pl.when(jnp.max(excess) > 0)
        def _():
            def j_body(step, jl):
                cand = jl + lax.shift_left(jnp.int32(1), (seq.bit_length() - 1) - step)
                cnt = count(lambda kt, col: (kt == r) & (col < cand))
                return jnp.where(cnt < need, cand, jl)

            jl = lax.fori_loop(0, seq.bit_length(), j_body, jnp.zeros((QB, LANES), I32))
            jlim_ref[...] = jnp.where(r == INT_MIN, -1, jl)

    m_ref[...] = jnp.full(m_ref.shape, NEG_BIG, F32)
    acc_ref[...] = jnp.zeros_like(acc_ref)

    part_g = ATT_W // LANES

    def tile_mask(g0, groups):
        negm = []
        for g in range(groups):
            kt = keys_ref[g0 + g]
            sel = (kt > thr_ref[...]) | ((kt == thr_ref[...]) & ((g0 + g) * LANES + lane <= jlim_ref[...]))
            negm.append(jnp.where(sel, 0.0, -jnp.inf))
        return negm

    def logits_pair(g0, groups, p, negm, near):
        heads = ((2 * p, slice(0, QB)), (2 * p + 1, slice(QB, 2 * QB)))
        peak = [None, None]
        for c in range(groups // part_g):
            ks = pl.ds(pl.multiple_of((g0 + c * part_g) * LANES, ATT_W), ATT_W)
            s = _dot_nt(qs_ref[2 * p * QB:(2 * p + 2) * QB, :], kb_ref[0, ks, p * LANES:(p + 1) * LANES])
            for gg in range(part_g):
                g = c * part_g + gg
                for n, (h, rows) in enumerate(heads):
                    extra = negm[g]
                    if near:
                        extra = extra + nbias_ref[jnp.clip(i - (g0 + g), 0, NEAR_D - 1) * N_HEADS_B + h]
                    v = s[rows, gg * LANES:(gg + 1) * LANES] + extra
                    s_ref[0, h, g] = v
                    peak[n] = v if peak[n] is None else jnp.maximum(peak[n], v)
        for n, (h, _) in enumerate(heads):
            peak_ref[0, h] = peak[n]

    def softmax_head(h, groups, bias_scalar):
        m_old = m_ref[h]
        m_new = jnp.maximum(m_old, jnp.max(peak_ref[0, h], axis=1, keepdims=True) + bias_scalar)
        shift = jnp.concatenate([m_new - bias_scalar] * part_g, axis=1)
        for c in range(groups // part_g):
            part = jnp.concatenate([s_ref[0, h, c * part_g + g] for g in range(part_g)], axis=1)
            p_ref[0, h, :, c * ATT_W:(c + 1) * ATT_W] = jnp.exp2(part - shift).astype(BF16)
        acc_ref[h] = jnp.exp2(m_old - m_new) * acc_ref[h]
        m_ref[h] = m_new

    def pv_head(g0, groups, h):
        ks = pl.ds(pl.multiple_of(g0 * LANES, FAR_T), groups * LANES)
        acc_ref[h] += jnp.dot(p_ref[0, h, :, :groups * LANES], va_ref[0, ks, h * LANES:(h + 1) * LANES],
                              preferred_element_type=F32)

    def tile_body(groups, near):
        def body(t, c0):
            g0 = t * groups
            negm = tile_mask(g0, groups)
            n_pair = N_HEADS_B // 2
            for step in range(n_pair + 2):
                if step < n_pair:
                    logits_pair(g0, groups, step, negm, near)
                if 1 <= step <= n_pair:
                    for h in (2 * step - 2, 2 * step - 1):
                        softmax_head(h, groups, 0.0 if near else rb_ref[FAR_BUCKET, h] * LOG2E)
                if step >= 2:
                    for h in (2 * step - 4, 2 * step - 3):
                        pv_head(g0, groups, h)
            return c0

        return body

    far_tiles = jnp.maximum(i - NEAR_MIN, 0) // FAR_G
    lax.fori_loop(0, far_tiles, tile_body(FAR_G, False), 0)
    lax.fori_loop(far_tiles, n_tiles, tile_body(FAR_G, True), 0)

    def head_out(h):
        a = acc_ref[h]
        return a * (1.0 / a[:, HEAD_DIM_B:HEAD_DIM_B + 1])

    for p in range(N_HEADS_B // 2):
        o_odd = pltpu.roll(head_out(2 * p + 1), HEAD_DIM_B, axis=1)
        y_ref[0, :, p * LANES:(p + 1) * LANES] = jnp.where(lane < HEAD_DIM_B, head_out(2 * p), o_odd).astype(BF16)


def _dsa(qb, kb, va, qi, small, rel_bias):
    bsz, s, _ = qb.shape
    assert s % FAR_T == 0 and s // LANES <= 2 * PLANE_G
    k_sel = min(TOPK_KEYS_MAX, s // 4)
    kidx = small[:, :, S_KIDX:S_KIDX + IDX_DIM].astype(BF16)
    kidx2 = jnp.concatenate([kidx, kidx], axis=-1)
    tab = jnp.asarray(_near_bucket_table())
    blk = lambda b, i: (b, i, 0)
    full = lambda b, i: (b, 0, 0)
    one = pl.Buffered(1)
    return pl.pallas_call(
        functools.partial(_dsa_kernel, seq=s, k_sel=k_sel),
        grid=(bsz, s // QB),
        in_specs=[
            pl.BlockSpec(memory_space=pltpu.SMEM),
            pl.BlockSpec((1, QB, WIDTH_B), blk),
            pl.BlockSpec((1, QB, IDX_HEADS * IDX_DIM), blk),
            pl.BlockSpec((1, QB, LANES), blk),
            pl.BlockSpec((1, s, WIDTH_B), full, pipeline_mode=one),
            pl.BlockSpec((1, s, N_HEADS_B * LANES), full, pipeline_mode=one),
            pl.BlockSpec((1, s, LANES), full, pipeline_mode=one),
            pl.BlockSpec((NEAR_D, QB, LANES), lambda b, i: (0, 0, 0), pipeline_mode=one),
        ],
        out_specs=pl.BlockSpec((1, QB, WIDTH_B), blk),
        out_shape=jax.ShapeDtypeStruct((bsz, s, WIDTH_B), BF16),
        scratch_shapes=[
            pltpu.VMEM((IDX_HEADS * QB, LANES), BF16),
            pltpu.VMEM((N_HEADS_B * QB, LANES), BF16),
            pltpu.VMEM((IDX_HEADS, QB, LANES), F32),
            pltpu.VMEM((2 * PLANE_G, QB, LANES), I32),
            pltpu.VMEM((NEAR_D * N_HEADS_B, QB, LANES), F32),
            pltpu.VMEM((QB, LANES), I32),
            pltpu.VMEM((QB, LANES), I32),
            pltpu.VMEM((N_HEADS_B, QB, LANES), F32),
            pltpu.VMEM((N_HEADS_B, QB, LANES), F32),
            pltpu.VMEM((1, N_HEADS_B, FAR_G, QB, LANES), F32),
            pltpu.VMEM((1, N_HEADS_B, QB, FAR_T), BF16),
            pltpu.VMEM((1, N_HEADS_B, QB, LANES), F32),
            pltpu.VMEM((2, 32, QB, LANES), I32),
            pltpu.VMEM((QB, LANES), I32),
            pltpu.VMEM((QB, LANES), I32),
        ],
        compiler_params=_cparams(("parallel", "arbitrary")),
        name="dsa",
    )(rel_bias, qb, qi, small, kb, va, kidx2, tab)


HALF_MASK = 0xFFFF0000


def _pack_halves(t):
    w = t.shape[1] // 2
    bits = pltpu.bitcast(t.astype(BF16).astype(F32), U32)
    return (bits[:, :w] >> 16) | (bits[:, w:] & jnp.uint32(HALF_MASK))


def _unpack_halves(p):
    lo = pltpu.bitcast(p << 16, F32)
    hi = pltpu.bitcast(p & jnp.uint32(HALF_MASK), F32)
    return jnp.concatenate([lo, hi], axis=1)


def _outproj_kernel(ya_ref, yb_ref, x_ref, mod_ref, wo_ref, nw_ref, rw_ref, rbias_ref,
                    xn_ref, hp_ref, ridx_ref, gate_ref):
    wa = ya_ref.shape[2]
    y = (jnp.dot(ya_ref[0], wo_ref[0:wa, :], preferred_element_type=F32)
         + jnp.dot(yb_ref[0], wo_ref[wa:, :], preferred_element_type=F32))
    xn = x_ref[0] + mod_ref[0, 2:3, :] * y
    xn_ref[0] = xn
    ms = jnp.mean(xn * xn, axis=-1, keepdims=True)
    h = xn * lax.rsqrt(ms + EPS) * nw_ref[...] * (1.0 + mod_ref[0, 4:5, :]) + mod_ref[0, 3:4, :]
    hp_ref[0] = _pack_halves(h)

    logits = jnp.dot(h, rw_ref[...], precision=HIGHEST, preferred_element_type=F32) + rbias_ref[...]
    lane = lax.broadcasted_iota(I32, logits.shape, 1)
    cur = logits
    vals, ridx = [], jnp.zeros(logits.shape, I32)
    for k in range(TOP_K):
        mx = jnp.max(cur, axis=1, keepdims=True)
        am = jnp.min(jnp.where(cur == mx, lane, LANES), axis=1, keepdims=True)
        cur = jnp.where(lane == am, -jnp.inf, cur)
        vals.append(mx)
        ridx = jnp.where(lane == k, am, ridx)
    ex = [jnp.exp(v - vals[0]) for v in vals]
    inv = 1.0 / (ex[0] + ex[1] + ex[2] + ex[3])
    gate = jnp.zeros(logits.shape, F32)
    for k in range(TOP_K):
        gate = jnp.where(lane == k, ex[k] * inv, gate)
    ridx_ref[0] = ridx
    gate_ref[0] = gate


def _outproj(y_a, y_b, x, mod_l, w_out_bf, norm_w, router_w, router_b, tm):
    bsz, s, d = x.shape
    n_e = router_w.shape[1]
    rw = jnp.zeros((d, LANES), F32).at[:, :n_e].set(router_w)
    rbias = jnp.full((1, LANES), NEG_BIG, F32).at[0, :n_e].set(router_b)
    blk = lambda b, i: (b, i, 0)
    const2 = lambda b, i: (0, 0)
    return pl.pallas_call(
        _outproj_kernel,
        grid=(bsz, s // tm),
        in_specs=[
            pl.BlockSpec((1, tm, y_a.shape[2]), blk),
            pl.BlockSpec((1, tm, y_b.shape[2]), blk),
            pl.BlockSpec((1, tm, d), blk),
            pl.BlockSpec((1, 6, d), lambda b, i: (b, 0, 0)),
            pl.BlockSpec((d, d), const2),
            pl.BlockSpec((1, d), const2),
            pl.BlockSpec((d, LANES), const2),
            pl.BlockSpec((1, LANES), const2),
        ],
        out_specs=[pl.BlockSpec((1, tm, d), blk), pl.BlockSpec((1, tm, d // 2), blk),
                   pl.BlockSpec((1, tm, LANES), blk), pl.BlockSpec((1, tm, LANES), blk)],
        out_shape=[jax.ShapeDtypeStruct((bsz, s, d), F32), jax.ShapeDtypeStruct((bsz, s, d // 2), U32),
                   jax.ShapeDtypeStruct((bsz, s, LANES), I32), jax.ShapeDtypeStruct((bsz, s, LANES), F32)],
        compiler_params=_cparams(("parallel", "parallel")),
        name="outproj_router",
    )(y_a, y_b, x, mod_l, w_out_bf, norm_w.reshape(1, d), rw, rbias)


MOE_TB = 2048
MOE_RB = 512
MOE_M = 144


def _moe_kernel(first_ref, nch_ref, cbase_ref, cvalid_ref, list_ref, hp_ref, w1_ref, b1_ref, w2_ref, b2_ref,
                gate_ref, x_ref, g2_ref, o_ref, slots_ref, xg_ref, yb_ref, *, tb, rb, table_len):
    sb = pl.program_id(0)
    e = pl.program_id(1)
    dff = w2_ref.shape[1]
    table = sb * table_len + 1

    def gather(j):
        base = cbase_ref[table + j]
        buf = (j + 2) % 2
        for r in range(MOE_M):
            code = list_ref[0, 0, base + r]
            xg_ref[buf, pl.ds(r, 1), :] = hp_ref[pl.ds(code >> 2, 1), :]

    def scatter(j):
        base = cbase_ref[table + j]
        n_valid = cvalid_ref[table + j]
        buf = (j + 2) % 2
        for r in range(MOE_M):
            code = jnp.where(r < n_valid, list_ref[0, 0, base + r], TOP_K * tb)
            slots_ref[code & (TOP_K - 1), pl.ds(code >> 2, 1), :] = yb_ref[buf, pl.ds(r, 1), :]

    @pl.when(e == 0)
    def _():
        gather(0)

    @pl.when(e < N_EXPERTS)
    def _():
        j0 = first_ref[sb * N_EXPERTS + e]

        def chunk(j, carry):
            buf = j % 2
            xb = _unpack_halves(xg_ref[buf]).astype(BF16)
            gather(j + 1)
            scatter(j - 1)
            u = jnp.dot(xb, w1_ref[0], preferred_element_type=F32) + b1_ref[0]
            glu = jnp.minimum(u[:, :dff], SWIGLU_LIMIT)
            lin = jnp.clip(u[:, dff:], -SWIGLU_LIMIT, SWIGLU_LIMIT)
            act = glu * _sigmoid(SWIGLU_ALPHA * glu) * (lin + 1.0)
            y = jnp.dot(act.astype(BF16), w2_ref[0], preferred_element_type=F32) + b2_ref[0]
            yb_ref[buf] = _pack_halves(y)
            return carry

        lax.fori_loop(j0, j0 + nch_ref[sb * N_EXPERTS + e], chunk, 0)

    @pl.when(e == N_EXPERTS)
    def _():
        n_total = first_ref[sb * N_EXPERTS + N_EXPERTS - 1] + nch_ref[sb * N_EXPERTS + N_EXPERTS - 1]
        scatter(n_total - 1)

    @pl.when(e >= N_EXPERTS)
    def _():
        r0 = pl.multiple_of((e - N_EXPERTS) * rb, rb)
        acc = gate_ref[:, 0:1] * _unpack_halves(slots_ref[0, pl.ds(r0, rb), :])
        for k in range(1, TOP_K):
            acc = acc + gate_ref[:, k:k + 1] * _unpack_halves(slots_ref[k, pl.ds(r0, rb), :])
        o_ref[...] = x_ref[...] + g2_ref[0] * acc


def _moe(xn, hp, ridx, gate, g2, w1p, b1p, w2b, b2, tb, rb, layer=0):
    bsz, s, d = xn.shape
    t = bsz * s
    n_super = t // tb
    n_piece = tb // rb
    dff = w2b.shape[1]
    w2map = lambda sb, e, *_: (layer * N_EXPERTS + jnp.minimum(e, N_EXPERTS - 1), 0, 0)
    flat_e = ridx[:, :, :TOP_K].reshape(n_super, tb * TOP_K)
    order = jnp.argsort(flat_e, axis=1, stable=True).astype(I32)
    counts = jnp.sum(flat_e[:, :, None] == jnp.arange(N_EXPERTS, dtype=I32)[None, None, :], axis=1).astype(I32)
    offs = (jnp.cumsum(counts, axis=1) - counts).astype(I32)
    nch = (counts + MOE_M - 1) // MOE_M
    first = (jnp.cumsum(nch, axis=1) - nch).astype(I32)
    table_len = -(-(tb * TOP_K // MOE_M + N_EXPERTS + 2) // SUBLANES) * SUBLANES
    j = jnp.arange(table_len - 1, dtype=I32)[None, :]
    owner = jnp.sum(j[:, :, None] >= (first + nch)[:, None, :], axis=2)
    owner_c = jnp.minimum(owner, N_EXPERTS - 1)
    within = j - jnp.take_along_axis(first, owner_c, axis=1)
    c_valid = jnp.where(owner < N_EXPERTS,
                        jnp.minimum(jnp.take_along_axis(counts, owner_c, axis=1) - within * MOE_M, MOE_M), 0)
    c_base = jnp.where(owner < N_EXPERTS, jnp.take_along_axis(offs, owner_c, axis=1) + within * MOE_M, 0)
    zero = jnp.zeros((n_super, 1), I32)
    c_valid = jnp.concatenate([zero, c_valid.astype(I32)], axis=1)
    c_base = jnp.concatenate([zero, c_base.astype(I32)], axis=1)

    piece = lambda sb, e, *_: (sb * n_piece + jnp.maximum(e - N_EXPERTS, 0), 0)
    wmap = lambda sb, e, *_: (jnp.minimum(e, N_EXPERTS - 1), 0, 0)
    grid_spec = pltpu.PrefetchScalarGridSpec(
        num_scalar_prefetch=4,
        grid=(n_super, N_EXPERTS + n_piece),
        in_specs=[
            pl.BlockSpec((1, 1, tb * TOP_K + MOE_M), lambda sb, e, *_: (sb, 0, 0), memory_space=pltpu.SMEM),
            pl.BlockSpec((tb, d // 2), lambda sb, e, *_: (sb, 0), pipeline_mode=pl.Buffered(1)),
            pl.BlockSpec((1, d, 2 * dff), wmap),
            pl.BlockSpec((1, 1, 2 * dff), wmap),
            pl.BlockSpec((1, dff, d), w2map),
            pl.BlockSpec((1, 1, d), wmap),
            pl.BlockSpec((rb, LANES), piece),
            pl.BlockSpec((rb, d), piece),
            pl.BlockSpec((1, 1, d), lambda sb, e, *_: ((sb * tb) // s, 0, 0)),
        ],
        out_specs=pl.BlockSpec((rb, d), piece),
        scratch_shapes=[
            pltpu.VMEM((TOP_K, tb + SUBLANES, d // 2), U32),
            pltpu.VMEM((2, MOE_M, d // 2), U32),
            pltpu.VMEM((2, MOE_M, d // 2), U32),
        ],
    )
    out = pl.pallas_call(
        functools.partial(_moe_kernel, tb=tb, rb=rb, table_len=table_len),
        grid_spec=grid_spec,
        out_shape=jax.ShapeDtypeStruct((t, d), F32),
        compiler_params=_cparams(("arbitrary", "arbitrary")),
        name="moe",
    )(first.reshape(-1), nch.astype(I32).reshape(-1), c_base.reshape(-1), c_valid.reshape(-1),
      jnp.pad(order, ((0, 0), (0, MOE_M))).reshape(n_super, 1, tb * TOP_K + MOE_M), hp.reshape(t, d // 2),
      w1p, b1p, w2b, b2, gate.reshape(t, LANES), xn.reshape(t, d), g2.reshape(bsz, 1, d))
    return out.reshape(bsz, s, d)


MXU_COLS = 256


def _deinterleave_kernel(w_ref, perm_ref, o_ref):
    half = w_ref.shape[2] // 2
    hw = MXU_COLS // 2
    for b in range(w_ref.shape[2] // MXU_COLS):
        blk = w_ref[0, :, b * MXU_COLS:(b + 1) * MXU_COLS].astype(BF16)
        y = jnp.dot(blk, perm_ref[...], preferred_element_type=F32).astype(BF16)
        o_ref[0, :, b * hw:(b + 1) * hw] = y[:, :hw]
        o_ref[0, :, half + b * hw:half + (b + 1) * hw] = y[:, hw:]


def _deinterleave_cast(w1, layer, rows=512):
    depth, n_e, d, two_f = w1.shape
    src = np.concatenate([np.arange(0, MXU_COLS, 2), np.arange(1, MXU_COLS, 2)])
    perm = np.zeros((MXU_COLS, MXU_COLS), np.float32)
    perm[src, np.arange(MXU_COLS)] = 1.0
    return pl.pallas_call(
        _deinterleave_kernel,
        grid=(n_e, d // rows),
        in_specs=[pl.BlockSpec((1, rows, two_f), lambda e, r: (layer * n_e + e, r, 0)),
                  pl.BlockSpec((MXU_COLS, MXU_COLS), lambda e, r: (0, 0))],
        out_specs=pl.BlockSpec((1, rows, two_f), lambda e, r: (e, r, 0)),
        out_shape=jax.ShapeDtypeStruct((n_e, d, two_f), BF16),
        compiler_params=_cparams(("parallel", "parallel")),
        name="w1_deinterleave",
    )(w1.reshape(depth * n_e, d, two_f), jnp.asarray(perm, BF16))


def _deinterleave_bias(b1_l):
    n_e, two_f = b1_l.shape
    return jnp.concatenate([b1_l[:, 0::2], b1_l[:, 1::2]], axis=1).reshape(n_e, 1, two_f)


def kernel(x, c, rel_bias, mod_w, mod_b, norm_mix_w, norm_ffn_w, w_in, conv_w, a_log, dt_bias, gdn_norm_w,
           q_norm_w, k_norm_w, w_out, router_w, router_b, w1, b1, w2, b2):
    depth = mod_w.shape[0]
    bsz, s, d = x.shape
    mod = _modulation(c, mod_w, mod_b)
    tm = min(512, s)
    tb = min(MOE_TB, bsz * s)
    rb = min(MOE_RB, tb)
    n_e, dff = w2.shape[1], w2.shape[2]
    w2b = w2.astype(BF16).reshape(depth * n_e, dff, d)
    for l in range(depth):
        qkv_a, z_a, qb, kb, vb, qi, small = _inproj(
            x, mod[l], norm_mix_w[l], _permute_w_in(w_in[l]), q_norm_w[l], k_norm_w[l], tm)
        y_a = _gdn(qkv_a, z_a, small, conv_w[l], a_log[l], dt_bias[l], gdn_norm_w[l], sb=tm)
        y_b = _dsa(qb, kb, vb, qi, small, rel_bias)
        xn, hp, ridx, gate = _outproj(y_a, y_b, x, mod[l], w_out[l].astype(BF16), norm_ffn_w[l],
                                      router_w[l], router_b[l], tm)
        x = _moe(xn, hp, ridx, gate, mod[l][:, 5], _deinterleave_cast(w1, l), _deinterleave_bias(b1[l]), w2b,
                 b2[l].reshape(n_e, 1, d), tb, rb, layer=l)
    return x
```

```python
import functools
import math

import jax
import jax.numpy as jnp
import numpy as np
from jax import lax
from jax.experimental import pallas as pl
from jax.experimental.pallas import tpu as pltpu

F32 = jnp.float32
BF16 = jnp.bfloat16
I32 = jnp.int32
U32 = jnp.uint32
HIGHEST = lax.Precision.HIGHEST

LANES = 128
SUBLANES = 8
VMEM_LIMIT_BYTES = 56 * 1024 * 1024

CHUNK = 64
HEAD_DIM_A = 128
N_HEADS_A = 4
WIDTH_A = N_HEADS_A * HEAD_DIM_A
CONV_K = 4
HEAD_DIM_B = 64
N_HEADS_B = 8
WIDTH_B = N_HEADS_B * HEAD_DIM_B
IDX_HEADS = 8
IDX_DIM = 64
TOPK_KEYS_MAX = 256
REL_BUCKETS = 32
REL_MAX_DIST = 1024
N_EXPERTS = 32
TOP_K = 4
SWIGLU_ALPHA = 1.702
SWIGLU_LIMIT = 7.0
EPS = 1e-6
NEG_BIG = -1e30

C_QKVA = 0
C_Z = C_QKVA + 3 * WIDTH_A
C_QB = C_Z + WIDTH_A
C_KB = C_QB + WIDTH_B
C_VB = C_KB + WIDTH_B
C_QI = C_VB + WIDTH_B
C_SMALL = C_QI + IDX_HEADS * IDX_DIM
D_IN_PAD = C_SMALL + LANES
S_KIDX = 0
S_B = IDX_DIM
S_A = S_B + N_HEADS_A
S_WIDX = S_A + N_HEADS_A


def _cparams(sem):
    return pltpu.CompilerParams(dimension_semantics=sem, vmem_limit_bytes=VMEM_LIMIT_BYTES)


def _silu(x):
    return x * (1.0 / (1.0 + jnp.exp(-x)))


def _sigmoid(x):
    return 1.0 / (1.0 + jnp.exp(-x))


def _softplus(x):
    return jnp.maximum(x, 0.0) + jnp.log(1.0 + jnp.exp(-jnp.abs(x)))


def _mod_kernel(c_ref, w_ref, b_ref, o_ref):
    a = _silu(c_ref[...])
    o_ref[0] = jnp.dot(a, w_ref[0], precision=HIGHEST, preferred_element_type=F32) + b_ref[0]


def _modulation(c, mod_w, mod_b):
    depth, d, n = mod_w.shape
    bsz = c.shape[0]
    rows = -(-bsz // SUBLANES) * SUBLANES
    c_pad = jnp.zeros((rows, d), F32).at[:bsz].set(c)
    tn = 1536
    out = pl.pallas_call(
        _mod_kernel,
        grid=(depth, n // tn),
        in_specs=[
            pl.BlockSpec((rows, d), lambda l, j: (0, 0)),
            pl.BlockSpec((1, d, tn), lambda l, j: (l, 0, j)),
            pl.BlockSpec((1, 1, tn), lambda l, j: (l, 0, j)),
        ],
        out_specs=pl.BlockSpec((1, rows, tn), lambda l, j: (l, 0, j)),
        out_shape=jax.ShapeDtypeStruct((depth, rows, n), F32),
        compiler_params=_cparams(("arbitrary", "arbitrary")),
        name="adaln_mod",
    )(c_pad, mod_w, mod_b.reshape(depth, 1, n))
    return out[:, :bsz].reshape(depth, bsz, 6, d)


def _head_rms(t, group_ref, wn, inv_dim):
    t2 = t * t
    hi = t2.astype(BF16)
    lo = (t2 - hi.astype(F32)).astype(BF16)
    ss = (jnp.dot(hi, group_ref[...], preferred_element_type=F32)
          + jnp.dot(lo, group_ref[...], preferred_element_type=F32))
    return t * lax.rsqrt(ss * inv_dim + EPS) * wn


def _inproj_kernel(x_ref, mod_ref, nw_ref, w_ref, group_ref, qn_ref, kn_ref,
                   qkva_ref, z_ref, qb_ref, kb_ref, vb_ref, qi_ref, small_ref):
    x = x_ref[0]
    ms = jnp.mean(x * x, axis=-1, keepdims=True)
    y = x * lax.rsqrt(ms + EPS) * nw_ref[...]
    h = y * (1.0 + mod_ref[0, 1:2, :]) + mod_ref[0, 0:1, :]
    hb = h.astype(BF16)

    def mm(lo, width):
        return jnp.dot(hb, w_ref[:, lo:lo + width], preferred_element_type=F32)

    qkva_ref[0] = mm(C_QKVA, 3 * WIDTH_A)
    z_ref[0] = mm(C_Z, WIDTH_A)
    q = _head_rms(mm(C_QB, WIDTH_B), group_ref, qn_ref[...], 1.0 / HEAD_DIM_B)
    qb_ref[0] = (q * (HEAD_DIM_B ** -0.5 * LOG2E)).astype(BF16)
    k = _head_rms(mm(C_KB, WIDTH_B), group_ref, kn_ref[...], 1.0 / HEAD_DIM_B)
    kb_ref[0] = k.astype(BF16)
    v = mm(C_VB, WIDTH_B)
    lane = lax.broadcasted_iota(I32, (v.shape[0], LANES), 1)
    tail = (lane == HEAD_DIM_B).astype(F32)
    for p in range(N_HEADS_B // 2):
        pair = v[:, p * LANES:(p + 1) * LANES]
        vb_ref[0, :, (2 * p) * LANES:(2 * p + 1) * LANES] = jnp.where(lane < HEAD_DIM_B, pair, tail).astype(BF16)
        vb_ref[0, :, (2 * p + 1) * LANES:(2 * p + 2) * LANES] = jnp.where(
            lane < HEAD_DIM_B, pltpu.roll(pair, HEAD_DIM_B, axis=1), tail).astype(BF16)
    qi_ref[0] = mm(C_QI, IDX_HEADS * IDX_DIM).astype(BF16)
    small_ref[0] = mm(C_SMALL, LANES)


def _permute_w_in(w_in_l):
    d = w_in_l.shape[0]
    o = 0
    qkva = w_in_l[:, o:o + 3 * WIDTH_A]; o += 3 * WIDTH_A
    z = w_in_l[:, o:o + WIDTH_A]; o += WIDTH_A
    b = w_in_l[:, o:o + N_HEADS_A]; o += N_HEADS_A
    a = w_in_l[:, o:o + N_HEADS_A]; o += N_HEADS_A
    qkvb = w_in_l[:, o:o + 3 * WIDTH_B]; o += 3 * WIDTH_B
    qi = w_in_l[:, o:o + IDX_HEADS * IDX_DIM]; o += IDX_HEADS * IDX_DIM
    ki = w_in_l[:, o:o + IDX_DIM]; o += IDX_DIM
    wi = w_in_l[:, o:o + IDX_HEADS]; o += IDX_HEADS
    pad = jnp.zeros((d, LANES - IDX_DIM - 2 * N_HEADS_A - IDX_HEADS), w_in_l.dtype)
    return jnp.concatenate([qkva, z, qkvb, qi, ki, b, a, wi, pad], axis=1).astype(BF16)


def _group_ones(width, group):
    g = np.arange(width) // group
    return jnp.asarray((g[:, None] == g[None, :]).astype(np.float32), dtype=BF16)


def _inproj(x, mod_l, norm_w, w_perm, q_norm_w, k_norm_w, tm):
    bsz, s, d = x.shape
    f = lambda b, i: (b, i, 0)
    const2 = lambda b, i: (0, 0)
    outs = [
        (3 * WIDTH_A, F32), (WIDTH_A, F32), (WIDTH_B, BF16), (WIDTH_B, BF16), (N_HEADS_B * LANES, BF16),
        (IDX_HEADS * IDX_DIM, BF16), (LANES, F32),
    ]
    return pl.pallas_call(
        _inproj_kernel,
        grid=(bsz, s // tm),
        in_specs=[
            pl.BlockSpec((1, tm, d), f),
            pl.BlockSpec((1, 6, d), lambda b, i: (b, 0, 0)),
            pl.BlockSpec((1, d), const2),
            pl.BlockSpec((d, D_IN_PAD), const2),
            pl.BlockSpec((WIDTH_B, WIDTH_B), const2),
            pl.BlockSpec((1, WIDTH_B), const2),
            pl.BlockSpec((1, WIDTH_B), const2),
        ],
        out_specs=[pl.BlockSpec((1, tm, w), f) for w, _ in outs],
        out_shape=[jax.ShapeDtypeStruct((bsz, s, w), dt) for w, dt in outs],
        compiler_params=_cparams(("parallel", "parallel")),
        name="inproj",
    )(x, mod_l, norm_w.reshape(1, d), w_perm, _group_ones(WIDTH_B, HEAD_DIM_B),
      jnp.tile(q_norm_w, N_HEADS_B).reshape(1, WIDTH_B), jnp.tile(k_norm_w, N_HEADS_B).reshape(1, WIDTH_B))


def _dot_nt(a, b, precision=None):
    return lax.dot_general(a, b, (((1,), (1,)), ((), ())), precision=precision, preferred_element_type=F32)


def _split2(x):
    hi = x.astype(BF16)
    return hi, (x - hi.astype(F32)).astype(BF16)


def _split3(x):
    hi = x.astype(BF16)
    r = x - hi.astype(F32)
    mid = r.astype(BF16)
    return hi, mid, (r - mid.astype(F32)).astype(BF16)


def _mm3(a, b):
    return (jnp.dot(a[0], b[0], preferred_element_type=F32) + jnp.dot(a[0], b[1], preferred_element_type=F32)
            + jnp.dot(a[1], b[0], preferred_element_type=F32))


GDN_PAR = 8


def _gdn_kernel(qkv_ref, z_ref, small_ref, convw_ref, alog_ref, dtb_ref, nw_ref,
                y_ref, xe_ref, u_ref, state_ref, uval_ref, wdec_ref, qg_ref, kdec_ref, attn_ref, egl_ref, *, sb):
    n_chunks = sb // CHUNK
    halo = SUBLANES

    @pl.when(pl.program_id(1) == 0)
    def _():
        xe_ref[0:halo, :] = jnp.zeros((halo, 3 * WIDTH_A), F32)
        state_ref[...] = jnp.zeros_like(state_ref)

    xe_ref[halo:halo + sb, :] = qkv_ref[0]

    rows = 128
    for g in range(3 * WIDTH_A // LANES):
        cs = slice(g * LANES, (g + 1) * LANES)
        for r in range(sb // rows):
            base = halo - (CONV_K - 1) + r * rows
            acc = xe_ref[base:base + rows, cs] * convw_ref[0:1, cs]
            for j in range(1, CONV_K):
                acc = acc + xe_ref[base + j:base + j + rows, cs] * convw_ref[j:j + 1, cs]
            u_ref[r * rows:(r + 1) * rows, cs] = _silu(acc)

    xe_ref[0:halo, :] = xe_ref[sb:sb + halo, :]

    wide = N_HEADS_A * CHUNK
    heads = range(N_HEADS_A)
    ii = lax.broadcasted_iota(I32, (CHUNK, wide), 0)
    jj = lax.broadcasted_iota(I32, (CHUNK, wide), 1) & (CHUNK - 1)
    eye_w = (ii == jj).astype(F32)
    tri = (lax.broadcasted_iota(I32, (CHUNK, CHUNK), 0)
           >= lax.broadcasted_iota(I32, (CHUNK, CHUNK), 1)).astype(F32).astype(BF16)
    shift = int(math.log2(CHUNK))
    bd_mask = ((lax.broadcasted_iota(I32, (wide, wide), 0) >> shift)
               == (lax.broadcasted_iota(I32, (wide, wide), 1) >> shift)).astype(F32)

    bd_mask = bd_mask.astype(BF16)

    def block_diag(parts):
        return tuple(jnp.concatenate([m] * N_HEADS_A, axis=0) * bd_mask for m in parts)

    def prepare(c):
        rs = pl.ds(pl.multiple_of(c * CHUNK, CHUNK), CHUNK)
        qn, kn, v, beta, g_b = [], [], [], [], []
        for h in heads:
            q = u_ref[rs, h * HEAD_DIM_A:(h + 1) * HEAD_DIM_A]
            k = u_ref[rs, WIDTH_A + h * HEAD_DIM_A:WIDTH_A + (h + 1) * HEAD_DIM_A]
            v.append(u_ref[rs, 2 * WIDTH_A + h * HEAD_DIM_A:2 * WIDTH_A + (h + 1) * HEAD_DIM_A])
            qn.append(q * (lax.rsqrt(jnp.sum(q * q, axis=-1, keepdims=True) + EPS) * (HEAD_DIM_A ** -0.5)))
            kn.append(k * lax.rsqrt(jnp.sum(k * k, axis=-1, keepdims=True) + EPS))
            beta.append(_sigmoid(small_ref[0, rs, S_B + h:S_B + h + 1]))
            g = -jnp.exp(alog_ref[0:1, h:h + 1]) * _softplus(small_ref[0, rs, S_A + h:S_A + h + 1]
                                                            + dtb_ref[0:1, h:h + 1])
            g_b.append(jnp.broadcast_to(g, (CHUNK, CHUNK)))
        gc_w = sum(jnp.dot(tri, part, preferred_element_type=F32) for part in _split3(jnp.concatenate(g_b, axis=1)))
        gc_row = jnp.sum(jnp.where(ii == jj, gc_w, 0.0), axis=0, keepdims=True)
        decay_w = jnp.exp(jnp.where(ii >= jj, gc_w - gc_row, NEG_BIG))
        k_beta = [kn[h] * beta[h] for h in heads]
        kk_w = jnp.concatenate([_dot_nt(k_beta[h].astype(BF16), kn[h].astype(BF16)) for h in heads], axis=1)
        a_w = -jnp.where(ii > jj, kk_w * decay_w, 0.0)
        gc = [gc_w[:, h * CHUNK:h * CHUNK + 1] for h in heads]
        egc = [jnp.exp(gc[h]) for h in heads]
        qk = [_dot_nt(qn[h].astype(BF16), kn[h].astype(BF16)) for h in heads]
        for h in heads:
            g_last = gc[h][CHUNK - 1:CHUNK, :]
            qg_ref[c, h] = (qn[h] * egc[h]).astype(BF16)
            kdec_ref[c, h] = kn[h] * jnp.exp(g_last - gc[h])
            attn_ref[c, h] = (qk[h] * decay_w[:, h * CHUNK:(h + 1) * CHUNK]).astype(BF16)
            egl_ref[c, h] = jnp.broadcast_to(jnp.exp(g_last), (SUBLANES, LANES))
        return a_w, [_split2(v[h] * beta[h]) for h in heads], [_split2(k_beta[h] * egc[h]) for h in heads]

    def solve_body(cg, carry):
        group = range(GDN_PAR)
        chunks = [cg * GDN_PAR + i for i in group]
        pre = [prepare(c) for c in chunks]
        t_w = [eye_w + pre[i][0] for i in group]
        p_parts = [_split2(pre[i][0]) for i in group]
        bd = [block_diag(p_parts[i]) for i in group]
        for _ in range(shift - 1):
            prod = [_mm3(p_parts[i], bd[i]) for i in group]
            p_parts = [_split2(prod[i]) for i in group]
            bd = [block_diag(p_parts[i]) for i in group]
            upd = [_mm3(_split2(t_w[i]), bd[i]) for i in group]
            t_w = [t_w[i] + upd[i] for i in group]
        for i in group:
            t_h = [_split2(t_w[i][:, h * CHUNK:(h + 1) * CHUNK]) for h in heads]
            u_val = [_mm3(t_h[h], pre[i][1][h]) for h in heads]
            w_dec = [_mm3(t_h[h], pre[i][2][h]) for h in heads]
            for h in heads:
                uval_ref[chunks[i], h] = u_val[h]
                wdec_ref[chunks[i], h] = w_dec[h].astype(BF16)
        return carry

    lax.fori_loop(0, n_chunks // GDN_PAR, solve_body, 0)

    def scan_body(c, carry):
        rs = pl.ds(pl.multiple_of(c * CHUNK, CHUNK), CHUNK)
        state = [state_ref[h] for h in heads]
        state_b = [s_h.astype(BF16) for s_h in state]
        w_s = [jnp.dot(wdec_ref[c, h], state_b[h], preferred_element_type=F32) for h in heads]
        v_new = [(uval_ref[c, h] - w_s[h]).astype(BF16) for h in heads]
        o = [jnp.dot(qg_ref[c, h], state_b[h], preferred_element_type=F32)
             + jnp.dot(attn_ref[c, h], v_new[h], preferred_element_type=F32) for h in heads]
        for h in heads:
            state_ref[h] = (state[h] * egl_ref[c, h][0:1, 0:1]
                            + jnp.dot(kdec_ref[c, h].T.astype(BF16), v_new[h], preferred_element_type=F32))
        for h in heads:
            hs = slice(h * HEAD_DIM_A, (h + 1) * HEAD_DIM_A)
            on = o[h] * lax.rsqrt(jnp.mean(o[h] * o[h], axis=-1, keepdims=True) + EPS) * nw_ref[...]
            y_ref[0, rs, hs] = (on * _silu(z_ref[0, rs, hs])).astype(BF16)
        return carry

    lax.fori_loop(0, n_chunks, scan_body, 0)


def _gdn(qkv_a, z_a, small, conv_w, a_log, dt_bias, norm_w, sb):
    bsz, s, _ = qkv_a.shape
    n_c = sb // CHUNK
    assert n_c % GDN_PAR == 0 and s % sb == 0
    per_head = (n_c, N_HEADS_A, CHUNK, HEAD_DIM_A)
    f = lambda b, i: (b, i, 0)
    const2 = lambda b, i: (0, 0)
    return pl.pallas_call(
        functools.partial(_gdn_kernel, sb=sb),
        grid=(bsz, s // sb),
        in_specs=[
            pl.BlockSpec((1, sb, 3 * WIDTH_A), f),
            pl.BlockSpec((1, sb, WIDTH_A), f),
            pl.BlockSpec((1, sb, LANES), f),
            pl.BlockSpec((CONV_K, 3 * WIDTH_A), const2),
            pl.BlockSpec((1, N_HEADS_A), const2),
            pl.BlockSpec((1, N_HEADS_A), const2),
            pl.BlockSpec((1, HEAD_DIM_A), const2),
        ],
        out_specs=pl.BlockSpec((1, sb, WIDTH_A), f),
        out_shape=jax.ShapeDtypeStruct((bsz, s, WIDTH_A), BF16),
        scratch_shapes=[
            pltpu.VMEM((sb + SUBLANES, 3 * WIDTH_A), F32),
            pltpu.VMEM((sb, 3 * WIDTH_A), F32),
            pltpu.VMEM((N_HEADS_A, HEAD_DIM_A, HEAD_DIM_A), F32),
            pltpu.VMEM(per_head, F32),
            pltpu.VMEM(per_head, BF16),
            pltpu.VMEM(per_head, BF16),
            pltpu.VMEM(per_head, F32),
            pltpu.VMEM((n_c, N_HEADS_A, CHUNK, CHUNK), BF16),
            pltpu.VMEM((n_c, N_HEADS_A, SUBLANES, LANES), F32),
        ],
        compiler_params=_cparams(("parallel", "arbitrary")),
        name="gdn",
    )(qkv_a, z_a, small, conv_w, a_log.reshape(1, -1), dt_bias.reshape(1, -1), norm_w.reshape(1, -1))


QB = 128
FAR_T = 512
FAR_G = FAR_T // LANES
ATT_W = 256
LOG2E = 1.4426950408889634
NEAR_D = 9
NEAR_MIN = 5
INT_MIN = -2 ** 31


def _t5_bucket_np(rel):
    nb = REL_BUCKETS // 2
    max_exact = nb // 2
    side = np.where(rel > 0, nb, 0)
    n = np.abs(rel)
    nf = np.maximum(n, 1).astype(np.float32)
    large = max_exact + (np.log(nf / np.float32(max_exact)) / np.float32(math.log(REL_MAX_DIST / max_exact))
                         * np.float32(nb - max_exact)).astype(np.int32)
    large = np.minimum(large, nb - 1)
    return (side + np.where(n < max_exact, n, large)).astype(np.int32)


def _near_bucket_table():
    r = np.arange(QB)[:, None]
    c = np.arange(LANES)[None, :]
    return np.stack([_t5_bucket_np(c - r - LANES * d) for d in range(NEAR_D)])


FAR_BUCKET = int(_t5_bucket_np(np.array([-(NEAR_MIN * LANES + 1)]))[0])
assert all(int(b) == FAR_BUCKET for b in _t5_bucket_np(-np.arange((NEAR_MIN + 1) * LANES - (QB - 1), 1 << 20, 997)))


def _sortable_key(score):
    bits = pltpu.bitcast(score + 0.0, I32)
    return bits ^ ((bits >> 31) & 0x7FFFFFFF)


PLANE_G = 32


def _bit_transpose32(words):
    a = list(words)
    mask, j = 0x0000FFFF, 16
    while j:
        k = 0
        while k < 32:
            t = (a[k] ^ lax.shift_right_logical(a[k + j], jnp.int32(j))) & mask
            a[k] = a[k] ^ t
            a[k + j] = a[k + j] ^ lax.shift_left(t, jnp.int32(j))
            k = (k + j + 1) & ~j
        j >>= 1
        mask = (mask ^ (mask << j)) & 0xFFFFFFFF
    return a


def _dsa_kernel(rb_ref, qb_ref, qi_ref, small_ref, kb_ref, va_ref, kidx2_ref, tab_ref,
                y_ref, qis_ref, qs_ref, wb_ref, keys_ref, nbias_ref, thr_ref, jlim_ref,
                m_ref, acc_ref, s_ref, p_ref, peak_ref, planes_ref, need_ref, excess_ref, *, seq, k_sel):
    i = pl.program_id(1)
    lane = lax.broadcasted_iota(I32, (QB, LANES), 1)
    row = lax.broadcasted_iota(I32, (QB, LANES), 0)
    even_f = (lane < HEAD_DIM_B).astype(F32)
    even_b = even_f.astype(BF16)
    odd_b = (1.0 - even_f).astype(BF16)

    @pl.when(i == 0)
    def _():
        nbias_ref[...] = jnp.zeros_like(nbias_ref)
        keys_ref[...] = jnp.full(keys_ref.shape, INT_MIN, I32)

        def d_body(d, c0):
            tab = tab_ref[d]

            def b_body(bk, c1):
                hit = tab == bk
                for h in range(N_HEADS_B):
                    nbias_ref[d * N_HEADS_B + h] = jnp.where(hit, rb_ref[bk, h] * LOG2E,
                                                             nbias_ref[d * N_HEADS_B + h])
                return c1

            return lax.fori_loop(0, REL_BUCKETS, b_body, c0)

        lax.fori_loop(0, NEAR_D, d_body, 0)

    for p in range(N_HEADS_B // 2):
        ps = slice(p * LANES, (p + 1) * LANES)
        qi_pair = qi_ref[0, :, ps]
        qis_ref[(2 * p) * QB:(2 * p + 1) * QB, :] = qi_pair * even_b
        qis_ref[(2 * p + 1) * QB:(2 * p + 2) * QB, :] = qi_pair * odd_b
        q_pair = qb_ref[0, :, ps]
        qs_ref[(2 * p) * QB:(2 * p + 1) * QB, :] = q_pair * even_b
        qs_ref[(2 * p + 1) * QB:(2 * p + 2) * QB, :] = q_pair * odd_b
    w_scale = IDX_HEADS ** -0.5 * IDX_DIM ** -0.5
    for h in range(IDX_HEADS):
        wb_ref[h] = jnp.broadcast_to(small_ref[0, :, S_WIDX + h:S_WIDX + h + 1] * w_scale, (QB, LANES))

    limit = i * QB + CHUNK + jnp.where(row >= CHUNK, CHUNK, 0)

    def score_body(t, c0):
        for c in range(FAR_T // ATT_W):
            k0 = t * FAR_T + c * ATT_W
            k_part = kidx2_ref[0, pl.ds(pl.multiple_of(k0, ATT_W), ATT_W), :]
            acc = None
            for h in range(IDX_HEADS):
                dots = _dot_nt(qis_ref[h * QB:(h + 1) * QB, :], k_part)
                term = jnp.maximum(dots, 0.0) * jnp.concatenate([wb_ref[h]] * (ATT_W // LANES), axis=1)
                acc = term if acc is None else acc + term
            for gg in range(ATT_W // LANES):
                col = k0 + gg * LANES + lane
                keys_ref[(k0 // LANES) + gg] = jnp.where(
                    col < limit, _sortable_key(acc[:, gg * LANES:(gg + 1) * LANES]), INT_MIN)
        return c0

    n_groups = i + 1
    n_tiles = i // FAR_G + 1
    lax.fori_loop(0, n_tiles, score_body, 0)

    @pl.when(n_tiles % 2 == 1)
    def _():
        for g in range(FAR_G):
            keys_ref[n_tiles * FAR_G + g] = jnp.full((QB, LANES), INT_MIN, I32)

    def count(pred):
        def t_body(t, acc):
            for g in range(2 * FAR_G):
                grp = t * (2 * FAR_G) + g
                acc = acc + jnp.where(pred(keys_ref[grp], grp * LANES + lane), 1, 0)
            return acc

        acc = lax.fori_loop(0, (n_tiles + 1) // 2, t_body, jnp.zeros((QB, LANES), I32))
        return jnp.broadcast_to(jnp.sum(acc, axis=1, keepdims=True), (QB, LANES))

    thr_ref[...] = jnp.full((QB, LANES), INT_MIN, I32)
    jlim_ref[...] = jnp.full((QB, LANES), -1, I32)

    def lane_total(x):
        return jnp.broadcast_to(jnp.sum(x, axis=1, keepdims=True), (QB, LANES))

    @pl.when(n_groups * QB > k_sel)
    def _():
        def transpose_half(half):
            def row_body(rr, c0):
                rows = pl.ds(pl.multiple_of(rr * SUBLANES, SUBLANES), SUBLANES)
                planes = _bit_transpose32([keys_ref[half * PLANE_G + g, rows, :] ^ INT_MIN for g in range(PLANE_G)])
                for b in range(32):
                    planes_ref[half, b, rows, :] = planes[b]
                return c0

            lax.fori_loop(0, QB // SUBLANES, row_body, 0)

        def search(n_half):
            halves = range(n_half)
            row_groups = (pl.ds(0, QB // 2), pl.ds(QB // 2, QB // 2))
            shape = (QB // 2, LANES)

            def total(x):
                return jnp.broadcast_to(jnp.sum(x, axis=1, keepdims=True), shape)

            def digit(step, rows, state):
                r, eq, above = state
                cls = []
                for half in halves:
                    hi = eq[half] & planes_ref[half, 2 * step, rows, :]
                    lo_plane = planes_ref[half, 2 * step + 1, rows, :]
                    zero_hi = eq[half] ^ hi
                    e11 = hi & lo_plane
                    e01 = zero_hi & lo_plane
                    cls.append((e11, hi ^ e11, e01, zero_hi ^ e01))
                n11, n10, n01 = [sum(lax.population_count(cls[h][d]) for h in halves) for d in range(3)]
                c3 = above + n11
                c2 = c3 + n10
                c1 = c2 + n01
                d3, d2, d1 = [total(c) >= k_sel for c in (c3, c2, c1)]
                eq = tuple(jnp.where(d3, cls[h][0], jnp.where(d2, cls[h][1], jnp.where(d1, cls[h][2], cls[h][3])))
                           for h in halves)
                above = jnp.where(d3, above, jnp.where(d2, c3, jnp.where(d1, c2, c1)))
                value = jnp.where(d3, 3, jnp.where(d2, 2, jnp.where(d1, 1, 0)))
                return r | lax.shift_left(value, 30 - 2 * step), eq, above

            def digit_body(step, carry):
                return tuple(digit(step, rows, state) for rows, state in zip(row_groups, carry))

            zero = jnp.zeros(shape, I32)
            full = jnp.full(shape, -1, I32)
            start = (zero, (full,) * n_half, zero)
            for rows, (r, eq, above) in zip(row_groups, lax.fori_loop(0, 16, digit_body, (start, start))):
                thr_ref[rows, :] = r ^ INT_MIN
                need = k_sel - total(above)
                need_ref[rows, :] = need
                excess_ref[rows, :] = total(sum(lax.population_count(eq[h]) for h in halves)) - need

        transpose_half(0)

        @pl.when(n_groups > PLANE_G)
        def _():
            transpose_half(1)
            search(2)

        @pl.when(n_groups <= PLANE_G)
        def _():
            search(1)

        r = thr_ref[...]
        need = need_ref[...]
        excess = excess_ref[...]
        jlim_ref[...] = jnp.where(r == INT_MIN, -1, seq)

        @pl.when(jnp.max(excess) > 0)
        def _():
            def j_body(step, jl):
                cand = jl + lax.shift_left(jnp.int32(1), (seq.bit_length() - 1) - step)
                cnt = count(lambda kt, col: (kt == r) & (col < cand))
                return jnp.where(cnt < need, cand, jl)

            jl = lax.fori_loop(0, seq.bit_length(), j_body, jnp.zeros((QB, LANES), I32))
            jlim_ref[...] = jnp.where(r == INT_MIN, -1, jl)

    m_ref[...] = jnp.full(m_ref.shape, NEG_BIG, F32)
    acc_ref[...] = jnp.zeros_like(acc_ref)

    part_g = ATT_W // LANES

    def tile_mask(g0, groups):
        negm = []
        for g in range(groups):
            kt = keys_ref[g0 + g]
            sel = (kt > thr_ref[...]) | ((kt == thr_ref[...]) & ((g0 + g) * LANES + lane <= jlim_ref[...]))
            negm.append(jnp.where(sel, 0.0, -jnp.inf))
        return negm

    def logits_pair(g0, groups, p, negm, near):
        heads = ((2 * p, slice(0, QB)), (2 * p + 1, slice(QB, 2 * QB)))
        peak = [None, None]
        for c in range(groups // part_g):
            ks = pl.ds(pl.multiple_of((g0 + c * part_g) * LANES, ATT_W), ATT_W)
            s = _dot_nt(qs_ref[2 * p * QB:(2 * p + 2) * QB, :], kb_ref[0, ks, p * LANES:(p + 1) * LANES])
            for gg in range(part_g):
                g = c * part_g + gg
                for n, (h, rows) in enumerate(heads):
                    extra = negm[g]
                    if near:
                        extra = extra + nbias_ref[jnp.clip(i - (g0 + g), 0, NEAR_D - 1) * N_HEADS_B + h]
                    v = s[rows, gg * LANES:(gg + 1) * LANES] + extra
                    s_ref[0, h, g] = v
                    peak[n] = v if peak[n] is None else jnp.maximum(peak[n], v)
        for n, (h, _) in enumerate(heads):
            peak_ref[0, h] = peak[n]

    def softmax_head(h, groups, bias_scalar):
        m_old = m_ref[h]
        m_new = jnp.maximum(m_old, jnp.max(peak_ref[0, h], axis=1, keepdims=True) + bias_scalar)
        shift = jnp.concatenate([m_new - bias_scalar] * part_g, axis=1)
        for c in range(groups // part_g):
            part = jnp.concatenate([s_ref[0, h, c * part_g + g] for g in range(part_g)], axis=1)
            p_ref[0, h, :, c * ATT_W:(c + 1) * ATT_W] = jnp.exp2(part - shift).astype(BF16)
        acc_ref[h] = jnp.exp2(m_old - m_new) * acc_ref[h]
        m_ref[h] = m_new

    def pv_head(g0, groups, h):
        ks = pl.ds(pl.multiple_of(g0 * LANES, FAR_T), groups * LANES)
        acc_ref[h] += jnp.dot(p_ref[0, h, :, :groups * LANES], va_ref[0, ks, h * LANES:(h + 1) * LANES],
                              preferred_element_type=F32)

    def tile_body(groups, near):
        def body(t, c0):
            g0 = t * groups
            negm = tile_mask(g0, groups)
            n_pair = N_HEADS_B // 2
            for step in range(n_pair + 2):
                if step < n_pair:
                    logits_pair(g0, groups, step, negm, near)
                if 1 <= step <= n_pair:
                    for h in (2 * step - 2, 2 * step - 1):
                        softmax_head(h, groups, 0.0 if near else rb_ref[FAR_BUCKET, h] * LOG2E)
                if step >= 2:
                    for h in (2 * step - 4, 2 * step - 3):
                        pv_head(g0, groups, h)
            return c0

        return body

    far_tiles = jnp.maximum(i - NEAR_MIN, 0) // FAR_G
    lax.fori_loop(0, far_tiles, tile_body(FAR_G, False), 0)
    lax.fori_loop(far_tiles, n_tiles, tile_body(FAR_G, True), 0)

    def head_out(h):
        a = acc_ref[h]
        return a * (1.0 / a[:, HEAD_DIM_B:HEAD_DIM_B + 1])

    for p in range(N_HEADS_B // 2):
        o_odd = pltpu.roll(head_out(2 * p + 1), HEAD_DIM_B, axis=1)
        y_ref[0, :, p * LANES:(p + 1) * LANES] = jnp.where(lane < HEAD_DIM_B, head_out(2 * p), o_odd).astype(BF16)


def _dsa(qb, kb, va, qi, small, rel_bias):
    bsz, s, _ = qb.shape
    assert s % FAR_T == 0 and s // LANES <= 2 * PLANE_G
    k_sel = min(TOPK_KEYS_MAX, s // 4)
    kidx = small[:, :, S_KIDX:S_KIDX + IDX_DIM].astype(BF16)
    kidx2 = jnp.concatenate([kidx, kidx], axis=-1)
    tab = jnp.asarray(_near_bucket_table())
    blk = lambda b, i: (b, i, 0)
    full = lambda b, i: (b, 0, 0)
    one = pl.Buffered(1)
    return pl.pallas_call(
        functools.partial(_dsa_kernel, seq=s, k_sel=k_sel),
        grid=(bsz, s // QB),
        in_specs=[
            pl.BlockSpec(memory_space=pltpu.SMEM),
            pl.BlockSpec((1, QB, WIDTH_B), blk),
            pl.BlockSpec((1, QB, IDX_HEADS * IDX_DIM), blk),
            pl.BlockSpec((1, QB, LANES), blk),
            pl.BlockSpec((1, s, WIDTH_B), full, pipeline_mode=one),
            pl.BlockSpec((1, s, N_HEADS_B * LANES), full, pipeline_mode=one),
            pl.BlockSpec((1, s, LANES), full, pipeline_mode=one),
            pl.BlockSpec((NEAR_D, QB, LANES), lambda b, i: (0, 0, 0), pipeline_mode=one),
        ],
        out_specs=pl.BlockSpec((1, QB, WIDTH_B), blk),
        out_shape=jax.ShapeDtypeStruct((bsz, s, WIDTH_B), BF16),
        scratch_shapes=[
            pltpu.VMEM((IDX_HEADS * QB, LANES), BF16),
            pltpu.VMEM((N_HEADS_B * QB, LANES), BF16),
            pltpu.VMEM((IDX_HEADS, QB, LANES), F32),
            pltpu.VMEM((2 * PLANE_G, QB, LANES), I32),
            pltpu.VMEM((NEAR_D * N_HEADS_B, QB, LANES), F32),
            pltpu.VMEM((QB, LANES), I32),
            pltpu.VMEM((QB, LANES), I32),
            pltpu.VMEM((N_HEADS_B, QB, LANES), F32),
            pltpu.VMEM((N_HEADS_B, QB, LANES), F32),
            pltpu.VMEM((1, N_HEADS_B, FAR_G, QB, LANES), F32),
            pltpu.VMEM((1, N_HEADS_B, QB, FAR_T), BF16),
            pltpu.VMEM((1, N_HEADS_B, QB, LANES), F32),
            pltpu.VMEM((2, 32, QB, LANES), I32),
            pltpu.VMEM((QB, LANES), I32),
            pltpu.VMEM((QB, LANES), I32),
        ],
        compiler_params=_cparams(("parallel", "arbitrary")),
        name="dsa",
    )(rel_bias, qb, qi, small, kb, va, kidx2, tab)


HALF_MASK = 0xFFFF0000


def _pack_halves(t):
    w = t.shape[1] // 2
    bits = pltpu.bitcast(t.astype(BF16).astype(F32), U32)
    return (bits[:, :w] >> 16) | (bits[:, w:] & jnp.uint32(HALF_MASK))


def _unpack_halves(p):
    lo = pltpu.bitcast(p << 16, F32)
    hi = pltpu.bitcast(p & jnp.uint32(HALF_MASK), F32)
    return jnp.concatenate([lo, hi], axis=1)


def _outproj_kernel(ya_ref, yb_ref, x_ref, mod_ref, wo_ref, nw_ref, rw_ref, rbias_ref,
                    xn_ref, hp_ref, ridx_ref, gate_ref):
    wa = ya_ref.shape[2]
    y = (jnp.dot(ya_ref[0], wo_ref[0:wa, :], preferred_element_type=F32)
         + jnp.dot(yb_ref[0], wo_ref[wa:, :], preferred_element_type=F32))
    xn = x_ref[0] + mod_ref[0, 2:3, :] * y
    xn_ref[0] = xn
    ms = jnp.mean(xn * xn, axis=-1, keepdims=True)
    h = xn * lax.rsqrt(ms + EPS) * nw_ref[...] * (1.0 + mod_ref[0, 4:5, :]) + mod_ref[0, 3:4, :]
    hp_ref[0] = _pack_halves(h)

    logits = jnp.dot(h, rw_ref[...], precision=HIGHEST, preferred_element_type=F32) + rbias_ref[...]
    lane = lax.broadcasted_iota(I32, logits.shape, 1)
    cur = logits
    vals, ridx = [], jnp.zeros(logits.shape, I32)
    for k in range(TOP_K):
        mx = jnp.max(cur, axis=1, keepdims=True)
        am = jnp.min(jnp.where(cur == mx, lane, LANES), axis=1, keepdims=True)
        cur = jnp.where(lane == am, -jnp.inf, cur)
        vals.append(mx)
        ridx = jnp.where(lane == k, am, ridx)
    ex = [jnp.exp(v - vals[0]) for v in vals]
    inv = 1.0 / (ex[0] + ex[1] + ex[2] + ex[3])
    gate = jnp.zeros(logits.shape, F32)
    for k in range(TOP_K):
        gate = jnp.where(lane == k, ex[k] * inv, gate)
    ridx_ref[0] = ridx
    gate_ref[0] = gate


def _outproj(y_a, y_b, x, mod_l, w_out_bf, norm_w, router_w, router_b, tm):
    bsz, s, d = x.shape
    n_e = router_w.shape[1]
    rw = jnp.zeros((d, LANES), F32).at[:, :n_e].set(router_w)
    rbias = jnp.full((1, LANES), NEG_BIG, F32).at[0, :n_e].set(router_b)
    blk = lambda b, i: (b, i, 0)
    const2 = lambda b, i: (0, 0)
    return pl.pallas_call(
        _outproj_kernel,
        grid=(bsz, s // tm),
        in_specs=[
            pl.BlockSpec((1, tm, y_a.shape[2]), blk),
            pl.BlockSpec((1, tm, y_b.shape[2]), blk),
            pl.BlockSpec((1, tm, d), blk),
            pl.BlockSpec((1, 6, d), lambda b, i: (b, 0, 0)),
            pl.BlockSpec((d, d), const2),
            pl.BlockSpec((1, d), const2),
            pl.BlockSpec((d, LANES), const2),
            pl.BlockSpec((1, LANES), const2),
        ],
        out_specs=[pl.BlockSpec((1, tm, d), blk), pl.BlockSpec((1, tm, d // 2), blk),
                   pl.BlockSpec((1, tm, LANES), blk), pl.BlockSpec((1, tm, LANES), blk)],
        out_shape=[jax.ShapeDtypeStruct((bsz, s, d), F32), jax.ShapeDtypeStruct((bsz, s, d // 2), U32),
                   jax.ShapeDtypeStruct((bsz, s, LANES), I32), jax.ShapeDtypeStruct((bsz, s, LANES), F32)],
        compiler_params=_cparams(("parallel", "parallel")),
        name="outproj_router",
    )(y_a, y_b, x, mod_l, w_out_bf, norm_w.reshape(1, d), rw, rbias)


MOE_TB = 2048
MOE_RB = 512
MOE_M = 144


def _moe_kernel(first_ref, nch_ref, cbase_ref, tok_ref, row_ref, hp_ref, w1_ref, b1_ref, w2_ref, b2_ref,
                gate_ref, x_ref, g2_ref, o_ref, slots_ref, xg_ref, yb_ref, *, tb, rb, table_len):
    sb = pl.program_id(0)
    e = pl.program_id(1)
    dff = w2_ref.shape[1]
    plane = tb + SUBLANES
    table = sb * table_len + 1

    def gather(j):
        base = cbase_ref[table + j]
        buf = (j + 2) % 2
        for r in range(MOE_M):
            xg_ref[buf, pl.ds(r, 1), :] = hp_ref[pl.ds(tok_ref[0, 0, base + r], 1), :]

    def scatter(j):
        base = cbase_ref[table + j]
        buf = (j + 2) % 2
        for r in range(MOE_M):
            slots_ref[pl.ds(row_ref[0, 0, base + r], 1), :] = yb_ref[buf, pl.ds(r, 1), :]

    @pl.when(e == 0)
    def _():
        yb_ref[...] = jnp.zeros_like(yb_ref)
        gather(0)

    @pl.when(e < N_EXPERTS)
    def _():
        j0 = first_ref[sb * N_EXPERTS + e]

        def chunk(j, carry):
            buf = j % 2
            xb = _unpack_halves(xg_ref[buf]).astype(BF16)
            gather(j + 1)
            scatter(j - 1)
            u = jnp.dot(xb, w1_ref[0], preferred_element_type=F32) + b1_ref[0]
            glu = jnp.minimum(u[:, :dff], SWIGLU_LIMIT)
            lin = jnp.clip(u[:, dff:], -SWIGLU_LIMIT, SWIGLU_LIMIT)
            act = glu * _sigmoid(SWIGLU_ALPHA * glu) * (lin + 1.0)
            y = jnp.dot(act.astype(BF16), w2_ref[0], preferred_element_type=F32) + b2_ref[0]
            yb_ref[buf] = _pack_halves(y)
            return carry

        lax.fori_loop(j0, j0 + nch_ref[sb * N_EXPERTS + e], chunk, 0)

    @pl.when(e == N_EXPERTS)
    def _():
        n_total = first_ref[sb * N_EXPERTS + N_EXPERTS - 1] + nch_ref[sb * N_EXPERTS + N_EXPERTS - 1]
        scatter(n_total - 1)

    @pl.when(e >= N_EXPERTS)
    def _():
        r0 = pl.multiple_of((e - N_EXPERTS) * rb, rb)
        acc = gate_ref[:, 0:1] * _unpack_halves(slots_ref[pl.ds(r0, rb), :])
        for k in range(1, TOP_K):
            rows = pl.ds(pl.multiple_of(k * plane + r0, SUBLANES), rb)
            acc = acc + gate_ref[:, k:k + 1] * _unpack_halves(slots_ref[rows, :])
        o_ref[...] = x_ref[...] + g2_ref[0] * acc


def _moe(xn, hp, ridx, gate, g2, w1p, b1p, w2b, b2, tb, rb, layer=0):
    bsz, s, d = xn.shape
    t = bsz * s
    n_super = t // tb
    n_piece = tb // rb
    dff = w2b.shape[1]
    w2map = lambda sb, e, *_: (layer * N_EXPERTS + jnp.minimum(e, N_EXPERTS - 1), 0, 0)
    flat_e = ridx[:, :, :TOP_K].reshape(n_super, tb * TOP_K)
    order = jnp.argsort(flat_e, axis=1, stable=True).astype(I32)
    counts = jnp.sum(flat_e[:, :, None] == jnp.arange(N_EXPERTS, dtype=I32)[None, None, :], axis=1).astype(I32)
    offs = (jnp.cumsum(counts, axis=1) - counts).astype(I32)
    n_list = tb * TOP_K
    plane = tb + SUBLANES
    tok_list = jnp.pad(order >> 2, ((0, 0), (0, MOE_M)))
    row_list = jnp.pad((order & (TOP_K - 1)) * plane + (order >> 2), ((0, 0), (0, MOE_M)), constant_values=tb)
    nch = (counts + MOE_M - 1) // MOE_M
    first = (jnp.cumsum(nch, axis=1) - nch).astype(I32)
    table_len = -(-(n_list // MOE_M + N_EXPERTS + 2) // SUBLANES) * SUBLANES
    j = jnp.arange(table_len - 1, dtype=I32)[None, :]
    owner = jnp.sum(j[:, :, None] >= (first + nch)[:, None, :], axis=2)
    owner_c = jnp.minimum(owner, N_EXPERTS - 1)
    within = j - jnp.take_along_axis(first, owner_c, axis=1)
    c_base = jnp.where(owner < N_EXPERTS, jnp.take_along_axis(offs, owner_c, axis=1) + within * MOE_M, n_list)
    c_base = jnp.concatenate([jnp.full((n_super, 1), n_list, I32), c_base.astype(I32)], axis=1)

    piece = lambda sb, e, *_: (sb * n_piece + jnp.maximum(e - N_EXPERTS, 0), 0)
    wmap = lambda sb, e, *_: (jnp.minimum(e, N_EXPERTS - 1), 0, 0)
    grid_spec = pltpu.PrefetchScalarGridSpec(
        num_scalar_prefetch=3,
        grid=(n_super, N_EXPERTS + n_piece),
        in_specs=[
            pl.BlockSpec((1, 1, n_list + MOE_M), lambda sb, e, *_: (sb, 0, 0), memory_space=pltpu.SMEM),
            pl.BlockSpec((1, 1, n_list + MOE_M), lambda sb, e, *_: (sb, 0, 0), memory_space=pltpu.SMEM),
            pl.BlockSpec((tb, d // 2), lambda sb, e, *_: (sb, 0), pipeline_mode=pl.Buffered(1)),
            pl.BlockSpec((1, d, 2 * dff), wmap),
            pl.BlockSpec((1, 1, 2 * dff), wmap),
            pl.BlockSpec((1, dff, d), w2map),
            pl.BlockSpec((1, 1, d), wmap),
            pl.BlockSpec((rb, LANES), piece),
            pl.BlockSpec((rb, d), piece),
            pl.BlockSpec((1, 1, d), lambda sb, e, *_: ((sb * tb) // s, 0, 0)),
        ],
        out_specs=pl.BlockSpec((rb, d), piece),
        scratch_shapes=[
            pltpu.VMEM((TOP_K * plane, d // 2), U32),
            pltpu.VMEM((2, MOE_M, d // 2), U32),
            pltpu.VMEM((2, MOE_M, d // 2), U32),
        ],
    )
    out = pl.pallas_call(
        functools.partial(_moe_kernel, tb=tb, rb=rb, table_len=table_len),
        grid_spec=grid_spec,
        out_shape=jax.ShapeDtypeStruct((t, d), F32),
        compiler_params=_cparams(("arbitrary", "arbitrary")),
        name="moe",
    )(first.reshape(-1), nch.astype(I32).reshape(-1), c_base.reshape(-1),
      tok_list.reshape(n_super, 1, n_list + MOE_M), row_list.reshape(n_super, 1, n_list + MOE_M),
      hp.reshape(t, d // 2),
      w1p, b1p, w2b, b2, gate.reshape(t, LANES), xn.reshape(t, d), g2.reshape(bsz, 1, d))
    return out.reshape(bsz, s, d)


MXU_COLS = 256


def _deinterleave_kernel(w_ref, perm_ref, o_ref):
    half = w_ref.shape[2] // 2
    hw = MXU_COLS // 2
    for b in range(w_ref.shape[2] // MXU_COLS):
        blk = w_ref[0, :, b * MXU_COLS:(b + 1) * MXU_COLS].astype(BF16)
        y = jnp.dot(blk, perm_ref[...], preferred_element_type=F32).astype(BF16)
        o_ref[0, :, b * hw:(b + 1) * hw] = y[:, :hw]
        o_ref[0, :, half + b * hw:half + (b + 1) * hw] = y[:, hw:]


def _deinterleave_cast(w1, layer, rows=512):
    depth, n_e, d, two_f = w1.shape
    src = np.concatenate([np.arange(0, MXU_COLS, 2), np.arange(1, MXU_COLS, 2)])
    perm = np.zeros((MXU_COLS, MXU_COLS), np.float32)
    perm[src, np.arange(MXU_COLS)] = 1.0
    return pl.pallas_call(
        _deinterleave_kernel,
        grid=(n_e, d // rows),
        in_specs=[pl.BlockSpec((1, rows, two_f), lambda e, r: (layer * n_e + e, r, 0)),
                  pl.BlockSpec((MXU_COLS, MXU_COLS), lambda e, r: (0, 0))],
        out_specs=pl.BlockSpec((1, rows, two_f), lambda e, r: (e, r, 0)),
        out_shape=jax.ShapeDtypeStruct((n_e, d, two_f), BF16),
        compiler_params=_cparams(("parallel", "parallel")),
        name="w1_deinterleave",
    )(w1.reshape(depth * n_e, d, two_f), jnp.asarray(perm, BF16))


def _deinterleave_bias(b1_l):
    n_e, two_f = b1_l.shape
    return jnp.concatenate([b1_l[:, 0::2], b1_l[:, 1::2]], axis=1).reshape(n_e, 1, two_f)


def kernel(x, c, rel_bias, mod_w, mod_b, norm_mix_w, norm_ffn_w, w_in, conv_w, a_log, dt_bias, gdn_norm_w,
           q_norm_w, k_norm_w, w_out, router_w, router_b, w1, b1, w2, b2):
    depth = mod_w.shape[0]
    bsz, s, d = x.shape
    mod = _modulation(c, mod_w, mod_b)
    tm = min(512, s)
    tb = min(MOE_TB, bsz * s)
    rb = min(MOE_RB, tb)
    n_e, dff = w2.shape[1], w2.shape[2]
    w2b = w2.astype(BF16).reshape(depth * n_e, dff, d)
    for l in range(depth):
        qkv_a, z_a, qb, kb, vb, qi, small = _inproj(
            x, mod[l], norm_mix_w[l], _permute_w_in(w_in[l]), q_norm_w[l], k_norm_w[l], tm)
        y_a = _gdn(qkv_a, z_a, small, conv_w[l], a_log[l], dt_bias[l], gdn_norm_w[l], sb=tm)
        y_b = _dsa(qb, kb, vb, qi, small, rel_bias)
        xn, hp, ridx, gate = _outproj(y_a, y_b, x, mod[l], w_out[l].astype(BF16), norm_ffn_w[l],
                                      router_w[l], router_b[l], tm)
        x = _moe(xn, hp, ridx, gate, mod[l][:, 5], _deinterleave_cast(w1, l), _deinterleave_bias(b1[l]), w2b,
                 b2[l].reshape(n_e, 1, d), tb, rb, layer=l)
    return x
```

```python
import functools
import math

import jax
import jax.numpy as jnp
import numpy as np
from jax import lax
from jax.experimental import pallas as pl
from jax.experimental.pallas import tpu as pltpu

F32 = jnp.float32
BF16 = jnp.bfloat16
I32 = jnp.int32
U32 = jnp.uint32
HIGHEST = lax.Precision.HIGHEST

LANES = 128
SUBLANES = 8
VMEM_LIMIT_BYTES = 56 * 1024 * 1024

CHUNK = 64
HEAD_DIM_A = 128
N_HEADS_A = 4
WIDTH_A = N_HEADS_A * HEAD_DIM_A
CONV_K = 4
HEAD_DIM_B = 64
N_HEADS_B = 8
WIDTH_B = N_HEADS_B * HEAD_DIM_B
IDX_HEADS = 8
IDX_DIM = 64
TOPK_KEYS_MAX = 256
REL_BUCKETS = 32
REL_MAX_DIST = 1024
N_EXPERTS = 32
TOP_K = 4
SWIGLU_ALPHA = 1.702
SWIGLU_LIMIT = 7.0
EPS = 1e-6
NEG_BIG = -1e30

C_QKVA = 0
C_Z = C_QKVA + 3 * WIDTH_A
C_QB = C_Z + WIDTH_A
C_KB = C_QB + WIDTH_B
C_VB = C_KB + WIDTH_B
C_QI = C_VB + WIDTH_B
C_SMALL = C_QI + IDX_HEADS * IDX_DIM
D_IN_PAD = C_SMALL + LANES
S_KIDX = 0
S_B = IDX_DIM
S_A = S_B + N_HEADS_A
S_WIDX = S_A + N_HEADS_A


def _cparams(sem):
    return pltpu.CompilerParams(dimension_semantics=sem, vmem_limit_bytes=VMEM_LIMIT_BYTES)


def _sigmoid(x):
    return jax.nn.sigmoid(x)


def _silu(x):
    return x * _sigmoid(x)


def _softplus(x):
    return jnp.maximum(x, 0.0) + jnp.log(1.0 + jnp.exp(-jnp.abs(x)))


def _mod_kernel(c_ref, w_ref, b_ref, o_ref):
    a = _silu(c_ref[...])
    o_ref[0] = jnp.dot(a, w_ref[0], precision=HIGHEST, preferred_element_type=F32) + b_ref[0]


def _modulation(c, mod_w, mod_b):
    depth, d, n = mod_w.shape
    bsz = c.shape[0]
    rows = -(-bsz // SUBLANES) * SUBLANES
    c_pad = jnp.zeros((rows, d), F32).at[:bsz].set(c)
    tn = 1536
    out = pl.pallas_call(
        _mod_kernel,
        grid=(depth, n // tn),
        in_specs=[
            pl.BlockSpec((rows, d), lambda l, j: (0, 0)),
            pl.BlockSpec((1, d, tn), lambda l, j: (l, 0, j)),
            pl.BlockSpec((1, 1, tn), lambda l, j: (l, 0, j)),
        ],
        out_specs=pl.BlockSpec((1, rows, tn), lambda l, j: (l, 0, j)),
        out_shape=jax.ShapeDtypeStruct((depth, rows, n), F32),
        compiler_params=_cparams(("arbitrary", "arbitrary")),
        name="adaln_mod",
    )(c_pad, mod_w, mod_b.reshape(depth, 1, n))
    return out[:, :bsz].reshape(depth, bsz, 6, d)


def _head_rms(t, group_ref, wn, inv_dim):
    t2 = t * t
    hi = t2.astype(BF16)
    lo = (t2 - hi.astype(F32)).astype(BF16)
    ss = (jnp.dot(hi, group_ref[...], preferred_element_type=F32)
          + jnp.dot(lo, group_ref[...], preferred_element_type=F32))
    return t * lax.rsqrt(ss * inv_dim + EPS) * wn


def _inproj_kernel(x_ref, mod_ref, nw_ref, w_ref, group_ref, qn_ref, kn_ref,
                   qkva_ref, z_ref, qb_ref, kb_ref, vb_ref, qi_ref, small_ref):
    x = x_ref[0]
    ms = jnp.mean(x * x, axis=-1, keepdims=True)
    y = x * lax.rsqrt(ms + EPS) * nw_ref[...]
    h = y * (1.0 + mod_ref[0, 1:2, :]) + mod_ref[0, 0:1, :]
    hb = h.astype(BF16)

    def mm(lo, width):
        return jnp.dot(hb, w_ref[:, lo:lo + width], preferred_element_type=F32)

    qkva_ref[0] = mm(C_QKVA, 3 * WIDTH_A)
    z_ref[0] = mm(C_Z, WIDTH_A)
    q = _head_rms(mm(C_QB, WIDTH_B), group_ref, qn_ref[...], 1.0 / HEAD_DIM_B)
    qb_ref[0] = (q * (HEAD_DIM_B ** -0.5 * LOG2E)).astype(BF16)
    k = _head_rms(mm(C_KB, WIDTH_B), group_ref, kn_ref[...], 1.0 / HEAD_DIM_B)
    kb_ref[0] = k.astype(BF16)
    v = mm(C_VB, WIDTH_B)
    lane = lax.broadcasted_iota(I32, (v.shape[0], LANES), 1)
    tail = (lane == HEAD_DIM_B).astype(F32)
    for p in range(N_HEADS_B // 2):
        pair = v[:, p * LANES:(p + 1) * LANES]
        vb_ref[0, :, (2 * p) * LANES:(2 * p + 1) * LANES] = jnp.where(lane < HEAD_DIM_B, pair, tail).astype(BF16)
        vb_ref[0, :, (2 * p + 1) * LANES:(2 * p + 2) * LANES] = jnp.where(
            lane < HEAD_DIM_B, pltpu.roll(pair, HEAD_DIM_B, axis=1), tail).astype(BF16)
    qi_ref[0] = mm(C_QI, IDX_HEADS * IDX_DIM).astype(BF16)
    small_ref[0] = mm(C_SMALL, LANES)


def _permute_w_in(w_in_l):
    d = w_in_l.shape[0]
    o = 0
    qkva = w_in_l[:, o:o + 3 * WIDTH_A]; o += 3 * WIDTH_A
    z = w_in_l[:, o:o + WIDTH_A]; o += WIDTH_A
    b = w_in_l[:, o:o + N_HEADS_A]; o += N_HEADS_A
    a = w_in_l[:, o:o + N_HEADS_A]; o += N_HEADS_A
    qkvb = w_in_l[:, o:o + 3 * WIDTH_B]; o += 3 * WIDTH_B
    qi = w_in_l[:, o:o + IDX_HEADS * IDX_DIM]; o += IDX_HEADS * IDX_DIM
    ki = w_in_l[:, o:o + IDX_DIM]; o += IDX_DIM
    wi = w_in_l[:, o:o + IDX_HEADS]; o += IDX_HEADS
    pad = jnp.zeros((d, LANES - IDX_DIM - 2 * N_HEADS_A - IDX_HEADS), w_in_l.dtype)
    return jnp.concatenate([qkva, z, qkvb, qi, ki, b, a, wi, pad], axis=1).astype(BF16)


def _group_ones(width, group):
    g = np.arange(width) // group
    return jnp.asarray((g[:, None] == g[None, :]).astype(np.float32), dtype=BF16)


def _inproj(x, mod_l, norm_w, w_perm, q_norm_w, k_norm_w, tm):
    bsz, s, d = x.shape
    f = lambda b, i: (b, i, 0)
    const2 = lambda b, i: (0, 0)
    outs = [
        (3 * WIDTH_A, F32), (WIDTH_A, F32), (WIDTH_B, BF16), (WIDTH_B, BF16), (N_HEADS_B * LANES, BF16),
        (IDX_HEADS * IDX_DIM, BF16), (LANES, F32),
    ]
    return pl.pallas_call(
        _inproj_kernel,
        grid=(bsz, s // tm),
        in_specs=[
            pl.BlockSpec((1, tm, d), f),
            pl.BlockSpec((1, 6, d), lambda b, i: (b, 0, 0)),
            pl.BlockSpec((1, d), const2),
            pl.BlockSpec((d, D_IN_PAD), const2),
            pl.BlockSpec((WIDTH_B, WIDTH_B), const2),
            pl.BlockSpec((1, WIDTH_B), const2),
            pl.BlockSpec((1, WIDTH_B), const2),
        ],
        out_specs=[pl.BlockSpec((1, tm, w), f) for w, _ in outs],
        out_shape=[jax.ShapeDtypeStruct((bsz, s, w), dt) for w, dt in outs],
        compiler_params=_cparams(("parallel", "parallel")),
        name="inproj",
    )(x, mod_l, norm_w.reshape(1, d), w_perm, _group_ones(WIDTH_B, HEAD_DIM_B),
      jnp.tile(q_norm_w, N_HEADS_B).reshape(1, WIDTH_B), jnp.tile(k_norm_w, N_HEADS_B).reshape(1, WIDTH_B))


def _dot_nt(a, b, precision=None):
    return lax.dot_general(a, b, (((1,), (1,)), ((), ())), precision=precision, preferred_element_type=F32)


def _split2(x):
    hi = x.astype(BF16)
    return hi, (x - hi.astype(F32)).astype(BF16)


def _split3(x):
    hi = x.astype(BF16)
    r = x - hi.astype(F32)
    mid = r.astype(BF16)
    return hi, mid, (r - mid.astype(F32)).astype(BF16)


def _mm3(a, b):
    return (jnp.dot(a[0], b[0], preferred_element_type=F32) + jnp.dot(a[0], b[1], preferred_element_type=F32)
            + jnp.dot(a[1], b[0], preferred_element_type=F32))


GDN_PAR = 8


def _gdn_kernel(qkv_ref, z_ref, small_ref, convw_ref, alog_ref, dtb_ref, nw_ref,
                y_ref, xe_ref, u_ref, state_ref, uval_ref, wdec_ref, qg_ref, kdec_ref, attn_ref, egl_ref, *, sb):
    n_chunks = sb // CHUNK
    halo = SUBLANES

    @pl.when(pl.program_id(1) == 0)
    def _():
        xe_ref[0:halo, :] = jnp.zeros((halo, 3 * WIDTH_A), F32)
        state_ref[...] = jnp.zeros_like(state_ref)

    xe_ref[halo:halo + sb, :] = qkv_ref[0]

    rows = 128
    for g in range(3 * WIDTH_A // LANES):
        cs = slice(g * LANES, (g + 1) * LANES)
        for r in range(sb // rows):
            base = halo - (CONV_K - 1) + r * rows
            acc = xe_ref[base:base + rows, cs] * convw_ref[0:1, cs]
            for j in range(1, CONV_K):
                acc = acc + xe_ref[base + j:base + j + rows, cs] * convw_ref[j:j + 1, cs]
            u_ref[r * rows:(r + 1) * rows, cs] = _silu(acc)

    xe_ref[0:halo, :] = xe_ref[sb:sb + halo, :]

    wide = N_HEADS_A * CHUNK
    heads = range(N_HEADS_A)
    ii = lax.broadcasted_iota(I32, (CHUNK, wide), 0)
    jj = lax.broadcasted_iota(I32, (CHUNK, wide), 1) & (CHUNK - 1)
    eye_w = (ii == jj).astype(F32)
    tri = (lax.broadcasted_iota(I32, (CHUNK, CHUNK), 0)
           >= lax.broadcasted_iota(I32, (CHUNK, CHUNK), 1)).astype(F32).astype(BF16)
    shift = int(math.log2(CHUNK))
    bd_mask = ((lax.broadcasted_iota(I32, (wide, wide), 0) >> shift)
               == (lax.broadcasted_iota(I32, (wide, wide), 1) >> shift)).astype(F32)

    bd_mask = bd_mask.astype(BF16)

    def block_diag(parts):
        return tuple(jnp.concatenate([m] * N_HEADS_A, axis=0) * bd_mask for m in parts)

    def prepare(c):
        rs = pl.ds(pl.multiple_of(c * CHUNK, CHUNK), CHUNK)
        qn, kn, v, beta, g_b = [], [], [], [], []
        for h in heads:
            q = u_ref[rs, h * HEAD_DIM_A:(h + 1) * HEAD_DIM_A]
            k = u_ref[rs, WIDTH_A + h * HEAD_DIM_A:WIDTH_A + (h + 1) * HEAD_DIM_A]
            v.append(u_ref[rs, 2 * WIDTH_A + h * HEAD_DIM_A:2 * WIDTH_A + (h + 1) * HEAD_DIM_A])
            qn.append(q * (lax.rsqrt(jnp.sum(q * q, axis=-1, keepdims=True) + EPS) * (HEAD_DIM_A ** -0.5)))
            kn.append(k * lax.rsqrt(jnp.sum(k * k, axis=-1, keepdims=True) + EPS))
            beta.append(_sigmoid(small_ref[0, rs, S_B + h:S_B + h + 1]))
            g = -jnp.exp(alog_ref[0:1, h:h + 1]) * _softplus(small_ref[0, rs, S_A + h:S_A + h + 1]
                                                            + dtb_ref[0:1, h:h + 1])
            g_b.append(jnp.broadcast_to(g, (CHUNK, CHUNK)))
        gc_w = sum(jnp.dot(tri, part, preferred_element_type=F32) for part in _split3(jnp.concatenate(g_b, axis=1)))
        gc_row = jnp.sum(jnp.where(ii == jj, gc_w, 0.0), axis=0, keepdims=True)
        decay_w = jnp.exp(jnp.where(ii >= jj, gc_w - gc_row, NEG_BIG))
        k_beta = [kn[h] * beta[h] for h in heads]
        kk_w = jnp.concatenate([_dot_nt(k_beta[h].astype(BF16), kn[h].astype(BF16)) for h in heads], axis=1)
        a_w = -jnp.where(ii > jj, kk_w * decay_w, 0.0)
        gc = [gc_w[:, h * CHUNK:h * CHUNK + 1] for h in heads]
        egc = [jnp.exp(gc[h]) for h in heads]
        qk = [_dot_nt(qn[h].astype(BF16), kn[h].astype(BF16)) for h in heads]
        for h in heads:
            g_last = gc[h][CHUNK - 1:CHUNK, :]
            qg_ref[c, h] = (qn[h] * egc[h]).astype(BF16)
            kdec_ref[c, h] = kn[h] * jnp.exp(g_last - gc[h])
            attn_ref[c, h] = (qk[h] * decay_w[:, h * CHUNK:(h + 1) * CHUNK]).astype(BF16)
            egl_ref[c, h] = jnp.broadcast_to(jnp.exp(g_last), (SUBLANES, LANES))
        return a_w, [_split2(v[h] * beta[h]) for h in heads], [_split2(k_beta[h] * egc[h]) for h in heads]

    def solve_body(cg, carry):
        group = range(GDN_PAR)
        chunks = [cg * GDN_PAR + i for i in group]
        pre = [prepare(c) for c in chunks]
        t_w = [eye_w + pre[i][0] for i in group]
        p_parts = [_split2(pre[i][0]) for i in group]
        bd = [block_diag(p_parts[i]) for i in group]
        for _ in range(shift - 1):
            prod = [_mm3(p_parts[i], bd[i]) for i in group]
            p_parts = [_split2(prod[i]) for i in group]
            bd = [block_diag(p_parts[i]) for i in group]
            upd = [_mm3(_split2(t_w[i]), bd[i]) for i in group]
            t_w = [t_w[i] + upd[i] for i in group]
        for i in group:
            t_h = [_split2(t_w[i][:, h * CHUNK:(h + 1) * CHUNK]) for h in heads]
            u_val = [_mm3(t_h[h], pre[i][1][h]) for h in heads]
            w_dec = [_mm3(t_h[h], pre[i][2][h]) for h in heads]
            for h in heads:
                uval_ref[chunks[i], h] = u_val[h]
                wdec_ref[chunks[i], h] = w_dec[h].astype(BF16)
        return carry

    lax.fori_loop(0, n_chunks // GDN_PAR, solve_body, 0)

    def scan_body(c, carry):
        rs = pl.ds(pl.multiple_of(c * CHUNK, CHUNK), CHUNK)
        state = [state_ref[h] for h in heads]
        state_b = [s_h.astype(BF16) for s_h in state]
        w_s = [jnp.dot(wdec_ref[c, h], state_b[h], preferred_element_type=F32) for h in heads]
        v_new = [(uval_ref[c, h] - w_s[h]).astype(BF16) for h in heads]
        o = [jnp.dot(qg_ref[c, h], state_b[h], preferred_element_type=F32)
             + jnp.dot(attn_ref[c, h], v_new[h], preferred_element_type=F32) for h in heads]
        for h in heads:
            state_ref[h] = (state[h] * egl_ref[c, h][0:1, 0:1]
                            + jnp.dot(kdec_ref[c, h].T.astype(BF16), v_new[h], preferred_element_type=F32))
        for h in heads:
            hs = slice(h * HEAD_DIM_A, (h + 1) * HEAD_DIM_A)
            on = o[h] * lax.rsqrt(jnp.mean(o[h] * o[h], axis=-1, keepdims=True) + EPS) * nw_ref[...]
            y_ref[0, rs, hs] = (on * _silu(z_ref[0, rs, hs])).astype(BF16)
        return carry

    lax.fori_loop(0, n_chunks, scan_body, 0)


def _gdn(qkv_a, z_a, small, conv_w, a_log, dt_bias, norm_w, sb):
    bsz, s, _ = qkv_a.shape
    n_c = sb // CHUNK
    assert n_c % GDN_PAR == 0 and s % sb == 0
    per_head = (n_c, N_HEADS_A, CHUNK, HEAD_DIM_A)
    f = lambda b, i: (b, i, 0)
    const2 = lambda b, i: (0, 0)
    return pl.pallas_call(
        functools.partial(_gdn_kernel, sb=sb),
        grid=(bsz, s // sb),
        in_specs=[
            pl.BlockSpec((1, sb, 3 * WIDTH_A), f),
            pl.BlockSpec((1, sb, WIDTH_A), f),
            pl.BlockSpec((1, sb, LANES), f),
            pl.BlockSpec((CONV_K, 3 * WIDTH_A), const2),
            pl.BlockSpec((1, N_HEADS_A), const2),
            pl.BlockSpec((1, N_HEADS_A), const2),
            pl.BlockSpec((1, HEAD_DIM_A), const2),
        ],
        out_specs=pl.BlockSpec((1, sb, WIDTH_A), f),
        out_shape=jax.ShapeDtypeStruct((bsz, s, WIDTH_A), BF16),
        scratch_shapes=[
            pltpu.VMEM((sb + SUBLANES, 3 * WIDTH_A), F32),
            pltpu.VMEM((sb, 3 * WIDTH_A), F32),
            pltpu.VMEM((N_HEADS_A, HEAD_DIM_A, HEAD_DIM_A), F32),
            pltpu.VMEM(per_head, F32),
            pltpu.VMEM(per_head, BF16),
            pltpu.VMEM(per_head, BF16),
            pltpu.VMEM(per_head, F32),
            pltpu.VMEM((n_c, N_HEADS_A, CHUNK, CHUNK), BF16),
            pltpu.VMEM((n_c, N_HEADS_A, SUBLANES, LANES), F32),
        ],
        compiler_params=_cparams(("parallel", "arbitrary")),
        name="gdn",
    )(qkv_a, z_a, small, conv_w, a_log.reshape(1, -1), dt_bias.reshape(1, -1), norm_w.reshape(1, -1))


QB = 128
FAR_T = 512
FAR_G = FAR_T // LANES
ATT_W = 256
LOG2E = 1.4426950408889634
NEAR_D = 9
NEAR_MIN = 5
INT_MIN = -2 ** 31


def _t5_bucket_np(rel):
    nb = REL_BUCKETS // 2
    max_exact = nb // 2
    side = np.where(rel > 0, nb, 0)
    n = np.abs(rel)
    nf = np.maximum(n, 1).astype(np.float32)
    large = max_exact + (np.log(nf / np.float32(max_exact)) / np.float32(math.log(REL_MAX_DIST / max_exact))
                         * np.float32(nb - max_exact)).astype(np.int32)
    large = np.minimum(large, nb - 1)
    return (side + np.where(n < max_exact, n, large)).astype(np.int32)


def _near_bucket_table():
    r = np.arange(QB)[:, None]
    c = np.arange(LANES)[None, :]
    return np.stack([_t5_bucket_np(c - r - LANES * d) for d in range(NEAR_D)])


FAR_BUCKET = int(_t5_bucket_np(np.array([-(NEAR_MIN * LANES + 1)]))[0])
assert all(int(b) == FAR_BUCKET for b in _t5_bucket_np(-np.arange((NEAR_MIN + 1) * LANES - (QB - 1), 1 << 20, 997)))


def _sortable_key(score):
    bits = pltpu.bitcast(score + 0.0, I32)
    return bits ^ ((bits >> 31) & 0x7FFFFFFF)


PLANE_G = 32


def _bit_transpose32(words):
    a = list(words)
    mask, j = 0x0000FFFF, 16
    while j:
        k = 0
        while k < 32:
            t = (a[k] ^ lax.shift_right_logical(a[k + j], jnp.int32(j))) & mask
            a[k] = a[k] ^ t
            a[k + j] = a[k + j] ^ lax.shift_left(t, jnp.int32(j))
            k = (k + j + 1) & ~j
        j >>= 1
        mask = (mask ^ (mask << j)) & 0xFFFFFFFF
    return a


def _dsa_kernel(rb_ref, qb_ref, qi_ref, small_ref, kb_ref, va_ref, kidx2_ref, tab_ref,
                y_ref, qis_ref, qs_ref, wb_ref, keys_ref, nbias_ref, thr_ref, jlim_ref,
                m_ref, acc_ref, s_ref, p_ref, peak_ref, planes_ref, need_ref, excess_ref, *, seq, k_sel):
    i = pl.program_id(1)
    lane = lax.broadcasted_iota(I32, (QB, LANES), 1)
    row = lax.broadcasted_iota(I32, (QB, LANES), 0)
    even_f = (lane < HEAD_DIM_B).astype(F32)
    even_b = even_f.astype(BF16)
    odd_b = (1.0 - even_f).astype(BF16)

    @pl.when(i == 0)
    def _():
        nbias_ref[...] = jnp.zeros_like(nbias_ref)
        keys_ref[...] = jnp.full(keys_ref.shape, INT_MIN, I32)

        def d_body(d, c0):
            tab = tab_ref[d]

            def b_body(bk, c1):
                hit = tab == bk
                for h in range(N_HEADS_B):
                    nbias_ref[d * N_HEADS_B + h] = jnp.where(hit, rb_ref[bk, h] * LOG2E,
                                                             nbias_ref[d * N_HEADS_B + h])
                return c1

            return lax.fori_loop(0, REL_BUCKETS, b_body, c0)

        lax.fori_loop(0, NEAR_D, d_body, 0)

    for p in range(N_HEADS_B // 2):
        ps = slice(p * LANES, (p + 1) * LANES)
        qi_pair = qi_ref[0, :, ps]
        qis_ref[(2 * p) * QB:(2 * p + 1) * QB, :] = qi_pair * even_b
        qis_ref[(2 * p + 1) * QB:(2 * p + 2) * QB, :] = qi_pair * odd_b
        q_pair = qb_ref[0, :, ps]
        qs_ref[(2 * p) * QB:(2 * p + 1) * QB, :] = q_pair * even_b
        qs_ref[(2 * p + 1) * QB:(2 * p + 2) * QB, :] = q_pair * odd_b
    w_scale = IDX_HEADS ** -0.5 * IDX_DIM ** -0.5
    for h in range(IDX_HEADS):
        wb_ref[h] = jnp.broadcast_to(small_ref[0, :, S_WIDX + h:S_WIDX + h + 1] * w_scale, (QB, LANES))

    limit = i * QB + CHUNK + jnp.where(row >= CHUNK, CHUNK, 0)

    def score_body(t, c0):
        for c in range(FAR_T // ATT_W):
            k0 = t * FAR_T + c * ATT_W
            k_part = kidx2_ref[0, pl.ds(pl.multiple_of(k0, ATT_W), ATT_W), :]
            acc = None
            for h in range(IDX_HEADS):
                dots = _dot_nt(qis_ref[h * QB:(h + 1) * QB, :], k_part)
                term = jnp.maximum(dots, 0.0) * jnp.concatenate([wb_ref[h]] * (ATT_W // LANES), axis=1)
                acc = term if acc is None else acc + term
            for gg in range(ATT_W // LANES):
                col = k0 + gg * LANES + lane
                keys_ref[(k0 // LANES) + gg] = jnp.where(
                    col < limit, _sortable_key(acc[:, gg * LANES:(gg + 1) * LANES]), INT_MIN)
        return c0

    n_groups = i + 1
    n_tiles = i // FAR_G + 1
    lax.fori_loop(0, n_tiles, score_body, 0)

    @pl.when(n_tiles % 2 == 1)
    def _():
        for g in range(FAR_G):
            keys_ref[n_tiles * FAR_G + g] = jnp.full((QB, LANES), INT_MIN, I32)

    def count(pred):
        def t_body(t, acc):
            for g in range(2 * FAR_G):
                grp = t * (2 * FAR_G) + g
                acc = acc + jnp.where(pred(keys_ref[grp], grp * LANES + lane), 1, 0)
            return acc

        acc = lax.fori_loop(0, (n_tiles + 1) // 2, t_body, jnp.zeros((QB, LANES), I32))
        return jnp.broadcast_to(jnp.sum(acc, axis=1, keepdims=True), (QB, LANES))

    thr_ref[...] = jnp.full((QB, LANES), INT_MIN, I32)
    jlim_ref[...] = jnp.full((QB, LANES), -1, I32)

    def lane_total(x):
        return jnp.broadcast_to(jnp.sum(x, axis=1, keepdims=True), (QB, LANES))

    @pl.when(n_groups * QB > k_sel)
    def _():
        def transpose_half(half):
            def row_body(rr, c0):
                rows = pl.ds(pl.multiple_of(rr * SUBLANES, SUBLANES), SUBLANES)
                planes = _bit_transpose32([keys_ref[half * PLANE_G + g, rows, :] ^ INT_MIN for g in range(PLANE_G)])
                for b in range(32):
                    planes_ref[half, b, rows, :] = planes[b]
                return c0

            lax.fori_loop(0, QB // SUBLANES, row_body, 0)

        def search(n_half):
            halves = range(n_half)
            row_groups = (pl.ds(0, QB // 2), pl.ds(QB // 2, QB // 2))
            shape = (QB // 2, LANES)

            def total(x):
                return jnp.broadcast_to(jnp.sum(x, axis=1, keepdims=True), shape)

            def digit(step, rows, state):
                r, eq, above = state
                cls = []
                for half in halves:
                    hi = eq[half] & planes_ref[half, 2 * step, rows, :]
                    lo_plane = planes_ref[half, 2 * step + 1, rows, :]
                    zero_hi = eq[half] ^ hi
                    e11 = hi & lo_plane
                    e01 = zero_hi & lo_plane
                    cls.append((e11, hi ^ e11, e01, zero_hi ^ e01))
                n11, n10, n01 = [sum(lax.population_count(cls[h][d]) for h in halves) for d in range(3)]
                c3 = above + n11
                c2 = c3 + n10
                c1 = c2 + n01
                d3, d2, d1 = [total(c) >= k_sel for c in (c3, c2, c1)]
                eq = tuple(jnp.where(d3, cls[h][0], jnp.where(d2, cls[h][1], jnp.where(d1, cls[h][2], cls[h][3])))
                           for h in halves)
                above = jnp.where(d3, above, jnp.where(d2, c3, jnp.where(d1, c2, c1)))
                value = jnp.where(d3, 3, jnp.where(d2, 2, jnp.where(d1, 1, 0)))
                return r | lax.shift_left(value, 30 - 2 * step), eq, above

            def digit_body(step, carry):
                return tuple(digit(step, rows, state) for rows, state in zip(row_groups, carry))

            zero = jnp.zeros(shape, I32)
            full = jnp.full(shape, -1, I32)
            start = (zero, (full,) * n_half, zero)
            for rows, (r, eq, above) in zip(row_groups, lax.fori_loop(0, 16, digit_body, (start, start))):
                thr_ref[rows, :] = r ^ INT_MIN
                need = k_sel - total(above)
                need_ref[rows, :] = need
                excess_ref[rows, :] = total(sum(lax.population_count(eq[h]) for h in halves)) - need

        transpose_half(0)

        @pl.when(n_groups > PLANE_G)
        def _():
            transpose_half(1)
            search(2)

        @pl.when(n_groups <= PLANE_G)
        def _():
            search(1)

        r = thr_ref[...]
        need = need_ref[...]
        excess = excess_ref[...]
        jlim_ref[...] = jnp.where(r == INT_MIN, -1, seq)

        @pl.when(jnp.max(excess) > 0)
        def _():
            def j_body(step, jl):
                cand = jl + lax.shift_left(jnp.int32(1), (seq.bit_length() - 1) - step)
                cnt = count(lambda kt, col: (kt == r) & (col < cand))
                return jnp.where(cnt < need, cand, jl)

            jl = lax.fori_loop(0, seq.bit_length(), j_body, jnp.zeros((QB, LANES), I32))
            jlim_ref[...] = jnp.where(r == INT_MIN, -1, jl)

    m_ref[...] = jnp.full(m_ref.shape, NEG_BIG, F32)
    acc_ref[...] = jnp.zeros_like(acc_ref)

    part_g = ATT_W // LANES

    def tile_mask(g0, groups):
        negm = []
        for g in range(groups):
            kt = keys_ref[g0 + g]
            sel = (kt > thr_ref[...]) | ((kt == thr_ref[...]) & ((g0 + g) * LANES + lane <= jlim_ref[...]))
            negm.append(jnp.where(sel, 0.0, -jnp.inf))
        return negm

    def logits_pair(g0, groups, p, negm, near):
        heads = ((2 * p, slice(0, QB)), (2 * p + 1, slice(QB, 2 * QB)))
        peak = [None, None]
        for c in range(groups // part_g):
            ks = pl.ds(pl.multiple_of((g0 + c * part_g) * LANES, ATT_W), ATT_W)
            s = _dot_nt(qs_ref[2 * p * QB:(2 * p + 2) * QB, :], kb_ref[0, ks, p * LANES:(p + 1) * LANES])
            for gg in range(part_g):
                g = c * part_g + gg
                for n, (h, rows) in enumerate(heads):
                    extra = negm[g]
                    if near:
                        extra = extra + nbias_ref[jnp.clip(i - (g0 + g), 0, NEAR_D - 1) * N_HEADS_B + h]
                    v = s[rows, gg * LANES:(gg + 1) * LANES] + extra
                    s_ref[0, h, g] = v
                    peak[n] = v if peak[n] is None else jnp.maximum(peak[n], v)
        for n, (h, _) in enumerate(heads):
            peak_ref[0, h] = peak[n]

    def softmax_head(h, groups, bias_scalar):
        m_old = m_ref[h]
        m_new = jnp.maximum(m_old, jnp.max(peak_ref[0, h], axis=1, keepdims=True) + bias_scalar)
        shift = jnp.concatenate([m_new - bias_scalar] * part_g, axis=1)
        for c in range(groups // part_g):
            part = jnp.concatenate([s_ref[0, h, c * part_g + g] for g in range(part_g)], axis=1)
            p_ref[0, h, :, c * ATT_W:(c + 1) * ATT_W] = jnp.exp2(part - shift).astype(BF16)
        acc_ref[h] = jnp.exp2(m_old - m_new) * acc_ref[h]
        m_ref[h] = m_new

    def pv_head(g0, groups, h):
        ks = pl.ds(pl.multiple_of(g0 * LANES, FAR_T), groups * LANES)
        acc_ref[h] += jnp.dot(p_ref[0, h, :, :groups * LANES], va_ref[0, ks, h * LANES:(h + 1) * LANES],
                              preferred_element_type=F32)

    def tile_body(groups, near):
        def body(t, c0):
            g0 = t * groups
            negm = tile_mask(g0, groups)
            n_pair = N_HEADS_B // 2
            for step in range(n_pair + 2):
                if step < n_pair:
                    logits_pair(g0, groups, step, negm, near)
                if 1 <= step <= n_pair:
                    for h in (2 * step - 2, 2 * step - 1):
                        softmax_head(h, groups, 0.0 if near else rb_ref[FAR_BUCKET, h] * LOG2E)
                if step >= 2:
                    for h in (2 * step - 4, 2 * step - 3):
                        pv_head(g0, groups, h)
            return c0

        return body

    far_tiles = jnp.maximum(i - NEAR_MIN, 0) // FAR_G
    lax.fori_loop(0, far_tiles, tile_body(FAR_G, False), 0)
    lax.fori_loop(far_tiles, n_tiles, tile_body(FAR_G, True), 0)

    def head_out(h):
        a = acc_ref[h]
        return a * (1.0 / a[:, HEAD_DIM_B:HEAD_DIM_B + 1])

    for p in range(N_HEADS_B // 2):
        o_odd = pltpu.roll(head_out(2 * p + 1), HEAD_DIM_B, axis=1)
        y_ref[0, :, p * LANES:(p + 1) * LANES] = jnp.where(lane < HEAD_DIM_B, head_out(2 * p), o_odd).astype(BF16)


def _dsa(qb, kb, va, qi, small, rel_bias):
    bsz, s, _ = qb.shape
    assert s % FAR_T == 0 and s // LANES <= 2 * PLANE_G
    k_sel = min(TOPK_KEYS_MAX, s // 4)
    kidx = small[:, :, S_KIDX:S_KIDX + IDX_DIM].astype(BF16)
    kidx2 = jnp.concatenate([kidx, kidx], axis=-1)
    tab = jnp.asarray(_near_bucket_table())
    blk = lambda b, i: (b, i, 0)
    full = lambda b, i: (b, 0, 0)
    one = pl.Buffered(1)
    return pl.pallas_call(
        functools.partial(_dsa_kernel, seq=s, k_sel=k_sel),
        grid=(bsz, s // QB),
        in_specs=[
            pl.BlockSpec(memory_space=pltpu.SMEM),
            pl.BlockSpec((1, QB, WIDTH_B), blk),
            pl.BlockSpec((1, QB, IDX_HEADS * IDX_DIM), blk),
            pl.BlockSpec((1, QB, LANES), blk),
            pl.BlockSpec((1, s, WIDTH_B), full, pipeline_mode=one),
            pl.BlockSpec((1, s, N_HEADS_B * LANES), full, pipeline_mode=one),
            pl.BlockSpec((1, s, LANES), full, pipeline_mode=one),
            pl.BlockSpec((NEAR_D, QB, LANES), lambda b, i: (0, 0, 0), pipeline_mode=one),
        ],
        out_specs=pl.BlockSpec((1, QB, WIDTH_B), blk),
        out_shape=jax.ShapeDtypeStruct((bsz, s, WIDTH_B), BF16),
        scratch_shapes=[
            pltpu.VMEM((IDX_HEADS * QB, LANES), BF16),
            pltpu.VMEM((N_HEADS_B * QB, LANES), BF16),
            pltpu.VMEM((IDX_HEADS, QB, LANES), F32),
            pltpu.VMEM((2 * PLANE_G, QB, LANES), I32),
            pltpu.VMEM((NEAR_D * N_HEADS_B, QB, LANES), F32),
            pltpu.VMEM((QB, LANES), I32),
            pltpu.VMEM((QB, LANES), I32),
            pltpu.VMEM((N_HEADS_B, QB, LANES), F32),
            pltpu.VMEM((N_HEADS_B, QB, LANES), F32),
            pltpu.VMEM((1, N_HEADS_B, FAR_G, QB, LANES), F32),
            pltpu.VMEM((1, N_HEADS_B, QB, FAR_T), BF16),
            pltpu.VMEM((1, N_HEADS_B, QB, LANES), F32),
            pltpu.VMEM((2, 32, QB, LANES), I32),
            pltpu.VMEM((QB, LANES), I32),
            pltpu.VMEM((QB, LANES), I32),
        ],
        compiler_params=_cparams(("parallel", "arbitrary")),
        name="dsa",
    )(rel_bias, qb, qi, small, kb, va, kidx2, tab)


HALF_MASK = 0xFFFF0000


def _pack_halves(t):
    w = t.shape[1] // 2
    bits = pltpu.bitcast(t.astype(BF16).astype(F32), U32)
    return (bits[:, :w] >> 16) | (bits[:, w:] & jnp.uint32(HALF_MASK))


def _unpack_halves(p):
    lo = pltpu.bitcast(p << 16, F32)
    hi = pltpu.bitcast(p & jnp.uint32(HALF_MASK), F32)
    return jnp.concatenate([lo, hi], axis=1)


def _outproj_kernel(ya_ref, yb_ref, x_ref, mod_ref, wo_ref, nw_ref, rw_ref, rbias_ref,
                    xn_ref, hp_ref, ridx_ref, gate_ref):
    wa = ya_ref.shape[2]
    y = (jnp.dot(ya_ref[0], wo_ref[0:wa, :], preferred_element_type=F32)
         + jnp.dot(yb_ref[0], wo_ref[wa:, :], preferred_element_type=F32))
    xn = x_ref[0] + mod_ref[0, 2:3, :] * y
    xn_ref[0] = xn
    ms = jnp.mean(xn * xn, axis=-1, keepdims=True)
    h = xn * lax.rsqrt(ms + EPS) * nw_ref[...] * (1.0 + mod_ref[0, 4:5, :]) + mod_ref[0, 3:4, :]
    hp_ref[0] = _pack_halves(h)

    logits = jnp.dot(h, rw_ref[...], precision=HIGHEST, preferred_element_type=F32) + rbias_ref[...]
    lane = lax.broadcasted_iota(I32, logits.shape, 1)
    cur = logits
    vals, ridx = [], jnp.zeros(logits.shape, I32)
    for k in range(TOP_K):
        mx = jnp.max(cur, axis=1, keepdims=True)
        am = jnp.min(jnp.where(cur == mx, lane, LANES), axis=1, keepdims=True)
        cur = jnp.where(lane == am, -jnp.inf, cur)
        vals.append(mx)
        ridx = jnp.where(lane == k, am, ridx)
    ex = [jnp.exp(v - vals[0]) for v in vals]
    inv = 1.0 / (ex[0] + ex[1] + ex[2] + ex[3])
    gate = jnp.zeros(logits.shape, F32)
    for k in range(TOP_K):
        gate = jnp.where(lane == k, ex[k] * inv, gate)
    ridx_ref[0] = ridx
    gate_ref[0] = gate


def _outproj(y_a, y_b, x, mod_l, w_out_bf, norm_w, router_w, router_b, tm):
    bsz, s, d = x.shape
    n_e = router_w.shape[1]
    rw = jnp.zeros((d, LANES), F32).at[:, :n_e].set(router_w)
    rbias = jnp.full((1, LANES), NEG_BIG, F32).at[0, :n_e].set(router_b)
    blk = lambda b, i: (b, i, 0)
    const2 = lambda b, i: (0, 0)
    return pl.pallas_call(
        _outproj_kernel,
        grid=(bsz, s // tm),
        in_specs=[
            pl.BlockSpec((1, tm, y_a.shape[2]), blk),
            pl.BlockSpec((1, tm, y_b.shape[2]), blk),
            pl.BlockSpec((1, tm, d), blk),
            pl.BlockSpec((1, 6, d), lambda b, i: (b, 0, 0)),
            pl.BlockSpec((d, d), const2),
            pl.BlockSpec((1, d), const2),
            pl.BlockSpec((d, LANES), const2),
            pl.BlockSpec((1, LANES), const2),
        ],
        out_specs=[pl.BlockSpec((1, tm, d), blk), pl.BlockSpec((1, tm, d // 2), blk),
                   pl.BlockSpec((1, tm, LANES), blk), pl.BlockSpec((1, tm, LANES), blk)],
        out_shape=[jax.ShapeDtypeStruct((bsz, s, d), F32), jax.ShapeDtypeStruct((bsz, s, d // 2), U32),
                   jax.ShapeDtypeStruct((bsz, s, LANES), I32), jax.ShapeDtypeStruct((bsz, s, LANES), F32)],
        compiler_params=_cparams(("parallel", "parallel")),
        name="outproj_router",
    )(y_a, y_b, x, mod_l, w_out_bf, norm_w.reshape(1, d), rw, rbias)


MOE_TB = 2048
MOE_RB = 512
MOE_M = 144


def _moe_kernel(first_ref, nch_ref, cbase_ref, tok_ref, row_ref, hp_ref, w1_ref, b1_ref, w2_ref, b2_ref,
                gate_ref, x_ref, g2_ref, o_ref, slots_ref, xg_ref, yb_ref, *, tb, rb, table_len):
    sb = pl.program_id(0)
    e = pl.program_id(1)
    dff = w2_ref.shape[1]
    plane = tb + SUBLANES
    table = sb * table_len + 1

    def gather(j):
        base = cbase_ref[table + j]
        buf = (j + 2) % 2
        for r in range(MOE_M):
            xg_ref[buf, pl.ds(r, 1), :] = hp_ref[pl.ds(tok_ref[0, 0, base + r], 1), :]

    def scatter(j):
        base = cbase_ref[table + j]
        buf = (j + 2) % 2
        for r in range(MOE_M):
            slots_ref[pl.ds(row_ref[0, 0, base + r], 1), :] = yb_ref[buf, pl.ds(r, 1), :]

    @pl.when(e == 0)
    def _():
        yb_ref[...] = jnp.zeros_like(yb_ref)
        gather(0)

    @pl.when(e < N_EXPERTS)
    def _():
        j0 = first_ref[sb * N_EXPERTS + e]

        def chunk(j, carry):
            buf = j % 2
            xb = _unpack_halves(xg_ref[buf]).astype(BF16)
            gather(j + 1)
            scatter(j - 1)
            u = jnp.dot(xb, w1_ref[0], preferred_element_type=F32) + b1_ref[0]
            glu = jnp.minimum(u[:, :dff], SWIGLU_LIMIT)
            lin = jnp.clip(u[:, dff:], -SWIGLU_LIMIT, SWIGLU_LIMIT)
            act = glu * _sigmoid(SWIGLU_ALPHA * glu) * (lin + 1.0)
            y = jnp.dot(act.astype(BF16), w2_ref[0], preferred_element_type=F32) + b2_ref[0]
            yb_ref[buf] = _pack_halves(y)
            return carry

        lax.fori_loop(j0, j0 + nch_ref[sb * N_EXPERTS + e], chunk, 0)

    @pl.when(e == N_EXPERTS)
    def _():
        n_total = first_ref[sb * N_EXPERTS + N_EXPERTS - 1] + nch_ref[sb * N_EXPERTS + N_EXPERTS - 1]
        scatter(n_total - 1)

    @pl.when(e >= N_EXPERTS)
    def _():
        r0 = pl.multiple_of((e - N_EXPERTS) * rb, rb)
        acc = gate_ref[:, 0:1] * _unpack_halves(slots_ref[pl.ds(r0, rb), :])
        for k in range(1, TOP_K):
            rows = pl.ds(pl.multiple_of(k * plane + r0, SUBLANES), rb)
            acc = acc + gate_ref[:, k:k + 1] * _unpack_halves(slots_ref[rows, :])
        o_ref[...] = x_ref[...] + g2_ref[0] * acc


def _moe(xn, hp, ridx, gate, g2, w1p, b1p, w2b, b2, tb, rb, layer=0):
    bsz, s, d = xn.shape
    t = bsz * s
    n_super = t // tb
    n_piece = tb // rb
    dff = w2b.shape[1]
    w2map = lambda sb, e, *_: (layer * N_EXPERTS + jnp.minimum(e, N_EXPERTS - 1), 0, 0)
    flat_e = ridx[:, :, :TOP_K].reshape(n_super, tb * TOP_K)
    order = jnp.argsort(flat_e, axis=1, stable=True).astype(I32)
    counts = jnp.sum(flat_e[:, :, None] == jnp.arange(N_EXPERTS, dtype=I32)[None, None, :], axis=1).astype(I32)
    offs = (jnp.cumsum(counts, axis=1) - counts).astype(I32)
    n_list = tb * TOP_K
    plane = tb + SUBLANES
    tok_list = jnp.pad(order >> 2, ((0, 0), (0, MOE_M)))
    row_list = jnp.pad((order & (TOP_K - 1)) * plane + (order >> 2), ((0, 0), (0, MOE_M)), constant_values=tb)
    nch = (counts + MOE_M - 1) // MOE_M
    first = (jnp.cumsum(nch, axis=1) - nch).astype(I32)
    table_len = -(-(n_list // MOE_M + N_EXPERTS + 2) // SUBLANES) * SUBLANES
    j = jnp.arange(table_len - 1, dtype=I32)[None, :]
    owner = jnp.sum(j[:, :, None] >= (first + nch)[:, None, :], axis=2)
    owner_c = jnp.minimum(owner, N_EXPERTS - 1)
    within = j - jnp.take_along_axis(first, owner_c, axis=1)
    c_base = jnp.where(owner < N_EXPERTS, jnp.take_along_axis(offs, owner_c, axis=1) + within * MOE_M, n_list)
    c_base = jnp.concatenate([jnp.full((n_super, 1), n_list, I32), c_base.astype(I32)], axis=1)

    piece = lambda sb, e, *_: (sb * n_piece + jnp.maximum(e - N_EXPERTS, 0), 0)
    wmap = lambda sb, e, *_: (jnp.minimum(e, N_EXPERTS - 1), 0, 0)
    grid_spec = pltpu.PrefetchScalarGridSpec(
        num_scalar_prefetch=3,
        grid=(n_super, N_EXPERTS + n_piece),
        in_specs=[
            pl.BlockSpec((1, 1, n_list + MOE_M), lambda sb, e, *_: (sb, 0, 0), memory_space=pltpu.SMEM),
            pl.BlockSpec((1, 1, n_list + MOE_M), lambda sb, e, *_: (sb, 0, 0), memory_space=pltpu.SMEM),
            pl.BlockSpec((tb, d // 2), lambda sb, e, *_: (sb, 0), pipeline_mode=pl.Buffered(1)),
            pl.BlockSpec((1, d, 2 * dff), wmap),
            pl.BlockSpec((1, 1, 2 * dff), wmap),
            pl.BlockSpec((1, dff, d), w2map),
            pl.BlockSpec((1, 1, d), wmap),
            pl.BlockSpec((rb, LANES), piece),
            pl.BlockSpec((rb, d), piece),
            pl.BlockSpec((1, 1, d), lambda sb, e, *_: ((sb * tb) // s, 0, 0)),
        ],
        out_specs=pl.BlockSpec((rb, d), piece),
        scratch_shapes=[
            pltpu.VMEM((TOP_K * plane, d // 2), U32),
            pltpu.VMEM((2, MOE_M, d // 2), U32),
            pltpu.VMEM((2, MOE_M, d // 2), U32),
        ],
    )
    out = pl.pallas_call(
        functools.partial(_moe_kernel, tb=tb, rb=rb, table_len=table_len),
        grid_spec=grid_spec,
        out_shape=jax.ShapeDtypeStruct((t, d), F32),
        compiler_params=_cparams(("arbitrary", "arbitrary")),
        name="moe",
    )(first.reshape(-1), nch.astype(I32).reshape(-1), c_base.reshape(-1),
      tok_list.reshape(n_super, 1, n_list + MOE_M), row_list.reshape(n_super, 1, n_list + MOE_M),
      hp.reshape(t, d // 2),
      w1p, b1p, w2b, b2, gate.reshape(t, LANES), xn.reshape(t, d), g2.reshape(bsz, 1, d))
    return out.reshape(bsz, s, d)


MXU_COLS = 256


def _deinterleave_kernel(w_ref, perm_ref, o_ref):
    half = w_ref.shape[2] // 2
    hw = MXU_COLS // 2
    for b in range(w_ref.shape[2] // MXU_COLS):
        blk = w_ref[0, :, b * MXU_COLS:(b + 1) * MXU_COLS].astype(BF16)
        y = jnp.dot(blk, perm_ref[...], preferred_element_type=F32).astype(BF16)
        o_ref[0, :, b * hw:(b + 1) * hw] = y[:, :hw]
        o_ref[0, :, half + b * hw:half + (b + 1) * hw] = y[:, hw:]


def _deinterleave_cast(w1, layer, rows=512):
    depth, n_e, d, two_f = w1.shape
    src = np.concatenate([np.arange(0, MXU_COLS, 2), np.arange(1, MXU_COLS, 2)])
    perm = np.zeros((MXU_COLS, MXU_COLS), np.float32)
    perm[src, np.arange(MXU_COLS)] = 1.0
    return pl.pallas_call(
        _deinterleave_kernel,
        grid=(n_e, d // rows),
        in_specs=[pl.BlockSpec((1, rows, two_f), lambda e, r: (layer * n_e + e, r, 0)),
                  pl.BlockSpec((MXU_COLS, MXU_COLS), lambda e, r: (0, 0))],
        out_specs=pl.BlockSpec((1, rows, two_f), lambda e, r: (e, r, 0)),
        out_shape=jax.ShapeDtypeStruct((n_e, d, two_f), BF16),
        compiler_params=_cparams(("parallel", "parallel")),
        name="w1_deinterleave",
    )(w1.reshape(depth * n_e, d, two_f), jnp.asarray(perm, BF16))


def _deinterleave_bias(b1_l):
    n_e, two_f = b1_l.shape
    return jnp.concatenate([b1_l[:, 0::2], b1_l[:, 1::2]], axis=1).reshape(n_e, 1, two_f)


def kernel(x, c, rel_bias, mod_w, mod_b, norm_mix_w, norm_ffn_w, w_in, conv_w, a_log, dt_bias, gdn_norm_w,
           q_norm_w, k_norm_w, w_out, router_w, router_b, w1, b1, w2, b2):
    depth = mod_w.shape[0]
    bsz, s, d = x.shape
    mod = _modulation(c, mod_w, mod_b)
    tm = min(512, s)
    tb = min(MOE_TB, bsz * s)
    rb = min(MOE_RB, tb)
    n_e, dff = w2.shape[1], w2.shape[2]
    w2b = w2.astype(BF16).reshape(depth * n_e, dff, d)
    for l in range(depth):
        qkv_a, z_a, qb, kb, vb, qi, small = _inproj(
            x, mod[l], norm_mix_w[l], _permute_w_in(w_in[l]), q_norm_w[l], k_norm_w[l], tm)
        y_a = _gdn(qkv_a, z_a, small, conv_w[l], a_log[l], dt_bias[l], gdn_norm_w[l], sb=tm)
        y_b = _dsa(qb, kb, vb, qi, small, rel_bias)
        xn, hp, ridx, gate = _outproj(y_a, y_b, x, mod[l], w_out[l].astype(BF16), norm_ffn_w[l],
                                      router_w[l], router_b[l], tm)
        x = _moe(xn, hp, ridx, gate, mod[l][:, 5], _deinterleave_cast(w1, l), _deinterleave_bias(b1[l]), w2b,
                 b2[l].reshape(n_e, 1, d), tb, rb, layer=l)
    return x
```

```python
import functools
import math

import jax
import jax.numpy as jnp
import numpy as np
from jax import lax
from jax.experimental import pallas as pl
from jax.experimental.pallas import tpu as pltpu

F32 = jnp.float32
BF16 = jnp.bfloat16
I32 = jnp.int32
U32 = jnp.uint32
HIGHEST = lax.Precision.HIGHEST

LANES = 128
SUBLANES = 8
VMEM_LIMIT_BYTES = 56 * 1024 * 1024

CHUNK = 64
HEAD_DIM_A = 128
N_HEADS_A = 4
WIDTH_A = N_HEADS_A * HEAD_DIM_A
CONV_K = 4
HEAD_DIM_B = 64
N_HEADS_B = 8
WIDTH_B = N_HEADS_B * HEAD_DIM_B
IDX_HEADS = 8
IDX_DIM = 64
TOPK_KEYS_MAX = 256
REL_BUCKETS = 32
REL_MAX_DIST = 1024
N_EXPERTS = 32
TOP_K = 4
SWIGLU_ALPHA = 1.702
SWIGLU_LIMIT = 7.0
EPS = 1e-6
NEG_BIG = -1e30

C_QKVA = 0
C_Z = C_QKVA + 3 * WIDTH_A
C_QB = C_Z + WIDTH_A
C_KB = C_QB + WIDTH_B
C_VB = C_KB + WIDTH_B
C_QI = C_VB + WIDTH_B
C_SMALL = C_QI + IDX_HEADS * IDX_DIM
D_IN_PAD = C_SMALL + LANES
S_KIDX = 0
S_B = IDX_DIM
S_A = S_B + N_HEADS_A
S_WIDX = S_A + N_HEADS_A


def _cparams(sem):
    return pltpu.CompilerParams(dimension_semantics=sem, vmem_limit_bytes=VMEM_LIMIT_BYTES)


def _sigmoid(x):
    return jax.nn.sigmoid(x)


def _silu(x):
    return x * _sigmoid(x)


def _softplus(x):
    return jnp.maximum(x, 0.0) + jnp.log(1.0 + jnp.exp(-jnp.abs(x)))


def _mod_kernel(c_ref, w_ref, b_ref, o_ref):
    a = _silu(c_ref[...])
    o_ref[0] = jnp.dot(a, w_ref[0], precision=HIGHEST, preferred_element_type=F32) + b_ref[0]


def _modulation(c, mod_w, mod_b):
    depth, d, n = mod_w.shape
    bsz = c.shape[0]
    rows = -(-bsz // SUBLANES) * SUBLANES
    c_pad = jnp.zeros((rows, d), F32).at[:bsz].set(c)
    tn = 1536
    out = pl.pallas_call(
        _mod_kernel,
        grid=(depth, n // tn),
        in_specs=[
            pl.BlockSpec((rows, d), lambda l, j: (0, 0)),
            pl.BlockSpec((1, d, tn), lambda l, j: (l, 0, j)),
            pl.BlockSpec((1, 1, tn), lambda l, j: (l, 0, j)),
        ],
        out_specs=pl.BlockSpec((1, rows, tn), lambda l, j: (l, 0, j)),
        out_shape=jax.ShapeDtypeStruct((depth, rows, n), F32),
        compiler_params=_cparams(("arbitrary", "arbitrary")),
        name="adaln_mod",
    )(c_pad, mod_w, mod_b.reshape(depth, 1, n))
    return out[:, :bsz].reshape(depth, bsz, 6, d)


def _head_rms(t, group_ref, wn, inv_dim):
    t2 = t * t
    hi = t2.astype(BF16)
    lo = (t2 - hi.astype(F32)).astype(BF16)
    ss = (jnp.dot(hi, group_ref[...], preferred_element_type=F32)
          + jnp.dot(lo, group_ref[...], preferred_element_type=F32))
    return t * lax.rsqrt(ss * inv_dim + EPS) * wn


def _inproj_kernel(x_ref, mod_ref, nw_ref, w_ref, group_ref, qn_ref, kn_ref,
                   qkva_ref, z_ref, qb_ref, kb_ref, vb_ref, qi_ref, small_ref):
    x = x_ref[0]
    ms = jnp.mean(x * x, axis=-1, keepdims=True)
    y = x * lax.rsqrt(ms + EPS) * nw_ref[...]
    h = y * (1.0 + mod_ref[0, 1:2, :]) + mod_ref[0, 0:1, :]
    hb = h.astype(BF16)

    def mm(lo, width):
        return jnp.dot(hb, w_ref[:, lo:lo + width], preferred_element_type=F32)

    qkva_ref[0] = mm(C_QKVA, 3 * WIDTH_A)
    z_ref[0] = mm(C_Z, WIDTH_A)
    q = _head_rms(mm(C_QB, WIDTH_B), group_ref, qn_ref[...], 1.0 / HEAD_DIM_B)
    qb_ref[0] = (q * (HEAD_DIM_B ** -0.5 * LOG2E)).astype(BF16)
    k = _head_rms(mm(C_KB, WIDTH_B), group_ref, kn_ref[...], 1.0 / HEAD_DIM_B)
    kb_ref[0] = k.astype(BF16)
    v = mm(C_VB, WIDTH_B)
    lane = lax.broadcasted_iota(I32, (v.shape[0], LANES), 1)
    tail = (lane == HEAD_DIM_B).astype(F32)
    for p in range(N_HEADS_B // 2):
        pair = v[:, p * LANES:(p + 1) * LANES]
        vb_ref[0, :, (2 * p) * LANES:(2 * p + 1) * LANES] = jnp.where(lane < HEAD_DIM_B, pair, tail).astype(BF16)
        vb_ref[0, :, (2 * p + 1) * LANES:(2 * p + 2) * LANES] = jnp.where(
            lane < HEAD_DIM_B, pltpu.roll(pair, HEAD_DIM_B, axis=1), tail).astype(BF16)
    qi_ref[0] = mm(C_QI, IDX_HEADS * IDX_DIM).astype(BF16)
    small_ref[0] = mm(C_SMALL, LANES)


def _permute_w_in(w_in_l):
    d = w_in_l.shape[0]
    o = 0
    qkva = w_in_l[:, o:o + 3 * WIDTH_A]; o += 3 * WIDTH_A
    z = w_in_l[:, o:o + WIDTH_A]; o += WIDTH_A
    b = w_in_l[:, o:o + N_HEADS_A]; o += N_HEADS_A
    a = w_in_l[:, o:o + N_HEADS_A]; o += N_HEADS_A
    qkvb = w_in_l[:, o:o + 3 * WIDTH_B]; o += 3 * WIDTH_B
    qi = w_in_l[:, o:o + IDX_HEADS * IDX_DIM]; o += IDX_HEADS * IDX_DIM
    ki = w_in_l[:, o:o + IDX_DIM]; o += IDX_DIM
    wi = w_in_l[:, o:o + IDX_HEADS]; o += IDX_HEADS
    pad = jnp.zeros((d, LANES - IDX_DIM - 2 * N_HEADS_A - IDX_HEADS), w_in_l.dtype)
    return jnp.concatenate([qkva, z, qkvb, qi, ki, b, a, wi, pad], axis=1).astype(BF16)


def _group_ones(width, group):
    g = np.arange(width) // group
    return jnp.asarray((g[:, None] == g[None, :]).astype(np.float32), dtype=BF16)


def _inproj(x, mod_l, norm_w, w_perm, q_norm_w, k_norm_w, tm):
    bsz, s, d = x.shape
    f = lambda b, i: (b, i, 0)
    const2 = lambda b, i: (0, 0)
    outs = [
        (3 * WIDTH_A, F32), (WIDTH_A, F32), (WIDTH_B, BF16), (WIDTH_B, BF16), (N_HEADS_B * LANES, BF16),
        (IDX_HEADS * IDX_DIM, BF16), (LANES, F32),
    ]
    return pl.pallas_call(
        _inproj_kernel,
        grid=(bsz, s // tm),
        in_specs=[
            pl.BlockSpec((1, tm, d), f),
            pl.BlockSpec((1, 6, d), lambda b, i: (b, 0, 0)),
            pl.BlockSpec((1, d), const2),
            pl.BlockSpec((d, D_IN_PAD), const2),
            pl.BlockSpec((WIDTH_B, WIDTH_B), const2),
            pl.BlockSpec((1, WIDTH_B), const2),
            pl.BlockSpec((1, WIDTH_B), const2),
        ],
        out_specs=[pl.BlockSpec((1, tm, w), f) for w, _ in outs],
        out_shape=[jax.ShapeDtypeStruct((bsz, s, w), dt) for w, dt in outs],
        compiler_params=_cparams(("parallel", "parallel")),
        name="inproj",
    )(x, mod_l, norm_w.reshape(1, d), w_perm, _group_ones(WIDTH_B, HEAD_DIM_B),
      jnp.tile(q_norm_w, N_HEADS_B).reshape(1, WIDTH_B), jnp.tile(k_norm_w, N_HEADS_B).reshape(1, WIDTH_B))


def _dot_nt(a, b, precision=None):
    return lax.dot_general(a, b, (((1,), (1,)), ((), ())), precision=precision, preferred_element_type=F32)


def _split2(x):
    hi = x.astype(BF16)
    return hi, (x - hi.astype(F32)).astype(BF16)


def _split3(x):
    hi = x.astype(BF16)
    r = x - hi.astype(F32)
    mid = r.astype(BF16)
    return hi, mid, (r - mid.astype(F32)).astype(BF16)


def _mm3(a, b):
    return (jnp.dot(a[0], b[0], preferred_element_type=F32) + jnp.dot(a[0], b[1], preferred_element_type=F32)
            + jnp.dot(a[1], b[0], preferred_element_type=F32))


GDN_PAR = 8


def _gdn_kernel(qkv_ref, z_ref, small_ref, convw_ref, alog_ref, dtb_ref, nw_ref,
                y_ref, xe_ref, u_ref, state_ref, uval_ref, wdec_ref, qg_ref, kdec_ref, attn_ref, egl_ref, *, sb):
    n_chunks = sb // CHUNK
    halo = SUBLANES

    @pl.when(pl.program_id(1) == 0)
    def _():
        xe_ref[0:halo, :] = jnp.zeros((halo, 3 * WIDTH_A), F32)
        state_ref[...] = jnp.zeros_like(state_ref)

    xe_ref[halo:halo + sb, :] = qkv_ref[0]

    rows = 128
    for g in range(3 * WIDTH_A // LANES):
        cs = slice(g * LANES, (g + 1) * LANES)
        for r in range(sb // rows):
            base = halo - (CONV_K - 1) + r * rows
            acc = xe_ref[base:base + rows, cs] * convw_ref[0:1, cs]
            for j in range(1, CONV_K):
                acc = acc + xe_ref[base + j:base + j + rows, cs] * convw_ref[j:j + 1, cs]
            u_ref[r * rows:(r + 1) * rows, cs] = _silu(acc)

    xe_ref[0:halo, :] = xe_ref[sb:sb + halo, :]

    wide = N_HEADS_A * CHUNK
    heads = range(N_HEADS_A)
    ii = lax.broadcasted_iota(I32, (CHUNK, wide), 0)
    jj = lax.broadcasted_iota(I32, (CHUNK, wide), 1) & (CHUNK - 1)
    eye_w = (ii == jj).astype(F32)
    tri = (lax.broadcasted_iota(I32, (CHUNK, CHUNK), 0)
           >= lax.broadcasted_iota(I32, (CHUNK, CHUNK), 1)).astype(F32).astype(BF16)
    shift = int(math.log2(CHUNK))
    bd_mask = ((lax.broadcasted_iota(I32, (wide, wide), 0) >> shift)
               == (lax.broadcasted_iota(I32, (wide, wide), 1) >> shift)).astype(F32)

    bd_mask = bd_mask.astype(BF16)

    def block_diag(parts):
        return tuple(jnp.concatenate([m] * N_HEADS_A, axis=0) * bd_mask for m in parts)

    def prepare(c):
        rs = pl.ds(pl.multiple_of(c * CHUNK, CHUNK), CHUNK)
        qn, kn, v, beta, g_b = [], [], [], [], []
        for h in heads:
            q = u_ref[rs, h * HEAD_DIM_A:(h + 1) * HEAD_DIM_A]
            k = u_ref[rs, WIDTH_A + h * HEAD_DIM_A:WIDTH_A + (h + 1) * HEAD_DIM_A]
            v.append(u_ref[rs, 2 * WIDTH_A + h * HEAD_DIM_A:2 * WIDTH_A + (h + 1) * HEAD_DIM_A])
            qn.append(q * (lax.rsqrt(jnp.sum(q * q, axis=-1, keepdims=True) + EPS) * (HEAD_DIM_A ** -0.5)))
            kn.append(k * lax.rsqrt(jnp.sum(k * k, axis=-1, keepdims=True) + EPS))
            beta.append(_sigmoid(small_ref[0, rs, S_B + h:S_B + h + 1]))
            g = -jnp.exp(alog_ref[0:1, h:h + 1]) * _softplus(small_ref[0, rs, S_A + h:S_A + h + 1]
                                                            + dtb_ref[0:1, h:h + 1])
            g_b.append(jnp.broadcast_to(g, (CHUNK, CHUNK)))
        gc_w = sum(jnp.dot(tri, part, preferred_element_type=F32) for part in _split3(jnp.concatenate(g_b, axis=1)))
        gc_row = jnp.sum(jnp.where(ii == jj, gc_w, 0.0), axis=0, keepdims=True)
        decay_w = jnp.exp(jnp.where(ii >= jj, gc_w - gc_row, NEG_BIG))
        k_beta = [kn[h] * beta[h] for h in heads]
        kk_w = jnp.concatenate([_dot_nt(k_beta[h].astype(BF16), kn[h].astype(BF16)) for h in heads], axis=1)
        a_w = -jnp.where(ii > jj, kk_w * decay_w, 0.0)
        gc = [gc_w[:, h * CHUNK:h * CHUNK + 1] for h in heads]
        egc = [jnp.exp(gc[h]) for h in heads]
        qk = [_dot_nt(qn[h].astype(BF16), kn[h].astype(BF16)) for h in heads]
        for h in heads:
            g_last = gc[h][CHUNK - 1:CHUNK, :]
            qg_ref[c, h] = (qn[h] * egc[h]).astype(BF16)
            kdec_ref[c, h] = kn[h] * jnp.exp(g_last - gc[h])
            attn_ref[c, h] = (qk[h] * decay_w[:, h * CHUNK:(h + 1) * CHUNK]).astype(BF16)
            egl_ref[c, h] = jnp.broadcast_to(jnp.exp(g_last), (SUBLANES, LANES))
        return a_w, [_split2(v[h] * beta[h]) for h in heads], [_split2(k_beta[h] * egc[h]) for h in heads]

    def solve_body(cg, carry):
        group = range(GDN_PAR)
        chunks = [cg * GDN_PAR + i for i in group]
        pre = [prepare(c) for c in chunks]
        t_w = [eye_w + pre[i][0] for i in group]
        p_parts = [_split2(pre[i][0]) for i in group]
        bd = [block_diag(p_parts[i]) for i in group]
        for _ in range(shift - 1):
            prod = [_mm3(p_parts[i], bd[i]) for i in group]
            p_parts = [_split2(prod[i]) for i in group]
            bd = [block_diag(p_parts[i]) for i in group]
            upd = [_mm3(_split2(t_w[i]), bd[i]) for i in group]
            t_w = [t_w[i] + upd[i] for i in group]
        for i in group:
            t_h = [_split2(t_w[i][:, h * CHUNK:(h + 1) * CHUNK]) for h in heads]
            u_val = [_mm3(t_h[h], pre[i][1][h]) for h in heads]
            w_dec = [_mm3(t_h[h], pre[i][2][h]) for h in heads]
            for h in heads:
                uval_ref[chunks[i], h] = u_val[h]
                wdec_ref[chunks[i], h] = w_dec[h].astype(BF16)
        return carry

    lax.fori_loop(0, n_chunks // GDN_PAR, solve_body, 0)

    def scan_body(c, carry):
        rs = pl.ds(pl.multiple_of(c * CHUNK, CHUNK), CHUNK)
        state = [state_ref[h] for h in heads]
        state_b = [s_h.astype(BF16) for s_h in state]
        w_s = [jnp.dot(wdec_ref[c, h], state_b[h], preferred_element_type=F32) for h in heads]
        v_new = [(uval_ref[c, h] - w_s[h]).astype(BF16) for h in heads]
        o = [jnp.dot(qg_ref[c, h], state_b[h], preferred_element_type=F32)
             + jnp.dot(attn_ref[c, h], v_new[h], preferred_element_type=F32) for h in heads]
        for h in heads:
            state_ref[h] = (state[h] * egl_ref[c, h][0:1, 0:1]
                            + jnp.dot(kdec_ref[c, h].T.astype(BF16), v_new[h], preferred_element_type=F32))
        for h in heads:
            hs = slice(h * HEAD_DIM_A, (h + 1) * HEAD_DIM_A)
            on = o[h] * lax.rsqrt(jnp.mean(o[h] * o[h], axis=-1, keepdims=True) + EPS) * nw_ref[...]
            y_ref[0, rs, hs] = (on * _silu(z_ref[0, rs, hs])).astype(BF16)
        return carry

    lax.fori_loop(0, n_chunks, scan_body, 0)


def _gdn(qkv_a, z_a, small, conv_w, a_log, dt_bias, norm_w, sb):
    bsz, s, _ = qkv_a.shape
    n_c = sb // CHUNK
    assert n_c % GDN_PAR == 0 and s % sb == 0
    per_head = (n_c, N_HEADS_A, CHUNK, HEAD_DIM_A)
    f = lambda b, i: (b, i, 0)
    const2 = lambda b, i: (0, 0)
    return pl.pallas_call(
        functools.partial(_gdn_kernel, sb=sb),
        grid=(bsz, s // sb),
        in_specs=[
            pl.BlockSpec((1, sb, 3 * WIDTH_A), f),
            pl.BlockSpec((1, sb, WIDTH_A), f),
            pl.BlockSpec((1, sb, LANES), f),
            pl.BlockSpec((CONV_K, 3 * WIDTH_A), const2),
            pl.BlockSpec((1, N_HEADS_A), const2),
            pl.BlockSpec((1, N_HEADS_A), const2),
            pl.BlockSpec((1, HEAD_DIM_A), const2),
        ],
        out_specs=pl.BlockSpec((1, sb, WIDTH_A), f),
        out_shape=jax.ShapeDtypeStruct((bsz, s, WIDTH_A), BF16),
        scratch_shapes=[
            pltpu.VMEM((sb + SUBLANES, 3 * WIDTH_A), F32),
            pltpu.VMEM((sb, 3 * WIDTH_A), F32),
            pltpu.VMEM((N_HEADS_A, HEAD_DIM_A, HEAD_DIM_A), F32),
            pltpu.VMEM(per_head, F32),
            pltpu.VMEM(per_head, BF16),
            pltpu.VMEM(per_head, BF16),
            pltpu.VMEM(per_head, F32),
            pltpu.VMEM((n_c, N_HEADS_A, CHUNK, CHUNK), BF16),
            pltpu.VMEM((n_c, N_HEADS_A, SUBLANES, LANES), F32),
        ],
        compiler_params=_cparams(("parallel", "arbitrary")),
        name="gdn",
    )(qkv_a, z_a, small, conv_w, a_log.reshape(1, -1), dt_bias.reshape(1, -1), norm_w.reshape(1, -1))


QB = 128
FAR_T = 512
FAR_G = FAR_T // LANES
ATT_W = 256
LOG2E = 1.4426950408889634
NEAR_D = 9
NEAR_MIN = 5
INT_MIN = -2 ** 31


def _t5_bucket_np(rel):
    nb = REL_BUCKETS // 2
    max_exact = nb // 2
    side = np.where(rel > 0, nb, 0)
    n = np.abs(rel)
    nf = np.maximum(n, 1).astype(np.float32)
    large = max_exact + (np.log(nf / np.float32(max_exact)) / np.float32(math.log(REL_MAX_DIST / max_exact))
                         * np.float32(nb - max_exact)).astype(np.int32)
    large = np.minimum(large, nb - 1)
    return (side + np.where(n < max_exact, n, large)).astype(np.int32)


def _near_bucket_table():
    r = np.arange(QB)[:, None]
    c = np.arange(LANES)[None, :]
    return np.stack([_t5_bucket_np(c - r - LANES * d) for d in range(NEAR_D)])


FAR_BUCKET = int(_t5_bucket_np(np.array([-(NEAR_MIN * LANES + 1)]))[0])
assert all(int(b) == FAR_BUCKET for b in _t5_bucket_np(-np.arange((NEAR_MIN + 1) * LANES - (QB - 1), 1 << 20, 997)))


def _sortable_key(score):
    bits = pltpu.bitcast(score + 0.0, I32)
    return bits ^ ((bits >> 31) & 0x7FFFFFFF)


PLANE_G = 32


def _bit_transpose32(words):
    a = list(words)
    mask, j = 0x0000FFFF, 16
    while j:
        k = 0
        while k < 32:
            t = (a[k] ^ lax.shift_right_logical(a[k + j], jnp.int32(j))) & mask
            a[k] = a[k] ^ t
            a[k + j] = a[k + j] ^ lax.shift_left(t, jnp.int32(j))
            k = (k + j + 1) & ~j
        j >>= 1
        mask = (mask ^ (mask << j)) & 0xFFFFFFFF
    return a


def _dsa_kernel(rb_ref, qb_ref, qi_ref, small_ref, kb_ref, va_ref, kidx2_ref, tab_ref,
                y_ref, qis_ref, qs_ref, wb_ref, keys_ref, nbias_ref, thr_ref, jlim_ref,
                m_ref, acc_ref, s_ref, p_ref, peak_ref, planes_ref, need_ref, excess_ref, *, seq, k_sel):
    i = pl.program_id(1)
    lane = lax.broadcasted_iota(I32, (QB, LANES), 1)
    row = lax.broadcasted_iota(I32, (QB, LANES), 0)
    even_f = (lane < HEAD_DIM_B).astype(F32)
    even_b = even_f.astype(BF16)
    odd_b = (1.0 - even_f).astype(BF16)

    @pl.when(i == 0)
    def _():
        nbias_ref[...] = jnp.zeros_like(nbias_ref)
        keys_ref[...] = jnp.full(keys_ref.shape, INT_MIN, I32)

        def d_body(d, c0):
            tab = tab_ref[d]

            def b_body(bk, c1):
                hit = tab == bk
                for h in range(N_HEADS_B):
                    nbias_ref[d * N_HEADS_B + h] = jnp.where(hit, rb_ref[bk, h] * LOG2E,
                                                             nbias_ref[d * N_HEADS_B + h])
                return c1

            return lax.fori_loop(0, REL_BUCKETS, b_body, c0)

        lax.fori_loop(0, NEAR_D, d_body, 0)

    for p in range(N_HEADS_B // 2):
        ps = slice(p * LANES, (p + 1) * LANES)
        qi_pair = qi_ref[0, :, ps]
        qis_ref[(2 * p) * QB:(2 * p + 1) * QB, :] = qi_pair * even_b
        qis_ref[(2 * p + 1) * QB:(2 * p + 2) * QB, :] = qi_pair * odd_b
        q_pair = qb_ref[0, :, ps]
        qs_ref[(2 * p) * QB:(2 * p + 1) * QB, :] = q_pair * even_b
        qs_ref[(2 * p + 1) * QB:(2 * p + 2) * QB, :] = q_pair * odd_b
    w_scale = IDX_HEADS ** -0.5 * IDX_DIM ** -0.5
    for h in range(IDX_HEADS):
        wb_ref[h] = jnp.broadcast_to(small_ref[0, :, S_WIDX + h:S_WIDX + h + 1] * w_scale, (QB, LANES))

    limit = i * QB + CHUNK + jnp.where(row >= CHUNK, CHUNK, 0)

    def score_body(t, c0):
        for c in range(FAR_T // ATT_W):
            k0 = t * FAR_T + c * ATT_W
            k_part = kidx2_ref[0, pl.ds(pl.multiple_of(k0, ATT_W), ATT_W), :]
            acc = None
            for h in range(IDX_HEADS):
                dots = _dot_nt(qis_ref[h * QB:(h + 1) * QB, :], k_part)
                term = jnp.maximum(dots, 0.0) * jnp.concatenate([wb_ref[h]] * (ATT_W // LANES), axis=1)
                acc = term if acc is None else acc + term
            for gg in range(ATT_W // LANES):
                col = k0 + gg * LANES + lane
                keys_ref[(k0 // LANES) + gg] = jnp.where(
                    col < limit, _sortable_key(acc[:, gg * LANES:(gg + 1) * LANES]), INT_MIN)
        return c0

    n_groups = i + 1
    n_tiles = i // FAR_G + 1
    lax.fori_loop(0, n_tiles, score_body, 0)

    @pl.when(n_tiles % 2 == 1)
    def _():
        for g in range(FAR_G):
            keys_ref[n_tiles * FAR_G + g] = jnp.full((QB, LANES), INT_MIN, I32)

    def count(pred):
        def t_body(t, acc):
            for g in range(2 * FAR_G):
                grp = t * (2 * FAR_G) + g
                acc = acc + jnp.where(pred(keys_ref[grp], grp * LANES + lane), 1, 0)
            return acc

        acc = lax.fori_loop(0, (n_tiles + 1) // 2, t_body, jnp.zeros((QB, LANES), I32))
        return jnp.broadcast_to(jnp.sum(acc, axis=1, keepdims=True), (QB, LANES))

    thr_ref[...] = jnp.full((QB, LANES), INT_MIN, I32)
    jlim_ref[...] = jnp.full((QB, LANES), -1, I32)

    def lane_total(x):
        return jnp.broadcast_to(jnp.sum(x, axis=1, keepdims=True), (QB, LANES))

    @pl.when(n_groups * QB > k_sel)
    def _():
        def transpose_half(half):
            def row_body(rr, c0):
                rows = pl.ds(pl.multiple_of(rr * SUBLANES, SUBLANES), SUBLANES)
                planes = _bit_transpose32([keys_ref[half * PLANE_G + g, rows, :] ^ INT_MIN for g in range(PLANE_G)])
                for b in range(32):
                    planes_ref[half, b, rows, :] = planes[b]
                return c0

            lax.fori_loop(0, QB // SUBLANES, row_body, 0)

        def search(n_half):
            halves = range(n_half)
            row_groups = (pl.ds(0, QB // 2), pl.ds(QB // 2, QB // 2))
            shape = (QB // 2, LANES)

            def total(x):
                return jnp.broadcast_to(jnp.sum(x, axis=1, keepdims=True), shape)

            def digit(step, rows, state):
                r, eq, above = state
                cls = []
                for half in halves:
                    hi = eq[half] & planes_ref[half, 2 * step, rows, :]
                    lo_plane = planes_ref[half, 2 * step + 1, rows, :]
                    zero_hi = eq[half] ^ hi
                    e11 = hi & lo_plane
                    e01 = zero_hi & lo_plane
                    cls.append((e11, hi ^ e11, e01, zero_hi ^ e01))
                n11, n10, n01 = [sum(lax.population_count(cls[h][d]) for h in halves) for d in range(3)]
                c3 = above + n11
                c2 = c3 + n10
                c1 = c2 + n01
                d3, d2, d1 = [total(c) >= k_sel for c in (c3, c2, c1)]
                eq = tuple(jnp.where(d3, cls[h][0], jnp.where(d2, cls[h][1], jnp.where(d1, cls[h][2], cls[h][3])))
                           for h in halves)
                above = jnp.where(d3, above, jnp.where(d2, c3, jnp.where(d1, c2, c1)))
                value = jnp.where(d3, 3, jnp.where(d2, 2, jnp.where(d1, 1, 0)))
                return r | lax.shift_left(value, 30 - 2 * step), eq, above

            def digit_body(step, carry):
                return tuple(digit(step, rows, state) for rows, state in zip(row_groups, carry))

            zero = jnp.zeros(shape, I32)
            full = jnp.full(shape, -1, I32)
            start = (zero, (full,) * n_half, zero)
            for rows, (r, eq, above) in zip(row_groups, lax.fori_loop(0, 16, digit_body, (start, start))):
                thr_ref[rows, :] = r ^ INT_MIN
                need = k_sel - total(above)
                need_ref[rows, :] = need
                excess_ref[rows, :] = total(sum(lax.population_count(eq[h]) for h in halves)) - need

        transpose_half(0)

        @pl.when(n_groups > PLANE_G)
        def _():
            transpose_half(1)
            search(2)

        @pl.when(n_groups <= PLANE_G)
        def _():
            search(1)

        r = thr_ref[...]
        need = need_ref[...]
        excess = excess_ref[...]
        jlim_ref[...] = jnp.where(r == INT_MIN, -1, seq)

        @pl.when(jnp.max(excess) > 0)
        def _():
            def j_body(step, jl):
                cand = jl + lax.shift_left(jnp.int32(1), (seq.bit_length() - 1) - step)
                cnt = count(lambda kt, col: (kt == r) & (col < cand))
                return jnp.where(cnt < need, cand, jl)

            jl = lax.fori_loop(0, seq.bit_length(), j_body, jnp.zeros((QB, LANES), I32))
            jlim_ref[...] = jnp.where(r == INT_MIN, -1, jl)

    m_ref[...] = jnp.full(m_ref.shape, NEG_BIG, F32)
    acc_ref[...] = jnp.zeros_like(acc_ref)

    part_g = ATT_W // LANES

    def tile_mask(g0, groups):
        negm = []
        for g in range(groups):
            kt = keys_ref[g0 + g]
            sel = (kt > thr_ref[...]) | ((kt == thr_ref[...]) & ((g0 + g) * LANES + lane <= jlim_ref[...]))
            negm.append(jnp.where(sel, 0.0, -jnp.inf))
        return negm

    def logits_pair(g0, groups, p, negm, near, buf):
        heads = ((2 * p, slice(0, QB)), (2 * p + 1, slice(QB, 2 * QB)))
        peak = [None, None]
        for c in range(groups // part_g):
            ks = pl.ds(pl.multiple_of((g0 + c * part_g) * LANES, ATT_W), ATT_W)
            s = _dot_nt(qs_ref[2 * p * QB:(2 * p + 2) * QB, :], kb_ref[0, ks, p * LANES:(p + 1) * LANES])
            for gg in range(part_g):
                g = c * part_g + gg
                for n, (h, rows) in enumerate(heads):
                    extra = negm[g]
                    if near:
                        extra = extra + nbias_ref[jnp.clip(i - (g0 + g), 0, NEAR_D - 1) * N_HEADS_B + h]
                    v = s[rows, gg * LANES:(gg + 1) * LANES] + extra
                    s_ref[buf, h, g] = v
                    peak[n] = v if peak[n] is None else jnp.maximum(peak[n], v)
        for n, (h, _) in enumerate(heads):
            peak_ref[buf, h] = peak[n]

    def softmax_head(h, groups, bias_scalar, buf):
        m_old = m_ref[h]
        m_new = jnp.maximum(m_old, jnp.max(peak_ref[buf, h], axis=1, keepdims=True) + bias_scalar)
        shift = jnp.concatenate([m_new - bias_scalar] * part_g, axis=1)
        for c in range(groups // part_g):
            part = jnp.concatenate([s_ref[buf, h, c * part_g + g] for g in range(part_g)], axis=1)
            p_ref[buf, h, :, c * ATT_W:(c + 1) * ATT_W] = jnp.exp2(part - shift).astype(BF16)
        acc_ref[h] = jnp.exp2(m_old - m_new) * acc_ref[h]
        m_ref[h] = m_new

    def pv_head(g0, groups, h, buf):
        ks = pl.ds(pl.multiple_of(g0 * LANES, FAR_T), groups * LANES)
        acc_ref[h] += jnp.dot(p_ref[buf, h, :, :groups * LANES], va_ref[0, ks, h * LANES:(h + 1) * LANES],
                              preferred_element_type=F32)

    n_pair = N_HEADS_B // 2
    n_stage = n_pair + 2

    def stage(t, near, negm, step, buf):
        g0 = t * FAR_G
        if step < n_pair:
            logits_pair(g0, FAR_G, step, negm, near, buf)
        if 1 <= step <= n_pair:
            for h in (2 * step - 2, 2 * step - 1):
                softmax_head(h, FAR_G, 0.0 if near else rb_ref[FAR_BUCKET, h] * LOG2E, buf)
        if step >= 2:
            for h in (2 * step - 4, 2 * step - 3):
                pv_head(g0, FAR_G, h, buf)

    def tile_body(near):
        def body(t, c0):
            negm = tile_mask(t * FAR_G, FAR_G)
            for step in range(n_stage):
                stage(t, near, negm, step, 0)
            return c0

        return body

    def tile_pair_body(u, c0):
        first, second = 2 * u, 2 * u + 1
        masks = (tile_mask(first * FAR_G, FAR_G), tile_mask(second * FAR_G, FAR_G))
        lag = n_stage // 2
        for slot in range(n_stage + lag):
            if slot < n_stage:
                stage(first, False, masks[0], slot, 0)
            if lag <= slot:
                stage(second, False, masks[1], slot - lag, 1)
        return c0

    far_tiles = jnp.maximum(i - NEAR_MIN, 0) // FAR_G
    far_pairs = far_tiles // 2
    lax.fori_loop(0, far_pairs, tile_pair_body, 0)
    lax.fori_loop(2 * far_pairs, far_tiles, tile_body(False), 0)
    lax.fori_loop(far_tiles, n_tiles, tile_body(True), 0)

    def head_out(h):
        a = acc_ref[h]
        return a * (1.0 / a[:, HEAD_DIM_B:HEAD_DIM_B + 1])

    for p in range(N_HEADS_B // 2):
        o_odd = pltpu.roll(head_out(2 * p + 1), HEAD_DIM_B, axis=1)
        y_ref[0, :, p * LANES:(p + 1) * LANES] = jnp.where(lane < HEAD_DIM_B, head_out(2 * p), o_odd).astype(BF16)


def _dsa(qb, kb, va, qi, small, rel_bias):
    bsz, s, _ = qb.shape
    assert s % FAR_T == 0 and s // LANES <= 2 * PLANE_G
    k_sel = min(TOPK_KEYS_MAX, s // 4)
    kidx = small[:, :, S_KIDX:S_KIDX + IDX_DIM].astype(BF16)
    kidx2 = jnp.concatenate([kidx, kidx], axis=-1)
    tab = jnp.asarray(_near_bucket_table())
    blk = lambda b, i: (b, i, 0)
    full = lambda b, i: (b, 0, 0)
    one = pl.Buffered(1)
    return pl.pallas_call(
        functools.partial(_dsa_kernel, seq=s, k_sel=k_sel),
        grid=(bsz, s // QB),
        in_specs=[
            pl.BlockSpec(memory_space=pltpu.SMEM),
            pl.BlockSpec((1, QB, WIDTH_B), blk),
            pl.BlockSpec((1, QB, IDX_HEADS * IDX_DIM), blk),
            pl.BlockSpec((1, QB, LANES), blk),
            pl.BlockSpec((1, s, WIDTH_B), full, pipeline_mode=one),
            pl.BlockSpec((1, s, N_HEADS_B * LANES), full, pipeline_mode=one),
            pl.BlockSpec((1, s, LANES), full, pipeline_mode=one),
            pl.BlockSpec((NEAR_D, QB, LANES), lambda b, i: (0, 0, 0), pipeline_mode=one),
        ],
        out_specs=pl.BlockSpec((1, QB, WIDTH_B), blk),
        out_shape=jax.ShapeDtypeStruct((bsz, s, WIDTH_B), BF16),
        scratch_shapes=[
            pltpu.VMEM((IDX_HEADS * QB, LANES), BF16),
            pltpu.VMEM((N_HEADS_B * QB, LANES), BF16),
            pltpu.VMEM((IDX_HEADS, QB, LANES), F32),
            pltpu.VMEM((2 * PLANE_G, QB, LANES), I32),
            pltpu.VMEM((NEAR_D * N_HEADS_B, QB, LANES), F32),
            pltpu.VMEM((QB, LANES), I32),
            pltpu.VMEM((QB, LANES), I32),
            pltpu.VMEM((N_HEADS_B, QB, LANES), F32),
            pltpu.VMEM((N_HEADS_B, QB, LANES), F32),
            pltpu.VMEM((2, N_HEADS_B, FAR_G, QB, LANES), F32),
            pltpu.VMEM((2, N_HEADS_B, QB, FAR_T), BF16),
            pltpu.VMEM((2, N_HEADS_B, QB, LANES), F32),
            pltpu.VMEM((2, 32, QB, LANES), I32),
            pltpu.VMEM((QB, LANES), I32),
            pltpu.VMEM((QB, LANES), I32),
        ],
        compiler_params=_cparams(("parallel", "arbitrary")),
        name="dsa",
    )(rel_bias, qb, qi, small, kb, va, kidx2, tab)


HALF_MASK = 0xFFFF0000


def _pack_halves(t):
    w = t.shape[1] // 2
    bits = pltpu.bitcast(t.astype(BF16).astype(F32), U32)
    return (bits[:, :w] >> 16) | (bits[:, w:] & jnp.uint32(HALF_MASK))


def _unpack_halves(p):
    lo = pltpu.bitcast(p << 16, F32)
    hi = pltpu.bitcast(p & jnp.uint32(HALF_MASK), F32)
    return jnp.concatenate([lo, hi], axis=1)


def _outproj_kernel(ya_ref, yb_ref, x_ref, mod_ref, wo_ref, nw_ref, rw_ref, rbias_ref,
                    xn_ref, hp_ref, ridx_ref, gate_ref):
    wa = ya_ref.shape[2]
    y = (jnp.dot(ya_ref[0], wo_ref[0:wa, :], preferred_element_type=F32)
         + jnp.dot(yb_ref[0], wo_ref[wa:, :], preferred_element_type=F32))
    xn = x_ref[0] + mod_ref[0, 2:3, :] * y
    xn_ref[0] = xn
    ms = jnp.mean(xn * xn, axis=-1, keepdims=True)
    h = xn * lax.rsqrt(ms + EPS) * nw_ref[...] * (1.0 + mod_ref[0, 4:5, :]) + mod_ref[0, 3:4, :]
    hp_ref[0] = _pack_halves(h)

    logits = jnp.dot(h, rw_ref[...], precision=HIGHEST, preferred_element_type=F32) + rbias_ref[...]
    lane = lax.broadcasted_iota(I32, logits.shape, 1)
    cur = logits
    vals, ridx = [], jnp.zeros(logits.shape, I32)
    for k in range(TOP_K):
        mx = jnp.max(cur, axis=1, keepdims=True)
        am = jnp.min(jnp.where(cur == mx, lane, LANES), axis=1, keepdims=True)
        cur = jnp.where(lane == am, -jnp.inf, cur)
        vals.append(mx)
        ridx = jnp.where(lane == k, am, ridx)
    ex = [jnp.exp(v - vals[0]) for v in vals]
    inv = 1.0 / (ex[0] + ex[1] + ex[2] + ex[3])
    gate = jnp.zeros(logits.shape, F32)
    for k in range(TOP_K):
        gate = jnp.where(lane == k, ex[k] * inv, gate)
    ridx_ref[0] = ridx
    gate_ref[0] = gate


def _outproj(y_a, y_b, x, mod_l, w_out_bf, norm_w, router_w, router_b, tm):
    bsz, s, d = x.shape
    n_e = router_w.shape[1]
    rw = jnp.zeros((d, LANES), F32).at[:, :n_e].set(router_w)
    rbias = jnp.full((1, LANES), NEG_BIG, F32).at[0, :n_e].set(router_b)
    blk = lambda b, i: (b, i, 0)
    const2 = lambda b, i: (0, 0)
    return pl.pallas_call(
        _outproj_kernel,
        grid=(bsz, s // tm),
        in_specs=[
            pl.BlockSpec((1, tm, y_a.shape[2]), blk),
            pl.BlockSpec((1, tm, y_b.shape[2]), blk),
            pl.BlockSpec((1, tm, d), blk),
            pl.BlockSpec((1, 6, d), lambda b, i: (b, 0, 0)),
            pl.BlockSpec((d, d), const2),
            pl.BlockSpec((1, d), const2),
            pl.BlockSpec((d, LANES), const2),
            pl.BlockSpec((1, LANES), const2),
        ],
        out_specs=[pl.BlockSpec((1, tm, d), blk), pl.BlockSpec((1, tm, d // 2), blk),
                   pl.BlockSpec((1, tm, LANES), blk), pl.BlockSpec((1, tm, LANES), blk)],
        out_shape=[jax.ShapeDtypeStruct((bsz, s, d), F32), jax.ShapeDtypeStruct((bsz, s, d // 2), U32),
                   jax.ShapeDtypeStruct((bsz, s, LANES), I32), jax.ShapeDtypeStruct((bsz, s, LANES), F32)],
        compiler_params=_cparams(("parallel", "parallel")),
        name="outproj_router",
    )(y_a, y_b, x, mod_l, w_out_bf, norm_w.reshape(1, d), rw, rbias)


MOE_TB = 2048
MOE_RB = 512
MOE_M = 144


def _moe_kernel(first_ref, nch_ref, cbase_ref, tok_ref, row_ref, hp_ref, w1_ref, b1_ref, w2_ref, b2_ref,
                gate_ref, x_ref, g2_ref, o_ref, slots_ref, xg_ref, yb_ref, *, tb, rb, table_len):
    sb = pl.program_id(0)
    e = pl.program_id(1)
    dff = w2_ref.shape[1]
    plane = tb + SUBLANES
    table = sb * table_len + 1

    def gather(j):
        base = cbase_ref[table + j]
        buf = (j + 2) % 2
        for r in range(MOE_M):
            xg_ref[buf, pl.ds(r, 1), :] = hp_ref[pl.ds(tok_ref[0, 0, base + r], 1), :]

    def scatter(j):
        base = cbase_ref[table + j]
        buf = (j + 2) % 2
        for r in range(MOE_M):
            slots_ref[pl.ds(row_ref[0, 0, base + r], 1), :] = yb_ref[buf, pl.ds(r, 1), :]

    @pl.when(e == 0)
    def _():
        yb_ref[...] = jnp.zeros_like(yb_ref)
        gather(0)

    @pl.when(e < N_EXPERTS)
    def _():
        j0 = first_ref[sb * N_EXPERTS + e]

        def chunk(j, carry):
            buf = j % 2
            xb = _unpack_halves(xg_ref[buf]).astype(BF16)
            gather(j + 1)
            scatter(j - 1)
            u = jnp.dot(xb, w1_ref[0], preferred_element_type=F32) + b1_ref[0]
            glu = jnp.minimum(u[:, :dff], SWIGLU_LIMIT)
            lin = jnp.clip(u[:, dff:], -SWIGLU_LIMIT, SWIGLU_LIMIT)
            act = glu * _sigmoid(SWIGLU_ALPHA * glu) * (lin + 1.0)
            y = jnp.dot(act.astype(BF16), w2_ref[0], preferred_element_type=F32) + b2_ref[0]
            yb_ref[buf] = _pack_halves(y)
            return carry

        lax.fori_loop(j0, j0 + nch_ref[sb * N_EXPERTS + e], chunk, 0)

    @pl.when(e == N_EXPERTS)
    def _():
        n_total = first_ref[sb * N_EXPERTS + N_EXPERTS - 1] + nch_ref[sb * N_EXPERTS + N_EXPERTS - 1]
        scatter(n_total - 1)

    @pl.when(e >= N_EXPERTS)
    def _():
        r0 = pl.multiple_of((e - N_EXPERTS) * rb, rb)
        acc = gate_ref[:, 0:1] * _unpack_halves(slots_ref[pl.ds(r0, rb), :])
        for k in range(1, TOP_K):
            rows = pl.ds(pl.multiple_of(k * plane + r0, SUBLANES), rb)
            acc = acc + gate_ref[:, k:k + 1] * _unpack_halves(slots_ref[rows, :])
        o_ref[...] = x_ref[...] + g2_ref[0] * acc


def _moe(xn, hp, ridx, gate, g2, w1p, b1p, w2b, b2, tb, rb, layer=0):
    bsz, s, d = xn.shape
    t = bsz * s
    n_super = t // tb
    n_piece = tb // rb
    dff = w2b.shape[1]
    w2map = lambda sb, e, *_: (layer * N_EXPERTS + jnp.minimum(e, N_EXPERTS - 1), 0, 0)
    flat_e = ridx[:, :, :TOP_K].reshape(n_super, tb * TOP_K)
    order = jnp.argsort(flat_e, axis=1, stable=True).astype(I32)
    counts = jnp.sum(flat_e[:, :, None] == jnp.arange(N_EXPERTS, dtype=I32)[None, None, :], axis=1).astype(I32)
    offs = (jnp.cumsum(counts, axis=1) - counts).astype(I32)
    n_list = tb * TOP_K
    plane = tb + SUBLANES
    tok_list = jnp.pad(order >> 2, ((0, 0), (0, MOE_M)))
    row_list = jnp.pad((order & (TOP_K - 1)) * plane + (order >> 2), ((0, 0), (0, MOE_M)), constant_values=tb)
    nch = (counts + MOE_M - 1) // MOE_M
    first = (jnp.cumsum(nch, axis=1) - nch).astype(I32)
    table_len = -(-(n_list // MOE_M + N_EXPERTS + 2) // SUBLANES) * SUBLANES
    j = jnp.arange(table_len - 1, dtype=I32)[None, :]
    owner = jnp.sum(j[:, :, None] >= (first + nch)[:, None, :], axis=2)
    owner_c = jnp.minimum(owner, N_EXPERTS - 1)
    within = j - jnp.take_along_axis(first, owner_c, axis=1)
    c_base = jnp.where(owner < N_EXPERTS, jnp.take_along_axis(offs, owner_c, axis=1) + within * MOE_M, n_list)
    c_base = jnp.concatenate([jnp.full((n_super, 1), n_list, I32), c_base.astype(I32)], axis=1)

    piece = lambda sb, e, *_: (sb * n_piece + jnp.maximum(e - N_EXPERTS, 0), 0)
    wmap = lambda sb, e, *_: (jnp.minimum(e, N_EXPERTS - 1), 0, 0)
    grid_spec = pltpu.PrefetchScalarGridSpec(
        num_scalar_prefetch=3,
        grid=(n_super, N_EXPERTS + n_piece),
        in_specs=[
            pl.BlockSpec((1, 1, n_list + MOE_M), lambda sb, e, *_: (sb, 0, 0), memory_space=pltpu.SMEM),
            pl.BlockSpec((1, 1, n_list + MOE_M), lambda sb, e, *_: (sb, 0, 0), memory_space=pltpu.SMEM),
            pl.BlockSpec((tb, d // 2), lambda sb, e, *_: (sb, 0), pipeline_mode=pl.Buffered(1)),
            pl.BlockSpec((1, d, 2 * dff), wmap),
            pl.BlockSpec((1, 1, 2 * dff), wmap),
            pl.BlockSpec((1, dff, d), w2map),
            pl.BlockSpec((1, 1, d), wmap),
            pl.BlockSpec((rb, LANES), piece),
            pl.BlockSpec((rb, d), piece),
            pl.BlockSpec((1, 1, d), lambda sb, e, *_: ((sb * tb) // s, 0, 0)),
        ],
        out_specs=pl.BlockSpec((rb, d), piece),
        scratch_shapes=[
            pltpu.VMEM((TOP_K * plane, d // 2), U32),
            pltpu.VMEM((2, MOE_M, d // 2), U32),
            pltpu.VMEM((2, MOE_M, d // 2), U32),
        ],
    )
    out = pl.pallas_call(
        functools.partial(_moe_kernel, tb=tb, rb=rb, table_len=table_len),
        grid_spec=grid_spec,
        out_shape=jax.ShapeDtypeStruct((t, d), F32),
        compiler_params=_cparams(("arbitrary", "arbitrary")),
        name="moe",
    )(first.reshape(-1), nch.astype(I32).reshape(-1), c_base.reshape(-1),
      tok_list.reshape(n_super, 1, n_list + MOE_M), row_list.reshape(n_super, 1, n_list + MOE_M),
      hp.reshape(t, d // 2),
      w1p, b1p, w2b, b2, gate.reshape(t, LANES), xn.reshape(t, d), g2.reshape(bsz, 1, d))
    return out.reshape(bsz, s, d)


MXU_COLS = 256


def _deinterleave_kernel(w_ref, perm_ref, o_ref):
    half = w_ref.shape[2] // 2
    hw = MXU_COLS // 2
    for b in range(w_ref.shape[2] // MXU_COLS):
        blk = w_ref[0, :, b * MXU_COLS:(b + 1) * MXU_COLS].astype(BF16)
        y = jnp.dot(blk, perm_ref[...], preferred_element_type=F32).astype(BF16)
        o_ref[0, :, b * hw:(b + 1) * hw] = y[:, :hw]
        o_ref[0, :, half + b * hw:half + (b + 1) * hw] = y[:, hw:]


def _deinterleave_cast(w1, layer, rows=512):
    depth, n_e, d, two_f = w1.shape
    src = np.concatenate([np.arange(0, MXU_COLS, 2), np.arange(1, MXU_COLS, 2)])
    perm = np.zeros((MXU_COLS, MXU_COLS), np.float32)
    perm[src, np.arange(MXU_COLS)] = 1.0
    return pl.pallas_call(
        _deinterleave_kernel,
        grid=(n_e, d // rows),
        in_specs=[pl.BlockSpec((1, rows, two_f), lambda e, r: (layer * n_e + e, r, 0)),
                  pl.BlockSpec((MXU_COLS, MXU_COLS), lambda e, r: (0, 0))],
        out_specs=pl.BlockSpec((1, rows, two_f), lambda e, r: (e, r, 0)),
        out_shape=jax.ShapeDtypeStruct((n_e, d, two_f), BF16),
        compiler_params=_cparams(("parallel", "parallel")),
        name="w1_deinterleave",
    )(w1.reshape(depth * n_e, d, two_f), jnp.asarray(perm, BF16))


def _deinterleave_bias(b1_l):
    n_e, two_f = b1_l.shape
    return jnp.concatenate([b1_l[:, 0::2], b1_l[:, 1::2]], axis=1).reshape(n_e, 1, two_f)


def kernel(x, c, rel_bias, mod_w, mod_b, norm_mix_w, norm_ffn_w, w_in, conv_w, a_log, dt_bias, gdn_norm_w,
           q_norm_w, k_norm_w, w_out, router_w, router_b, w1, b1, w2, b2):
    depth = mod_w.shape[0]
    bsz, s, d = x.shape
    mod = _modulation(c, mod_w, mod_b)
    tm = min(512, s)
    tb = min(MOE_TB, bsz * s)
    rb = min(MOE_RB, tb)
    n_e, dff = w2.shape[1], w2.shape[2]
    w2b = w2.astype(BF16).reshape(depth * n_e, dff, d)
    for l in range(depth):
        qkv_a, z_a, qb, kb, vb, qi, small = _inproj(
            x, mod[l], norm_mix_w[l], _permute_w_in(w_in[l]), q_norm_w[l], k_norm_w[l], tm)
        y_a = _gdn(qkv_a, z_a, small, conv_w[l], a_log[l], dt_bias[l], gdn_norm_w[l], sb=tm)
        y_b = _dsa(qb, kb, vb, qi, small, rel_bias)
        xn, hp, ridx, gate = _outproj(y_a, y_b, x, mod[l], w_out[l].astype(BF16), norm_ffn_w[l],
                                      router_w[l], router_b[l], tm)
        x = _moe(xn, hp, ridx, gate, mod[l][:, 5], _deinterleave_cast(w1, l), _deinterleave_bias(b1[l]), w2b,
                 b2[l].reshape(n_e, 1, d), tb, rb, layer=l)
    return x
```

```python
import functools
import math

import jax
import jax.numpy as jnp
import numpy as np
from jax import lax
from jax.experimental import pallas as pl
from jax.experimental.pallas import tpu as pltpu

F32 = jnp.float32
BF16 = jnp.bfloat16
I32 = jnp.int32
U32 = jnp.uint32
HIGHEST = lax.Precision.HIGHEST

LANES = 128
SUBLANES = 8
VMEM_LIMIT_BYTES = 56 * 1024 * 1024

CHUNK = 64
HEAD_DIM_A = 128
N_HEADS_A = 4
WIDTH_A = N_HEADS_A * HEAD_DIM_A
CONV_K = 4
HEAD_DIM_B = 64
N_HEADS_B = 8
WIDTH_B = N_HEADS_B * HEAD_DIM_B
IDX_HEADS = 8
IDX_DIM = 64
TOPK_KEYS_MAX = 256
REL_BUCKETS = 32
REL_MAX_DIST = 1024
N_EXPERTS = 32
TOP_K = 4
SWIGLU_ALPHA = 1.702
SWIGLU_LIMIT = 7.0
EPS = 1e-6
NEG_BIG = -1e30

C_QKVA = 0
C_Z = C_QKVA + 3 * WIDTH_A
C_QB = C_Z + WIDTH_A
C_KB = C_QB + WIDTH_B
C_VB = C_KB + WIDTH_B
C_QI = C_VB + WIDTH_B
C_SMALL = C_QI + IDX_HEADS * IDX_DIM
D_IN_PAD = C_SMALL + LANES
S_KIDX = 0
S_B = IDX_DIM
S_A = S_B + N_HEADS_A
S_WIDX = S_A + N_HEADS_A


def _cparams(sem):
    return pltpu.CompilerParams(dimension_semantics=sem, vmem_limit_bytes=VMEM_LIMIT_BYTES)


def _sigmoid(x):
    return jax.nn.sigmoid(x)


def _silu(x):
    return x * _sigmoid(x)


def _softplus(x):
    return jnp.maximum(x, 0.0) + jnp.log(1.0 + jnp.exp(-jnp.abs(x)))


def _mod_kernel(c_ref, w_ref, b_ref, o_ref):
    a = _silu(c_ref[...])
    o_ref[0] = jnp.dot(a, w_ref[0], precision=HIGHEST, preferred_element_type=F32) + b_ref[0]


def _modulation(c, mod_w, mod_b):
    depth, d, n = mod_w.shape
    bsz = c.shape[0]
    rows = -(-bsz // SUBLANES) * SUBLANES
    c_pad = jnp.zeros((rows, d), F32).at[:bsz].set(c)
    tn = 1536
    out = pl.pallas_call(
        _mod_kernel,
        grid=(depth, n // tn),
        in_specs=[
            pl.BlockSpec((rows, d), lambda l, j: (0, 0)),
            pl.BlockSpec((1, d, tn), lambda l, j: (l, 0, j)),
            pl.BlockSpec((1, 1, tn), lambda l, j: (l, 0, j)),
        ],
        out_specs=pl.BlockSpec((1, rows, tn), lambda l, j: (l, 0, j)),
        out_shape=jax.ShapeDtypeStruct((depth, rows, n), F32),
        compiler_params=_cparams(("arbitrary", "arbitrary")),
        name="adaln_mod",
    )(c_pad, mod_w, mod_b.reshape(depth, 1, n))
    return out[:, :bsz].reshape(depth, bsz, 6, d)


def _head_rms(t, group_ref, wn, inv_dim):
    t2 = t * t
    hi = t2.astype(BF16)
    lo = (t2 - hi.astype(F32)).astype(BF16)
    ss = (jnp.dot(hi, group_ref[...], preferred_element_type=F32)
          + jnp.dot(lo, group_ref[...], preferred_element_type=F32))
    return t * lax.rsqrt(ss * inv_dim + EPS) * wn


def _inproj_kernel(x_ref, mod_ref, nw_ref, w_ref, group_ref, qn_ref, kn_ref,
                   qkva_ref, z_ref, qb_ref, kb_ref, vb_ref, qi_ref, small_ref):
    x = x_ref[0]
    ms = jnp.mean(x * x, axis=-1, keepdims=True)
    y = x * lax.rsqrt(ms + EPS) * nw_ref[...]
    h = y * (1.0 + mod_ref[0, 1:2, :]) + mod_ref[0, 0:1, :]
    hb = h.astype(BF16)

    def mm(lo, width):
        return jnp.dot(hb, w_ref[:, lo:lo + width], preferred_element_type=F32)

    qkva_ref[0] = mm(C_QKVA, 3 * WIDTH_A)
    z_ref[0] = mm(C_Z, WIDTH_A)
    q = _head_rms(mm(C_QB, WIDTH_B), group_ref, qn_ref[...], 1.0 / HEAD_DIM_B)
    qb_ref[0] = (q * (HEAD_DIM_B ** -0.5 * LOG2E)).astype(BF16)
    k = _head_rms(mm(C_KB, WIDTH_B), group_ref, kn_ref[...], 1.0 / HEAD_DIM_B)
    kb_ref[0] = k.astype(BF16)
    v = mm(C_VB, WIDTH_B)
    lane = lax.broadcasted_iota(I32, (v.shape[0], LANES), 1)
    tail = (lane == HEAD_DIM_B).astype(F32)
    for p in range(N_HEADS_B // 2):
        pair = v[:, p * LANES:(p + 1) * LANES]
        vb_ref[0, :, (2 * p) * LANES:(2 * p + 1) * LANES] = jnp.where(lane < HEAD_DIM_B, pair, tail).astype(BF16)
        vb_ref[0, :, (2 * p + 1) * LANES:(2 * p + 2) * LANES] = jnp.where(
            lane < HEAD_DIM_B, pltpu.roll(pair, HEAD_DIM_B, axis=1), tail).astype(BF16)
    qi_ref[0] = mm(C_QI, IDX_HEADS * IDX_DIM).astype(BF16)
    small_ref[0] = mm(C_SMALL, LANES)


def _permute_w_in(w_in_l):
    d = w_in_l.shape[0]
    o = 0
    qkva = w_in_l[:, o:o + 3 * WIDTH_A]; o += 3 * WIDTH_A
    z = w_in_l[:, o:o + WIDTH_A]; o += WIDTH_A
    b = w_in_l[:, o:o + N_HEADS_A]; o += N_HEADS_A
    a = w_in_l[:, o:o + N_HEADS_A]; o += N_HEADS_A
    qkvb = w_in_l[:, o:o + 3 * WIDTH_B]; o += 3 * WIDTH_B
    qi = w_in_l[:, o:o + IDX_HEADS * IDX_DIM]; o += IDX_HEADS * IDX_DIM
    ki = w_in_l[:, o:o + IDX_DIM]; o += IDX_DIM
    wi = w_in_l[:, o:o + IDX_HEADS]; o += IDX_HEADS
    pad = jnp.zeros((d, LANES - IDX_DIM - 2 * N_HEADS_A - IDX_HEADS), w_in_l.dtype)
    return jnp.concatenate([qkva, z, qkvb, qi, ki, b, a, wi, pad], axis=1).astype(BF16)


def _group_ones(width, group):
    g = np.arange(width) // group
    return jnp.asarray((g[:, None] == g[None, :]).astype(np.float32), dtype=BF16)


def _inproj(x, mod_l, norm_w, w_perm, q_norm_w, k_norm_w, tm):
    bsz, s, d = x.shape
    f = lambda b, i: (b, i, 0)
    const2 = lambda b, i: (0, 0)
    outs = [
        (3 * WIDTH_A, F32), (WIDTH_A, F32), (WIDTH_B, BF16), (WIDTH_B, BF16), (N_HEADS_B * LANES, BF16),
        (IDX_HEADS * IDX_DIM, BF16), (LANES, F32),
    ]
    return pl.pallas_call(
        _inproj_kernel,
        grid=(bsz, s // tm),
        in_specs=[
            pl.BlockSpec((1, tm, d), f),
            pl.BlockSpec((1, 6, d), lambda b, i: (b, 0, 0)),
            pl.BlockSpec((1, d), const2),
            pl.BlockSpec((d, D_IN_PAD), const2),
            pl.BlockSpec((WIDTH_B, WIDTH_B), const2),
            pl.BlockSpec((1, WIDTH_B), const2),
            pl.BlockSpec((1, WIDTH_B), const2),
        ],
        out_specs=[pl.BlockSpec((1, tm, w), f) for w, _ in outs],
        out_shape=[jax.ShapeDtypeStruct((bsz, s, w), dt) for w, dt in outs],
        compiler_params=_cparams(("parallel", "parallel")),
        name="inproj",
    )(x, mod_l, norm_w.reshape(1, d), w_perm, _group_ones(WIDTH_B, HEAD_DIM_B),
      jnp.tile(q_norm_w, N_HEADS_B).reshape(1, WIDTH_B), jnp.tile(k_norm_w, N_HEADS_B).reshape(1, WIDTH_B))


def _dot_nt(a, b, precision=None):
    return lax.dot_general(a, b, (((1,), (1,)), ((), ())), precision=precision, preferred_element_type=F32)


def _split2(x):
    hi = x.astype(BF16)
    return hi, (x - hi.astype(F32)).astype(BF16)


def _split3(x):
    hi = x.astype(BF16)
    r = x - hi.astype(F32)
    mid = r.astype(BF16)
    return hi, mid, (r - mid.astype(F32)).astype(BF16)


def _mm3(a, b):
    return (jnp.dot(a[0], b[0], preferred_element_type=F32) + jnp.dot(a[0], b[1], preferred_element_type=F32)
            + jnp.dot(a[1], b[0], preferred_element_type=F32))


GDN_PAR = 8


def _gdn_kernel(qkv_ref, z_ref, small_ref, convw_ref, alog_ref, dtb_ref, nw_ref,
                y_ref, xe_ref, u_ref, state_ref, uval_ref, wdec_ref, qg_ref, kdec_ref, attn_ref, egl_ref, *, sb):
    n_chunks = sb // CHUNK
    halo = SUBLANES

    @pl.when(pl.program_id(1) == 0)
    def _():
        xe_ref[0:halo, :] = jnp.zeros((halo, 3 * WIDTH_A), F32)
        state_ref[...] = jnp.zeros_like(state_ref)

    xe_ref[halo:halo + sb, :] = qkv_ref[0]

    rows = 128
    for g in range(3 * WIDTH_A // LANES):
        cs = slice(g * LANES, (g + 1) * LANES)
        for r in range(sb // rows):
            base = halo - (CONV_K - 1) + r * rows
            acc = xe_ref[base:base + rows, cs] * convw_ref[0:1, cs]
            for j in range(1, CONV_K):
                acc = acc + xe_ref[base + j:base + j + rows, cs] * convw_ref[j:j + 1, cs]
            u_ref[r * rows:(r + 1) * rows, cs] = _silu(acc)

    xe_ref[0:halo, :] = xe_ref[sb:sb + halo, :]

    wide = N_HEADS_A * CHUNK
    heads = range(N_HEADS_A)
    ii = lax.broadcasted_iota(I32, (CHUNK, wide), 0)
    jj = lax.broadcasted_iota(I32, (CHUNK, wide), 1) & (CHUNK - 1)
    eye_w = (ii == jj).astype(F32)
    tri = (lax.broadcasted_iota(I32, (CHUNK, CHUNK), 0)
           >= lax.broadcasted_iota(I32, (CHUNK, CHUNK), 1)).astype(F32).astype(BF16)
    shift = int(math.log2(CHUNK))
    bd_mask = ((lax.broadcasted_iota(I32, (wide, wide), 0) >> shift)
               == (lax.broadcasted_iota(I32, (wide, wide), 1) >> shift)).astype(F32)

    bd_mask = bd_mask.astype(BF16)

    def block_diag(parts):
        return tuple(jnp.concatenate([m] * N_HEADS_A, axis=0) * bd_mask for m in parts)

    def prepare(c):
        rs = pl.ds(pl.multiple_of(c * CHUNK, CHUNK), CHUNK)
        qn, kn, v, beta, g_b = [], [], [], [], []
        for h in heads:
            q = u_ref[rs, h * HEAD_DIM_A:(h + 1) * HEAD_DIM_A]
            k = u_ref[rs, WIDTH_A + h * HEAD_DIM_A:WIDTH_A + (h + 1) * HEAD_DIM_A]
            v.append(u_ref[rs, 2 * WIDTH_A + h * HEAD_DIM_A:2 * WIDTH_A + (h + 1) * HEAD_DIM_A])
            qn.append(q * (lax.rsqrt(jnp.sum(q * q, axis=-1, keepdims=True) + EPS) * (HEAD_DIM_A ** -0.5)))
            kn.append(k * lax.rsqrt(jnp.sum(k * k, axis=-1, keepdims=True) + EPS))
            beta.append(_sigmoid(small_ref[0, rs, S_B + h:S_B + h + 1]))
            g = -jnp.exp(alog_ref[0:1, h:h + 1]) * _softplus(small_ref[0, rs, S_A + h:S_A + h + 1]
                                                            + dtb_ref[0:1, h:h + 1])
            g_b.append(jnp.broadcast_to(g, (CHUNK, CHUNK)))
        gc_w = sum(jnp.dot(tri, part, preferred_element_type=F32) for part in _split3(jnp.concatenate(g_b, axis=1)))
        gc_row = jnp.sum(jnp.where(ii == jj, gc_w, 0.0), axis=0, keepdims=True)
        decay_w = jnp.exp(jnp.where(ii >= jj, gc_w - gc_row, NEG_BIG))
        k_beta = [kn[h] * beta[h] for h in heads]
        kk_w = jnp.concatenate([_dot_nt(k_beta[h].astype(BF16), kn[h].astype(BF16)) for h in heads], axis=1)
        a_w = -jnp.where(ii > jj, kk_w * decay_w, 0.0)
        gc = [gc_w[:, h * CHUNK:h * CHUNK + 1] for h in heads]
        egc = [jnp.exp(gc[h]) for h in heads]
        qk = [_dot_nt(qn[h].astype(BF16), kn[h].astype(BF16)) for h in heads]
        for h in heads:
            g_last = gc[h][CHUNK - 1:CHUNK, :]
            qg_ref[c, h] = (qn[h] * egc[h]).astype(BF16)
            kdec_ref[c, h] = kn[h] * jnp.exp(g_last - gc[h])
            attn_ref[c, h] = (qk[h] * decay_w[:, h * CHUNK:(h + 1) * CHUNK]).astype(BF16)
            egl_ref[c, h] = jnp.broadcast_to(jnp.exp(g_last), (SUBLANES, LANES))
        return a_w, [_split2(v[h] * beta[h]) for h in heads], [_split2(k_beta[h] * egc[h]) for h in heads]

    def solve_body(cg, carry):
        group = range(GDN_PAR)
        chunks = [cg * GDN_PAR + i for i in group]
        pre = [prepare(c) for c in chunks]
        t_w = [eye_w + pre[i][0] for i in group]
        p_parts = [_split2(pre[i][0]) for i in group]
        bd = [block_diag(p_parts[i]) for i in group]
        for _ in range(shift - 1):
            prod = [_mm3(p_parts[i], bd[i]) for i in group]
            p_parts = [_split2(prod[i]) for i in group]
            bd = [block_diag(p_parts[i]) for i in group]
            upd = [_mm3(_split2(t_w[i]), bd[i]) for i in group]
            t_w = [t_w[i] + upd[i] for i in group]
        for i in group:
            t_h = [_split2(t_w[i][:, h * CHUNK:(h + 1) * CHUNK]) for h in heads]
            u_val = [_mm3(t_h[h], pre[i][1][h]) for h in heads]
            w_dec = [_mm3(t_h[h], pre[i][2][h]) for h in heads]
            for h in heads:
                uval_ref[chunks[i], h] = u_val[h]
                wdec_ref[chunks[i], h] = w_dec[h].astype(BF16)
        return carry

    lax.fori_loop(0, n_chunks // GDN_PAR, solve_body, 0)

    def scan_body(c, carry):
        rs = pl.ds(pl.multiple_of(c * CHUNK, CHUNK), CHUNK)
        state = [state_ref[h] for h in heads]
        state_b = [s_h.astype(BF16) for s_h in state]
        w_s = [jnp.dot(wdec_ref[c, h], state_b[h], preferred_element_type=F32) for h in heads]
        v_new = [(uval_ref[c, h] - w_s[h]).astype(BF16) for h in heads]
        o = [jnp.dot(qg_ref[c, h], state_b[h], preferred_element_type=F32)
             + jnp.dot(attn_ref[c, h], v_new[h], preferred_element_type=F32) for h in heads]
        for h in heads:
            state_ref[h] = (state[h] * egl_ref[c, h][0:1, 0:1]
                            + jnp.dot(kdec_ref[c, h].T.astype(BF16), v_new[h], preferred_element_type=F32))
        for h in heads:
            hs = slice(h * HEAD_DIM_A, (h + 1) * HEAD_DIM_A)
            on = o[h] * lax.rsqrt(jnp.mean(o[h] * o[h], axis=-1, keepdims=True) + EPS) * nw_ref[...]
            y_ref[0, rs, hs] = (on * _silu(z_ref[0, rs, hs])).astype(BF16)
        return carry

    lax.fori_loop(0, n_chunks, scan_body, 0)


def _gdn(qkv_a, z_a, small, conv_w, a_log, dt_bias, norm_w, sb):
    bsz, s, _ = qkv_a.shape
    n_c = sb // CHUNK
    assert n_c % GDN_PAR == 0 and s % sb == 0
    per_head = (n_c, N_HEADS_A, CHUNK, HEAD_DIM_A)
    f = lambda b, i: (b, i, 0)
    const2 = lambda b, i: (0, 0)
    return pl.pallas_call(
        functools.partial(_gdn_kernel, sb=sb),
        grid=(bsz, s // sb),
        in_specs=[
            pl.BlockSpec((1, sb, 3 * WIDTH_A), f),
            pl.BlockSpec((1, sb, WIDTH_A), f),
            pl.BlockSpec((1, sb, LANES), f),
            pl.BlockSpec((CONV_K, 3 * WIDTH_A), const2),
            pl.BlockSpec((1, N_HEADS_A), const2),
            pl.BlockSpec((1, N_HEADS_A), const2),
            pl.BlockSpec((1, HEAD_DIM_A), const2),
        ],
        out_specs=pl.BlockSpec((1, sb, WIDTH_A), f),
        out_shape=jax.ShapeDtypeStruct((bsz, s, WIDTH_A), BF16),
        scratch_shapes=[
            pltpu.VMEM((sb + SUBLANES, 3 * WIDTH_A), F32),
            pltpu.VMEM((sb, 3 * WIDTH_A), F32),
            pltpu.VMEM((N_HEADS_A, HEAD_DIM_A, HEAD_DIM_A), F32),
            pltpu.VMEM(per_head, F32),
            pltpu.VMEM(per_head, BF16),
            pltpu.VMEM(per_head, BF16),
            pltpu.VMEM(per_head, F32),
            pltpu.VMEM((n_c, N_HEADS_A, CHUNK, CHUNK), BF16),
            pltpu.VMEM((n_c, N_HEADS_A, SUBLANES, LANES), F32),
        ],
        compiler_params=_cparams(("parallel", "arbitrary")),
        name="gdn",
    )(qkv_a, z_a, small, conv_w, a_log.reshape(1, -1), dt_bias.reshape(1, -1), norm_w.reshape(1, -1))


QB = 128
FAR_T = 512
FAR_G = FAR_T // LANES
ATT_W = 256
LOG2E = 1.4426950408889634
NEAR_D = 9
NEAR_MIN = 5
INT_MIN = -2 ** 31


def _t5_bucket_np(rel):
    nb = REL_BUCKETS // 2
    max_exact = nb // 2
    side = np.where(rel > 0, nb, 0)
    n = np.abs(rel)
    nf = np.maximum(n, 1).astype(np.float32)
    large = max_exact + (np.log(nf / np.float32(max_exact)) / np.float32(math.log(REL_MAX_DIST / max_exact))
                         * np.float32(nb - max_exact)).astype(np.int32)
    large = np.minimum(large, nb - 1)
    return (side + np.where(n < max_exact, n, large)).astype(np.int32)


def _near_bucket_table():
    r = np.arange(QB)[:, None]
    c = np.arange(LANES)[None, :]
    return np.stack([_t5_bucket_np(c - r - LANES * d) for d in range(NEAR_D)])


FAR_BUCKET = int(_t5_bucket_np(np.array([-(NEAR_MIN * LANES + 1)]))[0])
assert all(int(b) == FAR_BUCKET for b in _t5_bucket_np(-np.arange((NEAR_MIN + 1) * LANES - (QB - 1), 1 << 20, 997)))


def _sortable_key(score):
    bits = pltpu.bitcast(score + 0.0, I32)
    return bits ^ ((bits >> 31) & 0x7FFFFFFF)


PLANE_G = 32


def _bit_transpose32(words):
    a = list(words)
    mask, j = 0x0000FFFF, 16
    while j:
        k = 0
        while k < 32:
            t = (a[k] ^ lax.shift_right_logical(a[k + j], jnp.int32(j))) & mask
            a[k] = a[k] ^ t
            a[k + j] = a[k + j] ^ lax.shift_left(t, jnp.int32(j))
            k = (k + j + 1) & ~j
        j >>= 1
        mask = (mask ^ (mask << j)) & 0xFFFFFFFF
    return a


def _dsa_kernel(rb_ref, qb_ref, qi_ref, small_ref, kb_ref, va_ref, kidx2_ref, tab_ref,
                y_ref, qis_ref, qs_ref, wb_ref, keys_ref, nbias_ref, thr_ref, jlim_ref,
                m_ref, acc_ref, s_ref, p_ref, peak_ref, planes_ref, need_ref, excess_ref, *, seq, k_sel):
    i = pl.program_id(1)
    lane = lax.broadcasted_iota(I32, (QB, LANES), 1)
    row = lax.broadcasted_iota(I32, (QB, LANES), 0)
    even_f = (lane < HEAD_DIM_B).astype(F32)
    even_b = even_f.astype(BF16)
    odd_b = (1.0 - even_f).astype(BF16)

    @pl.when(i == 0)
    def _():
        nbias_ref[...] = jnp.zeros_like(nbias_ref)
        keys_ref[...] = jnp.full(keys_ref.shape, INT_MIN, I32)

        def d_body(d, c0):
            tab = tab_ref[d]

            def b_body(bk, c1):
                hit = tab == bk
                for h in range(N_HEADS_B):
                    nbias_ref[d * N_HEADS_B + h] = jnp.where(hit, rb_ref[bk, h] * LOG2E,
                                                             nbias_ref[d * N_HEADS_B + h])
                return c1

            return lax.fori_loop(0, REL_BUCKETS, b_body, c0)

        lax.fori_loop(0, NEAR_D, d_body, 0)

    for p in range(N_HEADS_B // 2):
        ps = slice(p * LANES, (p + 1) * LANES)
        qi_pair = qi_ref[0, :, ps]
        qis_ref[(2 * p) * QB:(2 * p + 1) * QB, :] = qi_pair * even_b
        qis_ref[(2 * p + 1) * QB:(2 * p + 2) * QB, :] = qi_pair * odd_b
        q_pair = qb_ref[0, :, ps]
        qs_ref[(2 * p) * QB:(2 * p + 1) * QB, :] = q_pair * even_b
        qs_ref[(2 * p + 1) * QB:(2 * p + 2) * QB, :] = q_pair * odd_b
    w_scale = IDX_HEADS ** -0.5 * IDX_DIM ** -0.5
    for h in range(IDX_HEADS):
        wb_ref[h] = jnp.broadcast_to(small_ref[0, :, S_WIDX + h:S_WIDX + h + 1] * w_scale, (QB, LANES))

    limit = i * QB + CHUNK + jnp.where(row >= CHUNK, CHUNK, 0)

    def score_tile(t):
        for c in range(FAR_T // ATT_W):
            k0 = t * FAR_T + c * ATT_W
            k_part = kidx2_ref[0, pl.ds(pl.multiple_of(k0, ATT_W), ATT_W), :]
            acc = None
            for h in range(IDX_HEADS):
                dots = _dot_nt(qis_ref[h * QB:(h + 1) * QB, :], k_part)
                term = jnp.maximum(dots, 0.0) * jnp.concatenate([wb_ref[h]] * (ATT_W // LANES), axis=1)
                acc = term if acc is None else acc + term
            for gg in range(ATT_W // LANES):
                col = k0 + gg * LANES + lane
                keys_ref[(k0 // LANES) + gg] = jnp.where(
                    col < limit, _sortable_key(acc[:, gg * LANES:(gg + 1) * LANES]), INT_MIN)

    def score_pair_body(u, c0):
        score_tile(2 * u)
        score_tile(2 * u + 1)
        return c0

    def score_body(t, c0):
        score_tile(t)
        return c0

    n_groups = i + 1
    n_tiles = i // FAR_G + 1
    lax.fori_loop(0, n_tiles // 2, score_pair_body, 0)
    lax.fori_loop(2 * (n_tiles // 2), n_tiles, score_body, 0)

    @pl.when(n_tiles % 2 == 1)
    def _():
        for g in range(FAR_G):
            keys_ref[n_tiles * FAR_G + g] = jnp.full((QB, LANES), INT_MIN, I32)

    def count(pred):
        def t_body(t, acc):
            for g in range(2 * FAR_G):
                grp = t * (2 * FAR_G) + g
                acc = acc + jnp.where(pred(keys_ref[grp], grp * LANES + lane), 1, 0)
            return acc

        acc = lax.fori_loop(0, (n_tiles + 1) // 2, t_body, jnp.zeros((QB, LANES), I32))
        return jnp.broadcast_to(jnp.sum(acc, axis=1, keepdims=True), (QB, LANES))

    thr_ref[...] = jnp.full((QB, LANES), INT_MIN, I32)
    jlim_ref[...] = jnp.full((QB, LANES), -1, I32)

    def lane_total(x):
        return jnp.broadcast_to(jnp.sum(x, axis=1, keepdims=True), (QB, LANES))

    @pl.when(n_groups * QB > k_sel)
    def _():
        def transpose_half(half):
            def row_body(rr, c0):
                rows = pl.ds(pl.multiple_of(rr * SUBLANES, SUBLANES), SUBLANES)
                planes = _bit_transpose32([keys_ref[half * PLANE_G + g, rows, :] ^ INT_MIN for g in range(PLANE_G)])
                for b in range(32):
                    planes_ref[half, b, rows, :] = planes[b]
                return c0

            lax.fori_loop(0, QB // SUBLANES, row_body, 0)

        def search(n_half):
            halves = range(n_half)
            row_groups = (pl.ds(0, QB // 2), pl.ds(QB // 2, QB // 2))
            shape = (QB // 2, LANES)

            def total(x):
                return jnp.broadcast_to(jnp.sum(x, axis=1, keepdims=True), shape)

            def digit(step, rows, state):
                r, eq, above = state
                cls = []
                for half in halves:
                    hi = eq[half] & planes_ref[half, 2 * step, rows, :]
                    lo_plane = planes_ref[half, 2 * step + 1, rows, :]
                    zero_hi = eq[half] ^ hi
                    e11 = hi & lo_plane
                    e01 = zero_hi & lo_plane
                    cls.append((e11, hi ^ e11, e01, zero_hi ^ e01))
                n11, n10, n01 = [sum(lax.population_count(cls[h][d]) for h in halves) for d in range(3)]
                c3 = above + n11
                c2 = c3 + n10
                c1 = c2 + n01
                d3, d2, d1 = [total(c) >= k_sel for c in (c3, c2, c1)]
                eq = tuple(jnp.where(d3, cls[h][0], jnp.where(d2, cls[h][1], jnp.where(d1, cls[h][2], cls[h][3])))
                           for h in halves)
                above = jnp.where(d3, above, jnp.where(d2, c3, jnp.where(d1, c2, c1)))
                value = jnp.where(d3, 3, jnp.where(d2, 2, jnp.where(d1, 1, 0)))
                return r | lax.shift_left(value, 30 - 2 * step), eq, above

            def digit_body(step, carry):
                return tuple(digit(step, rows, state) for rows, state in zip(row_groups, carry))

            zero = jnp.zeros(shape, I32)
            full = jnp.full(shape, -1, I32)
            start = (zero, (full,) * n_half, zero)
            for rows, (r, eq, above) in zip(row_groups, lax.fori_loop(0, 16, digit_body, (start, start))):
                thr_ref[rows, :] = r ^ INT_MIN
                need = k_sel - total(above)
                need_ref[rows, :] = need
                excess_ref[rows, :] = total(sum(lax.population_count(eq[h]) for h in halves)) - need

        transpose_half(0)

        @pl.when(n_groups > PLANE_G)
        def _():
            transpose_half(1)
            search(2)

        @pl.when(n_groups <= PLANE_G)
        def _():
            search(1)

        r = thr_ref[...]
        need = need_ref[...]
        excess = excess_ref[...]
        jlim_ref[...] = jnp.where(r == INT_MIN, -1, seq)

        @pl.when(jnp.max(excess) > 0)
        def _():
            def j_body(step, jl):
                cand = jl + lax.shift_left(jnp.int32(1), (seq.bit_length() - 1) - step)
                cnt = count(lambda kt, col: (kt == r) & (col < cand))
                return jnp.where(cnt < need, cand, jl)

            jl = lax.fori_loop(0, seq.bit_length(), j_body, jnp.zeros((QB, LANES), I32))
            jlim_ref[...] = jnp.where(r == INT_MIN, -1, jl)

    m_ref[...] = jnp.full(m_ref.shape, NEG_BIG, F32)
    acc_ref[...] = jnp.zeros_like(acc_ref)

    part_g = ATT_W // LANES

    def tile_mask(g0, groups):
        negm = []
        for g in range(groups):
            kt = keys_ref[g0 + g]
            sel = (kt > thr_ref[...]) | ((kt == thr_ref[...]) & ((g0 + g) * LANES + lane <= jlim_ref[...]))
            negm.append(jnp.where(sel, 0.0, -jnp.inf))
        return negm

    def logits_pair(g0, groups, p, negm, near, buf):
        heads = ((2 * p, slice(0, QB)), (2 * p + 1, slice(QB, 2 * QB)))
        peak = [None, None]
        for c in range(groups // part_g):
            ks = pl.ds(pl.multiple_of((g0 + c * part_g) * LANES, ATT_W), ATT_W)
            s = _dot_nt(qs_ref[2 * p * QB:(2 * p + 2) * QB, :], kb_ref[0, ks, p * LANES:(p + 1) * LANES])
            for gg in range(part_g):
                g = c * part_g + gg
                for n, (h, rows) in enumerate(heads):
                    extra = negm[g]
                    if near:
                        extra = extra + nbias_ref[jnp.clip(i - (g0 + g), 0, NEAR_D - 1) * N_HEADS_B + h]
                    v = s[rows, gg * LANES:(gg + 1) * LANES] + extra
                    s_ref[buf, h, g] = v
                    peak[n] = v if peak[n] is None else jnp.maximum(peak[n], v)
        for n, (h, _) in enumerate(heads):
            peak_ref[buf, h] = peak[n]

    def softmax_head(h, groups, bias_scalar, buf):
        m_old = m_ref[h]
        m_new = jnp.maximum(m_old, jnp.max(peak_ref[buf, h], axis=1, keepdims=True) + bias_scalar)
        shift = jnp.concatenate([m_new - bias_scalar] * part_g, axis=1)
        for c in range(groups // part_g):
            part = jnp.concatenate([s_ref[buf, h, c * part_g + g] for g in range(part_g)], axis=1)
            p_ref[buf, h, :, c * ATT_W:(c + 1) * ATT_W] = jnp.exp2(part - shift).astype(BF16)
        acc_ref[h] = jnp.exp2(m_old - m_new) * acc_ref[h]
        m_ref[h] = m_new

    def pv_head(g0, groups, h, buf):
        ks = pl.ds(pl.multiple_of(g0 * LANES, FAR_T), groups * LANES)
        acc_ref[h] += jnp.dot(p_ref[buf, h, :, :groups * LANES], va_ref[0, ks, h * LANES:(h + 1) * LANES],
                              preferred_element_type=F32)

    n_pair = N_HEADS_B // 2
    n_stage = n_pair + 2

    def stage(t, near, negm, step, buf):
        g0 = t * FAR_G
        if step < n_pair:
            logits_pair(g0, FAR_G, step, negm, near, buf)
        if 1 <= step <= n_pair:
            for h in (2 * step - 2, 2 * step - 1):
                softmax_head(h, FAR_G, 0.0 if near else rb_ref[FAR_BUCKET, h] * LOG2E, buf)
        if step >= 2:
            for h in (2 * step - 4, 2 * step - 3):
                pv_head(g0, FAR_G, h, buf)

    def tile_body(near):
        def body(t, c0):
            negm = tile_mask(t * FAR_G, FAR_G)
            for step in range(n_stage):
                stage(t, near, negm, step, 0)
            return c0

        return body

    def tile_pair_body(near, start):
        def body(u, c0):
            first, second = start + 2 * u, start + 2 * u + 1
            masks = (tile_mask(first * FAR_G, FAR_G), tile_mask(second * FAR_G, FAR_G))
            lag = n_stage // 2
            for slot in range(n_stage + lag):
                if slot < n_stage:
                    stage(first, near, masks[0], slot, 0)
                if lag <= slot:
                    stage(second, near, masks[1], slot - lag, 1)
            return c0

        return body

    far_tiles = jnp.maximum(i - NEAR_MIN, 0) // FAR_G
    far_pairs = far_tiles // 2
    near_pairs = (n_tiles - far_tiles) // 2
    lax.fori_loop(0, far_pairs, tile_pair_body(False, 0), 0)
    lax.fori_loop(2 * far_pairs, far_tiles, tile_body(False), 0)
    lax.fori_loop(0, near_pairs, tile_pair_body(True, far_tiles), 0)
    lax.fori_loop(far_tiles + 2 * near_pairs, n_tiles, tile_body(True), 0)

    def head_out(h):
        a = acc_ref[h]
        return a * (1.0 / a[:, HEAD_DIM_B:HEAD_DIM_B + 1])

    for p in range(N_HEADS_B // 2):
        o_odd = pltpu.roll(head_out(2 * p + 1), HEAD_DIM_B, axis=1)
        y_ref[0, :, p * LANES:(p + 1) * LANES] = jnp.where(lane < HEAD_DIM_B, head_out(2 * p), o_odd).astype(BF16)


def _dsa(qb, kb, va, qi, small, rel_bias):
    bsz, s, _ = qb.shape
    assert s % FAR_T == 0 and s // LANES <= 2 * PLANE_G
    k_sel = min(TOPK_KEYS_MAX, s // 4)
    kidx = small[:, :, S_KIDX:S_KIDX + IDX_DIM].astype(BF16)
    kidx2 = jnp.concatenate([kidx, kidx], axis=-1)
    tab = jnp.asarray(_near_bucket_table())
    blk = lambda b, i: (b, i, 0)
    full = lambda b, i: (b, 0, 0)
    one = pl.Buffered(1)
    return pl.pallas_call(
        functools.partial(_dsa_kernel, seq=s, k_sel=k_sel),
        grid=(bsz, s // QB),
        in_specs=[
            pl.BlockSpec(memory_space=pltpu.SMEM),
            pl.BlockSpec((1, QB, WIDTH_B), blk),
            pl.BlockSpec((1, QB, IDX_HEADS * IDX_DIM), blk),
            pl.BlockSpec((1, QB, LANES), blk),
            pl.BlockSpec((1, s, WIDTH_B), full, pipeline_mode=one),
            pl.BlockSpec((1, s, N_HEADS_B * LANES), full, pipeline_mode=one),
            pl.BlockSpec((1, s, LANES), full, pipeline_mode=one),
            pl.BlockSpec((NEAR_D, QB, LANES), lambda b, i: (0, 0, 0), pipeline_mode=one),
        ],
        out_specs=pl.BlockSpec((1, QB, WIDTH_B), blk),
        out_shape=jax.ShapeDtypeStruct((bsz, s, WIDTH_B), BF16),
        scratch_shapes=[
            pltpu.VMEM((IDX_HEADS * QB, LANES), BF16),
            pltpu.VMEM((N_HEADS_B * QB, LANES), BF16),
            pltpu.VMEM((IDX_HEADS, QB, LANES), F32),
            pltpu.VMEM((2 * PLANE_G, QB, LANES), I32),
            pltpu.VMEM((NEAR_D * N_HEADS_B, QB, LANES), F32),
            pltpu.VMEM((QB, LANES), I32),
            pltpu.VMEM((QB, LANES), I32),
            pltpu.VMEM((N_HEADS_B, QB, LANES), F32),
            pltpu.VMEM((N_HEADS_B, QB, LANES), F32),
            pltpu.VMEM((2, N_HEADS_B, FAR_G, QB, LANES), F32),
            pltpu.VMEM((2, N_HEADS_B, QB, FAR_T), BF16),
            pltpu.VMEM((2, N_HEADS_B, QB, LANES), F32),
            pltpu.VMEM((2, 32, QB, LANES), I32),
            pltpu.VMEM((QB, LANES), I32),
            pltpu.VMEM((QB, LANES), I32),
        ],
        compiler_params=_cparams(("parallel", "arbitrary")),
        name="dsa",
    )(rel_bias, qb, qi, small, kb, va, kidx2, tab)


HALF_MASK = 0xFFFF0000


def _pack_halves(t):
    w = t.shape[1] // 2
    bits = pltpu.bitcast(t.astype(BF16).astype(F32), U32)
    return (bits[:, :w] >> 16) | (bits[:, w:] & jnp.uint32(HALF_MASK))


def _unpack_halves(p):
    lo = pltpu.bitcast(p << 16, F32)
    hi = pltpu.bitcast(p & jnp.uint32(HALF_MASK), F32)
    return jnp.concatenate([lo, hi], axis=1)


def _outproj_kernel(ya_ref, yb_ref, x_ref, mod_ref, wo_ref, nw_ref, rw_ref, rbias_ref,
                    xn_ref, hp_ref, ridx_ref, gate_ref):
    wa = ya_ref.shape[2]
    y = (jnp.dot(ya_ref[0], wo_ref[0:wa, :], preferred_element_type=F32)
         + jnp.dot(yb_ref[0], wo_ref[wa:, :], preferred_element_type=F32))
    xn = x_ref[0] + mod_ref[0, 2:3, :] * y
    xn_ref[0] = xn
    ms = jnp.mean(xn * xn, axis=-1, keepdims=True)
    h = xn * lax.rsqrt(ms + EPS) * nw_ref[...] * (1.0 + mod_ref[0, 4:5, :]) + mod_ref[0, 3:4, :]
    hp_ref[0] = _pack_halves(h)

    logits = jnp.dot(h, rw_ref[...], precision=HIGHEST, preferred_element_type=F32) + rbias_ref[...]
    lane = lax.broadcasted_iota(I32, logits.shape, 1)
    cur = logits
    vals, ridx = [], jnp.zeros(logits.shape, I32)
    for k in range(TOP_K):
        mx = jnp.max(cur, axis=1, keepdims=True)
        am = jnp.min(jnp.where(cur == mx, lane, LANES), axis=1, keepdims=True)
        cur = jnp.where(lane == am, -jnp.inf, cur)
        vals.append(mx)
        ridx = jnp.where(lane == k, am, ridx)
    ex = [jnp.exp(v - vals[0]) for v in vals]
    inv = 1.0 / (ex[0] + ex[1] + ex[2] + ex[3])
    gate = jnp.zeros(logits.shape, F32)
    for k in range(TOP_K):
        gate = jnp.where(lane == k, ex[k] * inv, gate)
    ridx_ref[0] = ridx
    gate_ref[0] = gate


def _outproj(y_a, y_b, x, mod_l, w_out_bf, norm_w, router_w, router_b, tm):
    bsz, s, d = x.shape
    n_e = router_w.shape[1]
    rw = jnp.zeros((d, LANES), F32).at[:, :n_e].set(router_w)
    rbias = jnp.full((1, LANES), NEG_BIG, F32).at[0, :n_e].set(router_b)
    blk = lambda b, i: (b, i, 0)
    const2 = lambda b, i: (0, 0)
    return pl.pallas_call(
        _outproj_kernel,
        grid=(bsz, s // tm),
        in_specs=[
            pl.BlockSpec((1, tm, y_a.shape[2]), blk),
            pl.BlockSpec((1, tm, y_b.shape[2]), blk),
            pl.BlockSpec((1, tm, d), blk),
            pl.BlockSpec((1, 6, d), lambda b, i: (b, 0, 0)),
            pl.BlockSpec((d, d), const2),
            pl.BlockSpec((1, d), const2),
            pl.BlockSpec((d, LANES), const2),
            pl.BlockSpec((1, LANES), const2),
        ],
        out_specs=[pl.BlockSpec((1, tm, d), blk), pl.BlockSpec((1, tm, d // 2), blk),
                   pl.BlockSpec((1, tm, LANES), blk), pl.BlockSpec((1, tm, LANES), blk)],
        out_shape=[jax.ShapeDtypeStruct((bsz, s, d), F32), jax.ShapeDtypeStruct((bsz, s, d // 2), U32),
                   jax.ShapeDtypeStruct((bsz, s, LANES), I32), jax.ShapeDtypeStruct((bsz, s, LANES), F32)],
        compiler_params=_cparams(("parallel", "parallel")),
        name="outproj_router",
    )(y_a, y_b, x, mod_l, w_out_bf, norm_w.reshape(1, d), rw, rbias)


MOE_TB = 2048
MOE_RB = 512
MOE_M = 144


def _moe_kernel(first_ref, nch_ref, cbase_ref, tok_ref, row_ref, hp_ref, w1_ref, b1_ref, w2_ref, b2_ref,
                gate_ref, x_ref, g2_ref, o_ref, slots_ref, xg_ref, yb_ref, *, tb, rb, table_len):
    sb = pl.program_id(0)
    e = pl.program_id(1)
    dff = w2_ref.shape[1]
    plane = tb + SUBLANES
    table = sb * table_len + 1

    def gather(j):
        base = cbase_ref[table + j]
        buf = (j + 2) % 2
        for r in range(MOE_M):
            xg_ref[buf, pl.ds(r, 1), :] = hp_ref[pl.ds(tok_ref[0, 0, base + r], 1), :]

    def scatter(j):
        base = cbase_ref[table + j]
        buf = (j + 2) % 2
        for r in range(MOE_M):
            slots_ref[pl.ds(row_ref[0, 0, base + r], 1), :] = yb_ref[buf, pl.ds(r, 1), :]

    @pl.when(e == 0)
    def _():
        yb_ref[...] = jnp.zeros_like(yb_ref)
        gather(0)

    @pl.when(e < N_EXPERTS)
    def _():
        j0 = first_ref[sb * N_EXPERTS + e]

        def chunk(j, carry):
            buf = j % 2
            xb = _unpack_halves(xg_ref[buf]).astype(BF16)
            gather(j + 1)
            scatter(j - 1)
            u = jnp.dot(xb, w1_ref[0], preferred_element_type=F32) + b1_ref[0]
            glu = jnp.minimum(u[:, :dff], SWIGLU_LIMIT)
            lin = jnp.clip(u[:, dff:], -SWIGLU_LIMIT, SWIGLU_LIMIT)
            act = glu * _sigmoid(SWIGLU_ALPHA * glu) * (lin + 1.0)
            y = jnp.dot(act.astype(BF16), w2_ref[0], preferred_element_type=F32) + b2_ref[0]
            yb_ref[buf] = _pack_halves(y)
            return carry

        lax.fori_loop(j0, j0 + nch_ref[sb * N_EXPERTS + e], chunk, 0)

    @pl.when(e == N_EXPERTS)
    def _():
        n_total = first_ref[sb * N_EXPERTS + N_EXPERTS - 1] + nch_ref[sb * N_EXPERTS + N_EXPERTS - 1]
        scatter(n_total - 1)

    @pl.when(e >= N_EXPERTS)
    def _():
        r0 = pl.multiple_of((e - N_EXPERTS) * rb, rb)
        acc = gate_ref[:, 0:1] * _unpack_halves(slots_ref[pl.ds(r0, rb), :])
        for k in range(1, TOP_K):
            rows = pl.ds(pl.multiple_of(k * plane + r0, SUBLANES), rb)
            acc = acc + gate_ref[:, k:k + 1] * _unpack_halves(slots_ref[rows, :])
        o_ref[...] = x_ref[...] + g2_ref[0] * acc


def _moe(xn, hp, ridx, gate, g2, w1p, b1p, w2b, b2, tb, rb, layer=0):
    bsz, s, d = xn.shape
    t = bsz * s
    n_super = t // tb
    n_piece = tb // rb
    dff = w2b.shape[1]
    w2map = lambda sb, e, *_: (layer * N_EXPERTS + jnp.minimum(e, N_EXPERTS - 1), 0, 0)
    flat_e = ridx[:, :, :TOP_K].reshape(n_super, tb * TOP_K)
    order = jnp.argsort(flat_e, axis=1, stable=True).astype(I32)
    counts = jnp.sum(flat_e[:, :, None] == jnp.arange(N_EXPERTS, dtype=I32)[None, None, :], axis=1).astype(I32)
    offs = (jnp.cumsum(counts, axis=1) - counts).astype(I32)
    n_list = tb * TOP_K
    plane = tb + SUBLANES
    tok_list = jnp.pad(order >> 2, ((0, 0), (0, MOE_M)))
    row_list = jnp.pad((order & (TOP_K - 1)) * plane + (order >> 2), ((0, 0), (0, MOE_M)), constant_values=tb)
    nch = (counts + MOE_M - 1) // MOE_M
    first = (jnp.cumsum(nch, axis=1) - nch).astype(I32)
    table_len = -(-(n_list // MOE_M + N_EXPERTS + 2) // SUBLANES) * SUBLANES
    j = jnp.arange(table_len - 1, dtype=I32)[None, :]
    owner = jnp.sum(j[:, :, None] >= (first + nch)[:, None, :], axis=2)
    owner_c = jnp.minimum(owner, N_EXPERTS - 1)
    within = j - jnp.take_along_axis(first, owner_c, axis=1)
    c_base = jnp.where(owner < N_EXPERTS, jnp.take_along_axis(offs, owner_c, axis=1) + within * MOE_M, n_list)
    c_base = jnp.concatenate([jnp.full((n_super, 1), n_list, I32), c_base.astype(I32)], axis=1)

    piece = lambda sb, e, *_: (sb * n_piece + jnp.maximum(e - N_EXPERTS, 0), 0)
    wmap = lambda sb, e, *_: (jnp.minimum(e, N_EXPERTS - 1), 0, 0)
    grid_spec = pltpu.PrefetchScalarGridSpec(
        num_scalar_prefetch=3,
        grid=(n_super, N_EXPERTS + n_piece),
        in_specs=[
            pl.BlockSpec((1, 1, n_list + MOE_M), lambda sb, e, *_: (sb, 0, 0), memory_space=pltpu.SMEM),
            pl.BlockSpec((1, 1, n_list + MOE_M), lambda sb, e, *_: (sb, 0, 0), memory_space=pltpu.SMEM),
            pl.BlockSpec((tb, d // 2), lambda sb, e, *_: (sb, 0), pipeline_mode=pl.Buffered(1)),
            pl.BlockSpec((1, d, 2 * dff), wmap),
            pl.BlockSpec((1, 1, 2 * dff), wmap),
            pl.BlockSpec((1, dff, d), w2map),
            pl.BlockSpec((1, 1, d), wmap),
            pl.BlockSpec((rb, LANES), piece),
            pl.BlockSpec((rb, d), piece),
            pl.BlockSpec((1, 1, d), lambda sb, e, *_: ((sb * tb) // s, 0, 0)),
        ],
        out_specs=pl.BlockSpec((rb, d), piece),
        scratch_shapes=[
            pltpu.VMEM((TOP_K * plane, d // 2), U32),
            pltpu.VMEM((2, MOE_M, d // 2), U32),
            pltpu.VMEM((2, MOE_M, d // 2), U32),
        ],
    )
    out = pl.pallas_call(
        functools.partial(_moe_kernel, tb=tb, rb=rb, table_len=table_len),
        grid_spec=grid_spec,
        out_shape=jax.ShapeDtypeStruct((t, d), F32),
        compiler_params=_cparams(("arbitrary", "arbitrary")),
        name="moe",
    )(first.reshape(-1), nch.astype(I32).reshape(-1), c_base.reshape(-1),
      tok_list.reshape(n_super, 1, n_list + MOE_M), row_list.reshape(n_super, 1, n_list + MOE_M),
      hp.reshape(t, d // 2),
      w1p, b1p, w2b, b2, gate.reshape(t, LANES), xn.reshape(t, d), g2.reshape(bsz, 1, d))
    return out.reshape(bsz, s, d)


MXU_COLS = 256


def _deinterleave_kernel(w_ref, perm_ref, o_ref):
    half = w_ref.shape[2] // 2
    hw = MXU_COLS // 2
    for b in range(w_ref.shape[2] // MXU_COLS):
        blk = w_ref[0, :, b * MXU_COLS:(b + 1) * MXU_COLS].astype(BF16)
        y = jnp.dot(blk, perm_ref[...], preferred_element_type=F32).astype(BF16)
        o_ref[0, :, b * hw:(b + 1) * hw] = y[:, :hw]
        o_ref[0, :, half + b * hw:half + (b + 1) * hw] = y[:, hw:]


def _deinterleave_cast(w1, layer, rows=512):
    depth, n_e, d, two_f = w1.shape
    src = np.concatenate([np.arange(0, MXU_COLS, 2), np.arange(1, MXU_COLS, 2)])
    perm = np.zeros((MXU_COLS, MXU_COLS), np.float32)
    perm[src, np.arange(MXU_COLS)] = 1.0
    return pl.pallas_call(
        _deinterleave_kernel,
        grid=(n_e, d // rows),
        in_specs=[pl.BlockSpec((1, rows, two_f), lambda e, r: (layer * n_e + e, r, 0)),
                  pl.BlockSpec((MXU_COLS, MXU_COLS), lambda e, r: (0, 0))],
        out_specs=pl.BlockSpec((1, rows, two_f), lambda e, r: (e, r, 0)),
        out_shape=jax.ShapeDtypeStruct((n_e, d, two_f), BF16),
        compiler_params=_cparams(("parallel", "parallel")),
        name="w1_deinterleave",
    )(w1.reshape(depth * n_e, d, two_f), jnp.asarray(perm, BF16))


def _deinterleave_bias(b1_l):
    n_e, two_f = b1_l.shape
    return jnp.concatenate([b1_l[:, 0::2], b1_l[:, 1::2]], axis=1).reshape(n_e, 1, two_f)


def kernel(x, c, rel_bias, mod_w, mod_b, norm_mix_w, norm_ffn_w, w_in, conv_w, a_log, dt_bias, gdn_norm_w,
           q_norm_w, k_norm_w, w_out, router_w, router_b, w1, b1, w2, b2):
    depth = mod_w.shape[0]
    bsz, s, d = x.shape
    mod = _modulation(c, mod_w, mod_b)
    tm = min(512, s)
    tb = min(MOE_TB, bsz * s)
    rb = min(MOE_RB, tb)
    n_e, dff = w2.shape[1], w2.shape[2]
    w2b = w2.astype(BF16).reshape(depth * n_e, dff, d)
    for l in range(depth):
        qkv_a, z_a, qb, kb, vb, qi, small = _inproj(
            x, mod[l], norm_mix_w[l], _permute_w_in(w_in[l]), q_norm_w[l], k_norm_w[l], tm)
        y_a = _gdn(qkv_a, z_a, small, conv_w[l], a_log[l], dt_bias[l], gdn_norm_w[l], sb=tm)
        y_b = _dsa(qb, kb, vb, qi, small, rel_bias)
        xn, hp, ridx, gate = _outproj(y_a, y_b, x, mod[l], w_out[l].astype(BF16), norm_ffn_w[l],
                                      router_w[l], router_b[l], tm)
        x = _moe(xn, hp, ridx, gate, mod[l][:, 5], _deinterleave_cast(w1, l), _deinterleave_bias(b1[l]), w2b,
                 b2[l].reshape(n_e, 1, d), tb, rb, layer=l)
    return x
```

```python
import functools
import math

import jax
import jax.numpy as jnp
import numpy as np
from jax import lax
from jax.experimental import pallas as pl
from jax.experimental.pallas import tpu as pltpu

F32 = jnp.float32
BF16 = jnp.bfloat16
I32 = jnp.int32
U32 = jnp.uint32
HIGHEST = lax.Precision.HIGHEST

LANES = 128
SUBLANES = 8
VMEM_LIMIT_BYTES = 56 * 1024 * 1024

CHUNK = 64
HEAD_DIM_A = 128
N_HEADS_A = 4
WIDTH_A = N_HEADS_A * HEAD_DIM_A
CONV_K = 4
HEAD_DIM_B = 64
N_HEADS_B = 8
WIDTH_B = N_HEADS_B * HEAD_DIM_B
IDX_HEADS = 8
IDX_DIM = 64
TOPK_KEYS_MAX = 256
REL_BUCKETS = 32
REL_MAX_DIST = 1024
N_EXPERTS = 32
TOP_K = 4
SWIGLU_ALPHA = 1.702
SWIGLU_LIMIT = 7.0
EPS = 1e-6
NEG_BIG = -1e30

C_QKVA = 0
C_Z = C_QKVA + 3 * WIDTH_A
C_QB = C_Z + WIDTH_A
C_KB = C_QB + WIDTH_B
C_VB = C_KB + WIDTH_B
C_QI = C_VB + WIDTH_B
C_SMALL = C_QI + IDX_HEADS * IDX_DIM
D_IN_PAD = C_SMALL + LANES
S_KIDX = 0
S_B = IDX_DIM
S_A = S_B + N_HEADS_A
S_WIDX = S_A + N_HEADS_A


def _cparams(sem):
    return pltpu.CompilerParams(dimension_semantics=sem, vmem_limit_bytes=VMEM_LIMIT_BYTES)


def _sigmoid(x):
    return jax.nn.sigmoid(x)


def _silu(x):
    return x * _sigmoid(x)


def _softplus(x):
    return jnp.maximum(x, 0.0) + jnp.log(1.0 + jnp.exp(-jnp.abs(x)))


def _mod_kernel(c_ref, w_ref, b_ref, o_ref):
    a = _silu(c_ref[...])
    o_ref[0] = jnp.dot(a, w_ref[0], precision=HIGHEST, preferred_element_type=F32) + b_ref[0]


def _modulation(c, mod_w, mod_b):
    depth, d, n = mod_w.shape
    bsz = c.shape[0]
    rows = -(-bsz // SUBLANES) * SUBLANES
    c_pad = jnp.zeros((rows, d), F32).at[:bsz].set(c)
    tn = 1536
    out = pl.pallas_call(
        _mod_kernel,
        grid=(depth, n // tn),
        in_specs=[
            pl.BlockSpec((rows, d), lambda l, j: (0, 0)),
            pl.BlockSpec((1, d, tn), lambda l, j: (l, 0, j)),
            pl.BlockSpec((1, 1, tn), lambda l, j: (l, 0, j)),
        ],
        out_specs=pl.BlockSpec((1, rows, tn), lambda l, j: (l, 0, j)),
        out_shape=jax.ShapeDtypeStruct((depth, rows, n), F32),
        compiler_params=_cparams(("arbitrary", "arbitrary")),
        name="adaln_mod",
    )(c_pad, mod_w, mod_b.reshape(depth, 1, n))
    return out[:, :bsz].reshape(depth, bsz, 6, d)


def _head_rms(t, group_ref, wn, inv_dim):
    ss = jnp.dot((t * t).astype(BF16), group_ref[...], preferred_element_type=F32)
    return t * lax.rsqrt(ss * inv_dim + EPS) * wn


def _inproj_kernel(x_ref, mod_ref, nw_ref, w_ref, group_ref, qn_ref, kn_ref,
                   qkva_ref, z_ref, qb_ref, kb_ref, vb_ref, qi_ref, small_ref):
    half = x_ref.shape[1] // 2
    for rows in (slice(0, half), slice(half, 2 * half)):
        _inproj_rows(rows, x_ref, mod_ref, nw_ref, w_ref, group_ref, qn_ref, kn_ref,
                     qkva_ref, z_ref, qb_ref, kb_ref, vb_ref, qi_ref, small_ref)


def _inproj_rows(rows, x_ref, mod_ref, nw_ref, w_ref, group_ref, qn_ref, kn_ref,
                 qkva_ref, z_ref, qb_ref, kb_ref, vb_ref, qi_ref, small_ref):
    x = x_ref[0, rows, :]
    ms = jnp.mean(x * x, axis=-1, keepdims=True)
    y = x * lax.rsqrt(ms + EPS) * nw_ref[...]
    h = y * (1.0 + mod_ref[0, 1:2, :]) + mod_ref[0, 0:1, :]
    hb = h.astype(BF16)

    def mm(lo, width):
        return jnp.dot(hb, w_ref[:, lo:lo + width], preferred_element_type=F32)

    qkva_ref[0, rows, :] = mm(C_QKVA, 3 * WIDTH_A)
    z_ref[0, rows, :] = mm(C_Z, WIDTH_A)
    q = _head_rms(mm(C_QB, WIDTH_B), group_ref, qn_ref[...], 1.0 / HEAD_DIM_B)
    qb_ref[0, rows, :] = (q * (HEAD_DIM_B ** -0.5 * LOG2E)).astype(BF16)
    k = _head_rms(mm(C_KB, WIDTH_B), group_ref, kn_ref[...], 1.0 / HEAD_DIM_B)
    kb_ref[0, rows, :] = k.astype(BF16)
    v = mm(C_VB, WIDTH_B)
    lane = lax.broadcasted_iota(I32, (v.shape[0], LANES), 1)
    tail = (lane == HEAD_DIM_B).astype(F32)
    for p in range(N_HEADS_B // 2):
        pair = v[:, p * LANES:(p + 1) * LANES]
        vb_ref[0, rows, (2 * p) * LANES:(2 * p + 1) * LANES] = jnp.where(lane < HEAD_DIM_B, pair, tail).astype(BF16)
        vb_ref[0, rows, (2 * p + 1) * LANES:(2 * p + 2) * LANES] = jnp.where(
            lane < HEAD_DIM_B, pltpu.roll(pair, HEAD_DIM_B, axis=1), tail).astype(BF16)
    qi_ref[0, rows, :] = mm(C_QI, IDX_HEADS * IDX_DIM).astype(BF16)
    small_ref[0, rows, :] = mm(C_SMALL, LANES)


def _permute_w_in(w_in_l):
    d = w_in_l.shape[0]
    o = 0
    qkva = w_in_l[:, o:o + 3 * WIDTH_A]; o += 3 * WIDTH_A
    z = w_in_l[:, o:o + WIDTH_A]; o += WIDTH_A
    b = w_in_l[:, o:o + N_HEADS_A]; o += N_HEADS_A
    a = w_in_l[:, o:o + N_HEADS_A]; o += N_HEADS_A
    qkvb = w_in_l[:, o:o + 3 * WIDTH_B]; o += 3 * WIDTH_B
    qi = w_in_l[:, o:o + IDX_HEADS * IDX_DIM]; o += IDX_HEADS * IDX_DIM
    ki = w_in_l[:, o:o + IDX_DIM]; o += IDX_DIM
    wi = w_in_l[:, o:o + IDX_HEADS]; o += IDX_HEADS
    pad = jnp.zeros((d, LANES - IDX_DIM - 2 * N_HEADS_A - IDX_HEADS), w_in_l.dtype)
    return jnp.concatenate([qkva, z, qkvb, qi, ki, b, a, wi, pad], axis=1).astype(BF16)


def _group_ones(width, group):
    g = np.arange(width) // group
    return jnp.asarray((g[:, None] == g[None, :]).astype(np.float32), dtype=BF16)


def _inproj(x, mod_l, norm_w, w_perm, q_norm_w, k_norm_w, tm):
    bsz, s, d = x.shape
    f = lambda b, i: (b, i, 0)
    const2 = lambda b, i: (0, 0)
    outs = [
        (3 * WIDTH_A, F32), (WIDTH_A, F32), (WIDTH_B, BF16), (WIDTH_B, BF16), (N_HEADS_B * LANES, BF16),
        (IDX_HEADS * IDX_DIM, BF16), (LANES, F32),
    ]
    return pl.pallas_call(
        _inproj_kernel,
        grid=(bsz, s // tm),
        in_specs=[
            pl.BlockSpec((1, tm, d), f),
            pl.BlockSpec((1, 6, d), lambda b, i: (b, 0, 0)),
            pl.BlockSpec((1, d), const2),
            pl.BlockSpec((d, D_IN_PAD), const2),
            pl.BlockSpec((WIDTH_B, WIDTH_B), const2),
            pl.BlockSpec((1, WIDTH_B), const2),
            pl.BlockSpec((1, WIDTH_B), const2),
        ],
        out_specs=[pl.BlockSpec((1, tm, w), f) for w, _ in outs],
        out_shape=[jax.ShapeDtypeStruct((bsz, s, w), dt) for w, dt in outs],
        compiler_params=_cparams(("parallel", "parallel")),
        name="inproj",
    )(x, mod_l, norm_w.reshape(1, d), w_perm, _group_ones(WIDTH_B, HEAD_DIM_B),
      jnp.tile(q_norm_w, N_HEADS_B).reshape(1, WIDTH_B), jnp.tile(k_norm_w, N_HEADS_B).reshape(1, WIDTH_B))


def _dot_nt(a, b, precision=None):
    return lax.dot_general(a, b, (((1,), (1,)), ((), ())), precision=precision, preferred_element_type=F32)


def _split2(x):
    hi = x.astype(BF16)
    return hi, (x - hi.astype(F32)).astype(BF16)


def _split3(x):
    hi = x.astype(BF16)
    r = x - hi.astype(F32)
    mid = r.astype(BF16)
    return hi, mid, (r - mid.astype(F32)).astype(BF16)


def _mm3(a, b):
    return (jnp.dot(a[0], b[0], preferred_element_type=F32) + jnp.dot(a[0], b[1], preferred_element_type=F32)
            + jnp.dot(a[1], b[0], preferred_element_type=F32))


GDN_PAR = 8


def _gdn_kernel(qkv_ref, z_ref, small_ref, convw_ref, alog_ref, dtb_ref, nw_ref,
                y_ref, xe_ref, u_ref, state_ref, uval_ref, wdec_ref, qg_ref, kdec_ref, attn_ref, egl_ref, *, sb):
    n_chunks = sb // CHUNK
    halo = SUBLANES

    @pl.when(pl.program_id(1) == 0)
    def _():
        xe_ref[0:halo, :] = jnp.zeros((halo, 3 * WIDTH_A), F32)
        state_ref[...] = jnp.zeros_like(state_ref)

    xe_ref[halo:halo + sb, :] = qkv_ref[0]

    rows = 128
    for g in range(3 * WIDTH_A // LANES):
        cs = slice(g * LANES, (g + 1) * LANES)
        for r in range(sb // rows):
            base = halo - (CONV_K - 1) + r * rows
            acc = xe_ref[base:base + rows, cs] * convw_ref[0:1, cs]
            for j in range(1, CONV_K):
                acc = acc + xe_ref[base + j:base + j + rows, cs] * convw_ref[j:j + 1, cs]
            u_ref[r * rows:(r + 1) * rows, cs] = _silu(acc)

    xe_ref[0:halo, :] = xe_ref[sb:sb + halo, :]

    wide = N_HEADS_A * CHUNK
    heads = range(N_HEADS_A)
    ii = lax.broadcasted_iota(I32, (CHUNK, wide), 0)
    jj = lax.broadcasted_iota(I32, (CHUNK, wide), 1) & (CHUNK - 1)
    eye_w = (ii == jj).astype(F32)
    tri = (lax.broadcasted_iota(I32, (CHUNK, CHUNK), 0)
           >= lax.broadcasted_iota(I32, (CHUNK, CHUNK), 1)).astype(F32).astype(BF16)
    shift = int(math.log2(CHUNK))
    bd_mask = ((lax.broadcasted_iota(I32, (wide, wide), 0) >> shift)
               == (lax.broadcasted_iota(I32, (wide, wide), 1) >> shift)).astype(F32)

    bd_mask = bd_mask.astype(BF16)

    def block_diag(parts):
        return tuple(jnp.concatenate([m] * N_HEADS_A, axis=0) * bd_mask for m in parts)

    def prepare(c):
        rs = pl.ds(pl.multiple_of(c * CHUNK, CHUNK), CHUNK)
        qn, kn, v, beta, g_b = [], [], [], [], []
        for h in heads:
            q = u_ref[rs, h * HEAD_DIM_A:(h + 1) * HEAD_DIM_A]
            k = u_ref[rs, WIDTH_A + h * HEAD_DIM_A:WIDTH_A + (h + 1) * HEAD_DIM_A]
            v.append(u_ref[rs, 2 * WIDTH_A + h * HEAD_DIM_A:2 * WIDTH_A + (h + 1) * HEAD_DIM_A])
            qn.append(q * (lax.rsqrt(jnp.sum(q * q, axis=-1, keepdims=True) + EPS) * (HEAD_DIM_A ** -0.5)))
            kn.append(k * lax.rsqrt(jnp.sum(k * k, axis=-1, keepdims=True) + EPS))
            beta.append(_sigmoid(small_ref[0, rs, S_B + h:S_B + h + 1]))
            g = -jnp.exp(alog_ref[0:1, h:h + 1]) * _softplus(small_ref[0, rs, S_A + h:S_A + h + 1]
                                                            + dtb_ref[0:1, h:h + 1])
            g_b.append(jnp.broadcast_to(g, (CHUNK, CHUNK)))
        gc_w = sum(jnp.dot(tri, part, preferred_element_type=F32) for part in _split3(jnp.concatenate(g_b, axis=1)))
        gc_row = jnp.sum(jnp.where(ii == jj, gc_w, 0.0), axis=0, keepdims=True)
        decay_w = jnp.exp(jnp.where(ii >= jj, gc_w - gc_row, NEG_BIG))
        k_beta = [kn[h] * beta[h] for h in heads]
        kk_w = jnp.concatenate([_dot_nt(k_beta[h].astype(BF16), kn[h].astype(BF16)) for h in heads], axis=1)
        a_w = -jnp.where(ii > jj, kk_w * decay_w, 0.0)
        gc = [gc_w[:, h * CHUNK:h * CHUNK + 1] for h in heads]
        egc = [jnp.exp(gc[h]) for h in heads]
        qk = [_dot_nt(qn[h].astype(BF16), kn[h].astype(BF16)) for h in heads]
        for h in heads:
            g_last = gc[h][CHUNK - 1:CHUNK, :]
            qg_ref[c, h] = (qn[h] * egc[h]).astype(BF16)
            kdec_ref[c, h] = kn[h] * jnp.exp(g_last - gc[h])
            attn_ref[c, h] = (qk[h] * decay_w[:, h * CHUNK:(h + 1) * CHUNK]).astype(BF16)
            egl_ref[c, h] = jnp.broadcast_to(jnp.exp(g_last), (SUBLANES, LANES))
        return a_w, [_split2(v[h] * beta[h]) for h in heads], [_split2(k_beta[h] * egc[h]) for h in heads]

    def solve_body(cg, carry):
        group = range(GDN_PAR)
        chunks = [cg * GDN_PAR + i for i in group]
        pre = [prepare(c) for c in chunks]
        t_w = [eye_w + pre[i][0] for i in group]
        p_parts = [_split2(pre[i][0]) for i in group]
        bd = [block_diag(p_parts[i]) for i in group]
        for _ in range(shift - 1):
            prod = [_mm3(p_parts[i], bd[i]) for i in group]
            p_parts = [_split2(prod[i]) for i in group]
            bd = [block_diag(p_parts[i]) for i in group]
            upd = [_mm3(_split2(t_w[i]), bd[i]) for i in group]
            t_w = [t_w[i] + upd[i] for i in group]
        for i in group:
            t_h = [_split2(t_w[i][:, h * CHUNK:(h + 1) * CHUNK]) for h in heads]
            u_val = [_mm3(t_h[h], pre[i][1][h]) for h in heads]
            w_dec = [_mm3(t_h[h], pre[i][2][h]) for h in heads]
            for h in heads:
                uval_ref[chunks[i], h] = u_val[h]
                wdec_ref[chunks[i], h] = w_dec[h].astype(BF16)
        return carry

    lax.fori_loop(0, n_chunks // GDN_PAR, solve_body, 0)

    def scan_body(c, carry):
        rs = pl.ds(pl.multiple_of(c * CHUNK, CHUNK), CHUNK)
        state = [state_ref[h] for h in heads]
        state_b = [s_h.astype(BF16) for s_h in state]
        w_s = [jnp.dot(wdec_ref[c, h], state_b[h], preferred_element_type=F32) for h in heads]
        v_new = [(uval_ref[c, h] - w_s[h]).astype(BF16) for h in heads]
        o = [jnp.dot(qg_ref[c, h], state_b[h], preferred_element_type=F32)
             + jnp.dot(attn_ref[c, h], v_new[h], preferred_element_type=F32) for h in heads]
        for h in heads:
            state_ref[h] = (state[h] * egl_ref[c, h][0:1, 0:1]
                            + jnp.dot(kdec_ref[c, h].T.astype(BF16), v_new[h], preferred_element_type=F32))
        for h in heads:
            hs = slice(h * HEAD_DIM_A, (h + 1) * HEAD_DIM_A)
            on = o[h] * lax.rsqrt(jnp.mean(o[h] * o[h], axis=-1, keepdims=True) + EPS) * nw_ref[...]
            y_ref[0, rs, hs] = (on * _silu(z_ref[0, rs, hs])).astype(BF16)
        return carry

    lax.fori_loop(0, n_chunks, scan_body, 0)


def _gdn(qkv_a, z_a, small, conv_w, a_log, dt_bias, norm_w, sb):
    bsz, s, _ = qkv_a.shape
    n_c = sb // CHUNK
    assert n_c % GDN_PAR == 0 and s % sb == 0
    per_head = (n_c, N_HEADS_A, CHUNK, HEAD_DIM_A)
    f = lambda b, i: (b, i, 0)
    const2 = lambda b, i: (0, 0)
    return pl.pallas_call(
        functools.partial(_gdn_kernel, sb=sb),
        grid=(bsz, s // sb),
        in_specs=[
            pl.BlockSpec((1, sb, 3 * WIDTH_A), f),
            pl.BlockSpec((1, sb, WIDTH_A), f),
            pl.BlockSpec((1, sb, LANES), f),
            pl.BlockSpec((CONV_K, 3 * WIDTH_A), const2),
            pl.BlockSpec((1, N_HEADS_A), const2),
            pl.BlockSpec((1, N_HEADS_A), const2),
            pl.BlockSpec((1, HEAD_DIM_A), const2),
        ],
        out_specs=pl.BlockSpec((1, sb, WIDTH_A), f),
        out_shape=jax.ShapeDtypeStruct((bsz, s, WIDTH_A), BF16),
        scratch_shapes=[
            pltpu.VMEM((sb + SUBLANES, 3 * WIDTH_A), F32),
            pltpu.VMEM((sb, 3 * WIDTH_A), F32),
            pltpu.VMEM((N_HEADS_A, HEAD_DIM_A, HEAD_DIM_A), F32),
            pltpu.VMEM(per_head, F32),
            pltpu.VMEM(per_head, BF16),
            pltpu.VMEM(per_head, BF16),
            pltpu.VMEM(per_head, F32),
            pltpu.VMEM((n_c, N_HEADS_A, CHUNK, CHUNK), BF16),
            pltpu.VMEM((n_c, N_HEADS_A, SUBLANES, LANES), F32),
        ],
        compiler_params=_cparams(("parallel", "arbitrary")),
        name="gdn",
    )(qkv_a, z_a, small, conv_w, a_log.reshape(1, -1), dt_bias.reshape(1, -1), norm_w.reshape(1, -1))


QB = 128
FAR_T = 512
FAR_G = FAR_T // LANES
ATT_W = 256
LOG2E = 1.4426950408889634
NEAR_D = 9
NEAR_MIN = 5
INT_MIN = -2 ** 31


def _t5_bucket_np(rel):
    nb = REL_BUCKETS // 2
    max_exact = nb // 2
    side = np.where(rel > 0, nb, 0)
    n = np.abs(rel)
    nf = np.maximum(n, 1).astype(np.float32)
    large = max_exact + (np.log(nf / np.float32(max_exact)) / np.float32(math.log(REL_MAX_DIST / max_exact))
                         * np.float32(nb - max_exact)).astype(np.int32)
    large = np.minimum(large, nb - 1)
    return (side + np.where(n < max_exact, n, large)).astype(np.int32)


def _near_bucket_table():
    r = np.arange(QB)[:, None]
    c = np.arange(LANES)[None, :]
    return np.stack([_t5_bucket_np(c - r - LANES * d) for d in range(NEAR_D)])


FAR_BUCKET = int(_t5_bucket_np(np.array([-(NEAR_MIN * LANES + 1)]))[0])
assert all(int(b) == FAR_BUCKET for b in _t5_bucket_np(-np.arange((NEAR_MIN + 1) * LANES - (QB - 1), 1 << 20, 997)))


def _sortable_key(score):
    bits = pltpu.bitcast(score + 0.0, I32)
    return bits ^ ((bits >> 31) & 0x7FFFFFFF)


PLANE_G = 32


def _bit_transpose32(words):
    a = list(words)
    mask, j = 0x0000FFFF, 16
    while j:
        k = 0
        while k < 32:
            t = (a[k] ^ lax.shift_right_logical(a[k + j], jnp.int32(j))) & mask
            a[k] = a[k] ^ t
            a[k + j] = a[k + j] ^ lax.shift_left(t, jnp.int32(j))
            k = (k + j + 1) & ~j
        j >>= 1
        mask = (mask ^ (mask << j)) & 0xFFFFFFFF
    return a


def _dsa_kernel(rb_ref, qb_ref, qi_ref, small_ref, kb_ref, va_ref, kidx2_ref, tab_ref,
                y_ref, qis_ref, qs_ref, wb_ref, keys_ref, nbias_ref, thr_ref, jlim_ref,
                m_ref, acc_ref, s_ref, p_ref, peak_ref, planes_ref, need_ref, excess_ref, *, seq, k_sel):
    i = pl.program_id(1)
    lane = lax.broadcasted_iota(I32, (QB, LANES), 1)
    row = lax.broadcasted_iota(I32, (QB, LANES), 0)
    even_f = (lane < HEAD_DIM_B).astype(F32)
    even_b = even_f.astype(BF16)
    odd_b = (1.0 - even_f).astype(BF16)

    @pl.when(i == 0)
    def _():
        nbias_ref[...] = jnp.zeros_like(nbias_ref)
        keys_ref[...] = jnp.full(keys_ref.shape, INT_MIN, I32)

        def d_body(d, c0):
            tab = tab_ref[d]

            def b_body(bk, c1):
                hit = tab == bk
                for h in range(N_HEADS_B):
                    nbias_ref[d * N_HEADS_B + h] = jnp.where(hit, rb_ref[bk, h] * LOG2E,
                                                             nbias_ref[d * N_HEADS_B + h])
                return c1

            return lax.fori_loop(0, REL_BUCKETS, b_body, c0)

        lax.fori_loop(0, NEAR_D, d_body, 0)

    for p in range(N_HEADS_B // 2):
        ps = slice(p * LANES, (p + 1) * LANES)
        qi_pair = qi_ref[0, :, ps]
        qis_ref[(2 * p) * QB:(2 * p + 1) * QB, :] = qi_pair * even_b
        qis_ref[(2 * p + 1) * QB:(2 * p + 2) * QB, :] = qi_pair * odd_b
        q_pair = qb_ref[0, :, ps]
        qs_ref[(2 * p) * QB:(2 * p + 1) * QB, :] = q_pair * even_b
        qs_ref[(2 * p + 1) * QB:(2 * p + 2) * QB, :] = q_pair * odd_b
    w_scale = IDX_HEADS ** -0.5 * IDX_DIM ** -0.5
    for h in range(IDX_HEADS):
        wb_ref[h] = jnp.broadcast_to(small_ref[0, :, S_WIDX + h:S_WIDX + h + 1] * w_scale, (QB, LANES))

    limit = i * QB + CHUNK + jnp.where(row >= CHUNK, CHUNK, 0)

    def score_tile(t):
        for c in range(FAR_T // ATT_W):
            k0 = t * FAR_T + c * ATT_W
            k_part = kidx2_ref[0, pl.ds(pl.multiple_of(k0, ATT_W), ATT_W), :]
            acc = None
            for h in range(IDX_HEADS):
                dots = _dot_nt(qis_ref[h * QB:(h + 1) * QB, :], k_part)
                term = jnp.maximum(dots, 0.0) * jnp.concatenate([wb_ref[h]] * (ATT_W // LANES), axis=1)
                acc = term if acc is None else acc + term
            for gg in range(ATT_W // LANES):
                col = k0 + gg * LANES + lane
                keys_ref[(k0 // LANES) + gg] = jnp.where(
                    col < limit, _sortable_key(acc[:, gg * LANES:(gg + 1) * LANES]), INT_MIN)

    def score_pair_body(u, c0):
        score_tile(2 * u)
        score_tile(2 * u + 1)
        return c0

    def score_body(t, c0):
        score_tile(t)
        return c0

    n_groups = i + 1
    n_tiles = i // FAR_G + 1
    lax.fori_loop(0, n_tiles // 2, score_pair_body, 0)
    lax.fori_loop(2 * (n_tiles // 2), n_tiles, score_body, 0)

    @pl.when(n_tiles % 2 == 1)
    def _():
        for g in range(FAR_G):
            keys_ref[n_tiles * FAR_G + g] = jnp.full((QB, LANES), INT_MIN, I32)

    def count(pred):
        def t_body(t, acc):
            for g in range(2 * FAR_G):
                grp = t * (2 * FAR_G) + g
                acc = acc + jnp.where(pred(keys_ref[grp], grp * LANES + lane), 1, 0)
            return acc

        acc = lax.fori_loop(0, (n_tiles + 1) // 2, t_body, jnp.zeros((QB, LANES), I32))
        return jnp.broadcast_to(jnp.sum(acc, axis=1, keepdims=True), (QB, LANES))

    thr_ref[...] = jnp.full((QB, LANES), INT_MIN, I32)
    jlim_ref[...] = jnp.full((QB, LANES), -1, I32)

    def lane_total(x):
        return jnp.broadcast_to(jnp.sum(x, axis=1, keepdims=True), (QB, LANES))

    @pl.when(n_groups * QB > k_sel)
    def _():
        def transpose_half(half):
            def row_body(rr, c0):
                rows = pl.ds(pl.multiple_of(rr * SUBLANES, SUBLANES), SUBLANES)
                planes = _bit_transpose32([keys_ref[half * PLANE_G + g, rows, :] ^ INT_MIN for g in range(PLANE_G)])
                for b in range(32):
                    planes_ref[half, b, rows, :] = planes[b]
                return c0

            lax.fori_loop(0, QB // SUBLANES, row_body, 0)

        def search(n_half):
            halves = range(n_half)
            row_groups = (pl.ds(0, QB // 2), pl.ds(QB // 2, QB // 2))
            shape = (QB // 2, LANES)

            def total(x):
                return jnp.broadcast_to(jnp.sum(x, axis=1, keepdims=True), shape)

            def digit(step, rows, state):
                r, eq, above = state
                cls = []
                for half in halves:
                    hi = eq[half] & planes_ref[half, 2 * step, rows, :]
                    lo_plane = planes_ref[half, 2 * step + 1, rows, :]
                    zero_hi = eq[half] ^ hi
                    e11 = hi & lo_plane
                    e01 = zero_hi & lo_plane
                    cls.append((e11, hi ^ e11, e01, zero_hi ^ e01))
                n11, n10, n01 = [sum(lax.population_count(cls[h][d]) for h in halves) for d in range(3)]
                c3 = above + n11
                c2 = c3 + n10
                c1 = c2 + n01
                d3, d2, d1 = [total(c) >= k_sel for c in (c3, c2, c1)]
                eq = tuple(jnp.where(d3, cls[h][0], jnp.where(d2, cls[h][1], jnp.where(d1, cls[h][2], cls[h][3])))
                           for h in halves)
                above = jnp.where(d3, above, jnp.where(d2, c3, jnp.where(d1, c2, c1)))
                value = jnp.where(d3, 3, jnp.where(d2, 2, jnp.where(d1, 1, 0)))
                return r | lax.shift_left(value, 30 - 2 * step), eq, above

            def digit_body(step, carry):
                return tuple(digit(step, rows, state) for rows, state in zip(row_groups, carry))

            zero = jnp.zeros(shape, I32)
            full = jnp.full(shape, -1, I32)
            start = (zero, (full,) * n_half, zero)
            for rows, (r, eq, above) in zip(row_groups, lax.fori_loop(0, 16, digit_body, (start, start))):
                thr_ref[rows, :] = r ^ INT_MIN
                need = k_sel - total(above)
                need_ref[rows, :] = need
                excess_ref[rows, :] = total(sum(lax.population_count(eq[h]) for h in halves)) - need

        transpose_half(0)

        @pl.when(n_groups > PLANE_G)
        def _():
            transpose_half(1)
            search(2)

        @pl.when(n_groups <= PLANE_G)
        def _():
            search(1)

        r = thr_ref[...]
        need = need_ref[...]
        excess = excess_ref[...]
        jlim_ref[...] = jnp.where(r == INT_MIN, -1, seq)

        @pl.when(jnp.max(excess) > 0)
        def _():
            def j_body(step, jl):
                cand = jl + lax.shift_left(jnp.int32(1), (seq.bit_length() - 1) - step)
                cnt = count(lambda kt, col: (kt == r) & (col < cand))
                return jnp.where(cnt < need, cand, jl)

            jl = lax.fori_loop(0, seq.bit_length(), j_body, jnp.zeros((QB, LANES), I32))
            jlim_ref[...] = jnp.where(r == INT_MIN, -1, jl)

    m_ref[...] = jnp.full(m_ref.shape, NEG_BIG, F32)
    acc_ref[...] = jnp.zeros_like(acc_ref)

    part_g = ATT_W // LANES

    def tile_mask(g0, groups):
        negm = []
        for g in range(groups):
            kt = keys_ref[g0 + g]
            sel = (kt > thr_ref[...]) | ((kt == thr_ref[...]) & ((g0 + g) * LANES + lane <= jlim_ref[...]))
            negm.append(jnp.where(sel, 0.0, -jnp.inf))
        return negm

    def logits_pair(g0, groups, p, negm, near, buf):
        heads = ((2 * p, slice(0, QB)), (2 * p + 1, slice(QB, 2 * QB)))
        peak = [None, None]
        for c in range(groups // part_g):
            ks = pl.ds(pl.multiple_of((g0 + c * part_g) * LANES, ATT_W), ATT_W)
            s = _dot_nt(qs_ref[2 * p * QB:(2 * p + 2) * QB, :], kb_ref[0, ks, p * LANES:(p + 1) * LANES])
            for gg in range(part_g):
                g = c * part_g + gg
                for n, (h, rows) in enumerate(heads):
                    extra = negm[g]
                    if near:
                        extra = extra + nbias_ref[jnp.clip(i - (g0 + g), 0, NEAR_D - 1) * N_HEADS_B + h]
                    v = s[rows, gg * LANES:(gg + 1) * LANES] + extra
                    s_ref[buf, h, g] = v
                    peak[n] = v if peak[n] is None else jnp.maximum(peak[n], v)
        for n, (h, _) in enumerate(heads):
            peak_ref[buf, h] = peak[n]

    def softmax_head(h, groups, bias_scalar, buf):
        m_old = m_ref[h]
        m_new = jnp.maximum(m_old, jnp.max(peak_ref[buf, h], axis=1, keepdims=True) + bias_scalar)
        shift = jnp.concatenate([m_new - bias_scalar] * part_g, axis=1)
        for c in range(groups // part_g):
            part = jnp.concatenate([s_ref[buf, h, c * part_g + g] for g in range(part_g)], axis=1)
            p_ref[buf, h, :, c * ATT_W:(c + 1) * ATT_W] = jnp.exp2(part - shift).astype(BF16)
        acc_ref[h] = jnp.exp2(m_old - m_new) * acc_ref[h]
        m_ref[h] = m_new

    def pv_head(g0, groups, h, buf):
        ks = pl.ds(pl.multiple_of(g0 * LANES, FAR_T), groups * LANES)
        acc_ref[h] += jnp.dot(p_ref[buf, h, :, :groups * LANES], va_ref[0, ks, h * LANES:(h + 1) * LANES],
                              preferred_element_type=F32)

    n_pair = N_HEADS_B // 2
    n_stage = n_pair + 2

    def stage(t, near, negm, step, buf):
        g0 = t * FAR_G
        if step < n_pair:
            logits_pair(g0, FAR_G, step, negm, near, buf)
        if 1 <= step <= n_pair:
            for h in (2 * step - 2, 2 * step - 1):
                softmax_head(h, FAR_G, 0.0 if near else rb_ref[FAR_BUCKET, h] * LOG2E, buf)
        if step >= 2:
            for h in (2 * step - 4, 2 * step - 3):
                pv_head(g0, FAR_G, h, buf)

    def tile_body(near):
        def body(t, c0):
            negm = tile_mask(t * FAR_G, FAR_G)
            for step in range(n_stage):
                stage(t, near, negm, step, 0)
            return c0

        return body

    def tile_pair_body(near, start):
        def body(u, c0):
            first, second = start + 2 * u, start + 2 * u + 1
            masks = (tile_mask(first * FAR_G, FAR_G), tile_mask(second * FAR_G, FAR_G))
            lag = n_stage // 2
            for slot in range(n_stage + lag):
                if slot < n_stage:
                    stage(first, near, masks[0], slot, 0)
                if lag <= slot:
                    stage(second, near, masks[1], slot - lag, 1)
            return c0

        return body

    far_tiles = jnp.maximum(i - NEAR_MIN, 0) // FAR_G
    far_pairs = far_tiles // 2
    near_pairs = (n_tiles - far_tiles) // 2
    lax.fori_loop(0, far_pairs, tile_pair_body(False, 0), 0)
    lax.fori_loop(2 * far_pairs, far_tiles, tile_body(False), 0)
    lax.fori_loop(0, near_pairs, tile_pair_body(True, far_tiles), 0)
    lax.fori_loop(far_tiles + 2 * near_pairs, n_tiles, tile_body(True), 0)

    def head_out(h):
        a = acc_ref[h]
        return a * (1.0 / a[:, HEAD_DIM_B:HEAD_DIM_B + 1])

    for p in range(N_HEADS_B // 2):
        o_odd = pltpu.roll(head_out(2 * p + 1), HEAD_DIM_B, axis=1)
        y_ref[0, :, p * LANES:(p + 1) * LANES] = jnp.where(lane < HEAD_DIM_B, head_out(2 * p), o_odd).astype(BF16)


def _dsa(qb, kb, va, qi, small, rel_bias):
    bsz, s, _ = qb.shape
    assert s % FAR_T == 0 and s // LANES <= 2 * PLANE_G
    k_sel = min(TOPK_KEYS_MAX, s // 4)
    kidx = small[:, :, S_KIDX:S_KIDX + IDX_DIM].astype(BF16)
    kidx2 = jnp.concatenate([kidx, kidx], axis=-1)
    tab = jnp.asarray(_near_bucket_table())
    blk = lambda b, i: (b, i, 0)
    full = lambda b, i: (b, 0, 0)
    one = pl.Buffered(1)
    return pl.pallas_call(
        functools.partial(_dsa_kernel, seq=s, k_sel=k_sel),
        grid=(bsz, s // QB),
        in_specs=[
            pl.BlockSpec(memory_space=pltpu.SMEM),
            pl.BlockSpec((1, QB, WIDTH_B), blk),
            pl.BlockSpec((1, QB, IDX_HEADS * IDX_DIM), blk),
            pl.BlockSpec((1, QB, LANES), blk),
            pl.BlockSpec((1, s, WIDTH_B), full, pipeline_mode=one),
            pl.BlockSpec((1, s, N_HEADS_B * LANES), full, pipeline_mode=one),
            pl.BlockSpec((1, s, LANES), full, pipeline_mode=one),
            pl.BlockSpec((NEAR_D, QB, LANES), lambda b, i: (0, 0, 0), pipeline_mode=one),
        ],
        out_specs=pl.BlockSpec((1, QB, WIDTH_B), blk),
        out_shape=jax.ShapeDtypeStruct((bsz, s, WIDTH_B), BF16),
        scratch_shapes=[
            pltpu.VMEM((IDX_HEADS * QB, LANES), BF16),
            pltpu.VMEM((N_HEADS_B * QB, LANES), BF16),
            pltpu.VMEM((IDX_HEADS, QB, LANES), F32),
            pltpu.VMEM((2 * PLANE_G, QB, LANES), I32),
            pltpu.VMEM((NEAR_D * N_HEADS_B, QB, LANES), F32),
            pltpu.VMEM((QB, LANES), I32),
            pltpu.VMEM((QB, LANES), I32),
            pltpu.VMEM((N_HEADS_B, QB, LANES), F32),
            pltpu.VMEM((N_HEADS_B, QB, LANES), F32),
            pltpu.VMEM((2, N_HEADS_B, FAR_G, QB, LANES), F32),
            pltpu.VMEM((2, N_HEADS_B, QB, FAR_T), BF16),
            pltpu.VMEM((2, N_HEADS_B, QB, LANES), F32),
            pltpu.VMEM((2, 32, QB, LANES), I32),
            pltpu.VMEM((QB, LANES), I32),
            pltpu.VMEM((QB, LANES), I32),
        ],
        compiler_params=_cparams(("parallel", "arbitrary")),
        name="dsa",
    )(rel_bias, qb, qi, small, kb, va, kidx2, tab)


HALF_MASK = 0xFFFF0000


def _pack_halves(t):
    w = t.shape[1] // 2
    bits = pltpu.bitcast(t.astype(BF16).astype(F32), U32)
    return (bits[:, :w] >> 16) | (bits[:, w:] & jnp.uint32(HALF_MASK))


def _unpack_halves(p):
    lo = pltpu.bitcast(p << 16, F32)
    hi = pltpu.bitcast(p & jnp.uint32(HALF_MASK), F32)
    return jnp.concatenate([lo, hi], axis=1)


def _outproj_kernel(ya_ref, yb_ref, x_ref, mod_ref, wo_ref, nw_ref, rw_ref, rbias_ref,
                    xn_ref, hp_ref, ridx_ref, gate_ref):
    wa = ya_ref.shape[2]
    y = (jnp.dot(ya_ref[0], wo_ref[0:wa, :], preferred_element_type=F32)
         + jnp.dot(yb_ref[0], wo_ref[wa:, :], preferred_element_type=F32))
    xn = x_ref[0] + mod_ref[0, 2:3, :] * y
    xn_ref[0] = xn
    ms = jnp.mean(xn * xn, axis=-1, keepdims=True)
    h = xn * lax.rsqrt(ms + EPS) * nw_ref[...] * (1.0 + mod_ref[0, 4:5, :]) + mod_ref[0, 3:4, :]
    hp_ref[0] = _pack_halves(h)

    logits = jnp.dot(h, rw_ref[...], precision=HIGHEST, preferred_element_type=F32) + rbias_ref[...]
    lane = lax.broadcasted_iota(I32, logits.shape, 1)
    cur = logits
    vals, ridx = [], jnp.zeros(logits.shape, I32)
    for k in range(TOP_K):
        mx = jnp.max(cur, axis=1, keepdims=True)
        am = jnp.min(jnp.where(cur == mx, lane, LANES), axis=1, keepdims=True)
        cur = jnp.where(lane == am, -jnp.inf, cur)
        vals.append(mx)
        ridx = jnp.where(lane == k, am, ridx)
    ex = [jnp.exp(v - vals[0]) for v in vals]
    inv = 1.0 / (ex[0] + ex[1] + ex[2] + ex[3])
    gate = jnp.zeros(logits.shape, F32)
    for k in range(TOP_K):
        gate = jnp.where(lane == k, ex[k] * inv, gate)
    ridx_ref[0] = ridx
    gate_ref[0] = gate


def _outproj(y_a, y_b, x, mod_l, w_out_bf, norm_w, router_w, router_b, tm):
    bsz, s, d = x.shape
    n_e = router_w.shape[1]
    rw = jnp.zeros((d, LANES), F32).at[:, :n_e].set(router_w)
    rbias = jnp.full((1, LANES), NEG_BIG, F32).at[0, :n_e].set(router_b)
    blk = lambda b, i: (b, i, 0)
    const2 = lambda b, i: (0, 0)
    return pl.pallas_call(
        _outproj_kernel,
        grid=(bsz, s // tm),
        in_specs=[
            pl.BlockSpec((1, tm, y_a.shape[2]), blk),
            pl.BlockSpec((1, tm, y_b.shape[2]), blk),
            pl.BlockSpec((1, tm, d), blk),
            pl.BlockSpec((1, 6, d), lambda b, i: (b, 0, 0)),
            pl.BlockSpec((d, d), const2),
            pl.BlockSpec((1, d), const2),
            pl.BlockSpec((d, LANES), const2),
            pl.BlockSpec((1, LANES), const2),
        ],
        out_specs=[pl.BlockSpec((1, tm, d), blk), pl.BlockSpec((1, tm, d // 2), blk),
                   pl.BlockSpec((1, tm, LANES), blk), pl.BlockSpec((1, tm, LANES), blk)],
        out_shape=[jax.ShapeDtypeStruct((bsz, s, d), F32), jax.ShapeDtypeStruct((bsz, s, d // 2), U32),
                   jax.ShapeDtypeStruct((bsz, s, LANES), I32), jax.ShapeDtypeStruct((bsz, s, LANES), F32)],
        compiler_params=_cparams(("parallel", "parallel")),
        name="outproj_router",
    )(y_a, y_b, x, mod_l, w_out_bf, norm_w.reshape(1, d), rw, rbias)


MOE_TB = 2048
MOE_RB = 512
MOE_M = 144


def _moe_kernel(first_ref, nch_ref, cbase_ref, tok_ref, row_ref, hp_ref, w1_ref, b1_ref, w2_ref, b2_ref,
                gate_ref, x_ref, g2_ref, o_ref, slots_ref, xg_ref, yb_ref, *, tb, rb, table_len):
    sb = pl.program_id(0)
    e = pl.program_id(1)
    dff = w2_ref.shape[1]
    plane = tb + SUBLANES
    table = sb * table_len + 1

    def gather(j):
        base = cbase_ref[table + j]
        buf = (j + 2) % 2
        for r in range(MOE_M):
            xg_ref[buf, pl.ds(r, 1), :] = hp_ref[pl.ds(tok_ref[0, 0, base + r], 1), :]

    def scatter(j):
        base = cbase_ref[table + j]
        buf = (j + 2) % 2
        for r in range(MOE_M):
            slots_ref[pl.ds(row_ref[0, 0, base + r], 1), :] = yb_ref[buf, pl.ds(r, 1), :]

    @pl.when(e == 0)
    def _():
        yb_ref[...] = jnp.zeros_like(yb_ref)
        gather(0)

    @pl.when(e < N_EXPERTS)
    def _():
        j0 = first_ref[sb * N_EXPERTS + e]

        def chunk(j, carry):
            buf = j % 2
            xb = _unpack_halves(xg_ref[buf]).astype(BF16)
            gather(j + 1)
            scatter(j - 1)
            u = jnp.dot(xb, w1_ref[0], preferred_element_type=F32) + b1_ref[0]
            glu = jnp.minimum(u[:, :dff], SWIGLU_LIMIT)
            lin = jnp.clip(u[:, dff:], -SWIGLU_LIMIT, SWIGLU_LIMIT)
            act = glu * _sigmoid(SWIGLU_ALPHA * glu) * (lin + 1.0)
            y = jnp.dot(act.astype(BF16), w2_ref[0], preferred_element_type=F32) + b2_ref[0]
            yb_ref[buf] = _pack_halves(y)
            return carry

        lax.fori_loop(j0, j0 + nch_ref[sb * N_EXPERTS + e], chunk, 0)

    @pl.when(e == N_EXPERTS)
    def _():
        n_total = first_ref[sb * N_EXPERTS + N_EXPERTS - 1] + nch_ref[sb * N_EXPERTS + N_EXPERTS - 1]
        scatter(n_total - 1)

    @pl.when(e >= N_EXPERTS)
    def _():
        r0 = pl.multiple_of((e - N_EXPERTS) * rb, rb)
        acc = gate_ref[:, 0:1] * _unpack_halves(slots_ref[pl.ds(r0, rb), :])
        for k in range(1, TOP_K):
            rows = pl.ds(pl.multiple_of(k * plane + r0, SUBLANES), rb)
            acc = acc + gate_ref[:, k:k + 1] * _unpack_halves(slots_ref[rows, :])
        o_ref[...] = x_ref[...] + g2_ref[0] * acc


def _moe(xn, hp, ridx, gate, g2, w1p, b1p, w2b, b2, tb, rb, layer=0):
    bsz, s, d = xn.shape
    t = bsz * s
    n_super = t // tb
    n_piece = tb // rb
    dff = w2b.shape[1]
    w2map = lambda sb, e, *_: (layer * N_EXPERTS + jnp.minimum(e, N_EXPERTS - 1), 0, 0)
    flat_e = ridx[:, :, :TOP_K].reshape(n_super, tb * TOP_K)
    order = jnp.argsort(flat_e, axis=1, stable=True).astype(I32)
    counts = jnp.sum(flat_e[:, :, None] == jnp.arange(N_EXPERTS, dtype=I32)[None, None, :], axis=1).astype(I32)
    offs = (jnp.cumsum(counts, axis=1) - counts).astype(I32)
    n_list = tb * TOP_K
    plane = tb + SUBLANES
    tok_list = jnp.pad(order >> 2, ((0, 0), (0, MOE_M)))
    row_list = jnp.pad((order & (TOP_K - 1)) * plane + (order >> 2), ((0, 0), (0, MOE_M)), constant_values=tb)
    nch = (counts + MOE_M - 1) // MOE_M
    first = (jnp.cumsum(nch, axis=1) - nch).astype(I32)
    table_len = -(-(n_list // MOE_M + N_EXPERTS + 2) // SUBLANES) * SUBLANES
    j = jnp.arange(table_len - 1, dtype=I32)[None, :]
    owner = jnp.sum(j[:, :, None] >= (first + nch)[:, None, :], axis=2)
    owner_c = jnp.minimum(owner, N_EXPERTS - 1)
    within = j - jnp.take_along_axis(first, owner_c, axis=1)
    c_base = jnp.where(owner < N_EXPERTS, jnp.take_along_axis(offs, owner_c, axis=1) + within * MOE_M, n_list)
    c_base = jnp.concatenate([jnp.full((n_super, 1), n_list, I32), c_base.astype(I32)], axis=1)

    piece = lambda sb, e, *_: (sb * n_piece + jnp.maximum(e - N_EXPERTS, 0), 0)
    wmap = lambda sb, e, *_: (jnp.minimum(e, N_EXPERTS - 1), 0, 0)
    grid_spec = pltpu.PrefetchScalarGridSpec(
        num_scalar_prefetch=3,
        grid=(n_super, N_EXPERTS + n_piece),
        in_specs=[
            pl.BlockSpec((1, 1, n_list + MOE_M), lambda sb, e, *_: (sb, 0, 0), memory_space=pltpu.SMEM),
            pl.BlockSpec((1, 1, n_list + MOE_M), lambda sb, e, *_: (sb, 0, 0), memory_space=pltpu.SMEM),
            pl.BlockSpec((tb, d // 2), lambda sb, e, *_: (sb, 0), pipeline_mode=pl.Buffered(1)),
            pl.BlockSpec((1, d, 2 * dff), wmap),
            pl.BlockSpec((1, 1, 2 * dff), wmap),
            pl.BlockSpec((1, dff, d), w2map),
            pl.BlockSpec((1, 1, d), wmap),
            pl.BlockSpec((rb, LANES), piece),
            pl.BlockSpec((rb, d), piece),
            pl.BlockSpec((1, 1, d), lambda sb, e, *_: ((sb * tb) // s, 0, 0)),
        ],
        out_specs=pl.BlockSpec((rb, d), piece),
        scratch_shapes=[
            pltpu.VMEM((TOP_K * plane, d // 2), U32),
            pltpu.VMEM((2, MOE_M, d // 2), U32),
            pltpu.VMEM((2, MOE_M, d // 2), U32),
        ],
    )
    out = pl.pallas_call(
        functools.partial(_moe_kernel, tb=tb, rb=rb, table_len=table_len),
        grid_spec=grid_spec,
        out_shape=jax.ShapeDtypeStruct((t, d), F32),
        compiler_params=_cparams(("arbitrary", "arbitrary")),
        name="moe",
    )(first.reshape(-1), nch.astype(I32).reshape(-1), c_base.reshape(-1),
      tok_list.reshape(n_super, 1, n_list + MOE_M), row_list.reshape(n_super, 1, n_list + MOE_M),
      hp.reshape(t, d // 2),
      w1p, b1p, w2b, b2, gate.reshape(t, LANES), xn.reshape(t, d), g2.reshape(bsz, 1, d))
    return out.reshape(bsz, s, d)


MXU_COLS = 256


def _deinterleave_kernel(w_ref, perm_ref, o_ref):
    half = w_ref.shape[2] // 2
    hw = MXU_COLS // 2
    for b in range(w_ref.shape[2] // MXU_COLS):
        blk = w_ref[0, :, b * MXU_COLS:(b + 1) * MXU_COLS].astype(BF16)
        y = jnp.dot(blk, perm_ref[...], preferred_element_type=F32).astype(BF16)
        o_ref[0, :, b * hw:(b + 1) * hw] = y[:, :hw]
        o_ref[0, :, half + b * hw:half + (b + 1) * hw] = y[:, hw:]


def _deinterleave_cast(w1, layer, rows=512):
    depth, n_e, d, two_f = w1.shape
    src = np.concatenate([np.arange(0, MXU_COLS, 2), np.arange(1, MXU_COLS, 2)])
    perm = np.zeros((MXU_COLS, MXU_COLS), np.float32)
    perm[src, np.arange(MXU_COLS)] = 1.0
    return pl.pallas_call(
        _deinterleave_kernel,
        grid=(n_e, d // rows),
        in_specs=[pl.BlockSpec((1, rows, two_f), lambda e, r: (layer * n_e + e, r, 0)),
                  pl.BlockSpec((MXU_COLS, MXU_COLS), lambda e, r: (0, 0))],
        out_specs=pl.BlockSpec((1, rows, two_f), lambda e, r: (e, r, 0)),
        out_shape=jax.ShapeDtypeStruct((n_e, d, two_f), BF16),
        compiler_params=_cparams(("parallel", "parallel")),
        name="w1_deinterleave",
    )(w1.reshape(depth * n_e, d, two_f), jnp.asarray(perm, BF16))


def _deinterleave_bias(b1_l):
    n_e, two_f = b1_l.shape
    return jnp.concatenate([b1_l[:, 0::2], b1_l[:, 1::2]], axis=1).reshape(n_e, 1, two_f)


def kernel(x, c, rel_bias, mod_w, mod_b, norm_mix_w, norm_ffn_w, w_in, conv_w, a_log, dt_bias, gdn_norm_w,
           q_norm_w, k_norm_w, w_out, router_w, router_b, w1, b1, w2, b2):
    depth = mod_w.shape[0]
    bsz, s, d = x.shape
    mod = _modulation(c, mod_w, mod_b)
    tm = min(512, s)
    tb = min(MOE_TB, bsz * s)
    rb = min(MOE_RB, tb)
    n_e, dff = w2.shape[1], w2.shape[2]
    w2b = w2.astype(BF16).reshape(depth * n_e, dff, d)
    for l in range(depth):
        qkv_a, z_a, qb, kb, vb, qi, small = _inproj(
            x, mod[l], norm_mix_w[l], _permute_w_in(w_in[l]), q_norm_w[l], k_norm_w[l], tm)
        y_a = _gdn(qkv_a, z_a, small, conv_w[l], a_log[l], dt_bias[l], gdn_norm_w[l], sb=tm)
        y_b = _dsa(qb, kb, vb, qi, small, rel_bias)
        xn, hp, ridx, gate = _outproj(y_a, y_b, x, mod[l], w_out[l].astype(BF16), norm_ffn_w[l],
                                      router_w[l], router_b[l], tm)
        x = _moe(xn, hp, ridx, gate, mod[l][:, 5], _deinterleave_cast(w1, l), _deinterleave_bias(b1[l]), w2b,
                 b2[l].reshape(n_e, 1, d), tb, rb, layer=l)
    return x
```

```python
import functools
import math

import jax
import jax.numpy as jnp
import numpy as np
from jax import lax
from jax.experimental import pallas as pl
from jax.experimental.pallas import tpu as pltpu

F32 = jnp.float32
BF16 = jnp.bfloat16
I32 = jnp.int32
U32 = jnp.uint32
HIGHEST = lax.Precision.HIGHEST

LANES = 128
SUBLANES = 8
VMEM_LIMIT_BYTES = 56 * 1024 * 1024

CHUNK = 64
HEAD_DIM_A = 128
N_HEADS_A = 4
WIDTH_A = N_HEADS_A * HEAD_DIM_A
CONV_K = 4
HEAD_DIM_B = 64
N_HEADS_B = 8
WIDTH_B = N_HEADS_B * HEAD_DIM_B
IDX_HEADS = 8
IDX_DIM = 64
TOPK_KEYS_MAX = 256
REL_BUCKETS = 32
REL_MAX_DIST = 1024
N_EXPERTS = 32
TOP_K = 4
SWIGLU_ALPHA = 1.702
SWIGLU_LIMIT = 7.0
EPS = 1e-6
NEG_BIG = -1e30

C_QKVA = 0
C_Z = C_QKVA + 3 * WIDTH_A
C_QB = C_Z + WIDTH_A
C_KB = C_QB + WIDTH_B
C_VB = C_KB + WIDTH_B
C_QI = C_VB + WIDTH_B
C_SMALL = C_QI + IDX_HEADS * IDX_DIM
D_IN_PAD = C_SMALL + LANES
S_KIDX = 0
S_B = IDX_DIM
S_A = S_B + N_HEADS_A
S_WIDX = S_A + N_HEADS_A


def _cparams(sem):
    return pltpu.CompilerParams(dimension_semantics=sem, vmem_limit_bytes=VMEM_LIMIT_BYTES)


def _sigmoid(x):
    return jax.nn.sigmoid(x)


def _silu(x):
    return x * _sigmoid(x)


def _softplus(x):
    return jnp.maximum(x, 0.0) + jnp.log(1.0 + jnp.exp(-jnp.abs(x)))


def _mod_kernel(c_ref, w_ref, b_ref, o_ref):
    a = _silu(c_ref[...])
    o_ref[0] = jnp.dot(a, w_ref[0], precision=HIGHEST, preferred_element_type=F32) + b_ref[0]


def _modulation(c, mod_w, mod_b):
    depth, d, n = mod_w.shape
    bsz = c.shape[0]
    rows = -(-bsz // SUBLANES) * SUBLANES
    c_pad = jnp.zeros((rows, d), F32).at[:bsz].set(c)
    tn = 1536
    out = pl.pallas_call(
        _mod_kernel,
        grid=(depth, n // tn),
        in_specs=[
            pl.BlockSpec((rows, d), lambda l, j: (0, 0)),
            pl.BlockSpec((1, d, tn), lambda l, j: (l, 0, j)),
            pl.BlockSpec((1, 1, tn), lambda l, j: (l, 0, j)),
        ],
        out_specs=pl.BlockSpec((1, rows, tn), lambda l, j: (l, 0, j)),
        out_shape=jax.ShapeDtypeStruct((depth, rows, n), F32),
        compiler_params=_cparams(("arbitrary", "arbitrary")),
        name="adaln_mod",
    )(c_pad, mod_w, mod_b.reshape(depth, 1, n))
    return out[:, :bsz].reshape(depth, bsz, 6, d)


def _head_rms(t, group_ref, wn, inv_dim):
    t2 = t * t
    hi = t2.astype(BF16)
    lo = (t2 - hi.astype(F32)).astype(BF16)
    ss = (jnp.dot(hi, group_ref[...], preferred_element_type=F32)
          + jnp.dot(lo, group_ref[...], preferred_element_type=F32))
    return t * lax.rsqrt(ss * inv_dim + EPS) * wn


def _inproj_kernel(x_ref, mod_ref, nw_ref, w_ref, group_ref, qn_ref, kn_ref,
                   qkva_ref, z_ref, qb_ref, kb_ref, vb_ref, qi_ref, small_ref):
    x = x_ref[0]
    ms = jnp.mean(x * x, axis=-1, keepdims=True)
    y = x * lax.rsqrt(ms + EPS) * nw_ref[...]
    h = y * (1.0 + mod_ref[0, 1:2, :]) + mod_ref[0, 0:1, :]
    hb = h.astype(BF16)

    def mm(lo, width):
        return jnp.dot(hb, w_ref[:, lo:lo + width], preferred_element_type=F32)

    qkva_ref[0] = mm(C_QKVA, 3 * WIDTH_A)
    z_ref[0] = mm(C_Z, WIDTH_A)
    q = _head_rms(mm(C_QB, WIDTH_B), group_ref, qn_ref[...], 1.0 / HEAD_DIM_B)
    qb_ref[0] = (q * (HEAD_DIM_B ** -0.5 * LOG2E)).astype(BF16)
    k = _head_rms(mm(C_KB, WIDTH_B), group_ref, kn_ref[...], 1.0 / HEAD_DIM_B)
    kb_ref[0] = k.astype(BF16)
    v = mm(C_VB, WIDTH_B)
    lane = lax.broadcasted_iota(I32, (v.shape[0], LANES), 1)
    tail = (lane == HEAD_DIM_B).astype(F32)
    for p in range(N_HEADS_B // 2):
        pair = v[:, p * LANES:(p + 1) * LANES]
        vb_ref[0, :, (2 * p) * LANES:(2 * p + 1) * LANES] = jnp.where(lane < HEAD_DIM_B, pair, tail).astype(BF16)
        vb_ref[0, :, (2 * p + 1) * LANES:(2 * p + 2) * LANES] = jnp.where(
            lane < HEAD_DIM_B, pltpu.roll(pair, HEAD_DIM_B, axis=1), tail).astype(BF16)
    qi_ref[0] = mm(C_QI, IDX_HEADS * IDX_DIM).astype(BF16)
    small_ref[0] = mm(C_SMALL, LANES)


def _permute_w_in(w_in_l):
    d = w_in_l.shape[0]
    o = 0
    qkva = w_in_l[:, o:o + 3 * WIDTH_A]; o += 3 * WIDTH_A
    z = w_in_l[:, o:o + WIDTH_A]; o += WIDTH_A
    b = w_in_l[:, o:o + N_HEADS_A]; o += N_HEADS_A
    a = w_in_l[:, o:o + N_HEADS_A]; o += N_HEADS_A
    qkvb = w_in_l[:, o:o + 3 * WIDTH_B]; o += 3 * WIDTH_B
    qi = w_in_l[:, o:o + IDX_HEADS * IDX_DIM]; o += IDX_HEADS * IDX_DIM
    ki = w_in_l[:, o:o + IDX_DIM]; o += IDX_DIM
    wi = w_in_l[:, o:o + IDX_HEADS]; o += IDX_HEADS
    pad = jnp.zeros((d, LANES - IDX_DIM - 2 * N_HEADS_A - IDX_HEADS), w_in_l.dtype)
    return jnp.concatenate([qkva, z, qkvb, qi, ki, b, a, wi, pad], axis=1).astype(BF16)


def _group_ones(width, group):
    g = np.arange(width) // group
    return jnp.asarray((g[:, None] == g[None, :]).astype(np.float32), dtype=BF16)


def _inproj(x, mod_l, norm_w, w_perm, q_norm_w, k_norm_w, tm):
    bsz, s, d = x.shape
    f = lambda b, i: (b, i, 0)
    const2 = lambda b, i: (0, 0)
    outs = [
        (3 * WIDTH_A, F32), (WIDTH_A, F32), (WIDTH_B, BF16), (WIDTH_B, BF16), (N_HEADS_B * LANES, BF16),
        (IDX_HEADS * IDX_DIM, BF16), (LANES, F32),
    ]
    return pl.pallas_call(
        _inproj_kernel,
        grid=(bsz, s // tm),
        in_specs=[
            pl.BlockSpec((1, tm, d), f),
            pl.BlockSpec((1, 6, d), lambda b, i: (b, 0, 0)),
            pl.BlockSpec((1, d), const2),
            pl.BlockSpec((d, D_IN_PAD), const2),
            pl.BlockSpec((WIDTH_B, WIDTH_B), const2),
            pl.BlockSpec((1, WIDTH_B), const2),
            pl.BlockSpec((1, WIDTH_B), const2),
        ],
        out_specs=[pl.BlockSpec((1, tm, w), f) for w, _ in outs],
        out_shape=[jax.ShapeDtypeStruct((bsz, s, w), dt) for w, dt in outs],
        compiler_params=_cparams(("parallel", "parallel")),
        name="inproj",
    )(x, mod_l, norm_w.reshape(1, d), w_perm, _group_ones(WIDTH_B, HEAD_DIM_B),
      jnp.tile(q_norm_w, N_HEADS_B).reshape(1, WIDTH_B), jnp.tile(k_norm_w, N_HEADS_B).reshape(1, WIDTH_B))


def _dot_nt(a, b, precision=None):
    return lax.dot_general(a, b, (((1,), (1,)), ((), ())), precision=precision, preferred_element_type=F32)


def _split2(x):
    hi = x.astype(BF16)
    return hi, (x - hi.astype(F32)).astype(BF16)


def _split3(x):
    hi = x.astype(BF16)
    r = x - hi.astype(F32)
    mid = r.astype(BF16)
    return hi, mid, (r - mid.astype(F32)).astype(BF16)


def _mm3(a, b):
    return (jnp.dot(a[0], b[0], preferred_element_type=F32) + jnp.dot(a[0], b[1], preferred_element_type=F32)
            + jnp.dot(a[1], b[0], preferred_element_type=F32))


GDN_PAR = 8


def _gdn_kernel(qkv_ref, z_ref, small_ref, convw_ref, alog_ref, dtb_ref, nw_ref,
                y_ref, xe_ref, u_ref, state_ref, uval_ref, wdec_ref, qg_ref, kdec_ref, attn_ref, egl_ref, *, sb):
    n_chunks = sb // CHUNK
    halo = SUBLANES

    @pl.when(pl.program_id(1) == 0)
    def _():
        xe_ref[0:halo, :] = jnp.zeros((halo, 3 * WIDTH_A), F32)
        state_ref[...] = jnp.zeros_like(state_ref)

    xe_ref[halo:halo + sb, :] = qkv_ref[0]

    rows = 128
    for g in range(3 * WIDTH_A // LANES):
        cs = slice(g * LANES, (g + 1) * LANES)
        for r in range(sb // rows):
            base = halo - (CONV_K - 1) + r * rows
            acc = xe_ref[base:base + rows, cs] * convw_ref[0:1, cs]
            for j in range(1, CONV_K):
                acc = acc + xe_ref[base + j:base + j + rows, cs] * convw_ref[j:j + 1, cs]
            u_ref[r * rows:(r + 1) * rows, cs] = _silu(acc)

    xe_ref[0:halo, :] = xe_ref[sb:sb + halo, :]

    wide = N_HEADS_A * CHUNK
    heads = range(N_HEADS_A)
    ii = lax.broadcasted_iota(I32, (CHUNK, wide), 0)
    jj = lax.broadcasted_iota(I32, (CHUNK, wide), 1) & (CHUNK - 1)
    eye_w = (ii == jj).astype(F32)
    tri = (lax.broadcasted_iota(I32, (CHUNK, CHUNK), 0)
           >= lax.broadcasted_iota(I32, (CHUNK, CHUNK), 1)).astype(F32).astype(BF16)
    shift = int(math.log2(CHUNK))
    bd_mask = ((lax.broadcasted_iota(I32, (wide, wide), 0) >> shift)
               == (lax.broadcasted_iota(I32, (wide, wide), 1) >> shift)).astype(F32)

    bd_mask = bd_mask.astype(BF16)

    def block_diag(parts):
        return tuple(jnp.concatenate([m] * N_HEADS_A, axis=0) * bd_mask for m in parts)

    def prepare(c):
        rs = pl.ds(pl.multiple_of(c * CHUNK, CHUNK), CHUNK)
        qn, kn, v, beta, g_b = [], [], [], [], []
        for h in heads:
            q = u_ref[rs, h * HEAD_DIM_A:(h + 1) * HEAD_DIM_A]
            k = u_ref[rs, WIDTH_A + h * HEAD_DIM_A:WIDTH_A + (h + 1) * HEAD_DIM_A]
            v.append(u_ref[rs, 2 * WIDTH_A + h * HEAD_DIM_A:2 * WIDTH_A + (h + 1) * HEAD_DIM_A])
            qn.append(q * (lax.rsqrt(jnp.sum(q * q, axis=-1, keepdims=True) + EPS) * (HEAD_DIM_A ** -0.5)))
            kn.append(k * lax.rsqrt(jnp.sum(k * k, axis=-1, keepdims=True) + EPS))
            beta.append(_sigmoid(small_ref[0, rs, S_B + h:S_B + h + 1]))
            g = -jnp.exp(alog_ref[0:1, h:h + 1]) * _softplus(small_ref[0, rs, S_A + h:S_A + h + 1]
                                                            + dtb_ref[0:1, h:h + 1])
            g_b.append(jnp.broadcast_to(g, (CHUNK, CHUNK)))
        gc_w = sum(jnp.dot(tri, part, preferred_element_type=F32) for part in _split3(jnp.concatenate(g_b, axis=1)))
        gc_row = jnp.sum(jnp.where(ii == jj, gc_w, 0.0), axis=0, keepdims=True)
        decay_w = jnp.exp(jnp.where(ii >= jj, gc_w - gc_row, NEG_BIG))
        k_beta = [kn[h] * beta[h] for h in heads]
        kk_w = jnp.concatenate([_dot_nt(k_beta[h].astype(BF16), kn[h].astype(BF16)) for h in heads], axis=1)
        a_w = -jnp.where(ii > jj, kk_w * decay_w, 0.0)
        gc = [gc_w[:, h * CHUNK:h * CHUNK + 1] for h in heads]
        egc = [jnp.exp(gc[h]) for h in heads]
        qk = [_dot_nt(qn[h].astype(BF16), kn[h].astype(BF16)) for h in heads]
        for h in heads:
            g_last = gc[h][CHUNK - 1:CHUNK, :]
            qg_ref[c, h] = (qn[h] * egc[h]).astype(BF16)
            kdec_ref[c, h] = kn[h] * jnp.exp(g_last - gc[h])
            attn_ref[c, h] = (qk[h] * decay_w[:, h * CHUNK:(h + 1) * CHUNK]).astype(BF16)
            egl_ref[c, h] = jnp.broadcast_to(jnp.exp(g_last), (SUBLANES, LANES))
        return a_w, [_split2(v[h] * beta[h]) for h in heads], [_split2(k_beta[h] * egc[h]) for h in heads]

    def solve_body(cg, carry):
        group = range(GDN_PAR)
        chunks = [cg * GDN_PAR + i for i in group]
        pre = [prepare(c) for c in chunks]
        t_w = [eye_w + pre[i][0] for i in group]
        p_parts = [_split2(pre[i][0]) for i in group]
        bd = [block_diag(p_parts[i]) for i in group]
        for _ in range(shift - 1):
            prod = [_mm3(p_parts[i], bd[i]) for i in group]
            p_parts = [_split2(prod[i]) for i in group]
            bd = [block_diag(p_parts[i]) for i in group]
            upd = [_mm3(_split2(t_w[i]), bd[i]) for i in group]
            t_w = [t_w[i] + upd[i] for i in group]
        for i in group:
            t_h = [_split2(t_w[i][:, h * CHUNK:(h + 1) * CHUNK]) for h in heads]
            u_val = [_mm3(t_h[h], pre[i][1][h]) for h in heads]
            w_dec = [_mm3(t_h[h], pre[i][2][h]) for h in heads]
            for h in heads:
                uval_ref[chunks[i], h] = u_val[h]
                wdec_ref[chunks[i], h] = w_dec[h].astype(BF16)
        return carry

    lax.fori_loop(0, n_chunks // GDN_PAR, solve_body, 0)

    def scan_body(c, carry):
        rs = pl.ds(pl.multiple_of(c * CHUNK, CHUNK), CHUNK)
        state = [state_ref[h] for h in heads]
        state_b = [s_h.astype(BF16) for s_h in state]
        w_s = [jnp.dot(wdec_ref[c, h], state_b[h], preferred_element_type=F32) for h in heads]
        v_new = [(uval_ref[c, h] - w_s[h]).astype(BF16) for h in heads]
        o = [jnp.dot(qg_ref[c, h], state_b[h], preferred_element_type=F32)
             + jnp.dot(attn_ref[c, h], v_new[h], preferred_element_type=F32) for h in heads]
        for h in heads:
            state_ref[h] = (state[h] * egl_ref[c, h][0:1, 0:1]
                            + jnp.dot(kdec_ref[c, h].T.astype(BF16), v_new[h], preferred_element_type=F32))
        for h in heads:
            hs = slice(h * HEAD_DIM_A, (h + 1) * HEAD_DIM_A)
            on = o[h] * lax.rsqrt(jnp.mean(o[h] * o[h], axis=-1, keepdims=True) + EPS) * nw_ref[...]
            y_ref[0, rs, hs] = (on * _silu(z_ref[0, rs, hs])).astype(BF16)
        return carry

    lax.fori_loop(0, n_chunks, scan_body, 0)


def _gdn(qkv_a, z_a, small, conv_w, a_log, dt_bias, norm_w, sb):
    bsz, s, _ = qkv_a.shape
    n_c = sb // CHUNK
    assert n_c % GDN_PAR == 0 and s % sb == 0
    per_head = (n_c, N_HEADS_A, CHUNK, HEAD_DIM_A)
    f = lambda b, i: (b, i, 0)
    const2 = lambda b, i: (0, 0)
    return pl.pallas_call(
        functools.partial(_gdn_kernel, sb=sb),
        grid=(bsz, s // sb),
        in_specs=[
            pl.BlockSpec((1, sb, 3 * WIDTH_A), f),
            pl.BlockSpec((1, sb, WIDTH_A), f),
            pl.BlockSpec((1, sb, LANES), f),
            pl.BlockSpec((CONV_K, 3 * WIDTH_A), const2),
            pl.BlockSpec((1, N_HEADS_A), const2),
            pl.BlockSpec((1, N_HEADS_A), const2),
            pl.BlockSpec((1, HEAD_DIM_A), const2),
        ],
        out_specs=pl.BlockSpec((1, sb, WIDTH_A), f),
        out_shape=jax.ShapeDtypeStruct((bsz, s, WIDTH_A), BF16),
        scratch_shapes=[
            pltpu.VMEM((sb + SUBLANES, 3 * WIDTH_A), F32),
            pltpu.VMEM((sb, 3 * WIDTH_A), F32),
            pltpu.VMEM((N_HEADS_A, HEAD_DIM_A, HEAD_DIM_A), F32),
            pltpu.VMEM(per_head, F32),
            pltpu.VMEM(per_head, BF16),
            pltpu.VMEM(per_head, BF16),
            pltpu.VMEM(per_head, F32),
            pltpu.VMEM((n_c, N_HEADS_A, CHUNK, CHUNK), BF16),
            pltpu.VMEM((n_c, N_HEADS_A, SUBLANES, LANES), F32),
        ],
        compiler_params=_cparams(("parallel", "arbitrary")),
        name="gdn",
    )(qkv_a, z_a, small, conv_w, a_log.reshape(1, -1), dt_bias.reshape(1, -1), norm_w.reshape(1, -1))


QB = 128
FAR_T = 512
FAR_G = FAR_T // LANES
ATT_W = 256
LOG2E = 1.4426950408889634
NEAR_D = 9
NEAR_MIN = 5
INT_MIN = -2 ** 31


def _t5_bucket_np(rel):
    nb = REL_BUCKETS // 2
    max_exact = nb // 2
    side = np.where(rel > 0, nb, 0)
    n = np.abs(rel)
    nf = np.maximum(n, 1).astype(np.float32)
    large = max_exact + (np.log(nf / np.float32(max_exact)) / np.float32(math.log(REL_MAX_DIST / max_exact))
                         * np.float32(nb - max_exact)).astype(np.int32)
    large = np.minimum(large, nb - 1)
    return (side + np.where(n < max_exact, n, large)).astype(np.int32)


def _near_bucket_table():
    r = np.arange(QB)[:, None]
    c = np.arange(LANES)[None, :]
    return np.stack([_t5_bucket_np(c - r - LANES * d) for d in range(NEAR_D)])


FAR_BUCKET = int(_t5_bucket_np(np.array([-(NEAR_MIN * LANES + 1)]))[0])
assert all(int(b) == FAR_BUCKET for b in _t5_bucket_np(-np.arange((NEAR_MIN + 1) * LANES - (QB - 1), 1 << 20, 997)))


def _sortable_key(score):
    bits = pltpu.bitcast(score + 0.0, I32)
    return bits ^ ((bits >> 31) & 0x7FFFFFFF)


PLANE_G = 32


def _bit_transpose32(words):
    a = list(words)
    mask, j = 0x0000FFFF, 16
    while j:
        k = 0
        while k < 32:
            t = (a[k] ^ lax.shift_right_logical(a[k + j], jnp.int32(j))) & mask
            a[k] = a[k] ^ t
            a[k + j] = a[k + j] ^ lax.shift_left(t, jnp.int32(j))
            k = (k + j + 1) & ~j
        j >>= 1
        mask = (mask ^ (mask << j)) & 0xFFFFFFFF
    return a


def _dsa_kernel(rb_ref, qb_ref, qi_ref, small_ref, kb_ref, va_ref, kidx2_ref, tab_ref,
                y_ref, qis_ref, qs_ref, wb_ref, keys_ref, nbias_ref, thr_ref, jlim_ref,
                m_ref, acc_ref, s_ref, p_ref, peak_ref, planes_ref, need_ref, excess_ref, *, seq, k_sel):
    i = pl.program_id(1)
    lane = lax.broadcasted_iota(I32, (QB, LANES), 1)
    row = lax.broadcasted_iota(I32, (QB, LANES), 0)
    even_f = (lane < HEAD_DIM_B).astype(F32)
    even_b = even_f.astype(BF16)
    odd_b = (1.0 - even_f).astype(BF16)

    @pl.when(i == 0)
    def _():
        nbias_ref[...] = jnp.zeros_like(nbias_ref)
        keys_ref[...] = jnp.full(keys_ref.shape, INT_MIN, I32)

        def d_body(d, c0):
            tab = tab_ref[d]

            def b_body(bk, c1):
                hit = tab == bk
                for h in range(N_HEADS_B):
                    nbias_ref[d * N_HEADS_B + h] = jnp.where(hit, rb_ref[bk, h] * LOG2E,
                                                             nbias_ref[d * N_HEADS_B + h])
                return c1

            return lax.fori_loop(0, REL_BUCKETS, b_body, c0)

        lax.fori_loop(0, NEAR_D, d_body, 0)

    for p in range(N_HEADS_B // 2):
        ps = slice(p * LANES, (p + 1) * LANES)
        qi_pair = qi_ref[0, :, ps]
        qis_ref[(2 * p) * QB:(2 * p + 1) * QB, :] = qi_pair * even_b
        qis_ref[(2 * p + 1) * QB:(2 * p + 2) * QB, :] = qi_pair * odd_b
        q_pair = qb_ref[0, :, ps]
        qs_ref[(2 * p) * QB:(2 * p + 1) * QB, :] = q_pair * even_b
        qs_ref[(2 * p + 1) * QB:(2 * p + 2) * QB, :] = q_pair * odd_b
    w_scale = IDX_HEADS ** -0.5 * IDX_DIM ** -0.5
    for h in range(IDX_HEADS):
        wb_ref[h] = jnp.broadcast_to(small_ref[0, :, S_WIDX + h:S_WIDX + h + 1] * w_scale, (QB, LANES))

    limit = i * QB + CHUNK + jnp.where(row >= CHUNK, CHUNK, 0)

    def score_tile(t):
        for c in range(FAR_T // ATT_W):
            k0 = t * FAR_T + c * ATT_W
            k_part = kidx2_ref[0, pl.ds(pl.multiple_of(k0, ATT_W), ATT_W), :]
            acc = None
            for h in range(IDX_HEADS):
                dots = _dot_nt(qis_ref[h * QB:(h + 1) * QB, :], k_part)
                term = jnp.maximum(dots, 0.0) * jnp.concatenate([wb_ref[h]] * (ATT_W // LANES), axis=1)
                acc = term if acc is None else acc + term
            for gg in range(ATT_W // LANES):
                col = k0 + gg * LANES + lane
                keys_ref[(k0 // LANES) + gg] = jnp.where(
                    col < limit, _sortable_key(acc[:, gg * LANES:(gg + 1) * LANES]), INT_MIN)

    def score_pair_body(u, c0):
        score_tile(2 * u)
        score_tile(2 * u + 1)
        return c0

    def score_body(t, c0):
        score_tile(t)
        return c0

    n_groups = i + 1
    n_tiles = i // FAR_G + 1
    lax.fori_loop(0, n_tiles // 2, score_pair_body, 0)
    lax.fori_loop(2 * (n_tiles // 2), n_tiles, score_body, 0)

    @pl.when(n_tiles % 2 == 1)
    def _():
        for g in range(FAR_G):
            keys_ref[n_tiles * FAR_G + g] = jnp.full((QB, LANES), INT_MIN, I32)

    def count(pred):
        def t_body(t, acc):
            for g in range(2 * FAR_G):
                grp = t * (2 * FAR_G) + g
                acc = acc + jnp.where(pred(keys_ref[grp], grp * LANES + lane), 1, 0)
            return acc

        acc = lax.fori_loop(0, (n_tiles + 1) // 2, t_body, jnp.zeros((QB, LANES), I32))
        return jnp.broadcast_to(jnp.sum(acc, axis=1, keepdims=True), (QB, LANES))

    thr_ref[...] = jnp.full((QB, LANES), INT_MIN, I32)
    jlim_ref[...] = jnp.full((QB, LANES), -1, I32)

    def lane_total(x):
        return jnp.broadcast_to(jnp.sum(x, axis=1, keepdims=True), (QB, LANES))

    @pl.when(n_groups * QB > k_sel)
    def _():
        def transpose_half(half):
            def row_body(rr, c0):
                rows = pl.ds(pl.multiple_of(rr * SUBLANES, SUBLANES), SUBLANES)
                planes = _bit_transpose32([keys_ref[half * PLANE_G + g, rows, :] ^ INT_MIN for g in range(PLANE_G)])
                for b in range(32):
                    planes_ref[half, b, rows, :] = planes[b]
                return c0

            lax.fori_loop(0, QB // SUBLANES, row_body, 0)

        def search(n_half):
            halves = range(n_half)
            row_groups = (pl.ds(0, QB // 2), pl.ds(QB // 2, QB // 2))
            shape = (QB // 2, LANES)

            def total(x):
                return jnp.broadcast_to(jnp.sum(x, axis=1, keepdims=True), shape)

            def digit(step, rows, state):
                r, eq, above = state
                cls = []
                for half in halves:
                    hi = eq[half] & planes_ref[half, 2 * step, rows, :]
                    lo_plane = planes_ref[half, 2 * step + 1, rows, :]
                    zero_hi = eq[half] ^ hi
                    e11 = hi & lo_plane
                    e01 = zero_hi & lo_plane
                    cls.append((e11, hi ^ e11, e01, zero_hi ^ e01))
                n11, n10, n01 = [sum(lax.population_count(cls[h][d]) for h in halves) for d in range(3)]
                c3 = above + n11
                c2 = c3 + n10
                c1 = c2 + n01
                d3, d2, d1 = [total(c) >= k_sel for c in (c3, c2, c1)]
                eq = tuple(jnp.where(d3, cls[h][0], jnp.where(d2, cls[h][1], jnp.where(d1, cls[h][2], cls[h][3])))
                           for h in halves)
                above = jnp.where(d3, above, jnp.where(d2, c3, jnp.where(d1, c2, c1)))
                value = jnp.where(d3, 3, jnp.where(d2, 2, jnp.where(d1, 1, 0)))
                return r | lax.shift_left(value, 30 - 2 * step), eq, above

            def digit_body(step, carry):
                return tuple(digit(step, rows, state) for rows, state in zip(row_groups, carry))

            zero = jnp.zeros(shape, I32)
            full = jnp.full(shape, -1, I32)
            start = (zero, (full,) * n_half, zero)
            for rows, (r, eq, above) in zip(row_groups, lax.fori_loop(0, 16, digit_body, (start, start))):
                thr_ref[rows, :] = r ^ INT_MIN
                need = k_sel - total(above)
                need_ref[rows, :] = need
                excess_ref[rows, :] = total(sum(lax.population_count(eq[h]) for h in halves)) - need

        transpose_half(0)

        @pl.when(n_groups > PLANE_G)
        def _():
            transpose_half(1)
            search(2)

        @pl.when(n_groups <= PLANE_G)
        def _():
            search(1)

        r = thr_ref[...]
        need = need_ref[...]
        excess = excess_ref[...]
        jlim_ref[...] = jnp.where(r == INT_MIN, -1, seq)

        @pl.when(jnp.max(excess) > 0)
        def _():
            def j_body(step, jl):
                cand = jl + lax.shift_left(jnp.int32(1), (seq.bit_length() - 1) - step)
                cnt = count(lambda kt, col: (kt == r) & (col < cand))
                return jnp.where(cnt < need, cand, jl)

            jl = lax.fori_loop(0, seq.bit_length(), j_body, jnp.zeros((QB, LANES), I32))
            jlim_ref[...] = jnp.where(r == INT_MIN, -1, jl)

    m_ref[...] = jnp.full(m_ref.shape, NEG_BIG, F32)
    acc_ref[...] = jnp.zeros_like(acc_ref)

    part_g = ATT_W // LANES

    def tile_mask(g0, groups):
        negm = []
        for g in range(groups):
            kt = keys_ref[g0 + g]
            sel = (kt > thr_ref[...]) | ((kt == thr_ref[...]) & ((g0 + g) * LANES + lane <= jlim_ref[...]))
            negm.append(jnp.where(sel, 0.0, -jnp.inf))
        return negm

    def logits_pair(g0, groups, p, negm, near, buf):
        heads = ((2 * p, slice(0, QB)), (2 * p + 1, slice(QB, 2 * QB)))
        peak = [None, None]
        for c in range(groups // part_g):
            ks = pl.ds(pl.multiple_of((g0 + c * part_g) * LANES, ATT_W), ATT_W)
            s = _dot_nt(qs_ref[2 * p * QB:(2 * p + 2) * QB, :], kb_ref[0, ks, p * LANES:(p + 1) * LANES])
            for gg in range(part_g):
                g = c * part_g + gg
                for n, (h, rows) in enumerate(heads):
                    extra = negm[g]
                    if near:
                        extra = extra + nbias_ref[jnp.clip(i - (g0 + g), 0, NEAR_D - 1) * N_HEADS_B + h]
                    v = s[rows, gg * LANES:(gg + 1) * LANES] + extra
                    s_ref[buf, h, g] = v
                    peak[n] = v if peak[n] is None else jnp.maximum(peak[n], v)
        for n, (h, _) in enumerate(heads):
            peak_ref[buf, h] = peak[n]

    def softmax_head(h, groups, bias_scalar, buf):
        m_old = m_ref[h]
        m_new = jnp.maximum(m_old, jnp.max(peak_ref[buf, h], axis=1, keepdims=True) + bias_scalar)
        shift = jnp.concatenate([m_new - bias_scalar] * part_g, axis=1)
        for c in range(groups // part_g):
            part = jnp.concatenate([s_ref[buf, h, c * part_g + g] for g in range(part_g)], axis=1)
            p_ref[buf, h, :, c * ATT_W:(c + 1) * ATT_W] = jnp.exp2(part - shift).astype(BF16)
        acc_ref[h] = jnp.exp2(m_old - m_new) * acc_ref[h]
        m_ref[h] = m_new

    def pv_head(g0, groups, h, buf):
        ks = pl.ds(pl.multiple_of(g0 * LANES, FAR_T), groups * LANES)
        acc_ref[h] += jnp.dot(p_ref[buf, h, :, :groups * LANES], va_ref[0, ks, h * LANES:(h + 1) * LANES],
                              preferred_element_type=F32)

    n_pair = N_HEADS_B // 2
    n_stage = n_pair + 2

    def stage(t, near, negm, step, buf):
        g0 = t * FAR_G
        if step < n_pair:
            logits_pair(g0, FAR_G, step, negm, near, buf)
        if 1 <= step <= n_pair:
            for h in (2 * step - 2, 2 * step - 1):
                softmax_head(h, FAR_G, 0.0 if near else rb_ref[FAR_BUCKET, h] * LOG2E, buf)
        if step >= 2:
            for h in (2 * step - 4, 2 * step - 3):
                pv_head(g0, FAR_G, h, buf)

    def tile_body(near):
        def body(t, c0):
            negm = tile_mask(t * FAR_G, FAR_G)
            for step in range(n_stage):
                stage(t, near, negm, step, 0)
            return c0

        return body

    def tile_pair_body(near, start):
        def body(u, c0):
            first, second = start + 2 * u, start + 2 * u + 1
            masks = (tile_mask(first * FAR_G, FAR_G), tile_mask(second * FAR_G, FAR_G))
            lag = n_stage // 2
            for slot in range(n_stage + lag):
                if slot < n_stage:
                    stage(first, near, masks[0], slot, 0)
                if lag <= slot:
                    stage(second, near, masks[1], slot - lag, 1)
            return c0

        return body

    far_tiles = jnp.maximum(i - NEAR_MIN, 0) // FAR_G
    far_pairs = far_tiles // 2
    near_pairs = (n_tiles - far_tiles) // 2
    lax.fori_loop(0, far_pairs, tile_pair_body(False, 0), 0)
    lax.fori_loop(2 * far_pairs, far_tiles, tile_body(False), 0)
    lax.fori_loop(0, near_pairs, tile_pair_body(True, far_tiles), 0)
    lax.fori_loop(far_tiles + 2 * near_pairs, n_tiles, tile_body(True), 0)

    def head_out(h):
        a = acc_ref[h]
        return a * (1.0 / a[:, HEAD_DIM_B:HEAD_DIM_B + 1])

    for p in range(N_HEADS_B // 2):
        o_odd = pltpu.roll(head_out(2 * p + 1), HEAD_DIM_B, axis=1)
        y_ref[0, :, p * LANES:(p + 1) * LANES] = jnp.where(lane < HEAD_DIM_B, head_out(2 * p), o_odd).astype(BF16)


def _dsa(qb, kb, va, qi, small, rel_bias):
    bsz, s, _ = qb.shape
    assert s % FAR_T == 0 and s // LANES <= 2 * PLANE_G
    k_sel = min(TOPK_KEYS_MAX, s // 4)
    kidx = small[:, :, S_KIDX:S_KIDX + IDX_DIM].astype(BF16)
    kidx2 = jnp.concatenate([kidx, kidx], axis=-1)
    tab = jnp.asarray(_near_bucket_table())
    blk = lambda b, i: (b, i, 0)
    full = lambda b, i: (b, 0, 0)
    one = pl.Buffered(1)
    return pl.pallas_call(
        functools.partial(_dsa_kernel, seq=s, k_sel=k_sel),
        grid=(bsz, s // QB),
        in_specs=[
            pl.BlockSpec(memory_space=pltpu.SMEM),
            pl.BlockSpec((1, QB, WIDTH_B), blk),
            pl.BlockSpec((1, QB, IDX_HEADS * IDX_DIM), blk),
            pl.BlockSpec((1, QB, LANES), blk),
            pl.BlockSpec((1, s, WIDTH_B), full, pipeline_mode=one),
            pl.BlockSpec((1, s, N_HEADS_B * LANES), full, pipeline_mode=one),
            pl.BlockSpec((1, s, LANES), full, pipeline_mode=one),
            pl.BlockSpec((NEAR_D, QB, LANES), lambda b, i: (0, 0, 0), pipeline_mode=one),
        ],
        out_specs=pl.BlockSpec((1, QB, WIDTH_B), blk),
        out_shape=jax.ShapeDtypeStruct((bsz, s, WIDTH_B), BF16),
        scratch_shapes=[
            pltpu.VMEM((IDX_HEADS * QB, LANES), BF16),
            pltpu.VMEM((N_HEADS_B * QB, LANES), BF16),
            pltpu.VMEM((IDX_HEADS, QB, LANES), F32),
            pltpu.VMEM((2 * PLANE_G, QB, LANES), I32),
            pltpu.VMEM((NEAR_D * N_HEADS_B, QB, LANES), F32),
            pltpu.VMEM((QB, LANES), I32),
            pltpu.VMEM((QB, LANES), I32),
            pltpu.VMEM((N_HEADS_B, QB, LANES), F32),
            pltpu.VMEM((N_HEADS_B, QB, LANES), F32),
            pltpu.VMEM((2, N_HEADS_B, FAR_G, QB, LANES), F32),
            pltpu.VMEM((2, N_HEADS_B, QB, FAR_T), BF16),
            pltpu.VMEM((2, N_HEADS_B, QB, LANES), F32),
            pltpu.VMEM((2, 32, QB, LANES), I32),
            pltpu.VMEM((QB, LANES), I32),
            pltpu.VMEM((QB, LANES), I32),
        ],
        compiler_params=_cparams(("parallel", "arbitrary")),
        name="dsa",
    )(rel_bias, qb, qi, small, kb, va, kidx2, tab)


HALF_MASK = 0xFFFF0000


def _pack_halves(t):
    w = t.shape[1] // 2
    bits = pltpu.bitcast(t.astype(BF16).astype(F32), U32)
    return (bits[:, :w] >> 16) | (bits[:, w:] & jnp.uint32(HALF_MASK))


def _unpack_halves(p):
    lo = pltpu.bitcast(p << 16, F32)
    hi = pltpu.bitcast(p & jnp.uint32(HALF_MASK), F32)
    return jnp.concatenate([lo, hi], axis=1)


def _outproj_kernel(ya_ref, yb_ref, x_ref, mod_ref, wo_ref, nw_ref, rw_ref, rbias_ref,
                    xn_ref, hp_ref, ridx_ref, gate_ref):
    wa = ya_ref.shape[2]
    y = (jnp.dot(ya_ref[0], wo_ref[0:wa, :], preferred_element_type=F32)
         + jnp.dot(yb_ref[0], wo_ref[wa:, :], preferred_element_type=F32))
    xn = x_ref[0] + mod_ref[0, 2:3, :] * y
    xn_ref[0] = xn
    ms = jnp.mean(xn * xn, axis=-1, keepdims=True)
    h = xn * lax.rsqrt(ms + EPS) * nw_ref[...] * (1.0 + mod_ref[0, 4:5, :]) + mod_ref[0, 3:4, :]
    hp_ref[0] = _pack_halves(h)

    logits = jnp.dot(h, rw_ref[...], precision=HIGHEST, preferred_element_type=F32) + rbias_ref[...]
    lane = lax.broadcasted_iota(I32, logits.shape, 1)
    cur = logits
    vals, ridx = [], jnp.zeros(logits.shape, I32)
    for k in range(TOP_K):
        mx = jnp.max(cur, axis=1, keepdims=True)
        am = jnp.min(jnp.where(cur == mx, lane, LANES), axis=1, keepdims=True)
        cur = jnp.where(lane == am, -jnp.inf, cur)
        vals.append(mx)
        ridx = jnp.where(lane == k, am, ridx)
    ex = [jnp.exp(v - vals[0]) for v in vals]
    inv = 1.0 / (ex[0] + ex[1] + ex[2] + ex[3])
    gate = jnp.zeros(logits.shape, F32)
    for k in range(TOP_K):
        gate = jnp.where(lane == k, ex[k] * inv, gate)
    ridx_ref[0] = ridx
    gate_ref[0] = gate


def _outproj(y_a, y_b, x, mod_l, w_out_bf, norm_w, router_w, router_b, tm):
    bsz, s, d = x.shape
    n_e = router_w.shape[1]
    rw = jnp.zeros((d, LANES), F32).at[:, :n_e].set(router_w)
    rbias = jnp.full((1, LANES), NEG_BIG, F32).at[0, :n_e].set(router_b)
    blk = lambda b, i: (b, i, 0)
    const2 = lambda b, i: (0, 0)
    return pl.pallas_call(
        _outproj_kernel,
        grid=(bsz, s // tm),
        in_specs=[
            pl.BlockSpec((1, tm, y_a.shape[2]), blk),
            pl.BlockSpec((1, tm, y_b.shape[2]), blk),
            pl.BlockSpec((1, tm, d), blk),
            pl.BlockSpec((1, 6, d), lambda b, i: (b, 0, 0)),
            pl.BlockSpec((d, d), const2),
            pl.BlockSpec((1, d), const2),
            pl.BlockSpec((d, LANES), const2),
            pl.BlockSpec((1, LANES), const2),
        ],
        out_specs=[pl.BlockSpec((1, tm, d), blk), pl.BlockSpec((1, tm, d // 2), blk),
                   pl.BlockSpec((1, tm, LANES), blk), pl.BlockSpec((1, tm, LANES), blk)],
        out_shape=[jax.ShapeDtypeStruct((bsz, s, d), F32), jax.ShapeDtypeStruct((bsz, s, d // 2), U32),
                   jax.ShapeDtypeStruct((bsz, s, LANES), I32), jax.ShapeDtypeStruct((bsz, s, LANES), F32)],
        compiler_params=_cparams(("parallel", "parallel")),
        name="outproj_router",
    )(y_a, y_b, x, mod_l, w_out_bf, norm_w.reshape(1, d), rw, rbias)


MOE_TB = 2048
MOE_RB = 512
MOE_M = 144
MOE_WBUF = 3


def _moe_kernel(first_ref, nch_ref, cbase_ref, tok_ref, row_ref, hp_ref, w1_ref, b1_ref, w2_ref, b2_ref,
                gate_ref, x_ref, g2_ref, o_ref, slots_ref, xg_ref, yb_ref, w1_buf, w2_buf, w1_sem, w2_sem,
                *, tb, rb, table_len, layer, n_super):
    sb = pl.program_id(0)
    e = pl.program_id(1)
    dff = w2_ref.shape[1]
    plane = tb + SUBLANES
    table = sb * table_len + 1

    g_step = sb * N_EXPERTS + e
    n_steps = n_super * N_EXPERTS

    def weight_copies(step):
        expert = step % N_EXPERTS
        slot = step % MOE_WBUF
        return (pltpu.make_async_copy(w1_ref.at[expert], w1_buf.at[slot], w1_sem.at[slot]),
                pltpu.make_async_copy(w2_ref.at[layer * N_EXPERTS + expert], w2_buf.at[slot], w2_sem.at[slot]))

    @pl.when((sb == 0) & (e == 0))
    def _():
        for k in range(MOE_WBUF - 1):
            for cp in weight_copies(k):
                cp.start()

    def gather(j):
        base = cbase_ref[table + j]
        buf = (j + 2) % 2
        for r in range(MOE_M):
            xg_ref[buf, pl.ds(r, 1), :] = hp_ref[pl.ds(tok_ref[0, 0, base + r], 1), :]

    def scatter(j):
        base = cbase_ref[table + j]
        buf = (j + 2) % 2
        for r in range(MOE_M):
            slots_ref[pl.ds(row_ref[0, 0, base + r], 1), :] = yb_ref[buf, pl.ds(r, 1), :]

    @pl.when(e == 0)
    def _():
        yb_ref[...] = jnp.zeros_like(yb_ref)
        gather(0)

    @pl.when(e < N_EXPERTS)
    def _():
        j0 = first_ref[sb * N_EXPERTS + e]

        @pl.when(g_step + MOE_WBUF - 1 < n_steps)
        def _():
            for cp in weight_copies(g_step + MOE_WBUF - 1):
                cp.start()

        for cp in weight_copies(g_step):
            cp.wait()
        slot = g_step % MOE_WBUF

        def chunk(j, carry):
            buf = j % 2
            xb = _unpack_halves(xg_ref[buf]).astype(BF16)
            gather(j + 1)
            scatter(j - 1)
            u = jnp.dot(xb, w1_buf[slot], preferred_element_type=F32) + b1_ref[0]
            glu = jnp.minimum(u[:, :dff], SWIGLU_LIMIT)
            lin = jnp.clip(u[:, dff:], -SWIGLU_LIMIT, SWIGLU_LIMIT)
            act = glu * _sigmoid(SWIGLU_ALPHA * glu) * (lin + 1.0)
            y = jnp.dot(act.astype(BF16), w2_buf[slot], preferred_element_type=F32) + b2_ref[0]
            yb_ref[buf] = _pack_halves(y)
            return carry

        lax.fori_loop(j0, j0 + nch_ref[sb * N_EXPERTS + e], chunk, 0)

    @pl.when(e == N_EXPERTS)
    def _():
        n_total = first_ref[sb * N_EXPERTS + N_EXPERTS - 1] + nch_ref[sb * N_EXPERTS + N_EXPERTS - 1]
        scatter(n_total - 1)

    @pl.when(e >= N_EXPERTS)
    def _():
        r0 = pl.multiple_of((e - N_EXPERTS) * rb, rb)
        acc = gate_ref[:, 0:1] * _unpack_halves(slots_ref[pl.ds(r0, rb), :])
        for k in range(1, TOP_K):
            rows = pl.ds(pl.multiple_of(k * plane + r0, SUBLANES), rb)
            acc = acc + gate_ref[:, k:k + 1] * _unpack_halves(slots_ref[rows, :])
        o_ref[...] = x_ref[...] + g2_ref[0] * acc


def _moe(xn, hp, ridx, gate, g2, w1p, b1p, w2b, b2, tb, rb, layer=0):
    bsz, s, d = xn.shape
    t = bsz * s
    n_super = t // tb
    n_piece = tb // rb
    dff = w2b.shape[1]
    flat_e = ridx[:, :, :TOP_K].reshape(n_super, tb * TOP_K)
    order = jnp.argsort(flat_e, axis=1, stable=True).astype(I32)
    counts = jnp.sum(flat_e[:, :, None] == jnp.arange(N_EXPERTS, dtype=I32)[None, None, :], axis=1).astype(I32)
    offs = (jnp.cumsum(counts, axis=1) - counts).astype(I32)
    n_list = tb * TOP_K
    plane = tb + SUBLANES
    tok_list = jnp.pad(order >> 2, ((0, 0), (0, MOE_M)))
    row_list = jnp.pad((order & (TOP_K - 1)) * plane + (order >> 2), ((0, 0), (0, MOE_M)), constant_values=tb)
    nch = (counts + MOE_M - 1) // MOE_M
    first = (jnp.cumsum(nch, axis=1) - nch).astype(I32)
    table_len = -(-(n_list // MOE_M + N_EXPERTS + 2) // SUBLANES) * SUBLANES
    j = jnp.arange(table_len - 1, dtype=I32)[None, :]
    owner = jnp.sum(j[:, :, None] >= (first + nch)[:, None, :], axis=2)
    owner_c = jnp.minimum(owner, N_EXPERTS - 1)
    within = j - jnp.take_along_axis(first, owner_c, axis=1)
    c_base = jnp.where(owner < N_EXPERTS, jnp.take_along_axis(offs, owner_c, axis=1) + within * MOE_M, n_list)
    c_base = jnp.concatenate([jnp.full((n_super, 1), n_list, I32), c_base.astype(I32)], axis=1)

    piece = lambda sb, e, *_: (sb * n_piece + jnp.maximum(e - N_EXPERTS, 0), 0)
    wmap = lambda sb, e, *_: (jnp.minimum(e, N_EXPERTS - 1), 0, 0)
    grid_spec = pltpu.PrefetchScalarGridSpec(
        num_scalar_prefetch=3,
        grid=(n_super, N_EXPERTS + n_piece),
        in_specs=[
            pl.BlockSpec((1, 1, n_list + MOE_M), lambda sb, e, *_: (sb, 0, 0), memory_space=pltpu.SMEM),
            pl.BlockSpec((1, 1, n_list + MOE_M), lambda sb, e, *_: (sb, 0, 0), memory_space=pltpu.SMEM),
            pl.BlockSpec((tb, d // 2), lambda sb, e, *_: (sb, 0), pipeline_mode=pl.Buffered(1)),
            pl.BlockSpec(memory_space=pl.ANY),
            pl.BlockSpec((1, 1, 2 * dff), wmap),
            pl.BlockSpec(memory_space=pl.ANY),
            pl.BlockSpec((1, 1, d), wmap),
            pl.BlockSpec((rb, LANES), piece),
            pl.BlockSpec((rb, d), piece),
            pl.BlockSpec((1, 1, d), lambda sb, e, *_: ((sb * tb) // s, 0, 0)),
        ],
        out_specs=pl.BlockSpec((rb, d), piece),
        scratch_shapes=[
            pltpu.VMEM((TOP_K * plane, d // 2), U32),
            pltpu.VMEM((2, MOE_M, d // 2), U32),
            pltpu.VMEM((2, MOE_M, d // 2), U32),
            pltpu.VMEM((MOE_WBUF, d, 2 * dff), BF16),
            pltpu.VMEM((MOE_WBUF, dff, d), BF16),
            pltpu.SemaphoreType.DMA((MOE_WBUF,)),
            pltpu.SemaphoreType.DMA((MOE_WBUF,)),
        ],
    )
    out = pl.pallas_call(
        functools.partial(_moe_kernel, tb=tb, rb=rb, table_len=table_len, layer=layer, n_super=n_super),
        grid_spec=grid_spec,
        out_shape=jax.ShapeDtypeStruct((t, d), F32),
        compiler_params=_cparams(("arbitrary", "arbitrary")),
        name="moe",
    )(first.reshape(-1), nch.astype(I32).reshape(-1), c_base.reshape(-1),
      tok_list.reshape(n_super, 1, n_list + MOE_M), row_list.reshape(n_super, 1, n_list + MOE_M),
      hp.reshape(t, d // 2),
      w1p, b1p, w2b, b2, gate.reshape(t, LANES), xn.reshape(t, d), g2.reshape(bsz, 1, d))
    return out.reshape(bsz, s, d)


MXU_COLS = 256


def _deinterleave_kernel(w_ref, perm_ref, o_ref):
    half = w_ref.shape[2] // 2
    hw = MXU_COLS // 2
    for b in range(w_ref.shape[2] // MXU_COLS):
        blk = w_ref[0, :, b * MXU_COLS:(b + 1) * MXU_COLS].astype(BF16)
        y = jnp.dot(blk, perm_ref[...], preferred_element_type=F32).astype(BF16)
        o_ref[0, :, b * hw:(b + 1) * hw] = y[:, :hw]
        o_ref[0, :, half + b * hw:half + (b + 1) * hw] = y[:, hw:]


def _deinterleave_cast(w1, layer, rows=512):
    depth, n_e, d, two_f = w1.shape
    src = np.concatenate([np.arange(0, MXU_COLS, 2), np.arange(1, MXU_COLS, 2)])
    perm = np.zeros((MXU_COLS, MXU_COLS), np.float32)
    perm[src, np.arange(MXU_COLS)] = 1.0
    return pl.pallas_call(
        _deinterleave_kernel,
        grid=(n_e, d // rows),
        in_specs=[pl.BlockSpec((1, rows, two_f), lambda e, r: (layer * n_e + e, r, 0)),
                  pl.BlockSpec((MXU_COLS, MXU_COLS), lambda e, r: (0, 0))],
        out_specs=pl.BlockSpec((1, rows, two_f), lambda e, r: (e, r, 0)),
        out_shape=jax.ShapeDtypeStruct((n_e, d, two_f), BF16),
        compiler_params=_cparams(("parallel", "parallel")),
        name="w1_deinterleave",
    )(w1.reshape(depth * n_e, d, two_f), jnp.asarray(perm, BF16))


def _deinterleave_bias(b1_l):
    n_e, two_f = b1_l.shape
    return jnp.concatenate([b1_l[:, 0::2], b1_l[:, 1::2]], axis=1).reshape(n_e, 1, two_f)


def kernel(x, c, rel_bias, mod_w, mod_b, norm_mix_w, norm_ffn_w, w_in, conv_w, a_log, dt_bias, gdn_norm_w,
           q_norm_w, k_norm_w, w_out, router_w, router_b, w1, b1, w2, b2):
    depth = mod_w.shape[0]
    bsz, s, d = x.shape
    mod = _modulation(c, mod_w, mod_b)
    tm = min(512, s)
    tb = min(MOE_TB, bsz * s)
    rb = min(MOE_RB, tb)
    n_e, dff = w2.shape[1], w2.shape[2]
    w2b = w2.astype(BF16).reshape(depth * n_e, dff, d)
    for l in range(depth):
        qkv_a, z_a, qb, kb, vb, qi, small = _inproj(
            x, mod[l], norm_mix_w[l], _permute_w_in(w_in[l]), q_norm_w[l], k_norm_w[l], tm)
        y_a = _gdn(qkv_a, z_a, small, conv_w[l], a_log[l], dt_bias[l], gdn_norm_w[l], sb=tm)
        y_b = _dsa(qb, kb, vb, qi, small, rel_bias)
        xn, hp, ridx, gate = _outproj(y_a, y_b, x, mod[l], w_out[l].astype(BF16), norm_ffn_w[l],
                                      router_w[l], router_b[l], tm)
        x = _moe(xn, hp, ridx, gate, mod[l][:, 5], _deinterleave_cast(w1, l), _deinterleave_bias(b1[l]), w2b,
                 b2[l].reshape(n_e, 1, d), tb, rb, layer=l)
    return x
```

```python
import functools
import math

import jax
import jax.numpy as jnp
import numpy as np
from jax import lax
from jax.experimental import pallas as pl
from jax.experimental.pallas import tpu as pltpu

F32 = jnp.float32
BF16 = jnp.bfloat16
I32 = jnp.int32
U32 = jnp.uint32
HIGHEST = lax.Precision.HIGHEST

LANES = 128
SUBLANES = 8
VMEM_LIMIT_BYTES = 56 * 1024 * 1024

CHUNK = 64
HEAD_DIM_A = 128
N_HEADS_A = 4
WIDTH_A = N_HEADS_A * HEAD_DIM_A
CONV_K = 4
HEAD_DIM_B = 64
N_HEADS_B = 8
WIDTH_B = N_HEADS_B * HEAD_DIM_B
IDX_HEADS = 8
IDX_DIM = 64
TOPK_KEYS_MAX = 256
REL_BUCKETS = 32
REL_MAX_DIST = 1024
N_EXPERTS = 32
TOP_K = 4
SWIGLU_ALPHA = 1.702
SWIGLU_LIMIT = 7.0
EPS = 1e-6
NEG_BIG = -1e30

C_QKVA = 0
C_Z = C_QKVA + 3 * WIDTH_A
C_QB = C_Z + WIDTH_A
C_KB = C_QB + WIDTH_B
C_VB = C_KB + WIDTH_B
C_QI = C_VB + WIDTH_B
C_SMALL = C_QI + IDX_HEADS * IDX_DIM
D_IN_PAD = C_SMALL + LANES
S_KIDX = 0
S_B = IDX_DIM
S_A = S_B + N_HEADS_A
S_WIDX = S_A + N_HEADS_A


def _cparams(sem):
    return pltpu.CompilerParams(dimension_semantics=sem, vmem_limit_bytes=VMEM_LIMIT_BYTES)


def _sigmoid(x):
    return jax.nn.sigmoid(x)


def _silu(x):
    return x * _sigmoid(x)


def _softplus(x):
    return jnp.maximum(x, 0.0) + jnp.log(1.0 + jnp.exp(-jnp.abs(x)))


def _mod_kernel(c_ref, w_ref, b_ref, o_ref):
    a = _silu(c_ref[...])
    o_ref[0] = jnp.dot(a, w_ref[0], precision=HIGHEST, preferred_element_type=F32) + b_ref[0]


def _modulation(c, mod_w, mod_b):
    depth, d, n = mod_w.shape
    bsz = c.shape[0]
    rows = -(-bsz // SUBLANES) * SUBLANES
    c_pad = jnp.zeros((rows, d), F32).at[:bsz].set(c)
    tn = 1536
    out = pl.pallas_call(
        _mod_kernel,
        grid=(depth, n // tn),
        in_specs=[
            pl.BlockSpec((rows, d), lambda l, j: (0, 0)),
            pl.BlockSpec((1, d, tn), lambda l, j: (l, 0, j)),
            pl.BlockSpec((1, 1, tn), lambda l, j: (l, 0, j)),
        ],
        out_specs=pl.BlockSpec((1, rows, tn), lambda l, j: (l, 0, j)),
        out_shape=jax.ShapeDtypeStruct((depth, rows, n), F32),
        compiler_params=_cparams(("arbitrary", "arbitrary")),
        name="adaln_mod",
    )(c_pad, mod_w, mod_b.reshape(depth, 1, n))
    return out[:, :bsz].reshape(depth, bsz, 6, d)


def _head_rms(t, group_ref, wn, inv_dim):
    t2 = t * t
    hi = t2.astype(BF16)
    lo = (t2 - hi.astype(F32)).astype(BF16)
    ss = (jnp.dot(hi, group_ref[...], preferred_element_type=F32)
          + jnp.dot(lo, group_ref[...], preferred_element_type=F32))
    return t * lax.rsqrt(ss * inv_dim + EPS) * wn


def _inproj_kernel(x_ref, mod_ref, nw_ref, w_ref, group_ref, qn_ref, kn_ref,
                   qkva_ref, z_ref, qb_ref, kb_ref, vb_ref, qi_ref, small_ref):
    x = x_ref[0]
    ms = jnp.mean(x * x, axis=-1, keepdims=True)
    y = x * lax.rsqrt(ms + EPS) * nw_ref[...]
    h = y * (1.0 + mod_ref[0, 1:2, :]) + mod_ref[0, 0:1, :]
    hb = h.astype(BF16)

    def mm(lo, width):
        return jnp.dot(hb, w_ref[:, lo:lo + width], preferred_element_type=F32)

    qkva_ref[0] = mm(C_QKVA, 3 * WIDTH_A)
    z_ref[0] = mm(C_Z, WIDTH_A)
    q = _head_rms(mm(C_QB, WIDTH_B), group_ref, qn_ref[...], 1.0 / HEAD_DIM_B)
    qb_ref[0] = (q * (HEAD_DIM_B ** -0.5 * LOG2E)).astype(BF16)
    k = _head_rms(mm(C_KB, WIDTH_B), group_ref, kn_ref[...], 1.0 / HEAD_DIM_B)
    kb_ref[0] = k.astype(BF16)
    v = mm(C_VB, WIDTH_B)
    lane = lax.broadcasted_iota(I32, (v.shape[0], LANES), 1)
    tail = (lane == HEAD_DIM_B).astype(F32)
    for p in range(N_HEADS_B // 2):
        pair = v[:, p * LANES:(p + 1) * LANES]
        vb_ref[0, :, (2 * p) * LANES:(2 * p + 1) * LANES] = jnp.where(lane < HEAD_DIM_B, pair, tail).astype(BF16)
        vb_ref[0, :, (2 * p + 1) * LANES:(2 * p + 2) * LANES] = jnp.where(
            lane < HEAD_DIM_B, pltpu.roll(pair, HEAD_DIM_B, axis=1), tail).astype(BF16)
    qi_ref[0] = mm(C_QI, IDX_HEADS * IDX_DIM).astype(BF16)
    small_ref[0] = mm(C_SMALL, LANES)


def _permute_w_in(w_in_l):
    d = w_in_l.shape[0]
    o = 0
    qkva = w_in_l[:, o:o + 3 * WIDTH_A]; o += 3 * WIDTH_A
    z = w_in_l[:, o:o + WIDTH_A]; o += WIDTH_A
    b = w_in_l[:, o:o + N_HEADS_A]; o += N_HEADS_A
    a = w_in_l[:, o:o + N_HEADS_A]; o += N_HEADS_A
    qkvb = w_in_l[:, o:o + 3 * WIDTH_B]; o += 3 * WIDTH_B
    qi = w_in_l[:, o:o + IDX_HEADS * IDX_DIM]; o += IDX_HEADS * IDX_DIM
    ki = w_in_l[:, o:o + IDX_DIM]; o += IDX_DIM
    wi = w_in_l[:, o:o + IDX_HEADS]; o += IDX_HEADS
    pad = jnp.zeros((d, LANES - IDX_DIM - 2 * N_HEADS_A - IDX_HEADS), w_in_l.dtype)
    return jnp.concatenate([qkva, z, qkvb, qi, ki, b, a, wi, pad], axis=1).astype(BF16)


def _group_ones(width, group):
    g = np.arange(width) // group
    return jnp.asarray((g[:, None] == g[None, :]).astype(np.float32), dtype=BF16)


def _inproj(x, mod_l, norm_w, w_perm, q_norm_w, k_norm_w, tm):
    bsz, s, d = x.shape
    f = lambda b, i: (b, i, 0)
    const2 = lambda b, i: (0, 0)
    outs = [
        (3 * WIDTH_A, F32), (WIDTH_A, F32), (WIDTH_B, BF16), (WIDTH_B, BF16), (N_HEADS_B * LANES, BF16),
        (IDX_HEADS * IDX_DIM, BF16), (LANES, F32),
    ]
    return pl.pallas_call(
        _inproj_kernel,
        grid=(bsz, s // tm),
        in_specs=[
            pl.BlockSpec((1, tm, d), f),
            pl.BlockSpec((1, 6, d), lambda b, i: (b, 0, 0)),
            pl.BlockSpec((1, d), const2),
            pl.BlockSpec((d, D_IN_PAD), const2),
            pl.BlockSpec((WIDTH_B, WIDTH_B), const2),
            pl.BlockSpec((1, WIDTH_B), const2),
            pl.BlockSpec((1, WIDTH_B), const2),
        ],
        out_specs=[pl.BlockSpec((1, tm, w), f) for w, _ in outs],
        out_shape=[jax.ShapeDtypeStruct((bsz, s, w), dt) for w, dt in outs],
        compiler_params=_cparams(("parallel", "parallel")),
        name="inproj",
    )(x, mod_l, norm_w.reshape(1, d), w_perm, _group_ones(WIDTH_B, HEAD_DIM_B),
      jnp.tile(q_norm_w, N_HEADS_B).reshape(1, WIDTH_B), jnp.tile(k_norm_w, N_HEADS_B).reshape(1, WIDTH_B))


def _dot_nt(a, b, precision=None):
    return lax.dot_general(a, b, (((1,), (1,)), ((), ())), precision=precision, preferred_element_type=F32)


def _split2(x):
    hi = x.astype(BF16)
    return hi, (x - hi.astype(F32)).astype(BF16)


def _split3(x):
    hi = x.astype(BF16)
    r = x - hi.astype(F32)
    mid = r.astype(BF16)
    return hi, mid, (r - mid.astype(F32)).astype(BF16)


def _mm3(a, b):
    return (jnp.dot(a[0], b[0], preferred_element_type=F32) + jnp.dot(a[0], b[1], preferred_element_type=F32)
            + jnp.dot(a[1], b[0], preferred_element_type=F32))


GDN_PAR = 8


def _gdn_kernel(qkv_ref, z_ref, small_ref, convw_ref, alog_ref, dtb_ref, nw_ref,
                y_ref, xe_ref, u_ref, state_ref, uval_ref, wdec_ref, qg_ref, kdec_ref, attn_ref, egl_ref, *, sb):
    n_chunks = sb // CHUNK
    halo = SUBLANES

    @pl.when(pl.program_id(1) == 0)
    def _():
        xe_ref[0:halo, :] = jnp.zeros((halo, 3 * WIDTH_A), F32)
        state_ref[...] = jnp.zeros_like(state_ref)

    xe_ref[halo:halo + sb, :] = qkv_ref[0]

    rows = 128
    for g in range(3 * WIDTH_A // LANES):
        cs = slice(g * LANES, (g + 1) * LANES)
        for r in range(sb // rows):
            base = halo - (CONV_K - 1) + r * rows
            acc = xe_ref[base:base + rows, cs] * convw_ref[0:1, cs]
            for j in range(1, CONV_K):
                acc = acc + xe_ref[base + j:base + j + rows, cs] * convw_ref[j:j + 1, cs]
            u_ref[r * rows:(r + 1) * rows, cs] = _silu(acc)

    xe_ref[0:halo, :] = xe_ref[sb:sb + halo, :]

    wide = N_HEADS_A * CHUNK
    heads = range(N_HEADS_A)
    ii = lax.broadcasted_iota(I32, (CHUNK, wide), 0)
    jj = lax.broadcasted_iota(I32, (CHUNK, wide), 1) & (CHUNK - 1)
    eye_w = (ii == jj).astype(F32)
    tri = (lax.broadcasted_iota(I32, (CHUNK, CHUNK), 0)
           >= lax.broadcasted_iota(I32, (CHUNK, CHUNK), 1)).astype(F32).astype(BF16)
    shift = int(math.log2(CHUNK))
    bd_mask = ((lax.broadcasted_iota(I32, (wide, wide), 0) >> shift)
               == (lax.broadcasted_iota(I32, (wide, wide), 1) >> shift)).astype(F32)

    bd_mask = bd_mask.astype(BF16)

    def block_diag(parts):
        return tuple(jnp.concatenate([m] * N_HEADS_A, axis=0) * bd_mask for m in parts)

    def prepare(c):
        rs = pl.ds(pl.multiple_of(c * CHUNK, CHUNK), CHUNK)
        qn, kn, v, beta, g_b = [], [], [], [], []
        for h in heads:
            q = u_ref[rs, h * HEAD_DIM_A:(h + 1) * HEAD_DIM_A]
            k = u_ref[rs, WIDTH_A + h * HEAD_DIM_A:WIDTH_A + (h + 1) * HEAD_DIM_A]
            v.append(u_ref[rs, 2 * WIDTH_A + h * HEAD_DIM_A:2 * WIDTH_A + (h + 1) * HEAD_DIM_A])
            qn.append(q * (lax.rsqrt(jnp.sum(q * q, axis=-1, keepdims=True) + EPS) * (HEAD_DIM_A ** -0.5)))
            kn.append(k * lax.rsqrt(jnp.sum(k * k, axis=-1, keepdims=True) + EPS))
            beta.append(_sigmoid(small_ref[0, rs, S_B + h:S_B + h + 1]))
            g = -jnp.exp(alog_ref[0:1, h:h + 1]) * _softplus(small_ref[0, rs, S_A + h:S_A + h + 1]
                                                            + dtb_ref[0:1, h:h + 1])
            g_b.append(jnp.broadcast_to(g, (CHUNK, CHUNK)))
        gc_w = sum(jnp.dot(tri, part, preferred_element_type=F32) for part in _split3(jnp.concatenate(g_b, axis=1)))
        gc_row = jnp.sum(jnp.where(ii == jj, gc_w, 0.0), axis=0, keepdims=True)
        decay_w = jnp.exp(jnp.where(ii >= jj, gc_w - gc_row, NEG_BIG))
        k_beta = [kn[h] * beta[h] for h in heads]
        kk_w = jnp.concatenate([_dot_nt(k_beta[h].astype(BF16), kn[h].astype(BF16)) for h in heads], axis=1)
        a_w = -jnp.where(ii > jj, kk_w * decay_w, 0.0)
        gc = [gc_w[:, h * CHUNK:h * CHUNK + 1] for h in heads]
        egc = [jnp.exp(gc[h]) for h in heads]
        qk = [_dot_nt(qn[h].astype(BF16), kn[h].astype(BF16)) for h in heads]
        for h in heads:
            g_last = gc[h][CHUNK - 1:CHUNK, :]
            qg_ref[c, h] = (qn[h] * egc[h]).astype(BF16)
            kdec_ref[c, h] = kn[h] * jnp.exp(g_last - gc[h])
            attn_ref[c, h] = (qk[h] * decay_w[:, h * CHUNK:(h + 1) * CHUNK]).astype(BF16)
            egl_ref[c, h] = jnp.broadcast_to(jnp.exp(g_last), (SUBLANES, LANES))
        return a_w, [_split2(v[h] * beta[h]) for h in heads], [_split2(k_beta[h] * egc[h]) for h in heads]

    def solve_body(cg, carry):
        group = range(GDN_PAR)
        chunks = [cg * GDN_PAR + i for i in group]
        pre = [prepare(c) for c in chunks]
        t_w = [eye_w + pre[i][0] for i in group]
        p_parts = [_split2(pre[i][0]) for i in group]
        bd = [block_diag(p_parts[i]) for i in group]
        for _ in range(shift - 1):
            prod = [_mm3(p_parts[i], bd[i]) for i in group]
            p_parts = [_split2(prod[i]) for i in group]
            bd = [block_diag(p_parts[i]) for i in group]
            upd = [_mm3(_split2(t_w[i]), bd[i]) for i in group]
            t_w = [t_w[i] + upd[i] for i in group]
        for i in group:
            t_h = [_split2(t_w[i][:, h * CHUNK:(h + 1) * CHUNK]) for h in heads]
            u_val = [_mm3(t_h[h], pre[i][1][h]) for h in heads]
            w_dec = [_mm3(t_h[h], pre[i][2][h]) for h in heads]
            for h in heads:
                uval_ref[chunks[i], h] = u_val[h]
                wdec_ref[chunks[i], h] = w_dec[h].astype(BF16)
        return carry

    lax.fori_loop(0, n_chunks // GDN_PAR, solve_body, 0)

    def scan_body(c, carry):
        rs = pl.ds(pl.multiple_of(c * CHUNK, CHUNK), CHUNK)
        state = [state_ref[h] for h in heads]
        state_b = [s_h.astype(BF16) for s_h in state]
        w_s = [jnp.dot(wdec_ref[c, h], state_b[h], preferred_element_type=F32) for h in heads]
        v_new = [(uval_ref[c, h] - w_s[h]).astype(BF16) for h in heads]
        o = [jnp.dot(qg_ref[c, h], state_b[h], preferred_element_type=F32)
             + jnp.dot(attn_ref[c, h], v_new[h], preferred_element_type=F32) for h in heads]
        for h in heads:
            state_ref[h] = (state[h] * egl_ref[c, h][0:1, 0:1]
                            + jnp.dot(kdec_ref[c, h].T.astype(BF16), v_new[h], preferred_element_type=F32))
        for h in heads:
            hs = slice(h * HEAD_DIM_A, (h + 1) * HEAD_DIM_A)
            on = o[h] * lax.rsqrt(jnp.mean(o[h] * o[h], axis=-1, keepdims=True) + EPS) * nw_ref[...]
            y_ref[0, rs, hs] = (on * _silu(z_ref[0, rs, hs])).astype(BF16)
        return carry

    lax.fori_loop(0, n_chunks, scan_body, 0)


def _gdn(qkv_a, z_a, small, conv_w, a_log, dt_bias, norm_w, sb):
    bsz, s, _ = qkv_a.shape
    n_c = sb // CHUNK
    assert n_c % GDN_PAR == 0 and s % sb == 0
    per_head = (n_c, N_HEADS_A, CHUNK, HEAD_DIM_A)
    f = lambda b, i: (b, i, 0)
    const2 = lambda b, i: (0, 0)
    return pl.pallas_call(
        functools.partial(_gdn_kernel, sb=sb),
        grid=(bsz, s // sb),
        in_specs=[
            pl.BlockSpec((1, sb, 3 * WIDTH_A), f),
            pl.BlockSpec((1, sb, WIDTH_A), f),
            pl.BlockSpec((1, sb, LANES), f),
            pl.BlockSpec((CONV_K, 3 * WIDTH_A), const2),
            pl.BlockSpec((1, N_HEADS_A), const2),
            pl.BlockSpec((1, N_HEADS_A), const2),
            pl.BlockSpec((1, HEAD_DIM_A), const2),
        ],
        out_specs=pl.BlockSpec((1, sb, WIDTH_A), f),
        out_shape=jax.ShapeDtypeStruct((bsz, s, WIDTH_A), BF16),
        scratch_shapes=[
            pltpu.VMEM((sb + SUBLANES, 3 * WIDTH_A), F32),
            pltpu.VMEM((sb, 3 * WIDTH_A), F32),
            pltpu.VMEM((N_HEADS_A, HEAD_DIM_A, HEAD_DIM_A), F32),
            pltpu.VMEM(per_head, F32),
            pltpu.VMEM(per_head, BF16),
            pltpu.VMEM(per_head, BF16),
            pltpu.VMEM(per_head, F32),
            pltpu.VMEM((n_c, N_HEADS_A, CHUNK, CHUNK), BF16),
            pltpu.VMEM((n_c, N_HEADS_A, SUBLANES, LANES), F32),
        ],
        compiler_params=_cparams(("parallel", "arbitrary")),
        name="gdn",
    )(qkv_a, z_a, small, conv_w, a_log.reshape(1, -1), dt_bias.reshape(1, -1), norm_w.reshape(1, -1))


QB = 128
FAR_T = 512
FAR_G = FAR_T // LANES
ATT_W = 256
LOG2E = 1.4426950408889634
NEAR_D = 9
NEAR_MIN = 5
INT_MIN = -2 ** 31


def _t5_bucket_np(rel):
    nb = REL_BUCKETS // 2
    max_exact = nb // 2
    side = np.where(rel > 0, nb, 0)
    n = np.abs(rel)
    nf = np.maximum(n, 1).astype(np.float32)
    large = max_exact + (np.log(nf / np.float32(max_exact)) / np.float32(math.log(REL_MAX_DIST / max_exact))
                         * np.float32(nb - max_exact)).astype(np.int32)
    large = np.minimum(large, nb - 1)
    return (side + np.where(n < max_exact, n, large)).astype(np.int32)


def _near_bucket_table():
    r = np.arange(QB)[:, None]
    c = np.arange(LANES)[None, :]
    return np.stack([_t5_bucket_np(c - r - LANES * d) for d in range(NEAR_D)])


FAR_BUCKET = int(_t5_bucket_np(np.array([-(NEAR_MIN * LANES + 1)]))[0])
assert all(int(b) == FAR_BUCKET for b in _t5_bucket_np(-np.arange((NEAR_MIN + 1) * LANES - (QB - 1), 1 << 20, 997)))


def _sortable_key(score):
    bits = pltpu.bitcast(score + 0.0, I32)
    return bits ^ ((bits >> 31) & 0x7FFFFFFF)


PLANE_G = 32


def _bit_transpose32(words):
    a = list(words)
    mask, j = 0x0000FFFF, 16
    while j:
        k = 0
        while k < 32:
            t = (a[k] ^ lax.shift_right_logical(a[k + j], jnp.int32(j))) & mask
            a[k] = a[k] ^ t
            a[k + j] = a[k + j] ^ lax.shift_left(t, jnp.int32(j))
            k = (k + j + 1) & ~j
        j >>= 1
        mask = (mask ^ (mask << j)) & 0xFFFFFFFF
    return a


def _dsa_kernel(rb_ref, qb_ref, qi_ref, small_ref, kb_ref, va_ref, kidx2_ref, tab_ref,
                y_ref, qis_ref, qs_ref, wb_ref, keys_ref, nbias_ref, thr_ref, jlim_ref,
                m_ref, acc_ref, s_ref, p_ref, peak_ref, planes_ref, need_ref, excess_ref, *, seq, k_sel):
    i = pl.program_id(1)
    lane = lax.broadcasted_iota(I32, (QB, LANES), 1)
    row = lax.broadcasted_iota(I32, (QB, LANES), 0)
    even_f = (lane < HEAD_DIM_B).astype(F32)
    even_b = even_f.astype(BF16)
    odd_b = (1.0 - even_f).astype(BF16)

    @pl.when(i == 0)
    def _():
        nbias_ref[...] = jnp.zeros_like(nbias_ref)
        keys_ref[...] = jnp.full(keys_ref.shape, INT_MIN, I32)

        def d_body(d, c0):
            tab = tab_ref[d]

            def b_body(bk, c1):
                hit = tab == bk
                for h in range(N_HEADS_B):
                    nbias_ref[d * N_HEADS_B + h] = jnp.where(hit, rb_ref[bk, h] * LOG2E,
                                                             nbias_ref[d * N_HEADS_B + h])
                return c1

            return lax.fori_loop(0, REL_BUCKETS, b_body, c0)

        lax.fori_loop(0, NEAR_D, d_body, 0)

    for p in range(N_HEADS_B // 2):
        ps = slice(p * LANES, (p + 1) * LANES)
        qi_pair = qi_ref[0, :, ps]
        qis_ref[(2 * p) * QB:(2 * p + 1) * QB, :] = qi_pair * even_b
        qis_ref[(2 * p + 1) * QB:(2 * p + 2) * QB, :] = qi_pair * odd_b
        q_pair = qb_ref[0, :, ps]
        qs_ref[(2 * p) * QB:(2 * p + 1) * QB, :] = q_pair * even_b
        qs_ref[(2 * p + 1) * QB:(2 * p + 2) * QB, :] = q_pair * odd_b
    w_scale = IDX_HEADS ** -0.5 * IDX_DIM ** -0.5
    for h in range(IDX_HEADS):
        wb_ref[h] = jnp.broadcast_to(small_ref[0, :, S_WIDX + h:S_WIDX + h + 1] * w_scale, (QB, LANES))

    limit = i * QB + CHUNK + jnp.where(row >= CHUNK, CHUNK, 0)

    def score_tile(t):
        for c in range(FAR_T // ATT_W):
            k0 = t * FAR_T + c * ATT_W
            k_part = kidx2_ref[0, pl.ds(pl.multiple_of(k0, ATT_W), ATT_W), :]
            acc = None
            for h in range(IDX_HEADS):
                dots = _dot_nt(qis_ref[h * QB:(h + 1) * QB, :], k_part)
                term = jnp.maximum(dots, 0.0) * jnp.concatenate([wb_ref[h]] * (ATT_W // LANES), axis=1)
                acc = term if acc is None else acc + term
            for gg in range(ATT_W // LANES):
                col = k0 + gg * LANES + lane
                keys_ref[(k0 // LANES) + gg] = jnp.where(
                    col < limit, _sortable_key(acc[:, gg * LANES:(gg + 1) * LANES]), INT_MIN)

    def score_pair_body(u, c0):
        score_tile(2 * u)
        score_tile(2 * u + 1)
        return c0

    def score_body(t, c0):
        score_tile(t)
        return c0

    n_groups = i + 1
    n_tiles = i // FAR_G + 1
    lax.fori_loop(0, n_tiles // 2, score_pair_body, 0)
    lax.fori_loop(2 * (n_tiles // 2), n_tiles, score_body, 0)

    @pl.when(n_tiles % 2 == 1)
    def _():
        for g in range(FAR_G):
            keys_ref[n_tiles * FAR_G + g] = jnp.full((QB, LANES), INT_MIN, I32)

    def count(pred):
        def t_body(t, acc):
            for g in range(2 * FAR_G):
                grp = t * (2 * FAR_G) + g
                acc = acc + jnp.where(pred(keys_ref[grp], grp * LANES + lane), 1, 0)
            return acc

        acc = lax.fori_loop(0, (n_tiles + 1) // 2, t_body, jnp.zeros((QB, LANES), I32))
        return jnp.broadcast_to(jnp.sum(acc, axis=1, keepdims=True), (QB, LANES))

    thr_ref[...] = jnp.full((QB, LANES), INT_MIN, I32)
    jlim_ref[...] = jnp.full((QB, LANES), -1, I32)

    def lane_total(x):
        return jnp.broadcast_to(jnp.sum(x, axis=1, keepdims=True), (QB, LANES))

    @pl.when(n_groups * QB > k_sel)
    def _():
        def transpose_half(half):
            def row_body(rr, c0):
                rows = pl.ds(pl.multiple_of(rr * SUBLANES, SUBLANES), SUBLANES)
                planes = _bit_transpose32([keys_ref[half * PLANE_G + g, rows, :] ^ INT_MIN for g in range(PLANE_G)])
                for b in range(32):
                    planes_ref[half, b, rows, :] = planes[b]
                return c0

            lax.fori_loop(0, QB // SUBLANES, row_body, 0)

        def search(n_half):
            halves = range(n_half)
            row_groups = (pl.ds(0, QB // 2), pl.ds(QB // 2, QB // 2))
            shape = (QB // 2, LANES)

            def total(x):
                return jnp.broadcast_to(jnp.sum(x, axis=1, keepdims=True), shape)

            def digit(step, rows, state):
                r, eq, above = state
                cls = []
                for half in halves:
                    hi = eq[half] & planes_ref[half, 2 * step, rows, :]
                    lo_plane = planes_ref[half, 2 * step + 1, rows, :]
                    zero_hi = eq[half] ^ hi
                    e11 = hi & lo_plane
                    e01 = zero_hi & lo_plane
                    cls.append((e11, hi ^ e11, e01, zero_hi ^ e01))
                n11, n10, n01 = [sum(lax.population_count(cls[h][d]) for h in halves) for d in range(3)]
                c3 = above + n11
                c2 = c3 + n10
                c1 = c2 + n01
                d3, d2, d1 = [total(c) >= k_sel for c in (c3, c2, c1)]
                eq = tuple(jnp.where(d3, cls[h][0], jnp.where(d2, cls[h][1], jnp.where(d1, cls[h][2], cls[h][3])))
                           for h in halves)
                above = jnp.where(d3, above, jnp.where(d2, c3, jnp.where(d1, c2, c1)))
                value = jnp.where(d3, 3, jnp.where(d2, 2, jnp.where(d1, 1, 0)))
                return r | lax.shift_left(value, 30 - 2 * step), eq, above

            def digit_body(step, carry):
                return tuple(digit(step, rows, state) for rows, state in zip(row_groups, carry))

            zero = jnp.zeros(shape, I32)
            full = jnp.full(shape, -1, I32)
            start = (zero, (full,) * n_half, zero)
            for rows, (r, eq, above) in zip(row_groups, lax.fori_loop(0, 16, digit_body, (start, start))):
                thr_ref[rows, :] = r ^ INT_MIN
                need = k_sel - total(above)
                need_ref[rows, :] = need
                excess_ref[rows, :] = total(sum(lax.population_count(eq[h]) for h in halves)) - need

        transpose_half(0)

        @pl.when(n_groups > PLANE_G)
        def _():
            transpose_half(1)
            search(2)

        @pl.when(n_groups <= PLANE_G)
        def _():
            search(1)

        r = thr_ref[...]
        need = need_ref[...]
        excess = excess_ref[...]
        jlim_ref[...] = jnp.where(r == INT_MIN, -1, seq)

        @pl.when(jnp.max(excess) > 0)
        def _():
            def j_body(step, jl):
                cand = jl + lax.shift_left(jnp.int32(1), (seq.bit_length() - 1) - step)
                cnt = count(lambda kt, col: (kt == r) & (col < cand))
                return jnp.where(cnt < need, cand, jl)

            jl = lax.fori_loop(0, seq.bit_length(), j_body, jnp.zeros((QB, LANES), I32))
            jlim_ref[...] = jnp.where(r == INT_MIN, -1, jl)

    m_ref[...] = jnp.full(m_ref.shape, NEG_BIG, F32)
    acc_ref[...] = jnp.zeros_like(acc_ref)

    part_g = ATT_W // LANES

    def tile_mask(g0, groups):
        negm = []
        for g in range(groups):
            kt = keys_ref[g0 + g]
            sel = (kt > thr_ref[...]) | ((kt == thr_ref[...]) & ((g0 + g) * LANES + lane <= jlim_ref[...]))
            negm.append(jnp.where(sel, 0.0, -jnp.inf))
        return negm

    def logits_pair(g0, groups, p, negm, near, buf):
        heads = ((2 * p, slice(0, QB)), (2 * p + 1, slice(QB, 2 * QB)))
        peak = [None, None]
        for c in range(groups // part_g):
            ks = pl.ds(pl.multiple_of((g0 + c * part_g) * LANES, ATT_W), ATT_W)
            s = _dot_nt(qs_ref[2 * p * QB:(2 * p + 2) * QB, :], kb_ref[0, ks, p * LANES:(p + 1) * LANES])
            for gg in range(part_g):
                g = c * part_g + gg
                for n, (h, rows) in enumerate(heads):
                    extra = negm[g]
                    if near:
                        extra = extra + nbias_ref[jnp.clip(i - (g0 + g), 0, NEAR_D - 1) * N_HEADS_B + h]
                    v = s[rows, gg * LANES:(gg + 1) * LANES] + extra
                    s_ref[buf, h, g] = v
                    peak[n] = v if peak[n] is None else jnp.maximum(peak[n], v)
        for n, (h, _) in enumerate(heads):
            peak_ref[buf, h] = peak[n]

    def softmax_head(h, groups, bias_scalar, buf):
        m_old = m_ref[h]
        m_new = jnp.maximum(m_old, jnp.max(peak_ref[buf, h], axis=1, keepdims=True) + bias_scalar)
        shift = jnp.concatenate([m_new - bias_scalar] * part_g, axis=1)
        for c in range(groups // part_g):
            part = jnp.concatenate([s_ref[buf, h, c * part_g + g] for g in range(part_g)], axis=1)
            p_ref[buf, h, :, c * ATT_W:(c + 1) * ATT_W] = jnp.exp2(part - shift).astype(BF16)
        acc_ref[h] = jnp.exp2(m_old - m_new) * acc_ref[h]
        m_ref[h] = m_new

    def pv_head(g0, groups, h, buf):
        ks = pl.ds(pl.multiple_of(g0 * LANES, FAR_T), groups * LANES)
        acc_ref[h] += jnp.dot(p_ref[buf, h, :, :groups * LANES], va_ref[0, ks, h * LANES:(h + 1) * LANES],
                              preferred_element_type=F32)

    n_pair = N_HEADS_B // 2
    n_stage = n_pair + 2

    def stage(t, near, negm, step, buf):
        g0 = t * FAR_G
        if step < n_pair:
            logits_pair(g0, FAR_G, step, negm, near, buf)
        if 1 <= step <= n_pair:
            for h in (2 * step - 2, 2 * step - 1):
                softmax_head(h, FAR_G, 0.0 if near else rb_ref[FAR_BUCKET, h] * LOG2E, buf)
        if step >= 2:
            for h in (2 * step - 4, 2 * step - 3):
                pv_head(g0, FAR_G, h, buf)

    def tile_body(near):
        def body(t, c0):
            negm = tile_mask(t * FAR_G, FAR_G)
            for step in range(n_stage):
                stage(t, near, negm, step, 0)
            return c0

        return body

    def tile_pair_body(near, start):
        def body(u, c0):
            first, second = start + 2 * u, start + 2 * u + 1
            masks = (tile_mask(first * FAR_G, FAR_G), tile_mask(second * FAR_G, FAR_G))
            lag = n_stage // 2
            for slot in range(n_stage + lag):
                if slot < n_stage:
                    stage(first, near, masks[0], slot, 0)
                if lag <= slot:
                    stage(second, near, masks[1], slot - lag, 1)
            return c0

        return body

    far_tiles = jnp.maximum(i - NEAR_MIN, 0) // FAR_G
    far_pairs = far_tiles // 2
    near_pairs = (n_tiles - far_tiles) // 2
    lax.fori_loop(0, far_pairs, tile_pair_body(False, 0), 0)
    lax.fori_loop(2 * far_pairs, far_tiles, tile_body(False), 0)
    lax.fori_loop(0, near_pairs, tile_pair_body(True, far_tiles), 0)
    lax.fori_loop(far_tiles + 2 * near_pairs, n_tiles, tile_body(True), 0)

    def head_out(h):
        a = acc_ref[h]
        return a * (1.0 / a[:, HEAD_DIM_B:HEAD_DIM_B + 1])

    for p in range(N_HEADS_B // 2):
        o_odd = pltpu.roll(head_out(2 * p + 1), HEAD_DIM_B, axis=1)
        y_ref[0, :, p * LANES:(p + 1) * LANES] = jnp.where(lane < HEAD_DIM_B, head_out(2 * p), o_odd).astype(BF16)


def _dsa(qb, kb, va, qi, small, rel_bias):
    bsz, s, _ = qb.shape
    assert s % FAR_T == 0 and s // LANES <= 2 * PLANE_G
    k_sel = min(TOPK_KEYS_MAX, s // 4)
    kidx = small[:, :, S_KIDX:S_KIDX + IDX_DIM].astype(BF16)
    kidx2 = jnp.concatenate([kidx, kidx], axis=-1)
    tab = jnp.asarray(_near_bucket_table())
    blk = lambda b, i: (b, i, 0)
    full = lambda b, i: (b, 0, 0)
    one = pl.Buffered(1)
    return pl.pallas_call(
        functools.partial(_dsa_kernel, seq=s, k_sel=k_sel),
        grid=(bsz, s // QB),
        in_specs=[
            pl.BlockSpec(memory_space=pltpu.SMEM),
            pl.BlockSpec((1, QB, WIDTH_B), blk),
            pl.BlockSpec((1, QB, IDX_HEADS * IDX_DIM), blk),
            pl.BlockSpec((1, QB, LANES), blk),
            pl.BlockSpec((1, s, WIDTH_B), full, pipeline_mode=one),
            pl.BlockSpec((1, s, N_HEADS_B * LANES), full, pipeline_mode=one),
            pl.BlockSpec((1, s, LANES), full, pipeline_mode=one),
            pl.BlockSpec((NEAR_D, QB, LANES), lambda b, i: (0, 0, 0), pipeline_mode=one),
        ],
        out_specs=pl.BlockSpec((1, QB, WIDTH_B), blk),
        out_shape=jax.ShapeDtypeStruct((bsz, s, WIDTH_B), BF16),
        scratch_shapes=[
            pltpu.VMEM((IDX_HEADS * QB, LANES), BF16),
            pltpu.VMEM((N_HEADS_B * QB, LANES), BF16),
            pltpu.VMEM((IDX_HEADS, QB, LANES), F32),
            pltpu.VMEM((2 * PLANE_G, QB, LANES), I32),
            pltpu.VMEM((NEAR_D * N_HEADS_B, QB, LANES), F32),
            pltpu.VMEM((QB, LANES), I32),
            pltpu.VMEM((QB, LANES), I32),
            pltpu.VMEM((N_HEADS_B, QB, LANES), F32),
            pltpu.VMEM((N_HEADS_B, QB, LANES), F32),
            pltpu.VMEM((2, N_HEADS_B, FAR_G, QB, LANES), F32),
            pltpu.VMEM((2, N_HEADS_B, QB, FAR_T), BF16),
            pltpu.VMEM((2, N_HEADS_B, QB, LANES), F32),
            pltpu.VMEM((2, 32, QB, LANES), I32),
            pltpu.VMEM((QB, LANES), I32),
            pltpu.VMEM((QB, LANES), I32),
        ],
        compiler_params=_cparams(("parallel", "arbitrary")),
        name="dsa",
    )(rel_bias, qb, qi, small, kb, va, kidx2, tab)


HALF_MASK = 0xFFFF0000


def _pack_halves(t):
    w = t.shape[1] // 2
    bits = pltpu.bitcast(t.astype(BF16).astype(F32), U32)
    return (bits[:, :w] >> 16) | (bits[:, w:] & jnp.uint32(HALF_MASK))


def _unpack_halves(p):
    lo = pltpu.bitcast(p << 16, F32)
    hi = pltpu.bitcast(p & jnp.uint32(HALF_MASK), F32)
    return jnp.concatenate([lo, hi], axis=1)


def _outproj_kernel(ya_ref, yb_ref, x_ref, mod_ref, wo_ref, nw_ref, rw_ref, rbias_ref,
                    xn_ref, hp_ref, ridx_ref, gate_ref):
    wa = ya_ref.shape[2]
    y = (jnp.dot(ya_ref[0], wo_ref[0:wa, :], preferred_element_type=F32)
         + jnp.dot(yb_ref[0], wo_ref[wa:, :], preferred_element_type=F32))
    xn = x_ref[0] + mod_ref[0, 2:3, :] * y
    xn_ref[0] = xn
    ms = jnp.mean(xn * xn, axis=-1, keepdims=True)
    h = xn * lax.rsqrt(ms + EPS) * nw_ref[...] * (1.0 + mod_ref[0, 4:5, :]) + mod_ref[0, 3:4, :]
    hp_ref[0] = _pack_halves(h)

    logits = jnp.dot(h, rw_ref[...], precision=HIGHEST, preferred_element_type=F32) + rbias_ref[...]
    lane = lax.broadcasted_iota(I32, logits.shape, 1)
    cur = logits
    vals, ridx = [], jnp.zeros(logits.shape, I32)
    for k in range(TOP_K):
        mx = jnp.max(cur, axis=1, keepdims=True)
        am = jnp.min(jnp.where(cur == mx, lane, LANES), axis=1, keepdims=True)
        cur = jnp.where(lane == am, -jnp.inf, cur)
        vals.append(mx)
        ridx = jnp.where(lane == k, am, ridx)
    ex = [jnp.exp(v - vals[0]) for v in vals]
    inv = 1.0 / (ex[0] + ex[1] + ex[2] + ex[3])
    gate = jnp.zeros(logits.shape, F32)
    for k in range(TOP_K):
        gate = jnp.where(lane == k, ex[k] * inv, gate)
    ridx_ref[0] = ridx
    gate_ref[0] = gate


def _outproj(y_a, y_b, x, mod_l, w_out_bf, norm_w, router_w, router_b, tm):
    bsz, s, d = x.shape
    n_e = router_w.shape[1]
    rw = jnp.zeros((d, LANES), F32).at[:, :n_e].set(router_w)
    rbias = jnp.full((1, LANES), NEG_BIG, F32).at[0, :n_e].set(router_b)
    blk = lambda b, i: (b, i, 0)
    const2 = lambda b, i: (0, 0)
    return pl.pallas_call(
        _outproj_kernel,
        grid=(bsz, s // tm),
        in_specs=[
            pl.BlockSpec((1, tm, y_a.shape[2]), blk),
            pl.BlockSpec((1, tm, y_b.shape[2]), blk),
            pl.BlockSpec((1, tm, d), blk),
            pl.BlockSpec((1, 6, d), lambda b, i: (b, 0, 0)),
            pl.BlockSpec((d, d), const2),
            pl.BlockSpec((1, d), const2),
            pl.BlockSpec((d, LANES), const2),
            pl.BlockSpec((1, LANES), const2),
        ],
        out_specs=[pl.BlockSpec((1, tm, d), blk), pl.BlockSpec((1, tm, d // 2), blk),
                   pl.BlockSpec((1, tm, LANES), blk), pl.BlockSpec((1, tm, LANES), blk)],
        out_shape=[jax.ShapeDtypeStruct((bsz, s, d), F32), jax.ShapeDtypeStruct((bsz, s, d // 2), U32),
                   jax.ShapeDtypeStruct((bsz, s, LANES), I32), jax.ShapeDtypeStruct((bsz, s, LANES), F32)],
        compiler_params=_cparams(("parallel", "parallel")),
        name="outproj_router",
    )(y_a, y_b, x, mod_l, w_out_bf, norm_w.reshape(1, d), rw, rbias)


MOE_TB = 2048
MOE_RB = 512
MOE_M = 144
MOE_WBUF = 3


def _moe_kernel(first_ref, nch_ref, cbase_ref, tok_ref, row_ref, hp_ref, w1_ref, b1_ref, w2_ref, b2_ref,
                gate_ref, x_ref, g2_ref, o_ref, slots_ref, xg_ref, yb_ref, w1_buf, w2_buf, w1_sem, w2_sem,
                *, tb, rb, table_len, layer, n_super):
    sb = pl.program_id(0)
    e = pl.program_id(1)
    dff = w2_ref.shape[1]
    plane = tb + SUBLANES
    table = sb * table_len + 1

    g_step = sb * N_EXPERTS + e
    n_steps = n_super * N_EXPERTS

    def weight_copies(step):
        expert = step % N_EXPERTS
        slot = step % MOE_WBUF
        return (pltpu.make_async_copy(w1_ref.at[expert], w1_buf.at[slot], w1_sem.at[slot]),
                pltpu.make_async_copy(w2_ref.at[layer * N_EXPERTS + expert], w2_buf.at[slot], w2_sem.at[slot]))

    @pl.when((sb == 0) & (e == 0))
    def _():
        for k in range(MOE_WBUF - 1):
            for thread, cp in enumerate(weight_copies(k)):
                cp.start(priority=thread)

    def gather(j):
        base = cbase_ref[table + j]
        buf = (j + 2) % 2
        for r in range(MOE_M):
            xg_ref[buf, pl.ds(r, 1), :] = hp_ref[pl.ds(tok_ref[0, 0, base + r], 1), :]

    def scatter(j):
        base = cbase_ref[table + j]
        buf = (j + 2) % 2
        for r in range(MOE_M):
            slots_ref[pl.ds(row_ref[0, 0, base + r], 1), :] = yb_ref[buf, pl.ds(r, 1), :]

    @pl.when(e == 0)
    def _():
        yb_ref[...] = jnp.zeros_like(yb_ref)
        gather(0)

    @pl.when(e < N_EXPERTS)
    def _():
        j0 = first_ref[sb * N_EXPERTS + e]

        @pl.when(g_step + MOE_WBUF - 1 < n_steps)
        def _():
            for thread, cp in enumerate(weight_copies(g_step + MOE_WBUF - 1)):
                cp.start(priority=thread)

        for cp in weight_copies(g_step):
            cp.wait()
        slot = g_step % MOE_WBUF

        def chunk(j, carry):
            buf = j % 2
            xb = _unpack_halves(xg_ref[buf]).astype(BF16)
            gather(j + 1)
            scatter(j - 1)
            u = jnp.dot(xb, w1_buf[slot], preferred_element_type=F32) + b1_ref[0]
            glu = jnp.minimum(u[:, :dff], SWIGLU_LIMIT)
            lin = jnp.clip(u[:, dff:], -SWIGLU_LIMIT, SWIGLU_LIMIT)
            act = glu * _sigmoid(SWIGLU_ALPHA * glu) * (lin + 1.0)
            y = jnp.dot(act.astype(BF16), w2_buf[slot], preferred_element_type=F32) + b2_ref[0]
            yb_ref[buf] = _pack_halves(y)
            return carry

        lax.fori_loop(j0, j0 + nch_ref[sb * N_EXPERTS + e], chunk, 0)

    @pl.when(e == N_EXPERTS)
    def _():
        n_total = first_ref[sb * N_EXPERTS + N_EXPERTS - 1] + nch_ref[sb * N_EXPERTS + N_EXPERTS - 1]
        scatter(n_total - 1)

    @pl.when(e >= N_EXPERTS)
    def _():
        r0 = pl.multiple_of((e - N_EXPERTS) * rb, rb)
        acc = gate_ref[:, 0:1] * _unpack_halves(slots_ref[pl.ds(r0, rb), :])
        for k in range(1, TOP_K):
            rows = pl.ds(pl.multiple_of(k * plane + r0, SUBLANES), rb)
            acc = acc + gate_ref[:, k:k + 1] * _unpack_halves(slots_ref[rows, :])
        o_ref[...] = x_ref[...] + g2_ref[0] * acc


def _moe(xn, hp, ridx, gate, g2, w1p, b1p, w2b, b2, tb, rb, layer=0):
    bsz, s, d = xn.shape
    t = bsz * s
    n_super = t // tb
    n_piece = tb // rb
    dff = w2b.shape[1]
    flat_e = ridx[:, :, :TOP_K].reshape(n_super, tb * TOP_K)
    order = jnp.argsort(flat_e, axis=1, stable=True).astype(I32)
    counts = jnp.sum(flat_e[:, :, None] == jnp.arange(N_EXPERTS, dtype=I32)[None, None, :], axis=1).astype(I32)
    offs = (jnp.cumsum(counts, axis=1) - counts).astype(I32)
    n_list = tb * TOP_K
    plane = tb + SUBLANES
    tok_list = jnp.pad(order >> 2, ((0, 0), (0, MOE_M)))
    row_list = jnp.pad((order & (TOP_K - 1)) * plane + (order >> 2), ((0, 0), (0, MOE_M)), constant_values=tb)
    nch = (counts + MOE_M - 1) // MOE_M
    first = (jnp.cumsum(nch, axis=1) - nch).astype(I32)
    table_len = -(-(n_list // MOE_M + N_EXPERTS + 2) // SUBLANES) * SUBLANES
    j = jnp.arange(table_len - 1, dtype=I32)[None, :]
    owner = jnp.sum(j[:, :, None] >= (first + nch)[:, None, :], axis=2)
    owner_c = jnp.minimum(owner, N_EXPERTS - 1)
    within = j - jnp.take_along_axis(first, owner_c, axis=1)
    c_base = jnp.where(owner < N_EXPERTS, jnp.take_along_axis(offs, owner_c, axis=1) + within * MOE_M, n_list)
    c_base = jnp.concatenate([jnp.full((n_super, 1), n_list, I32), c_base.astype(I32)], axis=1)

    piece = lambda sb, e, *_: (sb * n_piece + jnp.maximum(e - N_EXPERTS, 0), 0)
    wmap = lambda sb, e, *_: (jnp.minimum(e, N_EXPERTS - 1), 0, 0)
    grid_spec = pltpu.PrefetchScalarGridSpec(
        num_scalar_prefetch=3,
        grid=(n_super, N_EXPERTS + n_piece),
        in_specs=[
            pl.BlockSpec((1, 1, n_list + MOE_M), lambda sb, e, *_: (sb, 0, 0), memory_space=pltpu.SMEM),
            pl.BlockSpec((1, 1, n_list + MOE_M), lambda sb, e, *_: (sb, 0, 0), memory_space=pltpu.SMEM),
            pl.BlockSpec((tb, d // 2), lambda sb, e, *_: (sb, 0), pipeline_mode=pl.Buffered(1)),
            pl.BlockSpec(memory_space=pl.ANY),
            pl.BlockSpec((1, 1, 2 * dff), wmap),
            pl.BlockSpec(memory_space=pl.ANY),
            pl.BlockSpec((1, 1, d), wmap),
            pl.BlockSpec((rb, LANES), piece),
            pl.BlockSpec((rb, d), piece),
            pl.BlockSpec((1, 1, d), lambda sb, e, *_: ((sb * tb) // s, 0, 0)),
        ],
        out_specs=pl.BlockSpec((rb, d), piece),
        scratch_shapes=[
            pltpu.VMEM((TOP_K * plane, d // 2), U32),
            pltpu.VMEM((2, MOE_M, d // 2), U32),
            pltpu.VMEM((2, MOE_M, d // 2), U32),
            pltpu.VMEM((MOE_WBUF, d, 2 * dff), BF16),
            pltpu.VMEM((MOE_WBUF, dff, d), BF16),
            pltpu.SemaphoreType.DMA((MOE_WBUF,)),
            pltpu.SemaphoreType.DMA((MOE_WBUF,)),
        ],
    )
    out = pl.pallas_call(
        functools.partial(_moe_kernel, tb=tb, rb=rb, table_len=table_len, layer=layer, n_super=n_super),
        grid_spec=grid_spec,
        out_shape=jax.ShapeDtypeStruct((t, d), F32),
        compiler_params=_cparams(("arbitrary", "arbitrary")),
        name="moe",
    )(first.reshape(-1), nch.astype(I32).reshape(-1), c_base.reshape(-1),
      tok_list.reshape(n_super, 1, n_list + MOE_M), row_list.reshape(n_super, 1, n_list + MOE_M),
      hp.reshape(t, d // 2),
      w1p, b1p, w2b, b2, gate.reshape(t, LANES), xn.reshape(t, d), g2.reshape(bsz, 1, d))
    return out.reshape(bsz, s, d)


MXU_COLS = 256


def _deinterleave_kernel(w_ref, perm_ref, o_ref):
    half = w_ref.shape[2] // 2
    hw = MXU_COLS // 2
    for b in range(w_ref.shape[2] // MXU_COLS):
        blk = w_ref[0, :, b * MXU_COLS:(b + 1) * MXU_COLS].astype(BF16)
        y = jnp.dot(blk, perm_ref[...], preferred_element_type=F32).astype(BF16)
        o_ref[0, :, b * hw:(b + 1) * hw] = y[:, :hw]
        o_ref[0, :, half + b * hw:half + (b + 1) * hw] = y[:, hw:]


def _deinterleave_cast(w1, layer, rows=512):
    depth, n_e, d, two_f = w1.shape
    src = np.concatenate([np.arange(0, MXU_COLS, 2), np.arange(1, MXU_COLS, 2)])
    perm = np.zeros((MXU_COLS, MXU_COLS), np.float32)
    perm[src, np.arange(MXU_COLS)] = 1.0
    return pl.pallas_call(
        _deinterleave_kernel,
        grid=(n_e, d // rows),
        in_specs=[pl.BlockSpec((1, rows, two_f), lambda e, r: (layer * n_e + e, r, 0)),
                  pl.BlockSpec((MXU_COLS, MXU_COLS), lambda e, r: (0, 0))],
        out_specs=pl.BlockSpec((1, rows, two_f), lambda e, r: (e, r, 0)),
        out_shape=jax.ShapeDtypeStruct((n_e, d, two_f), BF16),
        compiler_params=_cparams(("parallel", "parallel")),
        name="w1_deinterleave",
    )(w1.reshape(depth * n_e, d, two_f), jnp.asarray(perm, BF16))


def _deinterleave_bias(b1_l):
    n_e, two_f = b1_l.shape
    return jnp.concatenate([b1_l[:, 0::2], b1_l[:, 1::2]], axis=1).reshape(n_e, 1, two_f)


def kernel(x, c, rel_bias, mod_w, mod_b, norm_mix_w, norm_ffn_w, w_in, conv_w, a_log, dt_bias, gdn_norm_w,
           q_norm_w, k_norm_w, w_out, router_w, router_b, w1, b1, w2, b2):
    depth = mod_w.shape[0]
    bsz, s, d = x.shape
    mod = _modulation(c, mod_w, mod_b)
    tm = min(512, s)
    tb = min(MOE_TB, bsz * s)
    rb = min(MOE_RB, tb)
    n_e, dff = w2.shape[1], w2.shape[2]
    w2b = w2.astype(BF16).reshape(depth * n_e, dff, d)
    for l in range(depth):
        qkv_a, z_a, qb, kb, vb, qi, small = _inproj(
            x, mod[l], norm_mix_w[l], _permute_w_in(w_in[l]), q_norm_w[l], k_norm_w[l], tm)
        y_a = _gdn(qkv_a, z_a, small, conv_w[l], a_log[l], dt_bias[l], gdn_norm_w[l], sb=tm)
        y_b = _dsa(qb, kb, vb, qi, small, rel_bias)
        xn, hp, ridx, gate = _outproj(y_a, y_b, x, mod[l], w_out[l].astype(BF16), norm_ffn_w[l],
                                      router_w[l], router_b[l], tm)
        x = _moe(xn, hp, ridx, gate, mod[l][:, 5], _deinterleave_cast(w1, l), _deinterleave_bias(b1[l]), w2b,
                 b2[l].reshape(n_e, 1, d), tb, rb, layer=l)
    return x
```
